```python
import math
import jax, jax.numpy as jnp
from jax import lax
import numpy as np

D_MODEL = 1024
BATCH = 16
SEQ = 256
DEPTH = 2
DEC_BATCH = 4
DEC_SEQ = 2048
PAST_LEN = 512

GRID_W = 64
N_EVEN = (DEPTH + 1) // 2
N_ODD = DEPTH // 2
MLSTM_HEADS = 4
MLSTM_WIDTH = D_MODEL // 2
MLSTM_HD = MLSTM_WIDTH // MLSTM_HEADS
MLSTM_CHUNK = 64
HYENA_WIDTH = D_MODEL // 2
N_BANDS = 16
FILTER_EMB = 2 * N_BANDS + 1
FILTER_HIDDEN = 64
DECAY_FAST = 0.3
DECAY_SLOW = 1.5
DECAY_TARGET = 1e-2
NA_HEADS = 16
NA_HD = D_MODEL // NA_HEADS
NA_KH = 8
NA_KW = 16
Q_BLOCK = 128
FF = 2816
CONV_W = 3
EPS = 1e-6
IN_AB = 4 * MLSTM_WIDTH + 4 * MLSTM_HEADS + 3 * HYENA_WIDTH

kernel_name = 'hybrid_mlstm_hyena_natten_dit_step'

F32 = jnp.float32


def rmsnorm(x, g):
    x32 = x.astype(F32)
    y = x32 * lax.rsqrt(jnp.mean(x32 * x32, axis=-1, keepdims=True) + EPS)
    return (y * g.astype(F32)).astype(x.dtype)


def dwconv3(x, w):
    xp = jnp.pad(x, ((0, 0), (1, 1), (0, 0)))
    return xp[:, :-2] * w[0] + xp[:, 1:-1] * w[1] + xp[:, 2:] * w[2]


def adaln(cvec, w, b):
    mod = jax.nn.silu(cvec) @ w + b
    return jnp.split(mod[:, None, :], 6, axis=-1)


def mlstm_chunked(q, k, v, ig, lf, C0, n0, m0):
    B, H, L, d = q.shape
    nc = L // MLSTM_CHUNK

    def to_chunks(t):
        return jnp.moveaxis(t.reshape((B, H, nc, MLSTM_CHUNK) + t.shape[3:]), 2, 0)

    lower = jnp.tril(jnp.ones((MLSTM_CHUNK, MLSTM_CHUNK), dtype=bool))

    def step(carry, xs):
        C, n, m = carry
        qc, kc, vc, ic, fc = xs
        b = jnp.cumsum(fc, axis=-1)
        dmat = jnp.where(lower, b[..., :, None] - b[..., None, :] + ic[..., None, :], -jnp.inf)
        m_inter = b + m[..., None]
        m_t = jnp.maximum(m_inter, jnp.max(dmat, axis=-1))
        s = jnp.exp(dmat - m_t[..., None]) * jnp.einsum('bhtd,bhsd->bhts', qc, kc)
        sc = jnp.exp(m_inter - m_t)
        num = sc[..., None] * jnp.einsum('bhtd,bhde->bhte', qc, C) + jnp.einsum('bhts,bhse->bhte', s, vc)
        den = sc * jnp.einsum('bhtd,bhd->bht', qc, n) + jnp.sum(s, axis=-1)
        hc = num / jnp.maximum(jnp.abs(den), jnp.exp(-m_t))[..., None]
        b_end = b[..., -1]
        to_end = b_end[..., None] - b + ic
        m_new = jnp.maximum(b_end + m, jnp.max(to_end, axis=-1))
        w = jnp.exp(to_end - m_new[..., None])
        decay = jnp.exp(b_end + m - m_new)
        C_new = decay[..., None, None] * C + jnp.einsum('bhs,bhsd,bhse->bhde', w, kc, vc)
        n_new = decay[..., None] * n + jnp.einsum('bhs,bhsd->bhd', w, kc)
        return (C_new, n_new, m_new), hc

    xs = (to_chunks(q), to_chunks(k), to_chunks(v), to_chunks(ig), to_chunks(lf))
    (C, n, m), hs = lax.scan(step, (C0, n0, m0), xs)
    return jnp.moveaxis(hs, 0, 2).reshape(B, H, L, d), (C, n, m)


def mlstm_bidir(q, k, v, gates, state0):
    B, L, _ = q.shape

    def heads(t):
        return t.astype(F32).reshape(B, L, MLSTM_HEADS, MLSTM_HD).transpose(0, 2, 1, 3)

    qh, kh, vh = heads(q), heads(k) * (MLSTM_HD ** -0.5), heads(v)
    g = gates.astype(F32).transpose(0, 2, 1)
    i_f, f_f, i_b, f_b = jnp.split(g, 4, axis=1)
    C0, n0, m0 = state0[0].astype(F32), state0[1].astype(F32), state0[2].astype(F32)
    h_f, (Cf, nf, mf) = mlstm_chunked(qh, kh, vh, i_f, jax.nn.log_sigmoid(f_f), C0[:, 0], n0[:, 0], m0[:, 0])

    def flip(t):
        return jnp.flip(t, axis=2)

    h_b, (Cb, nb, mb) = mlstm_chunked(flip(qh), flip(kh), flip(vh), flip(i_b), flip(jax.nn.log_sigmoid(f_b)),
                                      C0[:, 1], n0[:, 1], m0[:, 1])
    h = h_f + flip(h_b)
    state = (jnp.stack([Cf, Cb], axis=1), jnp.stack([nf, nb], axis=1), jnp.stack([mf, mb], axis=1))
    return h, state


def hyena_filters(L, w1, b1, w2, b2, w3, freq):
    t = jnp.linspace(0.0, 1.0, L, dtype=F32)[:, None]
    wpos = 2.0 * math.pi * jnp.arange(L, dtype=F32)[:, None] / L
    bands = jnp.linspace(1e-4, N_BANDS - 1, N_BANDS, dtype=F32)[None, :]
    z = jnp.concatenate([t, jnp.cos(bands * wpos), -jnp.sin(bands * wpos)], axis=-1)
    fr = freq.astype(F32)
    hdn = jnp.sin(fr * (z @ w1.astype(F32) + b1.astype(F32)))
    hdn = jnp.sin(fr * (hdn @ w2.astype(F32) + b2.astype(F32)))
    filt = hdn @ w3.astype(F32)
    max_decay = math.log(DECAY_TARGET) / DECAY_FAST
    min_decay = math.log(DECAY_TARGET) / DECAY_SLOW
    deltas = jnp.abs(jnp.linspace(min_decay, max_decay, HYENA_WIDTH, dtype=F32))
    decay = jnp.exp(-t * jnp.concatenate([deltas, deltas])[None, :])
    filt = filt * decay
    return filt[:, :HYENA_WIDTH], filt[:, HYENA_WIDTH:]


def long_conv_bidir(u, h_past, h_future):
    B, L, C = u.shape
    two_sided = jnp.concatenate([h_past, jnp.zeros((1, C), F32), h_future[1:][::-1]], axis=0)
    U = jnp.fft.rfft(u, n=2 * L, axis=1)
    K = jnp.fft.rfft(two_sided, axis=0)
    return jnp.fft.irfft(U * K[None], n=2 * L, axis=1)[:, :L]


def mixer_ab(h, state0, w_in, b_gates, w_conv_qk, g_head, w_conv_hy, w_f1, b_f1, w_f2, b_f2, w_f3, freq,
             hy_bias, w_out):
    B, L, _ = h.shape
    MW = MLSTM_WIDTH
    proj = h @ w_in
    qk_raw = proj[..., :2 * MW]
    v_m = proj[..., 2 * MW:3 * MW]
    o_pre = proj[..., 3 * MW:4 * MW]
    gates = proj[..., 4 * MW:4 * MW + 4 * MLSTM_HEADS] + b_gates
    hy = proj[..., 4 * MW + 4 * MLSTM_HEADS:]
    qk = jax.nn.silu(dwconv3(qk_raw, w_conv_qk))
    hm, state = mlstm_bidir(qk[..., :MW], qk[..., MW:], v_m, gates, state0)
    hm = hm * lax.rsqrt(jnp.mean(hm * hm, axis=-1, keepdims=True) + EPS)
    hm = hm.transpose(0, 2, 1, 3).reshape(B, L, MW) * g_head.astype(F32)
    y_m = (hm * jax.nn.sigmoid(o_pre.astype(F32))).astype(h.dtype)
    hy = dwconv3(hy, w_conv_hy)
    v_h, x1, x2 = jnp.split(hy, 3, axis=-1)
    u = (x1 * v_h).astype(F32)
    h_past, h_future = hyena_filters(L, w_f1, b_f1, w_f2, b_f2, w_f3, freq)
    y_h = x2.astype(F32) * (long_conv_bidir(u, h_past, h_future) + hy_bias.astype(F32) * u)
    out = jnp.concatenate([y_m, y_h.astype(h.dtype)], axis=-1) @ w_out
    return out, (state[0].astype(h.dtype), state[1].astype(h.dtype), state[2].astype(h.dtype))


def na_heads(t):
    B, L, _ = t.shape
    return t.reshape(B, L, NA_HEADS, NA_HD).transpose(0, 2, 1, 3)


def dense_attention(q, k, v):
    B, H, L, d = q.shape
    nb = L // Q_BLOCK
    qb = jnp.moveaxis(q.reshape(B, H, nb, Q_BLOCK, d), 2, 0)

    def block(qi):
        s = jnp.einsum('bhqd,bhkd->bhqk', qi, k).astype(F32) * (d ** -0.5)
        p = jax.nn.softmax(s, axis=-1)
        return jnp.einsum('bhqk,bhkd->bhqd', p.astype(v.dtype), v)

    o = lax.map(block, qb)
    return jnp.moveaxis(o, 0, 2).reshape(B, H, L, d)


def mixer_c_context(h, w_in, w_out):
    B, L, _ = h.shape
    q, k, v = jnp.split(h @ w_in, 3, axis=-1)
    q, k, v = na_heads(q), na_heads(k), na_heads(v)
    o = dense_attention(q, k, v)
    return o.transpose(0, 2, 1, 3).reshape(B, L, D_MODEL) @ w_out, k, v


def na_attention(q, k, v, k_ctx, v_ctx, rpb):
    B, H, L, d = q.shape
    R = L // GRID_W
    kh = min(NA_KH, R)
    rows = jnp.arange(R)
    cols = jnp.arange(GRID_W)
    r_start = jnp.clip(rows - kh // 2, 0, R - kh)
    key_rows = r_start[:, None] + jnp.arange(kh)[None, :]
    c_start = jnp.clip(cols - NA_KW // 2, 0, GRID_W - NA_KW)
    col_mask = (cols[None, :] >= c_start[:, None]) & (cols[None, :] < c_start[:, None] + NA_KW)
    qg = q.reshape(B, H, R, GRID_W, d) * (d ** -0.5)
    kg = k.reshape(B, H, R, GRID_W, d)[:, :, key_rows]
    vg = v.reshape(B, H, R, GRID_W, d)[:, :, key_rows]
    s_loc = jnp.einsum('bhrqd,bhrjwd->bhrqjw', qg, kg).astype(F32)
    idx_r = key_rows - rows[:, None] + (NA_KH - 1)
    idx_c = jnp.clip(cols[None, :] - cols[:, None] + (NA_KW - 1), 0, 2 * NA_KW - 2)
    bias = rpb[:, idx_r[:, None, :, None], idx_c[None, :, None, :]]
    s_loc = jnp.where(col_mask[:, None, :], s_loc + bias.astype(F32)[None], -jnp.inf)
    s_ctx = jnp.einsum('bhrqd,bhnd->bhrqn', qg, k_ctx).astype(F32)
    n_loc = kh * GRID_W
    s = jnp.concatenate([s_loc.reshape(B, H, R, GRID_W, n_loc), s_ctx], axis=-1)
    p = jax.nn.softmax(s, axis=-1).astype(v.dtype)
    p_loc = p[..., :n_loc].reshape(B, H, R, GRID_W, kh, GRID_W)
    p_ctx = p[..., n_loc:]
    o = jnp.einsum('bhrqjw,bhrjwd->bhrqd', p_loc, vg) + jnp.einsum('bhrqn,bhnd->bhrqd', p_ctx, v_ctx)
    return o.reshape(B, H, L, d)


def mixer_c_latent(h, k_ctx, v_ctx, w_in, rpb, w_out):
    B, L, _ = h.shape
    q, k, v = jnp.split(h @ w_in, 3, axis=-1)
    o = na_attention(na_heads(q), na_heads(k), na_heads(v), k_ctx, v_ctx, rpb)
    return o.transpose(0, 2, 1, 3).reshape(B, L, D_MODEL) @ w_out


def conv_ffn(h, w_up, w_conv, w_down):
    a, g = jnp.split(h @ w_up, 2, axis=-1)
    return (jax.nn.gelu(dwconv3(a, w_conv)) * g) @ w_down


def setup_inputs(seed: int = 0) -> dict:
    key = jax.random.key(seed)
    ks = jax.random.split(key, 40)

    def nrm(i, shape, scale):
        return jax.random.normal(ks[i], shape, F32) * scale

    D = D_MODEL
    H = MLSTM_HEADS
    gate_noise = nrm(10, (N_EVEN, 4, H), 0.1)
    f_bias = jnp.linspace(3.0, 6.0, H, dtype=F32)[None, :]
    b_gates = jnp.concatenate([gate_noise[:, 0], gate_noise[:, 1] + f_bias,
                               gate_noise[:, 2], gate_noise[:, 3] + f_bias], axis=-1)
    return {
        'x_prompt': nrm(0, (BATCH, SEQ, D), 1.0),
        'x_sample': nrm(1, (DEC_BATCH, DEC_SEQ, D), 1.0),
        'state_mlstm_C': nrm(2, (DEC_BATCH, N_EVEN, 2, H, MLSTM_HD, MLSTM_HD), 0.1),
        'state_mlstm_n': nrm(3, (DEC_BATCH, N_EVEN, 2, H, MLSTM_HD), 0.1),
        'state_mlstm_m': nrm(4, (DEC_BATCH, N_EVEN, 2, H), 0.5),
        'cache_na_k': nrm(5, (DEC_BATCH, N_ODD, NA_HEADS, PAST_LEN, NA_HD), 1.0),
        'cache_na_v': nrm(6, (DEC_BATCH, N_ODD, NA_HEADS, PAST_LEN, NA_HD), 1.0),
        'c': nrm(7, (DEC_BATCH, D), 1.0),
        'c_ctx': nrm(8, (D,), 1.0),
        'w_ada': nrm(9, (DEPTH, D, 6 * D), D ** -0.5),
        'b_ada': nrm(11, (DEPTH, 6 * D), 0.02),
        'g_mix': 1.0 + nrm(12, (DEPTH, D), 0.05),
        'g_ffn': 1.0 + nrm(13, (DEPTH, D), 0.05),
        'g_final': 1.0 + nrm(14, (D,), 0.05),
        'w_in_ab': nrm(15, (N_EVEN, D, IN_AB), D ** -0.5),
        'b_gates': b_gates,
        'w_conv_qk': nrm(16, (N_EVEN, CONV_W, 2 * MLSTM_WIDTH), 0.5),
        'g_mlstm': 1.0 + nrm(17, (N_EVEN, MLSTM_WIDTH), 0.05),
        'w_conv_hy': nrm(18, (N_EVEN, CONV_W, 3 * HYENA_WIDTH), 0.5),
        'w_filt1': nrm(19, (N_EVEN, FILTER_EMB, FILTER_HIDDEN), FILTER_EMB ** -0.5),
        'b_filt1': nrm(20, (N_EVEN, FILTER_HIDDEN), 0.1),
        'w_filt2': nrm(21, (N_EVEN, FILTER_HIDDEN, FILTER_HIDDEN), FILTER_HIDDEN ** -0.5),
        'b_filt2': nrm(22, (N_EVEN, FILTER_HIDDEN), 0.1),
        'w_filt3': nrm(23, (N_EVEN, FILTER_HIDDEN, 2 * HYENA_WIDTH), 0.01),
        'filt_freq': 1.0 + nrm(24, (N_EVEN, FILTER_HIDDEN), 0.1),
        'hyena_bias': nrm(25, (N_EVEN, HYENA_WIDTH), 0.5),
        'w_out_ab': nrm(26, (N_EVEN, D, D), D ** -0.5),
        'w_in_c': nrm(27, (N_ODD, D, 3 * D), D ** -0.5),
        'rpb_c': nrm(28, (N_ODD, NA_HEADS, 2 * NA_KH - 1, 2 * NA_KW - 1), 0.1),
        'w_out_c': nrm(29, (N_ODD, D, D), D ** -0.5),
        'w_up': nrm(30, (DEPTH, D, 2 * FF), D ** -0.5),
        'w_conv_ffn': nrm(31, (DEPTH, CONV_W, FF), 0.5),
        'w_down': nrm(32, (DEPTH, FF, D), FF ** -0.5),
    }


def reference(x_prompt, x_sample, state_mlstm_C, state_mlstm_n, state_mlstm_m, cache_na_k, cache_na_v, c, c_ctx,
              w_ada, b_ada, g_mix, g_ffn, g_final, w_in_ab, b_gates, w_conv_qk, g_mlstm, w_conv_hy,
              w_filt1, b_filt1, w_filt2, b_filt2, w_filt3, filt_freq, hyena_bias, w_out_ab,
              w_in_c, rpb_c, w_out_c, w_up, w_conv_ffn, w_down):
    xp, xs = x_prompt, x_sample
    Bp = xp.shape[0]
    zero_state = (jnp.zeros((Bp, 2, MLSTM_HEADS, MLSTM_HD, MLSTM_HD), xp.dtype),
                  jnp.zeros((Bp, 2, MLSTM_HEADS, MLSTM_HD), xp.dtype),
                  jnp.zeros((Bp, 2, MLSTM_HEADS), xp.dtype))
    new_C, new_n, new_m, new_k, new_v = [], [], [], [], []
    for l in range(DEPTH):
        sh_p1, sc_p1, gt_p1, sh_p2, sc_p2, gt_p2 = adaln(c_ctx[None, :], w_ada[l], b_ada[l])
        sh_s1, sc_s1, gt_s1, sh_s2, sc_s2, gt_s2 = adaln(c, w_ada[l], b_ada[l])
        hp = rmsnorm(xp, g_mix[l]) * (1.0 + sc_p1) + sh_p1
        hs = rmsnorm(xs, g_mix[l]) * (1.0 + sc_s1) + sh_s1
        e = l // 2
        if l % 2 == 0:
            ab = (w_in_ab[e], b_gates[e], w_conv_qk[e], g_mlstm[e], w_conv_hy[e], w_filt1[e], b_filt1[e],
                  w_filt2[e], b_filt2[e], w_filt3[e], filt_freq[e], hyena_bias[e], w_out_ab[e])
            op, (Cp, np_, mp) = mixer_ab(hp, zero_state, *ab)
            os_, _ = mixer_ab(hs, (state_mlstm_C[:, e], state_mlstm_n[:, e], state_mlstm_m[:, e]), *ab)
            new_C.append(Cp)
            new_n.append(np_)
            new_m.append(mp)
        else:
            op, kp, vp = mixer_c_context(hp, w_in_c[e], w_out_c[e])
            os_ = mixer_c_latent(hs, cache_na_k[:, e], cache_na_v[:, e], w_in_c[e], rpb_c[e], w_out_c[e])
            new_k.append(kp)
            new_v.append(vp)
        xp = xp + gt_p1 * op
        xs = xs + gt_s1 * os_
        xp = xp + gt_p2 * conv_ffn(rmsnorm(xp, g_ffn[l]) * (1.0 + sc_p2) + sh_p2, w_up[l], w_conv_ffn[l], w_down[l])
        xs = xs + gt_s2 * conv_ffn(rmsnorm(xs, g_ffn[l]) * (1.0 + sc_s2) + sh_s2, w_up[l], w_conv_ffn[l], w_down[l])
    y_prompt = rmsnorm(xp, g_final)
    y_sample = rmsnorm(xs, g_final)
    new_state_mlstm_C = jnp.stack(new_C, axis=1)
    new_state_mlstm_n = jnp.stack(new_n, axis=1)
    new_state_mlstm_m = jnp.stack(new_m, axis=1)
    new_cache_na_k = jnp.stack(new_k, axis=1)
    new_cache_na_v = jnp.stack(new_v, axis=1)
    return (y_prompt, y_sample, new_state_mlstm_C, new_state_mlstm_n, new_state_mlstm_m, new_cache_na_k, new_cache_na_v)
```

```python
import functools
import math

import numpy as np
import jax
import jax.numpy as jnp
from jax import lax
from jax.experimental import pallas as pl
from jax.experimental.pallas import tpu as pltpu

F32 = jnp.float32
BF16 = jnp.bfloat16

D = 1024
BATCH, SEQ = 16, 256
DEC_BATCH, DEC_SEQ = 4, 2048
PAST_LEN = 512
DEPTH = 2
GRID_W = 64
GRID_R = DEC_SEQ // GRID_W
HEADS_M = 4
MW = D // 2
HD_M = MW // HEADS_M
CHUNK = 64
HW = D // 2
N_BANDS = 16
FILTER_EMB = 2 * N_BANDS + 1
FILTER_HIDDEN = 64
DECAY_FAST, DECAY_SLOW, DECAY_TARGET = 0.3, 1.5, 1e-2
NA_HEADS = 16
NA_HD = D // NA_HEADS
NA_KH, NA_KW = 8, 16
FF = 2816
EPS = 1e-6

N_P = BATCH * SEQ
N_S = DEC_BATCH * DEC_SEQ
N_TOK = N_P + N_S
CTX_ROW = DEC_BATCH
MOD_ROWS = 8
LANE = 128
VMEM_LIMIT = 48 * 1024 * 1024
HIGHEST = lax.Precision.HIGHEST


def _cparams(sem):
    return pltpu.CompilerParams(dimension_semantics=sem, vmem_limit_bytes=VMEM_LIMIT)


def _mod_row(i, bm):
    return jnp.where(i < N_P // bm, CTX_ROW, (i - N_P // bm) // (DEC_SEQ // bm))


def _dot(a, b):
    return jnp.dot(a, b, preferred_element_type=F32)


def _adaln_kernel(c_ref, w_ref, b_ref, o_ref):
    cv = c_ref[...]
    s = cv * jax.nn.sigmoid(cv)
    o_ref[...] = _dot(s.astype(BF16), w_ref[...].astype(BF16)) + b_ref[...]


def adaln_all(cmat, w_ada, b_ada):
    tn = 1024
    return pl.pallas_call(
        _adaln_kernel,
        out_shape=jax.ShapeDtypeStruct((DEPTH, MOD_ROWS, 6 * D), F32),
        grid=(DEPTH, 6 * D // tn),
        in_specs=[
            pl.BlockSpec((MOD_ROWS, D), lambda l, j: (0, 0)),
            pl.BlockSpec((None, D, tn), lambda l, j: (l, 0, j)),
            pl.BlockSpec((None, 1, tn), lambda l, j: (l, 0, j)),
        ],
        out_specs=pl.BlockSpec((None, MOD_ROWS, tn), lambda l, j: (l, 0, j)),
        compiler_params=_cparams(("arbitrary", "arbitrary")),
    )(cmat, w_ada, b_ada.reshape(DEPTH, 1, 6 * D))


def _norm_mod_kernel(x_ref, g_ref, sh_ref, sc_ref, o_ref):
    x = x_ref[...]
    y = x * lax.rsqrt(jnp.mean(x * x, axis=-1, keepdims=True) + EPS)
    o_ref[...] = ((y * g_ref[...]) * (1.0 + sc_ref[...]) + sh_ref[...]).astype(o_ref.dtype)


def norm_mod(x, g, mod, sh_idx, sc_idx):
    bm = 512
    return pl.pallas_call(
        _norm_mod_kernel,
        out_shape=jax.ShapeDtypeStruct((N_TOK, D), BF16),
        grid=(N_TOK // bm,),
        in_specs=[
            pl.BlockSpec((bm, D), lambda i: (i, 0)),
            pl.BlockSpec((1, D), lambda i: (0, 0)),
            pl.BlockSpec((None, None, 1, D), lambda i: (_mod_row(i, bm), sh_idx, 0, 0)),
            pl.BlockSpec((None, None, 1, D), lambda i: (_mod_row(i, bm), sc_idx, 0, 0)),
        ],
        out_specs=pl.BlockSpec((bm, D), lambda i: (i, 0)),
        compiler_params=_cparams(("arbitrary",)),
    )(x, g.reshape(1, D), mod, mod)


def _final_norm_kernel(x_ref, g_ref, o_ref):
    x = x_ref[...]
    y = x * lax.rsqrt(jnp.mean(x * x, axis=-1, keepdims=True) + EPS)
    o_ref[...] = y * g_ref[...]


def final_norm(x, g, row0, nrows):
    bm = 512
    return pl.pallas_call(
        _final_norm_kernel,
        out_shape=jax.ShapeDtypeStruct((nrows, D), F32),
        grid=(nrows // bm,),
        in_specs=[
            pl.BlockSpec((bm, D), lambda i: (i + row0 // bm, 0)),
            pl.BlockSpec((1, D), lambda i: (0, 0)),
        ],
        out_specs=pl.BlockSpec((bm, D), lambda i: (i, 0)),
        compiler_params=_cparams(("arbitrary",)),
    )(x, g.reshape(1, D))


def _mm_kernel(*refs, n_a, mode, heads_per_tile):
    a_refs = refs[:n_a]
    w_refs = refs[n_a:2 * n_a]
    rest = refs[2 * n_a:]
    if mode == "residual":
        x_ref, gt_ref, o_ref = rest[0], rest[1], rest[2]
        wbf = rest[3:]
    elif mode == "bias":
        b_ref, o_ref = rest[0], rest[1]
        wbf = rest[2:]
    else:
        o_ref = rest[0]
        wbf = rest[1:]

    @pl.when(pl.program_id(1) == 0)
    def _():
        for w_ref, wb in zip(w_refs, wbf):
            wb[...] = w_ref[...].astype(BF16)

    acc = _dot(a_refs[0][...], wbf[0][...])
    for a_ref, wb in zip(a_refs[1:], wbf[1:]):
        acc = acc + _dot(a_ref[...], wb[...])
    if mode == "residual":
        o_ref[...] = x_ref[...] + gt_ref[...] * acc
    elif mode == "bias":
        o_ref[...] = acc + b_ref[...]
    elif mode == "heads":
        for hh in range(heads_per_tile):
            o_ref[hh] = acc[:, hh * NA_HD:(hh + 1) * NA_HD]
    else:
        o_ref[...] = acc.astype(o_ref.dtype)


def token_matmul(a_list, w_list, *, tn, bm=512, mode="plain", x=None, mod=None, gt_idx=None, bias=None,
                 out_dtype=F32):
    n_a = len(a_list)
    nout = w_list[0].shape[1]
    grid = (nout // tn, N_TOK // bm)
    in_specs = [pl.BlockSpec((bm, a.shape[1]), lambda j, i: (i, 0)) for a in a_list]
    in_specs += [pl.BlockSpec((w.shape[0], tn), lambda j, i: (0, j)) for w in w_list]
    args = list(a_list) + list(w_list)
    heads_per_tile = tn // NA_HD
    if mode == "residual":
        in_specs += [pl.BlockSpec((bm, tn), lambda j, i: (i, j)),
                     pl.BlockSpec((None, None, 1, tn), lambda j, i: (_mod_row(i, bm), gt_idx, 0, j))]
        args += [x, mod]
    elif mode == "bias":
        in_specs += [pl.BlockSpec((1, tn), lambda j, i: (0, j))]
        args += [bias]
    if mode == "heads":
        tiles_per_part = D // tn
        out_shape = jax.ShapeDtypeStruct((3, NA_HEADS, N_TOK, NA_HD), F32)
        out_spec = pl.BlockSpec((None, heads_per_tile, bm, NA_HD),
                                lambda j, i: (j // tiles_per_part, j % tiles_per_part, i, 0))
    else:
        out_shape = jax.ShapeDtypeStruct((N_TOK, nout), out_dtype)
        out_spec = pl.BlockSpec((bm, tn), lambda j, i: (i, j))
    return pl.pallas_call(
        functools.partial(_mm_kernel, n_a=n_a, mode=mode, heads_per_tile=heads_per_tile),
        out_shape=out_shape,
        grid=grid,
        in_specs=in_specs,
        out_specs=out_spec,
        scratch_shapes=[pltpu.VMEM((w.shape[0], tn), BF16) for w in w_list],
        compiler_params=_cparams(("arbitrary", "arbitrary")),
    )(*args)


SEQ_BLOCK = DEC_SEQ
N_PBLK = N_P // SEQ_BLOCK


def _seq_edges(rows, is_prompt):
    r = lax.broadcasted_iota(jnp.int32, (rows, 1), 0)
    first = (r == 0) | (is_prompt & (r % SEQ == 0))
    last = (r == rows - 1) | (is_prompt & (r % SEQ == SEQ - 1))
    return first, last


def _dwconv3(x, w_ref, first, last):
    rows = x.shape[0]
    prev = jnp.where(first, 0.0, pltpu.roll(x, 1, 0))
    nxt = jnp.where(last, 0.0, pltpu.roll(x, rows - 1, 0))
    return prev * w_ref[0:1, :] + x * w_ref[1:2, :] + nxt * w_ref[2:3, :]


def _gelu_tanh(x):
    return 0.5 * x * (1.0 + jnp.tanh(math.sqrt(2.0 / math.pi) * (x + 0.044715 * (x * x * x))))


def _ffn_mid_kernel(a_ref, g_ref, w_ref, o_ref):
    first, last = _seq_edges(SEQ_BLOCK, pl.program_id(0) < N_PBLK)
    o_ref[...] = (_gelu_tanh(_dwconv3(a_ref[...], w_ref, first, last)) * g_ref[...]).astype(o_ref.dtype)


def ffn_mid(ag, w_conv):
    tc = 256
    nct = FF // tc
    return pl.pallas_call(
        _ffn_mid_kernel,
        out_shape=jax.ShapeDtypeStruct((N_TOK, FF), BF16),
        grid=(N_TOK // SEQ_BLOCK, nct),
        in_specs=[
            pl.BlockSpec((SEQ_BLOCK, tc), lambda i, j: (i, j)),
            pl.BlockSpec((SEQ_BLOCK, tc), lambda i, j: (i, j + nct)),
            pl.BlockSpec((3, tc), lambda i, j: (0, j)),
        ],
        out_specs=pl.BlockSpec((SEQ_BLOCK, tc), lambda i, j: (i, j)),
        compiler_params=_cparams(("arbitrary", "arbitrary")),
    )(ag, ag, w_conv)


def _ctx_attn_kernel(q_ref, k_ref, v_ref, o_ref, *, heads):
    outs = []
    for hh in range(heads):
        q = q_ref[hh].astype(BF16)
        k = k_ref[hh].astype(BF16)
        s = lax.dot_general(q, k, (((1,), (1,)), ((), ())), preferred_element_type=F32) * (NA_HD ** -0.5)
        m = jnp.max(s, axis=-1, keepdims=True)
        p = jnp.exp(s - m)
        l = jnp.sum(p, axis=-1, keepdims=True)
        outs.append(_dot(p.astype(BF16), v_ref[hh].astype(BF16)) / l)
    o_ref[...] = jnp.concatenate(outs, axis=-1).astype(o_ref.dtype)


def ctx_attention(qkv):
    heads = 4
    spec = lambda part: pl.BlockSpec((None, heads, SEQ, NA_HD), lambda b, h: (part, h, b, 0))
    return pl.pallas_call(
        functools.partial(_ctx_attn_kernel, heads=heads),
        out_shape=jax.ShapeDtypeStruct((N_P, D), BF16),
        grid=(BATCH, NA_HEADS // heads),
        in_specs=[spec(0), spec(1), spec(2)],
        out_specs=pl.BlockSpec((SEQ, heads * NA_HD), lambda b, h: (b, h)),
        compiler_params=_cparams(("arbitrary", "arbitrary")),
    )(qkv, qkv, qkv)


def _na_tables():
    q = np.arange(GRID_W)[:, None]
    w = np.arange(GRID_W)[None, :]
    idx_c = np.clip(w - q + (NA_KW - 1), 0, 2 * NA_KW - 2)
    onehot = (idx_c.reshape(1, -1) == np.arange(32)[:, None]).astype(np.float32)
    c_start = np.clip(np.arange(GRID_W) - NA_KW // 2, 0, GRID_W - NA_KW)[:, None]
    inside = (w >= c_start) & (w < c_start + NA_KW)
    cmask = np.where(inside, 0.0, -np.inf).astype(np.float32)
    return onehot, np.tile(cmask, (1, NA_KH))


def _rpb_expand_kernel(r_ref, e_ref, o_ref):
    o_ref[...] = jnp.dot(r_ref[...], e_ref[...], precision=HIGHEST, preferred_element_type=F32)


def rpb_expand(rpb):
    onehot, _ = _na_tables()
    rp = jnp.pad(rpb, ((0, 0), (0, 1), (0, 1)))
    return pl.pallas_call(
        _rpb_expand_kernel,
        out_shape=jax.ShapeDtypeStruct((NA_HEADS, 16, GRID_W * GRID_W), F32),
        grid=(NA_HEADS,),
        in_specs=[pl.BlockSpec((None, 16, 32), lambda h: (h, 0, 0)),
                  pl.BlockSpec((32, GRID_W * GRID_W), lambda h: (0, 0))],
        out_specs=pl.BlockSpec((None, 16, GRID_W * GRID_W), lambda h: (h, 0, 0)),
        compiler_params=_cparams(("arbitrary",)),
    )(rp, jnp.asarray(onehot))


def _na_attn_kernel(q_ref, k_ref, v_ref, kc_ref, vc_ref, bias_ref, cmask_ref, o_ref, *, heads):
    n_loc = NA_KH * GRID_W
    cmask = cmask_ref[...]
    for hh in range(heads):
        kc = kc_ref[hh].astype(BF16)
        vc = vc_ref[hh].astype(BF16)

        def row(r, carry, hh=hh, kc=kc, vc=vc):
            r_start = jnp.clip(r - NA_KH // 2, 0, GRID_R - NA_KH)
            off = r_start - r + (NA_KH - 1)
            q0 = pl.multiple_of(r * GRID_W, GRID_W)
            k0 = pl.multiple_of(r_start * GRID_W, GRID_W)
            q = (q_ref[hh, pl.ds(q0, GRID_W), :] * (NA_HD ** -0.5)).astype(BF16)
            kl = k_ref[hh, pl.ds(k0, n_loc), :].astype(BF16)
            vl = v_ref[hh, pl.ds(k0, n_loc), :].astype(BF16)
            s_loc = lax.dot_general(q, kl, (((1,), (1,)), ((), ())), preferred_element_type=F32)
            s_loc = s_loc + bias_ref[hh, off] + cmask
            s_ctx = lax.dot_general(q, kc, (((1,), (1,)), ((), ())), preferred_element_type=F32)
            m = jnp.maximum(jnp.max(s_loc, axis=-1, keepdims=True), jnp.max(s_ctx, axis=-1, keepdims=True))
            p_loc = jnp.exp(s_loc - m)
            p_ctx = jnp.exp(s_ctx - m)
            l = jnp.sum(p_loc, axis=-1, keepdims=True) + jnp.sum(p_ctx, axis=-1, keepdims=True)
            o = (_dot(p_loc.astype(BF16), vl) + _dot(p_ctx.astype(BF16), vc)) / l
            o_ref[pl.ds(q0, GRID_W), hh * NA_HD:(hh + 1) * NA_HD] = o.astype(o_ref.dtype)
            return carry

        lax.fori_loop(0, GRID_R, row, 0)


def na_attention(qkv, k_ctx, v_ctx, bias8):
    heads = 2
    _, cmask = _na_tables()
    blk0 = N_P // DEC_SEQ
    spec = lambda part: pl.BlockSpec((None, heads, DEC_SEQ, NA_HD), lambda b, h: (part, h, b + blk0, 0))
    cspec = pl.BlockSpec((None, heads, PAST_LEN, NA_HD), lambda b, h: (b, h, 0, 0))
    return pl.pallas_call(
        functools.partial(_na_attn_kernel, heads=heads),
        out_shape=jax.ShapeDtypeStruct((N_S, D), BF16),
        grid=(DEC_BATCH, NA_HEADS // heads),
        in_specs=[spec(0), spec(1), spec(2), cspec, cspec,
                  pl.BlockSpec((heads, NA_KH, GRID_W, NA_KH * GRID_W), lambda b, h: (h, 0, 0, 0)),
                  pl.BlockSpec((GRID_W, NA_KH * GRID_W), lambda b, h: (0, 0))],
        out_specs=pl.BlockSpec((DEC_SEQ, heads * NA_HD), lambda b, h: (b, h)),
        compiler_params=_cparams(("arbitrary", "arbitrary")),
    )(qkv, qkv, qkv, k_ctx, v_ctx, bias8, jnp.asarray(cmask))


def mixer_c(h, k_ctx, v_ctx, w_in, rpb, w_out_unused=None):
    qkv = token_matmul([h], [w_in], tn=512, mode="heads")
    o_p = ctx_attention(qkv)
    b15 = rpb_expand(rpb).reshape(NA_HEADS, 16, GRID_W, GRID_W)
    bias8 = jnp.stack([b15[:, o:o + NA_KH] for o in range(NA_KH)], axis=1)
    bias8 = bias8.transpose(0, 1, 3, 2, 4).reshape(NA_HEADS, NA_KH, GRID_W, NA_KH * GRID_W)
    o_s = na_attention(qkv, k_ctx, v_ctx, bias8)
    return jnp.concatenate([o_p, o_s], axis=0), qkv


def _mlstm_kernel(*refs, seq, zero_state, emit_state):
    q_ref, k_ref, v_ref, og_ref, gates_ref, wq_ref, wk_ref, gh_ref = refs[:8]
    pos = 8
    if not zero_state:
        c0_ref, n0_ref, m0_ref = refs[pos:pos + 3]
        pos += 3
    y_ref = refs[pos]
    pos += 1
    if emit_state:
        cn_ref, nn_ref, mn_ref = refs[pos:pos + 3]
        pos += 3
    qs, ks, gl, hdir, cst, nst, mst = refs[pos:]

    head = pl.program_id(1)
    nc = seq // CHUNK
    r = lax.broadcasted_iota(jnp.int32, (seq, 1), 0)
    first, last = r == 0, r == seq - 1
    qc_all = _dwconv3(q_ref[...], wq_ref, first, last)
    qs[...] = qc_all * jax.nn.sigmoid(qc_all)
    kc_all = _dwconv3(k_ref[...], wk_ref, first, last)
    ks[...] = kc_all * jax.nn.sigmoid(kc_all) * (HD_M ** -0.5)
    lane_g = lax.broadcasted_iota(jnp.int32, (1, LANE), 1)
    is_forget = (lane_g % 8) >= HEADS_M
    g_all = gates_ref[...]
    gl[...] = jnp.where(is_forget, jax.nn.log_sigmoid(g_all), g_all)

    if zero_state:
        cst[...] = jnp.zeros_like(cst)
        nst[...] = jnp.zeros_like(nst)
        mst[...] = jnp.zeros_like(mst)
    else:
        cst[...] = c0_ref[...]
        nst[...] = n0_ref[...]
        mst[...] = jnp.broadcast_to(m0_ref[...], mst.shape)

    tt = lax.broadcasted_iota(jnp.int32, (CHUNK, CHUNK), 0)
    ss = lax.broadcasted_iota(jnp.int32, (CHUNK, CHUNK), 1)
    lane = lax.broadcasted_iota(jnp.int32, (CHUNK, LANE), 1)
    sub = lax.broadcasted_iota(jnp.int32, (LANE, CHUNK), 0)

    def chunk(c, carry):
        for d in range(2):
            cc = c if d == 0 else nc - 1 - c
            r0 = pl.multiple_of(cc * CHUNK, CHUNK)
            mask = (ss <= tt) if d == 0 else (ss >= tt)
            tri = mask.astype(F32)
            g = gl[pl.ds(r0, CHUNK), :]
            gt = g.T
            bcum = jnp.dot(tri, g, precision=HIGHEST, preferred_element_type=F32)
            bcum_t = lax.dot_general(gt, tri, (((1,), (1,)), ((), ())), precision=HIGHEST,
                                     preferred_element_type=F32)
            i_idx = d * 2 * HEADS_M + head
            f_idx = i_idx + HEADS_M
            bcol = jnp.sum(jnp.where(lane == f_idx, bcum, 0.0), axis=1, keepdims=True)
            icol = jnp.sum(jnp.where(lane == i_idx, g, 0.0), axis=1, keepdims=True)
            brow = jnp.sum(jnp.where(sub == f_idx, bcum_t, 0.0), axis=0, keepdims=True)
            irow = jnp.sum(jnp.where(sub == i_idx, gt, 0.0), axis=0, keepdims=True)
            bend = bcol[CHUNK - 1:CHUNK, :] if d == 0 else bcol[0:1, :]
            m_prev = mst[d][:, 0:1]
            c_prev = cst[d]
            n_prev = nst[d]
            qc = qs[pl.ds(r0, CHUNK), :]
            kc = ks[pl.ds(r0, CHUNK), :]
            vc = v_ref[pl.ds(r0, CHUNK), :]
            qb, kb, vb = qc.astype(BF16), kc.astype(BF16), vc.astype(BF16)

            dmat = jnp.where(mask, bcol - brow + irow, -jnp.inf)
            m_inter = bcol + m_prev
            m_t = jnp.maximum(m_inter, jnp.max(dmat, axis=1, keepdims=True))
            qk = lax.dot_general(qb, kb, (((1,), (1,)), ((), ())), preferred_element_type=F32)
            s = jnp.exp(dmat - m_t) * qk
            sc = jnp.exp(m_inter - m_t)
            num = sc * _dot(qb, c_prev.astype(BF16)) + _dot(s.astype(BF16), vb)
            den = sc * jnp.sum(qc * n_prev, axis=1, keepdims=True) + jnp.sum(s, axis=1, keepdims=True)
            hdir[d, pl.ds(r0, CHUNK), :] = num / jnp.maximum(jnp.abs(den), jnp.exp(-m_t))

            to_end = bend - bcol + icol
            m_new = jnp.maximum(bend + m_prev, jnp.max(to_end, axis=0, keepdims=True))
            w = jnp.exp(to_end - m_new)
            decay = jnp.exp(bend + m_prev - m_new)
            kw = kc * w
            cst[d] = decay * c_prev + _dot(kw.T.astype(BF16), vb)
            nst[d] = decay * n_prev + jnp.sum(kw, axis=0, keepdims=True)
            mst[d] = jnp.broadcast_to(m_new, (1, LANE))
        return carry

    lax.fori_loop(0, nc, chunk, 0)

    hsum = hdir[0] + hdir[1]
    hn = hsum * lax.rsqrt(jnp.mean(hsum * hsum, axis=-1, keepdims=True) + EPS) * gh_ref[...]
    y_ref[...] = (hn * jax.nn.sigmoid(og_ref[...])).astype(y_ref.dtype)
    if emit_state:
        cn_ref[...] = cst[...]
        nn_ref[...] = nst[...]
        mn_ref[...] = mst[...]


def mlstm(proj, gates, w_conv_qk, g_head, state, *, nb, seq, row0):
    blk0 = row0 // seq
    zero_state = state is None
    col = lambda part: pl.BlockSpec((seq, HD_M), lambda b, h: (b + blk0, part * HEADS_M + h))
    in_specs = [col(0), col(1), col(2), col(3),
                pl.BlockSpec((seq, LANE), lambda b, h: (b + blk0, 0)),
                pl.BlockSpec((3, HD_M), lambda b, h: (0, h)),
                pl.BlockSpec((3, HD_M), lambda b, h: (0, HEADS_M + h)),
                pl.BlockSpec((1, HD_M), lambda b, h: (0, h))]
    args = [proj, proj, proj, proj, gates, w_conv_qk, w_conv_qk, g_head.reshape(1, MW)]
    if not zero_state:
        c0, n0, m0 = state
        in_specs += [pl.BlockSpec((None, 2, None, HD_M, HD_M), lambda b, h: (b, 0, h, 0, 0)),
                     pl.BlockSpec((None, 2, None, 1, HD_M), lambda b, h: (b, 0, h, 0, 0)),
                     pl.BlockSpec((None, 2, None, 1, 1), lambda b, h: (b, 0, h, 0, 0))]
        args += [c0, n0.reshape(nb, 2, HEADS_M, 1, HD_M), m0.reshape(nb, 2, HEADS_M, 1, 1)]
    out_shape = [jax.ShapeDtypeStruct((nb * seq, MW), BF16)]
    out_specs = [pl.BlockSpec((seq, HD_M), lambda b, h: (b, h))]
    if zero_state:
        out_shape += [jax.ShapeDtypeStruct((nb, 2, HEADS_M, HD_M, HD_M), F32),
                      jax.ShapeDtypeStruct((nb, 2, HEADS_M, 1, HD_M), F32),
                      jax.ShapeDtypeStruct((nb, 2, HEADS_M, 1, LANE), F32)]
        out_specs += [pl.BlockSpec((None, 2, None, HD_M, HD_M), lambda b, h: (b, 0, h, 0, 0)),
                      pl.BlockSpec((None, 2, None, 1, HD_M), lambda b, h: (b, 0, h, 0, 0)),
                      pl.BlockSpec((None, 2, None, 1, LANE), lambda b, h: (b, 0, h, 0, 0))]
    return pl.pallas_call(
        functools.partial(_mlstm_kernel, seq=seq, zero_state=zero_state, emit_state=zero_state),
        out_shape=out_shape,
        grid=(nb, HEADS_M),
        in_specs=in_specs,
        out_specs=out_specs,
        scratch_shapes=[pltpu.VMEM((seq, HD_M), F32), pltpu.VMEM((seq, HD_M), F32), pltpu.VMEM((seq, LANE), F32),
                        pltpu.VMEM((2, seq, HD_M), F32), pltpu.VMEM((2, HD_M, HD_M), F32),
                        pltpu.VMEM((2, 1, HD_M), F32), pltpu.VMEM((2, 1, LANE), F32)],
        compiler_params=_cparams(("arbitrary", "arbitrary")),
    )(*args)


def _hyena_pre_kernel(v_ref, x1_ref, x2_ref, wv_ref, w1_ref, w2_ref, u_ref, x2c_ref):
    first, last = _seq_edges(SEQ_BLOCK, pl.program_id(0) < N_PBLK)
    u_ref[...] = _dwconv3(x1_ref[...], w1_ref, first, last) * _dwconv3(v_ref[...], wv_ref, first, last)
    x2c_ref[...] = _dwconv3(x2_ref[...], w2_ref, first, last)


def hyena_pre(proj, w_conv_hy):
    tc = 256
    nct = HW // tc
    c0 = 4 * MW // tc
    pcol = lambda part: pl.BlockSpec((SEQ_BLOCK, tc), lambda i, j: (i, c0 + part * nct + j))
    wcol = lambda part: pl.BlockSpec((3, tc), lambda i, j: (0, part * nct + j))
    out = jax.ShapeDtypeStruct((N_TOK, HW), F32)
    ospec = pl.BlockSpec((SEQ_BLOCK, tc), lambda i, j: (i, j))
    return pl.pallas_call(
        _hyena_pre_kernel,
        out_shape=[out, out],
        grid=(N_TOK // SEQ_BLOCK, nct),
        in_specs=[pcol(0), pcol(1), pcol(2), wcol(0), wcol(1), wcol(2)],
        out_specs=[ospec, ospec],
        compiler_params=_cparams(("arbitrary", "arbitrary")),
    )(proj, proj, proj, w_conv_hy, w_conv_hy, w_conv_hy)


@functools.lru_cache(maxsize=None)
def _filter_tables(seq):
    t = np.linspace(0.0, 1.0, seq)[:, None]
    wpos = 2.0 * np.pi * np.arange(seq)[:, None] / seq
    bands = np.linspace(1e-4, N_BANDS - 1, N_BANDS)[None, :]
    z = np.concatenate([t, np.cos(bands * wpos), -np.sin(bands * wpos)], axis=-1)
    z = np.pad(z, ((0, 0), (0, LANE - FILTER_EMB)))
    max_decay = math.log(DECAY_TARGET) / DECAY_FAST
    min_decay = math.log(DECAY_TARGET) / DECAY_SLOW
    deltas = np.abs(np.linspace(min_decay, max_decay, HW))
    decay = np.exp(-t * np.concatenate([deltas, deltas])[None, :])
    return z.astype(np.float32), decay.astype(np.float32)


@functools.lru_cache(maxsize=None)
def _dft_tables(seq, tk):
    n = 2 * seq
    k = np.arange(seq)[:, None]
    t = np.arange(seq)[None, :]
    ang = 2.0 * np.pi * ((k * t) % n) / n
    alt = np.where(np.arange(seq) % 2 == 0, 1.0, -1.0)
    cm, sm = np.cos(ang), np.sin(ang)
    sm[0, :] = alt
    fwd = np.stack([cm.reshape(seq // tk, tk, seq), sm.reshape(seq // tk, tk, seq)], axis=1)
    wk = np.where(np.arange(seq) == 0, 1.0, 2.0)[None, :]
    ci = (np.cos(ang.T) * wk) / n
    si = np.sin(ang.T) * 2.0 / n
    si[:, 0] = alt / n
    inv = np.concatenate([ci, si], axis=1)
    return fwd.astype(np.float32), inv.astype(np.float32)


def _filter_kernel(z_ref, w1_ref, b1_ref, w2_ref, b2_ref, w3_ref, fr_ref, dec_ref, hs_ref, hd_ref):
    fr = fr_ref[...]
    hp = functools.partial(jnp.dot, precision=HIGHEST, preferred_element_type=F32)
    h1 = jnp.sin(fr * (hp(z_ref[...], w1_ref[...]) + b1_ref[...]))
    h2 = jnp.sin(fr * (hp(h1, w2_ref[...]) + b2_ref[...]))
    filt = hp(h2, w3_ref[...]) * dec_ref[...]
    past, fut = filt[:, :HW], filt[:, HW:]
    rows = filt.shape[0]
    grow = lax.broadcasted_iota(jnp.int32, (rows, 1), 0) + pl.program_id(0) * rows
    fut = jnp.where(grow == 0, 0.0, fut)
    hs_ref[...] = past + fut
    hd_ref[...] = past - fut


def filter_gen(seq, w1, b1, w2, b2, w3, freq):
    z, decay = _filter_tables(seq)
    tl = 256
    fh = FILTER_HIDDEN
    full = lambda shape: pl.BlockSpec(shape, lambda i: (0, 0))
    out = jax.ShapeDtypeStruct((seq, HW), F32)
    return pl.pallas_call(
        _filter_kernel,
        out_shape=[out, out],
        grid=(seq // tl,),
        in_specs=[pl.BlockSpec((tl, LANE), lambda i: (i, 0)), full((LANE, fh)), full((1, fh)), full((fh, fh)),
                  full((1, fh)), full((fh, 2 * HW)), full((1, fh)), pl.BlockSpec((tl, 2 * HW), lambda i: (i, 0))],
        out_specs=[pl.BlockSpec((tl, HW), lambda i: (i, 0))] * 2,
        compiler_params=_cparams(("arbitrary",)),
    )(jnp.asarray(z), jnp.pad(w1, ((0, LANE - FILTER_EMB), (0, 0))), b1.reshape(1, fh), w2, b2.reshape(1, fh), w3,
      freq.reshape(1, fh), jnp.asarray(decay))


def _dft_filter_kernel(a_ref, hs_ref, hd_ref, k_ref, hs_bf, hd_bf):
    @pl.when(pl.program_id(0) == 0)
    def _():
        hs_bf[...] = hs_ref[...].astype(BF16)
        hd_bf[...] = hd_ref[...].astype(BF16)

    k_ref[0] = _dot(a_ref[0], hs_bf[...])
    k_ref[1] = _dot(a_ref[1], hd_bf[...])

    @pl.when(pl.program_id(0) == 0)
    def _():
        k_ref[1, 0:1, :] = _dot(a_ref[1, 0:8, :], hs_bf[...])[0:1, :]


def dft_filter(seq, tk, hs, hd):
    fwd, _ = _dft_tables(seq, tk)
    return pl.pallas_call(
        _dft_filter_kernel,
        out_shape=jax.ShapeDtypeStruct((2, seq, HW), F32),
        grid=(seq // tk,),
        in_specs=[pl.BlockSpec((None, 2, tk, seq), lambda m: (m, 0, 0, 0)),
                  pl.BlockSpec((seq, HW), lambda m: (0, 0)), pl.BlockSpec((seq, HW), lambda m: (0, 0))],
        out_specs=pl.BlockSpec((2, tk, HW), lambda m: (0, m, 0)),
        scratch_shapes=[pltpu.VMEM((seq, HW), BF16), pltpu.VMEM((seq, HW), BF16)],
        compiler_params=_cparams(("arbitrary",)),
    )(jnp.asarray(fwd).astype(BF16), hs, hd)


def _dft_fwd_kernel(a_ref, u_ref, k_ref, y_ref, u_bf):
    m = pl.program_id(1)

    @pl.when(m == 0)
    def _():
        u_bf[...] = u_ref[...].astype(BF16)

    ure = _dot(a_ref[0], u_bf[...])
    uim = _dot(a_ref[1], u_bf[...])
    kre, kim = k_ref[0], k_ref[1]
    packed = (lax.broadcasted_iota(jnp.int32, (ure.shape[0], 1), 0) == 0) & (m == 0)
    y_ref[0] = jnp.where(packed, ure * kre, ure * kre - uim * kim).astype(y_ref.dtype)
    y_ref[1] = jnp.where(packed, uim * kim, ure * kim + uim * kre).astype(y_ref.dtype)


def dft_fwd(seq, tk, u, kf, *, nb, row0):
    fwd, _ = _dft_tables(seq, tk)
    blk0 = row0 // seq
    return pl.pallas_call(
        _dft_fwd_kernel,
        out_shape=jax.ShapeDtypeStruct((nb, 2, seq, HW), BF16),
        grid=(nb, seq // tk),
        in_specs=[pl.BlockSpec((None, 2, tk, seq), lambda b, m: (m, 0, 0, 0)),
                  pl.BlockSpec((seq, HW), lambda b, m: (b + blk0, 0)),
                  pl.BlockSpec((2, tk, HW), lambda b, m: (0, m, 0))],
        out_specs=pl.BlockSpec((None, 2, tk, HW), lambda b, m: (b, 0, m, 0)),
        scratch_shapes=[pltpu.VMEM((seq, HW), BF16)],
        compiler_params=_cparams(("arbitrary", "arbitrary")),
    )(jnp.asarray(fwd).astype(BF16), u, kf)


def _dft_inv_kernel(a_ref, y_ref, u_ref, x2_ref, bias_ref, o_ref, *, seq):
    conv = _dot(a_ref[:, :seq], y_ref[0]) + _dot(a_ref[:, seq:], y_ref[1])
    o_ref[...] = (x2_ref[...] * (conv + bias_ref[...] * u_ref[...])).astype(o_ref.dtype)


def dft_inv(seq, tk, y, u, x2c, bias, *, nb, row0):
    _, inv = _dft_tables(seq, tk)
    tm = min(seq, 512)
    nt = seq // tm
    blk0 = row0 // tm
    rows = lambda b, t: (b * nt + t + blk0, 0)
    return pl.pallas_call(
        functools.partial(_dft_inv_kernel, seq=seq),
        out_shape=jax.ShapeDtypeStruct((nb * seq, HW), BF16),
        grid=(nb, nt),
        in_specs=[pl.BlockSpec((tm, 2 * seq), lambda b, t: (t, 0)),
                  pl.BlockSpec((None, 2, seq, HW), lambda b, t: (b, 0, 0, 0)),
                  pl.BlockSpec((tm, HW), rows), pl.BlockSpec((tm, HW), rows),
                  pl.BlockSpec((1, HW), lambda b, t: (0, 0))],
        out_specs=pl.BlockSpec((tm, HW), lambda b, t: (b * nt + t, 0)),
        compiler_params=_cparams(("arbitrary", "arbitrary")),
    )(jnp.asarray(inv).astype(BF16), y, u, x2c, bias.reshape(1, HW))


def hyena_group(seq, u, x2c, filt_w, bias, *, nb, row0):
    tk = min(seq, 512)
    hs, hd = filter_gen(seq, *filt_w)
    kf = dft_filter(seq, tk, hs, hd)
    y = dft_fwd(seq, tk, u, kf, nb=nb, row0=row0)
    return dft_inv(seq, tk, y, u, x2c, bias, nb=nb, row0=row0)


def mixer_ab_parts(h, state_s, w_in, b_gates, w_conv_qk, g_head, w_conv_hy, w_f1, b_f1, w_f2, b_f2, w_f3, freq,
                   hy_bias):
    n_g = 4 * HEADS_M
    w_main = jnp.concatenate([w_in[:, :4 * MW], w_in[:, 4 * MW + n_g:]], axis=1)
    w_g = jnp.pad(w_in[:, 4 * MW:4 * MW + n_g], ((0, 0), (0, LANE - n_g)))
    b_g = jnp.pad(b_gates, (0, LANE - n_g)).reshape(1, LANE)
    proj = token_matmul([h], [w_main], tn=512)
    gates = token_matmul([h], [w_g], tn=LANE, mode="bias", bias=b_g)
    ym_p, c_p, n_p, m_p = mlstm(proj, gates, w_conv_qk, g_head, None, nb=BATCH, seq=SEQ, row0=0)
    (ym_s,) = mlstm(proj, gates, w_conv_qk, g_head, state_s, nb=DEC_BATCH, seq=DEC_SEQ, row0=N_P)
    u, x2c = hyena_pre(proj, w_conv_hy)
    filt_w = (w_f1, b_f1, w_f2, b_f2, w_f3, freq)
    yh_p = hyena_group(SEQ, u, x2c, filt_w, hy_bias, nb=BATCH, row0=0)
    yh_s = hyena_group(DEC_SEQ, u, x2c, filt_w, hy_bias, nb=DEC_BATCH, row0=N_P)
    state_p = (c_p, n_p[:, :, :, 0, :], m_p[:, :, :, 0, 0])
    return jnp.concatenate([ym_p, ym_s], axis=0), jnp.concatenate([yh_p, yh_s], axis=0), state_p


def conv_ffn(x, mod, g, w_up, w_conv, w_down):
    h = norm_mod(x, g, mod, 3, 4)
    mid = ffn_mid(token_matmul([h], [w_up], tn=512), w_conv)
    return token_matmul([mid], [w_down], tn=512, mode="residual", x=x, mod=mod, gt_idx=5)


def kernel(x_prompt, x_sample, state_mlstm_C, state_mlstm_n, state_mlstm_m, cache_na_k, cache_na_v, c, c_ctx, w_ada, b_ada, g_mix, g_ffn, g_final, w_in_ab, b_gates, w_conv_qk, g_mlstm, w_conv_hy, w_filt1, b_filt1, w_filt2, b_filt2, w_filt3, filt_freq, hyena_bias, w_out_ab, w_in_c, rpb_c, w_out_c, w_up, w_conv_ffn, w_down):
    cmat = jnp.concatenate([c, c_ctx[None, :], jnp.zeros((MOD_ROWS - DEC_BATCH - 1, D), F32)], axis=0)
    mod_all = adaln_all(cmat, w_ada, b_ada).reshape(DEPTH, MOD_ROWS, 6, 1, D)
    x = jnp.concatenate([x_prompt.reshape(N_P, D), x_sample.reshape(N_S, D)], axis=0)
    new_c, new_n, new_m, new_k, new_v = [], [], [], [], []
    for l in range(DEPTH):
        mod = mod_all[l]
        h = norm_mod(x, g_mix[l], mod, 0, 1)
        e = l // 2
        if l % 2 == 0:
            state_s = (state_mlstm_C[:, e], state_mlstm_n[:, e], state_mlstm_m[:, e])
            y_m, y_h, (c_p, n_p, m_p) = mixer_ab_parts(
                h, state_s, w_in_ab[e], b_gates[e], w_conv_qk[e], g_mlstm[e], w_conv_hy[e], w_filt1[e], b_filt1[e],
                w_filt2[e], b_filt2[e], w_filt3[e], filt_freq[e], hyena_bias[e])
            x = token_matmul([y_m, y_h], [w_out_ab[e][:MW], w_out_ab[e][MW:]], tn=512, mode="residual", x=x, mod=mod,
                             gt_idx=2)
            new_c.append(c_p)
            new_n.append(n_p)
            new_m.append(m_p)
        else:
            o, qkv = mixer_c(h, cache_na_k[:, e], cache_na_v[:, e], w_in_c[e], rpb_c[e])
            x = token_matmul([o], [w_out_c[e]], tn=512, mode="residual", x=x, mod=mod, gt_idx=2)
            kv_p = qkv[1:, :, :N_P].reshape(2, NA_HEADS, BATCH, SEQ, NA_HD).transpose(0, 2, 1, 3, 4)
            new_k.append(kv_p[0])
            new_v.append(kv_p[1])
        x = conv_ffn(x, mod, g_ffn[l], w_up[l], w_conv_ffn[l], w_down[l])
    y_prompt = final_norm(x, g_final, 0, N_P).reshape(BATCH, SEQ, D)
    y_sample = final_norm(x, g_final, N_P, N_S).reshape(DEC_BATCH, DEC_SEQ, D)
    return (y_prompt, y_sample, jnp.stack(new_c, axis=1), jnp.stack(new_n, axis=1), jnp.stack(new_m, axis=1),
            jnp.stack(new_k, axis=1), jnp.stack(new_v, axis=1))
```

```python
import functools
import math

import numpy as np
import jax
import jax.numpy as jnp
from jax import lax
from jax.experimental import pallas as pl
from jax.experimental.pallas import tpu as pltpu

F32 = jnp.float32
BF16 = jnp.bfloat16

D = 1024
BATCH, SEQ = 16, 256
DEC_BATCH, DEC_SEQ = 4, 2048
PAST_LEN = 512
DEPTH = 2
GRID_W = 64
GRID_R = DEC_SEQ // GRID_W
HEADS_M = 4
MW = D // 2
HD_M = MW // HEADS_M
CHUNK = 64
HW = D // 2
N_BANDS = 16
FILTER_EMB = 2 * N_BANDS + 1
FILTER_HIDDEN = 64
DECAY_FAST, DECAY_SLOW, DECAY_TARGET = 0.3, 1.5, 1e-2
NA_HEADS = 16
NA_HD = D // NA_HEADS
NA_KH, NA_KW = 8, 16
FF = 2816
EPS = 1e-6

N_P = BATCH * SEQ
N_S = DEC_BATCH * DEC_SEQ
N_TOK = N_P + N_S
CTX_ROW = DEC_BATCH
MOD_ROWS = 8
LANE = 128
VMEM_LIMIT = 48 * 1024 * 1024
HIGHEST = lax.Precision.HIGHEST


def _cparams(sem):
    return pltpu.CompilerParams(dimension_semantics=sem, vmem_limit_bytes=VMEM_LIMIT)


def _mod_row(i, bm):
    return jnp.where(i < N_P // bm, CTX_ROW, (i - N_P // bm) // (DEC_SEQ // bm))


def _dot(a, b):
    return jnp.dot(a, b, preferred_element_type=F32)


def _adaln_kernel(c_ref, w_ref, b_ref, o_ref):
    cv = c_ref[...]
    s = cv * jax.nn.sigmoid(cv)
    o_ref[...] = _dot(s.astype(BF16), w_ref[...].astype(BF16)) + b_ref[...]


def adaln_all(cmat, w_ada, b_ada):
    tn = 1024
    return pl.pallas_call(
        _adaln_kernel,
        out_shape=jax.ShapeDtypeStruct((DEPTH, MOD_ROWS, 6 * D), F32),
        grid=(DEPTH, 6 * D // tn),
        in_specs=[
            pl.BlockSpec((MOD_ROWS, D), lambda l, j: (0, 0)),
            pl.BlockSpec((None, D, tn), lambda l, j: (l, 0, j)),
            pl.BlockSpec((None, 1, tn), lambda l, j: (l, 0, j)),
        ],
        out_specs=pl.BlockSpec((None, MOD_ROWS, tn), lambda l, j: (l, 0, j)),
        compiler_params=_cparams(("arbitrary", "arbitrary")),
        name="adaln",
    )(cmat, w_ada, b_ada.reshape(DEPTH, 1, 6 * D))


def _norm_mod_kernel(x_ref, g_ref, sh_ref, sc_ref, o_ref):
    x = x_ref[...]
    y = x * lax.rsqrt(jnp.mean(x * x, axis=-1, keepdims=True) + EPS)
    o_ref[...] = ((y * g_ref[...]) * (1.0 + sc_ref[...]) + sh_ref[...]).astype(o_ref.dtype)


def norm_mod(x, g, mod, sh_idx, sc_idx):
    bm = 512
    return pl.pallas_call(
        _norm_mod_kernel,
        out_shape=jax.ShapeDtypeStruct((N_TOK, D), BF16),
        grid=(N_TOK // bm,),
        in_specs=[
            pl.BlockSpec((bm, D), lambda i: (i, 0)),
            pl.BlockSpec((1, D), lambda i: (0, 0)),
            pl.BlockSpec((None, None, 1, D), lambda i: (_mod_row(i, bm), sh_idx, 0, 0)),
            pl.BlockSpec((None, None, 1, D), lambda i: (_mod_row(i, bm), sc_idx, 0, 0)),
        ],
        out_specs=pl.BlockSpec((bm, D), lambda i: (i, 0)),
        compiler_params=_cparams(("arbitrary",)),
        name="norm_mod",
    )(x, g.reshape(1, D), mod, mod)


def _final_norm_kernel(x_ref, g_ref, o_ref):
    x = x_ref[...]
    y = x * lax.rsqrt(jnp.mean(x * x, axis=-1, keepdims=True) + EPS)
    o_ref[...] = y * g_ref[...]


def final_norm(x, g, row0, nrows):
    bm = 512
    return pl.pallas_call(
        _final_norm_kernel,
        out_shape=jax.ShapeDtypeStruct((nrows, D), F32),
        grid=(nrows // bm,),
        in_specs=[
            pl.BlockSpec((bm, D), lambda i: (i + row0 // bm, 0)),
            pl.BlockSpec((1, D), lambda i: (0, 0)),
        ],
        out_specs=pl.BlockSpec((bm, D), lambda i: (i, 0)),
        compiler_params=_cparams(("arbitrary",)),
        name="final_norm",
    )(x, g.reshape(1, D))


def _mm_kernel(*refs, n_a, mode, heads_per_tile):
    a_refs = refs[:n_a]
    w_refs = refs[n_a:2 * n_a]
    rest = refs[2 * n_a:]
    if mode == "residual":
        x_ref, gt_ref, o_ref = rest[0], rest[1], rest[2]
        wbf = rest[3:]
    elif mode == "bias":
        b_ref, o_ref = rest[0], rest[1]
        wbf = rest[2:]
    else:
        o_ref = rest[0]
        wbf = rest[1:]

    @pl.when(pl.program_id(1) == 0)
    def _():
        for w_ref, wb in zip(w_refs, wbf):
            wb[...] = w_ref[...].astype(BF16)

    acc = _dot(a_refs[0][...], wbf[0][...])
    for a_ref, wb in zip(a_refs[1:], wbf[1:]):
        acc = acc + _dot(a_ref[...], wb[...])
    if mode == "residual":
        o_ref[...] = x_ref[...] + gt_ref[...] * acc
    elif mode == "bias":
        o_ref[...] = acc + b_ref[...]
    elif mode == "heads":
        for hh in range(heads_per_tile):
            o_ref[hh] = acc[:, hh * NA_HD:(hh + 1) * NA_HD]
    else:
        o_ref[...] = acc.astype(o_ref.dtype)


def token_matmul(a_list, w_list, *, tn, bm=512, mode="plain", x=None, mod=None, gt_idx=None, bias=None,
                 out_dtype=F32, name="token_matmul"):
    n_a = len(a_list)
    nout = w_list[0].shape[1]
    grid = (nout // tn, N_TOK // bm)
    in_specs = [pl.BlockSpec((bm, a.shape[1]), lambda j, i: (i, 0)) for a in a_list]
    in_specs += [pl.BlockSpec((w.shape[0], tn), lambda j, i: (0, j)) for w in w_list]
    args = list(a_list) + list(w_list)
    heads_per_tile = tn // NA_HD
    if mode == "residual":
        in_specs += [pl.BlockSpec((bm, tn), lambda j, i: (i, j)),
                     pl.BlockSpec((None, None, 1, tn), lambda j, i: (_mod_row(i, bm), gt_idx, 0, j))]
        args += [x, mod]
    elif mode == "bias":
        in_specs += [pl.BlockSpec((1, tn), lambda j, i: (0, j))]
        args += [bias]
    if mode == "heads":
        tiles_per_part = D // tn
        out_shape = jax.ShapeDtypeStruct((3, NA_HEADS, N_TOK, NA_HD), F32)
        out_spec = pl.BlockSpec((None, heads_per_tile, bm, NA_HD),
                                lambda j, i: (j // tiles_per_part, j % tiles_per_part, i, 0))
    else:
        out_shape = jax.ShapeDtypeStruct((N_TOK, nout), out_dtype)
        out_spec = pl.BlockSpec((bm, tn), lambda j, i: (i, j))
    return pl.pallas_call(
        functools.partial(_mm_kernel, n_a=n_a, mode=mode, heads_per_tile=heads_per_tile),
        out_shape=out_shape,
        grid=grid,
        in_specs=in_specs,
        out_specs=out_spec,
        scratch_shapes=[pltpu.VMEM((w.shape[0], tn), BF16) for w in w_list],
        compiler_params=_cparams(("arbitrary", "arbitrary")),
        name=name,
    )(*args)


SEQ_BLOCK = DEC_SEQ
N_PBLK = N_P // SEQ_BLOCK


def _seq_edges(rows, is_prompt):
    r = lax.broadcasted_iota(jnp.int32, (rows, 1), 0)
    first = (r == 0) | (is_prompt & (r % SEQ == 0))
    last = (r == rows - 1) | (is_prompt & (r % SEQ == SEQ - 1))
    return first, last


def _dwconv3(x, w_ref, first, last):
    rows = x.shape[0]
    prev = jnp.where(first, 0.0, pltpu.roll(x, 1, 0))
    nxt = jnp.where(last, 0.0, pltpu.roll(x, rows - 1, 0))
    return prev * w_ref[0:1, :] + x * w_ref[1:2, :] + nxt * w_ref[2:3, :]


def _gelu_tanh(x):
    return 0.5 * x * (1.0 + jnp.tanh(math.sqrt(2.0 / math.pi) * (x + 0.044715 * (x * x * x))))


def _ffn_up_kernel(h_ref, wa_ref, wg_ref, wc_ref, o_ref):
    first, last = _seq_edges(SEQ_BLOCK, pl.program_id(0) < N_PBLK)
    h = h_ref[...]
    a = _dot(h, wa_ref[...].astype(BF16))
    g = _dot(h, wg_ref[...].astype(BF16))
    o_ref[...] = (_gelu_tanh(_dwconv3(a, wc_ref, first, last)) * g).astype(o_ref.dtype)


def ffn_up(h, w_up, w_conv):
    tc = 256
    nct = FF // tc
    return pl.pallas_call(
        _ffn_up_kernel,
        out_shape=jax.ShapeDtypeStruct((N_TOK, FF), BF16),
        grid=(N_TOK // SEQ_BLOCK, nct),
        in_specs=[
            pl.BlockSpec((SEQ_BLOCK, D), lambda i, j: (i, 0)),
            pl.BlockSpec((D, tc), lambda i, j: (0, j)),
            pl.BlockSpec((D, tc), lambda i, j: (0, j + nct)),
            pl.BlockSpec((3, tc), lambda i, j: (0, j)),
        ],
        out_specs=pl.BlockSpec((SEQ_BLOCK, tc), lambda i, j: (i, j)),
        compiler_params=_cparams(("arbitrary", "arbitrary")),
        name="ffn_up",
    )(h, w_up, w_up, w_conv)


def _ctx_attn_kernel(q_ref, k_ref, v_ref, o_ref, *, heads):
    outs = []
    for hh in range(heads):
        q = q_ref[hh].astype(BF16)
        k = k_ref[hh].astype(BF16)
        s = lax.dot_general(q, k, (((1,), (1,)), ((), ())), preferred_element_type=F32) * (NA_HD ** -0.5)
        m = jnp.max(s, axis=-1, keepdims=True)
        p = jnp.exp(s - m)
        l = jnp.sum(p, axis=-1, keepdims=True)
        outs.append(_dot(p.astype(BF16), v_ref[hh].astype(BF16)) / l)
    o_ref[...] = jnp.concatenate(outs, axis=-1).astype(o_ref.dtype)


def ctx_attention(qkv):
    heads = 4
    spec = lambda part: pl.BlockSpec((None, heads, SEQ, NA_HD), lambda b, h: (part, h, b, 0))
    return pl.pallas_call(
        functools.partial(_ctx_attn_kernel, heads=heads),
        out_shape=jax.ShapeDtypeStruct((N_P, D), BF16),
        grid=(BATCH, NA_HEADS // heads),
        in_specs=[spec(0), spec(1), spec(2)],
        out_specs=pl.BlockSpec((SEQ, heads * NA_HD), lambda b, h: (b, h)),
        compiler_params=_cparams(("arbitrary", "arbitrary")),
        name="ctx_attn",
    )(qkv, qkv, qkv)


def _na_tables():
    q = np.arange(GRID_W)[:, None]
    w = np.arange(GRID_W)[None, :]
    idx_c = np.clip(w - q + (NA_KW - 1), 0, 2 * NA_KW - 2)
    onehot = (idx_c.reshape(1, -1) == np.arange(32)[:, None]).astype(np.float32)
    c_start = np.clip(np.arange(GRID_W) - NA_KW // 2, 0, GRID_W - NA_KW)[:, None]
    inside = (w >= c_start) & (w < c_start + NA_KW)
    cmask = np.where(inside, 0.0, -np.inf).astype(np.float32)
    return onehot, np.tile(cmask, (1, 2))


def _rpb_expand_kernel(r_ref, e_ref, o_ref):
    o_ref[...] = jnp.dot(r_ref[...], e_ref[...], precision=HIGHEST, preferred_element_type=F32)


def rpb_expand(rpb):
    onehot, _ = _na_tables()
    rp = jnp.pad(rpb, ((0, 0), (0, 1), (0, 1)))
    return pl.pallas_call(
        _rpb_expand_kernel,
        out_shape=jax.ShapeDtypeStruct((NA_HEADS, 16, GRID_W * GRID_W), F32),
        grid=(NA_HEADS,),
        in_specs=[pl.BlockSpec((None, 16, 32), lambda h: (h, 0, 0)),
                  pl.BlockSpec((32, GRID_W * GRID_W), lambda h: (0, 0))],
        out_specs=pl.BlockSpec((None, 16, GRID_W * GRID_W), lambda h: (h, 0, 0)),
        compiler_params=_cparams(("arbitrary",)),
        name="rpb_expand",
    )(rp, jnp.asarray(onehot))


NA_QROWS = 8
NA_WIN = 2 * NA_QROWS


def _na_attn_kernel(q_ref, k_ref, v_ref, kc_ref, vc_ref, t_ref, o_ref, p_loc, p_ctx, *, heads):
    nq = NA_QROWS * GRID_W
    nk = NA_WIN * GRID_W
    pair = 2 * GRID_W
    nt = (((1,), (1,)), ((), ()))
    lane = lax.broadcasted_iota(jnp.int32, (GRID_W, pair), 1)
    zero_tile = jnp.zeros((GRID_W, pair), BF16)
    for hh in range(heads):
        kc = kc_ref[hh].astype(BF16)
        vc = vc_ref[hh].astype(BF16)
        for blk in range(GRID_R // NA_QROWS):
            k0 = min(max(NA_QROWS * blk - NA_KH // 2, 0), GRID_R - NA_WIN)
            q = (q_ref[hh, blk * nq:(blk + 1) * nq, :] * (NA_HD ** -0.5)).astype(BF16)
            kw = k_ref[hh, k0 * GRID_W:k0 * GRID_W + nk, :].astype(BF16)
            vw = v_ref[hh, k0 * GRID_W:k0 * GRID_W + nk, :].astype(BF16)
            s_loc = lax.dot_general(q, kw, nt, preferred_element_type=F32)
            s_ctx = lax.dot_general(q, kc, nt, preferred_element_type=F32)
            denoms = []
            for qi in range(NA_QROWS):
                r = NA_QROWS * blk + qi
                r_start = min(max(r - NA_KH // 2, 0), GRID_R - NA_KH)
                rows = slice(qi * GRID_W, (qi + 1) * GRID_W)
                ctx_tiles = [s_ctx[rows, c * pair:(c + 1) * pair] for c in range(PAST_LEN // pair)]
                tiles = {}
                for j in range(nk // pair):
                    kr = k0 + 2 * j
                    ok0 = r_start <= kr < r_start + NA_KH
                    ok1 = r_start <= kr + 1 < r_start + NA_KH
                    if not (ok0 or ok1):
                        continue
                    sb = s_loc[rows, j * pair:(j + 1) * pair] + t_ref[hh, kr - r + NA_KH]
                    if not (ok0 and ok1):
                        sb = jnp.where((lane < GRID_W) if ok0 else (lane >= GRID_W), sb, -jnp.inf)
                    tiles[j] = sb
                mx = functools.reduce(jnp.maximum, list(tiles.values()) + ctx_tiles)
                m = jnp.max(mx, axis=1, keepdims=True)
                acc = None
                for j in range(nk // pair):
                    if j in tiles:
                        p = jnp.exp(tiles[j] - m)
                        acc = p if acc is None else acc + p
                        p_loc[rows, j * pair:(j + 1) * pair] = p.astype(BF16)
                    else:
                        p_loc[rows, j * pair:(j + 1) * pair] = zero_tile
                for c, t in enumerate(ctx_tiles):
                    p = jnp.exp(t - m)
                    acc = acc + p
                    p_ctx[rows, c * pair:(c + 1) * pair] = p.astype(BF16)
                denoms.append(jnp.sum(acc, axis=1, keepdims=True))
            o = (_dot(p_loc[...], vw) + _dot(p_ctx[...], vc)) / jnp.concatenate(denoms, axis=0)
            o_ref[blk * nq:(blk + 1) * nq, hh * NA_HD:(hh + 1) * NA_HD] = o.astype(o_ref.dtype)


def na_attention(qkv, k_ctx, v_ctx, bias_pairs):
    heads = 2
    blk0 = N_P // DEC_SEQ
    spec = lambda part: pl.BlockSpec((None, heads, DEC_SEQ, NA_HD), lambda b, h: (part, h, b + blk0, 0))
    cspec = pl.BlockSpec((None, heads, PAST_LEN, NA_HD), lambda b, h: (b, h, 0, 0))
    return pl.pallas_call(
        functools.partial(_na_attn_kernel, heads=heads),
        out_shape=jax.ShapeDtypeStruct((N_S, D), BF16),
        grid=(DEC_BATCH, NA_HEADS // heads),
        in_specs=[spec(0), spec(1), spec(2), cspec, cspec,
                  pl.BlockSpec((heads, 16, GRID_W, 2 * GRID_W), lambda b, h: (h, 0, 0, 0))],
        out_specs=pl.BlockSpec((DEC_SEQ, heads * NA_HD), lambda b, h: (b, h)),
        scratch_shapes=[pltpu.VMEM((NA_QROWS * GRID_W, NA_WIN * GRID_W), BF16),
                        pltpu.VMEM((NA_QROWS * GRID_W, PAST_LEN), BF16)],
        compiler_params=_cparams(("arbitrary", "arbitrary")),
        name="na_attn",
    )(qkv, qkv, qkv, k_ctx, v_ctx, bias_pairs)


def mixer_c(h, k_ctx, v_ctx, w_in, rpb):
    qkv = token_matmul([h], [w_in], tn=1024, mode="heads", name="qkv_proj")
    o_p = ctx_attention(qkv)
    _, cmask2 = _na_tables()
    b15 = rpb_expand(rpb).reshape(NA_HEADS, 16, GRID_W, GRID_W)
    b17 = jnp.pad(b15, ((0, 0), (1, 0), (0, 0), (0, 0)))
    bias_pairs = jnp.concatenate([b17[:, :16], b17[:, 1:]], axis=-1) + jnp.asarray(cmask2)
    o_s = na_attention(qkv, k_ctx, v_ctx, bias_pairs)
    return jnp.concatenate([o_p, o_s], axis=0), qkv


def _mlstm_kernel(*refs, seq, zero_state, emit_state):
    q_ref, k_ref, v_ref, og_ref, gates_ref, wq_ref, wk_ref, gh_ref = refs[:8]
    pos = 8
    if not zero_state:
        c0_ref, n0_ref, m0_ref = refs[pos:pos + 3]
        pos += 3
    y_ref = refs[pos]
    pos += 1
    if emit_state:
        cn_ref, nn_ref, mn_ref = refs[pos:pos + 3]
        pos += 3
    qs, ks, gl, hdir, cst, nst, mst = refs[pos:]

    head = pl.program_id(1)
    nc = seq // CHUNK
    r = lax.broadcasted_iota(jnp.int32, (seq, 1), 0)
    first, last = r == 0, r == seq - 1
    qc_all = _dwconv3(q_ref[...], wq_ref, first, last)
    qs[...] = qc_all * jax.nn.sigmoid(qc_all)
    kc_all = _dwconv3(k_ref[...], wk_ref, first, last)
    ks[...] = kc_all * jax.nn.sigmoid(kc_all) * (HD_M ** -0.5)
    lane_g = lax.broadcasted_iota(jnp.int32, (1, LANE), 1)
    is_forget = (lane_g % 8) >= HEADS_M
    g_all = gates_ref[...]
    gl[...] = jnp.where(is_forget, jax.nn.log_sigmoid(g_all), g_all)

    if zero_state:
        cst[...] = jnp.zeros_like(cst)
        nst[...] = jnp.zeros_like(nst)
        mst[...] = jnp.zeros_like(mst)
    else:
        cst[...] = c0_ref[...]
        nst[...] = n0_ref[...]
        mst[...] = jnp.broadcast_to(m0_ref[...], mst.shape)

    tt = lax.broadcasted_iota(jnp.int32, (CHUNK, CHUNK), 0)
    ss = lax.broadcasted_iota(jnp.int32, (CHUNK, CHUNK), 1)
    lane = lax.broadcasted_iota(jnp.int32, (CHUNK, LANE), 1)
    sub = lax.broadcasted_iota(jnp.int32, (LANE, CHUNK), 0)

    def chunk(c, carry):
        for d in range(2):
            cc = c if d == 0 else nc - 1 - c
            r0 = pl.multiple_of(cc * CHUNK, CHUNK)
            mask = (ss <= tt) if d == 0 else (ss >= tt)
            tri = mask.astype(F32)
            g = gl[pl.ds(r0, CHUNK), :]
            gt = g.T
            bcum = jnp.dot(tri, g, precision=HIGHEST, preferred_element_type=F32)
            bcum_t = lax.dot_general(gt, tri, (((1,), (1,)), ((), ())), precision=HIGHEST,
                                     preferred_element_type=F32)
            i_idx = d * 2 * HEADS_M + head
            f_idx = i_idx + HEADS_M
            bcol = jnp.sum(jnp.where(lane == f_idx, bcum, 0.0), axis=1, keepdims=True)
            icol = jnp.sum(jnp.where(lane == i_idx, g, 0.0), axis=1, keepdims=True)
            brow = jnp.sum(jnp.where(sub == f_idx, bcum_t, 0.0), axis=0, keepdims=True)
            irow = jnp.sum(jnp.where(sub == i_idx, gt, 0.0), axis=0, keepdims=True)
            bend = bcol[CHUNK - 1:CHUNK, :] if d == 0 else bcol[0:1, :]
            m_prev = mst[d][:, 0:1]
            c_prev = cst[d]
            n_prev = nst[d]
            qc = qs[pl.ds(r0, CHUNK), :]
            kc = ks[pl.ds(r0, CHUNK), :]
            vc = v_ref[pl.ds(r0, CHUNK), :]
            qb, kb, vb = qc.astype(BF16), kc.astype(BF16), vc.astype(BF16)

            dmat = jnp.where(mask, bcol - brow + irow, -jnp.inf)
            m_inter = bcol + m_prev
            m_t = jnp.maximum(m_inter, jnp.max(dmat, axis=1, keepdims=True))
            qk = lax.dot_general(qb, kb, (((1,), (1,)), ((), ())), preferred_element_type=F32)
            s = jnp.exp(dmat - m_t) * qk
            sc = jnp.exp(m_inter - m_t)
            num = sc * _dot(qb, c_prev.astype(BF16)) + _dot(s.astype(BF16), vb)
            den = sc * jnp.sum(qc * n_prev, axis=1, keepdims=True) + jnp.sum(s, axis=1, keepdims=True)
            hdir[d, pl.ds(r0, CHUNK), :] = num / jnp.maximum(jnp.abs(den), jnp.exp(-m_t))

            to_end = bend - bcol + icol
            m_new = jnp.maximum(bend + m_prev, jnp.max(to_end, axis=0, keepdims=True))
            w = jnp.exp(to_end - m_new)
            decay = jnp.exp(bend + m_prev - m_new)
            kw = kc * w
            cst[d] = decay * c_prev + _dot(kw.T.astype(BF16), vb)
            nst[d] = decay * n_prev + jnp.sum(kw, axis=0, keepdims=True)
            mst[d] = jnp.broadcast_to(m_new, (1, LANE))
        return carry

    lax.fori_loop(0, nc, chunk, 0)

    hsum = hdir[0] + hdir[1]
    hn = hsum * lax.rsqrt(jnp.mean(hsum * hsum, axis=-1, keepdims=True) + EPS) * gh_ref[...]
    y_ref[...] = (hn * jax.nn.sigmoid(og_ref[...])).astype(y_ref.dtype)
    if emit_state:
        cn_ref[...] = cst[...]
        nn_ref[...] = nst[...]
        mn_ref[...] = mst[...]


def mlstm(proj, gates, w_conv_qk, g_head, state, *, nb, seq, row0):
    blk0 = row0 // seq
    zero_state = state is None
    col = lambda part: pl.BlockSpec((seq, HD_M), lambda b, h: (b + blk0, part * HEADS_M + h))
    in_specs = [col(0), col(1), col(2), col(3),
                pl.BlockSpec((seq, LANE), lambda b, h: (b + blk0, 0)),
                pl.BlockSpec((3, HD_M), lambda b, h: (0, h)),
                pl.BlockSpec((3, HD_M), lambda b, h: (0, HEADS_M + h)),
                pl.BlockSpec((1, HD_M), lambda b, h: (0, h))]
    args = [proj, proj, proj, proj, gates, w_conv_qk, w_conv_qk, g_head.reshape(1, MW)]
    if not zero_state:
        c0, n0, m0 = state
        in_specs += [pl.BlockSpec((None, 2, None, HD_M, HD_M), lambda b, h: (b, 0, h, 0, 0)),
                     pl.BlockSpec((None, 2, None, 1, HD_M), lambda b, h: (b, 0, h, 0, 0)),
                     pl.BlockSpec((None, 2, None, 1, 1), lambda b, h: (b, 0, h, 0, 0))]
        args += [c0, n0.reshape(nb, 2, HEADS_M, 1, HD_M), m0.reshape(nb, 2, HEADS_M, 1, 1)]
    out_shape = [jax.ShapeDtypeStruct((nb * seq, MW), BF16)]
    out_specs = [pl.BlockSpec((seq, HD_M), lambda b, h: (b, h))]
    if zero_state:
        out_shape += [jax.ShapeDtypeStruct((nb, 2, HEADS_M, HD_M, HD_M), F32),
                      jax.ShapeDtypeStruct((nb, 2, HEADS_M, 1, HD_M), F32),
                      jax.ShapeDtypeStruct((nb, 2, HEADS_M, 1, LANE), F32)]
        out_specs += [pl.BlockSpec((None, 2, None, HD_M, HD_M), lambda b, h: (b, 0, h, 0, 0)),
                      pl.BlockSpec((None, 2, None, 1, HD_M), lambda b, h: (b, 0, h, 0, 0)),
                      pl.BlockSpec((None, 2, None, 1, LANE), lambda b, h: (b, 0, h, 0, 0))]
    return pl.pallas_call(
        functools.partial(_mlstm_kernel, seq=seq, zero_state=zero_state, emit_state=zero_state),
        out_shape=out_shape,
        grid=(nb, HEADS_M),
        in_specs=in_specs,
        out_specs=out_specs,
        scratch_shapes=[pltpu.VMEM((seq, HD_M), F32), pltpu.VMEM((seq, HD_M), F32), pltpu.VMEM((seq, LANE), F32),
                        pltpu.VMEM((2, seq, HD_M), F32), pltpu.VMEM((2, HD_M, HD_M), F32),
                        pltpu.VMEM((2, 1, HD_M), F32), pltpu.VMEM((2, 1, LANE), F32)],
        compiler_params=_cparams(("arbitrary", "arbitrary")),
        name="mlstm_%d" % seq,
    )(*args)


def _hyena_pre_kernel(v_ref, x1_ref, x2_ref, wv_ref, w1_ref, w2_ref, u_ref, x2c_ref):
    first, last = _seq_edges(SEQ_BLOCK, pl.program_id(0) < N_PBLK)
    u_ref[...] = _dwconv3(x1_ref[...], w1_ref, first, last) * _dwconv3(v_ref[...], wv_ref, first, last)
    x2c_ref[...] = _dwconv3(x2_ref[...], w2_ref, first, last)


def hyena_pre(proj, w_conv_hy):
    tc = 256
    nct = HW // tc
    c0 = 4 * MW // tc
    pcol = lambda part: pl.BlockSpec((SEQ_BLOCK, tc), lambda i, j: (i, c0 + part * nct + j))
    wcol = lambda part: pl.BlockSpec((3, tc), lambda i, j: (0, part * nct + j))
    out = jax.ShapeDtypeStruct((N_TOK, HW), F32)
    ospec = pl.BlockSpec((SEQ_BLOCK, tc), lambda i, j: (i, j))
    return pl.pallas_call(
        _hyena_pre_kernel,
        out_shape=[out, out],
        grid=(N_TOK // SEQ_BLOCK, nct),
        in_specs=[pcol(0), pcol(1), pcol(2), wcol(0), wcol(1), wcol(2)],
        out_specs=[ospec, ospec],
        compiler_params=_cparams(("arbitrary", "arbitrary")),
        name="hyena_pre",
    )(proj, proj, proj, w_conv_hy, w_conv_hy, w_conv_hy)


@functools.lru_cache(maxsize=None)
def _filter_tables(seq):
    t = np.linspace(0.0, 1.0, seq)[:, None]
    wpos = 2.0 * np.pi * np.arange(seq)[:, None] / seq
    bands = np.linspace(1e-4, N_BANDS - 1, N_BANDS)[None, :]
    z = np.concatenate([t, np.cos(bands * wpos), -np.sin(bands * wpos)], axis=-1)
    z = np.pad(z, ((0, 0), (0, LANE - FILTER_EMB)))
    max_decay = math.log(DECAY_TARGET) / DECAY_FAST
    min_decay = math.log(DECAY_TARGET) / DECAY_SLOW
    deltas = np.abs(np.linspace(min_decay, max_decay, HW))
    decay = np.exp(-t * np.concatenate([deltas, deltas])[None, :])
    return z.astype(np.float32), decay.astype(np.float32)


@functools.lru_cache(maxsize=None)
def _dft_tables(seq, tk):
    n = 2 * seq
    k = np.arange(seq)[:, None]
    t = np.arange(seq)[None, :]
    ang = 2.0 * np.pi * ((k * t) % n) / n
    alt = np.where(np.arange(seq) % 2 == 0, 1.0, -1.0)
    cm, sm = np.cos(ang), np.sin(ang)
    sm[0, :] = alt
    fwd = np.stack([cm.reshape(seq // tk, tk, seq), sm.reshape(seq // tk, tk, seq)], axis=1)
    wk = np.where(np.arange(seq) == 0, 1.0, 2.0)[None, :]
    ci = (np.cos(ang.T) * wk) / n
    si = np.sin(ang.T) * 2.0 / n
    si[:, 0] = alt / n
    inv = np.concatenate([ci, si], axis=1)
    return fwd.astype(np.float32), inv.astype(np.float32)


def _filter_kernel(z_ref, w1_ref, b1_ref, w2_ref, b2_ref, w3_ref, fr_ref, dec_ref, hs_ref, hd_ref):
    fr = fr_ref[...]
    hp = functools.partial(jnp.dot, precision=HIGHEST, preferred_element_type=F32)
    h1 = jnp.sin(fr * (hp(z_ref[...], w1_ref[...]) + b1_ref[...]))
    h2 = jnp.sin(fr * (hp(h1, w2_ref[...]) + b2_ref[...]))
    filt = hp(h2, w3_ref[...]) * dec_ref[...]
    past, fut = filt[:, :HW], filt[:, HW:]
    rows = filt.shape[0]
    grow = lax.broadcasted_iota(jnp.int32, (rows, 1), 0) + pl.program_id(0) * rows
    fut = jnp.where(grow == 0, 0.0, fut)
    hs_ref[...] = past + fut
    hd_ref[...] = past - fut


def filter_gen(seq, w1, b1, w2, b2, w3, freq):
    z, decay = _filter_tables(seq)
    tl = 256
    fh = FILTER_HIDDEN
    full = lambda shape: pl.BlockSpec(shape, lambda i: (0, 0))
    out = jax.ShapeDtypeStruct((seq, HW), F32)
    return pl.pallas_call(
        _filter_kernel,
        out_shape=[out, out],
        grid=(seq // tl,),
        in_specs=[pl.BlockSpec((tl, LANE), lambda i: (i, 0)), full((LANE, fh)), full((1, fh)), full((fh, fh)),
                  full((1, fh)), full((fh, 2 * HW)), full((1, fh)), pl.BlockSpec((tl, 2 * HW), lambda i: (i, 0))],
        out_specs=[pl.BlockSpec((tl, HW), lambda i: (i, 0))] * 2,
        compiler_params=_cparams(("arbitrary",)),
        name="filter_gen",
    )(jnp.asarray(z), jnp.pad(w1, ((0, LANE - FILTER_EMB), (0, 0))), b1.reshape(1, fh), w2, b2.reshape(1, fh), w3,
      freq.reshape(1, fh), jnp.asarray(decay))


def _dft_filter_kernel(a_ref, hs_ref, hd_ref, k_ref, hs_bf, hd_bf):
    @pl.when(pl.program_id(0) == 0)
    def _():
        hs_bf[...] = hs_ref[...].astype(BF16)
        hd_bf[...] = hd_ref[...].astype(BF16)

    k_ref[0] = _dot(a_ref[0], hs_bf[...])
    k_ref[1] = _dot(a_ref[1], hd_bf[...])

    @pl.when(pl.program_id(0) == 0)
    def _():
        k_ref[1, 0:1, :] = _dot(a_ref[1, 0:8, :], hs_bf[...])[0:1, :]


def dft_filter(seq, tk, hs, hd):
    fwd, _ = _dft_tables(seq, tk)
    return pl.pallas_call(
        _dft_filter_kernel,
        out_shape=jax.ShapeDtypeStruct((2, seq, HW), F32),
        grid=(seq // tk,),
        in_specs=[pl.BlockSpec((None, 2, tk, seq), lambda m: (m, 0, 0, 0)),
                  pl.BlockSpec((seq, HW), lambda m: (0, 0)), pl.BlockSpec((seq, HW), lambda m: (0, 0))],
        out_specs=pl.BlockSpec((2, tk, HW), lambda m: (0, m, 0)),
        scratch_shapes=[pltpu.VMEM((seq, HW), BF16), pltpu.VMEM((seq, HW), BF16)],
        compiler_params=_cparams(("arbitrary",)),
        name="dft_filter",
    )(jnp.asarray(fwd).astype(BF16), hs, hd)


def _dft_fwd_kernel(a_ref, u_ref, k_ref, y_ref, u_bf):
    m = pl.program_id(1)

    @pl.when(m == 0)
    def _():
        u_bf[...] = u_ref[...].astype(BF16)

    ure = _dot(a_ref[0], u_bf[...])
    uim = _dot(a_ref[1], u_bf[...])
    kre, kim = k_ref[0], k_ref[1]
    packed = (lax.broadcasted_iota(jnp.int32, (ure.shape[0], 1), 0) == 0) & (m == 0)
    y_ref[0] = jnp.where(packed, ure * kre, ure * kre - uim * kim).astype(y_ref.dtype)
    y_ref[1] = jnp.where(packed, uim * kim, ure * kim + uim * kre).astype(y_ref.dtype)


def dft_fwd(seq, tk, u, kf, *, nb, row0):
    fwd, _ = _dft_tables(seq, tk)
    blk0 = row0 // seq
    return pl.pallas_call(
        _dft_fwd_kernel,
        out_shape=jax.ShapeDtypeStruct((nb, 2, seq, HW), BF16),
        grid=(nb, seq // tk),
        in_specs=[pl.BlockSpec((None, 2, tk, seq), lambda b, m: (m, 0, 0, 0)),
                  pl.BlockSpec((seq, HW), lambda b, m: (b + blk0, 0)),
                  pl.BlockSpec((2, tk, HW), lambda b, m: (0, m, 0))],
        out_specs=pl.BlockSpec((None, 2, tk, HW), lambda b, m: (b, 0, m, 0)),
        scratch_shapes=[pltpu.VMEM((seq, HW), BF16)],
        compiler_params=_cparams(("arbitrary", "arbitrary")),
        name="dft_fwd",
    )(jnp.asarray(fwd).astype(BF16), u, kf)


def _dft_inv_kernel(a_ref, y_ref, u_ref, x2_ref, bias_ref, o_ref, *, seq):
    conv = _dot(a_ref[:, :seq], y_ref[0]) + _dot(a_ref[:, seq:], y_ref[1])
    o_ref[...] = (x2_ref[...] * (conv + bias_ref[...] * u_ref[...])).astype(o_ref.dtype)


def dft_inv(seq, tk, y, u, x2c, bias, *, nb, row0):
    _, inv = _dft_tables(seq, tk)
    tm = min(seq, 512)
    nt = seq // tm
    blk0 = row0 // tm
    rows = lambda b, t: (b * nt + t + blk0, 0)
    return pl.pallas_call(
        functools.partial(_dft_inv_kernel, seq=seq),
        out_shape=jax.ShapeDtypeStruct((nb * seq, HW), BF16),
        grid=(nb, nt),
        in_specs=[pl.BlockSpec((tm, 2 * seq), lambda b, t: (t, 0)),
                  pl.BlockSpec((None, 2, seq, HW), lambda b, t: (b, 0, 0, 0)),
                  pl.BlockSpec((tm, HW), rows), pl.BlockSpec((tm, HW), rows),
                  pl.BlockSpec((1, HW), lambda b, t: (0, 0))],
        out_specs=pl.BlockSpec((tm, HW), lambda b, t: (b * nt + t, 0)),
        compiler_params=_cparams(("arbitrary", "arbitrary")),
        name="dft_inv",
    )(jnp.asarray(inv).astype(BF16), y, u, x2c, bias.reshape(1, HW))


def hyena_group(seq, u, x2c, filt_w, bias, *, nb, row0):
    tk = min(seq, 512)
    hs, hd = filter_gen(seq, *filt_w)
    kf = dft_filter(seq, tk, hs, hd)
    y = dft_fwd(seq, tk, u, kf, nb=nb, row0=row0)
    return dft_inv(seq, tk, y, u, x2c, bias, nb=nb, row0=row0)


def mixer_ab_parts(h, state_s, w_in, b_gates, w_conv_qk, g_head, w_conv_hy, w_f1, b_f1, w_f2, b_f2, w_f3, freq,
                   hy_bias):
    n_g = 4 * HEADS_M
    w_main = jnp.concatenate([w_in[:, :4 * MW], w_in[:, 4 * MW + n_g:]], axis=1)
    w_g = jnp.pad(w_in[:, 4 * MW:4 * MW + n_g], ((0, 0), (0, LANE - n_g)))
    b_g = jnp.pad(b_gates, (0, LANE - n_g)).reshape(1, LANE)
    proj = token_matmul([h], [w_main], tn=896, name="ab_proj")
    gates = token_matmul([h], [w_g], tn=LANE, mode="bias", bias=b_g, name="ab_gates")
    ym_p, c_p, n_p, m_p = mlstm(proj, gates, w_conv_qk, g_head, None, nb=BATCH, seq=SEQ, row0=0)
    (ym_s,) = mlstm(proj, gates, w_conv_qk, g_head, state_s, nb=DEC_BATCH, seq=DEC_SEQ, row0=N_P)
    u, x2c = hyena_pre(proj, w_conv_hy)
    filt_w = (w_f1, b_f1, w_f2, b_f2, w_f3, freq)
    yh_p = hyena_group(SEQ, u, x2c, filt_w, hy_bias, nb=BATCH, row0=0)
    yh_s = hyena_group(DEC_SEQ, u, x2c, filt_w, hy_bias, nb=DEC_BATCH, row0=N_P)
    state_p = (c_p, n_p[:, :, :, 0, :], m_p[:, :, :, 0, 0])
    return jnp.concatenate([ym_p, ym_s], axis=0), jnp.concatenate([yh_p, yh_s], axis=0), state_p


def conv_ffn(x, mod, g, w_up, w_conv, w_down):
    h = norm_mod(x, g, mod, 3, 4)
    mid = ffn_up(h, w_up, w_conv)
    return token_matmul([mid], [w_down], tn=512, mode="residual", x=x, mod=mod, gt_idx=5, name="ffn_down")


def kernel(x_prompt, x_sample, state_mlstm_C, state_mlstm_n, state_mlstm_m, cache_na_k, cache_na_v, c, c_ctx, w_ada, b_ada, g_mix, g_ffn, g_final, w_in_ab, b_gates, w_conv_qk, g_mlstm, w_conv_hy, w_filt1, b_filt1, w_filt2, b_filt2, w_filt3, filt_freq, hyena_bias, w_out_ab, w_in_c, rpb_c, w_out_c, w_up, w_conv_ffn, w_down):
    cmat = jnp.concatenate([c, c_ctx[None, :], jnp.zeros((MOD_ROWS - DEC_BATCH - 1, D), F32)], axis=0)
    mod_all = adaln_all(cmat, w_ada, b_ada).reshape(DEPTH, MOD_ROWS, 6, 1, D)
    x = jnp.concatenate([x_prompt.reshape(N_P, D), x_sample.reshape(N_S, D)], axis=0)
    new_c, new_n, new_m, new_k, new_v = [], [], [], [], []
    for l in range(DEPTH):
        mod = mod_all[l]
        h = norm_mod(x, g_mix[l], mod, 0, 1)
        e = l // 2
        if l % 2 == 0:
            state_s = (state_mlstm_C[:, e], state_mlstm_n[:, e], state_mlstm_m[:, e])
            y_m, y_h, (c_p, n_p, m_p) = mixer_ab_parts(
                h, state_s, w_in_ab[e], b_gates[e], w_conv_qk[e], g_mlstm[e], w_conv_hy[e], w_filt1[e], b_filt1[e],
                w_filt2[e], b_filt2[e], w_filt3[e], filt_freq[e], hyena_bias[e])
            x = token_matmul([y_m, y_h], [w_out_ab[e][:MW], w_out_ab[e][MW:]], tn=512, mode="residual", x=x, mod=mod,
                             gt_idx=2, name="ab_out")
            new_c.append(c_p)
            new_n.append(n_p)
            new_m.append(m_p)
        else:
            o, qkv = mixer_c(h, cache_na_k[:, e], cache_na_v[:, e], w_in_c[e], rpb_c[e])
            x = token_matmul([o], [w_out_c[e]], tn=512, mode="residual", x=x, mod=mod, gt_idx=2, name="c_out")
            kv_p = qkv[1:, :, :N_P].reshape(2, NA_HEADS, BATCH, SEQ, NA_HD).transpose(0, 2, 1, 3, 4)
            new_k.append(kv_p[0])
            new_v.append(kv_p[1])
        x = conv_ffn(x, mod, g_ffn[l], w_up[l], w_conv_ffn[l], w_down[l])
    y_prompt = final_norm(x, g_final, 0, N_P).reshape(BATCH, SEQ, D)
    y_sample = final_norm(x, g_final, N_P, N_S).reshape(DEC_BATCH, DEC_SEQ, D)
    return (y_prompt, y_sample, jnp.stack(new_c, axis=1), jnp.stack(new_n, axis=1), jnp.stack(new_m, axis=1),
            jnp.stack(new_k, axis=1), jnp.stack(new_v, axis=1))
```

```python
import functools
import math

import numpy as np
import jax
import jax.numpy as jnp
from jax import lax
from jax.experimental import pallas as pl
from jax.experimental.pallas import tpu as pltpu

F32 = jnp.float32
BF16 = jnp.bfloat16

D = 1024
BATCH, SEQ = 16, 256
DEC_BATCH, DEC_SEQ = 4, 2048
PAST_LEN = 512
DEPTH = 2
GRID_W = 64
GRID_R = DEC_SEQ // GRID_W
HEADS_M = 4
MW = D // 2
HD_M = MW // HEADS_M
CHUNK = 64
HW = D // 2
N_BANDS = 16
FILTER_EMB = 2 * N_BANDS + 1
FILTER_HIDDEN = 64
DECAY_FAST, DECAY_SLOW, DECAY_TARGET = 0.3, 1.5, 1e-2
NA_HEADS = 16
NA_HD = D // NA_HEADS
NA_KH, NA_KW = 8, 16
FF = 2816
EPS = 1e-6

N_P = BATCH * SEQ
N_S = DEC_BATCH * DEC_SEQ
N_TOK = N_P + N_S
CTX_ROW = DEC_BATCH
MOD_ROWS = 8
LANE = 128
VMEM_LIMIT = 48 * 1024 * 1024
HIGHEST = lax.Precision.HIGHEST


def _cparams(sem):
    return pltpu.CompilerParams(dimension_semantics=sem, vmem_limit_bytes=VMEM_LIMIT)


def _mod_row(i, bm):
    return jnp.where(i < N_P // bm, CTX_ROW, (i - N_P // bm) // (DEC_SEQ // bm))


def _dot(a, b):
    return jnp.dot(a, b, preferred_element_type=F32)


def _adaln_kernel(c_ref, w_ref, b_ref, o_ref):
    cv = c_ref[...]
    s = cv * jax.nn.sigmoid(cv)
    o_ref[...] = _dot(s.astype(BF16), w_ref[...].astype(BF16)) + b_ref[...]


def adaln_all(cmat, w_ada, b_ada):
    tn = 1024
    return pl.pallas_call(
        _adaln_kernel,
        out_shape=jax.ShapeDtypeStruct((DEPTH, MOD_ROWS, 6 * D), F32),
        grid=(DEPTH, 6 * D // tn),
        in_specs=[
            pl.BlockSpec((MOD_ROWS, D), lambda l, j: (0, 0)),
            pl.BlockSpec((None, D, tn), lambda l, j: (l, 0, j)),
            pl.BlockSpec((None, 1, tn), lambda l, j: (l, 0, j)),
        ],
        out_specs=pl.BlockSpec((None, MOD_ROWS, tn), lambda l, j: (l, 0, j)),
        compiler_params=_cparams(("arbitrary", "arbitrary")),
        name="adaln",
    )(cmat, w_ada, b_ada.reshape(DEPTH, 1, 6 * D))


def _norm_mod_kernel(x_ref, g_ref, sh_ref, sc_ref, o_ref):
    x = x_ref[...]
    y = x * lax.rsqrt(jnp.mean(x * x, axis=-1, keepdims=True) + EPS)
    o_ref[...] = ((y * g_ref[...]) * (1.0 + sc_ref[...]) + sh_ref[...]).astype(o_ref.dtype)


def norm_mod(x, g, mod, sh_idx, sc_idx):
    bm = 512
    return pl.pallas_call(
        _norm_mod_kernel,
        out_shape=jax.ShapeDtypeStruct((N_TOK, D), BF16),
        grid=(N_TOK // bm,),
        in_specs=[
            pl.BlockSpec((bm, D), lambda i: (i, 0)),
            pl.BlockSpec((1, D), lambda i: (0, 0)),
            pl.BlockSpec((None, None, 1, D), lambda i: (_mod_row(i, bm), sh_idx, 0, 0)),
            pl.BlockSpec((None, None, 1, D), lambda i: (_mod_row(i, bm), sc_idx, 0, 0)),
        ],
        out_specs=pl.BlockSpec((bm, D), lambda i: (i, 0)),
        compiler_params=_cparams(("arbitrary",)),
        name="norm_mod",
    )(x, g.reshape(1, D), mod, mod)


def _final_norm_kernel(x_ref, g_ref, o_ref):
    x = x_ref[...]
    y = x * lax.rsqrt(jnp.mean(x * x, axis=-1, keepdims=True) + EPS)
    o_ref[...] = y * g_ref[...]


def final_norm(x, g, row0, nrows):
    bm = 512
    return pl.pallas_call(
        _final_norm_kernel,
        out_shape=jax.ShapeDtypeStruct((nrows, D), F32),
        grid=(nrows // bm,),
        in_specs=[
            pl.BlockSpec((bm, D), lambda i: (i + row0 // bm, 0)),
            pl.BlockSpec((1, D), lambda i: (0, 0)),
        ],
        out_specs=pl.BlockSpec((bm, D), lambda i: (i, 0)),
        compiler_params=_cparams(("arbitrary",)),
        name="final_norm",
    )(x, g.reshape(1, D))


def _mm_kernel(*refs, n_a, mode, heads_per_tile):
    a_refs = refs[:n_a]
    w_refs = refs[n_a:2 * n_a]
    rest = refs[2 * n_a:]
    if mode == "residual":
        x_ref, gt_ref, o_ref = rest[0], rest[1], rest[2]
        wbf = rest[3:]
    elif mode == "bias":
        b_ref, o_ref = rest[0], rest[1]
        wbf = rest[2:]
    else:
        o_ref = rest[0]
        wbf = rest[1:]

    @pl.when(pl.program_id(1) == 0)
    def _():
        for w_ref, wb in zip(w_refs, wbf):
            wb[...] = w_ref[...].astype(BF16)

    acc = _dot(a_refs[0][...], wbf[0][...])
    for a_ref, wb in zip(a_refs[1:], wbf[1:]):
        acc = acc + _dot(a_ref[...], wb[...])
    if mode == "residual":
        o_ref[...] = x_ref[...] + gt_ref[...] * acc
    elif mode == "bias":
        o_ref[...] = acc + b_ref[...]
    elif mode == "heads":
        for hh in range(heads_per_tile):
            o_ref[hh] = acc[:, hh * NA_HD:(hh + 1) * NA_HD]
    else:
        o_ref[...] = acc.astype(o_ref.dtype)


def token_matmul(a_list, w_list, *, tn, bm=512, mode="plain", x=None, mod=None, gt_idx=None, bias=None,
                 out_dtype=F32, name="token_matmul"):
    n_a = len(a_list)
    nout = w_list[0].shape[1]
    grid = (nout // tn, N_TOK // bm)
    in_specs = [pl.BlockSpec((bm, a.shape[1]), lambda j, i: (i, 0)) for a in a_list]
    in_specs += [pl.BlockSpec((w.shape[0], tn), lambda j, i: (0, j)) for w in w_list]
    args = list(a_list) + list(w_list)
    heads_per_tile = tn // NA_HD
    if mode == "residual":
        in_specs += [pl.BlockSpec((bm, tn), lambda j, i: (i, j)),
                     pl.BlockSpec((None, None, 1, tn), lambda j, i: (_mod_row(i, bm), gt_idx, 0, j))]
        args += [x, mod]
    elif mode == "bias":
        in_specs += [pl.BlockSpec((1, tn), lambda j, i: (0, j))]
        args += [bias]
    if mode == "heads":
        tiles_per_part = D // tn
        out_shape = jax.ShapeDtypeStruct((3, NA_HEADS, N_TOK, NA_HD), F32)
        out_spec = pl.BlockSpec((None, heads_per_tile, bm, NA_HD),
                                lambda j, i: (j // tiles_per_part, j % tiles_per_part, i, 0))
    else:
        out_shape = jax.ShapeDtypeStruct((N_TOK, nout), out_dtype)
        out_spec = pl.BlockSpec((bm, tn), lambda j, i: (i, j))
    return pl.pallas_call(
        functools.partial(_mm_kernel, n_a=n_a, mode=mode, heads_per_tile=heads_per_tile),
        out_shape=out_shape,
        grid=grid,
        in_specs=in_specs,
        out_specs=out_spec,
        scratch_shapes=[pltpu.VMEM((w.shape[0], tn), BF16) for w in w_list],
        compiler_params=_cparams(("arbitrary", "arbitrary")),
        name=name,
    )(*args)


SEQ_BLOCK = DEC_SEQ
N_PBLK = N_P // SEQ_BLOCK


def _seq_edges(rows, is_prompt):
    r = lax.broadcasted_iota(jnp.int32, (rows, 1), 0)
    first = (r == 0) | (is_prompt & (r % SEQ == 0))
    last = (r == rows - 1) | (is_prompt & (r % SEQ == SEQ - 1))
    return first, last


def _dwconv3(x, w_ref, first, last):
    rows = x.shape[0]
    prev = jnp.where(first, 0.0, pltpu.roll(x, 1, 0))
    nxt = jnp.where(last, 0.0, pltpu.roll(x, rows - 1, 0))
    return prev * w_ref[0:1, :] + x * w_ref[1:2, :] + nxt * w_ref[2:3, :]


def _gelu_tanh(x):
    return 0.5 * x * (1.0 + jnp.tanh(math.sqrt(2.0 / math.pi) * (x + 0.044715 * (x * x * x))))


def _ffn_up_kernel(h_ref, wa_ref, wg_ref, wc_ref, o_ref):
    first, last = _seq_edges(SEQ_BLOCK, pl.program_id(0) < N_PBLK)
    h = h_ref[...]
    a = _dot(h, wa_ref[...].astype(BF16))
    g = _dot(h, wg_ref[...].astype(BF16))
    o_ref[...] = (_gelu_tanh(_dwconv3(a, wc_ref, first, last)) * g).astype(o_ref.dtype)


def ffn_up(h, w_up, w_conv):
    tc = 256
    nct = FF // tc
    return pl.pallas_call(
        _ffn_up_kernel,
        out_shape=jax.ShapeDtypeStruct((N_TOK, FF), BF16),
        grid=(N_TOK // SEQ_BLOCK, nct),
        in_specs=[
            pl.BlockSpec((SEQ_BLOCK, D), lambda i, j: (i, 0)),
            pl.BlockSpec((D, tc), lambda i, j: (0, j)),
            pl.BlockSpec((D, tc), lambda i, j: (0, j + nct)),
            pl.BlockSpec((3, tc), lambda i, j: (0, j)),
        ],
        out_specs=pl.BlockSpec((SEQ_BLOCK, tc), lambda i, j: (i, j)),
        compiler_params=_cparams(("arbitrary", "arbitrary")),
        name="ffn_up",
    )(h, w_up, w_up, w_conv)


def _ctx_attn_kernel(q_ref, k_ref, v_ref, o_ref, *, heads):
    outs = []
    for hh in range(heads):
        q = q_ref[hh].astype(BF16)
        k = k_ref[hh].astype(BF16)
        s = lax.dot_general(q, k, (((1,), (1,)), ((), ())), preferred_element_type=F32) * (NA_HD ** -0.5)
        m = jnp.max(s, axis=-1, keepdims=True)
        p = jnp.exp(s - m)
        l = jnp.sum(p, axis=-1, keepdims=True)
        outs.append(_dot(p.astype(BF16), v_ref[hh].astype(BF16)) / l)
    o_ref[...] = jnp.concatenate(outs, axis=-1).astype(o_ref.dtype)


def ctx_attention(qkv):
    heads = 4
    spec = lambda part: pl.BlockSpec((None, heads, SEQ, NA_HD), lambda b, h: (part, h, b, 0))
    return pl.pallas_call(
        functools.partial(_ctx_attn_kernel, heads=heads),
        out_shape=jax.ShapeDtypeStruct((N_P, D), BF16),
        grid=(BATCH, NA_HEADS // heads),
        in_specs=[spec(0), spec(1), spec(2)],
        out_specs=pl.BlockSpec((SEQ, heads * NA_HD), lambda b, h: (b, h)),
        compiler_params=_cparams(("arbitrary", "arbitrary")),
        name="ctx_attn",
    )(qkv, qkv, qkv)


def _na_tables():
    q = np.arange(GRID_W)[:, None]
    w = np.arange(GRID_W)[None, :]
    idx_c = np.clip(w - q + (NA_KW - 1), 0, 2 * NA_KW - 2)
    onehot = (idx_c.reshape(1, -1) == np.arange(32)[:, None]).astype(np.float32)
    c_start = np.clip(np.arange(GRID_W) - NA_KW // 2, 0, GRID_W - NA_KW)[:, None]
    inside = (w >= c_start) & (w < c_start + NA_KW)
    cmask = np.where(inside, 0.0, -np.inf).astype(np.float32)
    return onehot, np.tile(cmask, (1, 2))


def _rpb_expand_kernel(r_ref, e_ref, o_ref):
    o_ref[...] = jnp.dot(r_ref[...], e_ref[...], precision=HIGHEST, preferred_element_type=F32)


def rpb_expand(rpb):
    onehot, _ = _na_tables()
    rp = jnp.pad(rpb, ((0, 0), (0, 1), (0, 1)))
    return pl.pallas_call(
        _rpb_expand_kernel,
        out_shape=jax.ShapeDtypeStruct((NA_HEADS, 16, GRID_W * GRID_W), F32),
        grid=(NA_HEADS,),
        in_specs=[pl.BlockSpec((None, 16, 32), lambda h: (h, 0, 0)),
                  pl.BlockSpec((32, GRID_W * GRID_W), lambda h: (0, 0))],
        out_specs=pl.BlockSpec((None, 16, GRID_W * GRID_W), lambda h: (h, 0, 0)),
        compiler_params=_cparams(("arbitrary",)),
        name="rpb_expand",
    )(rp, jnp.asarray(onehot))


NA_QROWS = 8
NA_WIN = 2 * NA_QROWS


def _na_attn_kernel(q_ref, k_ref, v_ref, kc_ref, vc_ref, t_ref, o_ref, p_loc, p_ctx, *, heads):
    nq = NA_QROWS * GRID_W
    nk = NA_WIN * GRID_W
    pair = 2 * GRID_W
    nt = (((1,), (1,)), ((), ()))
    lane = lax.broadcasted_iota(jnp.int32, (GRID_W, pair), 1)
    zero_tile = jnp.zeros((GRID_W, pair), BF16)
    for hh in range(heads):
        kc = kc_ref[hh].astype(BF16)
        vc = vc_ref[hh].astype(BF16)
        for blk in range(GRID_R // NA_QROWS):
            k0 = min(max(NA_QROWS * blk - NA_KH // 2, 0), GRID_R - NA_WIN)
            q = (q_ref[hh, blk * nq:(blk + 1) * nq, :] * (NA_HD ** -0.5)).astype(BF16)
            kw = k_ref[hh, k0 * GRID_W:k0 * GRID_W + nk, :].astype(BF16)
            vw = v_ref[hh, k0 * GRID_W:k0 * GRID_W + nk, :].astype(BF16)
            s_loc = lax.dot_general(q, kw, nt, preferred_element_type=F32)
            s_ctx = lax.dot_general(q, kc, nt, preferred_element_type=F32)
            denoms = []
            for qi in range(NA_QROWS):
                r = NA_QROWS * blk + qi
                r_start = min(max(r - NA_KH // 2, 0), GRID_R - NA_KH)
                rows = slice(qi * GRID_W, (qi + 1) * GRID_W)
                ctx_tiles = [s_ctx[rows, c * pair:(c + 1) * pair] for c in range(PAST_LEN // pair)]
                tiles = {}
                for j in range(nk // pair):
                    kr = k0 + 2 * j
                    ok0 = r_start <= kr < r_start + NA_KH
                    ok1 = r_start <= kr + 1 < r_start + NA_KH
                    if not (ok0 or ok1):
                        continue
                    sb = s_loc[rows, j * pair:(j + 1) * pair] + t_ref[hh, kr - r + NA_KH]
                    if not (ok0 and ok1):
                        sb = jnp.where((lane < GRID_W) if ok0 else (lane >= GRID_W), sb, -jnp.inf)
                    tiles[j] = sb
                mx = functools.reduce(jnp.maximum, list(tiles.values()) + ctx_tiles)
                m = jnp.max(mx, axis=1, keepdims=True)
                acc = None
                for j in range(nk // pair):
                    if j in tiles:
                        p = jnp.exp(tiles[j] - m)
                        acc = p if acc is None else acc + p
                        p_loc[rows, j * pair:(j + 1) * pair] = p.astype(BF16)
                    else:
                        p_loc[rows, j * pair:(j + 1) * pair] = zero_tile
                for c, t in enumerate(ctx_tiles):
                    p = jnp.exp(t - m)
                    acc = acc + p
                    p_ctx[rows, c * pair:(c + 1) * pair] = p.astype(BF16)
                denoms.append(jnp.sum(acc, axis=1, keepdims=True))
            o = (_dot(p_loc[...], vw) + _dot(p_ctx[...], vc)) / jnp.concatenate(denoms, axis=0)
            o_ref[blk * nq:(blk + 1) * nq, hh * NA_HD:(hh + 1) * NA_HD] = o.astype(o_ref.dtype)


def na_attention(qkv, k_ctx, v_ctx, bias_pairs):
    heads = 2
    blk0 = N_P // DEC_SEQ
    spec = lambda part: pl.BlockSpec((None, heads, DEC_SEQ, NA_HD), lambda b, h: (part, h, b + blk0, 0))
    cspec = pl.BlockSpec((None, heads, PAST_LEN, NA_HD), lambda b, h: (b, h, 0, 0))
    return pl.pallas_call(
        functools.partial(_na_attn_kernel, heads=heads),
        out_shape=jax.ShapeDtypeStruct((N_S, D), BF16),
        grid=(DEC_BATCH, NA_HEADS // heads),
        in_specs=[spec(0), spec(1), spec(2), cspec, cspec,
                  pl.BlockSpec((heads, 16, GRID_W, 2 * GRID_W), lambda b, h: (h, 0, 0, 0))],
        out_specs=pl.BlockSpec((DEC_SEQ, heads * NA_HD), lambda b, h: (b, h)),
        scratch_shapes=[pltpu.VMEM((NA_QROWS * GRID_W, NA_WIN * GRID_W), BF16),
                        pltpu.VMEM((NA_QROWS * GRID_W, PAST_LEN), BF16)],
        compiler_params=_cparams(("arbitrary", "arbitrary")),
        name="na_attn",
    )(qkv, qkv, qkv, k_ctx, v_ctx, bias_pairs)


def mixer_c(h, k_ctx, v_ctx, w_in, rpb):
    qkv = token_matmul([h], [w_in], tn=1024, mode="heads", name="qkv_proj")
    o_p = ctx_attention(qkv)
    _, cmask2 = _na_tables()
    b15 = rpb_expand(rpb).reshape(NA_HEADS, 16, GRID_W, GRID_W)
    b17 = jnp.pad(b15, ((0, 0), (1, 0), (0, 0), (0, 0)))
    bias_pairs = jnp.concatenate([b17[:, :16], b17[:, 1:]], axis=-1) + jnp.asarray(cmask2)
    o_s = na_attention(qkv, k_ctx, v_ctx, bias_pairs)
    return jnp.concatenate([o_p, o_s], axis=0), qkv


SCAN_BLOCK = HD_M


def _mlstm_kernel(*refs, seq, zero_state, emit_state):
    q_ref, k_ref, v_ref, og_ref, gates_ref, wq_ref, wk_ref, gh_ref = refs[:8]
    pos = 8
    if not zero_state:
        c0_ref, n0_ref, m0_ref = refs[pos:pos + 3]
        pos += 3
    y_ref = refs[pos]
    pos += 1
    if emit_state:
        cn_ref, nn_ref, mn_ref = refs[pos:pos + 3]
        pos += 3
    kv_s, ks_s, be_s, mk_s, cp_s, np_s, mp_s, cst, nst, mst = refs[pos:]

    head = pl.program_id(1)
    blk = SCAN_BLOCK
    nc = seq // blk
    r = lax.broadcasted_iota(jnp.int32, (seq, 1), 0)
    first, last = r == 0, r == seq - 1
    qc_all = _dwconv3(q_ref[...], wq_ref, first, last)
    q3 = (qc_all * jax.nn.sigmoid(qc_all)).reshape(nc, blk, HD_M)
    kc_all = _dwconv3(k_ref[...], wk_ref, first, last)
    k3 = (kc_all * jax.nn.sigmoid(kc_all) * (HD_M ** -0.5)).reshape(nc, blk, HD_M)
    qb, kb = q3.astype(BF16), k3.astype(BF16)
    vb = v_ref[...].astype(BF16).reshape(nc, blk, HD_M)

    g_all = gates_ref[...]
    lf = jax.nn.log_sigmoid(g_all)
    rin = r % blk
    pre, suf = lf, lf
    for sh in [1 << i for i in range(blk.bit_length() - 1)]:
        pre = pre + jnp.where(rin >= sh, pltpu.roll(pre, sh, 0), 0.0)
        suf = suf + jnp.where(rin < blk - sh, pltpu.roll(suf, seq - sh, 0), 0.0)
    g3 = g_all.reshape(nc, blk, LANE)
    gt3 = jnp.swapaxes(g3, 1, 2)[:, :4 * HEADS_M, :]

    if zero_state:
        cst[...] = jnp.zeros_like(cst)
        nst[...] = jnp.zeros_like(nst)
        mst[...] = jnp.zeros_like(mst)
    else:
        cst[...] = c0_ref[...]
        nst[...] = n0_ref[...]
        mst[...] = jnp.broadcast_to(m0_ref[...], mst.shape)

    tt = lax.broadcasted_iota(jnp.int32, (1, blk, blk), 1)
    ss = lax.broadcasted_iota(jnp.int32, (1, blk, blk), 2)
    lane = lax.broadcasted_iota(jnp.int32, (1, 1, LANE), 2)
    sub = lax.broadcasted_iota(jnp.int32, (1, 4 * HEADS_M, 1), 1)
    hsum = None
    for d in range(2):
        i_idx = d * 2 * HEADS_M + head
        f_idx = i_idx + HEADS_M
        mask = (ss <= tt) if d == 0 else (ss >= tt)
        b3 = (pre if d == 0 else suf).reshape(nc, blk, LANE)
        bt3 = jnp.swapaxes(b3, 1, 2)[:, :4 * HEADS_M, :]
        bcol = jnp.sum(jnp.where(lane == f_idx, b3, 0.0), axis=2, keepdims=True)
        icol = jnp.sum(jnp.where(lane == i_idx, g3, 0.0), axis=2, keepdims=True)
        brow = jnp.sum(jnp.where(sub == f_idx, bt3, 0.0), axis=1, keepdims=True)
        irow = jnp.sum(jnp.where(sub == i_idx, gt3, 0.0), axis=1, keepdims=True)
        bend = bcol[:, blk - 1:blk, :] if d == 0 else bcol[:, 0:1, :]

        dmat = jnp.where(mask, bcol - brow + irow, -jnp.inf)
        mloc = jnp.max(dmat, axis=2, keepdims=True)
        qk = jnp.einsum('ctd,csd->cts', qb, kb, preferred_element_type=F32)
        s_loc = jnp.exp(dmat - mloc) * qk
        num_loc = jnp.einsum('cts,csd->ctd', s_loc.astype(BF16), vb, preferred_element_type=F32)
        den_loc = jnp.sum(s_loc, axis=2, keepdims=True)
        to_end = bend - bcol + icol
        mk = jnp.max(to_end, axis=1, keepdims=True)
        kw = k3 * jnp.exp(to_end - mk)
        kv_s[...] = jnp.einsum('cds,cse->cde', jnp.swapaxes(kw, 1, 2).astype(BF16), vb,
                               preferred_element_type=F32)
        ks_s[...] = jnp.sum(kw, axis=1, keepdims=True)
        be_s[...] = jnp.broadcast_to(bend, be_s.shape)
        mk_s[...] = jnp.broadcast_to(mk, mk_s.shape)

        def step(j, carry, d=d):
            c = j if d == 0 else nc - 1 - j
            m_prev, c_prev, n_prev = mst[d], cst[d], nst[d]
            cp_s[c] = c_prev.astype(BF16)
            np_s[c] = n_prev
            mp_s[c] = m_prev
            be, mkc = be_s[c], mk_s[c]
            m_new = jnp.maximum(be + m_prev, mkc)
            keep = jnp.exp(be + m_prev - m_new)
            add = jnp.exp(mkc - m_new)
            cst[d] = keep * c_prev + add * kv_s[c]
            nst[d] = keep * n_prev + add * ks_s[c]
            mst[d] = m_new
            return carry

        lax.fori_loop(0, nc, step, 0)

        m_inter = bcol + mp_s[...][:, :, 0:1]
        m_t = jnp.maximum(m_inter, mloc)
        w_state = jnp.exp(m_inter - m_t)
        w_loc = jnp.exp(mloc - m_t)
        inter = jnp.einsum('ctd,cde->cte', qb, cp_s[...], preferred_element_type=F32)
        num = w_state * inter + w_loc * num_loc
        den = w_state * jnp.sum(q3 * np_s[...], axis=2, keepdims=True) + w_loc * den_loc
        h = num / jnp.maximum(jnp.abs(den), jnp.exp(-m_t))
        hsum = h if hsum is None else hsum + h

    hsum = hsum.reshape(seq, HD_M)
    hn = hsum * lax.rsqrt(jnp.mean(hsum * hsum, axis=-1, keepdims=True) + EPS) * gh_ref[...]
    y_ref[...] = (hn * jax.nn.sigmoid(og_ref[...])).astype(y_ref.dtype)
    if emit_state:
        cn_ref[...] = cst[...]
        nn_ref[...] = nst[...]
        mn_ref[...] = mst[...]


def mlstm(proj, gates, w_conv_qk, g_head, state, *, nb, seq, row0):
    blk0 = row0 // seq
    nblk = seq // SCAN_BLOCK
    zero_state = state is None
    col = lambda part: pl.BlockSpec((seq, HD_M), lambda b, h: (b + blk0, part * HEADS_M + h))
    in_specs = [col(0), col(1), col(2), col(3),
                pl.BlockSpec((seq, LANE), lambda b, h: (b + blk0, 0)),
                pl.BlockSpec((3, HD_M), lambda b, h: (0, h)),
                pl.BlockSpec((3, HD_M), lambda b, h: (0, HEADS_M + h)),
                pl.BlockSpec((1, HD_M), lambda b, h: (0, h))]
    args = [proj, proj, proj, proj, gates, w_conv_qk, w_conv_qk, g_head.reshape(1, MW)]
    if not zero_state:
        c0, n0, m0 = state
        in_specs += [pl.BlockSpec((None, 2, None, HD_M, HD_M), lambda b, h: (b, 0, h, 0, 0)),
                     pl.BlockSpec((None, 2, None, 1, HD_M), lambda b, h: (b, 0, h, 0, 0)),
                     pl.BlockSpec((None, 2, None, 1, 1), lambda b, h: (b, 0, h, 0, 0))]
        args += [c0, n0.reshape(nb, 2, HEADS_M, 1, HD_M), m0.reshape(nb, 2, HEADS_M, 1, 1)]
    out_shape = [jax.ShapeDtypeStruct((nb * seq, MW), BF16)]
    out_specs = [pl.BlockSpec((seq, HD_M), lambda b, h: (b, h))]
    if zero_state:
        out_shape += [jax.ShapeDtypeStruct((nb, 2, HEADS_M, HD_M, HD_M), F32),
                      jax.ShapeDtypeStruct((nb, 2, HEADS_M, 1, HD_M), F32),
                      jax.ShapeDtypeStruct((nb, 2, HEADS_M, 1, LANE), F32)]
        out_specs += [pl.BlockSpec((None, 2, None, HD_M, HD_M), lambda b, h: (b, 0, h, 0, 0)),
                      pl.BlockSpec((None, 2, None, 1, HD_M), lambda b, h: (b, 0, h, 0, 0)),
                      pl.BlockSpec((None, 2, None, 1, LANE), lambda b, h: (b, 0, h, 0, 0))]
    return pl.pallas_call(
        functools.partial(_mlstm_kernel, seq=seq, zero_state=zero_state, emit_state=zero_state),
        out_shape=out_shape,
        grid=(nb, HEADS_M),
        in_specs=in_specs,
        out_specs=out_specs,
        scratch_shapes=[pltpu.VMEM((nblk, HD_M, HD_M), F32), pltpu.VMEM((nblk, 1, HD_M), F32),
                        pltpu.VMEM((nblk, 1, LANE), F32), pltpu.VMEM((nblk, 1, LANE), F32),
                        pltpu.VMEM((nblk, HD_M, HD_M), BF16), pltpu.VMEM((nblk, 1, HD_M), F32),
                        pltpu.VMEM((nblk, 1, LANE), F32),
                        pltpu.VMEM((2, HD_M, HD_M), F32), pltpu.VMEM((2, 1, HD_M), F32),
                        pltpu.VMEM((2, 1, LANE), F32)],
        compiler_params=_cparams(("arbitrary", "arbitrary")),
        name="mlstm_%d" % seq,
    )(*args)


def _hyena_pre_kernel(v_ref, x1_ref, x2_ref, wv_ref, w1_ref, w2_ref, u_ref, x2c_ref):
    first, last = _seq_edges(SEQ_BLOCK, pl.program_id(0) < N_PBLK)
    u_ref[...] = _dwconv3(x1_ref[...], w1_ref, first, last) * _dwconv3(v_ref[...], wv_ref, first, last)
    x2c_ref[...] = _dwconv3(x2_ref[...], w2_ref, first, last)


def hyena_pre(proj, w_conv_hy):
    tc = 256
    nct = HW // tc
    c0 = 4 * MW // tc
    pcol = lambda part: pl.BlockSpec((SEQ_BLOCK, tc), lambda i, j: (i, c0 + part * nct + j))
    wcol = lambda part: pl.BlockSpec((3, tc), lambda i, j: (0, part * nct + j))
    out = jax.ShapeDtypeStruct((N_TOK, HW), F32)
    ospec = pl.BlockSpec((SEQ_BLOCK, tc), lambda i, j: (i, j))
    return pl.pallas_call(
        _hyena_pre_kernel,
        out_shape=[out, out],
        grid=(N_TOK // SEQ_BLOCK, nct),
        in_specs=[pcol(0), pcol(1), pcol(2), wcol(0), wcol(1), wcol(2)],
        out_specs=[ospec, ospec],
        compiler_params=_cparams(("arbitrary", "arbitrary")),
        name="hyena_pre",
    )(proj, proj, proj, w_conv_hy, w_conv_hy, w_conv_hy)


@functools.lru_cache(maxsize=None)
def _filter_tables(seq):
    t = np.linspace(0.0, 1.0, seq)[:, None]
    wpos = 2.0 * np.pi * np.arange(seq)[:, None] / seq
    bands = np.linspace(1e-4, N_BANDS - 1, N_BANDS)[None, :]
    z = np.concatenate([t, np.cos(bands * wpos), -np.sin(bands * wpos)], axis=-1)
    z = np.pad(z, ((0, 0), (0, LANE - FILTER_EMB)))
    max_decay = math.log(DECAY_TARGET) / DECAY_FAST
    min_decay = math.log(DECAY_TARGET) / DECAY_SLOW
    deltas = np.abs(np.linspace(min_decay, max_decay, HW))
    decay = np.exp(-t * np.concatenate([deltas, deltas])[None, :])
    return z.astype(np.float32), decay.astype(np.float32)


@functools.lru_cache(maxsize=None)
def _dft_tables(seq, tk):
    n = 2 * seq
    k = np.arange(seq)[:, None]
    t = np.arange(seq)[None, :]
    ang = 2.0 * np.pi * ((k * t) % n) / n
    alt = np.where(np.arange(seq) % 2 == 0, 1.0, -1.0)
    cm, sm = np.cos(ang), np.sin(ang)
    sm[0, :] = alt
    fwd = np.stack([cm.reshape(seq // tk, tk, seq), sm.reshape(seq // tk, tk, seq)], axis=1)
    wk = np.where(np.arange(seq) == 0, 1.0, 2.0)[None, :]
    ci = (np.cos(ang.T) * wk) / n
    si = np.sin(ang.T) * 2.0 / n
    si[:, 0] = alt / n
    inv = np.concatenate([ci, si], axis=1)
    return fwd.astype(np.float32), inv.astype(np.float32)


def _filter_kernel(z_ref, w1_ref, b1_ref, w2_ref, b2_ref, w3_ref, fr_ref, dec_ref, hs_ref, hd_ref):
    fr = fr_ref[...]
    hp = functools.partial(jnp.dot, precision=HIGHEST, preferred_element_type=F32)
    h1 = jnp.sin(fr * (hp(z_ref[...], w1_ref[...]) + b1_ref[...]))
    h2 = jnp.sin(fr * (hp(h1, w2_ref[...]) + b2_ref[...]))
    filt = hp(h2, w3_ref[...]) * dec_ref[...]
    past, fut = filt[:, :HW], filt[:, HW:]
    rows = filt.shape[0]
    grow = lax.broadcasted_iota(jnp.int32, (rows, 1), 0) + pl.program_id(0) * rows
    fut = jnp.where(grow == 0, 0.0, fut)
    hs_ref[...] = past + fut
    hd_ref[...] = past - fut


def filter_gen(seq, w1, b1, w2, b2, w3, freq):
    z, decay = _filter_tables(seq)
    tl = 256
    fh = FILTER_HIDDEN
    full = lambda shape: pl.BlockSpec(shape, lambda i: (0, 0))
    out = jax.ShapeDtypeStruct((seq, HW), F32)
    return pl.pallas_call(
        _filter_kernel,
        out_shape=[out, out],
        grid=(seq // tl,),
        in_specs=[pl.BlockSpec((tl, LANE), lambda i: (i, 0)), full((LANE, fh)), full((1, fh)), full((fh, fh)),
                  full((1, fh)), full((fh, 2 * HW)), full((1, fh)), pl.BlockSpec((tl, 2 * HW), lambda i: (i, 0))],
        out_specs=[pl.BlockSpec((tl, HW), lambda i: (i, 0))] * 2,
        compiler_params=_cparams(("arbitrary",)),
        name="filter_gen",
    )(jnp.asarray(z), jnp.pad(w1, ((0, LANE - FILTER_EMB), (0, 0))), b1.reshape(1, fh), w2, b2.reshape(1, fh), w3,
      freq.reshape(1, fh), jnp.asarray(decay))


def _dft_filter_kernel(a_ref, hs_ref, hd_ref, k_ref, hs_bf, hd_bf):
    @pl.when(pl.program_id(0) == 0)
    def _():
        hs_bf[...] = hs_ref[...].astype(BF16)
        hd_bf[...] = hd_ref[...].astype(BF16)

    k_ref[0] = _dot(a_ref[0], hs_bf[...])
    k_ref[1] = _dot(a_ref[1], hd_bf[...])

    @pl.when(pl.program_id(0) == 0)
    def _():
        k_ref[1, 0:1, :] = _dot(a_ref[1, 0:8, :], hs_bf[...])[0:1, :]


def dft_filter(seq, tk, hs, hd):
    fwd, _ = _dft_tables(seq, tk)
    return pl.pallas_call(
        _dft_filter_kernel,
        out_shape=jax.ShapeDtypeStruct((2, seq, HW), F32),
        grid=(seq // tk,),
        in_specs=[pl.BlockSpec((None, 2, tk, seq), lambda m: (m, 0, 0, 0)),
                  pl.BlockSpec((seq, HW), lambda m: (0, 0)), pl.BlockSpec((seq, HW), lambda m: (0, 0))],
        out_specs=pl.BlockSpec((2, tk, HW), lambda m: (0, m, 0)),
        scratch_shapes=[pltpu.VMEM((seq, HW), BF16), pltpu.VMEM((seq, HW), BF16)],
        compiler_params=_cparams(("arbitrary",)),
        name="dft_filter",
    )(jnp.asarray(fwd).astype(BF16), hs, hd)


def _dft_fwd_kernel(a_ref, u_ref, k_ref, y_ref, u_bf):
    m = pl.program_id(1)

    @pl.when(m == 0)
    def _():
        u_bf[...] = u_ref[...].astype(BF16)

    ure = _dot(a_ref[0], u_bf[...])
    uim = _dot(a_ref[1], u_bf[...])
    kre, kim = k_ref[0], k_ref[1]
    packed = (lax.broadcasted_iota(jnp.int32, (ure.shape[0], 1), 0) == 0) & (m == 0)
    y_ref[0] = jnp.where(packed, ure * kre, ure * kre - uim * kim).astype(y_ref.dtype)
    y_ref[1] = jnp.where(packed, uim * kim, ure * kim + uim * kre).astype(y_ref.dtype)


def dft_fwd(seq, tk, u, kf, *, nb, row0):
    fwd, _ = _dft_tables(seq, tk)
    blk0 = row0 // seq
    return pl.pallas_call(
        _dft_fwd_kernel,
        out_shape=jax.ShapeDtypeStruct((nb, 2, seq, HW), BF16),
        grid=(nb, seq // tk),
        in_specs=[pl.BlockSpec((None, 2, tk, seq), lambda b, m: (m, 0, 0, 0)),
                  pl.BlockSpec((seq, HW), lambda b, m: (b + blk0, 0)),
                  pl.BlockSpec((2, tk, HW), lambda b, m: (0, m, 0))],
        out_specs=pl.BlockSpec((None, 2, tk, HW), lambda b, m: (b, 0, m, 0)),
        scratch_shapes=[pltpu.VMEM((seq, HW), BF16)],
        compiler_params=_cparams(("arbitrary", "arbitrary")),
        name="dft_fwd",
    )(jnp.asarray(fwd).astype(BF16), u, kf)


def _dft_inv_kernel(a_ref, y_ref, u_ref, x2_ref, bias_ref, o_ref, *, seq):
    conv = _dot(a_ref[:, :seq], y_ref[0]) + _dot(a_ref[:, seq:], y_ref[1])
    o_ref[...] = (x2_ref[...] * (conv + bias_ref[...] * u_ref[...])).astype(o_ref.dtype)


def dft_inv(seq, tk, y, u, x2c, bias, *, nb, row0):
    _, inv = _dft_tables(seq, tk)
    tm = min(seq, 512)
    nt = seq // tm
    blk0 = row0 // tm
    rows = lambda b, t: (b * nt + t + blk0, 0)
    return pl.pallas_call(
        functools.partial(_dft_inv_kernel, seq=seq),
        out_shape=jax.ShapeDtypeStruct((nb * seq, HW), BF16),
        grid=(nb, nt),
        in_specs=[pl.BlockSpec((tm, 2 * seq), lambda b, t: (t, 0)),
                  pl.BlockSpec((None, 2, seq, HW), lambda b, t: (b, 0, 0, 0)),
                  pl.BlockSpec((tm, HW), rows), pl.BlockSpec((tm, HW), rows),
                  pl.BlockSpec((1, HW), lambda b, t: (0, 0))],
        out_specs=pl.BlockSpec((tm, HW), lambda b, t: (b * nt + t, 0)),
        compiler_params=_cparams(("arbitrary", "arbitrary")),
        name="dft_inv",
    )(jnp.asarray(inv).astype(BF16), y, u, x2c, bias.reshape(1, HW))


def hyena_group(seq, u, x2c, filt_w, bias, *, nb, row0):
    tk = min(seq, 512)
    hs, hd = filter_gen(seq, *filt_w)
    kf = dft_filter(seq, tk, hs, hd)
    y = dft_fwd(seq, tk, u, kf, nb=nb, row0=row0)
    return dft_inv(seq, tk, y, u, x2c, bias, nb=nb, row0=row0)


def mixer_ab_parts(h, state_s, w_in, b_gates, w_conv_qk, g_head, w_conv_hy, w_f1, b_f1, w_f2, b_f2, w_f3, freq,
                   hy_bias):
    n_g = 4 * HEADS_M
    w_main = jnp.concatenate([w_in[:, :4 * MW], w_in[:, 4 * MW + n_g:]], axis=1)
    w_g = jnp.pad(w_in[:, 4 * MW:4 * MW + n_g], ((0, 0), (0, LANE - n_g)))
    b_g = jnp.pad(b_gates, (0, LANE - n_g)).reshape(1, LANE)
    proj = token_matmul([h], [w_main], tn=896, name="ab_proj")
    gates = token_matmul([h], [w_g], tn=LANE, mode="bias", bias=b_g, name="ab_gates")
    ym_p, c_p, n_p, m_p = mlstm(proj, gates, w_conv_qk, g_head, None, nb=BATCH, seq=SEQ, row0=0)
    (ym_s,) = mlstm(proj, gates, w_conv_qk, g_head, state_s, nb=DEC_BATCH, seq=DEC_SEQ, row0=N_P)
    u, x2c = hyena_pre(proj, w_conv_hy)
    filt_w = (w_f1, b_f1, w_f2, b_f2, w_f3, freq)
    yh_p = hyena_group(SEQ, u, x2c, filt_w, hy_bias, nb=BATCH, row0=0)
    yh_s = hyena_group(DEC_SEQ, u, x2c, filt_w, hy_bias, nb=DEC_BATCH, row0=N_P)
    state_p = (c_p, n_p[:, :, :, 0, :], m_p[:, :, :, 0, 0])
    return jnp.concatenate([ym_p, ym_s], axis=0), jnp.concatenate([yh_p, yh_s], axis=0), state_p


def conv_ffn(x, mod, g, w_up, w_conv, w_down):
    h = norm_mod(x, g, mod, 3, 4)
    mid = ffn_up(h, w_up, w_conv)
    return token_matmul([mid], [w_down], tn=512, mode="residual", x=x, mod=mod, gt_idx=5, name="ffn_down")


def kernel(x_prompt, x_sample, state_mlstm_C, state_mlstm_n, state_mlstm_m, cache_na_k, cache_na_v, c, c_ctx, w_ada, b_ada, g_mix, g_ffn, g_final, w_in_ab, b_gates, w_conv_qk, g_mlstm, w_conv_hy, w_filt1, b_filt1, w_filt2, b_filt2, w_filt3, filt_freq, hyena_bias, w_out_ab, w_in_c, rpb_c, w_out_c, w_up, w_conv_ffn, w_down):
    cmat = jnp.concatenate([c, c_ctx[None, :], jnp.zeros((MOD_ROWS - DEC_BATCH - 1, D), F32)], axis=0)
    mod_all = adaln_all(cmat, w_ada, b_ada).reshape(DEPTH, MOD_ROWS, 6, 1, D)
    x = jnp.concatenate([x_prompt.reshape(N_P, D), x_sample.reshape(N_S, D)], axis=0)
    new_c, new_n, new_m, new_k, new_v = [], [], [], [], []
    for l in range(DEPTH):
        mod = mod_all[l]
        h = norm_mod(x, g_mix[l], mod, 0, 1)
        e = l // 2
        if l % 2 == 0:
            state_s = (state_mlstm_C[:, e], state_mlstm_n[:, e], state_mlstm_m[:, e])
            y_m, y_h, (c_p, n_p, m_p) = mixer_ab_parts(
                h, state_s, w_in_ab[e], b_gates[e], w_conv_qk[e], g_mlstm[e], w_conv_hy[e], w_filt1[e], b_filt1[e],
                w_filt2[e], b_filt2[e], w_filt3[e], filt_freq[e], hyena_bias[e])
            x = token_matmul([y_m, y_h], [w_out_ab[e][:MW], w_out_ab[e][MW:]], tn=512, mode="residual", x=x, mod=mod,
                             gt_idx=2, name="ab_out")
            new_c.append(c_p)
            new_n.append(n_p)
            new_m.append(m_p)
        else:
            o, qkv = mixer_c(h, cache_na_k[:, e], cache_na_v[:, e], w_in_c[e], rpb_c[e])
            x = token_matmul([o], [w_out_c[e]], tn=512, mode="residual", x=x, mod=mod, gt_idx=2, name="c_out")
            kv_p = qkv[1:, :, :N_P].reshape(2, NA_HEADS, BATCH, SEQ, NA_HD).transpose(0, 2, 1, 3, 4)
            new_k.append(kv_p[0])
            new_v.append(kv_p[1])
        x = conv_ffn(x, mod, g_ffn[l], w_up[l], w_conv_ffn[l], w_down[l])
    y_prompt = final_norm(x, g_final, 0, N_P).reshape(BATCH, SEQ, D)
    y_sample = final_norm(x, g_final, N_P, N_S).reshape(DEC_BATCH, DEC_SEQ, D)
    return (y_prompt, y_sample, jnp.stack(new_c, axis=1), jnp.stack(new_n, axis=1), jnp.stack(new_m, axis=1),
            jnp.stack(new_k, axis=1), jnp.stack(new_v, axis=1))
```

```python
import functools
import math

import numpy as np
import jax
import jax.numpy as jnp
from jax import lax
from jax.experimental import pallas as pl
from jax.experimental.pallas import tpu as pltpu

F32 = jnp.float32
BF16 = jnp.bfloat16

D = 1024
BATCH, SEQ = 16, 256
DEC_BATCH, DEC_SEQ = 4, 2048
PAST_LEN = 512
DEPTH = 2
GRID_W = 64
GRID_R = DEC_SEQ // GRID_W
HEADS_M = 4
MW = D // 2
HD_M = MW // HEADS_M
CHUNK = 64
HW = D // 2
N_BANDS = 16
FILTER_EMB = 2 * N_BANDS + 1
FILTER_HIDDEN = 64
DECAY_FAST, DECAY_SLOW, DECAY_TARGET = 0.3, 1.5, 1e-2
NA_HEADS = 16
NA_HD = D // NA_HEADS
NA_KH, NA_KW = 8, 16
FF = 2816
EPS = 1e-6

N_P = BATCH * SEQ
N_S = DEC_BATCH * DEC_SEQ
N_TOK = N_P + N_S
CTX_ROW = DEC_BATCH
MOD_ROWS = 8
LANE = 128
VMEM_LIMIT = 48 * 1024 * 1024
HIGHEST = lax.Precision.HIGHEST


def _cparams(sem):
    return pltpu.CompilerParams(dimension_semantics=sem, vmem_limit_bytes=VMEM_LIMIT)


def _mod_row(i, bm):
    return jnp.where(i < N_P // bm, CTX_ROW, (i - N_P // bm) // (DEC_SEQ // bm))


def _dot(a, b):
    return jnp.dot(a, b, preferred_element_type=F32)


def _adaln_kernel(c_ref, w_ref, b_ref, o_ref):
    cv = c_ref[...]
    s = cv * jax.nn.sigmoid(cv)
    o_ref[...] = _dot(s.astype(BF16), w_ref[...].astype(BF16)) + b_ref[...]


def adaln_all(cmat, w_ada, b_ada):
    tn = 1024
    return pl.pallas_call(
        _adaln_kernel,
        out_shape=jax.ShapeDtypeStruct((DEPTH, MOD_ROWS, 6 * D), F32),
        grid=(DEPTH, 6 * D // tn),
        in_specs=[
            pl.BlockSpec((MOD_ROWS, D), lambda l, j: (0, 0)),
            pl.BlockSpec((None, D, tn), lambda l, j: (l, 0, j)),
            pl.BlockSpec((None, 1, tn), lambda l, j: (l, 0, j)),
        ],
        out_specs=pl.BlockSpec((None, MOD_ROWS, tn), lambda l, j: (l, 0, j)),
        compiler_params=_cparams(("arbitrary", "arbitrary")),
        name="adaln",
    )(cmat, w_ada, b_ada.reshape(DEPTH, 1, 6 * D))


ROW_BM = 512
ROW_NPB = N_P // ROW_BM


def _row_specs(arr, cols):
    if isinstance(arr, tuple):
        return ([pl.BlockSpec((ROW_BM, cols), lambda i: (jnp.minimum(i, ROW_NPB - 1), 0)),
                 pl.BlockSpec((ROW_BM, cols), lambda i: (jnp.maximum(i - ROW_NPB, 0), 0))], list(arr))
    return [pl.BlockSpec((ROW_BM, cols), lambda i: (i, 0))], [arr]


def _pick_rows(refs):
    if len(refs) == 1:
        return refs[0][...]
    return jnp.where(pl.program_id(0) < ROW_NPB, refs[0][...], refs[1][...])


def _mod_spec(layer, which):
    return pl.BlockSpec((None, None, None, 1, D), lambda i: (layer, _mod_row(i, ROW_BM), which, 0, 0))


def _layer_vec_spec(layer):
    return pl.BlockSpec((None, 1, D), lambda i: (layer, 0, 0))


def _rms_mod(x, g_ref, sh_ref, sc_ref):
    y = x * lax.rsqrt(jnp.mean(x * x, axis=-1, keepdims=True) + EPS)
    return (y * g_ref[...]) * (1.0 + sc_ref[...]) + sh_ref[...]


def _norm_mod_kernel(*refs):
    x_refs, (g_ref, sh_ref, sc_ref, o_ref) = refs[:-4], refs[-4:]
    o_ref[...] = _rms_mod(_pick_rows(x_refs), g_ref, sh_ref, sc_ref).astype(o_ref.dtype)


def norm_mod(x, g_all, mod_all, layer, sh_idx, sc_idx):
    x_specs, x_args = _row_specs(x, D)
    return pl.pallas_call(
        _norm_mod_kernel,
        out_shape=jax.ShapeDtypeStruct((N_TOK, D), BF16),
        grid=(N_TOK // ROW_BM,),
        in_specs=x_specs + [_layer_vec_spec(layer), _mod_spec(layer, sh_idx), _mod_spec(layer, sc_idx)],
        out_specs=pl.BlockSpec((ROW_BM, D), lambda i: (i, 0)),
        compiler_params=_cparams(("arbitrary",)),
        name="norm_mod",
    )(*x_args, g_all.reshape(-1, 1, D), mod_all, mod_all)


def _out_proj_kernel(*refs, a_counts, n_x, final):
    pos = 0
    a_groups = []
    for cnt in a_counts:
        a_groups.append(refs[pos:pos + cnt])
        pos += cnt
    w_refs = refs[pos:pos + len(a_counts)]
    pos += len(a_counts)
    x_refs = refs[pos:pos + n_x]
    pos += n_x
    gt_ref, g_ref = refs[pos:pos + 2]
    pos += 2
    if not final:
        sh_ref, sc_ref = refs[pos:pos + 2]
        pos += 2
    out_a, out_b = refs[pos:pos + 2]
    wbf = refs[pos + 2:]

    @pl.when(pl.program_id(0) == 0)
    def _():
        for w_ref, wb in zip(w_refs, wbf):
            wb[...] = w_ref[...].astype(BF16)

    acc = None
    for group, wb in zip(a_groups, wbf):
        part = _dot(_pick_rows(group), wb[...])
        acc = part if acc is None else acc + part
    xn = _pick_rows(x_refs) + gt_ref[...] * acc
    if final:
        y = xn * lax.rsqrt(jnp.mean(xn * xn, axis=-1, keepdims=True) + EPS) * g_ref[...]

        @pl.when(pl.program_id(0) < ROW_NPB)
        def _():
            out_a[...] = y

        @pl.when(pl.program_id(0) >= ROW_NPB)
        def _():
            out_b[...] = y
    else:
        out_a[...] = xn
        out_b[...] = _rms_mod(xn, g_ref, sh_ref, sc_ref).astype(out_b.dtype)


def out_proj(a_list, w_list, x, mod_all, layer, gt_idx, g_all, g_layer, norm_mod_idx=None, name="out_proj"):
    final = norm_mod_idx is None
    in_specs, args, a_counts = [], [], []
    for a in a_list:
        cols = (a[0] if isinstance(a, tuple) else a).shape[1]
        specs, ops = _row_specs(a, cols)
        in_specs += specs
        args += ops
        a_counts.append(len(ops))
    w_shapes = []
    for w, block, imap in w_list:
        in_specs.append(pl.BlockSpec(block, imap, pipeline_mode=pl.Buffered(1)))
        args.append(w)
        w_shapes.append(tuple(b for b in block if b is not None))
    x_specs, x_args = _row_specs(x, D)
    in_specs += x_specs + [_mod_spec(layer, gt_idx), _layer_vec_spec(g_layer)]
    args += x_args + [mod_all, g_all.reshape(-1, 1, D)]
    if final:
        out_shape = [jax.ShapeDtypeStruct((N_P, D), F32), jax.ShapeDtypeStruct((N_S, D), F32)]
        out_specs = [pl.BlockSpec((ROW_BM, D), lambda i: (jnp.minimum(i, ROW_NPB - 1), 0)),
                     pl.BlockSpec((ROW_BM, D), lambda i: (jnp.maximum(i - ROW_NPB, 0), 0))]
    else:
        n_layer, sh_idx, sc_idx = norm_mod_idx
        in_specs += [_mod_spec(n_layer, sh_idx), _mod_spec(n_layer, sc_idx)]
        args += [mod_all, mod_all]
        out_shape = [jax.ShapeDtypeStruct((N_TOK, D), F32), jax.ShapeDtypeStruct((N_TOK, D), BF16)]
        out_specs = [pl.BlockSpec((ROW_BM, D), lambda i: (i, 0))] * 2
    return pl.pallas_call(
        functools.partial(_out_proj_kernel, a_counts=tuple(a_counts), n_x=len(x_args), final=final),
        out_shape=out_shape,
        grid=(N_TOK // ROW_BM,),
        in_specs=in_specs,
        out_specs=out_specs,
        scratch_shapes=[pltpu.VMEM(s, BF16) for s in w_shapes],
        compiler_params=_cparams(("arbitrary",)),
        name=name,
    )(*args)


def _mm_kernel(*refs, n_a, mode, heads_per_tile):
    a_refs = refs[:n_a]
    w_refs = refs[n_a:2 * n_a]
    rest = refs[2 * n_a:]
    if mode == "residual":
        x_ref, gt_ref, o_ref = rest[0], rest[1], rest[2]
        wbf = rest[3:]
    elif mode == "bias":
        b_ref, o_ref = rest[0], rest[1]
        wbf = rest[2:]
    else:
        o_ref = rest[0]
        wbf = rest[1:]

    @pl.when(pl.program_id(1) == 0)
    def _():
        for w_ref, wb in zip(w_refs, wbf):
            wb[...] = w_ref[...].astype(BF16)

    acc = _dot(a_refs[0][...], wbf[0][...])
    for a_ref, wb in zip(a_refs[1:], wbf[1:]):
        acc = acc + _dot(a_ref[...], wb[...])
    if mode == "residual":
        o_ref[...] = x_ref[...] + gt_ref[...] * acc
    elif mode == "bias":
        o_ref[...] = acc + b_ref[...]
    elif mode == "heads":
        for hh in range(heads_per_tile):
            o_ref[hh] = acc[:, hh * NA_HD:(hh + 1) * NA_HD]
    else:
        o_ref[...] = acc.astype(o_ref.dtype)


def token_matmul(a_list, w_list, *, tn, bm=512, mode="plain", x=None, mod=None, gt_idx=None, bias=None,
                 out_dtype=F32, name="token_matmul"):
    n_a = len(a_list)
    nout = w_list[0].shape[1]
    grid = (nout // tn, N_TOK // bm)
    in_specs = [pl.BlockSpec((bm, a.shape[1]), lambda j, i: (i, 0)) for a in a_list]
    in_specs += [pl.BlockSpec((w.shape[0], tn), lambda j, i: (0, j)) for w in w_list]
    args = list(a_list) + list(w_list)
    heads_per_tile = tn // NA_HD
    if mode == "residual":
        in_specs += [pl.BlockSpec((bm, tn), lambda j, i: (i, j)),
                     pl.BlockSpec((None, None, 1, tn), lambda j, i: (_mod_row(i, bm), gt_idx, 0, j))]
        args += [x, mod]
    elif mode == "bias":
        in_specs += [pl.BlockSpec((1, tn), lambda j, i: (0, j))]
        args += [bias]
    if mode == "heads":
        tiles_per_part = D // tn
        out_shape = jax.ShapeDtypeStruct((3, NA_HEADS, N_TOK, NA_HD), F32)
        out_spec = pl.BlockSpec((None, heads_per_tile, bm, NA_HD),
                                lambda j, i: (j // tiles_per_part, j % tiles_per_part, i, 0))
    else:
        out_shape = jax.ShapeDtypeStruct((N_TOK, nout), out_dtype)
        out_spec = pl.BlockSpec((bm, tn), lambda j, i: (i, j))
    return pl.pallas_call(
        functools.partial(_mm_kernel, n_a=n_a, mode=mode, heads_per_tile=heads_per_tile),
        out_shape=out_shape,
        grid=grid,
        in_specs=in_specs,
        out_specs=out_spec,
        scratch_shapes=[pltpu.VMEM((w.shape[0], tn), BF16) for w in w_list],
        compiler_params=_cparams(("arbitrary", "arbitrary")),
        name=name,
    )(*args)


SEQ_BLOCK = DEC_SEQ
N_PBLK = N_P // SEQ_BLOCK


def _seq_edges(rows, is_prompt):
    r = lax.broadcasted_iota(jnp.int32, (rows, 1), 0)
    first = (r == 0) | (is_prompt & (r % SEQ == 0))
    last = (r == rows - 1) | (is_prompt & (r % SEQ == SEQ - 1))
    return first, last


def _dwconv3(x, w_ref, first, last):
    rows = x.shape[0]
    prev = jnp.where(first, 0.0, pltpu.roll(x, 1, 0))
    nxt = jnp.where(last, 0.0, pltpu.roll(x, rows - 1, 0))
    return prev * w_ref[0:1, :] + x * w_ref[1:2, :] + nxt * w_ref[2:3, :]


def _gelu_tanh(x):
    return 0.5 * x * (1.0 + jnp.tanh(math.sqrt(2.0 / math.pi) * (x + 0.044715 * (x * x * x))))


def _ffn_up_kernel(h_ref, wa_ref, wg_ref, wc_ref, o_ref):
    first, last = _seq_edges(SEQ_BLOCK, pl.program_id(0) < N_PBLK)
    h = h_ref[...]
    a = _dot(h, wa_ref[...].astype(BF16))
    g = _dot(h, wg_ref[...].astype(BF16))
    o_ref[...] = (_gelu_tanh(_dwconv3(a, wc_ref, first, last)) * g).astype(o_ref.dtype)


def ffn_up(h, w_up, w_conv, layer):
    tc = 256
    nct = FF // tc
    return pl.pallas_call(
        _ffn_up_kernel,
        out_shape=jax.ShapeDtypeStruct((N_TOK, FF), BF16),
        grid=(N_TOK // SEQ_BLOCK, nct),
        in_specs=[
            pl.BlockSpec((SEQ_BLOCK, D), lambda i, j: (i, 0)),
            pl.BlockSpec((None, D, tc), lambda i, j: (layer, 0, j)),
            pl.BlockSpec((None, D, tc), lambda i, j: (layer, 0, j + nct)),
            pl.BlockSpec((None, 3, tc), lambda i, j: (layer, 0, j)),
        ],
        out_specs=pl.BlockSpec((SEQ_BLOCK, tc), lambda i, j: (i, j)),
        compiler_params=_cparams(("arbitrary", "arbitrary")),
        name="ffn_up",
    )(h, w_up, w_up, w_conv)


def _ctx_attn_kernel(q_ref, k_ref, v_ref, o_ref, ko_ref, vo_ref, *, heads):
    outs = []
    for hh in range(heads):
        q = q_ref[hh].astype(BF16)
        k = k_ref[hh].astype(BF16)
        s = lax.dot_general(q, k, (((1,), (1,)), ((), ())), preferred_element_type=F32) * (NA_HD ** -0.5)
        m = jnp.max(s, axis=-1, keepdims=True)
        p = jnp.exp(s - m)
        l = jnp.sum(p, axis=-1, keepdims=True)
        outs.append(_dot(p.astype(BF16), v_ref[hh].astype(BF16)) / l)
    o_ref[...] = jnp.concatenate(outs, axis=-1).astype(o_ref.dtype)
    ko_ref[...] = k_ref[...]
    vo_ref[...] = v_ref[...]


def ctx_attention(qkv):
    heads = 4
    spec = lambda part: pl.BlockSpec((None, heads, SEQ, NA_HD), lambda b, h: (part, h, b, 0))
    cache = jax.ShapeDtypeStruct((BATCH, 1, NA_HEADS, SEQ, NA_HD), F32)
    cache_spec = pl.BlockSpec((None, None, heads, SEQ, NA_HD), lambda b, h: (b, 0, h, 0, 0))
    return pl.pallas_call(
        functools.partial(_ctx_attn_kernel, heads=heads),
        out_shape=[jax.ShapeDtypeStruct((N_P, D), BF16), cache, cache],
        grid=(BATCH, NA_HEADS // heads),
        in_specs=[spec(0), spec(1), spec(2)],
        out_specs=[pl.BlockSpec((SEQ, heads * NA_HD), lambda b, h: (b, h)), cache_spec, cache_spec],
        compiler_params=_cparams(("arbitrary", "arbitrary")),
        name="ctx_attn",
    )(qkv, qkv, qkv)


def _na_tables():
    q = np.arange(GRID_W)[:, None]
    w = np.arange(GRID_W)[None, :]
    idx_c = np.clip(w - q + (NA_KW - 1), 0, 2 * NA_KW - 2)
    onehot = (idx_c.reshape(1, -1) == np.arange(32)[:, None]).astype(np.float32)
    c_start = np.clip(np.arange(GRID_W) - NA_KW // 2, 0, GRID_W - NA_KW)[:, None]
    inside = (w >= c_start) & (w < c_start + NA_KW)
    cmask = np.where(inside, 0.0, -np.inf).astype(np.float32)
    return onehot, np.tile(cmask, (1, 2))


def _rpb_expand_kernel(r_ref, e_ref, o_ref):
    o_ref[...] = jnp.dot(r_ref[...], e_ref[...], precision=HIGHEST, preferred_element_type=F32)


def rpb_expand(rpb):
    onehot, _ = _na_tables()
    rp = jnp.pad(rpb, ((0, 0), (0, 1), (0, 1)))
    return pl.pallas_call(
        _rpb_expand_kernel,
        out_shape=jax.ShapeDtypeStruct((NA_HEADS, 16, GRID_W * GRID_W), F32),
        grid=(NA_HEADS,),
        in_specs=[pl.BlockSpec((None, 16, 32), lambda h: (h, 0, 0)),
                  pl.BlockSpec((32, GRID_W * GRID_W), lambda h: (0, 0))],
        out_specs=pl.BlockSpec((None, 16, GRID_W * GRID_W), lambda h: (h, 0, 0)),
        compiler_params=_cparams(("arbitrary",)),
        name="rpb_expand",
    )(rp, jnp.asarray(onehot))


NA_QROWS = 8
NA_WIN = 2 * NA_QROWS


def _na_attn_kernel(q_ref, k_ref, v_ref, kc_ref, vc_ref, t_ref, o_ref, p_loc, p_ctx, *, heads):
    nq = NA_QROWS * GRID_W
    nk = NA_WIN * GRID_W
    pair = 2 * GRID_W
    nt = (((1,), (1,)), ((), ()))
    lane = lax.broadcasted_iota(jnp.int32, (GRID_W, pair), 1)
    zero_tile = jnp.zeros((GRID_W, pair), BF16)
    for hh in range(heads):
        kc = kc_ref[hh].astype(BF16)
        vc = vc_ref[hh].astype(BF16)
        for blk in range(GRID_R // NA_QROWS):
            k0 = min(max(NA_QROWS * blk - NA_KH // 2, 0), GRID_R - NA_WIN)
            q = (q_ref[hh, blk * nq:(blk + 1) * nq, :] * (NA_HD ** -0.5)).astype(BF16)
            kw = k_ref[hh, k0 * GRID_W:k0 * GRID_W + nk, :].astype(BF16)
            vw = v_ref[hh, k0 * GRID_W:k0 * GRID_W + nk, :].astype(BF16)
            s_loc = lax.dot_general(q, kw, nt, preferred_element_type=F32)
            s_ctx = lax.dot_general(q, kc, nt, preferred_element_type=F32)
            denoms = []
            for qi in range(NA_QROWS):
                r = NA_QROWS * blk + qi
                r_start = min(max(r - NA_KH // 2, 0), GRID_R - NA_KH)
                rows = slice(qi * GRID_W, (qi + 1) * GRID_W)
                ctx_tiles = [s_ctx[rows, c * pair:(c + 1) * pair] for c in range(PAST_LEN // pair)]
                tiles = {}
                for j in range(nk // pair):
                    kr = k0 + 2 * j
                    ok0 = r_start <= kr < r_start + NA_KH
                    ok1 = r_start <= kr + 1 < r_start + NA_KH
                    if not (ok0 or ok1):
                        continue
                    sb = s_loc[rows, j * pair:(j + 1) * pair] + t_ref[hh, kr - r + NA_KH]
                    if not (ok0 and ok1):
                        sb = jnp.where((lane < GRID_W) if ok0 else (lane >= GRID_W), sb, -jnp.inf)
                    tiles[j] = sb
                mx = functools.reduce(jnp.maximum, list(tiles.values()) + ctx_tiles)
                m = jnp.max(mx, axis=1, keepdims=True)
                acc = None
                for j in range(nk // pair):
                    if j in tiles:
                        p = jnp.exp(tiles[j] - m)
                        acc = p if acc is None else acc + p
                        p_loc[rows, j * pair:(j + 1) * pair] = p.astype(BF16)
                    else:
                        p_loc[rows, j * pair:(j + 1) * pair] = zero_tile
                for c, t in enumerate(ctx_tiles):
                    p = jnp.exp(t - m)
                    acc = acc + p
                    p_ctx[rows, c * pair:(c + 1) * pair] = p.astype(BF16)
                denoms.append(jnp.sum(acc, axis=1, keepdims=True))
            o = (_dot(p_loc[...], vw) + _dot(p_ctx[...], vc)) / jnp.concatenate(denoms, axis=0)
            o_ref[blk * nq:(blk + 1) * nq, hh * NA_HD:(hh + 1) * NA_HD] = o.astype(o_ref.dtype)


def na_attention(qkv, k_ctx, v_ctx, bias_pairs):
    heads = 2
    blk0 = N_P // DEC_SEQ
    spec = lambda part: pl.BlockSpec((None, heads, DEC_SEQ, NA_HD), lambda b, h: (part, h, b + blk0, 0))
    cspec = pl.BlockSpec((None, heads, PAST_LEN, NA_HD), lambda b, h: (b, h, 0, 0))
    return pl.pallas_call(
        functools.partial(_na_attn_kernel, heads=heads),
        out_shape=jax.ShapeDtypeStruct((N_S, D), BF16),
        grid=(DEC_BATCH, NA_HEADS // heads),
        in_specs=[spec(0), spec(1), spec(2), cspec, cspec,
                  pl.BlockSpec((heads, 16, GRID_W, 2 * GRID_W), lambda b, h: (h, 0, 0, 0))],
        out_specs=pl.BlockSpec((DEC_SEQ, heads * NA_HD), lambda b, h: (b, h)),
        scratch_shapes=[pltpu.VMEM((NA_QROWS * GRID_W, NA_WIN * GRID_W), BF16),
                        pltpu.VMEM((NA_QROWS * GRID_W, PAST_LEN), BF16)],
        compiler_params=_cparams(("arbitrary", "arbitrary")),
        name="na_attn",
    )(qkv, qkv, qkv, k_ctx, v_ctx, bias_pairs)


def mixer_c(h, k_ctx, v_ctx, w_in, rpb):
    qkv = token_matmul([h], [w_in], tn=1024, mode="heads", name="qkv_proj")
    o_p, k_new, v_new = ctx_attention(qkv)
    _, cmask2 = _na_tables()
    b15 = rpb_expand(rpb).reshape(NA_HEADS, 16, GRID_W, GRID_W)
    b17 = jnp.pad(b15, ((0, 0), (1, 0), (0, 0), (0, 0)))
    bias_pairs = jnp.concatenate([b17[:, :16], b17[:, 1:]], axis=-1) + jnp.asarray(cmask2)
    o_s = na_attention(qkv, k_ctx, v_ctx, bias_pairs)
    return (o_p, o_s), k_new, v_new


SCAN_BLOCK = HD_M


def _mlstm_kernel(*refs, seq, zero_state, emit_state):
    q_ref, k_ref, v_ref, og_ref, gates_ref, wq_ref, wk_ref, gh_ref = refs[:8]
    pos = 8
    if not zero_state:
        c0_ref, n0_ref, m0_ref = refs[pos:pos + 3]
        pos += 3
    y_ref = refs[pos]
    pos += 1
    if emit_state:
        cn_ref, nn_ref, mn_ref = refs[pos:pos + 3]
        pos += 3
    kv_s, ks_s, be_s, mk_s, cp_s, np_s, mp_s, cst, nst, mst = refs[pos:]

    head = pl.program_id(1)
    blk = SCAN_BLOCK
    nc = seq // blk
    r = lax.broadcasted_iota(jnp.int32, (seq, 1), 0)
    first, last = r == 0, r == seq - 1
    qc_all = _dwconv3(q_ref[...], wq_ref, first, last)
    q3 = (qc_all * jax.nn.sigmoid(qc_all)).reshape(nc, blk, HD_M)
    kc_all = _dwconv3(k_ref[...], wk_ref, first, last)
    k3 = (kc_all * jax.nn.sigmoid(kc_all) * (HD_M ** -0.5)).reshape(nc, blk, HD_M)
    qb, kb = q3.astype(BF16), k3.astype(BF16)
    vb = v_ref[...].astype(BF16).reshape(nc, blk, HD_M)

    g_all = gates_ref[...]
    lf = jax.nn.log_sigmoid(g_all)
    rin = r % blk
    pre, suf = lf, lf
    for sh in [1 << i for i in range(blk.bit_length() - 1)]:
        pre = pre + jnp.where(rin >= sh, pltpu.roll(pre, sh, 0), 0.0)
        suf = suf + jnp.where(rin < blk - sh, pltpu.roll(suf, seq - sh, 0), 0.0)
    g3 = g_all.reshape(nc, blk, LANE)
    gt3 = jnp.swapaxes(g3, 1, 2)[:, :4 * HEADS_M, :]

    if zero_state:
        cst[...] = jnp.zeros_like(cst)
        nst[...] = jnp.zeros_like(nst)
        mst[...] = jnp.zeros_like(mst)
    else:
        cst[...] = c0_ref[...]
        nst[...] = n0_ref[...]
        mst[...] = jnp.broadcast_to(m0_ref[...], mst.shape)

    tt = lax.broadcasted_iota(jnp.int32, (1, blk, blk), 1)
    ss = lax.broadcasted_iota(jnp.int32, (1, blk, blk), 2)
    lane = lax.broadcasted_iota(jnp.int32, (1, 1, LANE), 2)
    sub = lax.broadcasted_iota(jnp.int32, (1, 4 * HEADS_M, 1), 1)
    hsum = None
    for d in range(2):
        i_idx = d * 2 * HEADS_M + head
        f_idx = i_idx + HEADS_M
        mask = (ss <= tt) if d == 0 else (ss >= tt)
        b3 = (pre if d == 0 else suf).reshape(nc, blk, LANE)
        bt3 = jnp.swapaxes(b3, 1, 2)[:, :4 * HEADS_M, :]
        bcol = jnp.sum(jnp.where(lane == f_idx, b3, 0.0), axis=2, keepdims=True)
        icol = jnp.sum(jnp.where(lane == i_idx, g3, 0.0), axis=2, keepdims=True)
        brow = jnp.sum(jnp.where(sub == f_idx, bt3, 0.0), axis=1, keepdims=True)
        irow = jnp.sum(jnp.where(sub == i_idx, gt3, 0.0), axis=1, keepdims=True)
        bend = bcol[:, blk - 1:blk, :] if d == 0 else bcol[:, 0:1, :]

        dmat = jnp.where(mask, bcol - brow + irow, -jnp.inf)
        mloc = jnp.max(dmat, axis=2, keepdims=True)
        qk = jnp.einsum('ctd,csd->cts', qb, kb, preferred_element_type=F32)
        s_loc = jnp.exp(dmat - mloc) * qk
        num_loc = jnp.einsum('cts,csd->ctd', s_loc.astype(BF16), vb, preferred_element_type=F32)
        den_loc = jnp.sum(s_loc, axis=2, keepdims=True)
        to_end = bend - bcol + icol
        mk = jnp.max(to_end, axis=1, keepdims=True)
        kw = k3 * jnp.exp(to_end - mk)
        kv_s[...] = jnp.einsum('cds,cse->cde', jnp.swapaxes(kw, 1, 2).astype(BF16), vb,
                               preferred_element_type=F32)
        ks_s[...] = jnp.sum(kw, axis=1, keepdims=True)
        be_s[...] = jnp.broadcast_to(bend, be_s.shape)
        mk_s[...] = jnp.broadcast_to(mk, mk_s.shape)

        def step(j, carry, d=d):
            c = j if d == 0 else nc - 1 - j
            m_prev, c_prev, n_prev = mst[d], cst[d], nst[d]
            cp_s[c] = c_prev.astype(BF16)
            np_s[c] = n_prev
            mp_s[c] = m_prev
            be, mkc = be_s[c], mk_s[c]
            m_new = jnp.maximum(be + m_prev, mkc)
            keep = jnp.exp(be + m_prev - m_new)
            add = jnp.exp(mkc - m_new)
            cst[d] = keep * c_prev + add * kv_s[c]
            nst[d] = keep * n_prev + add * ks_s[c]
            mst[d] = m_new
            return carry

        lax.fori_loop(0, nc, step, 0)

        m_inter = bcol + mp_s[...][:, :, 0:1]
        m_t = jnp.maximum(m_inter, mloc)
        w_state = jnp.exp(m_inter - m_t)
        w_loc = jnp.exp(mloc - m_t)
        inter = jnp.einsum('ctd,cde->cte', qb, cp_s[...], preferred_element_type=F32)
        num = w_state * inter + w_loc * num_loc
        den = w_state * jnp.sum(q3 * np_s[...], axis=2, keepdims=True) + w_loc * den_loc
        h = num / jnp.maximum(jnp.abs(den), jnp.exp(-m_t))
        hsum = h if hsum is None else hsum + h

    hsum = hsum.reshape(seq, HD_M)
    hn = hsum * lax.rsqrt(jnp.mean(hsum * hsum, axis=-1, keepdims=True) + EPS) * gh_ref[...]
    y_ref[...] = (hn * jax.nn.sigmoid(og_ref[...])).astype(y_ref.dtype)
    if emit_state:
        cn_ref[...] = cst[...]
        nn_ref[...] = nst[...]
        mn_ref[...] = mst[...]


def mlstm(proj, gates, w_conv_qk, g_head, state, *, nb, seq, row0):
    blk0 = row0 // seq
    nblk = seq // SCAN_BLOCK
    zero_state = state is None
    col = lambda part: pl.BlockSpec((seq, HD_M), lambda b, h: (b + blk0, part * HEADS_M + h))
    in_specs = [col(0), col(1), col(2), col(3),
                pl.BlockSpec((seq, LANE), lambda b, h: (b + blk0, 0)),
                pl.BlockSpec((3, HD_M), lambda b, h: (0, h)),
                pl.BlockSpec((3, HD_M), lambda b, h: (0, HEADS_M + h)),
                pl.BlockSpec((1, HD_M), lambda b, h: (0, h))]
    args = [proj, proj, proj, proj, gates, w_conv_qk, w_conv_qk, g_head.reshape(1, MW)]
    if not zero_state:
        c0, n0, m0 = state
        in_specs += [pl.BlockSpec((None, 2, None, HD_M, HD_M), lambda b, h: (b, 0, h, 0, 0)),
                     pl.BlockSpec((None, 2, None, 1, HD_M), lambda b, h: (b, 0, h, 0, 0)),
                     pl.BlockSpec((None, 2, None, 1, 1), lambda b, h: (b, 0, h, 0, 0))]
        args += [c0, n0.reshape(nb, 2, HEADS_M, 1, HD_M), m0.reshape(nb, 2, HEADS_M, 1, 1)]
    out_shape = [jax.ShapeDtypeStruct((nb * seq, MW), BF16)]
    out_specs = [pl.BlockSpec((seq, HD_M), lambda b, h: (b, h))]
    if zero_state:
        out_shape += [jax.ShapeDtypeStruct((nb, 2, HEADS_M, HD_M, HD_M), F32),
                      jax.ShapeDtypeStruct((nb, 2, HEADS_M, 1, HD_M), F32),
                      jax.ShapeDtypeStruct((nb, 2, HEADS_M, 1, LANE), F32)]
        out_specs += [pl.BlockSpec((None, 2, None, HD_M, HD_M), lambda b, h: (b, 0, h, 0, 0)),
                      pl.BlockSpec((None, 2, None, 1, HD_M), lambda b, h: (b, 0, h, 0, 0)),
                      pl.BlockSpec((None, 2, None, 1, LANE), lambda b, h: (b, 0, h, 0, 0))]
    return pl.pallas_call(
        functools.partial(_mlstm_kernel, seq=seq, zero_state=zero_state, emit_state=zero_state),
        out_shape=out_shape,
        grid=(nb, HEADS_M),
        in_specs=in_specs,
        out_specs=out_specs,
        scratch_shapes=[pltpu.VMEM((nblk, HD_M, HD_M), F32), pltpu.VMEM((nblk, 1, HD_M), F32),
                        pltpu.VMEM((nblk, 1, LANE), F32), pltpu.VMEM((nblk, 1, LANE), F32),
                        pltpu.VMEM((nblk, HD_M, HD_M), BF16), pltpu.VMEM((nblk, 1, HD_M), F32),
                        pltpu.VMEM((nblk, 1, LANE), F32),
                        pltpu.VMEM((2, HD_M, HD_M), F32), pltpu.VMEM((2, 1, HD_M), F32),
                        pltpu.VMEM((2, 1, LANE), F32)],
        compiler_params=_cparams(("arbitrary", "arbitrary")),
        name="mlstm_%d" % seq,
    )(*args)


def _hyena_pre_kernel(v_ref, x1_ref, x2_ref, wv_ref, w1_ref, w2_ref, u_ref, x2c_ref):
    first, last = _seq_edges(SEQ_BLOCK, pl.program_id(0) < N_PBLK)
    u_ref[...] = _dwconv3(x1_ref[...], w1_ref, first, last) * _dwconv3(v_ref[...], wv_ref, first, last)
    x2c_ref[...] = _dwconv3(x2_ref[...], w2_ref, first, last)


def hyena_pre(proj, w_conv_hy):
    tc = 256
    nct = HW // tc
    c0 = 4 * MW // tc
    pcol = lambda part: pl.BlockSpec((SEQ_BLOCK, tc), lambda i, j: (i, c0 + part * nct + j))
    wcol = lambda part: pl.BlockSpec((3, tc), lambda i, j: (0, part * nct + j))
    out = jax.ShapeDtypeStruct((N_TOK, HW), F32)
    ospec = pl.BlockSpec((SEQ_BLOCK, tc), lambda i, j: (i, j))
    return pl.pallas_call(
        _hyena_pre_kernel,
        out_shape=[out, out],
        grid=(N_TOK // SEQ_BLOCK, nct),
        in_specs=[pcol(0), pcol(1), pcol(2), wcol(0), wcol(1), wcol(2)],
        out_specs=[ospec, ospec],
        compiler_params=_cparams(("arbitrary", "arbitrary")),
        name="hyena_pre",
    )(proj, proj, proj, w_conv_hy, w_conv_hy, w_conv_hy)


@functools.lru_cache(maxsize=None)
def _filter_tables(seq):
    t = np.linspace(0.0, 1.0, seq)[:, None]
    wpos = 2.0 * np.pi * np.arange(seq)[:, None] / seq
    bands = np.linspace(1e-4, N_BANDS - 1, N_BANDS)[None, :]
    z = np.concatenate([t, np.cos(bands * wpos), -np.sin(bands * wpos)], axis=-1)
    z = np.pad(z, ((0, 0), (0, LANE - FILTER_EMB)))
    max_decay = math.log(DECAY_TARGET) / DECAY_FAST
    min_decay = math.log(DECAY_TARGET) / DECAY_SLOW
    deltas = np.abs(np.linspace(min_decay, max_decay, HW))
    decay = np.exp(-t * np.concatenate([deltas, deltas])[None, :])
    return z.astype(np.float32), decay.astype(np.float32)


@functools.lru_cache(maxsize=None)
def _dft_tables(seq, tk):
    n = 2 * seq
    k = np.arange(seq)[:, None]
    t = np.arange(seq)[None, :]
    ang = 2.0 * np.pi * ((k * t) % n) / n
    alt = np.where(np.arange(seq) % 2 == 0, 1.0, -1.0)
    cm, sm = np.cos(ang), np.sin(ang)
    sm[0, :] = alt
    fwd = np.stack([cm.reshape(seq // tk, tk, seq), sm.reshape(seq // tk, tk, seq)], axis=1)
    wk = np.where(np.arange(seq) == 0, 1.0, 2.0)[None, :]
    ci = (np.cos(ang.T) * wk) / n
    si = np.sin(ang.T) * 2.0 / n
    si[:, 0] = alt / n
    inv = np.concatenate([ci, si], axis=1)
    return fwd.astype(np.float32), inv.astype(np.float32)


def _filter_kernel(z_ref, w1_ref, b1_ref, w2_ref, b2_ref, w3_ref, fr_ref, dec_ref, hs_ref, hd_ref):
    fr = fr_ref[...]
    hp = functools.partial(jnp.dot, precision=HIGHEST, preferred_element_type=F32)
    h1 = jnp.sin(fr * (hp(z_ref[...], w1_ref[...]) + b1_ref[...]))
    h2 = jnp.sin(fr * (hp(h1, w2_ref[...]) + b2_ref[...]))
    filt = hp(h2, w3_ref[...]) * dec_ref[...]
    past, fut = filt[:, :HW], filt[:, HW:]
    rows = filt.shape[0]
    grow = lax.broadcasted_iota(jnp.int32, (rows, 1), 0) + pl.program_id(0) * rows
    fut = jnp.where(grow == 0, 0.0, fut)
    hs_ref[...] = past + fut
    hd_ref[...] = past - fut


def filter_gen(seq, w1, b1, w2, b2, w3, freq):
    z, decay = _filter_tables(seq)
    tl = 256
    fh = FILTER_HIDDEN
    full = lambda shape: pl.BlockSpec(shape, lambda i: (0, 0))
    out = jax.ShapeDtypeStruct((seq, HW), F32)
    return pl.pallas_call(
        _filter_kernel,
        out_shape=[out, out],
        grid=(seq // tl,),
        in_specs=[pl.BlockSpec((tl, LANE), lambda i: (i, 0)), full((LANE, fh)), full((1, fh)), full((fh, fh)),
                  full((1, fh)), full((fh, 2 * HW)), full((1, fh)), pl.BlockSpec((tl, 2 * HW), lambda i: (i, 0))],
        out_specs=[pl.BlockSpec((tl, HW), lambda i: (i, 0))] * 2,
        compiler_params=_cparams(("arbitrary",)),
        name="filter_gen",
    )(jnp.asarray(z), jnp.pad(w1, ((0, LANE - FILTER_EMB), (0, 0))), b1.reshape(1, fh), w2, b2.reshape(1, fh), w3,
      freq.reshape(1, fh), jnp.asarray(decay))


def _dft_filter_kernel(a_ref, hs_ref, hd_ref, k_ref, hs_bf, hd_bf):
    @pl.when(pl.program_id(0) == 0)
    def _():
        hs_bf[...] = hs_ref[...].astype(BF16)
        hd_bf[...] = hd_ref[...].astype(BF16)

    k_ref[0] = _dot(a_ref[0], hs_bf[...])
    k_ref[1] = _dot(a_ref[1], hd_bf[...])

    @pl.when(pl.program_id(0) == 0)
    def _():
        k_ref[1, 0:1, :] = _dot(a_ref[1, 0:8, :], hs_bf[...])[0:1, :]


def dft_filter(seq, tk, hs, hd):
    fwd, _ = _dft_tables(seq, tk)
    return pl.pallas_call(
        _dft_filter_kernel,
        out_shape=jax.ShapeDtypeStruct((2, seq, HW), F32),
        grid=(seq // tk,),
        in_specs=[pl.BlockSpec((None, 2, tk, seq), lambda m: (m, 0, 0, 0)),
                  pl.BlockSpec((seq, HW), lambda m: (0, 0)), pl.BlockSpec((seq, HW), lambda m: (0, 0))],
        out_specs=pl.BlockSpec((2, tk, HW), lambda m: (0, m, 0)),
        scratch_shapes=[pltpu.VMEM((seq, HW), BF16), pltpu.VMEM((seq, HW), BF16)],
        compiler_params=_cparams(("arbitrary",)),
        name="dft_filter",
    )(jnp.asarray(fwd).astype(BF16), hs, hd)


def _dft_fwd_kernel(a_ref, u_ref, k_ref, y_ref, u_bf):
    m = pl.program_id(1)

    @pl.when(m == 0)
    def _():
        u_bf[...] = u_ref[...].astype(BF16)

    ure = _dot(a_ref[0], u_bf[...])
    uim = _dot(a_ref[1], u_bf[...])
    kre, kim = k_ref[0], k_ref[1]
    packed = (lax.broadcasted_iota(jnp.int32, (ure.shape[0], 1), 0) == 0) & (m == 0)
    y_ref[0] = jnp.where(packed, ure * kre, ure * kre - uim * kim).astype(y_ref.dtype)
    y_ref[1] = jnp.where(packed, uim * kim, ure * kim + uim * kre).astype(y_ref.dtype)


def dft_fwd(seq, tk, u, kf, *, nb, row0):
    fwd, _ = _dft_tables(seq, tk)
    blk0 = row0 // seq
    return pl.pallas_call(
        _dft_fwd_kernel,
        out_shape=jax.ShapeDtypeStruct((nb, 2, seq, HW), BF16),
        grid=(nb, seq // tk),
        in_specs=[pl.BlockSpec((None, 2, tk, seq), lambda b, m: (m, 0, 0, 0)),
                  pl.BlockSpec((seq, HW), lambda b, m: (b + blk0, 0)),
                  pl.BlockSpec((2, tk, HW), lambda b, m: (0, m, 0))],
        out_specs=pl.BlockSpec((None, 2, tk, HW), lambda b, m: (b, 0, m, 0)),
        scratch_shapes=[pltpu.VMEM((seq, HW), BF16)],
        compiler_params=_cparams(("arbitrary", "arbitrary")),
        name="dft_fwd",
    )(jnp.asarray(fwd).astype(BF16), u, kf)


def _dft_inv_kernel(a_ref, y_ref, u_ref, x2_ref, bias_ref, o_ref, *, seq):
    conv = _dot(a_ref[:, :seq], y_ref[0]) + _dot(a_ref[:, seq:], y_ref[1])
    o_ref[...] = (x2_ref[...] * (conv + bias_ref[...] * u_ref[...])).astype(o_ref.dtype)


def dft_inv(seq, tk, y, u, x2c, bias, *, nb, row0):
    _, inv = _dft_tables(seq, tk)
    tm = min(seq, 512)
    nt = seq // tm
    blk0 = row0 // tm
    rows = lambda b, t: (b * nt + t + blk0, 0)
    return pl.pallas_call(
        functools.partial(_dft_inv_kernel, seq=seq),
        out_shape=jax.ShapeDtypeStruct((nb * seq, HW), BF16),
        grid=(nb, nt),
        in_specs=[pl.BlockSpec((tm, 2 * seq), lambda b, t: (t, 0)),
                  pl.BlockSpec((None, 2, seq, HW), lambda b, t: (b, 0, 0, 0)),
                  pl.BlockSpec((tm, HW), rows), pl.BlockSpec((tm, HW), rows),
                  pl.BlockSpec((1, HW), lambda b, t: (0, 0))],
        out_specs=pl.BlockSpec((tm, HW), lambda b, t: (b * nt + t, 0)),
        compiler_params=_cparams(("arbitrary", "arbitrary")),
        name="dft_inv",
    )(jnp.asarray(inv).astype(BF16), y, u, x2c, bias.reshape(1, HW))


def hyena_group(seq, u, x2c, filt_w, bias, *, nb, row0):
    tk = min(seq, 512)
    hs, hd = filter_gen(seq, *filt_w)
    kf = dft_filter(seq, tk, hs, hd)
    y = dft_fwd(seq, tk, u, kf, nb=nb, row0=row0)
    return dft_inv(seq, tk, y, u, x2c, bias, nb=nb, row0=row0)


def mixer_ab_parts(h, state_s, w_in, b_gates, w_conv_qk, g_head, w_conv_hy, w_f1, b_f1, w_f2, b_f2, w_f3, freq,
                   hy_bias):
    n_g = 4 * HEADS_M
    w_main = jnp.concatenate([w_in[:, :4 * MW], w_in[:, 4 * MW + n_g:]], axis=1)
    w_g = jnp.pad(w_in[:, 4 * MW:4 * MW + n_g], ((0, 0), (0, LANE - n_g)))
    b_g = jnp.pad(b_gates, (0, LANE - n_g)).reshape(1, LANE)
    proj = token_matmul([h], [w_main], tn=896, name="ab_proj")
    gates = token_matmul([h], [w_g], tn=LANE, mode="bias", bias=b_g, name="ab_gates")
    ym_p, c_p, n_p, m_p = mlstm(proj, gates, w_conv_qk, g_head, None, nb=BATCH, seq=SEQ, row0=0)
    (ym_s,) = mlstm(proj, gates, w_conv_qk, g_head, state_s, nb=DEC_BATCH, seq=DEC_SEQ, row0=N_P)
    u, x2c = hyena_pre(proj, w_conv_hy)
    filt_w = (w_f1, b_f1, w_f2, b_f2, w_f3, freq)
    yh_p = hyena_group(SEQ, u, x2c, filt_w, hy_bias, nb=BATCH, row0=0)
    yh_s = hyena_group(DEC_SEQ, u, x2c, filt_w, hy_bias, nb=DEC_BATCH, row0=N_P)
    state_p = (c_p, n_p[:, :, :, 0, :], m_p[:, :, :, 0, 0])
    return (ym_p, ym_s), (yh_p, yh_s), state_p


def kernel(x_prompt, x_sample, state_mlstm_C, state_mlstm_n, state_mlstm_m, cache_na_k, cache_na_v, c, c_ctx, w_ada, b_ada, g_mix, g_ffn, g_final, w_in_ab, b_gates, w_conv_qk, g_mlstm, w_conv_hy, w_filt1, b_filt1, w_filt2, b_filt2, w_filt3, filt_freq, hyena_bias, w_out_ab, w_in_c, rpb_c, w_out_c, w_up, w_conv_ffn, w_down):
    cmat = jnp.concatenate([c, c_ctx[None, :], jnp.zeros((MOD_ROWS - DEC_BATCH - 1, D), F32)], axis=0)
    mod_all = adaln_all(cmat, w_ada, b_ada).reshape(DEPTH, MOD_ROWS, 6, 1, D)
    x = (x_prompt.reshape(N_P, D), x_sample.reshape(N_S, D))
    h = norm_mod(x, g_mix, mod_all, 0, 0, 1)
    new_c, new_n, new_m, new_k, new_v = [], [], [], [], []
    for l in range(DEPTH):
        e = l // 2
        if l % 2 == 0:
            state_s = (state_mlstm_C[:, e], state_mlstm_n[:, e], state_mlstm_m[:, e])
            y_m, y_h, (c_p, n_p, m_p) = mixer_ab_parts(
                h, state_s, w_in_ab[e], b_gates[e], w_conv_qk[e], g_mlstm[e], w_conv_hy[e], w_filt1[e], b_filt1[e],
                w_filt2[e], b_filt2[e], w_filt3[e], filt_freq[e], hyena_bias[e])
            a_list = [y_m, y_h]
            w_list = [(w_out_ab, (None, MW, D), lambda i, e=e: (e, 0, 0)),
                      (w_out_ab, (None, HW, D), lambda i, e=e: (e, 1, 0))]
            new_c.append(c_p[:, None])
            new_n.append(n_p[:, None])
            new_m.append(m_p[:, None])
        else:
            o, k_new, v_new = mixer_c(h, cache_na_k[:, e], cache_na_v[:, e], w_in_c[e], rpb_c[e])
            a_list = [o]
            w_list = [(w_out_c, (None, D, D), lambda i, e=e: (e, 0, 0))]
            new_k.append(k_new)
            new_v.append(v_new)
        x, h = out_proj(a_list, w_list, x, mod_all, l, 2, g_ffn, l, (l, 3, 4), name="mixer_out")
        mid = ffn_up(h, w_up, w_conv_ffn, l)
        w_list = [(w_down, (None, FF, D), lambda i, l=l: (l, 0, 0))]
        if l + 1 < DEPTH:
            x, h = out_proj([mid], w_list, x, mod_all, l, 5, g_mix, l + 1, (l + 1, 0, 1), name="ffn_down")
        else:
            y_p, y_s = out_proj([mid], w_list, x, mod_all, l, 5, g_final, 0, None, name="ffn_down_final")
    cat = lambda parts: parts[0] if len(parts) == 1 else jnp.concatenate(parts, axis=1)
    return (y_p.reshape(BATCH, SEQ, D), y_s.reshape(DEC_BATCH, DEC_SEQ, D), cat(new_c), cat(new_n), cat(new_m),
            cat(new_k), cat(new_v))
```

```python
import functools
import math

import numpy as np
import jax
import jax.numpy as jnp
from jax import lax
from jax.experimental import pallas as pl
from jax.experimental.pallas import tpu as pltpu

F32 = jnp.float32
BF16 = jnp.bfloat16

D = 1024
BATCH, SEQ = 16, 256
DEC_BATCH, DEC_SEQ = 4, 2048
PAST_LEN = 512
DEPTH = 2
GRID_W = 64
GRID_R = DEC_SEQ // GRID_W
HEADS_M = 4
MW = D // 2
HD_M = MW // HEADS_M
CHUNK = 64
HW = D // 2
N_BANDS = 16
FILTER_EMB = 2 * N_BANDS + 1
FILTER_HIDDEN = 64
DECAY_FAST, DECAY_SLOW, DECAY_TARGET = 0.3, 1.5, 1e-2
NA_HEADS = 16
NA_HD = D // NA_HEADS
NA_KH, NA_KW = 8, 16
FF = 2816
EPS = 1e-6

N_P = BATCH * SEQ
N_S = DEC_BATCH * DEC_SEQ
N_TOK = N_P + N_S
CTX_ROW = DEC_BATCH
MOD_ROWS = 8
LANE = 128
VMEM_LIMIT = 48 * 1024 * 1024
HIGHEST = lax.Precision.HIGHEST


def _cparams(sem):
    return pltpu.CompilerParams(dimension_semantics=sem, vmem_limit_bytes=VMEM_LIMIT)


def _mod_row(i, bm):
    return jnp.where(i < N_P // bm, CTX_ROW, (i - N_P // bm) // (DEC_SEQ // bm))


def _dot(a, b):
    return jnp.dot(a, b, preferred_element_type=F32)


def _adaln_kernel(c_ref, w_ref, b_ref, o_ref):
    cv = c_ref[...]
    s = cv * jax.nn.sigmoid(cv)
    o_ref[...] = _dot(s.astype(BF16), w_ref[...].astype(BF16)) + b_ref[...]


def adaln_all(cmat, w_ada, b_ada):
    tn = 1024
    return pl.pallas_call(
        _adaln_kernel,
        out_shape=jax.ShapeDtypeStruct((DEPTH, MOD_ROWS, 6 * D), F32),
        grid=(DEPTH, 6 * D // tn),
        in_specs=[
            pl.BlockSpec((MOD_ROWS, D), lambda l, j: (0, 0)),
            pl.BlockSpec((None, D, tn), lambda l, j: (l, 0, j)),
            pl.BlockSpec((None, 1, tn), lambda l, j: (l, 0, j)),
        ],
        out_specs=pl.BlockSpec((None, MOD_ROWS, tn), lambda l, j: (l, 0, j)),
        compiler_params=_cparams(("arbitrary", "arbitrary")),
        name="adaln",
    )(cmat, w_ada, b_ada.reshape(DEPTH, 1, 6 * D))


ROW_BM = 512
ROW_NPB = N_P // ROW_BM


def _row_specs(arr, cols):
    if isinstance(arr, tuple):
        return ([pl.BlockSpec((ROW_BM, cols), lambda i: (jnp.minimum(i, ROW_NPB - 1), 0)),
                 pl.BlockSpec((ROW_BM, cols), lambda i: (jnp.maximum(i - ROW_NPB, 0), 0))], list(arr))
    return [pl.BlockSpec((ROW_BM, cols), lambda i: (i, 0))], [arr]


def _pick_rows(refs):
    if len(refs) == 1:
        return refs[0][...]
    return jnp.where(pl.program_id(0) < ROW_NPB, refs[0][...], refs[1][...])


def _mod_spec(layer, which):
    return pl.BlockSpec((None, None, None, 1, D), lambda i: (layer, _mod_row(i, ROW_BM), which, 0, 0))


def _layer_vec_spec(layer):
    return pl.BlockSpec((None, 1, D), lambda i: (layer, 0, 0))


def _rms_mod(x, g_ref, sh_ref, sc_ref):
    y = x * lax.rsqrt(jnp.mean(x * x, axis=-1, keepdims=True) + EPS)
    return (y * g_ref[...]) * (1.0 + sc_ref[...]) + sh_ref[...]


def _norm_mod_kernel(*refs):
    x_refs, (g_ref, sh_ref, sc_ref, o_ref) = refs[:-4], refs[-4:]
    o_ref[...] = _rms_mod(_pick_rows(x_refs), g_ref, sh_ref, sc_ref).astype(o_ref.dtype)


def norm_mod(x, g_all, mod_all, layer, sh_idx, sc_idx):
    x_specs, x_args = _row_specs(x, D)
    return pl.pallas_call(
        _norm_mod_kernel,
        out_shape=jax.ShapeDtypeStruct((N_TOK, D), BF16),
        grid=(N_TOK // ROW_BM,),
        in_specs=x_specs + [_layer_vec_spec(layer), _mod_spec(layer, sh_idx), _mod_spec(layer, sc_idx)],
        out_specs=pl.BlockSpec((ROW_BM, D), lambda i: (i, 0)),
        compiler_params=_cparams(("arbitrary",)),
        name="norm_mod",
    )(*x_args, g_all.reshape(-1, 1, D), mod_all, mod_all)


def _out_proj_kernel(*refs, a_counts, n_x, final):
    pos = 0
    a_groups = []
    for cnt in a_counts:
        a_groups.append(refs[pos:pos + cnt])
        pos += cnt
    w_refs = refs[pos:pos + len(a_counts)]
    pos += len(a_counts)
    x_refs = refs[pos:pos + n_x]
    pos += n_x
    gt_ref, g_ref = refs[pos:pos + 2]
    pos += 2
    if not final:
        sh_ref, sc_ref = refs[pos:pos + 2]
        pos += 2
    out_a, out_b = refs[pos:pos + 2]
    wbf = refs[pos + 2:]

    @pl.when(pl.program_id(0) == 0)
    def _():
        for w_ref, wb in zip(w_refs, wbf):
            wb[...] = w_ref[...].astype(BF16)

    acc = None
    for group, wb in zip(a_groups, wbf):
        part = _dot(_pick_rows(group), wb[...])
        acc = part if acc is None else acc + part
    xn = _pick_rows(x_refs) + gt_ref[...] * acc
    if final:
        y = xn * lax.rsqrt(jnp.mean(xn * xn, axis=-1, keepdims=True) + EPS) * g_ref[...]

        @pl.when(pl.program_id(0) < ROW_NPB)
        def _():
            out_a[...] = y

        @pl.when(pl.program_id(0) >= ROW_NPB)
        def _():
            out_b[...] = y
    else:
        out_a[...] = xn
        out_b[...] = _rms_mod(xn, g_ref, sh_ref, sc_ref).astype(out_b.dtype)


def out_proj(a_list, w_list, x, mod_all, layer, gt_idx, g_all, g_layer, norm_mod_idx=None, name="out_proj"):
    final = norm_mod_idx is None
    in_specs, args, a_counts = [], [], []
    for a in a_list:
        cols = (a[0] if isinstance(a, tuple) else a).shape[1]
        specs, ops = _row_specs(a, cols)
        in_specs += specs
        args += ops
        a_counts.append(len(ops))
    w_shapes = []
    for w, block, imap in w_list:
        in_specs.append(pl.BlockSpec(block, imap, pipeline_mode=pl.Buffered(1)))
        args.append(w)
        w_shapes.append(tuple(b for b in block if b is not None))
    x_specs, x_args = _row_specs(x, D)
    in_specs += x_specs + [_mod_spec(layer, gt_idx), _layer_vec_spec(g_layer)]
    args += x_args + [mod_all, g_all.reshape(-1, 1, D)]
    if final:
        out_shape = [jax.ShapeDtypeStruct((N_P, D), F32), jax.ShapeDtypeStruct((N_S, D), F32)]
        out_specs = [pl.BlockSpec((ROW_BM, D), lambda i: (jnp.minimum(i, ROW_NPB - 1), 0)),
                     pl.BlockSpec((ROW_BM, D), lambda i: (jnp.maximum(i - ROW_NPB, 0), 0))]
    else:
        n_layer, sh_idx, sc_idx = norm_mod_idx
        in_specs += [_mod_spec(n_layer, sh_idx), _mod_spec(n_layer, sc_idx)]
        args += [mod_all, mod_all]
        out_shape = [jax.ShapeDtypeStruct((N_TOK, D), F32), jax.ShapeDtypeStruct((N_TOK, D), BF16)]
        out_specs = [pl.BlockSpec((ROW_BM, D), lambda i: (i, 0))] * 2
    return pl.pallas_call(
        functools.partial(_out_proj_kernel, a_counts=tuple(a_counts), n_x=len(x_args), final=final),
        out_shape=out_shape,
        grid=(N_TOK // ROW_BM,),
        in_specs=in_specs,
        out_specs=out_specs,
        scratch_shapes=[pltpu.VMEM(s, BF16) for s in w_shapes],
        compiler_params=_cparams(("arbitrary",)),
        name=name,
    )(*args)


def _mm_kernel(*refs, has_bias):
    a_ref, w_ref = refs[:2]
    b_ref = refs[2] if has_bias else None
    o_ref, wbf = refs[-2:]

    @pl.when(pl.program_id(1) == 0)
    def _():
        wbf[...] = w_ref[...].astype(BF16)

    acc = _dot(a_ref[...], wbf[...])
    if has_bias:
        acc = acc + b_ref[...]
    o_ref[...] = acc.astype(o_ref.dtype)


def token_matmul(a, w, *, tn, bm=ROW_BM, bias=None, out_dtype=F32, name="token_matmul"):
    k, nout = w.shape
    in_specs = [pl.BlockSpec((bm, k), lambda j, i: (i, 0)), pl.BlockSpec((k, tn), lambda j, i: (0, j))]
    args = [a, w]
    if bias is not None:
        in_specs.append(pl.BlockSpec((1, tn), lambda j, i: (0, j)))
        args.append(bias)
    return pl.pallas_call(
        functools.partial(_mm_kernel, has_bias=bias is not None),
        out_shape=jax.ShapeDtypeStruct((N_TOK, nout), out_dtype),
        grid=(nout // tn, N_TOK // bm),
        in_specs=in_specs,
        out_specs=pl.BlockSpec((bm, tn), lambda j, i: (i, j)),
        scratch_shapes=[pltpu.VMEM((k, tn), BF16)],
        compiler_params=_cparams(("arbitrary", "arbitrary")),
        name=name,
    )(*args)


def _qkv_kernel(a_ref, w_ref, o_ref, kc_ref, vc_ref, wbf):
    part, i = pl.program_id(0), pl.program_id(1)

    @pl.when(i == 0)
    def _():
        wbf[...] = w_ref[...].astype(BF16)

    acc = _dot(a_ref[...], wbf[...])
    for hh in range(NA_HEADS):
        o_ref[hh] = acc[:, hh * NA_HD:(hh + 1) * NA_HD].astype(o_ref.dtype)

    def to_cache(c_ref):
        for b in range(ROW_BM // SEQ):
            for hh in range(NA_HEADS):
                c_ref[b, hh] = acc[b * SEQ:(b + 1) * SEQ, hh * NA_HD:(hh + 1) * NA_HD]

    @pl.when((part == 1) & (i < ROW_NPB))
    def _():
        to_cache(kc_ref)

    @pl.when((part == 2) & (i < ROW_NPB))
    def _():
        to_cache(vc_ref)


def qkv_proj(h, w_in):
    last = ROW_NPB - 1
    k_rows = lambda j, i: jnp.where(j == 1, jnp.minimum(i, last), jnp.where(j == 0, 0, last))
    v_rows = lambda j, i: jnp.where(j == 2, jnp.minimum(i, last), 0)
    seqs = ROW_BM // SEQ
    cache = jax.ShapeDtypeStruct((BATCH, 1, NA_HEADS, SEQ, NA_HD), F32)
    return pl.pallas_call(
        _qkv_kernel,
        out_shape=[jax.ShapeDtypeStruct((3, NA_HEADS, N_TOK, NA_HD), BF16), cache, cache],
        grid=(3, N_TOK // ROW_BM),
        in_specs=[pl.BlockSpec((ROW_BM, D), lambda j, i: (i, 0)), pl.BlockSpec((D, D), lambda j, i: (0, j))],
        out_specs=[pl.BlockSpec((None, NA_HEADS, ROW_BM, NA_HD), lambda j, i: (j, 0, i, 0)),
                   pl.BlockSpec((seqs, None, NA_HEADS, SEQ, NA_HD), lambda j, i: (k_rows(j, i), 0, 0, 0, 0)),
                   pl.BlockSpec((seqs, None, NA_HEADS, SEQ, NA_HD), lambda j, i: (v_rows(j, i), 0, 0, 0, 0))],
        scratch_shapes=[pltpu.VMEM((D, D), BF16)],
        compiler_params=_cparams(("arbitrary", "arbitrary")),
        name="qkv_proj",
    )(h, w_in)


SEQ_BLOCK = DEC_SEQ
N_PBLK = N_P // SEQ_BLOCK


def _seq_edges(rows, is_prompt):
    r = lax.broadcasted_iota(jnp.int32, (rows, 1), 0)
    first = (r == 0) | (is_prompt & (r % SEQ == 0))
    last = (r == rows - 1) | (is_prompt & (r % SEQ == SEQ - 1))
    return first, last


def _dwconv3(x, w_ref, first, last):
    rows = x.shape[0]
    prev = jnp.where(first, 0.0, pltpu.roll(x, 1, 0))
    nxt = jnp.where(last, 0.0, pltpu.roll(x, rows - 1, 0))
    return prev * w_ref[0:1, :] + x * w_ref[1:2, :] + nxt * w_ref[2:3, :]


def _gated_gelu_of_half(y, g):
    c = math.sqrt(2.0 / math.pi)
    t = jnp.tanh(y * (2.0 * c + (8.0 * 0.044715 * c) * (y * y)))
    return (y + y * t) * g


def _ffn_up_kernel(h_ref, wa_ref, wg_ref, wc_ref, o_ref):
    first, last = _seq_edges(SEQ_BLOCK, pl.program_id(0) < N_PBLK)
    h = h_ref[...]
    a = _dot(h, wa_ref[...].astype(BF16))
    g = _dot(h, wg_ref[...].astype(BF16))
    half_conv = _dwconv3(a, 0.5 * wc_ref[...], first, last)
    o_ref[...] = _gated_gelu_of_half(half_conv, g).astype(o_ref.dtype)


def ffn_up(h, w_up, w_conv, layer):
    tc = 256
    nct = FF // tc
    return pl.pallas_call(
        _ffn_up_kernel,
        out_shape=jax.ShapeDtypeStruct((N_TOK, FF), BF16),
        grid=(N_TOK // SEQ_BLOCK, nct),
        in_specs=[
            pl.BlockSpec((SEQ_BLOCK, D), lambda i, j: (i, 0)),
            pl.BlockSpec((None, D, tc), lambda i, j: (layer, 0, j)),
            pl.BlockSpec((None, D, tc), lambda i, j: (layer, 0, j + nct)),
            pl.BlockSpec((None, 3, tc), lambda i, j: (layer, 0, j)),
        ],
        out_specs=pl.BlockSpec((SEQ_BLOCK, tc), lambda i, j: (i, j)),
        compiler_params=_cparams(("arbitrary", "arbitrary")),
        name="ffn_up",
    )(h, w_up, w_up, w_conv)


def _ctx_attn_kernel(q_ref, k_ref, v_ref, o_ref, *, heads):
    outs = []
    for hh in range(heads):
        q = q_ref[hh].astype(BF16)
        k = k_ref[hh].astype(BF16)
        s = lax.dot_general(q, k, (((1,), (1,)), ((), ())), preferred_element_type=F32) * (NA_HD ** -0.5)
        m = jnp.max(s, axis=-1, keepdims=True)
        p = jnp.exp(s - m)
        l = jnp.sum(p, axis=-1, keepdims=True)
        outs.append(_dot(p.astype(BF16), v_ref[hh].astype(BF16)) / l)
    o_ref[...] = jnp.concatenate(outs, axis=-1).astype(o_ref.dtype)


def ctx_attention(qkv):
    heads = 8
    spec = lambda part: pl.BlockSpec((None, heads, SEQ, NA_HD), lambda b, h: (part, h, b, 0))
    return pl.pallas_call(
        functools.partial(_ctx_attn_kernel, heads=heads),
        out_shape=jax.ShapeDtypeStruct((N_P, D), BF16),
        grid=(BATCH, NA_HEADS // heads),
        in_specs=[spec(0), spec(1), spec(2)],
        out_specs=pl.BlockSpec((SEQ, heads * NA_HD), lambda b, h: (b, h)),
        compiler_params=_cparams(("arbitrary", "arbitrary")),
        name="ctx_attn",
    )(qkv, qkv, qkv)


def _na_tables():
    q = np.arange(GRID_W)[:, None]
    w = np.arange(GRID_W)[None, :]
    idx_c = np.clip(w - q + (NA_KW - 1), 0, 2 * NA_KW - 2)
    onehot = (idx_c.reshape(1, -1) == np.arange(32)[:, None]).astype(np.float32)
    c_start = np.clip(np.arange(GRID_W) - NA_KW // 2, 0, GRID_W - NA_KW)[:, None]
    inside = (w >= c_start) & (w < c_start + NA_KW)
    cmask = np.where(inside, 0.0, -np.inf).astype(np.float32)
    return onehot, np.tile(cmask, (1, 2))


def _rpb_expand_kernel(r_ref, e_ref, o_ref):
    o_ref[...] = jnp.dot(r_ref[...], e_ref[...], precision=HIGHEST, preferred_element_type=F32)


def rpb_expand(rpb):
    onehot, _ = _na_tables()
    rp = jnp.pad(rpb, ((0, 0), (0, 1), (0, 1)))
    return pl.pallas_call(
        _rpb_expand_kernel,
        out_shape=jax.ShapeDtypeStruct((NA_HEADS, 16, GRID_W * GRID_W), F32),
        grid=(NA_HEADS,),
        in_specs=[pl.BlockSpec((None, 16, 32), lambda h: (h, 0, 0)),
                  pl.BlockSpec((32, GRID_W * GRID_W), lambda h: (0, 0))],
        out_specs=pl.BlockSpec((None, 16, GRID_W * GRID_W), lambda h: (h, 0, 0)),
        compiler_params=_cparams(("arbitrary",)),
        name="rpb_expand",
    )(rp, jnp.asarray(onehot))


NA_QROWS = 8
NA_WIN = 2 * NA_QROWS


def _na_attn_kernel(q_ref, k_ref, v_ref, kc_ref, vc_ref, t_ref, o_ref, p_loc, p_ctx, *, heads):
    nq = NA_QROWS * GRID_W
    nk = NA_WIN * GRID_W
    pair = 2 * GRID_W
    nt = (((1,), (1,)), ((), ()))
    lane = lax.broadcasted_iota(jnp.int32, (GRID_W, pair), 1)
    zero_tile = jnp.zeros((GRID_W, pair), BF16)
    for hh in range(heads):
        kc = kc_ref[hh].astype(BF16)
        vc = vc_ref[hh].astype(BF16)
        for blk in range(GRID_R // NA_QROWS):
            k0 = min(max(NA_QROWS * blk - NA_KH // 2, 0), GRID_R - NA_WIN)
            q = (q_ref[hh, blk * nq:(blk + 1) * nq, :] * (NA_HD ** -0.5)).astype(BF16)
            kw = k_ref[hh, k0 * GRID_W:k0 * GRID_W + nk, :].astype(BF16)
            vw = v_ref[hh, k0 * GRID_W:k0 * GRID_W + nk, :].astype(BF16)
            s_loc = lax.dot_general(q, kw, nt, preferred_element_type=F32)
            s_ctx = lax.dot_general(q, kc, nt, preferred_element_type=F32)
            denoms = []
            for qi in range(NA_QROWS):
                r = NA_QROWS * blk + qi
                r_start = min(max(r - NA_KH // 2, 0), GRID_R - NA_KH)
                rows = slice(qi * GRID_W, (qi + 1) * GRID_W)
                ctx_tiles = [s_ctx[rows, c * pair:(c + 1) * pair] for c in range(PAST_LEN // pair)]
                tiles = {}
                for j in range(nk // pair):
                    kr = k0 + 2 * j
                    ok0 = r_start <= kr < r_start + NA_KH
                    ok1 = r_start <= kr + 1 < r_start + NA_KH
                    if not (ok0 or ok1):
                        continue
                    sb = s_loc[rows, j * pair:(j + 1) * pair] + t_ref[hh, kr - r + NA_KH]
                    if not (ok0 and ok1):
                        sb = jnp.where((lane < GRID_W) if ok0 else (lane >= GRID_W), sb, -jnp.inf)
                    tiles[j] = sb
                mx = functools.reduce(jnp.maximum, list(tiles.values()) + ctx_tiles)
                m = jnp.max(mx, axis=1, keepdims=True)
                acc = None
                for j in range(nk // pair):
                    if j in tiles:
                        p = jnp.exp(tiles[j] - m)
                        acc = p if acc is None else acc + p
                        p_loc[rows, j * pair:(j + 1) * pair] = p.astype(BF16)
                    else:
                        p_loc[rows, j * pair:(j + 1) * pair] = zero_tile
                for c, t in enumerate(ctx_tiles):
                    p = jnp.exp(t - m)
                    acc = acc + p
                    p_ctx[rows, c * pair:(c + 1) * pair] = p.astype(BF16)
                denoms.append(jnp.sum(acc, axis=1, keepdims=True))
            o = (_dot(p_loc[...], vw) + _dot(p_ctx[...], vc)) / jnp.concatenate(denoms, axis=0)
            o_ref[blk * nq:(blk + 1) * nq, hh * NA_HD:(hh + 1) * NA_HD] = o.astype(o_ref.dtype)


def na_attention(qkv, k_ctx, v_ctx, bias_pairs):
    heads = 2
    blk0 = N_P // DEC_SEQ
    spec = lambda part: pl.BlockSpec((None, heads, DEC_SEQ, NA_HD), lambda b, h: (part, h, b + blk0, 0))
    cspec = pl.BlockSpec((None, heads, PAST_LEN, NA_HD), lambda b, h: (b, h, 0, 0))
    return pl.pallas_call(
        functools.partial(_na_attn_kernel, heads=heads),
        out_shape=jax.ShapeDtypeStruct((N_S, D), BF16),
        grid=(DEC_BATCH, NA_HEADS // heads),
        in_specs=[spec(0), spec(1), spec(2), cspec, cspec,
                  pl.BlockSpec((heads, 16, GRID_W, 2 * GRID_W), lambda b, h: (h, 0, 0, 0))],
        out_specs=pl.BlockSpec((DEC_SEQ, heads * NA_HD), lambda b, h: (b, h)),
        scratch_shapes=[pltpu.VMEM((NA_QROWS * GRID_W, NA_WIN * GRID_W), BF16),
                        pltpu.VMEM((NA_QROWS * GRID_W, PAST_LEN), BF16)],
        compiler_params=_cparams(("arbitrary", "arbitrary")),
        name="na_attn",
    )(qkv, qkv, qkv, k_ctx, v_ctx, bias_pairs)


def mixer_c(h, k_ctx, v_ctx, w_in, rpb):
    qkv, k_new, v_new = qkv_proj(h, w_in)
    o_p = ctx_attention(qkv)
    _, cmask2 = _na_tables()
    b15 = rpb_expand(rpb).reshape(NA_HEADS, 16, GRID_W, GRID_W)
    b17 = jnp.pad(b15, ((0, 0), (1, 0), (0, 0), (0, 0)))
    bias_pairs = jnp.concatenate([b17[:, :16], b17[:, 1:]], axis=-1) + jnp.asarray(cmask2)
    o_s = na_attention(qkv, k_ctx, v_ctx, bias_pairs)
    return (o_p, o_s), k_new, v_new


SCAN_BLOCK = HD_M


def _mlstm_kernel(*refs, seq, zero_state, emit_state):
    q_ref, k_ref, v_ref, og_ref, gates_ref, wq_ref, wk_ref, gh_ref = refs[:8]
    pos = 8
    if not zero_state:
        c0_ref, n0_ref, m0_ref = refs[pos:pos + 3]
        pos += 3
    y_ref = refs[pos]
    pos += 1
    if emit_state:
        cn_ref, nn_ref, mn_ref = refs[pos:pos + 3]
        pos += 3
    kv_s, ks_s, be_s, mk_s, cp_s, np_s, mp_s, cst, nst, mst = refs[pos:]

    head = pl.program_id(1)
    blk = SCAN_BLOCK
    nc = seq // blk
    r = lax.broadcasted_iota(jnp.int32, (seq, 1), 0)
    first, last = r == 0, r == seq - 1
    qc_all = _dwconv3(q_ref[...].astype(F32), wq_ref, first, last)
    q3 = (qc_all * jax.nn.sigmoid(qc_all)).reshape(nc, blk, HD_M)
    kc_all = _dwconv3(k_ref[...].astype(F32), wk_ref, first, last)
    k3 = (kc_all * jax.nn.sigmoid(kc_all) * (HD_M ** -0.5)).reshape(nc, blk, HD_M)
    qb, kb = q3.astype(BF16), k3.astype(BF16)
    vb = v_ref[...].astype(BF16).reshape(nc, blk, HD_M)

    g_all = gates_ref[...]
    lf = jax.nn.log_sigmoid(g_all)
    rin = r % blk
    pre, suf = lf, lf
    for sh in [1 << i for i in range(blk.bit_length() - 1)]:
        pre = pre + jnp.where(rin >= sh, pltpu.roll(pre, sh, 0), 0.0)
        suf = suf + jnp.where(rin < blk - sh, pltpu.roll(suf, seq - sh, 0), 0.0)
    g3 = g_all.reshape(nc, blk, LANE)
    gt3 = jnp.swapaxes(g3, 1, 2)[:, :4 * HEADS_M, :]

    if zero_state:
        cst[...] = jnp.zeros_like(cst)
        nst[...] = jnp.zeros_like(nst)
        mst[...] = jnp.zeros_like(mst)
    else:
        cst[...] = c0_ref[...]
        nst[...] = n0_ref[...]
        mst[...] = jnp.broadcast_to(m0_ref[...], mst.shape)

    tt = lax.broadcasted_iota(jnp.int32, (1, blk, blk), 1)
    ss = lax.broadcasted_iota(jnp.int32, (1, blk, blk), 2)
    lane = lax.broadcasted_iota(jnp.int32, (1, 1, LANE), 2)
    sub = lax.broadcasted_iota(jnp.int32, (1, 4 * HEADS_M, 1), 1)
    hsum = None
    for d in range(2):
        i_idx = d * 2 * HEADS_M + head
        f_idx = i_idx + HEADS_M
        mask = (ss <= tt) if d == 0 else (ss >= tt)
        b3 = (pre if d == 0 else suf).reshape(nc, blk, LANE)
        bt3 = jnp.swapaxes(b3, 1, 2)[:, :4 * HEADS_M, :]
        bcol = jnp.sum(jnp.where(lane == f_idx, b3, 0.0), axis=2, keepdims=True)
        icol = jnp.sum(jnp.where(lane == i_idx, g3, 0.0), axis=2, keepdims=True)
        brow = jnp.sum(jnp.where(sub == f_idx, bt3, 0.0), axis=1, keepdims=True)
        irow = jnp.sum(jnp.where(sub == i_idx, gt3, 0.0), axis=1, keepdims=True)
        bend = bcol[:, blk - 1:blk, :] if d == 0 else bcol[:, 0:1, :]

        dmat = jnp.where(mask, bcol - brow + irow, -jnp.inf)
        mloc = jnp.max(dmat, axis=2, keepdims=True)
        qk = jnp.einsum('ctd,csd->cts', qb, kb, preferred_element_type=F32)
        s_loc = jnp.exp(dmat - mloc) * qk
        num_loc = jnp.einsum('cts,csd->ctd', s_loc.astype(BF16), vb, preferred_element_type=F32)
        den_loc = jnp.sum(s_loc, axis=2, keepdims=True)
        to_end = bend - bcol + icol
        mk = jnp.max(to_end, axis=1, keepdims=True)
        kw = k3 * jnp.exp(to_end - mk)
        kv_s[...] = jnp.einsum('cds,cse->cde', jnp.swapaxes(kw, 1, 2).astype(BF16), vb,
                               preferred_element_type=F32)
        ks_s[...] = jnp.sum(kw, axis=1, keepdims=True)
        be_s[...] = jnp.broadcast_to(bend, be_s.shape)
        mk_s[...] = jnp.broadcast_to(mk, mk_s.shape)

        def step(j, carry, d=d):
            c = j if d == 0 else nc - 1 - j
            m_prev, c_prev, n_prev = mst[d], cst[d], nst[d]
            cp_s[c] = c_prev.astype(BF16)
            np_s[c] = n_prev
            mp_s[c] = m_prev
            be, mkc = be_s[c], mk_s[c]
            m_new = jnp.maximum(be + m_prev, mkc)
            keep = jnp.exp(be + m_prev - m_new)
            add = jnp.exp(mkc - m_new)
            cst[d] = keep * c_prev + add * kv_s[c]
            nst[d] = keep * n_prev + add * ks_s[c]
            mst[d] = m_new
            return carry

        lax.fori_loop(0, nc, step, 0)

        m_inter = bcol + mp_s[...][:, :, 0:1]
        m_t = jnp.maximum(m_inter, mloc)
        w_state = jnp.exp(m_inter - m_t)
        w_loc = jnp.exp(mloc - m_t)
        inter = jnp.einsum('ctd,cde->cte', qb, cp_s[...], preferred_element_type=F32)
        num = w_state * inter + w_loc * num_loc
        den = w_state * jnp.sum(q3 * np_s[...], axis=2, keepdims=True) + w_loc * den_loc
        h = num / jnp.maximum(jnp.abs(den), jnp.exp(-m_t))
        hsum = h if hsum is None else hsum + h

    hsum = hsum.reshape(seq, HD_M)
    hn = hsum * lax.rsqrt(jnp.mean(hsum * hsum, axis=-1, keepdims=True) + EPS) * gh_ref[...]
    y_ref[...] = (hn * jax.nn.sigmoid(og_ref[...].astype(F32))).astype(y_ref.dtype)
    if emit_state:
        cn_ref[...] = cst[...]
        nn_ref[...] = nst[...]
        mn_ref[...] = mst[...]


def mlstm(proj, gates, w_conv_qk, g_head, state, *, nb, seq, row0):
    blk0 = row0 // seq
    nblk = seq // SCAN_BLOCK
    zero_state = state is None
    col = lambda part: pl.BlockSpec((seq, HD_M), lambda b, h: (b + blk0, part * HEADS_M + h))
    in_specs = [col(0), col(1), col(2), col(3),
                pl.BlockSpec((seq, LANE), lambda b, h: (b + blk0, 0)),
                pl.BlockSpec((3, HD_M), lambda b, h: (0, h)),
                pl.BlockSpec((3, HD_M), lambda b, h: (0, HEADS_M + h)),
                pl.BlockSpec((1, HD_M), lambda b, h: (0, h))]
    args = [proj, proj, proj, proj, gates, w_conv_qk, w_conv_qk, g_head.reshape(1, MW)]
    if not zero_state:
        c0, n0, m0 = state
        in_specs += [pl.BlockSpec((None, 2, None, HD_M, HD_M), lambda b, h: (b, 0, h, 0, 0)),
                     pl.BlockSpec((None, 2, None, 1, HD_M), lambda b, h: (b, 0, h, 0, 0)),
                     pl.BlockSpec((None, 2, None, 1, 1), lambda b, h: (b, 0, h, 0, 0))]
        args += [c0, n0.reshape(nb, 2, HEADS_M, 1, HD_M), m0.reshape(nb, 2, HEADS_M, 1, 1)]
    out_shape = [jax.ShapeDtypeStruct((nb * seq, MW), BF16)]
    out_specs = [pl.BlockSpec((seq, HD_M), lambda b, h: (b, h))]
    if zero_state:
        out_shape += [jax.ShapeDtypeStruct((nb, 2, HEADS_M, HD_M, HD_M), F32),
                      jax.ShapeDtypeStruct((nb, 2, HEADS_M, 1, HD_M), F32),
                      jax.ShapeDtypeStruct((nb, 2, HEADS_M, 1, LANE), F32)]
        out_specs += [pl.BlockSpec((None, 2, None, HD_M, HD_M), lambda b, h: (b, 0, h, 0, 0)),
                      pl.BlockSpec((None, 2, None, 1, HD_M), lambda b, h: (b, 0, h, 0, 0)),
                      pl.BlockSpec((None, 2, None, 1, LANE), lambda b, h: (b, 0, h, 0, 0))]
    return pl.pallas_call(
        functools.partial(_mlstm_kernel, seq=seq, zero_state=zero_state, emit_state=zero_state),
        out_shape=out_shape,
        grid=(nb, HEADS_M),
        in_specs=in_specs,
        out_specs=out_specs,
        scratch_shapes=[pltpu.VMEM((nblk, HD_M, HD_M), F32), pltpu.VMEM((nblk, 1, HD_M), F32),
                        pltpu.VMEM((nblk, 1, LANE), F32), pltpu.VMEM((nblk, 1, LANE), F32),
                        pltpu.VMEM((nblk, HD_M, HD_M), BF16), pltpu.VMEM((nblk, 1, HD_M), F32),
                        pltpu.VMEM((nblk, 1, LANE), F32),
                        pltpu.VMEM((2, HD_M, HD_M), F32), pltpu.VMEM((2, 1, HD_M), F32),
                        pltpu.VMEM((2, 1, LANE), F32)],
        compiler_params=_cparams(("arbitrary", "arbitrary")),
        name="mlstm_%d" % seq,
    )(*args)


def _hyena_pre_kernel(v_ref, x1_ref, x2_ref, wv_ref, w1_ref, w2_ref, u_ref, x2c_ref):
    first, last = _seq_edges(SEQ_BLOCK, pl.program_id(0) < N_PBLK)
    x1c = _dwconv3(x1_ref[...].astype(F32), w1_ref, first, last)
    u_ref[...] = x1c * _dwconv3(v_ref[...].astype(F32), wv_ref, first, last)
    x2c_ref[...] = _dwconv3(x2_ref[...].astype(F32), w2_ref, first, last)


def hyena_pre(proj, w_conv_hy):
    tc = 256
    nct = HW // tc
    c0 = 4 * MW // tc
    pcol = lambda part: pl.BlockSpec((SEQ_BLOCK, tc), lambda i, j: (i, c0 + part * nct + j))
    wcol = lambda part: pl.BlockSpec((3, tc), lambda i, j: (0, part * nct + j))
    out = jax.ShapeDtypeStruct((N_TOK, HW), F32)
    ospec = pl.BlockSpec((SEQ_BLOCK, tc), lambda i, j: (i, j))
    return pl.pallas_call(
        _hyena_pre_kernel,
        out_shape=[out, out],
        grid=(N_TOK // SEQ_BLOCK, nct),
        in_specs=[pcol(0), pcol(1), pcol(2), wcol(0), wcol(1), wcol(2)],
        out_specs=[ospec, ospec],
        compiler_params=_cparams(("arbitrary", "arbitrary")),
        name="hyena_pre",
    )(proj, proj, proj, w_conv_hy, w_conv_hy, w_conv_hy)


@functools.lru_cache(maxsize=None)
def _filter_tables(seq):
    t = np.linspace(0.0, 1.0, seq)[:, None]
    wpos = 2.0 * np.pi * np.arange(seq)[:, None] / seq
    bands = np.linspace(1e-4, N_BANDS - 1, N_BANDS)[None, :]
    z = np.concatenate([t, np.cos(bands * wpos), -np.sin(bands * wpos)], axis=-1)
    z = np.pad(z, ((0, 0), (0, LANE - FILTER_EMB)))
    max_decay = math.log(DECAY_TARGET) / DECAY_FAST
    min_decay = math.log(DECAY_TARGET) / DECAY_SLOW
    deltas = np.abs(np.linspace(min_decay, max_decay, HW))
    decay = np.exp(-t * np.concatenate([deltas, deltas])[None, :])
    return z.astype(np.float32), decay.astype(np.float32)


@functools.lru_cache(maxsize=None)
def _dft_tables(seq, tk):
    n = 2 * seq
    k = np.arange(seq)[:, None]
    t = np.arange(seq)[None, :]
    ang = 2.0 * np.pi * ((k * t) % n) / n
    alt = np.where(np.arange(seq) % 2 == 0, 1.0, -1.0)
    cm, sm = np.cos(ang), np.sin(ang)
    sm[0, :] = alt
    fwd = np.stack([cm.reshape(seq // tk, tk, seq), sm.reshape(seq // tk, tk, seq)], axis=1)
    wk = np.where(np.arange(seq) == 0, 1.0, 2.0)[None, :]
    ci = (np.cos(ang.T) * wk) / n
    si = np.sin(ang.T) * 2.0 / n
    si[:, 0] = alt / n
    inv = np.concatenate([ci, si], axis=1)
    return fwd.astype(np.float32), inv.astype(np.float32)


def _filter_kernel(z_ref, w1_ref, b1_ref, w2_ref, b2_ref, w3_ref, fr_ref, dec_ref, hs_ref, hd_ref):
    fr = fr_ref[...]
    hp = functools.partial(jnp.dot, precision=HIGHEST, preferred_element_type=F32)
    h1 = jnp.sin(fr * (hp(z_ref[...], w1_ref[...]) + b1_ref[...]))
    h2 = jnp.sin(fr * (hp(h1, w2_ref[...]) + b2_ref[...]))
    filt = hp(h2, w3_ref[...]) * dec_ref[...]
    past, fut = filt[:, :HW], filt[:, HW:]
    rows = filt.shape[0]
    grow = lax.broadcasted_iota(jnp.int32, (rows, 1), 0) + pl.program_id(0) * rows
    fut = jnp.where(grow == 0, 0.0, fut)
    hs_ref[...] = past + fut
    hd_ref[...] = past - fut


def filter_gen(seq, w1, b1, w2, b2, w3, freq):
    z, decay = _filter_tables(seq)
    tl = 256
    fh = FILTER_HIDDEN
    full = lambda shape: pl.BlockSpec(shape, lambda i: (0, 0))
    out = jax.ShapeDtypeStruct((seq, HW), F32)
    return pl.pallas_call(
        _filter_kernel,
        out_shape=[out, out],
        grid=(seq // tl,),
        in_specs=[pl.BlockSpec((tl, LANE), lambda i: (i, 0)), full((LANE, fh)), full((1, fh)), full((fh, fh)),
                  full((1, fh)), full((fh, 2 * HW)), full((1, fh)), pl.BlockSpec((tl, 2 * HW), lambda i: (i, 0))],
        out_specs=[pl.BlockSpec((tl, HW), lambda i: (i, 0))] * 2,
        compiler_params=_cparams(("arbitrary",)),
        name="filter_gen",
    )(jnp.asarray(z), jnp.pad(w1, ((0, LANE - FILTER_EMB), (0, 0))), b1.reshape(1, fh), w2, b2.reshape(1, fh), w3,
      freq.reshape(1, fh), jnp.asarray(decay))


def _dft_filter_kernel(a_ref, hs_ref, hd_ref, k_ref, hs_bf, hd_bf):
    @pl.when(pl.program_id(0) == 0)
    def _():
        hs_bf[...] = hs_ref[...].astype(BF16)
        hd_bf[...] = hd_ref[...].astype(BF16)

    k_ref[0] = _dot(a_ref[0], hs_bf[...])
    k_ref[1] = _dot(a_ref[1], hd_bf[...])

    @pl.when(pl.program_id(0) == 0)
    def _():
        k_ref[1, 0:1, :] = _dot(a_ref[1, 0:8, :], hs_bf[...])[0:1, :]


def dft_filter(seq, tk, hs, hd):
    fwd, _ = _dft_tables(seq, tk)
    return pl.pallas_call(
        _dft_filter_kernel,
        out_shape=jax.ShapeDtypeStruct((2, seq, HW), F32),
        grid=(seq // tk,),
        in_specs=[pl.BlockSpec((None, 2, tk, seq), lambda m: (m, 0, 0, 0)),
                  pl.BlockSpec((seq, HW), lambda m: (0, 0)), pl.BlockSpec((seq, HW), lambda m: (0, 0))],
        out_specs=pl.BlockSpec((2, tk, HW), lambda m: (0, m, 0)),
        scratch_shapes=[pltpu.VMEM((seq, HW), BF16), pltpu.VMEM((seq, HW), BF16)],
        compiler_params=_cparams(("arbitrary",)),
        name="dft_filter",
    )(jnp.asarray(fwd).astype(BF16), hs, hd)


def _dft_fwd_kernel(a_ref, u_ref, k_ref, y_ref, u_bf):
    m = pl.program_id(1)

    @pl.when(m == 0)
    def _():
        u_bf[...] = u_ref[...].astype(BF16)

    ure = _dot(a_ref[0], u_bf[...])
    uim = _dot(a_ref[1], u_bf[...])
    kre, kim = k_ref[0], k_ref[1]
    packed = (lax.broadcasted_iota(jnp.int32, (ure.shape[0], 1), 0) == 0) & (m == 0)
    y_ref[0] = jnp.where(packed, ure * kre, ure * kre - uim * kim).astype(y_ref.dtype)
    y_ref[1] = jnp.where(packed, uim * kim, ure * kim + uim * kre).astype(y_ref.dtype)


def dft_fwd(seq, tk, u, kf, *, nb, row0):
    fwd, _ = _dft_tables(seq, tk)
    blk0 = row0 // seq
    return pl.pallas_call(
        _dft_fwd_kernel,
        out_shape=jax.ShapeDtypeStruct((nb, 2, seq, HW), BF16),
        grid=(nb, seq // tk),
        in_specs=[pl.BlockSpec((None, 2, tk, seq), lambda b, m: (m, 0, 0, 0)),
                  pl.BlockSpec((seq, HW), lambda b, m: (b + blk0, 0)),
                  pl.BlockSpec((2, tk, HW), lambda b, m: (0, m, 0))],
        out_specs=pl.BlockSpec((None, 2, tk, HW), lambda b, m: (b, 0, m, 0)),
        scratch_shapes=[pltpu.VMEM((seq, HW), BF16)],
        compiler_params=_cparams(("arbitrary", "arbitrary")),
        name="dft_fwd",
    )(jnp.asarray(fwd).astype(BF16), u, kf)


def _dft_inv_kernel(a_ref, y_ref, u_ref, x2_ref, bias_ref, o_ref, *, seq):
    conv = _dot(a_ref[:, :seq], y_ref[0]) + _dot(a_ref[:, seq:], y_ref[1])
    o_ref[...] = (x2_ref[...] * (conv + bias_ref[...] * u_ref[...])).astype(o_ref.dtype)


def dft_inv(seq, tk, y, u, x2c, bias, *, nb, row0):
    _, inv = _dft_tables(seq, tk)
    tm = min(seq, 512)
    nt = seq // tm
    blk0 = row0 // tm
    rows = lambda b, t: (b * nt + t + blk0, 0)
    return pl.pallas_call(
        functools.partial(_dft_inv_kernel, seq=seq),
        out_shape=jax.ShapeDtypeStruct((nb * seq, HW), BF16),
        grid=(nb, nt),
        in_specs=[pl.BlockSpec((tm, 2 * seq), lambda b, t: (t, 0)),
                  pl.BlockSpec((None, 2, seq, HW), lambda b, t: (b, 0, 0, 0)),
                  pl.BlockSpec((tm, HW), rows), pl.BlockSpec((tm, HW), rows),
                  pl.BlockSpec((1, HW), lambda b, t: (0, 0))],
        out_specs=pl.BlockSpec((tm, HW), lambda b, t: (b * nt + t, 0)),
        compiler_params=_cparams(("arbitrary", "arbitrary")),
        name="dft_inv",
    )(jnp.asarray(inv).astype(BF16), y, u, x2c, bias.reshape(1, HW))


def hyena_group(seq, u, x2c, filt_w, bias, *, nb, row0):
    tk = min(seq, 512)
    hs, hd = filter_gen(seq, *filt_w)
    kf = dft_filter(seq, tk, hs, hd)
    y = dft_fwd(seq, tk, u, kf, nb=nb, row0=row0)
    return dft_inv(seq, tk, y, u, x2c, bias, nb=nb, row0=row0)


def mixer_ab_parts(h, state_s, w_in, b_gates, w_conv_qk, g_head, w_conv_hy, w_f1, b_f1, w_f2, b_f2, w_f3, freq,
                   hy_bias):
    n_g = 4 * HEADS_M
    w_main = jnp.concatenate([w_in[:, :4 * MW], w_in[:, 4 * MW + n_g:]], axis=1)
    w_g = jnp.pad(w_in[:, 4 * MW:4 * MW + n_g], ((0, 0), (0, LANE - n_g)))
    b_g = jnp.pad(b_gates, (0, LANE - n_g)).reshape(1, LANE)
    proj = token_matmul(h, w_main, tn=896, out_dtype=BF16, name="ab_proj")
    gates = token_matmul(h, w_g, tn=LANE, bias=b_g, name="ab_gates")
    ym_p, c_p, n_p, m_p = mlstm(proj, gates, w_conv_qk, g_head, None, nb=BATCH, seq=SEQ, row0=0)
    (ym_s,) = mlstm(proj, gates, w_conv_qk, g_head, state_s, nb=DEC_BATCH, seq=DEC_SEQ, row0=N_P)
    u, x2c = hyena_pre(proj, w_conv_hy)
    filt_w = (w_f1, b_f1, w_f2, b_f2, w_f3, freq)
    yh_p = hyena_group(SEQ, u, x2c, filt_w, hy_bias, nb=BATCH, row0=0)
    yh_s = hyena_group(DEC_SEQ, u, x2c, filt_w, hy_bias, nb=DEC_BATCH, row0=N_P)
    state_p = (c_p, n_p[:, :, :, 0, :], m_p[:, :, :, 0, 0])
    return (ym_p, ym_s), (yh_p, yh_s), state_p


def kernel(x_prompt, x_sample, state_mlstm_C, state_mlstm_n, state_mlstm_m, cache_na_k, cache_na_v, c, c_ctx, w_ada, b_ada, g_mix, g_ffn, g_final, w_in_ab, b_gates, w_conv_qk, g_mlstm, w_conv_hy, w_filt1, b_filt1, w_filt2, b_filt2, w_filt3, filt_freq, hyena_bias, w_out_ab, w_in_c, rpb_c, w_out_c, w_up, w_conv_ffn, w_down):
    cmat = jnp.concatenate([c, c_ctx[None, :], jnp.zeros((MOD_ROWS - DEC_BATCH - 1, D), F32)], axis=0)
    mod_all = adaln_all(cmat, w_ada, b_ada).reshape(DEPTH, MOD_ROWS, 6, 1, D)
    x = (x_prompt.reshape(N_P, D), x_sample.reshape(N_S, D))
    h = norm_mod(x, g_mix, mod_all, 0, 0, 1)
    new_c, new_n, new_m, new_k, new_v = [], [], [], [], []
    for l in range(DEPTH):
        e = l // 2
        if l % 2 == 0:
            state_s = (state_mlstm_C[:, e], state_mlstm_n[:, e], state_mlstm_m[:, e])
            y_m, y_h, (c_p, n_p, m_p) = mixer_ab_parts(
                h, state_s, w_in_ab[e], b_gates[e], w_conv_qk[e], g_mlstm[e], w_conv_hy[e], w_filt1[e], b_filt1[e],
                w_filt2[e], b_filt2[e], w_filt3[e], filt_freq[e], hyena_bias[e])
            a_list = [y_m, y_h]
            w_list = [(w_out_ab, (None, MW, D), lambda i, e=e: (e, 0, 0)),
                      (w_out_ab, (None, HW, D), lambda i, e=e: (e, 1, 0))]
            new_c.append(c_p[:, None])
            new_n.append(n_p[:, None])
            new_m.append(m_p[:, None])
        else:
            o, k_new, v_new = mixer_c(h, cache_na_k[:, e], cache_na_v[:, e], w_in_c[e], rpb_c[e])
            a_list = [o]
            w_list = [(w_out_c, (None, D, D), lambda i, e=e: (e, 0, 0))]
            new_k.append(k_new)
            new_v.append(v_new)
        x, h = out_proj(a_list, w_list, x, mod_all, l, 2, g_ffn, l, (l, 3, 4), name="mixer_out")
        mid = ffn_up(h, w_up, w_conv_ffn, l)
        w_list = [(w_down, (None, FF, D), lambda i, l=l: (l, 0, 0))]
        if l + 1 < DEPTH:
            x, h = out_proj([mid], w_list, x, mod_all, l, 5, g_mix, l + 1, (l + 1, 0, 1), name="ffn_down")
        else:
            y_p, y_s = out_proj([mid], w_list, x, mod_all, l, 5, g_final, 0, None, name="ffn_down_final")
    cat = lambda parts: parts[0] if len(parts) == 1 else jnp.concatenate(parts, axis=1)
    return (y_p.reshape(BATCH, SEQ, D), y_s.reshape(DEC_BATCH, DEC_SEQ, D), cat(new_c), cat(new_n), cat(new_m),
            cat(new_k), cat(new_v))
```

```python
import functools
import math

import numpy as np
import jax
import jax.numpy as jnp
from jax import lax
from jax.experimental import pallas as pl
from jax.experimental.pallas import tpu as pltpu

F32 = jnp.float32
BF16 = jnp.bfloat16

D = 1024
BATCH, SEQ = 16, 256
DEC_BATCH, DEC_SEQ = 4, 2048
PAST_LEN = 512
DEPTH = 2
GRID_W = 64
GRID_R = DEC_SEQ // GRID_W
HEADS_M = 4
MW = D // 2
HD_M = MW // HEADS_M
CHUNK = 64
HW = D // 2
N_BANDS = 16
FILTER_EMB = 2 * N_BANDS + 1
FILTER_HIDDEN = 64
DECAY_FAST, DECAY_SLOW, DECAY_TARGET = 0.3, 1.5, 1e-2
NA_HEADS = 16
NA_HD = D // NA_HEADS
NA_KH, NA_KW = 8, 16
FF = 2816
EPS = 1e-6

N_P = BATCH * SEQ
N_S = DEC_BATCH * DEC_SEQ
N_TOK = N_P + N_S
CTX_ROW = DEC_BATCH
MOD_ROWS = 8
LANE = 128
VMEM_LIMIT = 48 * 1024 * 1024
QKV_VMEM_LIMIT = 56 * 1024 * 1024
HIGHEST = lax.Precision.HIGHEST


def _cparams(sem, vmem_limit=VMEM_LIMIT):
    return pltpu.CompilerParams(dimension_semantics=sem, vmem_limit_bytes=vmem_limit)


def _mod_row(i, bm):
    return jnp.where(i < N_P // bm, CTX_ROW, (i - N_P // bm) // (DEC_SEQ // bm))


def _dot(a, b):
    return jnp.dot(a, b, preferred_element_type=F32)


def _adaln_kernel(c_ref, w_ref, b_ref, o_ref):
    cv = c_ref[...]
    s = cv * jax.nn.sigmoid(cv)
    o_ref[...] = _dot(s.astype(BF16), w_ref[...].astype(BF16)) + b_ref[...]


def adaln_all(cmat, w_ada, b_ada):
    tn = 1024
    return pl.pallas_call(
        _adaln_kernel,
        out_shape=jax.ShapeDtypeStruct((DEPTH, MOD_ROWS, 6 * D), F32),
        grid=(DEPTH, 6 * D // tn),
        in_specs=[
            pl.BlockSpec((MOD_ROWS, D), lambda l, j: (0, 0)),
            pl.BlockSpec((None, D, tn), lambda l, j: (l, 0, j)),
            pl.BlockSpec((None, 1, tn), lambda l, j: (l, 0, j)),
        ],
        out_specs=pl.BlockSpec((None, MOD_ROWS, tn), lambda l, j: (l, 0, j)),
        compiler_params=_cparams(("arbitrary", "arbitrary")),
        name="adaln",
    )(cmat, w_ada, b_ada.reshape(DEPTH, 1, 6 * D))


ROW_BM = 512
ROW_NPB = N_P // ROW_BM


def _row_specs(arr, cols):
    if isinstance(arr, tuple):
        return ([pl.BlockSpec((ROW_BM, cols), lambda i: (jnp.minimum(i, ROW_NPB - 1), 0)),
                 pl.BlockSpec((ROW_BM, cols), lambda i: (jnp.maximum(i - ROW_NPB, 0), 0))], list(arr))
    return [pl.BlockSpec((ROW_BM, cols), lambda i: (i, 0))], [arr]


def _pick_rows(refs):
    if len(refs) == 1:
        return refs[0][...]
    return jnp.where(pl.program_id(0) < ROW_NPB, refs[0][...], refs[1][...])


def _mod_spec(layer, which):
    return pl.BlockSpec((None, None, None, 1, D), lambda i: (layer, _mod_row(i, ROW_BM), which, 0, 0))


def _layer_vec_spec(layer):
    return pl.BlockSpec((None, 1, D), lambda i: (layer, 0, 0))


def _rms_mod(x, g_ref, sh_ref, sc_ref):
    y = x * lax.rsqrt(jnp.mean(x * x, axis=-1, keepdims=True) + EPS)
    return (y * g_ref[...]) * (1.0 + sc_ref[...]) + sh_ref[...]


def _out_proj_kernel(*refs, a_counts, n_x, final):
    pos = 0
    a_groups = []
    for cnt in a_counts:
        a_groups.append(refs[pos:pos + cnt])
        pos += cnt
    w_refs = refs[pos:pos + len(a_counts)]
    pos += len(a_counts)
    x_refs = refs[pos:pos + n_x]
    pos += n_x
    gt_ref, g_ref = refs[pos:pos + 2]
    pos += 2
    if not final:
        sh_ref, sc_ref = refs[pos:pos + 2]
        pos += 2
    out_a, out_b = refs[pos:pos + 2]
    wbf = refs[pos + 2:]

    @pl.when(pl.program_id(0) == 0)
    def _():
        for w_ref, wb in zip(w_refs, wbf):
            wb[...] = w_ref[...].astype(BF16)

    acc = None
    for group, wb in zip(a_groups, wbf):
        part = _dot(_pick_rows(group), wb[...])
        acc = part if acc is None else acc + part
    xn = _pick_rows(x_refs) + gt_ref[...] * acc
    if final:
        y = xn * lax.rsqrt(jnp.mean(xn * xn, axis=-1, keepdims=True) + EPS) * g_ref[...]

        @pl.when(pl.program_id(0) < ROW_NPB)
        def _():
            out_a[...] = y

        @pl.when(pl.program_id(0) >= ROW_NPB)
        def _():
            out_b[...] = y
    else:
        out_a[...] = xn
        out_b[...] = _rms_mod(xn, g_ref, sh_ref, sc_ref).astype(out_b.dtype)


def out_proj(a_list, w_list, x, mod_all, layer, gt_idx, g_all, g_layer, norm_mod_idx=None, name="out_proj"):
    final = norm_mod_idx is None
    in_specs, args, a_counts = [], [], []
    for a in a_list:
        cols = (a[0] if isinstance(a, tuple) else a).shape[1]
        specs, ops = _row_specs(a, cols)
        in_specs += specs
        args += ops
        a_counts.append(len(ops))
    w_shapes = []
    for w, block, imap in w_list:
        in_specs.append(pl.BlockSpec(block, imap, pipeline_mode=pl.Buffered(1)))
        args.append(w)
        w_shapes.append(tuple(b for b in block if b is not None))
    x_specs, x_args = _row_specs(x, D)
    in_specs += x_specs + [_mod_spec(layer, gt_idx), _layer_vec_spec(g_layer)]
    args += x_args + [mod_all, g_all.reshape(-1, 1, D)]
    if final:
        out_shape = [jax.ShapeDtypeStruct((N_P, D), F32), jax.ShapeDtypeStruct((N_S, D), F32)]
        out_specs = [pl.BlockSpec((ROW_BM, D), lambda i: (jnp.minimum(i, ROW_NPB - 1), 0)),
                     pl.BlockSpec((ROW_BM, D), lambda i: (jnp.maximum(i - ROW_NPB, 0), 0))]
    else:
        n_layer, sh_idx, sc_idx = norm_mod_idx
        in_specs += [_mod_spec(n_layer, sh_idx), _mod_spec(n_layer, sc_idx)]
        args += [mod_all, mod_all]
        out_shape = [jax.ShapeDtypeStruct((N_TOK, D), F32), jax.ShapeDtypeStruct((N_TOK, D), BF16)]
        out_specs = [pl.BlockSpec((ROW_BM, D), lambda i: (i, 0))] * 2
    return pl.pallas_call(
        functools.partial(_out_proj_kernel, a_counts=tuple(a_counts), n_x=len(x_args), final=final),
        out_shape=out_shape,
        grid=(N_TOK // ROW_BM,),
        in_specs=in_specs,
        out_specs=out_specs,
        scratch_shapes=[pltpu.VMEM(s, BF16) for s in w_shapes],
        compiler_params=_cparams(("arbitrary",)),
        name=name,
    )(*args)


def _qkv_kernel(a_ref, w_ref, o_ref, kc_ref, vc_ref, wbf):
    @pl.when(pl.program_id(0) == 0)
    def _():
        wbf[...] = w_ref[...].astype(BF16)

    a = a_ref[...]
    for part in range(3):
        acc = _dot(a, wbf[:, part * D:(part + 1) * D])
        for hh in range(NA_HEADS):
            o_ref[part, hh] = acc[:, hh * NA_HD:(hh + 1) * NA_HD].astype(o_ref.dtype)
        if part > 0:
            c_ref = kc_ref if part == 1 else vc_ref

            @pl.when(pl.program_id(0) < ROW_NPB)
            def _(acc=acc, c_ref=c_ref):
                for b in range(ROW_BM // SEQ):
                    for hh in range(NA_HEADS):
                        c_ref[b, hh] = acc[b * SEQ:(b + 1) * SEQ, hh * NA_HD:(hh + 1) * NA_HD]


def qkv_proj(h, w_in, layer):
    seqs = ROW_BM // SEQ
    cache = jax.ShapeDtypeStruct((BATCH, 1, NA_HEADS, SEQ, NA_HD), F32)
    cache_spec = pl.BlockSpec((seqs, None, NA_HEADS, SEQ, NA_HD),
                              lambda i: (jnp.minimum(i, ROW_NPB - 1), 0, 0, 0, 0))
    return pl.pallas_call(
        _qkv_kernel,
        out_shape=[jax.ShapeDtypeStruct((3, NA_HEADS, N_TOK, NA_HD), BF16), cache, cache],
        grid=(N_TOK // ROW_BM,),
        in_specs=[pl.BlockSpec((ROW_BM, D), lambda i: (i, 0)),
                  pl.BlockSpec((None, D, 3 * D), lambda i: (layer, 0, 0), pipeline_mode=pl.Buffered(1))],
        out_specs=[pl.BlockSpec((3, NA_HEADS, ROW_BM, NA_HD), lambda i: (0, 0, i, 0)), cache_spec, cache_spec],
        scratch_shapes=[pltpu.VMEM((D, 3 * D), BF16)],
        compiler_params=_cparams(("arbitrary",), QKV_VMEM_LIMIT),
        name="qkv_proj",
    )(h, w_in)


AB_MAIN = 4 * MW + 3 * HW
N_GATES = 4 * HEADS_M


def _ab_in_kernel(*refs):
    x_refs = refs[:-8]
    g_ref, sh_ref, sc_ref, w_ref, b_ref, proj_ref, gates_ref, wbf = refs[-8:]

    @pl.when(pl.program_id(0) == 0)
    def _():
        wbf[:, :4 * MW] = w_ref[:, :4 * MW].astype(BF16)
        wbf[:, 4 * MW:AB_MAIN] = w_ref[:, 4 * MW + N_GATES:].astype(BF16)
        lane = lax.broadcasted_iota(jnp.int32, (1, LANE), 1)
        wbf[:, AB_MAIN:] = jnp.where(lane < N_GATES, w_ref[:, 4 * MW:4 * MW + LANE], 0.0).astype(BF16)

    h = _rms_mod(_pick_rows(x_refs), g_ref, sh_ref, sc_ref).astype(BF16)
    proj_ref[...] = _dot(h, wbf[:, :AB_MAIN]).astype(proj_ref.dtype)
    gates_ref[...] = _dot(h, wbf[:, AB_MAIN:]) + b_ref[...]


def ab_in(x, g_all, mod_all, layer, w_in, b_gates, e):
    x_specs, x_args = _row_specs(x, D)
    b_pad = jnp.pad(b_gates, (0, LANE - N_GATES)).reshape(1, LANE)
    return pl.pallas_call(
        _ab_in_kernel,
        out_shape=[jax.ShapeDtypeStruct((N_TOK, AB_MAIN), BF16), jax.ShapeDtypeStruct((N_TOK, LANE), F32)],
        grid=(N_TOK // ROW_BM,),
        in_specs=x_specs + [_layer_vec_spec(layer), _mod_spec(layer, 0), _mod_spec(layer, 1),
                            pl.BlockSpec((None, D, w_in.shape[2]), lambda i: (e, 0, 0),
                                         pipeline_mode=pl.Buffered(1)),
                            pl.BlockSpec((1, LANE), lambda i: (0, 0))],
        out_specs=[pl.BlockSpec((ROW_BM, AB_MAIN), lambda i: (i, 0)), pl.BlockSpec((ROW_BM, LANE), lambda i: (i, 0))],
        scratch_shapes=[pltpu.VMEM((D, AB_MAIN + LANE), BF16)],
        compiler_params=_cparams(("arbitrary",)),
        name="ab_in",
    )(*x_args, g_all.reshape(-1, 1, D), mod_all, mod_all, w_in, b_pad)


SEQ_BLOCK = DEC_SEQ
N_PBLK = N_P // SEQ_BLOCK


def _seq_edges(rows, is_prompt):
    r = lax.broadcasted_iota(jnp.int32, (rows, 1), 0)
    first = (r == 0) | (is_prompt & (r % SEQ == 0))
    last = (r == rows - 1) | (is_prompt & (r % SEQ == SEQ - 1))
    return first, last


def _dwconv3(x, w_ref, first, last):
    rows = x.shape[0]
    prev = jnp.where(first, 0.0, pltpu.roll(x, 1, 0))
    nxt = jnp.where(last, 0.0, pltpu.roll(x, rows - 1, 0))
    return prev * w_ref[0:1, :] + x * w_ref[1:2, :] + nxt * w_ref[2:3, :]


def _gated_gelu_of_half(y, g):
    c = math.sqrt(2.0 / math.pi)
    t = jnp.tanh(y * (2.0 * c + (8.0 * 0.044715 * c) * (y * y)))
    return (y + y * t) * g


def _ffn_up_kernel(h_ref, wa_ref, wg_ref, wc_ref, o_ref):
    first, last = _seq_edges(SEQ_BLOCK, pl.program_id(0) < N_PBLK)
    h = h_ref[...]
    a = _dot(h, wa_ref[...].astype(BF16))
    g = _dot(h, wg_ref[...].astype(BF16))
    half_conv = _dwconv3(a, 0.5 * wc_ref[...], first, last)
    o_ref[...] = _gated_gelu_of_half(half_conv, g).astype(o_ref.dtype)


def ffn_up(h, w_up, w_conv, layer):
    tc = 256
    nct = FF // tc
    return pl.pallas_call(
        _ffn_up_kernel,
        out_shape=jax.ShapeDtypeStruct((N_TOK, FF), BF16),
        grid=(N_TOK // SEQ_BLOCK, nct),
        in_specs=[
            pl.BlockSpec((SEQ_BLOCK, D), lambda i, j: (i, 0)),
            pl.BlockSpec((None, D, tc), lambda i, j: (layer, 0, j)),
            pl.BlockSpec((None, D, tc), lambda i, j: (layer, 0, j + nct)),
            pl.BlockSpec((None, 3, tc), lambda i, j: (layer, 0, j)),
        ],
        out_specs=pl.BlockSpec((SEQ_BLOCK, tc), lambda i, j: (i, j)),
        compiler_params=_cparams(("arbitrary", "arbitrary")),
        name="ffn_up",
    )(h, w_up, w_up, w_conv)


def _ctx_attn_kernel(q_ref, k_ref, v_ref, o_ref, *, heads):
    outs = []
    for hh in range(heads):
        q = q_ref[hh].astype(BF16)
        k = k_ref[hh].astype(BF16)
        s = lax.dot_general(q, k, (((1,), (1,)), ((), ())), preferred_element_type=F32) * (NA_HD ** -0.5)
        m = jnp.max(s, axis=-1, keepdims=True)
        p = jnp.exp(s - m)
        l = jnp.sum(p, axis=-1, keepdims=True)
        outs.append(_dot(p.astype(BF16), v_ref[hh].astype(BF16)) / l)
    o_ref[...] = jnp.concatenate(outs, axis=-1).astype(o_ref.dtype)


def ctx_attention(qkv):
    heads = 8
    spec = lambda part: pl.BlockSpec((None, heads, SEQ, NA_HD), lambda b, h: (part, h, b, 0))
    return pl.pallas_call(
        functools.partial(_ctx_attn_kernel, heads=heads),
        out_shape=jax.ShapeDtypeStruct((N_P, D), BF16),
        grid=(BATCH, NA_HEADS // heads),
        in_specs=[spec(0), spec(1), spec(2)],
        out_specs=pl.BlockSpec((SEQ, heads * NA_HD), lambda b, h: (b, h)),
        compiler_params=_cparams(("arbitrary", "arbitrary")),
        name="ctx_attn",
    )(qkv, qkv, qkv)


def _na_tables():
    q = np.arange(GRID_W)[:, None]
    w = np.arange(GRID_W)[None, :]
    idx_c = np.clip(w - q + (NA_KW - 1), 0, 2 * NA_KW - 2)
    onehot = (idx_c.reshape(1, -1) == np.arange(32)[:, None]).astype(np.float32)
    c_start = np.clip(np.arange(GRID_W) - NA_KW // 2, 0, GRID_W - NA_KW)[:, None]
    inside = (w >= c_start) & (w < c_start + NA_KW)
    cmask = np.where(inside, 0.0, -np.inf).astype(np.float32)
    return onehot, np.tile(cmask, (1, 2))


def _rpb_expand_kernel(r_ref, e_ref, o_ref):
    o_ref[...] = jnp.dot(r_ref[...], e_ref[...], precision=HIGHEST, preferred_element_type=F32)


def rpb_expand(rpb):
    onehot, _ = _na_tables()
    rp = jnp.pad(rpb, ((0, 0), (0, 1), (0, 1)))
    return pl.pallas_call(
        _rpb_expand_kernel,
        out_shape=jax.ShapeDtypeStruct((NA_HEADS, 16, GRID_W * GRID_W), F32),
        grid=(NA_HEADS,),
        in_specs=[pl.BlockSpec((None, 16, 32), lambda h: (h, 0, 0)),
                  pl.BlockSpec((32, GRID_W * GRID_W), lambda h: (0, 0))],
        out_specs=pl.BlockSpec((None, 16, GRID_W * GRID_W), lambda h: (h, 0, 0)),
        compiler_params=_cparams(("arbitrary",)),
        name="rpb_expand",
    )(rp, jnp.asarray(onehot))


NA_QROWS = 8
NA_WIN = 2 * NA_QROWS


def _na_attn_kernel(q_ref, k_ref, v_ref, kc_ref, vc_ref, t_ref, o_ref, p_loc, p_ctx, *, heads):
    nq = NA_QROWS * GRID_W
    nk = NA_WIN * GRID_W
    pair = 2 * GRID_W
    nt = (((1,), (1,)), ((), ()))
    lane = lax.broadcasted_iota(jnp.int32, (GRID_W, pair), 1)
    zero_tile = jnp.zeros((GRID_W, pair), BF16)
    for hh in range(heads):
        kc = kc_ref[hh].astype(BF16)
        vc = vc_ref[hh].astype(BF16)
        for blk in range(GRID_R // NA_QROWS):
            k0 = min(max(NA_QROWS * blk - NA_KH // 2, 0), GRID_R - NA_WIN)
            q = (q_ref[hh, blk * nq:(blk + 1) * nq, :] * (NA_HD ** -0.5)).astype(BF16)
            kw = k_ref[hh, k0 * GRID_W:k0 * GRID_W + nk, :].astype(BF16)
            vw = v_ref[hh, k0 * GRID_W:k0 * GRID_W + nk, :].astype(BF16)
            s_loc = lax.dot_general(q, kw, nt, preferred_element_type=F32)
            s_ctx = lax.dot_general(q, kc, nt, preferred_element_type=F32)
            denoms = []
            for qi in range(NA_QROWS):
                r = NA_QROWS * blk + qi
                r_start = min(max(r - NA_KH // 2, 0), GRID_R - NA_KH)
                rows = slice(qi * GRID_W, (qi + 1) * GRID_W)
                ctx_tiles = [s_ctx[rows, c * pair:(c + 1) * pair] for c in range(PAST_LEN // pair)]
                tiles = {}
                for j in range(nk // pair):
                    kr = k0 + 2 * j
                    ok0 = r_start <= kr < r_start + NA_KH
                    ok1 = r_start <= kr + 1 < r_start + NA_KH
                    if not (ok0 or ok1):
                        continue
                    sb = s_loc[rows, j * pair:(j + 1) * pair] + t_ref[hh, kr - r + NA_KH]
                    if not (ok0 and ok1):
                        sb = jnp.where((lane < GRID_W) if ok0 else (lane >= GRID_W), sb, -jnp.inf)
                    tiles[j] = sb
                mx = functools.reduce(jnp.maximum, list(tiles.values()) + ctx_tiles)
                m = jnp.max(mx, axis=1, keepdims=True)
                acc = None
                for j in range(nk // pair):
                    if j in tiles:
                        p = jnp.exp(tiles[j] - m)
                        acc = p if acc is None else acc + p
                        p_loc[rows, j * pair:(j + 1) * pair] = p.astype(BF16)
                    else:
                        p_loc[rows, j * pair:(j + 1) * pair] = zero_tile
                for c, t in enumerate(ctx_tiles):
                    p = jnp.exp(t - m)
                    acc = acc + p
                    p_ctx[rows, c * pair:(c + 1) * pair] = p.astype(BF16)
                denoms.append(jnp.sum(acc, axis=1, keepdims=True))
            o = (_dot(p_loc[...], vw) + _dot(p_ctx[...], vc)) / jnp.concatenate(denoms, axis=0)
            o_ref[blk * nq:(blk + 1) * nq, hh * NA_HD:(hh + 1) * NA_HD] = o.astype(o_ref.dtype)


def na_attention(qkv, k_ctx, v_ctx, bias_pairs):
    heads = 2
    blk0 = N_P // DEC_SEQ
    spec = lambda part: pl.BlockSpec((None, heads, DEC_SEQ, NA_HD), lambda b, h: (part, h, b + blk0, 0))
    cspec = pl.BlockSpec((None, heads, PAST_LEN, NA_HD), lambda b, h: (b, h, 0, 0))
    return pl.pallas_call(
        functools.partial(_na_attn_kernel, heads=heads),
        out_shape=jax.ShapeDtypeStruct((N_S, D), BF16),
        grid=(DEC_BATCH, NA_HEADS // heads),
        in_specs=[spec(0), spec(1), spec(2), cspec, cspec,
                  pl.BlockSpec((heads, 16, GRID_W, 2 * GRID_W), lambda b, h: (h, 0, 0, 0))],
        out_specs=pl.BlockSpec((DEC_SEQ, heads * NA_HD), lambda b, h: (b, h)),
        scratch_shapes=[pltpu.VMEM((NA_QROWS * GRID_W, NA_WIN * GRID_W), BF16),
                        pltpu.VMEM((NA_QROWS * GRID_W, PAST_LEN), BF16)],
        compiler_params=_cparams(("arbitrary", "arbitrary")),
        name="na_attn",
    )(qkv, qkv, qkv, k_ctx, v_ctx, bias_pairs)


def mixer_c(h, k_ctx, v_ctx, w_in_all, layer, rpb):
    qkv, k_new, v_new = qkv_proj(h, w_in_all, layer)
    o_p = ctx_attention(qkv)
    _, cmask2 = _na_tables()
    b15 = rpb_expand(rpb).reshape(NA_HEADS, 16, GRID_W, GRID_W)
    b17 = jnp.pad(b15, ((0, 0), (1, 0), (0, 0), (0, 0)))
    bias_pairs = jnp.concatenate([b17[:, :16], b17[:, 1:]], axis=-1) + jnp.asarray(cmask2)
    o_s = na_attention(qkv, k_ctx, v_ctx, bias_pairs)
    return (o_p, o_s), k_new, v_new


SCAN_BLOCK = HD_M


def _mlstm_kernel(*refs, seq, zero_state, emit_state):
    q_ref, k_ref, v_ref, og_ref, gates_ref, wq_ref, wk_ref, gh_ref = refs[:8]
    pos = 8
    if not zero_state:
        c0_ref, n0_ref, m0_ref = refs[pos:pos + 3]
        pos += 3
    y_ref = refs[pos]
    pos += 1
    if emit_state:
        cn_ref, nn_ref, mn_ref = refs[pos:pos + 3]
        pos += 3
    kv_s, ks_s, be_s, mk_s, cp_s, np_s, mp_s, cst, nst, mst = refs[pos:]

    head = pl.program_id(1)
    blk = SCAN_BLOCK
    nc = seq // blk
    r = lax.broadcasted_iota(jnp.int32, (seq, 1), 0)
    first, last = r == 0, r == seq - 1
    qc_all = _dwconv3(q_ref[...].astype(F32), wq_ref, first, last)
    q3 = (qc_all * jax.nn.sigmoid(qc_all)).reshape(nc, blk, HD_M)
    kc_all = _dwconv3(k_ref[...].astype(F32), wk_ref, first, last)
    k3 = (kc_all * jax.nn.sigmoid(kc_all) * (HD_M ** -0.5)).reshape(nc, blk, HD_M)
    qb, kb = q3.astype(BF16), k3.astype(BF16)
    vb = v_ref[...].astype(BF16).reshape(nc, blk, HD_M)

    g_all = gates_ref[...]
    lf = jax.nn.log_sigmoid(g_all)
    rin = r % blk
    pre, suf = lf, lf
    for sh in [1 << i for i in range(blk.bit_length() - 1)]:
        pre = pre + jnp.where(rin >= sh, pltpu.roll(pre, sh, 0), 0.0)
        suf = suf + jnp.where(rin < blk - sh, pltpu.roll(suf, seq - sh, 0), 0.0)
    g3 = g_all.reshape(nc, blk, LANE)
    gt3 = jnp.swapaxes(g3, 1, 2)[:, :4 * HEADS_M, :]

    if zero_state:
        cst[...] = jnp.zeros_like(cst)
        nst[...] = jnp.zeros_like(nst)
        mst[...] = jnp.zeros_like(mst)
    else:
        cst[...] = c0_ref[...]
        nst[...] = n0_ref[...]
        mst[...] = jnp.broadcast_to(m0_ref[...], mst.shape)

    tt = lax.broadcasted_iota(jnp.int32, (1, blk, blk), 1)
    ss = lax.broadcasted_iota(jnp.int32, (1, blk, blk), 2)
    lane = lax.broadcasted_iota(jnp.int32, (1, 1, LANE), 2)
    sub = lax.broadcasted_iota(jnp.int32, (1, 4 * HEADS_M, 1), 1)
    hsum = None
    for d in range(2):
        i_idx = d * 2 * HEADS_M + head
        f_idx = i_idx + HEADS_M
        mask = (ss <= tt) if d == 0 else (ss >= tt)
        b3 = (pre if d == 0 else suf).reshape(nc, blk, LANE)
        bt3 = jnp.swapaxes(b3, 1, 2)[:, :4 * HEADS_M, :]
        bcol = jnp.sum(jnp.where(lane == f_idx, b3, 0.0), axis=2, keepdims=True)
        icol = jnp.sum(jnp.where(lane == i_idx, g3, 0.0), axis=2, keepdims=True)
        brow = jnp.sum(jnp.where(sub == f_idx, bt3, 0.0), axis=1, keepdims=True)
        irow = jnp.sum(jnp.where(sub == i_idx, gt3, 0.0), axis=1, keepdims=True)
        bend = bcol[:, blk - 1:blk, :] if d == 0 else bcol[:, 0:1, :]

        dmat = jnp.where(mask, bcol - brow + irow, -jnp.inf)
        mloc = jnp.max(dmat, axis=2, keepdims=True)
        qk = jnp.einsum('ctd,csd->cts', qb, kb, preferred_element_type=F32)
        s_loc = jnp.exp(dmat - mloc) * qk
        num_loc = jnp.einsum('cts,csd->ctd', s_loc.astype(BF16), vb, preferred_element_type=F32)
        den_loc = jnp.sum(s_loc, axis=2, keepdims=True)
        to_end = bend - bcol + icol
        mk = jnp.max(to_end, axis=1, keepdims=True)
        kw = k3 * jnp.exp(to_end - mk)
        kv_s[...] = jnp.einsum('cds,cse->cde', jnp.swapaxes(kw, 1, 2).astype(BF16), vb,
                               preferred_element_type=F32)
        ks_s[...] = jnp.sum(kw, axis=1, keepdims=True)
        be_s[...] = jnp.broadcast_to(bend, be_s.shape)
        mk_s[...] = jnp.broadcast_to(mk, mk_s.shape)

        def step(j, carry, d=d):
            c = j if d == 0 else nc - 1 - j
            m_prev, c_prev, n_prev = mst[d], cst[d], nst[d]
            cp_s[c] = c_prev.astype(BF16)
            np_s[c] = n_prev
            mp_s[c] = m_prev
            be, mkc = be_s[c], mk_s[c]
            m_new = jnp.maximum(be + m_prev, mkc)
            keep = jnp.exp(be + m_prev - m_new)
            add = jnp.exp(mkc - m_new)
            cst[d] = keep * c_prev + add * kv_s[c]
            nst[d] = keep * n_prev + add * ks_s[c]
            mst[d] = m_new
            return carry

        lax.fori_loop(0, nc, step, 0)

        m_inter = bcol + mp_s[...][:, :, 0:1]
        m_t = jnp.maximum(m_inter, mloc)
        w_state = jnp.exp(m_inter - m_t)
        w_loc = jnp.exp(mloc - m_t)
        inter = jnp.einsum('ctd,cde->cte', qb, cp_s[...], preferred_element_type=F32)
        num = w_state * inter + w_loc * num_loc
        den = w_state * jnp.sum(q3 * np_s[...], axis=2, keepdims=True) + w_loc * den_loc
        h = num / jnp.maximum(jnp.abs(den), jnp.exp(-m_t))
        hsum = h if hsum is None else hsum + h

    hsum = hsum.reshape(seq, HD_M)
    hn = hsum * lax.rsqrt(jnp.mean(hsum * hsum, axis=-1, keepdims=True) + EPS) * gh_ref[...]
    y_ref[...] = (hn * jax.nn.sigmoid(og_ref[...].astype(F32))).astype(y_ref.dtype)
    if emit_state:
        cn_ref[...] = cst[...]
        nn_ref[...] = nst[...]
        mn_ref[...] = mst[...]


def mlstm(proj, gates, w_conv_qk, g_head, state, *, nb, seq, row0):
    blk0 = row0 // seq
    nblk = seq // SCAN_BLOCK
    zero_state = state is None
    col = lambda part: pl.BlockSpec((seq, HD_M), lambda b, h: (b + blk0, part * HEADS_M + h))
    in_specs = [col(0), col(1), col(2), col(3),
                pl.BlockSpec((seq, LANE), lambda b, h: (b + blk0, 0)),
                pl.BlockSpec((3, HD_M), lambda b, h: (0, h)),
                pl.BlockSpec((3, HD_M), lambda b, h: (0, HEADS_M + h)),
                pl.BlockSpec((1, HD_M), lambda b, h: (0, h))]
    args = [proj, proj, proj, proj, gates, w_conv_qk, w_conv_qk, g_head.reshape(1, MW)]
    if not zero_state:
        c0, n0, m0 = state
        in_specs += [pl.BlockSpec((None, 2, None, HD_M, HD_M), lambda b, h: (b, 0, h, 0, 0)),
                     pl.BlockSpec((None, 2, None, 1, HD_M), lambda b, h: (b, 0, h, 0, 0)),
                     pl.BlockSpec((None, 2, None, 1, 1), lambda b, h: (b, 0, h, 0, 0))]
        args += [c0, n0.reshape(nb, 2, HEADS_M, 1, HD_M), m0.reshape(nb, 2, HEADS_M, 1, 1)]
    out_shape = [jax.ShapeDtypeStruct((nb * seq, MW), BF16)]
    out_specs = [pl.BlockSpec((seq, HD_M), lambda b, h: (b, h))]
    if zero_state:
        out_shape += [jax.ShapeDtypeStruct((nb, 2, HEADS_M, HD_M, HD_M), F32),
                      jax.ShapeDtypeStruct((nb, 2, HEADS_M, 1, HD_M), F32),
                      jax.ShapeDtypeStruct((nb, 2, HEADS_M, 1, LANE), F32)]
        out_specs += [pl.BlockSpec((None, 2, None, HD_M, HD_M), lambda b, h: (b, 0, h, 0, 0)),
                      pl.BlockSpec((None, 2, None, 1, HD_M), lambda b, h: (b, 0, h, 0, 0)),
                      pl.BlockSpec((None, 2, None, 1, LANE), lambda b, h: (b, 0, h, 0, 0))]
    return pl.pallas_call(
        functools.partial(_mlstm_kernel, seq=seq, zero_state=zero_state, emit_state=zero_state),
        out_shape=out_shape,
        grid=(nb, HEADS_M),
        in_specs=in_specs,
        out_specs=out_specs,
        scratch_shapes=[pltpu.VMEM((nblk, HD_M, HD_M), F32), pltpu.VMEM((nblk, 1, HD_M), F32),
                        pltpu.VMEM((nblk, 1, LANE), F32), pltpu.VMEM((nblk, 1, LANE), F32),
                        pltpu.VMEM((nblk, HD_M, HD_M), BF16), pltpu.VMEM((nblk, 1, HD_M), F32),
                        pltpu.VMEM((nblk, 1, LANE), F32),
                        pltpu.VMEM((2, HD_M, HD_M), F32), pltpu.VMEM((2, 1, HD_M), F32),
                        pltpu.VMEM((2, 1, LANE), F32)],
        compiler_params=_cparams(("arbitrary", "arbitrary")),
        name="mlstm_%d" % seq,
    )(*args)


def _hyena_pre_kernel(v_ref, x1_ref, x2_ref, wv_ref, w1_ref, w2_ref, u_ref, x2c_ref):
    first, last = _seq_edges(SEQ_BLOCK, pl.program_id(0) < N_PBLK)
    x1c = _dwconv3(x1_ref[...].astype(F32), w1_ref, first, last)
    u_ref[...] = x1c * _dwconv3(v_ref[...].astype(F32), wv_ref, first, last)
    x2c_ref[...] = _dwconv3(x2_ref[...].astype(F32), w2_ref, first, last)


def hyena_pre(proj, w_conv_hy):
    tc = 256
    nct = HW // tc
    c0 = 4 * MW // tc
    pcol = lambda part: pl.BlockSpec((SEQ_BLOCK, tc), lambda i, j: (i, c0 + part * nct + j))
    wcol = lambda part: pl.BlockSpec((3, tc), lambda i, j: (0, part * nct + j))
    out = jax.ShapeDtypeStruct((N_TOK, HW), F32)
    ospec = pl.BlockSpec((SEQ_BLOCK, tc), lambda i, j: (i, j))
    return pl.pallas_call(
        _hyena_pre_kernel,
        out_shape=[out, out],
        grid=(N_TOK // SEQ_BLOCK, nct),
        in_specs=[pcol(0), pcol(1), pcol(2), wcol(0), wcol(1), wcol(2)],
        out_specs=[ospec, ospec],
        compiler_params=_cparams(("arbitrary", "arbitrary")),
        name="hyena_pre",
    )(proj, proj, proj, w_conv_hy, w_conv_hy, w_conv_hy)


@functools.lru_cache(maxsize=None)
def _filter_tables(seq):
    t = np.linspace(0.0, 1.0, seq)[:, None]
    wpos = 2.0 * np.pi * np.arange(seq)[:, None] / seq
    bands = np.linspace(1e-4, N_BANDS - 1, N_BANDS)[None, :]
    z = np.concatenate([t, np.cos(bands * wpos), -np.sin(bands * wpos)], axis=-1)
    z = np.pad(z, ((0, 0), (0, LANE - FILTER_EMB)))
    max_decay = math.log(DECAY_TARGET) / DECAY_FAST
    min_decay = math.log(DECAY_TARGET) / DECAY_SLOW
    deltas = np.abs(np.linspace(min_decay, max_decay, HW))
    decay = np.exp(-t * np.concatenate([deltas, deltas])[None, :])
    return z.astype(np.float32), decay.astype(np.float32)


@functools.lru_cache(maxsize=None)
def _dft_tables(seq, tk):
    n = 2 * seq
    k = np.arange(seq)[:, None]
    t = np.arange(seq)[None, :]
    ang = 2.0 * np.pi * ((k * t) % n) / n
    alt = np.where(np.arange(seq) % 2 == 0, 1.0, -1.0)
    cm, sm = np.cos(ang), np.sin(ang)
    sm[0, :] = alt
    fwd = np.stack([cm.reshape(seq // tk, tk, seq), sm.reshape(seq // tk, tk, seq)], axis=1)
    wk = np.where(np.arange(seq) == 0, 1.0, 2.0)[None, :]
    ci = (np.cos(ang.T) * wk) / n
    si = np.sin(ang.T) * 2.0 / n
    si[:, 0] = alt / n
    inv = np.concatenate([ci, si], axis=1)
    return fwd.astype(np.float32), inv.astype(np.float32)


def _filter_kernel(z_ref, w1_ref, b1_ref, w2_ref, b2_ref, w3_ref, fr_ref, dec_ref, hs_ref, hd_ref):
    fr = fr_ref[...]
    hp = functools.partial(jnp.dot, precision=HIGHEST, preferred_element_type=F32)
    h1 = jnp.sin(fr * (hp(z_ref[...], w1_ref[...]) + b1_ref[...]))
    h2 = jnp.sin(fr * (hp(h1, w2_ref[...]) + b2_ref[...]))
    filt = hp(h2, w3_ref[...]) * dec_ref[...]
    past, fut = filt[:, :HW], filt[:, HW:]
    rows = filt.shape[0]
    grow = lax.broadcasted_iota(jnp.int32, (rows, 1), 0) + pl.program_id(0) * rows
    fut = jnp.where(grow == 0, 0.0, fut)
    hs_ref[...] = past + fut
    hd_ref[...] = past - fut


def filter_gen(seq, w1, b1, w2, b2, w3, freq):
    z, decay = _filter_tables(seq)
    tl = 256
    fh = FILTER_HIDDEN
    full = lambda shape: pl.BlockSpec(shape, lambda i: (0, 0))
    out = jax.ShapeDtypeStruct((seq, HW), F32)
    return pl.pallas_call(
        _filter_kernel,
        out_shape=[out, out],
        grid=(seq // tl,),
        in_specs=[pl.BlockSpec((tl, LANE), lambda i: (i, 0)), full((LANE, fh)), full((1, fh)), full((fh, fh)),
                  full((1, fh)), full((fh, 2 * HW)), full((1, fh)), pl.BlockSpec((tl, 2 * HW), lambda i: (i, 0))],
        out_specs=[pl.BlockSpec((tl, HW), lambda i: (i, 0))] * 2,
        compiler_params=_cparams(("arbitrary",)),
        name="filter_gen",
    )(jnp.asarray(z), jnp.pad(w1, ((0, LANE - FILTER_EMB), (0, 0))), b1.reshape(1, fh), w2, b2.reshape(1, fh), w3,
      freq.reshape(1, fh), jnp.asarray(decay))


def _dft_filter_kernel(a_ref, hs_ref, hd_ref, k_ref, hs_bf, hd_bf):
    @pl.when(pl.program_id(0) == 0)
    def _():
        hs_bf[...] = hs_ref[...].astype(BF16)
        hd_bf[...] = hd_ref[...].astype(BF16)

    k_ref[0] = _dot(a_ref[0], hs_bf[...])
    k_ref[1] = _dot(a_ref[1], hd_bf[...])

    @pl.when(pl.program_id(0) == 0)
    def _():
        k_ref[1, 0:1, :] = _dot(a_ref[1, 0:8, :], hs_bf[...])[0:1, :]


def dft_filter(seq, tk, hs, hd):
    fwd, _ = _dft_tables(seq, tk)
    return pl.pallas_call(
        _dft_filter_kernel,
        out_shape=jax.ShapeDtypeStruct((2, seq, HW), F32),
        grid=(seq // tk,),
        in_specs=[pl.BlockSpec((None, 2, tk, seq), lambda m: (m, 0, 0, 0)),
                  pl.BlockSpec((seq, HW), lambda m: (0, 0)), pl.BlockSpec((seq, HW), lambda m: (0, 0))],
        out_specs=pl.BlockSpec((2, tk, HW), lambda m: (0, m, 0)),
        scratch_shapes=[pltpu.VMEM((seq, HW), BF16), pltpu.VMEM((seq, HW), BF16)],
        compiler_params=_cparams(("arbitrary",)),
        name="dft_filter",
    )(jnp.asarray(fwd).astype(BF16), hs, hd)


def _dft_fwd_kernel(a_ref, u_ref, k_ref, y_ref, u_bf):
    m = pl.program_id(1)

    @pl.when(m == 0)
    def _():
        u_bf[...] = u_ref[...].astype(BF16)

    ure = _dot(a_ref[0], u_bf[...])
    uim = _dot(a_ref[1], u_bf[...])
    kre, kim = k_ref[0], k_ref[1]
    packed = (lax.broadcasted_iota(jnp.int32, (ure.shape[0], 1), 0) == 0) & (m == 0)
    y_ref[0] = jnp.where(packed, ure * kre, ure * kre - uim * kim).astype(y_ref.dtype)
    y_ref[1] = jnp.where(packed, uim * kim, ure * kim + uim * kre).astype(y_ref.dtype)


def dft_fwd(seq, tk, u, kf, *, nb, row0):
    fwd, _ = _dft_tables(seq, tk)
    blk0 = row0 // seq
    return pl.pallas_call(
        _dft_fwd_kernel,
        out_shape=jax.ShapeDtypeStruct((nb, 2, seq, HW), BF16),
        grid=(nb, seq // tk),
        in_specs=[pl.BlockSpec((None, 2, tk, seq), lambda b, m: (m, 0, 0, 0)),
                  pl.BlockSpec((seq, HW), lambda b, m: (b + blk0, 0)),
                  pl.BlockSpec((2, tk, HW), lambda b, m: (0, m, 0))],
        out_specs=pl.BlockSpec((None, 2, tk, HW), lambda b, m: (b, 0, m, 0)),
        scratch_shapes=[pltpu.VMEM((seq, HW), BF16)],
        compiler_params=_cparams(("arbitrary", "arbitrary")),
        name="dft_fwd",
    )(jnp.asarray(fwd).astype(BF16), u, kf)


def _dft_inv_kernel(a_ref, y_ref, u_ref, x2_ref, bias_ref, o_ref, *, seq):
    conv = _dot(a_ref[:, :seq], y_ref[0]) + _dot(a_ref[:, seq:], y_ref[1])
    o_ref[...] = (x2_ref[...] * (conv + bias_ref[...] * u_ref[...])).astype(o_ref.dtype)


def dft_inv(seq, tk, y, u, x2c, bias, *, nb, row0):
    _, inv = _dft_tables(seq, tk)
    tm = min(seq, 512)
    nt = seq // tm
    blk0 = row0 // tm
    rows = lambda b, t: (b * nt + t + blk0, 0)
    return pl.pallas_call(
        functools.partial(_dft_inv_kernel, seq=seq),
        out_shape=jax.ShapeDtypeStruct((nb * seq, HW), BF16),
        grid=(nb, nt),
        in_specs=[pl.BlockSpec((tm, 2 * seq), lambda b, t: (t, 0)),
                  pl.BlockSpec((None, 2, seq, HW), lambda b, t: (b, 0, 0, 0)),
                  pl.BlockSpec((tm, HW), rows), pl.BlockSpec((tm, HW), rows),
                  pl.BlockSpec((1, HW), lambda b, t: (0, 0))],
        out_specs=pl.BlockSpec((tm, HW), lambda b, t: (b * nt + t, 0)),
        compiler_params=_cparams(("arbitrary", "arbitrary")),
        name="dft_inv",
    )(jnp.asarray(inv).astype(BF16), y, u, x2c, bias.reshape(1, HW))


def hyena_group(seq, u, x2c, filt_w, bias, *, nb, row0):
    tk = min(seq, 512)
    hs, hd = filter_gen(seq, *filt_w)
    kf = dft_filter(seq, tk, hs, hd)
    y = dft_fwd(seq, tk, u, kf, nb=nb, row0=row0)
    return dft_inv(seq, tk, y, u, x2c, bias, nb=nb, row0=row0)


def mixer_ab_parts(proj, gates, state_s, w_conv_qk, g_head, w_conv_hy, w_f1, b_f1, w_f2, b_f2, w_f3, freq, hy_bias):
    ym_p, c_p, n_p, m_p = mlstm(proj, gates, w_conv_qk, g_head, None, nb=BATCH, seq=SEQ, row0=0)
    (ym_s,) = mlstm(proj, gates, w_conv_qk, g_head, state_s, nb=DEC_BATCH, seq=DEC_SEQ, row0=N_P)
    u, x2c = hyena_pre(proj, w_conv_hy)
    filt_w = (w_f1, b_f1, w_f2, b_f2, w_f3, freq)
    yh_p = hyena_group(SEQ, u, x2c, filt_w, hy_bias, nb=BATCH, row0=0)
    yh_s = hyena_group(DEC_SEQ, u, x2c, filt_w, hy_bias, nb=DEC_BATCH, row0=N_P)
    state_p = (c_p, n_p[:, :, :, 0, :], m_p[:, :, :, 0, 0])
    return (ym_p, ym_s), (yh_p, yh_s), state_p


def kernel(x_prompt, x_sample, state_mlstm_C, state_mlstm_n, state_mlstm_m, cache_na_k, cache_na_v, c, c_ctx, w_ada, b_ada, g_mix, g_ffn, g_final, w_in_ab, b_gates, w_conv_qk, g_mlstm, w_conv_hy, w_filt1, b_filt1, w_filt2, b_filt2, w_filt3, filt_freq, hyena_bias, w_out_ab, w_in_c, rpb_c, w_out_c, w_up, w_conv_ffn, w_down):
    cmat = jnp.concatenate([c, c_ctx[None, :], jnp.zeros((MOD_ROWS - DEC_BATCH - 1, D), F32)], axis=0)
    mod_all = adaln_all(cmat, w_ada, b_ada).reshape(DEPTH, MOD_ROWS, 6, 1, D)
    x = (x_prompt.reshape(N_P, D), x_sample.reshape(N_S, D))
    h = None
    new_c, new_n, new_m, new_k, new_v = [], [], [], [], []
    for l in range(DEPTH):
        e = l // 2
        if l % 2 == 0:
            assert l == 0, "a later mixer A/B layer would take the fused norm of the layer before it"
            proj, gates = ab_in(x, g_mix, mod_all, l, w_in_ab, b_gates[e], e)
            state_s = (state_mlstm_C[:, e], state_mlstm_n[:, e], state_mlstm_m[:, e])
            y_m, y_h, (c_p, n_p, m_p) = mixer_ab_parts(
                proj, gates, state_s, w_conv_qk[e], g_mlstm[e], w_conv_hy[e], w_filt1[e], b_filt1[e],
                w_filt2[e], b_filt2[e], w_filt3[e], filt_freq[e], hyena_bias[e])
            a_list = [y_m, y_h]
            w_list = [(w_out_ab, (None, MW, D), lambda i, e=e: (e, 0, 0)),
                      (w_out_ab, (None, HW, D), lambda i, e=e: (e, 1, 0))]
            new_c.append(c_p[:, None])
            new_n.append(n_p[:, None])
            new_m.append(m_p[:, None])
        else:
            o, k_new, v_new = mixer_c(h, cache_na_k[:, e], cache_na_v[:, e], w_in_c, e, rpb_c[e])
            a_list = [o]
            w_list = [(w_out_c, (None, D, D), lambda i, e=e: (e, 0, 0))]
            new_k.append(k_new)
            new_v.append(v_new)
        x, h = out_proj(a_list, w_list, x, mod_all, l, 2, g_ffn, l, (l, 3, 4), name="mixer_out")
        mid = ffn_up(h, w_up, w_conv_ffn, l)
        w_list = [(w_down, (None, FF, D), lambda i, l=l: (l, 0, 0))]
        if l + 1 < DEPTH:
            x, h = out_proj([mid], w_list, x, mod_all, l, 5, g_mix, l + 1, (l + 1, 0, 1), name="ffn_down")
        else:
            y_p, y_s = out_proj([mid], w_list, x, mod_all, l, 5, g_final, 0, None, name="ffn_down_final")
    cat = lambda parts: parts[0] if len(parts) == 1 else jnp.concatenate(parts, axis=1)
    return (y_p.reshape(BATCH, SEQ, D), y_s.reshape(DEC_BATCH, DEC_SEQ, D), cat(new_c), cat(new_n), cat(new_m),
            cat(new_k), cat(new_v))
```

```python
import functools
import math

import numpy as np
import jax
import jax.numpy as jnp
from jax import lax
from jax.experimental import pallas as pl
from jax.experimental.pallas import tpu as pltpu

F32 = jnp.float32
BF16 = jnp.bfloat16

D = 1024
BATCH, SEQ = 16, 256
DEC_BATCH, DEC_SEQ = 4, 2048
PAST_LEN = 512
DEPTH = 2
GRID_W = 64
GRID_R = DEC_SEQ // GRID_W
HEADS_M = 4
MW = D // 2
HD_M = MW // HEADS_M
CHUNK = 64
HW = D // 2
N_BANDS = 16
FILTER_EMB = 2 * N_BANDS + 1
FILTER_HIDDEN = 64
DECAY_FAST, DECAY_SLOW, DECAY_TARGET = 0.3, 1.5, 1e-2
NA_HEADS = 16
NA_HD = D // NA_HEADS
NA_KH, NA_KW = 8, 16
FF = 2816
EPS = 1e-6

N_P = BATCH * SEQ
N_S = DEC_BATCH * DEC_SEQ
N_TOK = N_P + N_S
CTX_ROW = DEC_BATCH
MOD_ROWS = 8
LANE = 128
VMEM_LIMIT = 48 * 1024 * 1024
QKV_VMEM_LIMIT = 56 * 1024 * 1024
HIGHEST = lax.Precision.HIGHEST


def _cparams(sem, vmem_limit=VMEM_LIMIT):
    return pltpu.CompilerParams(dimension_semantics=sem, vmem_limit_bytes=vmem_limit)


def _mod_row(i, bm):
    return jnp.where(i < N_P // bm, CTX_ROW, (i - N_P // bm) // (DEC_SEQ // bm))


def _dot(a, b):
    return jnp.dot(a, b, preferred_element_type=F32)


def _adaln_kernel(c_ref, w_ref, b_ref, o_ref):
    cv = c_ref[...]
    s = cv * jax.nn.sigmoid(cv)
    o_ref[...] = _dot(s.astype(BF16), w_ref[...].astype(BF16)) + b_ref[...]


def adaln_all(cmat, w_ada, b_ada):
    tn = 1024
    return pl.pallas_call(
        _adaln_kernel,
        out_shape=jax.ShapeDtypeStruct((DEPTH, MOD_ROWS, 6 * D), F32),
        grid=(DEPTH, 6 * D // tn),
        in_specs=[
            pl.BlockSpec((MOD_ROWS, D), lambda l, j: (0, 0)),
            pl.BlockSpec((None, D, tn), lambda l, j: (l, 0, j)),
            pl.BlockSpec((None, 1, tn), lambda l, j: (l, 0, j)),
        ],
        out_specs=pl.BlockSpec((None, MOD_ROWS, tn), lambda l, j: (l, 0, j)),
        compiler_params=_cparams(("arbitrary", "arbitrary")),
        name="adaln",
    )(cmat, w_ada, b_ada.reshape(DEPTH, 1, 6 * D))


ROW_BM = 512
ROW_NPB = N_P // ROW_BM


def _row_specs(arr, cols):
    if isinstance(arr, tuple):
        return ([pl.BlockSpec((ROW_BM, cols), lambda i: (jnp.minimum(i, ROW_NPB - 1), 0)),
                 pl.BlockSpec((ROW_BM, cols), lambda i: (jnp.maximum(i - ROW_NPB, 0), 0))], list(arr))
    return [pl.BlockSpec((ROW_BM, cols), lambda i: (i, 0))], [arr]


def _pick_rows(refs):
    if len(refs) == 1:
        return refs[0][...]
    return jnp.where(pl.program_id(0) < ROW_NPB, refs[0][...], refs[1][...])


def _mod_spec(layer, which):
    return pl.BlockSpec((None, None, None, 1, D), lambda i: (layer, _mod_row(i, ROW_BM), which, 0, 0))


def _layer_vec_spec(layer):
    return pl.BlockSpec((None, 1, D), lambda i: (layer, 0, 0))


def _rms_mod(x, g_ref, sh_ref, sc_ref):
    y = x * lax.rsqrt(jnp.mean(x * x, axis=-1, keepdims=True) + EPS)
    return (y * g_ref[...]) * (1.0 + sc_ref[...]) + sh_ref[...]


def _out_proj_kernel(*refs, a_counts, n_x, final):
    pos = 0
    a_groups = []
    for cnt in a_counts:
        a_groups.append(refs[pos:pos + cnt])
        pos += cnt
    w_refs = refs[pos:pos + len(a_counts)]
    pos += len(a_counts)
    x_refs = refs[pos:pos + n_x]
    pos += n_x
    gt_ref, g_ref = refs[pos:pos + 2]
    pos += 2
    if not final:
        sh_ref, sc_ref = refs[pos:pos + 2]
        pos += 2
    out_a, out_b = refs[pos:pos + 2]
    wbf = refs[pos + 2:]

    @pl.when(pl.program_id(0) == 0)
    def _():
        for w_ref, wb in zip(w_refs, wbf):
            wb[...] = w_ref[...].astype(BF16)

    acc = None
    for group, wb in zip(a_groups, wbf):
        part = _dot(_pick_rows(group), wb[...])
        acc = part if acc is None else acc + part
    xn = _pick_rows(x_refs) + gt_ref[...] * acc
    if final:
        y = xn * lax.rsqrt(jnp.mean(xn * xn, axis=-1, keepdims=True) + EPS) * g_ref[...]

        @pl.when(pl.program_id(0) < ROW_NPB)
        def _():
            out_a[...] = y

        @pl.when(pl.program_id(0) >= ROW_NPB)
        def _():
            out_b[...] = y
    else:
        out_a[...] = xn
        out_b[...] = _rms_mod(xn, g_ref, sh_ref, sc_ref).astype(out_b.dtype)


def out_proj(a_list, w_list, x, mod_all, layer, gt_idx, g_all, g_layer, norm_mod_idx=None, name="out_proj"):
    final = norm_mod_idx is None
    in_specs, args, a_counts = [], [], []
    for a in a_list:
        cols = (a[0] if isinstance(a, tuple) else a).shape[1]
        specs, ops = _row_specs(a, cols)
        in_specs += specs
        args += ops
        a_counts.append(len(ops))
    w_shapes = []
    for w, block, imap in w_list:
        in_specs.append(pl.BlockSpec(block, imap, pipeline_mode=pl.Buffered(1)))
        args.append(w)
        w_shapes.append(tuple(b for b in block if b is not None))
    x_specs, x_args = _row_specs(x, D)
    in_specs += x_specs + [_mod_spec(layer, gt_idx), _layer_vec_spec(g_layer)]
    args += x_args + [mod_all, g_all.reshape(-1, 1, D)]
    if final:
        out_shape = [jax.ShapeDtypeStruct((N_P, D), F32), jax.ShapeDtypeStruct((N_S, D), F32)]
        out_specs = [pl.BlockSpec((ROW_BM, D), lambda i: (jnp.minimum(i, ROW_NPB - 1), 0)),
                     pl.BlockSpec((ROW_BM, D), lambda i: (jnp.maximum(i - ROW_NPB, 0), 0))]
    else:
        n_layer, sh_idx, sc_idx = norm_mod_idx
        in_specs += [_mod_spec(n_layer, sh_idx), _mod_spec(n_layer, sc_idx)]
        args += [mod_all, mod_all]
        out_shape = [jax.ShapeDtypeStruct((N_TOK, D), F32), jax.ShapeDtypeStruct((N_TOK, D), BF16)]
        out_specs = [pl.BlockSpec((ROW_BM, D), lambda i: (i, 0))] * 2
    return pl.pallas_call(
        functools.partial(_out_proj_kernel, a_counts=tuple(a_counts), n_x=len(x_args), final=final),
        out_shape=out_shape,
        grid=(N_TOK // ROW_BM,),
        in_specs=in_specs,
        out_specs=out_specs,
        scratch_shapes=[pltpu.VMEM(s, BF16) for s in w_shapes],
        compiler_params=_cparams(("arbitrary",)),
        name=name,
    )(*args)


def _qkv_kernel(a_ref, w_ref, o_ref, kc_ref, vc_ref, wbf):
    @pl.when(pl.program_id(0) == 0)
    def _():
        wbf[...] = w_ref[...].astype(BF16)

    a = a_ref[...]
    for part in range(3):
        acc = _dot(a, wbf[:, part * D:(part + 1) * D])
        for pp in range(NA_HEADS // 2):
            o_ref[part, pp] = acc[:, pp * 2 * NA_HD:(pp + 1) * 2 * NA_HD].astype(o_ref.dtype)
        if part > 0:
            c_ref = kc_ref if part == 1 else vc_ref

            @pl.when(pl.program_id(0) < ROW_NPB)
            def _(acc=acc, c_ref=c_ref):
                for b in range(ROW_BM // SEQ):
                    for hh in range(NA_HEADS):
                        c_ref[b, hh] = acc[b * SEQ:(b + 1) * SEQ, hh * NA_HD:(hh + 1) * NA_HD]


def qkv_proj(h, w_in, layer):
    seqs = ROW_BM // SEQ
    cache = jax.ShapeDtypeStruct((BATCH, 1, NA_HEADS, SEQ, NA_HD), F32)
    cache_spec = pl.BlockSpec((seqs, None, NA_HEADS, SEQ, NA_HD),
                              lambda i: (jnp.minimum(i, ROW_NPB - 1), 0, 0, 0, 0))
    return pl.pallas_call(
        _qkv_kernel,
        out_shape=[jax.ShapeDtypeStruct((3, NA_HEADS // 2, N_TOK, 2 * NA_HD), BF16), cache, cache],
        grid=(N_TOK // ROW_BM,),
        in_specs=[pl.BlockSpec((ROW_BM, D), lambda i: (i, 0)),
                  pl.BlockSpec((None, D, 3 * D), lambda i: (layer, 0, 0), pipeline_mode=pl.Buffered(1))],
        out_specs=[pl.BlockSpec((3, NA_HEADS // 2, ROW_BM, 2 * NA_HD), lambda i: (0, 0, i, 0)),
                   cache_spec, cache_spec],
        scratch_shapes=[pltpu.VMEM((D, 3 * D), BF16)],
        compiler_params=_cparams(("arbitrary",), QKV_VMEM_LIMIT),
        name="qkv_proj",
    )(h, w_in)


AB_MAIN = 4 * MW + 3 * HW
N_GATES = 4 * HEADS_M


def _ab_in_kernel(*refs):
    x_refs = refs[:-8]
    g_ref, sh_ref, sc_ref, w_ref, b_ref, proj_ref, gates_ref, wbf = refs[-8:]

    @pl.when(pl.program_id(0) == 0)
    def _():
        wbf[:, :4 * MW] = w_ref[:, :4 * MW].astype(BF16)
        wbf[:, 4 * MW:AB_MAIN] = w_ref[:, 4 * MW + N_GATES:].astype(BF16)
        lane = lax.broadcasted_iota(jnp.int32, (1, LANE), 1)
        wbf[:, AB_MAIN:] = jnp.where(lane < N_GATES, w_ref[:, 4 * MW:4 * MW + LANE], 0.0).astype(BF16)

    h = _rms_mod(_pick_rows(x_refs), g_ref, sh_ref, sc_ref).astype(BF16)
    proj_ref[...] = _dot(h, wbf[:, :AB_MAIN]).astype(proj_ref.dtype)
    gates_ref[...] = _dot(h, wbf[:, AB_MAIN:]) + b_ref[...]


def ab_in(x, g_all, mod_all, layer, w_in, b_gates, e):
    x_specs, x_args = _row_specs(x, D)
    b_pad = jnp.pad(b_gates, (0, LANE - N_GATES)).reshape(1, LANE)
    return pl.pallas_call(
        _ab_in_kernel,
        out_shape=[jax.ShapeDtypeStruct((N_TOK, AB_MAIN), BF16), jax.ShapeDtypeStruct((N_TOK, LANE), F32)],
        grid=(N_TOK // ROW_BM,),
        in_specs=x_specs + [_layer_vec_spec(layer), _mod_spec(layer, 0), _mod_spec(layer, 1),
                            pl.BlockSpec((None, D, w_in.shape[2]), lambda i: (e, 0, 0),
                                         pipeline_mode=pl.Buffered(1)),
                            pl.BlockSpec((1, LANE), lambda i: (0, 0))],
        out_specs=[pl.BlockSpec((ROW_BM, AB_MAIN), lambda i: (i, 0)), pl.BlockSpec((ROW_BM, LANE), lambda i: (i, 0))],
        scratch_shapes=[pltpu.VMEM((D, AB_MAIN + LANE), BF16)],
        compiler_params=_cparams(("arbitrary",)),
        name="ab_in",
    )(*x_args, g_all.reshape(-1, 1, D), mod_all, mod_all, w_in, b_pad)


SEQ_BLOCK = DEC_SEQ
N_PBLK = N_P // SEQ_BLOCK


def _seq_edges(rows, is_prompt):
    r = lax.broadcasted_iota(jnp.int32, (rows, 1), 0)
    first = (r == 0) | (is_prompt & (r % SEQ == 0))
    last = (r == rows - 1) | (is_prompt & (r % SEQ == SEQ - 1))
    return first, last


def _dwconv3(x, w_ref, first, last):
    rows = x.shape[0]
    prev = jnp.where(first, 0.0, pltpu.roll(x, 1, 0))
    nxt = jnp.where(last, 0.0, pltpu.roll(x, rows - 1, 0))
    return prev * w_ref[0:1, :] + x * w_ref[1:2, :] + nxt * w_ref[2:3, :]


def _gated_gelu_of_half(y, g):
    c = math.sqrt(2.0 / math.pi)
    t = jnp.tanh(y * (2.0 * c + (8.0 * 0.044715 * c) * (y * y)))
    return (y + y * t) * g


def _ffn_up_kernel(h_ref, wa_ref, wg_ref, wc_ref, o_ref):
    first, last = _seq_edges(SEQ_BLOCK, pl.program_id(0) < N_PBLK)
    h = h_ref[...]
    a = _dot(h, wa_ref[...].astype(BF16))
    g = _dot(h, wg_ref[...].astype(BF16))
    half_conv = _dwconv3(a, 0.5 * wc_ref[...], first, last)
    o_ref[...] = _gated_gelu_of_half(half_conv, g).astype(o_ref.dtype)


def ffn_up(h, w_up, w_conv, layer):
    tc = 256
    nct = FF // tc
    return pl.pallas_call(
        _ffn_up_kernel,
        out_shape=jax.ShapeDtypeStruct((N_TOK, FF), BF16),
        grid=(N_TOK // SEQ_BLOCK, nct),
        in_specs=[
            pl.BlockSpec((SEQ_BLOCK, D), lambda i, j: (i, 0)),
            pl.BlockSpec((None, D, tc), lambda i, j: (layer, 0, j)),
            pl.BlockSpec((None, D, tc), lambda i, j: (layer, 0, j + nct)),
            pl.BlockSpec((None, 3, tc), lambda i, j: (layer, 0, j)),
        ],
        out_specs=pl.BlockSpec((SEQ_BLOCK, tc), lambda i, j: (i, j)),
        compiler_params=_cparams(("arbitrary", "arbitrary")),
        name="ffn_up",
    )(h, w_up, w_up, w_conv)


HEAD_PAIRS = NA_HEADS // 2
NT_DIMS = (((1,), (1,)), ((), ()))


def _pair_mask(shape):
    return lax.broadcasted_iota(jnp.int32, shape, len(shape) - 1) < NA_HD


def _one_head(x2, first):
    keep = _pair_mask(x2.shape) if first else ~_pair_mask(x2.shape)
    return jnp.where(keep, x2, jnp.zeros_like(x2))


def _ctx_attn_kernel(q_ref, k_ref, v_ref, o_ref, *, pairs):
    outs = []
    for pp in range(pairs):
        q2, k2, v2 = q_ref[pp], k_ref[pp], v_ref[pp]
        res = []
        for first in (True, False):
            s = lax.dot_general(_one_head(q2, first), k2, NT_DIMS, preferred_element_type=F32) * (NA_HD ** -0.5)
            m = jnp.max(s, axis=-1, keepdims=True)
            p = jnp.exp(s - m)
            l = jnp.sum(p, axis=-1, keepdims=True)
            res.append(_dot(p.astype(BF16), v2) / l)
        outs.append(jnp.where(_pair_mask(res[0].shape), res[0], res[1]))
    o_ref[...] = jnp.concatenate(outs, axis=-1).astype(o_ref.dtype)


def ctx_attention(qkv):
    pairs = 4
    spec = lambda part: pl.BlockSpec((None, pairs, SEQ, 2 * NA_HD), lambda b, h: (part, h, b, 0))
    return pl.pallas_call(
        functools.partial(_ctx_attn_kernel, pairs=pairs),
        out_shape=jax.ShapeDtypeStruct((N_P, D), BF16),
        grid=(BATCH, HEAD_PAIRS // pairs),
        in_specs=[spec(0), spec(1), spec(2)],
        out_specs=pl.BlockSpec((SEQ, pairs * 2 * NA_HD), lambda b, h: (b, h)),
        compiler_params=_cparams(("arbitrary", "arbitrary")),
        name="ctx_attn",
    )(qkv, qkv, qkv)


def _na_tables():
    q = np.arange(GRID_W)[:, None]
    w = np.arange(GRID_W)[None, :]
    idx_c = np.clip(w - q + (NA_KW - 1), 0, 2 * NA_KW - 2)
    onehot = (idx_c.reshape(1, -1) == np.arange(32)[:, None]).astype(np.float32)
    c_start = np.clip(np.arange(GRID_W) - NA_KW // 2, 0, GRID_W - NA_KW)[:, None]
    inside = (w >= c_start) & (w < c_start + NA_KW)
    cmask = np.where(inside, 0.0, -np.inf).astype(np.float32)
    return onehot, np.tile(cmask, (1, 2))


def _rpb_expand_kernel(r_ref, e_ref, o_ref):
    o_ref[...] = jnp.dot(r_ref[...], e_ref[...], precision=HIGHEST, preferred_element_type=F32)


def rpb_expand(rpb):
    onehot, _ = _na_tables()
    rp = jnp.pad(rpb, ((0, 0), (0, 1), (0, 1)))
    return pl.pallas_call(
        _rpb_expand_kernel,
        out_shape=jax.ShapeDtypeStruct((NA_HEADS, 16, GRID_W * GRID_W), F32),
        grid=(NA_HEADS,),
        in_specs=[pl.BlockSpec((None, 16, 32), lambda h: (h, 0, 0)),
                  pl.BlockSpec((32, GRID_W * GRID_W), lambda h: (0, 0))],
        out_specs=pl.BlockSpec((None, 16, GRID_W * GRID_W), lambda h: (h, 0, 0)),
        compiler_params=_cparams(("arbitrary",)),
        name="rpb_expand",
    )(rp, jnp.asarray(onehot))


NA_QROWS = 8
NA_WIN = 2 * NA_QROWS


def _na_attn_kernel(q_ref, k_ref, v_ref, kc_ref, vc_ref, t_ref, o_ref, p_loc, p_ctx):
    nq = NA_QROWS * GRID_W
    nk = NA_WIN * GRID_W
    pair = 2 * GRID_W
    nt = NT_DIMS
    lane = lax.broadcasted_iota(jnp.int32, (GRID_W, pair), 1)
    zero_tile = jnp.zeros((GRID_W, pair), BF16)
    kc = jnp.concatenate([kc_ref[0], kc_ref[1]], axis=-1).astype(BF16)
    vc = jnp.concatenate([vc_ref[0], vc_ref[1]], axis=-1).astype(BF16)
    for blk in range(GRID_R // NA_QROWS):
        k0 = min(max(NA_QROWS * blk - NA_KH // 2, 0), GRID_R - NA_WIN)
        q2 = q_ref[blk * nq:(blk + 1) * nq, :] * (NA_HD ** -0.5)
        kw = k_ref[k0 * GRID_W:k0 * GRID_W + nk, :]
        vw = v_ref[k0 * GRID_W:k0 * GRID_W + nk, :]
        outs = []
        for hh in range(2):
            q = _one_head(q2, hh == 0)
            s_loc = lax.dot_general(q, kw, nt, preferred_element_type=F32)
            s_ctx = lax.dot_general(q, kc, nt, preferred_element_type=F32)
            denoms = []
            for qi in range(NA_QROWS):
                r = NA_QROWS * blk + qi
                r_start = min(max(r - NA_KH // 2, 0), GRID_R - NA_KH)
                rows = slice(qi * GRID_W, (qi + 1) * GRID_W)
                ctx_tiles = [s_ctx[rows, c * pair:(c + 1) * pair] for c in range(PAST_LEN // pair)]
                tiles = {}
                for j in range(nk // pair):
                    kr = k0 + 2 * j
                    ok0 = r_start <= kr < r_start + NA_KH
                    ok1 = r_start <= kr + 1 < r_start + NA_KH
                    if not (ok0 or ok1):
                        continue
                    sb = s_loc[rows, j * pair:(j + 1) * pair] + t_ref[hh, kr - r + NA_KH]
                    if not (ok0 and ok1):
                        sb = jnp.where((lane < GRID_W) if ok0 else (lane >= GRID_W), sb, -jnp.inf)
                    tiles[j] = sb
                mx = functools.reduce(jnp.maximum, list(tiles.values()) + ctx_tiles)
                m = jnp.max(mx, axis=1, keepdims=True)
                acc = None
                for j in range(nk // pair):
                    if j in tiles:
                        p = jnp.exp(tiles[j] - m)
                        acc = p if acc is None else acc + p
                        p_loc[rows, j * pair:(j + 1) * pair] = p.astype(BF16)
                    else:
                        p_loc[rows, j * pair:(j + 1) * pair] = zero_tile
                for c, t in enumerate(ctx_tiles):
                    p = jnp.exp(t - m)
                    acc = acc + p
                    p_ctx[rows, c * pair:(c + 1) * pair] = p.astype(BF16)
                denoms.append(jnp.sum(acc, axis=1, keepdims=True))
            outs.append((_dot(p_loc[...], vw) + _dot(p_ctx[...], vc)) / jnp.concatenate(denoms, axis=0))
        o_ref[blk * nq:(blk + 1) * nq, :] = jnp.where(_pair_mask(outs[0].shape), outs[0], outs[1]).astype(o_ref.dtype)


def na_attention(qkv, k_ctx, v_ctx, bias_pairs):
    blk0 = N_P // DEC_SEQ
    spec = lambda part: pl.BlockSpec((None, None, DEC_SEQ, 2 * NA_HD), lambda b, h: (part, h, b + blk0, 0))
    cspec = pl.BlockSpec((None, 2, PAST_LEN, NA_HD), lambda b, h: (b, h, 0, 0))
    return pl.pallas_call(
        _na_attn_kernel,
        out_shape=jax.ShapeDtypeStruct((N_S, D), BF16),
        grid=(DEC_BATCH, HEAD_PAIRS),
        in_specs=[spec(0), spec(1), spec(2), cspec, cspec,
                  pl.BlockSpec((2, 16, GRID_W, 2 * GRID_W), lambda b, h: (h, 0, 0, 0))],
        out_specs=pl.BlockSpec((DEC_SEQ, 2 * NA_HD), lambda b, h: (b, h)),
        scratch_shapes=[pltpu.VMEM((NA_QROWS * GRID_W, NA_WIN * GRID_W), BF16),
                        pltpu.VMEM((NA_QROWS * GRID_W, PAST_LEN), BF16)],
        compiler_params=_cparams(("arbitrary", "arbitrary")),
        name="na_attn",
    )(qkv, qkv, qkv, k_ctx, v_ctx, bias_pairs)


def mixer_c(h, k_ctx, v_ctx, w_in_all, layer, rpb):
    qkv, k_new, v_new = qkv_proj(h, w_in_all, layer)
    o_p = ctx_attention(qkv)
    _, cmask2 = _na_tables()
    b15 = rpb_expand(rpb).reshape(NA_HEADS, 16, GRID_W, GRID_W)
    b17 = jnp.pad(b15, ((0, 0), (1, 0), (0, 0), (0, 0)))
    bias_pairs = jnp.concatenate([b17[:, :16], b17[:, 1:]], axis=-1) + jnp.asarray(cmask2)
    o_s = na_attention(qkv, k_ctx, v_ctx, bias_pairs)
    return (o_p, o_s), k_new, v_new


SCAN_BLOCK = HD_M


def _mlstm_kernel(*refs, seq, zero_state, emit_state):
    q_ref, k_ref, v_ref, og_ref, gates_ref, wq_ref, wk_ref, gh_ref = refs[:8]
    pos = 8
    if not zero_state:
        c0_ref, n0_ref, m0_ref = refs[pos:pos + 3]
        pos += 3
    y_ref = refs[pos]
    pos += 1
    if emit_state:
        cn_ref, nn_ref, mn_ref = refs[pos:pos + 3]
        pos += 3
    kv_s, ks_s, be_s, mk_s, cp_s, np_s, mp_s, cst, nst, mst = refs[pos:]

    head = pl.program_id(1)
    blk = SCAN_BLOCK
    nc = seq // blk
    r = lax.broadcasted_iota(jnp.int32, (seq, 1), 0)
    first, last = r == 0, r == seq - 1
    qc_all = _dwconv3(q_ref[...].astype(F32), wq_ref, first, last)
    q3 = (qc_all * jax.nn.sigmoid(qc_all)).reshape(nc, blk, HD_M)
    kc_all = _dwconv3(k_ref[...].astype(F32), wk_ref, first, last)
    k3 = (kc_all * jax.nn.sigmoid(kc_all) * (HD_M ** -0.5)).reshape(nc, blk, HD_M)
    qb, kb = q3.astype(BF16), k3.astype(BF16)
    vb = v_ref[...].astype(BF16).reshape(nc, blk, HD_M)

    g_all = gates_ref[...]
    lf = jax.nn.log_sigmoid(g_all)
    rin = r % blk
    pre, suf = lf, lf
    for sh in [1 << i for i in range(blk.bit_length() - 1)]:
        pre = pre + jnp.where(rin >= sh, pltpu.roll(pre, sh, 0), 0.0)
        suf = suf + jnp.where(rin < blk - sh, pltpu.roll(suf, seq - sh, 0), 0.0)
    g3 = g_all.reshape(nc, blk, LANE)
    gt3 = jnp.swapaxes(g3, 1, 2)[:, :4 * HEADS_M, :]

    if zero_state:
        cst[...] = jnp.zeros_like(cst)
        nst[...] = jnp.zeros_like(nst)
        mst[...] = jnp.zeros_like(mst)
    else:
        cst[...] = c0_ref[...]
        nst[...] = n0_ref[...]
        mst[...] = jnp.broadcast_to(m0_ref[...], mst.shape)

    tt = lax.broadcasted_iota(jnp.int32, (1, blk, blk), 1)
    ss = lax.broadcasted_iota(jnp.int32, (1, blk, blk), 2)
    lane = lax.broadcasted_iota(jnp.int32, (1, 1, LANE), 2)
    sub = lax.broadcasted_iota(jnp.int32, (1, 4 * HEADS_M, 1), 1)
    hsum = None
    for d in range(2):
        i_idx = d * 2 * HEADS_M + head
        f_idx = i_idx + HEADS_M
        mask = (ss <= tt) if d == 0 else (ss >= tt)
        b3 = (pre if d == 0 else suf).reshape(nc, blk, LANE)
        bt3 = jnp.swapaxes(b3, 1, 2)[:, :4 * HEADS_M, :]
        bcol = jnp.sum(jnp.where(lane == f_idx, b3, 0.0), axis=2, keepdims=True)
        icol = jnp.sum(jnp.where(lane == i_idx, g3, 0.0), axis=2, keepdims=True)
        brow = jnp.sum(jnp.where(sub == f_idx, bt3, 0.0), axis=1, keepdims=True)
        irow = jnp.sum(jnp.where(sub == i_idx, gt3, 0.0), axis=1, keepdims=True)
        bend = bcol[:, blk - 1:blk, :] if d == 0 else bcol[:, 0:1, :]

        dmat = jnp.where(mask, bcol - brow + irow, -jnp.inf)
        mloc = jnp.max(dmat, axis=2, keepdims=True)
        qk = jnp.einsum('ctd,csd->cts', qb, kb, preferred_element_type=F32)
        s_loc = jnp.exp(dmat - mloc) * qk
        num_loc = jnp.einsum('cts,csd->ctd', s_loc.astype(BF16), vb, preferred_element_type=F32)
        den_loc = jnp.sum(s_loc, axis=2, keepdims=True)
        to_end = bend - bcol + icol
        mk = jnp.max(to_end, axis=1, keepdims=True)
        kw = k3 * jnp.exp(to_end - mk)
        kv_s[...] = jnp.einsum('cds,cse->cde', jnp.swapaxes(kw, 1, 2).astype(BF16), vb,
                               preferred_element_type=F32)
        ks_s[...] = jnp.sum(kw, axis=1, keepdims=True)
        be_s[...] = jnp.broadcast_to(bend, be_s.shape)
        mk_s[...] = jnp.broadcast_to(mk, mk_s.shape)

        def step(j, carry, d=d):
            c = j if d == 0 else nc - 1 - j
            m_prev, c_prev, n_prev = mst[d], cst[d], nst[d]
            cp_s[c] = c_prev.astype(BF16)
            np_s[c] = n_prev
            mp_s[c] = m_prev
            be, mkc = be_s[c], mk_s[c]
            m_new = jnp.maximum(be + m_prev, mkc)
            keep = jnp.exp(be + m_prev - m_new)
            add = jnp.exp(mkc - m_new)
            cst[d] = keep * c_prev + add * kv_s[c]
            nst[d] = keep * n_prev + add * ks_s[c]
            mst[d] = m_new
            return carry

        lax.fori_loop(0, nc, step, 0)

        m_inter = bcol + mp_s[...][:, :, 0:1]
        m_t = jnp.maximum(m_inter, mloc)
        w_state = jnp.exp(m_inter - m_t)
        w_loc = jnp.exp(mloc - m_t)
        inter = jnp.einsum('ctd,cde->cte', qb, cp_s[...], preferred_element_type=F32)
        num = w_state * inter + w_loc * num_loc
        den = w_state * jnp.sum(q3 * np_s[...], axis=2, keepdims=True) + w_loc * den_loc
        h = num / jnp.maximum(jnp.abs(den), jnp.exp(-m_t))
        hsum = h if hsum is None else hsum + h

    hsum = hsum.reshape(seq, HD_M)
    hn = hsum * lax.rsqrt(jnp.mean(hsum * hsum, axis=-1, keepdims=True) + EPS) * gh_ref[...]
    y_ref[...] = (hn * jax.nn.sigmoid(og_ref[...].astype(F32))).astype(y_ref.dtype)
    if emit_state:
        cn_ref[...] = cst[...]
        nn_ref[...] = nst[...]
        mn_ref[...] = mst[...]


def mlstm(proj, gates, w_conv_qk, g_head, state, *, nb, seq, row0):
    blk0 = row0 // seq
    nblk = seq // SCAN_BLOCK
    zero_state = state is None
    col = lambda part: pl.BlockSpec((seq, HD_M), lambda b, h: (b + blk0, part * HEADS_M + h))
    in_specs = [col(0), col(1), col(2), col(3),
                pl.BlockSpec((seq, LANE), lambda b, h: (b + blk0, 0)),
                pl.BlockSpec((3, HD_M), lambda b, h: (0, h)),
                pl.BlockSpec((3, HD_M), lambda b, h: (0, HEADS_M + h)),
                pl.BlockSpec((1, HD_M), lambda b, h: (0, h))]
    args = [proj, proj, proj, proj, gates, w_conv_qk, w_conv_qk, g_head.reshape(1, MW)]
    if not zero_state:
        c0, n0, m0 = state
        in_specs += [pl.BlockSpec((None, 2, None, HD_M, HD_M), lambda b, h: (b, 0, h, 0, 0)),
                     pl.BlockSpec((None, 2, None, 1, HD_M), lambda b, h: (b, 0, h, 0, 0)),
                     pl.BlockSpec((None, 2, None, 1, 1), lambda b, h: (b, 0, h, 0, 0))]
        args += [c0, n0.reshape(nb, 2, HEADS_M, 1, HD_M), m0.reshape(nb, 2, HEADS_M, 1, 1)]
    out_shape = [jax.ShapeDtypeStruct((nb * seq, MW), BF16)]
    out_specs = [pl.BlockSpec((seq, HD_M), lambda b, h: (b, h))]
    if zero_state:
        out_shape += [jax.ShapeDtypeStruct((nb, 2, HEADS_M, HD_M, HD_M), F32),
                      jax.ShapeDtypeStruct((nb, 2, HEADS_M, 1, HD_M), F32),
                      jax.ShapeDtypeStruct((nb, 2, HEADS_M, 1, LANE), F32)]
        out_specs += [pl.BlockSpec((None, 2, None, HD_M, HD_M), lambda b, h: (b, 0, h, 0, 0)),
                      pl.BlockSpec((None, 2, None, 1, HD_M), lambda b, h: (b, 0, h, 0, 0)),
                      pl.BlockSpec((None, 2, None, 1, LANE), lambda b, h: (b, 0, h, 0, 0))]
    return pl.pallas_call(
        functools.partial(_mlstm_kernel, seq=seq, zero_state=zero_state, emit_state=zero_state),
        out_shape=out_shape,
        grid=(nb, HEADS_M),
        in_specs=in_specs,
        out_specs=out_specs,
        scratch_shapes=[pltpu.VMEM((nblk, HD_M, HD_M), F32), pltpu.VMEM((nblk, 1, HD_M), F32),
                        pltpu.VMEM((nblk, 1, LANE), F32), pltpu.VMEM((nblk, 1, LANE), F32),
                        pltpu.VMEM((nblk, HD_M, HD_M), BF16), pltpu.VMEM((nblk, 1, HD_M), F32),
                        pltpu.VMEM((nblk, 1, LANE), F32),
                        pltpu.VMEM((2, HD_M, HD_M), F32), pltpu.VMEM((2, 1, HD_M), F32),
                        pltpu.VMEM((2, 1, LANE), F32)],
        compiler_params=_cparams(("arbitrary", "arbitrary")),
        name="mlstm_%d" % seq,
    )(*args)


def _hyena_pre_kernel(v_ref, x1_ref, x2_ref, wv_ref, w1_ref, w2_ref, u_ref, x2c_ref):
    first, last = _seq_edges(SEQ_BLOCK, pl.program_id(0) < N_PBLK)
    x1c = _dwconv3(x1_ref[...].astype(F32), w1_ref, first, last)
    u_ref[...] = x1c * _dwconv3(v_ref[...].astype(F32), wv_ref, first, last)
    x2c_ref[...] = _dwconv3(x2_ref[...].astype(F32), w2_ref, first, last)


def hyena_pre(proj, w_conv_hy):
    tc = 256
    nct = HW // tc
    c0 = 4 * MW // tc
    pcol = lambda part: pl.BlockSpec((SEQ_BLOCK, tc), lambda i, j: (i, c0 + part * nct + j))
    wcol = lambda part: pl.BlockSpec((3, tc), lambda i, j: (0, part * nct + j))
    out = jax.ShapeDtypeStruct((N_TOK, HW), F32)
    ospec = pl.BlockSpec((SEQ_BLOCK, tc), lambda i, j: (i, j))
    return pl.pallas_call(
        _hyena_pre_kernel,
        out_shape=[out, out],
        grid=(N_TOK // SEQ_BLOCK, nct),
        in_specs=[pcol(0), pcol(1), pcol(2), wcol(0), wcol(1), wcol(2)],
        out_specs=[ospec, ospec],
        compiler_params=_cparams(("arbitrary", "arbitrary")),
        name="hyena_pre",
    )(proj, proj, proj, w_conv_hy, w_conv_hy, w_conv_hy)


@functools.lru_cache(maxsize=None)
def _filter_tables(seq):
    t = np.linspace(0.0, 1.0, seq)[:, None]
    wpos = 2.0 * np.pi * np.arange(seq)[:, None] / seq
    bands = np.linspace(1e-4, N_BANDS - 1, N_BANDS)[None, :]
    z = np.concatenate([t, np.cos(bands * wpos), -np.sin(bands * wpos)], axis=-1)
    z = np.pad(z, ((0, 0), (0, LANE - FILTER_EMB)))
    max_decay = math.log(DECAY_TARGET) / DECAY_FAST
    min_decay = math.log(DECAY_TARGET) / DECAY_SLOW
    deltas = np.abs(np.linspace(min_decay, max_decay, HW))
    decay = np.exp(-t * np.concatenate([deltas, deltas])[None, :])
    return z.astype(np.float32), decay.astype(np.float32)


@functools.lru_cache(maxsize=None)
def _dft_tables(seq, tk):
    n = 2 * seq
    k = np.arange(seq)[:, None]
    t = np.arange(seq)[None, :]
    ang = 2.0 * np.pi * ((k * t) % n) / n
    alt = np.where(np.arange(seq) % 2 == 0, 1.0, -1.0)
    cm, sm = np.cos(ang), np.sin(ang)
    sm[0, :] = alt
    fwd = np.stack([cm.reshape(seq // tk, tk, seq), sm.reshape(seq // tk, tk, seq)], axis=1)
    wk = np.where(np.arange(seq) == 0, 1.0, 2.0)[None, :]
    ci = (np.cos(ang.T) * wk) / n
    si = np.sin(ang.T) * 2.0 / n
    si[:, 0] = alt / n
    inv = np.concatenate([ci, si], axis=1)
    return fwd.astype(np.float32), inv.astype(np.float32)


def _filter_kernel(z_ref, w1_ref, b1_ref, w2_ref, b2_ref, w3_ref, fr_ref, dec_ref, hs_ref, hd_ref):
    fr = fr_ref[...]
    hp = functools.partial(jnp.dot, precision=HIGHEST, preferred_element_type=F32)
    h1 = jnp.sin(fr * (hp(z_ref[...], w1_ref[...]) + b1_ref[...]))
    h2 = jnp.sin(fr * (hp(h1, w2_ref[...]) + b2_ref[...]))
    filt = hp(h2, w3_ref[...]) * dec_ref[...]
    past, fut = filt[:, :HW], filt[:, HW:]
    rows = filt.shape[0]
    grow = lax.broadcasted_iota(jnp.int32, (rows, 1), 0) + pl.program_id(0) * rows
    fut = jnp.where(grow == 0, 0.0, fut)
    hs_ref[...] = past + fut
    hd_ref[...] = past - fut


def filter_gen(seq, w1, b1, w2, b2, w3, freq):
    z, decay = _filter_tables(seq)
    tl = 256
    fh = FILTER_HIDDEN
    full = lambda shape: pl.BlockSpec(shape, lambda i: (0, 0))
    out = jax.ShapeDtypeStruct((seq, HW), F32)
    return pl.pallas_call(
        _filter_kernel,
        out_shape=[out, out],
        grid=(seq // tl,),
        in_specs=[pl.BlockSpec((tl, LANE), lambda i: (i, 0)), full((LANE, fh)), full((1, fh)), full((fh, fh)),
                  full((1, fh)), full((fh, 2 * HW)), full((1, fh)), pl.BlockSpec((tl, 2 * HW), lambda i: (i, 0))],
        out_specs=[pl.BlockSpec((tl, HW), lambda i: (i, 0))] * 2,
        compiler_params=_cparams(("arbitrary",)),
        name="filter_gen",
    )(jnp.asarray(z), jnp.pad(w1, ((0, LANE - FILTER_EMB), (0, 0))), b1.reshape(1, fh), w2, b2.reshape(1, fh), w3,
      freq.reshape(1, fh), jnp.asarray(decay))


def _dft_filter_kernel(a_ref, hs_ref, hd_ref, k_ref, hs_bf, hd_bf):
    @pl.when(pl.program_id(0) == 0)
    def _():
        hs_bf[...] = hs_ref[...].astype(BF16)
        hd_bf[...] = hd_ref[...].astype(BF16)

    k_ref[0] = _dot(a_ref[0], hs_bf[...])
    k_ref[1] = _dot(a_ref[1], hd_bf[...])

    @pl.when(pl.program_id(0) == 0)
    def _():
        k_ref[1, 0:1, :] = _dot(a_ref[1, 0:8, :], hs_bf[...])[0:1, :]


def dft_filter(seq, tk, hs, hd):
    fwd, _ = _dft_tables(seq, tk)
    return pl.pallas_call(
        _dft_filter_kernel,
        out_shape=jax.ShapeDtypeStruct((2, seq, HW), F32),
        grid=(seq // tk,),
        in_specs=[pl.BlockSpec((None, 2, tk, seq), lambda m: (m, 0, 0, 0)),
                  pl.BlockSpec((seq, HW), lambda m: (0, 0)), pl.BlockSpec((seq, HW), lambda m: (0, 0))],
        out_specs=pl.BlockSpec((2, tk, HW), lambda m: (0, m, 0)),
        scratch_shapes=[pltpu.VMEM((seq, HW), BF16), pltpu.VMEM((seq, HW), BF16)],
        compiler_params=_cparams(("arbitrary",)),
        name="dft_filter",
    )(jnp.asarray(fwd).astype(BF16), hs, hd)


def _dft_fwd_kernel(a_ref, u_ref, k_ref, y_ref, u_bf):
    m = pl.program_id(1)

    @pl.when(m == 0)
    def _():
        u_bf[...] = u_ref[...].astype(BF16)

    ure = _dot(a_ref[0], u_bf[...])
    uim = _dot(a_ref[1], u_bf[...])
    kre, kim = k_ref[0], k_ref[1]
    packed = (lax.broadcasted_iota(jnp.int32, (ure.shape[0], 1), 0) == 0) & (m == 0)
    y_ref[0] = jnp.where(packed, ure * kre, ure * kre - uim * kim).astype(y_ref.dtype)
    y_ref[1] = jnp.where(packed, uim * kim, ure * kim + uim * kre).astype(y_ref.dtype)


def dft_fwd(seq, tk, u, kf, *, nb, row0):
    fwd, _ = _dft_tables(seq, tk)
    blk0 = row0 // seq
    return pl.pallas_call(
        _dft_fwd_kernel,
        out_shape=jax.ShapeDtypeStruct((nb, 2, seq, HW), BF16),
        grid=(nb, seq // tk),
        in_specs=[pl.BlockSpec((None, 2, tk, seq), lambda b, m: (m, 0, 0, 0)),
                  pl.BlockSpec((seq, HW), lambda b, m: (b + blk0, 0)),
                  pl.BlockSpec((2, tk, HW), lambda b, m: (0, m, 0))],
        out_specs=pl.BlockSpec((None, 2, tk, HW), lambda b, m: (b, 0, m, 0)),
        scratch_shapes=[pltpu.VMEM((seq, HW), BF16)],
        compiler_params=_cparams(("arbitrary", "arbitrary")),
        name="dft_fwd",
    )(jnp.asarray(fwd).astype(BF16), u, kf)


def _dft_inv_kernel(a_ref, y_ref, u_ref, x2_ref, bias_ref, o_ref, *, seq):
    conv = _dot(a_ref[:, :seq], y_ref[0]) + _dot(a_ref[:, seq:], y_ref[1])
    o_ref[...] = (x2_ref[...] * (conv + bias_ref[...] * u_ref[...])).astype(o_ref.dtype)


def dft_inv(seq, tk, y, u, x2c, bias, *, nb, row0):
    _, inv = _dft_tables(seq, tk)
    tm = min(seq, 512)
    nt = seq // tm
    blk0 = row0 // tm
    rows = lambda b, t: (b * nt + t + blk0, 0)
    return pl.pallas_call(
        functools.partial(_dft_inv_kernel, seq=seq),
        out_shape=jax.ShapeDtypeStruct((nb * seq, HW), BF16),
        grid=(nb, nt),
        in_specs=[pl.BlockSpec((tm, 2 * seq), lambda b, t: (t, 0)),
                  pl.BlockSpec((None, 2, seq, HW), lambda b, t: (b, 0, 0, 0)),
                  pl.BlockSpec((tm, HW), rows), pl.BlockSpec((tm, HW), rows),
                  pl.BlockSpec((1, HW), lambda b, t: (0, 0))],
        out_specs=pl.BlockSpec((tm, HW), lambda b, t: (b * nt + t, 0)),
        compiler_params=_cparams(("arbitrary", "arbitrary")),
        name="dft_inv",
    )(jnp.asarray(inv).astype(BF16), y, u, x2c, bias.reshape(1, HW))


def hyena_group(seq, u, x2c, filt_w, bias, *, nb, row0):
    tk = min(seq, 512)
    hs, hd = filter_gen(seq, *filt_w)
    kf = dft_filter(seq, tk, hs, hd)
    y = dft_fwd(seq, tk, u, kf, nb=nb, row0=row0)
    return dft_inv(seq, tk, y, u, x2c, bias, nb=nb, row0=row0)


def mixer_ab_parts(proj, gates, state_s, w_conv_qk, g_head, w_conv_hy, w_f1, b_f1, w_f2, b_f2, w_f3, freq, hy_bias):
    ym_p, c_p, n_p, m_p = mlstm(proj, gates, w_conv_qk, g_head, None, nb=BATCH, seq=SEQ, row0=0)
    (ym_s,) = mlstm(proj, gates, w_conv_qk, g_head, state_s, nb=DEC_BATCH, seq=DEC_SEQ, row0=N_P)
    u, x2c = hyena_pre(proj, w_conv_hy)
    filt_w = (w_f1, b_f1, w_f2, b_f2, w_f3, freq)
    yh_p = hyena_group(SEQ, u, x2c, filt_w, hy_bias, nb=BATCH, row0=0)
    yh_s = hyena_group(DEC_SEQ, u, x2c, filt_w, hy_bias, nb=DEC_BATCH, row0=N_P)
    state_p = (c_p, n_p[:, :, :, 0, :], m_p[:, :, :, 0, 0])
    return (ym_p, ym_s), (yh_p, yh_s), state_p


def kernel(x_prompt, x_sample, state_mlstm_C, state_mlstm_n, state_mlstm_m, cache_na_k, cache_na_v, c, c_ctx, w_ada, b_ada, g_mix, g_ffn, g_final, w_in_ab, b_gates, w_conv_qk, g_mlstm, w_conv_hy, w_filt1, b_filt1, w_filt2, b_filt2, w_filt3, filt_freq, hyena_bias, w_out_ab, w_in_c, rpb_c, w_out_c, w_up, w_conv_ffn, w_down):
    cmat = jnp.concatenate([c, c_ctx[None, :], jnp.zeros((MOD_ROWS - DEC_BATCH - 1, D), F32)], axis=0)
    mod_all = adaln_all(cmat, w_ada, b_ada).reshape(DEPTH, MOD_ROWS, 6, 1, D)
    x = (x_prompt.reshape(N_P, D), x_sample.reshape(N_S, D))
    h = None
    new_c, new_n, new_m, new_k, new_v = [], [], [], [], []
    for l in range(DEPTH):
        e = l // 2
        if l % 2 == 0:
            assert l == 0, "a later mixer A/B layer would take the fused norm of the layer before it"
            proj, gates = ab_in(x, g_mix, mod_all, l, w_in_ab, b_gates[e], e)
            state_s = (state_mlstm_C[:, e], state_mlstm_n[:, e], state_mlstm_m[:, e])
            y_m, y_h, (c_p, n_p, m_p) = mixer_ab_parts(
                proj, gates, state_s, w_conv_qk[e], g_mlstm[e], w_conv_hy[e], w_filt1[e], b_filt1[e],
                w_filt2[e], b_filt2[e], w_filt3[e], filt_freq[e], hyena_bias[e])
            a_list = [y_m, y_h]
            w_list = [(w_out_ab, (None, MW, D), lambda i, e=e: (e, 0, 0)),
                      (w_out_ab, (None, HW, D), lambda i, e=e: (e, 1, 0))]
            new_c.append(c_p[:, None])
            new_n.append(n_p[:, None])
            new_m.append(m_p[:, None])
        else:
            o, k_new, v_new = mixer_c(h, cache_na_k[:, e], cache_na_v[:, e], w_in_c, e, rpb_c[e])
            a_list = [o]
            w_list = [(w_out_c, (None, D, D), lambda i, e=e: (e, 0, 0))]
            new_k.append(k_new)
            new_v.append(v_new)
        x, h = out_proj(a_list, w_list, x, mod_all, l, 2, g_ffn, l, (l, 3, 4), name="mixer_out")
        mid = ffn_up(h, w_up, w_conv_ffn, l)
        w_list = [(w_down, (None, FF, D), lambda i, l=l: (l, 0, 0))]
        if l + 1 < DEPTH:
            x, h = out_proj([mid], w_list, x, mod_all, l, 5, g_mix, l + 1, (l + 1, 0, 1), name="ffn_down")
        else:
            y_p, y_s = out_proj([mid], w_list, x, mod_all, l, 5, g_final, 0, None, name="ffn_down_final")
    cat = lambda parts: parts[0] if len(parts) == 1 else jnp.concatenate(parts, axis=1)
    return (y_p.reshape(BATCH, SEQ, D), y_s.reshape(DEC_BATCH, DEC_SEQ, D), cat(new_c), cat(new_n), cat(new_m),
            cat(new_k), cat(new_v))
```

```python
import functools
import math

import numpy as np
import jax
import jax.numpy as jnp
from jax import lax
from jax.experimental import pallas as pl
from jax.experimental.pallas import tpu as pltpu

F32 = jnp.float32
BF16 = jnp.bfloat16

D = 1024
BATCH, SEQ = 16, 256
DEC_BATCH, DEC_SEQ = 4, 2048
PAST_LEN = 512
DEPTH = 2
GRID_W = 64
GRID_R = DEC_SEQ // GRID_W
HEADS_M = 4
MW = D // 2
HD_M = MW // HEADS_M
CHUNK = 64
HW = D // 2
N_BANDS = 16
FILTER_EMB = 2 * N_BANDS + 1
FILTER_HIDDEN = 64
DECAY_FAST, DECAY_SLOW, DECAY_TARGET = 0.3, 1.5, 1e-2
NA_HEADS = 16
NA_HD = D // NA_HEADS
NA_KH, NA_KW = 8, 16
FF = 2816
EPS = 1e-6

N_P = BATCH * SEQ
N_S = DEC_BATCH * DEC_SEQ
N_TOK = N_P + N_S
CTX_ROW = DEC_BATCH
MOD_ROWS = 8
LANE = 128
VMEM_LIMIT = 48 * 1024 * 1024
QKV_VMEM_LIMIT = 56 * 1024 * 1024
HIGHEST = lax.Precision.HIGHEST


def _cparams(sem, vmem_limit=VMEM_LIMIT):
    return pltpu.CompilerParams(dimension_semantics=sem, vmem_limit_bytes=vmem_limit)


def _mod_row(i, bm):
    return jnp.where(i < N_P // bm, CTX_ROW, (i - N_P // bm) // (DEC_SEQ // bm))


def _dot(a, b):
    return jnp.dot(a, b, preferred_element_type=F32)


def _adaln_kernel(c_ref, w_ref, b_ref, o_ref):
    cv = c_ref[...]
    s = cv * jax.nn.sigmoid(cv)
    o_ref[...] = _dot(s.astype(BF16), w_ref[...].astype(BF16)) + b_ref[...]


def adaln_all(cmat, w_ada, b_ada):
    tn = 1024
    return pl.pallas_call(
        _adaln_kernel,
        out_shape=jax.ShapeDtypeStruct((DEPTH, MOD_ROWS, 6 * D), F32),
        grid=(DEPTH, 6 * D // tn),
        in_specs=[
            pl.BlockSpec((MOD_ROWS, D), lambda l, j: (0, 0)),
            pl.BlockSpec((None, D, tn), lambda l, j: (l, 0, j)),
            pl.BlockSpec((None, 1, tn), lambda l, j: (l, 0, j)),
        ],
        out_specs=pl.BlockSpec((None, MOD_ROWS, tn), lambda l, j: (l, 0, j)),
        compiler_params=_cparams(("arbitrary", "arbitrary")),
        name="adaln",
    )(cmat, w_ada, b_ada.reshape(DEPTH, 1, 6 * D))


ROW_BM = 512
ROW_NPB = N_P // ROW_BM


def _row_specs(arr, cols):
    if isinstance(arr, tuple):
        return ([pl.BlockSpec((ROW_BM, cols), lambda i: (jnp.minimum(i, ROW_NPB - 1), 0)),
                 pl.BlockSpec((ROW_BM, cols), lambda i: (jnp.maximum(i - ROW_NPB, 0), 0))], list(arr))
    return [pl.BlockSpec((ROW_BM, cols), lambda i: (i, 0))], [arr]


def _pick_rows(refs):
    if len(refs) == 1:
        return refs[0][...]
    return jnp.where(pl.program_id(0) < ROW_NPB, refs[0][...], refs[1][...])


def _mod_spec(layer, which):
    return pl.BlockSpec((None, None, None, 1, D), lambda i: (layer, _mod_row(i, ROW_BM), which, 0, 0))


def _layer_vec_spec(layer):
    return pl.BlockSpec((None, 1, D), lambda i: (layer, 0, 0))


def _rms_mod(x, g_ref, sh_ref, sc_ref):
    y = x * lax.rsqrt(jnp.mean(x * x, axis=-1, keepdims=True) + EPS)
    return (y * g_ref[...]) * (1.0 + sc_ref[...]) + sh_ref[...]


def _out_proj_kernel(*refs, a_counts, n_x, final):
    pos = 0
    a_groups = []
    for cnt in a_counts:
        a_groups.append(refs[pos:pos + cnt])
        pos += cnt
    w_refs = refs[pos:pos + len(a_counts)]
    pos += len(a_counts)
    x_refs = refs[pos:pos + n_x]
    pos += n_x
    gt_ref, g_ref = refs[pos:pos + 2]
    pos += 2
    if not final:
        sh_ref, sc_ref = refs[pos:pos + 2]
        pos += 2
    out_a, out_b = refs[pos:pos + 2]
    wbf = refs[pos + 2:]

    @pl.when(pl.program_id(0) == 0)
    def _():
        for w_ref, wb in zip(w_refs, wbf):
            wb[...] = w_ref[...].astype(BF16)

    acc = None
    for group, wb in zip(a_groups, wbf):
        part = _dot(_pick_rows(group), wb[...])
        acc = part if acc is None else acc + part
    xn = _pick_rows(x_refs) + gt_ref[...] * acc
    if final:
        y = xn * lax.rsqrt(jnp.mean(xn * xn, axis=-1, keepdims=True) + EPS) * g_ref[...]

        @pl.when(pl.program_id(0) < ROW_NPB)
        def _():
            out_a[...] = y

        @pl.when(pl.program_id(0) >= ROW_NPB)
        def _():
            out_b[...] = y
    else:
        out_a[...] = xn
        out_b[...] = _rms_mod(xn, g_ref, sh_ref, sc_ref).astype(out_b.dtype)


def out_proj(a_list, w_list, x, mod_all, layer, gt_idx, g_all, g_layer, norm_mod_idx=None, name="out_proj"):
    final = norm_mod_idx is None
    in_specs, args, a_counts = [], [], []
    for a in a_list:
        cols = (a[0] if isinstance(a, tuple) else a).shape[1]
        specs, ops = _row_specs(a, cols)
        in_specs += specs
        args += ops
        a_counts.append(len(ops))
    w_shapes = []
    for w, block, imap in w_list:
        in_specs.append(pl.BlockSpec(block, imap, pipeline_mode=pl.Buffered(1)))
        args.append(w)
        w_shapes.append(tuple(b for b in block if b is not None))
    x_specs, x_args = _row_specs(x, D)
    in_specs += x_specs + [_mod_spec(layer, gt_idx), _layer_vec_spec(g_layer)]
    args += x_args + [mod_all, g_all.reshape(-1, 1, D)]
    if final:
        out_shape = [jax.ShapeDtypeStruct((N_P, D), F32), jax.ShapeDtypeStruct((N_S, D), F32)]
        out_specs = [pl.BlockSpec((ROW_BM, D), lambda i: (jnp.minimum(i, ROW_NPB - 1), 0)),
                     pl.BlockSpec((ROW_BM, D), lambda i: (jnp.maximum(i - ROW_NPB, 0), 0))]
    else:
        n_layer, sh_idx, sc_idx = norm_mod_idx
        in_specs += [_mod_spec(n_layer, sh_idx), _mod_spec(n_layer, sc_idx)]
        args += [mod_all, mod_all]
        out_shape = [jax.ShapeDtypeStruct((N_TOK, D), F32), jax.ShapeDtypeStruct((N_TOK, D), BF16)]
        out_specs = [pl.BlockSpec((ROW_BM, D), lambda i: (i, 0))] * 2
    return pl.pallas_call(
        functools.partial(_out_proj_kernel, a_counts=tuple(a_counts), n_x=len(x_args), final=final),
        out_shape=out_shape,
        grid=(N_TOK // ROW_BM,),
        in_specs=in_specs,
        out_specs=out_specs,
        scratch_shapes=[pltpu.VMEM(s, BF16) for s in w_shapes],
        compiler_params=_cparams(("arbitrary",)),
        name=name,
    )(*args)


def _qkv_kernel(a_ref, w_ref, o_ref, kc_ref, vc_ref, wbf):
    @pl.when(pl.program_id(0) == 0)
    def _():
        wbf[...] = w_ref[...].astype(BF16)

    a = a_ref[...]
    for part in range(3):
        acc = _dot(a, wbf[:, part * D:(part + 1) * D])
        for pp in range(NA_HEADS // 2):
            o_ref[part, pp] = acc[:, pp * 2 * NA_HD:(pp + 1) * 2 * NA_HD].astype(o_ref.dtype)
        if part > 0:
            c_ref = kc_ref if part == 1 else vc_ref

            @pl.when(pl.program_id(0) < ROW_NPB)
            def _(acc=acc, c_ref=c_ref):
                for b in range(ROW_BM // SEQ):
                    for hh in range(NA_HEADS):
                        c_ref[b, hh] = acc[b * SEQ:(b + 1) * SEQ, hh * NA_HD:(hh + 1) * NA_HD]


def qkv_proj(h, w_in, layer):
    seqs = ROW_BM // SEQ
    cache = jax.ShapeDtypeStruct((BATCH, 1, NA_HEADS, SEQ, NA_HD), F32)
    cache_spec = pl.BlockSpec((seqs, None, NA_HEADS, SEQ, NA_HD),
                              lambda i: (jnp.minimum(i, ROW_NPB - 1), 0, 0, 0, 0))
    return pl.pallas_call(
        _qkv_kernel,
        out_shape=[jax.ShapeDtypeStruct((3, NA_HEADS // 2, N_TOK, 2 * NA_HD), BF16), cache, cache],
        grid=(N_TOK // ROW_BM,),
        in_specs=[pl.BlockSpec((ROW_BM, D), lambda i: (i, 0)),
                  pl.BlockSpec((None, D, 3 * D), lambda i: (layer, 0, 0), pipeline_mode=pl.Buffered(1))],
        out_specs=[pl.BlockSpec((3, NA_HEADS // 2, ROW_BM, 2 * NA_HD), lambda i: (0, 0, i, 0)),
                   cache_spec, cache_spec],
        scratch_shapes=[pltpu.VMEM((D, 3 * D), BF16)],
        compiler_params=_cparams(("arbitrary",), QKV_VMEM_LIMIT),
        name="qkv_proj",
    )(h, w_in)


AB_MAIN = 4 * MW + 3 * HW
N_GATES = 4 * HEADS_M


def _ab_in_kernel(*refs):
    x_refs = refs[:-8]
    g_ref, sh_ref, sc_ref, w_ref, b_ref, proj_ref, gates_ref, wbf = refs[-8:]

    @pl.when(pl.program_id(0) == 0)
    def _():
        wbf[:, :4 * MW] = w_ref[:, :4 * MW].astype(BF16)
        wbf[:, 4 * MW:AB_MAIN] = w_ref[:, 4 * MW + N_GATES:].astype(BF16)
        lane = lax.broadcasted_iota(jnp.int32, (1, LANE), 1)
        wbf[:, AB_MAIN:] = jnp.where(lane < N_GATES, w_ref[:, 4 * MW:4 * MW + LANE], 0.0).astype(BF16)

    h = _rms_mod(_pick_rows(x_refs), g_ref, sh_ref, sc_ref).astype(BF16)
    proj_ref[...] = _dot(h, wbf[:, :AB_MAIN]).astype(proj_ref.dtype)
    gates_ref[...] = _dot(h, wbf[:, AB_MAIN:]) + b_ref[...]


def ab_in(x, g_all, mod_all, layer, w_in, b_gates, e):
    x_specs, x_args = _row_specs(x, D)
    b_pad = jnp.pad(b_gates, (0, LANE - N_GATES)).reshape(1, LANE)
    return pl.pallas_call(
        _ab_in_kernel,
        out_shape=[jax.ShapeDtypeStruct((N_TOK, AB_MAIN), BF16), jax.ShapeDtypeStruct((N_TOK, LANE), F32)],
        grid=(N_TOK // ROW_BM,),
        in_specs=x_specs + [_layer_vec_spec(layer), _mod_spec(layer, 0), _mod_spec(layer, 1),
                            pl.BlockSpec((None, D, w_in.shape[2]), lambda i: (e, 0, 0),
                                         pipeline_mode=pl.Buffered(1)),
                            pl.BlockSpec((1, LANE), lambda i: (0, 0))],
        out_specs=[pl.BlockSpec((ROW_BM, AB_MAIN), lambda i: (i, 0)), pl.BlockSpec((ROW_BM, LANE), lambda i: (i, 0))],
        scratch_shapes=[pltpu.VMEM((D, AB_MAIN + LANE), BF16)],
        compiler_params=_cparams(("arbitrary",)),
        name="ab_in",
    )(*x_args, g_all.reshape(-1, 1, D), mod_all, mod_all, w_in, b_pad)


SEQ_BLOCK = DEC_SEQ
N_PBLK = N_P // SEQ_BLOCK


def _seq_edges(rows, is_prompt):
    r = lax.broadcasted_iota(jnp.int32, (rows, 1), 0)
    first = (r == 0) | (is_prompt & (r % SEQ == 0))
    last = (r == rows - 1) | (is_prompt & (r % SEQ == SEQ - 1))
    return first, last


def _dwconv3(x, w_ref, first, last):
    rows = x.shape[0]
    prev = jnp.where(first, 0.0, pltpu.roll(x, 1, 0))
    nxt = jnp.where(last, 0.0, pltpu.roll(x, rows - 1, 0))
    return prev * w_ref[0:1, :] + x * w_ref[1:2, :] + nxt * w_ref[2:3, :]


def _gated_gelu_of_half(y, g):
    c = math.sqrt(2.0 / math.pi)
    t = jnp.tanh(y * (2.0 * c + (8.0 * 0.044715 * c) * (y * y)))
    return (y + y * t) * g


def _ffn_up_kernel(h_ref, wa_ref, wg_ref, wc_ref, o_ref):
    first, last = _seq_edges(SEQ_BLOCK, pl.program_id(0) < N_PBLK)
    h = h_ref[...]
    a = _dot(h, wa_ref[...].astype(BF16))
    g = _dot(h, wg_ref[...].astype(BF16))
    half_conv = _dwconv3(a, 0.5 * wc_ref[...], first, last)
    o_ref[...] = _gated_gelu_of_half(half_conv, g).astype(o_ref.dtype)


def ffn_up(h, w_up, w_conv, layer):
    tc = 256
    nct = FF // tc
    return pl.pallas_call(
        _ffn_up_kernel,
        out_shape=jax.ShapeDtypeStruct((N_TOK, FF), BF16),
        grid=(N_TOK // SEQ_BLOCK, nct),
        in_specs=[
            pl.BlockSpec((SEQ_BLOCK, D), lambda i, j: (i, 0)),
            pl.BlockSpec((None, D, tc), lambda i, j: (layer, 0, j)),
            pl.BlockSpec((None, D, tc), lambda i, j: (layer, 0, j + nct)),
            pl.BlockSpec((None, 3, tc), lambda i, j: (layer, 0, j)),
        ],
        out_specs=pl.BlockSpec((SEQ_BLOCK, tc), lambda i, j: (i, j)),
        compiler_params=_cparams(("arbitrary", "arbitrary")),
        name="ffn_up",
    )(h, w_up, w_up, w_conv)


HEAD_PAIRS = NA_HEADS // 2
NT_DIMS = (((1,), (1,)), ((), ()))


def _pair_mask(shape):
    return lax.broadcasted_iota(jnp.int32, shape, len(shape) - 1) < NA_HD


def _one_head(x2, first):
    keep = _pair_mask(x2.shape) if first else ~_pair_mask(x2.shape)
    return jnp.where(keep, x2, jnp.zeros_like(x2))


def _ctx_attn_kernel(q_ref, k_ref, v_ref, o_ref, *, pairs):
    outs = []
    for pp in range(pairs):
        q2, k2, v2 = q_ref[pp], k_ref[pp], v_ref[pp]
        res = []
        for first in (True, False):
            s = lax.dot_general(_one_head(q2, first), k2, NT_DIMS, preferred_element_type=F32) * (NA_HD ** -0.5)
            m = jnp.max(s, axis=-1, keepdims=True)
            p = jnp.exp(s - m)
            l = jnp.sum(p, axis=-1, keepdims=True)
            res.append(_dot(p.astype(BF16), v2) / l)
        outs.append(jnp.where(_pair_mask(res[0].shape), res[0], res[1]))
    o_ref[...] = jnp.concatenate(outs, axis=-1).astype(o_ref.dtype)


def ctx_attention(qkv):
    pairs = 4
    spec = lambda part: pl.BlockSpec((None, pairs, SEQ, 2 * NA_HD), lambda b, h: (part, h, b, 0))
    return pl.pallas_call(
        functools.partial(_ctx_attn_kernel, pairs=pairs),
        out_shape=jax.ShapeDtypeStruct((N_P, D), BF16),
        grid=(BATCH, HEAD_PAIRS // pairs),
        in_specs=[spec(0), spec(1), spec(2)],
        out_specs=pl.BlockSpec((SEQ, pairs * 2 * NA_HD), lambda b, h: (b, h)),
        compiler_params=_cparams(("arbitrary", "arbitrary")),
        name="ctx_attn",
    )(qkv, qkv, qkv)


def _na_tables():
    q = np.arange(GRID_W)[:, None]
    w = np.arange(GRID_W)[None, :]
    idx_c = np.clip(w - q + (NA_KW - 1), 0, 2 * NA_KW - 2)
    onehot = (idx_c.reshape(1, -1) == np.arange(32)[:, None]).astype(np.float32)
    c_start = np.clip(np.arange(GRID_W) - NA_KW // 2, 0, GRID_W - NA_KW)[:, None]
    inside = (w >= c_start) & (w < c_start + NA_KW)
    cmask = np.where(inside, 0.0, -np.inf).astype(np.float32)
    return onehot, np.tile(cmask, (1, 2))


def _rpb_expand_kernel(r_ref, e_ref, o_ref):
    o_ref[...] = jnp.dot(r_ref[...], e_ref[...], precision=HIGHEST, preferred_element_type=F32)


def rpb_expand(rpb):
    onehot, _ = _na_tables()
    rp = jnp.pad(rpb, ((0, 0), (0, 1), (0, 1)))
    return pl.pallas_call(
        _rpb_expand_kernel,
        out_shape=jax.ShapeDtypeStruct((NA_HEADS, 16, GRID_W * GRID_W), F32),
        grid=(NA_HEADS,),
        in_specs=[pl.BlockSpec((None, 16, 32), lambda h: (h, 0, 0)),
                  pl.BlockSpec((32, GRID_W * GRID_W), lambda h: (0, 0))],
        out_specs=pl.BlockSpec((None, 16, GRID_W * GRID_W), lambda h: (h, 0, 0)),
        compiler_params=_cparams(("arbitrary",)),
        name="rpb_expand",
    )(rp, jnp.asarray(onehot))


NA_QROWS = 8
NA_WIN = 2 * NA_QROWS


def _na_attn_kernel(q_ref, k_ref, v_ref, kc_ref, vc_ref, t_ref, o_ref, p_loc, p_ctx):
    nq = NA_QROWS * GRID_W
    nk = NA_WIN * GRID_W
    pair = 2 * GRID_W
    nt = NT_DIMS
    lane = lax.broadcasted_iota(jnp.int32, (GRID_W, pair), 1)
    zero_tile = jnp.zeros((GRID_W, pair), BF16)
    kc = jnp.concatenate([kc_ref[0], kc_ref[1]], axis=-1).astype(BF16)
    vc = jnp.concatenate([vc_ref[0], vc_ref[1]], axis=-1).astype(BF16)
    for blk in range(GRID_R // NA_QROWS):
        k0 = min(max(NA_QROWS * blk - NA_KH // 2, 0), GRID_R - NA_WIN)
        q2 = q_ref[blk * nq:(blk + 1) * nq, :] * (NA_HD ** -0.5)
        kw = k_ref[k0 * GRID_W:k0 * GRID_W + nk, :]
        vw = v_ref[k0 * GRID_W:k0 * GRID_W + nk, :]
        outs = []
        for hh in range(2):
            q = _one_head(q2, hh == 0)
            s_loc = lax.dot_general(q, kw, nt, preferred_element_type=F32)
            s_ctx = lax.dot_general(q, kc, nt, preferred_element_type=F32)
            denoms = []
            for qi in range(NA_QROWS):
                r = NA_QROWS * blk + qi
                r_start = min(max(r - NA_KH // 2, 0), GRID_R - NA_KH)
                rows = slice(qi * GRID_W, (qi + 1) * GRID_W)
                ctx_tiles = [s_ctx[rows, c * pair:(c + 1) * pair] for c in range(PAST_LEN // pair)]
                tiles = {}
                for j in range(nk // pair):
                    kr = k0 + 2 * j
                    ok0 = r_start <= kr < r_start + NA_KH
                    ok1 = r_start <= kr + 1 < r_start + NA_KH
                    if not (ok0 or ok1):
                        continue
                    sb = s_loc[rows, j * pair:(j + 1) * pair] + t_ref[hh, kr - r + NA_KH]
                    if not (ok0 and ok1):
                        sb = jnp.where((lane < GRID_W) if ok0 else (lane >= GRID_W), sb, -jnp.inf)
                    tiles[j] = sb
                mx = functools.reduce(jnp.maximum, list(tiles.values()) + ctx_tiles)
                m = jnp.max(mx, axis=1, keepdims=True)
                acc = None
                for j in range(nk // pair):
                    if j in tiles:
                        p = jnp.exp(tiles[j] - m)
                        acc = p if acc is None else acc + p
                        p_loc[rows, j * pair:(j + 1) * pair] = p.astype(BF16)
                    else:
                        p_loc[rows, j * pair:(j + 1) * pair] = zero_tile
                for c, t in enumerate(ctx_tiles):
                    p = jnp.exp(t - m)
                    acc = acc + p
                    p_ctx[rows, c * pair:(c + 1) * pair] = p.astype(BF16)
                denoms.append(jnp.sum(acc, axis=1, keepdims=True))
            outs.append((_dot(p_loc[...], vw) + _dot(p_ctx[...], vc)) / jnp.concatenate(denoms, axis=0))
        o_ref[blk * nq:(blk + 1) * nq, :] = jnp.where(_pair_mask(outs[0].shape), outs[0], outs[1]).astype(o_ref.dtype)


def na_attention(qkv, k_ctx, v_ctx, bias_pairs):
    blk0 = N_P // DEC_SEQ
    spec = lambda part: pl.BlockSpec((None, None, DEC_SEQ, 2 * NA_HD), lambda h, b: (part, h, b + blk0, 0))
    cspec = pl.BlockSpec((None, 2, PAST_LEN, NA_HD), lambda h, b: (b, h, 0, 0))
    return pl.pallas_call(
        _na_attn_kernel,
        out_shape=jax.ShapeDtypeStruct((N_S, D), BF16),
        grid=(HEAD_PAIRS, DEC_BATCH),
        in_specs=[spec(0), spec(1), spec(2), cspec, cspec,
                  pl.BlockSpec((2, 16, GRID_W, 2 * GRID_W), lambda h, b: (h, 0, 0, 0))],
        out_specs=pl.BlockSpec((DEC_SEQ, 2 * NA_HD), lambda h, b: (b, h)),
        scratch_shapes=[pltpu.VMEM((NA_QROWS * GRID_W, NA_WIN * GRID_W), BF16),
                        pltpu.VMEM((NA_QROWS * GRID_W, PAST_LEN), BF16)],
        compiler_params=_cparams(("arbitrary", "arbitrary")),
        name="na_attn",
    )(qkv, qkv, qkv, k_ctx, v_ctx, bias_pairs)


def mixer_c(h, k_ctx, v_ctx, w_in_all, layer, rpb):
    qkv, k_new, v_new = qkv_proj(h, w_in_all, layer)
    o_p = ctx_attention(qkv)
    _, cmask2 = _na_tables()
    b15 = rpb_expand(rpb).reshape(NA_HEADS, 16, GRID_W, GRID_W)
    b17 = jnp.pad(b15, ((0, 0), (1, 0), (0, 0), (0, 0)))
    bias_pairs = jnp.concatenate([b17[:, :16], b17[:, 1:]], axis=-1) + jnp.asarray(cmask2)
    o_s = na_attention(qkv, k_ctx, v_ctx, bias_pairs)
    return (o_p, o_s), k_new, v_new


SCAN_BLOCK = HD_M


def _mlstm_kernel(*refs, seq, heads, zero_state, emit_state):
    q_ref, k_ref, v_ref, og_ref, gates_ref, wq_ref, wk_ref, gh_ref = refs[:8]
    pos = 8
    if not zero_state:
        c0_ref, n0_ref, m0_ref = refs[pos:pos + 3]
        pos += 3
    y_ref = refs[pos]
    pos += 1
    if emit_state:
        cn_ref, nn_ref, mn_ref = refs[pos:pos + 3]
        pos += 3
    pre_s, suf_s, gt_s, pret_s, suft_s, kv_s, ks_s, be_s, mk_s, cp_s, np_s, mp_s, cst, nst, mst = refs[pos:]

    blk = SCAN_BLOCK
    nc = seq // blk
    nbatch = nc * heads
    n_gates = 4 * HEADS_M
    r = lax.broadcasted_iota(jnp.int32, (seq, 1), 0)

    @pl.when(pl.program_id(1) == 0)
    def _():
        g_all = gates_ref[...]
        lf = jax.nn.log_sigmoid(g_all)
        rin = r % blk
        pre, suf = lf, lf
        for sh in [1 << i for i in range(blk.bit_length() - 1)]:
            pre = pre + jnp.where(rin >= sh, pltpu.roll(pre, sh, 0), 0.0)
            suf = suf + jnp.where(rin < blk - sh, pltpu.roll(suf, seq - sh, 0), 0.0)
        pre3, suf3 = pre.reshape(nc, blk, LANE), suf.reshape(nc, blk, LANE)
        pre_s[...] = pre3
        suf_s[...] = suf3
        gt_s[...] = jnp.swapaxes(g_all.reshape(nc, blk, LANE), 1, 2)[:, :n_gates, :]
        pret_s[...] = jnp.swapaxes(pre3, 1, 2)[:, :n_gates, :]
        suft_s[...] = jnp.swapaxes(suf3, 1, 2)[:, :n_gates, :]

    def split_heads(x):
        x3 = x.reshape(nc, blk, heads * HD_M)
        if heads == 1:
            return x3
        tiles = jnp.stack([x3[:, :, hh * HD_M:(hh + 1) * HD_M] for hh in range(heads)], axis=1)
        return tiles.reshape(nbatch, blk, HD_M)

    def per_head(x):
        if heads == 1:
            return x
        return jnp.broadcast_to(x[:, None], (nc, heads) + x.shape[1:]).reshape((nbatch,) + x.shape[1:])

    first, last = r == 0, r == seq - 1
    qc_all = _dwconv3(q_ref[...].astype(F32), wq_ref, first, last)
    q3 = split_heads(qc_all * jax.nn.sigmoid(qc_all))
    kc_all = _dwconv3(k_ref[...].astype(F32), wk_ref, first, last)
    k3 = split_heads(kc_all * jax.nn.sigmoid(kc_all) * (HD_M ** -0.5))
    qb, kb = q3.astype(BF16), k3.astype(BF16)
    vb = split_heads(v_ref[...])
    g3 = per_head(gates_ref[...].reshape(nc, blk, LANE))
    gt3 = per_head(gt_s[...])

    if zero_state:
        cst[...] = jnp.zeros_like(cst)
        nst[...] = jnp.zeros_like(nst)
        mst[...] = jnp.zeros_like(mst)
    else:
        cst[...] = c0_ref[...]
        nst[...] = n0_ref[...]
        mst[...] = jnp.broadcast_to(m0_ref[...], mst.shape)

    tt = lax.broadcasted_iota(jnp.int32, (1, blk, blk), 1)
    ss = lax.broadcasted_iota(jnp.int32, (1, blk, blk), 2)
    lane = lax.broadcasted_iota(jnp.int32, (1, 1, LANE), 2)
    sub = lax.broadcasted_iota(jnp.int32, (1, n_gates, 1), 1)
    head = lax.broadcasted_iota(jnp.int32, (nbatch, 1, 1), 0) % heads + pl.program_id(1) * heads
    hsum = None
    for d in range(2):
        i_idx = d * 2 * HEADS_M + head
        f_idx = i_idx + HEADS_M
        mask = (ss <= tt) if d == 0 else (ss >= tt)
        b3 = per_head((pre_s if d == 0 else suf_s)[...])
        bt3 = per_head((pret_s if d == 0 else suft_s)[...])
        bcol = jnp.sum(jnp.where(lane == f_idx, b3, 0.0), axis=2, keepdims=True)
        icol = jnp.sum(jnp.where(lane == i_idx, g3, 0.0), axis=2, keepdims=True)
        brow = jnp.sum(jnp.where(sub == f_idx, bt3, 0.0), axis=1, keepdims=True)
        irow = jnp.sum(jnp.where(sub == i_idx, gt3, 0.0), axis=1, keepdims=True)
        bend = bcol[:, blk - 1:blk, :] if d == 0 else bcol[:, 0:1, :]

        dmat = jnp.where(mask, bcol - brow + irow, -jnp.inf)
        mloc = jnp.max(dmat, axis=2, keepdims=True)
        qk = jnp.einsum('ctd,csd->cts', qb, kb, preferred_element_type=F32)
        s_loc = jnp.exp(dmat - mloc) * qk
        num_loc = jnp.einsum('cts,csd->ctd', s_loc.astype(BF16), vb, preferred_element_type=F32)
        den_loc = jnp.sum(s_loc, axis=2, keepdims=True)
        to_end = bend - bcol + icol
        mk = jnp.max(to_end, axis=1, keepdims=True)
        kw = k3 * jnp.exp(to_end - mk)
        kv_s[...] = jnp.einsum('cds,cse->cde', jnp.swapaxes(kw, 1, 2).astype(BF16), vb,
                               preferred_element_type=F32)
        ks_s[...] = jnp.sum(kw, axis=1, keepdims=True)
        be_s[...] = jnp.broadcast_to(bend, be_s.shape)
        mk_s[...] = jnp.broadcast_to(mk, mk_s.shape)

        def step(j, carry, d=d):
            c = j if d == 0 else nc - 1 - j
            sl = pl.ds(c * heads, heads)
            m_prev, c_prev, n_prev = mst[d], cst[d], nst[d]
            cp_s[sl] = c_prev.astype(BF16)
            np_s[sl] = n_prev
            mp_s[sl] = m_prev
            be, mkc = be_s[sl], mk_s[sl]
            m_new = jnp.maximum(be + m_prev, mkc)
            keep = jnp.exp(be + m_prev - m_new)
            add = jnp.exp(mkc - m_new)
            cst[d] = keep * c_prev + add * kv_s[sl]
            nst[d] = keep * n_prev + add * ks_s[sl]
            mst[d] = m_new
            return carry

        lax.fori_loop(0, nc, step, 0)

        m_inter = bcol + mp_s[...][:, :, 0:1]
        m_t = jnp.maximum(m_inter, mloc)
        w_state = jnp.exp(m_inter - m_t)
        w_loc = jnp.exp(mloc - m_t)
        inter = jnp.einsum('ctd,cde->cte', qb, cp_s[...], preferred_element_type=F32)
        num = w_state * inter + w_loc * num_loc
        den = w_state * jnp.sum(q3 * np_s[...], axis=2, keepdims=True) + w_loc * den_loc
        h = num / jnp.maximum(jnp.abs(den), jnp.exp(-m_t))
        hsum = h if hsum is None else hsum + h

    hn = hsum * lax.rsqrt(jnp.mean(hsum * hsum, axis=-1, keepdims=True) + EPS)
    og3 = split_heads(og_ref[...].astype(F32))
    for c in range(nc):
        for hh in range(heads):
            cols = slice(hh * HD_M, (hh + 1) * HD_M)
            e = c * heads + hh
            y_ref[c * blk:(c + 1) * blk, cols] = (hn[e] * gh_ref[:, cols] * jax.nn.sigmoid(og3[e])).astype(y_ref.dtype)
    if emit_state:
        cn_ref[...] = cst[...]
        nn_ref[...] = nst[...]
        mn_ref[...] = mst[...]


def mlstm(proj, gates, w_conv_qk, g_head, state, *, nb, seq, row0, heads):
    blk0 = row0 // seq
    nblk = seq // SCAN_BLOCK
    groups = HEADS_M // heads
    n_gates = 4 * HEADS_M
    zero_state = state is None
    col = lambda part: pl.BlockSpec((seq, heads * HD_M), lambda b, h: (b + blk0, part * groups + h))
    in_specs = [col(0), col(1), col(2), col(3),
                pl.BlockSpec((seq, LANE), lambda b, h: (b + blk0, 0)),
                pl.BlockSpec((3, heads * HD_M), lambda b, h: (0, h)),
                pl.BlockSpec((3, heads * HD_M), lambda b, h: (0, groups + h)),
                pl.BlockSpec((1, heads * HD_M), lambda b, h: (0, h))]
    args = [proj, proj, proj, proj, gates, w_conv_qk, w_conv_qk, g_head.reshape(1, MW)]
    state_specs = [pl.BlockSpec((None, 2, heads, HD_M, HD_M), lambda b, h: (b, 0, h, 0, 0)),
                   pl.BlockSpec((None, 2, heads, 1, HD_M), lambda b, h: (b, 0, h, 0, 0))]
    if not zero_state:
        c0, n0, m0 = state
        in_specs += state_specs + [pl.BlockSpec((None, 2, heads, 1, 1), lambda b, h: (b, 0, h, 0, 0))]
        args += [c0, n0.reshape(nb, 2, HEADS_M, 1, HD_M), m0.reshape(nb, 2, HEADS_M, 1, 1)]
    out_shape = [jax.ShapeDtypeStruct((nb * seq, MW), BF16)]
    out_specs = [pl.BlockSpec((seq, heads * HD_M), lambda b, h: (b, h))]
    if zero_state:
        out_shape += [jax.ShapeDtypeStruct((nb, 2, HEADS_M, HD_M, HD_M), F32),
                      jax.ShapeDtypeStruct((nb, 2, HEADS_M, 1, HD_M), F32),
                      jax.ShapeDtypeStruct((nb, 2, HEADS_M, 1, LANE), F32)]
        out_specs += state_specs + [pl.BlockSpec((None, 2, heads, 1, LANE), lambda b, h: (b, 0, h, 0, 0))]
    per_block = lambda *shape: pltpu.VMEM((nblk,) + shape, F32)
    per_entry = lambda *shape, dtype=F32: pltpu.VMEM((nblk * heads,) + shape, dtype)
    return pl.pallas_call(
        functools.partial(_mlstm_kernel, seq=seq, heads=heads, zero_state=zero_state, emit_state=zero_state),
        out_shape=out_shape,
        grid=(nb, groups),
        in_specs=in_specs,
        out_specs=out_specs,
        scratch_shapes=[per_block(SCAN_BLOCK, LANE), per_block(SCAN_BLOCK, LANE),
                        per_block(n_gates, SCAN_BLOCK), per_block(n_gates, SCAN_BLOCK), per_block(n_gates, SCAN_BLOCK),
                        per_entry(HD_M, HD_M), per_entry(1, HD_M), per_entry(1, LANE), per_entry(1, LANE),
                        per_entry(HD_M, HD_M, dtype=BF16), per_entry(1, HD_M), per_entry(1, LANE),
                        pltpu.VMEM((2, heads, HD_M, HD_M), F32), pltpu.VMEM((2, heads, 1, HD_M), F32),
                        pltpu.VMEM((2, heads, 1, LANE), F32)],
        compiler_params=_cparams(("arbitrary", "arbitrary")),
        name="mlstm_%d" % seq,
    )(*args)


def _hyena_pre_kernel(v_ref, x1_ref, x2_ref, wv_ref, w1_ref, w2_ref, u_ref, x2c_ref):
    first, last = _seq_edges(SEQ_BLOCK, pl.program_id(0) < N_PBLK)
    x1c = _dwconv3(x1_ref[...].astype(F32), w1_ref, first, last)
    u_ref[...] = x1c * _dwconv3(v_ref[...].astype(F32), wv_ref, first, last)
    x2c_ref[...] = _dwconv3(x2_ref[...].astype(F32), w2_ref, first, last)


def hyena_pre(proj, w_conv_hy):
    tc = 256
    nct = HW // tc
    c0 = 4 * MW // tc
    pcol = lambda part: pl.BlockSpec((SEQ_BLOCK, tc), lambda i, j: (i, c0 + part * nct + j))
    wcol = lambda part: pl.BlockSpec((3, tc), lambda i, j: (0, part * nct + j))
    out = jax.ShapeDtypeStruct((N_TOK, HW), F32)
    ospec = pl.BlockSpec((SEQ_BLOCK, tc), lambda i, j: (i, j))
    return pl.pallas_call(
        _hyena_pre_kernel,
        out_shape=[out, out],
        grid=(N_TOK // SEQ_BLOCK, nct),
        in_specs=[pcol(0), pcol(1), pcol(2), wcol(0), wcol(1), wcol(2)],
        out_specs=[ospec, ospec],
        compiler_params=_cparams(("arbitrary", "arbitrary")),
        name="hyena_pre",
    )(proj, proj, proj, w_conv_hy, w_conv_hy, w_conv_hy)


@functools.lru_cache(maxsize=None)
def _filter_tables(seq):
    t = np.linspace(0.0, 1.0, seq)[:, None]
    wpos = 2.0 * np.pi * np.arange(seq)[:, None] / seq
    bands = np.linspace(1e-4, N_BANDS - 1, N_BANDS)[None, :]
    z = np.concatenate([t, np.cos(bands * wpos), -np.sin(bands * wpos)], axis=-1)
    z = np.pad(z, ((0, 0), (0, LANE - FILTER_EMB)))
    max_decay = math.log(DECAY_TARGET) / DECAY_FAST
    min_decay = math.log(DECAY_TARGET) / DECAY_SLOW
    deltas = np.abs(np.linspace(min_decay, max_decay, HW))
    decay = np.exp(-t * np.concatenate([deltas, deltas])[None, :])
    return z.astype(np.float32), decay.astype(np.float32)


@functools.lru_cache(maxsize=None)
def _dft_tables(seq, tk):
    n = 2 * seq
    k = np.arange(seq)[:, None]
    t = np.arange(seq)[None, :]
    ang = 2.0 * np.pi * ((k * t) % n) / n
    alt = np.where(np.arange(seq) % 2 == 0, 1.0, -1.0)
    cm, sm = np.cos(ang), np.sin(ang)
    sm[0, :] = alt
    fwd = np.stack([cm.reshape(seq // tk, tk, seq), sm.reshape(seq // tk, tk, seq)], axis=1)
    wk = np.where(np.arange(seq) == 0, 1.0, 2.0)[None, :]
    ci = (np.cos(ang.T) * wk) / n
    si = np.sin(ang.T) * 2.0 / n
    si[:, 0] = alt / n
    inv = np.concatenate([ci, si], axis=1)
    return fwd.astype(np.float32), inv.astype(np.float32)


def _filter_kernel(z_ref, w1_ref, b1_ref, w2_ref, b2_ref, w3_ref, fr_ref, dec_ref, hs_ref, hd_ref):
    fr = fr_ref[...]
    hp = functools.partial(jnp.dot, precision=HIGHEST, preferred_element_type=F32)
    h1 = jnp.sin(fr * (hp(z_ref[...], w1_ref[...]) + b1_ref[...]))
    h2 = jnp.sin(fr * (hp(h1, w2_ref[...]) + b2_ref[...]))
    filt = hp(h2, w3_ref[...]) * dec_ref[...]
    past, fut = filt[:, :HW], filt[:, HW:]
    rows = filt.shape[0]
    grow = lax.broadcasted_iota(jnp.int32, (rows, 1), 0) + pl.program_id(0) * rows
    fut = jnp.where(grow == 0, 0.0, fut)
    hs_ref[...] = past + fut
    hd_ref[...] = past - fut


def filter_gen(seq, w1, b1, w2, b2, w3, freq):
    z, decay = _filter_tables(seq)
    tl = 256
    fh = FILTER_HIDDEN
    full = lambda shape: pl.BlockSpec(shape, lambda i: (0, 0))
    out = jax.ShapeDtypeStruct((seq, HW), F32)
    return pl.pallas_call(
        _filter_kernel,
        out_shape=[out, out],
        grid=(seq // tl,),
        in_specs=[pl.BlockSpec((tl, LANE), lambda i: (i, 0)), full((LANE, fh)), full((1, fh)), full((fh, fh)),
                  full((1, fh)), full((fh, 2 * HW)), full((1, fh)), pl.BlockSpec((tl, 2 * HW), lambda i: (i, 0))],
        out_specs=[pl.BlockSpec((tl, HW), lambda i: (i, 0))] * 2,
        compiler_params=_cparams(("arbitrary",)),
        name="filter_gen",
    )(jnp.asarray(z), jnp.pad(w1, ((0, LANE - FILTER_EMB), (0, 0))), b1.reshape(1, fh), w2, b2.reshape(1, fh), w3,
      freq.reshape(1, fh), jnp.asarray(decay))


def _dft_filter_kernel(a_ref, hs_ref, hd_ref, k_ref, hs_bf, hd_bf):
    @pl.when(pl.program_id(0) == 0)
    def _():
        hs_bf[...] = hs_ref[...].astype(BF16)
        hd_bf[...] = hd_ref[...].astype(BF16)

    k_ref[0] = _dot(a_ref[0], hs_bf[...])
    k_ref[1] = _dot(a_ref[1], hd_bf[...])

    @pl.when(pl.program_id(0) == 0)
    def _():
        k_ref[1, 0:1, :] = _dot(a_ref[1, 0:8, :], hs_bf[...])[0:1, :]


def dft_filter(seq, tk, hs, hd):
    fwd, _ = _dft_tables(seq, tk)
    return pl.pallas_call(
        _dft_filter_kernel,
        out_shape=jax.ShapeDtypeStruct((2, seq, HW), F32),
        grid=(seq // tk,),
        in_specs=[pl.BlockSpec((None, 2, tk, seq), lambda m: (m, 0, 0, 0)),
                  pl.BlockSpec((seq, HW), lambda m: (0, 0)), pl.BlockSpec((seq, HW), lambda m: (0, 0))],
        out_specs=pl.BlockSpec((2, tk, HW), lambda m: (0, m, 0)),
        scratch_shapes=[pltpu.VMEM((seq, HW), BF16), pltpu.VMEM((seq, HW), BF16)],
        compiler_params=_cparams(("arbitrary",)),
        name="dft_filter",
    )(jnp.asarray(fwd).astype(BF16), hs, hd)


def _dft_fwd_kernel(a_ref, u_ref, k_ref, y_ref, u_bf):
    m = pl.program_id(1)

    @pl.when(m == 0)
    def _():
        u_bf[...] = u_ref[...].astype(BF16)

    ure = _dot(a_ref[0], u_bf[...])
    uim = _dot(a_ref[1], u_bf[...])
    kre, kim = k_ref[0], k_ref[1]
    packed = (lax.broadcasted_iota(jnp.int32, (ure.shape[0], 1), 0) == 0) & (m == 0)
    y_ref[0] = jnp.where(packed, ure * kre, ure * kre - uim * kim).astype(y_ref.dtype)
    y_ref[1] = jnp.where(packed, uim * kim, ure * kim + uim * kre).astype(y_ref.dtype)


def dft_fwd(seq, tk, u, kf, *, nb, row0):
    fwd, _ = _dft_tables(seq, tk)
    blk0 = row0 // seq
    return pl.pallas_call(
        _dft_fwd_kernel,
        out_shape=jax.ShapeDtypeStruct((nb, 2, seq, HW), BF16),
        grid=(nb, seq // tk),
        in_specs=[pl.BlockSpec((None, 2, tk, seq), lambda b, m: (m, 0, 0, 0)),
                  pl.BlockSpec((seq, HW), lambda b, m: (b + blk0, 0)),
                  pl.BlockSpec((2, tk, HW), lambda b, m: (0, m, 0))],
        out_specs=pl.BlockSpec((None, 2, tk, HW), lambda b, m: (b, 0, m, 0)),
        scratch_shapes=[pltpu.VMEM((seq, HW), BF16)],
        compiler_params=_cparams(("arbitrary", "arbitrary")),
        name="dft_fwd",
    )(jnp.asarray(fwd).astype(BF16), u, kf)


def _dft_inv_kernel(a_ref, y_ref, u_ref, x2_ref, bias_ref, o_ref, *, seq):
    conv = _dot(a_ref[:, :seq], y_ref[0]) + _dot(a_ref[:, seq:], y_ref[1])
    o_ref[...] = (x2_ref[...] * (conv + bias_ref[...] * u_ref[...])).astype(o_ref.dtype)


def dft_inv(seq, tk, y, u, x2c, bias, *, nb, row0):
    _, inv = _dft_tables(seq, tk)
    tm = min(seq, 512)
    nt = seq // tm
    blk0 = row0 // tm
    rows = lambda b, t: (b * nt + t + blk0, 0)
    return pl.pallas_call(
        functools.partial(_dft_inv_kernel, seq=seq),
        out_shape=jax.ShapeDtypeStruct((nb * seq, HW), BF16),
        grid=(nb, nt),
        in_specs=[pl.BlockSpec((tm, 2 * seq), lambda b, t: (t, 0)),
                  pl.BlockSpec((None, 2, seq, HW), lambda b, t: (b, 0, 0, 0)),
                  pl.BlockSpec((tm, HW), rows), pl.BlockSpec((tm, HW), rows),
                  pl.BlockSpec((1, HW), lambda b, t: (0, 0))],
        out_specs=pl.BlockSpec((tm, HW), lambda b, t: (b * nt + t, 0)),
        compiler_params=_cparams(("arbitrary", "arbitrary")),
        name="dft_inv",
    )(jnp.asarray(inv).astype(BF16), y, u, x2c, bias.reshape(1, HW))


def hyena_group(seq, u, x2c, filt_w, bias, *, nb, row0):
    tk = min(seq, 512)
    hs, hd = filter_gen(seq, *filt_w)
    kf = dft_filter(seq, tk, hs, hd)
    y = dft_fwd(seq, tk, u, kf, nb=nb, row0=row0)
    return dft_inv(seq, tk, y, u, x2c, bias, nb=nb, row0=row0)


def mixer_ab_parts(proj, gates, state_s, w_conv_qk, g_head, w_conv_hy, w_f1, b_f1, w_f2, b_f2, w_f3, freq, hy_bias):
    ym_p, c_p, n_p, m_p = mlstm(proj, gates, w_conv_qk, g_head, None, nb=BATCH, seq=SEQ, row0=0, heads=HEADS_M)
    (ym_s,) = mlstm(proj, gates, w_conv_qk, g_head, state_s, nb=DEC_BATCH, seq=DEC_SEQ, row0=N_P, heads=1)
    u, x2c = hyena_pre(proj, w_conv_hy)
    filt_w = (w_f1, b_f1, w_f2, b_f2, w_f3, freq)
    yh_p = hyena_group(SEQ, u, x2c, filt_w, hy_bias, nb=BATCH, row0=0)
    yh_s = hyena_group(DEC_SEQ, u, x2c, filt_w, hy_bias, nb=DEC_BATCH, row0=N_P)
    state_p = (c_p, n_p[:, :, :, 0, :], m_p[:, :, :, 0, 0])
    return (ym_p, ym_s), (yh_p, yh_s), state_p


def kernel(x_prompt, x_sample, state_mlstm_C, state_mlstm_n, state_mlstm_m, cache_na_k, cache_na_v, c, c_ctx, w_ada, b_ada, g_mix, g_ffn, g_final, w_in_ab, b_gates, w_conv_qk, g_mlstm, w_conv_hy, w_filt1, b_filt1, w_filt2, b_filt2, w_filt3, filt_freq, hyena_bias, w_out_ab, w_in_c, rpb_c, w_out_c, w_up, w_conv_ffn, w_down):
    cmat = jnp.concatenate([c, c_ctx[None, :], jnp.zeros((MOD_ROWS - DEC_BATCH - 1, D), F32)], axis=0)
    mod_all = adaln_all(cmat, w_ada, b_ada).reshape(DEPTH, MOD_ROWS, 6, 1, D)
    x = (x_prompt.reshape(N_P, D), x_sample.reshape(N_S, D))
    h = None
    new_c, new_n, new_m, new_k, new_v = [], [], [], [], []
    for l in range(DEPTH):
        e = l // 2
        if l % 2 == 0:
            assert l == 0, "a later mixer A/B layer would take the fused norm of the layer before it"
            proj, gates = ab_in(x, g_mix, mod_all, l, w_in_ab, b_gates[e], e)
            state_s = (state_mlstm_C[:, e], state_mlstm_n[:, e], state_mlstm_m[:, e])
            y_m, y_h, (c_p, n_p, m_p) = mixer_ab_parts(
                proj, gates, state_s, w_conv_qk[e], g_mlstm[e], w_conv_hy[e], w_filt1[e], b_filt1[e],
                w_filt2[e], b_filt2[e], w_filt3[e], filt_freq[e], hyena_bias[e])
            a_list = [y_m, y_h]
            w_list = [(w_out_ab, (None, MW, D), lambda i, e=e: (e, 0, 0)),
                      (w_out_ab, (None, HW, D), lambda i, e=e: (e, 1, 0))]
            new_c.append(c_p[:, None])
            new_n.append(n_p[:, None])
            new_m.append(m_p[:, None])
        else:
            o, k_new, v_new = mixer_c(h, cache_na_k[:, e], cache_na_v[:, e], w_in_c, e, rpb_c[e])
            a_list = [o]
            w_list = [(w_out_c, (None, D, D), lambda i, e=e: (e, 0, 0))]
            new_k.append(k_new)
            new_v.append(v_new)
        x, h = out_proj(a_list, w_list, x, mod_all, l, 2, g_ffn, l, (l, 3, 4), name="mixer_out")
        mid = ffn_up(h, w_up, w_conv_ffn, l)
        w_list = [(w_down, (None, FF, D), lambda i, l=l: (l, 0, 0))]
        if l + 1 < DEPTH:
            x, h = out_proj([mid], w_list, x, mod_all, l, 5, g_mix, l + 1, (l + 1, 0, 1), name="ffn_down")
        else:
            y_p, y_s = out_proj([mid], w_list, x, mod_all, l, 5, g_final, 0, None, name="ffn_down_final")
    cat = lambda parts: parts[0] if len(parts) == 1 else jnp.concatenate(parts, axis=1)
    return (y_p.reshape(BATCH, SEQ, D), y_s.reshape(DEC_BATCH, DEC_SEQ, D), cat(new_c), cat(new_n), cat(new_m),
            cat(new_k), cat(new_v))
```

```python
import functools
import math

import numpy as np
import jax
import jax.numpy as jnp
from jax import lax
from jax.experimental import pallas as pl
from jax.experimental.pallas import tpu as pltpu

F32 = jnp.float32
BF16 = jnp.bfloat16

D = 1024
BATCH, SEQ = 16, 256
DEC_BATCH, DEC_SEQ = 4, 2048
PAST_LEN = 512
DEPTH = 2
GRID_W = 64
GRID_R = DEC_SEQ // GRID_W
HEADS_M = 4
MW = D // 2
HD_M = MW // HEADS_M
CHUNK = 64
HW = D // 2
N_BANDS = 16
FILTER_EMB = 2 * N_BANDS + 1
FILTER_HIDDEN = 64
DECAY_FAST, DECAY_SLOW, DECAY_TARGET = 0.3, 1.5, 1e-2
NA_HEADS = 16
NA_HD = D // NA_HEADS
NA_KH, NA_KW = 8, 16
FF = 2816
EPS = 1e-6

N_P = BATCH * SEQ
N_S = DEC_BATCH * DEC_SEQ
N_TOK = N_P + N_S
CTX_ROW = DEC_BATCH
MOD_ROWS = 8
LANE = 128
VMEM_LIMIT = 48 * 1024 * 1024
QKV_VMEM_LIMIT = 56 * 1024 * 1024
DFT_VMEM_LIMIT = 56 * 1024 * 1024
HIGHEST = lax.Precision.HIGHEST


def _cparams(sem, vmem_limit=VMEM_LIMIT):
    return pltpu.CompilerParams(dimension_semantics=sem, vmem_limit_bytes=vmem_limit)


def _mod_row(i, bm):
    return jnp.where(i < N_P // bm, CTX_ROW, (i - N_P // bm) // (DEC_SEQ // bm))


def _dot(a, b):
    return jnp.dot(a, b, preferred_element_type=F32)


def _adaln_kernel(c_ref, w_ref, b_ref, o_ref):
    cv = c_ref[...]
    s = cv * jax.nn.sigmoid(cv)
    o_ref[...] = _dot(s.astype(BF16), w_ref[...].astype(BF16)) + b_ref[...]


def adaln_all(cmat, w_ada, b_ada):
    tn = 1024
    return pl.pallas_call(
        _adaln_kernel,
        out_shape=jax.ShapeDtypeStruct((DEPTH, MOD_ROWS, 6 * D), F32),
        grid=(DEPTH, 6 * D // tn),
        in_specs=[
            pl.BlockSpec((MOD_ROWS, D), lambda l, j: (0, 0)),
            pl.BlockSpec((None, D, tn), lambda l, j: (l, 0, j)),
            pl.BlockSpec((None, 1, tn), lambda l, j: (l, 0, j)),
        ],
        out_specs=pl.BlockSpec((None, MOD_ROWS, tn), lambda l, j: (l, 0, j)),
        compiler_params=_cparams(("arbitrary", "arbitrary")),
        name="adaln",
    )(cmat, w_ada, b_ada.reshape(DEPTH, 1, 6 * D))


ROW_BM = 512
ROW_NPB = N_P // ROW_BM


def _row_specs(arr, cols):
    if isinstance(arr, tuple):
        return ([pl.BlockSpec((ROW_BM, cols), lambda i: (jnp.minimum(i, ROW_NPB - 1), 0)),
                 pl.BlockSpec((ROW_BM, cols), lambda i: (jnp.maximum(i - ROW_NPB, 0), 0))], list(arr))
    return [pl.BlockSpec((ROW_BM, cols), lambda i: (i, 0))], [arr]


def _pick_rows(refs):
    if len(refs) == 1:
        return refs[0][...]
    return jnp.where(pl.program_id(0) < ROW_NPB, refs[0][...], refs[1][...])


def _mod_spec(layer, which):
    return pl.BlockSpec((None, None, None, 1, D), lambda i: (layer, _mod_row(i, ROW_BM), which, 0, 0))


def _layer_vec_spec(layer):
    return pl.BlockSpec((None, 1, D), lambda i: (layer, 0, 0))


def _rms_mod(x, g_ref, sh_ref, sc_ref):
    y = x * lax.rsqrt(jnp.mean(x * x, axis=-1, keepdims=True) + EPS)
    return (y * g_ref[...]) * (1.0 + sc_ref[...]) + sh_ref[...]


def _out_proj_kernel(*refs, a_counts, n_x, final):
    pos = 0
    a_groups = []
    for cnt in a_counts:
        a_groups.append(refs[pos:pos + cnt])
        pos += cnt
    w_refs = refs[pos:pos + len(a_counts)]
    pos += len(a_counts)
    x_refs = refs[pos:pos + n_x]
    pos += n_x
    gt_ref, g_ref = refs[pos:pos + 2]
    pos += 2
    if not final:
        sh_ref, sc_ref = refs[pos:pos + 2]
        pos += 2
    out_a, out_b = refs[pos:pos + 2]
    wbf = refs[pos + 2:]

    @pl.when(pl.program_id(0) == 0)
    def _():
        for w_ref, wb in zip(w_refs, wbf):
            wb[...] = w_ref[...].astype(BF16)

    acc = None
    for group, wb in zip(a_groups, wbf):
        part = _dot(_pick_rows(group), wb[...])
        acc = part if acc is None else acc + part
    xn = _pick_rows(x_refs) + gt_ref[...] * acc
    if final:
        y = xn * lax.rsqrt(jnp.mean(xn * xn, axis=-1, keepdims=True) + EPS) * g_ref[...]

        @pl.when(pl.program_id(0) < ROW_NPB)
        def _():
            out_a[...] = y

        @pl.when(pl.program_id(0) >= ROW_NPB)
        def _():
            out_b[...] = y
    else:
        out_a[...] = xn
        out_b[...] = _rms_mod(xn, g_ref, sh_ref, sc_ref).astype(out_b.dtype)


def out_proj(a_list, w_list, x, mod_all, layer, gt_idx, g_all, g_layer, norm_mod_idx=None, name="out_proj"):
    final = norm_mod_idx is None
    in_specs, args, a_counts = [], [], []
    for a in a_list:
        cols = (a[0] if isinstance(a, tuple) else a).shape[1]
        specs, ops = _row_specs(a, cols)
        in_specs += specs
        args += ops
        a_counts.append(len(ops))
    w_shapes = []
    for w, block, imap in w_list:
        in_specs.append(pl.BlockSpec(block, imap, pipeline_mode=pl.Buffered(1)))
        args.append(w)
        w_shapes.append(tuple(b for b in block if b is not None))
    x_specs, x_args = _row_specs(x, D)
    in_specs += x_specs + [_mod_spec(layer, gt_idx), _layer_vec_spec(g_layer)]
    args += x_args + [mod_all, g_all.reshape(-1, 1, D)]
    if final:
        out_shape = [jax.ShapeDtypeStruct((N_P, D), F32), jax.ShapeDtypeStruct((N_S, D), F32)]
        out_specs = [pl.BlockSpec((ROW_BM, D), lambda i: (jnp.minimum(i, ROW_NPB - 1), 0)),
                     pl.BlockSpec((ROW_BM, D), lambda i: (jnp.maximum(i - ROW_NPB, 0), 0))]
    else:
        n_layer, sh_idx, sc_idx = norm_mod_idx
        in_specs += [_mod_spec(n_layer, sh_idx), _mod_spec(n_layer, sc_idx)]
        args += [mod_all, mod_all]
        out_shape = [jax.ShapeDtypeStruct((N_TOK, D), F32), jax.ShapeDtypeStruct((N_TOK, D), BF16)]
        out_specs = [pl.BlockSpec((ROW_BM, D), lambda i: (i, 0))] * 2
    return pl.pallas_call(
        functools.partial(_out_proj_kernel, a_counts=tuple(a_counts), n_x=len(x_args), final=final),
        out_shape=out_shape,
        grid=(N_TOK // ROW_BM,),
        in_specs=in_specs,
        out_specs=out_specs,
        scratch_shapes=[pltpu.VMEM(s, BF16) for s in w_shapes],
        compiler_params=_cparams(("arbitrary",)),
        name=name,
    )(*args)


def _qkv_kernel(a_ref, w_ref, o_ref, kc_ref, vc_ref, wbf):
    @pl.when(pl.program_id(0) == 0)
    def _():
        wbf[...] = w_ref[...].astype(BF16)

    a = a_ref[...]
    for part in range(3):
        acc = _dot(a, wbf[:, part * D:(part + 1) * D])
        for pp in range(NA_HEADS // 2):
            o_ref[part, pp] = acc[:, pp * 2 * NA_HD:(pp + 1) * 2 * NA_HD].astype(o_ref.dtype)
        if part > 0:
            c_ref = kc_ref if part == 1 else vc_ref

            @pl.when(pl.program_id(0) < ROW_NPB)
            def _(acc=acc, c_ref=c_ref):
                for b in range(ROW_BM // SEQ):
                    for hh in range(NA_HEADS):
                        c_ref[b, hh] = acc[b * SEQ:(b + 1) * SEQ, hh * NA_HD:(hh + 1) * NA_HD]


def qkv_proj(h, w_in, layer):
    seqs = ROW_BM // SEQ
    cache = jax.ShapeDtypeStruct((BATCH, 1, NA_HEADS, SEQ, NA_HD), F32)
    cache_spec = pl.BlockSpec((seqs, None, NA_HEADS, SEQ, NA_HD),
                              lambda i: (jnp.minimum(i, ROW_NPB - 1), 0, 0, 0, 0))
    return pl.pallas_call(
        _qkv_kernel,
        out_shape=[jax.ShapeDtypeStruct((3, NA_HEADS // 2, N_TOK, 2 * NA_HD), BF16), cache, cache],
        grid=(N_TOK // ROW_BM,),
        in_specs=[pl.BlockSpec((ROW_BM, D), lambda i: (i, 0)),
                  pl.BlockSpec((None, D, 3 * D), lambda i: (layer, 0, 0), pipeline_mode=pl.Buffered(1))],
        out_specs=[pl.BlockSpec((3, NA_HEADS // 2, ROW_BM, 2 * NA_HD), lambda i: (0, 0, i, 0)),
                   cache_spec, cache_spec],
        scratch_shapes=[pltpu.VMEM((D, 3 * D), BF16)],
        compiler_params=_cparams(("arbitrary",), QKV_VMEM_LIMIT),
        name="qkv_proj",
    )(h, w_in)


AB_MAIN = 4 * MW + 3 * HW
N_GATES = 4 * HEADS_M


def _ab_in_kernel(*refs):
    x_refs = refs[:-8]
    g_ref, sh_ref, sc_ref, w_ref, b_ref, proj_ref, gates_ref, wbf = refs[-8:]

    @pl.when(pl.program_id(0) == 0)
    def _():
        wbf[:, :4 * MW] = w_ref[:, :4 * MW].astype(BF16)
        wbf[:, 4 * MW:AB_MAIN] = w_ref[:, 4 * MW + N_GATES:].astype(BF16)
        lane = lax.broadcasted_iota(jnp.int32, (1, LANE), 1)
        wbf[:, AB_MAIN:] = jnp.where(lane < N_GATES, w_ref[:, 4 * MW:4 * MW + LANE], 0.0).astype(BF16)

    h = _rms_mod(_pick_rows(x_refs), g_ref, sh_ref, sc_ref).astype(BF16)
    proj_ref[...] = _dot(h, wbf[:, :AB_MAIN]).astype(proj_ref.dtype)
    gates_ref[...] = _dot(h, wbf[:, AB_MAIN:]) + b_ref[...]


def ab_in(x, g_all, mod_all, layer, w_in, b_gates, e):
    x_specs, x_args = _row_specs(x, D)
    b_pad = jnp.pad(b_gates, (0, LANE - N_GATES)).reshape(1, LANE)
    return pl.pallas_call(
        _ab_in_kernel,
        out_shape=[jax.ShapeDtypeStruct((N_TOK, AB_MAIN), BF16), jax.ShapeDtypeStruct((N_TOK, LANE), F32)],
        grid=(N_TOK // ROW_BM,),
        in_specs=x_specs + [_layer_vec_spec(layer), _mod_spec(layer, 0), _mod_spec(layer, 1),
                            pl.BlockSpec((None, D, w_in.shape[2]), lambda i: (e, 0, 0),
                                         pipeline_mode=pl.Buffered(1)),
                            pl.BlockSpec((1, LANE), lambda i: (0, 0))],
        out_specs=[pl.BlockSpec((ROW_BM, AB_MAIN), lambda i: (i, 0)), pl.BlockSpec((ROW_BM, LANE), lambda i: (i, 0))],
        scratch_shapes=[pltpu.VMEM((D, AB_MAIN + LANE), BF16)],
        compiler_params=_cparams(("arbitrary",)),
        name="ab_in",
    )(*x_args, g_all.reshape(-1, 1, D), mod_all, mod_all, w_in, b_pad)


SEQ_BLOCK = DEC_SEQ
N_PBLK = N_P // SEQ_BLOCK


def _seq_edges(rows, is_prompt):
    r = lax.broadcasted_iota(jnp.int32, (rows, 1), 0)
    first = (r == 0) | (is_prompt & (r % SEQ == 0))
    last = (r == rows - 1) | (is_prompt & (r % SEQ == SEQ - 1))
    return first, last


def _dwconv3(x, w_ref, first, last):
    rows = x.shape[0]
    prev = jnp.where(first, 0.0, pltpu.roll(x, 1, 0))
    nxt = jnp.where(last, 0.0, pltpu.roll(x, rows - 1, 0))
    return prev * w_ref[0:1, :] + x * w_ref[1:2, :] + nxt * w_ref[2:3, :]


def _gated_gelu_of_half(y, g):
    c = math.sqrt(2.0 / math.pi)
    t = jnp.tanh(y * (2.0 * c + (8.0 * 0.044715 * c) * (y * y)))
    return (y + y * t) * g


def _ffn_up_kernel(h_ref, wa_ref, wg_ref, wc_ref, o_ref):
    first, last = _seq_edges(SEQ_BLOCK, pl.program_id(0) < N_PBLK)
    h = h_ref[...]
    a = _dot(h, wa_ref[...].astype(BF16))
    g = _dot(h, wg_ref[...].astype(BF16))
    half_conv = _dwconv3(a, 0.5 * wc_ref[...], first, last)
    o_ref[...] = _gated_gelu_of_half(half_conv, g).astype(o_ref.dtype)


def ffn_up(h, w_up, w_conv, layer):
    tc = 256
    nct = FF // tc
    return pl.pallas_call(
        _ffn_up_kernel,
        out_shape=jax.ShapeDtypeStruct((N_TOK, FF), BF16),
        grid=(N_TOK // SEQ_BLOCK, nct),
        in_specs=[
            pl.BlockSpec((SEQ_BLOCK, D), lambda i, j: (i, 0)),
            pl.BlockSpec((None, D, tc), lambda i, j: (layer, 0, j)),
            pl.BlockSpec((None, D, tc), lambda i, j: (layer, 0, j + nct)),
            pl.BlockSpec((None, 3, tc), lambda i, j: (layer, 0, j)),
        ],
        out_specs=pl.BlockSpec((SEQ_BLOCK, tc), lambda i, j: (i, j)),
        compiler_params=_cparams(("arbitrary", "arbitrary")),
        name="ffn_up",
    )(h, w_up, w_up, w_conv)


HEAD_PAIRS = NA_HEADS // 2
NT_DIMS = (((1,), (1,)), ((), ()))


def _pair_mask(shape):
    return lax.broadcasted_iota(jnp.int32, shape, len(shape) - 1) < NA_HD


def _one_head(x2, first):
    keep = _pair_mask(x2.shape) if first else ~_pair_mask(x2.shape)
    return jnp.where(keep, x2, jnp.zeros_like(x2))


def _ctx_attn_kernel(q_ref, k_ref, v_ref, o_ref, *, pairs):
    outs = []
    for pp in range(pairs):
        q2, k2, v2 = q_ref[pp], k_ref[pp], v_ref[pp]
        res = []
        for first in (True, False):
            s = lax.dot_general(_one_head(q2, first), k2, NT_DIMS, preferred_element_type=F32) * (NA_HD ** -0.5)
            m = jnp.max(s, axis=-1, keepdims=True)
            p = jnp.exp(s - m)
            l = jnp.sum(p, axis=-1, keepdims=True)
            res.append(_dot(p.astype(BF16), v2) / l)
        outs.append(jnp.where(_pair_mask(res[0].shape), res[0], res[1]))
    o_ref[...] = jnp.concatenate(outs, axis=-1).astype(o_ref.dtype)


def ctx_attention(qkv):
    pairs = 4
    spec = lambda part: pl.BlockSpec((None, pairs, SEQ, 2 * NA_HD), lambda b, h: (part, h, b, 0))
    return pl.pallas_call(
        functools.partial(_ctx_attn_kernel, pairs=pairs),
        out_shape=jax.ShapeDtypeStruct((N_P, D), BF16),
        grid=(BATCH, HEAD_PAIRS // pairs),
        in_specs=[spec(0), spec(1), spec(2)],
        out_specs=pl.BlockSpec((SEQ, pairs * 2 * NA_HD), lambda b, h: (b, h)),
        compiler_params=_cparams(("arbitrary", "arbitrary")),
        name="ctx_attn",
    )(qkv, qkv, qkv)


def _na_tables():
    q = np.arange(GRID_W)[:, None]
    w = np.arange(GRID_W)[None, :]
    idx_c = np.clip(w - q + (NA_KW - 1), 0, 2 * NA_KW - 2)
    onehot = (idx_c.reshape(1, -1) == np.arange(32)[:, None]).astype(np.float32)
    c_start = np.clip(np.arange(GRID_W) - NA_KW // 2, 0, GRID_W - NA_KW)[:, None]
    inside = (w >= c_start) & (w < c_start + NA_KW)
    cmask = np.where(inside, 0.0, -np.inf).astype(np.float32)
    return onehot, np.tile(cmask, (1, 2))


def _rpb_expand_kernel(r_ref, e_ref, o_ref):
    o_ref[...] = jnp.dot(r_ref[...], e_ref[...], precision=HIGHEST, preferred_element_type=F32)


def rpb_expand(rpb):
    onehot, _ = _na_tables()
    rp = jnp.pad(rpb, ((0, 0), (0, 1), (0, 1)))
    return pl.pallas_call(
        _rpb_expand_kernel,
        out_shape=jax.ShapeDtypeStruct((NA_HEADS, 16, GRID_W * GRID_W), F32),
        grid=(NA_HEADS,),
        in_specs=[pl.BlockSpec((None, 16, 32), lambda h: (h, 0, 0)),
                  pl.BlockSpec((32, GRID_W * GRID_W), lambda h: (0, 0))],
        out_specs=pl.BlockSpec((None, 16, GRID_W * GRID_W), lambda h: (h, 0, 0)),
        compiler_params=_cparams(("arbitrary",)),
        name="rpb_expand",
    )(rp, jnp.asarray(onehot))


NA_QROWS = 8
NA_WIN = 2 * NA_QROWS


def _na_attn_kernel(q_ref, k_ref, v_ref, kc_ref, vc_ref, t_ref, o_ref, p_loc, p_ctx):
    nq = NA_QROWS * GRID_W
    nk = NA_WIN * GRID_W
    pair = 2 * GRID_W
    nt = NT_DIMS
    lane = lax.broadcasted_iota(jnp.int32, (GRID_W, pair), 1)
    zero_tile = jnp.zeros((GRID_W, pair), BF16)
    kc = jnp.concatenate([kc_ref[0], kc_ref[1]], axis=-1).astype(BF16)
    vc = jnp.concatenate([vc_ref[0], vc_ref[1]], axis=-1).astype(BF16)
    for blk in range(GRID_R // NA_QROWS):
        k0 = min(max(NA_QROWS * blk - NA_KH // 2, 0), GRID_R - NA_WIN)
        q2 = q_ref[blk * nq:(blk + 1) * nq, :] * (NA_HD ** -0.5)
        kw = k_ref[k0 * GRID_W:k0 * GRID_W + nk, :]
        vw = v_ref[k0 * GRID_W:k0 * GRID_W + nk, :]
        outs = []
        for hh in range(2):
            q = _one_head(q2, hh == 0)
            s_loc = lax.dot_general(q, kw, nt, preferred_element_type=F32)
            s_ctx = lax.dot_general(q, kc, nt, preferred_element_type=F32)
            denoms = []
            for qi in range(NA_QROWS):
                r = NA_QROWS * blk + qi
                r_start = min(max(r - NA_KH // 2, 0), GRID_R - NA_KH)
                rows = slice(qi * GRID_W, (qi + 1) * GRID_W)
                ctx_tiles = [s_ctx[rows, c * pair:(c + 1) * pair] for c in range(PAST_LEN // pair)]
                tiles = {}
                for j in range(nk // pair):
                    kr = k0 + 2 * j
                    ok0 = r_start <= kr < r_start + NA_KH
                    ok1 = r_start <= kr + 1 < r_start + NA_KH
                    if not (ok0 or ok1):
                        continue
                    sb = s_loc[rows, j * pair:(j + 1) * pair] + t_ref[hh, kr - r + NA_KH]
                    if not (ok0 and ok1):
                        sb = jnp.where((lane < GRID_W) if ok0 else (lane >= GRID_W), sb, -jnp.inf)
                    tiles[j] = sb
                mx = functools.reduce(jnp.maximum, list(tiles.values()) + ctx_tiles)
                m = jnp.max(mx, axis=1, keepdims=True)
                acc = None
                for j in range(nk // pair):
                    if j in tiles:
                        p = jnp.exp(tiles[j] - m)
                        acc = p if acc is None else acc + p
                        p_loc[rows, j * pair:(j + 1) * pair] = p.astype(BF16)
                    else:
                        p_loc[rows, j * pair:(j + 1) * pair] = zero_tile
                for c, t in enumerate(ctx_tiles):
                    p = jnp.exp(t - m)
                    acc = acc + p
                    p_ctx[rows, c * pair:(c + 1) * pair] = p.astype(BF16)
                denoms.append(jnp.sum(acc, axis=1, keepdims=True))
            outs.append((_dot(p_loc[...], vw) + _dot(p_ctx[...], vc)) / jnp.concatenate(denoms, axis=0))
        o_ref[blk * nq:(blk + 1) * nq, :] = jnp.where(_pair_mask(outs[0].shape), outs[0], outs[1]).astype(o_ref.dtype)


def na_attention(qkv, k_ctx, v_ctx, bias_pairs):
    blk0 = N_P // DEC_SEQ
    spec = lambda part: pl.BlockSpec((None, None, DEC_SEQ, 2 * NA_HD), lambda h, b: (part, h, b + blk0, 0))
    cspec = pl.BlockSpec((None, 2, PAST_LEN, NA_HD), lambda h, b: (b, h, 0, 0))
    return pl.pallas_call(
        _na_attn_kernel,
        out_shape=jax.ShapeDtypeStruct((N_S, D), BF16),
        grid=(HEAD_PAIRS, DEC_BATCH),
        in_specs=[spec(0), spec(1), spec(2), cspec, cspec,
                  pl.BlockSpec((2, 16, GRID_W, 2 * GRID_W), lambda h, b: (h, 0, 0, 0))],
        out_specs=pl.BlockSpec((DEC_SEQ, 2 * NA_HD), lambda h, b: (b, h)),
        scratch_shapes=[pltpu.VMEM((NA_QROWS * GRID_W, NA_WIN * GRID_W), BF16),
                        pltpu.VMEM((NA_QROWS * GRID_W, PAST_LEN), BF16)],
        compiler_params=_cparams(("arbitrary", "arbitrary")),
        name="na_attn",
    )(qkv, qkv, qkv, k_ctx, v_ctx, bias_pairs)


def mixer_c(h, k_ctx, v_ctx, w_in_all, layer, rpb):
    qkv, k_new, v_new = qkv_proj(h, w_in_all, layer)
    o_p = ctx_attention(qkv)
    _, cmask2 = _na_tables()
    b15 = rpb_expand(rpb).reshape(NA_HEADS, 16, GRID_W, GRID_W)
    b17 = jnp.pad(b15, ((0, 0), (1, 0), (0, 0), (0, 0)))
    bias_pairs = jnp.concatenate([b17[:, :16], b17[:, 1:]], axis=-1) + jnp.asarray(cmask2)
    o_s = na_attention(qkv, k_ctx, v_ctx, bias_pairs)
    return (o_p, o_s), k_new, v_new


SCAN_BLOCK = HD_M


def _mlstm_kernel(*refs, seq, heads, zero_state, emit_state):
    q_ref, k_ref, v_ref, og_ref, gates_ref, wq_ref, wk_ref, gh_ref = refs[:8]
    pos = 8
    if not zero_state:
        c0_ref, n0_ref, m0_ref = refs[pos:pos + 3]
        pos += 3
    y_ref = refs[pos]
    pos += 1
    if emit_state:
        cn_ref, nn_ref, mn_ref = refs[pos:pos + 3]
        pos += 3
    pre_s, suf_s, gt_s, pret_s, suft_s, kv_s, ks_s, be_s, mk_s, cp_s, np_s, mp_s, cst, nst, mst = refs[pos:]

    blk = SCAN_BLOCK
    nc = seq // blk
    nbatch = nc * heads
    n_gates = 4 * HEADS_M
    r = lax.broadcasted_iota(jnp.int32, (seq, 1), 0)

    @pl.when(pl.program_id(1) == 0)
    def _():
        g_all = gates_ref[...]
        lf = jax.nn.log_sigmoid(g_all)
        rin = r % blk
        pre, suf = lf, lf
        for sh in [1 << i for i in range(blk.bit_length() - 1)]:
            pre = pre + jnp.where(rin >= sh, pltpu.roll(pre, sh, 0), 0.0)
            suf = suf + jnp.where(rin < blk - sh, pltpu.roll(suf, seq - sh, 0), 0.0)
        pre3, suf3 = pre.reshape(nc, blk, LANE), suf.reshape(nc, blk, LANE)
        pre_s[...] = pre3
        suf_s[...] = suf3
        gt_s[...] = jnp.swapaxes(g_all.reshape(nc, blk, LANE), 1, 2)[:, :n_gates, :]
        pret_s[...] = jnp.swapaxes(pre3, 1, 2)[:, :n_gates, :]
        suft_s[...] = jnp.swapaxes(suf3, 1, 2)[:, :n_gates, :]

    def split_heads(x):
        x3 = x.reshape(nc, blk, heads * HD_M)
        if heads == 1:
            return x3
        tiles = jnp.stack([x3[:, :, hh * HD_M:(hh + 1) * HD_M] for hh in range(heads)], axis=1)
        return tiles.reshape(nbatch, blk, HD_M)

    def per_head(x):
        if heads == 1:
            return x
        return jnp.broadcast_to(x[:, None], (nc, heads) + x.shape[1:]).reshape((nbatch,) + x.shape[1:])

    first, last = r == 0, r == seq - 1
    qc_all = _dwconv3(q_ref[...].astype(F32), wq_ref, first, last)
    q3 = split_heads(qc_all * jax.nn.sigmoid(qc_all))
    kc_all = _dwconv3(k_ref[...].astype(F32), wk_ref, first, last)
    k3 = split_heads(kc_all * jax.nn.sigmoid(kc_all) * (HD_M ** -0.5))
    qb, kb = q3.astype(BF16), k3.astype(BF16)
    vb = split_heads(v_ref[...])
    g3 = per_head(gates_ref[...].reshape(nc, blk, LANE))
    gt3 = per_head(gt_s[...])

    if zero_state:
        cst[...] = jnp.zeros_like(cst)
        nst[...] = jnp.zeros_like(nst)
        mst[...] = jnp.zeros_like(mst)
    else:
        cst[...] = c0_ref[...]
        nst[...] = n0_ref[...]
        mst[...] = jnp.broadcast_to(m0_ref[...], mst.shape)

    tt = lax.broadcasted_iota(jnp.int32, (1, blk, blk), 1)
    ss = lax.broadcasted_iota(jnp.int32, (1, blk, blk), 2)
    lane = lax.broadcasted_iota(jnp.int32, (1, 1, LANE), 2)
    sub = lax.broadcasted_iota(jnp.int32, (1, n_gates, 1), 1)
    head = lax.broadcasted_iota(jnp.int32, (nbatch, 1, 1), 0) % heads + pl.program_id(1) * heads
    hsum = None
    for d in range(2):
        i_idx = d * 2 * HEADS_M + head
        f_idx = i_idx + HEADS_M
        mask = (ss <= tt) if d == 0 else (ss >= tt)
        b3 = per_head((pre_s if d == 0 else suf_s)[...])
        bt3 = per_head((pret_s if d == 0 else suft_s)[...])
        bcol = jnp.sum(jnp.where(lane == f_idx, b3, 0.0), axis=2, keepdims=True)
        icol = jnp.sum(jnp.where(lane == i_idx, g3, 0.0), axis=2, keepdims=True)
        brow = jnp.sum(jnp.where(sub == f_idx, bt3, 0.0), axis=1, keepdims=True)
        irow = jnp.sum(jnp.where(sub == i_idx, gt3, 0.0), axis=1, keepdims=True)
        bend = bcol[:, blk - 1:blk, :] if d == 0 else bcol[:, 0:1, :]

        dmat = jnp.where(mask, bcol - brow + irow, -jnp.inf)
        mloc = jnp.max(dmat, axis=2, keepdims=True)
        qk = jnp.einsum('ctd,csd->cts', qb, kb, preferred_element_type=F32)
        s_loc = jnp.exp(dmat - mloc) * qk
        num_loc = jnp.einsum('cts,csd->ctd', s_loc.astype(BF16), vb, preferred_element_type=F32)
        den_loc = jnp.sum(s_loc, axis=2, keepdims=True)
        to_end = bend - bcol + icol
        mk = jnp.max(to_end, axis=1, keepdims=True)
        kw = k3 * jnp.exp(to_end - mk)
        kv_s[...] = jnp.einsum('cds,cse->cde', jnp.swapaxes(kw, 1, 2).astype(BF16), vb,
                               preferred_element_type=F32)
        ks_s[...] = jnp.sum(kw, axis=1, keepdims=True)
        be_s[...] = jnp.broadcast_to(bend, be_s.shape)
        mk_s[...] = jnp.broadcast_to(mk, mk_s.shape)

        def step(j, carry, d=d):
            c = j if d == 0 else nc - 1 - j
            sl = pl.ds(c * heads, heads)
            m_prev, c_prev, n_prev = mst[d], cst[d], nst[d]
            cp_s[sl] = c_prev.astype(BF16)
            np_s[sl] = n_prev
            mp_s[sl] = m_prev
            be, mkc = be_s[sl], mk_s[sl]
            m_new = jnp.maximum(be + m_prev, mkc)
            keep = jnp.exp(be + m_prev - m_new)
            add = jnp.exp(mkc - m_new)
            cst[d] = keep * c_prev + add * kv_s[sl]
            nst[d] = keep * n_prev + add * ks_s[sl]
            mst[d] = m_new
            return carry

        lax.fori_loop(0, nc, step, 0)

        m_inter = bcol + mp_s[...][:, :, 0:1]
        m_t = jnp.maximum(m_inter, mloc)
        w_state = jnp.exp(m_inter - m_t)
        w_loc = jnp.exp(mloc - m_t)
        inter = jnp.einsum('ctd,cde->cte', qb, cp_s[...], preferred_element_type=F32)
        num = w_state * inter + w_loc * num_loc
        den = w_state * jnp.sum(q3 * np_s[...], axis=2, keepdims=True) + w_loc * den_loc
        h = num / jnp.maximum(jnp.abs(den), jnp.exp(-m_t))
        hsum = h if hsum is None else hsum + h

    hn = hsum * lax.rsqrt(jnp.mean(hsum * hsum, axis=-1, keepdims=True) + EPS)
    og3 = split_heads(og_ref[...].astype(F32))
    for c in range(nc):
        for hh in range(heads):
            cols = slice(hh * HD_M, (hh + 1) * HD_M)
            e = c * heads + hh
            y_ref[c * blk:(c + 1) * blk, cols] = (hn[e] * gh_ref[:, cols] * jax.nn.sigmoid(og3[e])).astype(y_ref.dtype)
    if emit_state:
        cn_ref[...] = cst[...]
        nn_ref[...] = nst[...]
        mn_ref[...] = mst[...]


def mlstm(proj, gates, w_conv_qk, g_head, state, *, nb, seq, row0, heads):
    blk0 = row0 // seq
    nblk = seq // SCAN_BLOCK
    groups = HEADS_M // heads
    n_gates = 4 * HEADS_M
    zero_state = state is None
    col = lambda part: pl.BlockSpec((seq, heads * HD_M), lambda b, h: (b + blk0, part * groups + h))
    in_specs = [col(0), col(1), col(2), col(3),
                pl.BlockSpec((seq, LANE), lambda b, h: (b + blk0, 0)),
                pl.BlockSpec((3, heads * HD_M), lambda b, h: (0, h)),
                pl.BlockSpec((3, heads * HD_M), lambda b, h: (0, groups + h)),
                pl.BlockSpec((1, heads * HD_M), lambda b, h: (0, h))]
    args = [proj, proj, proj, proj, gates, w_conv_qk, w_conv_qk, g_head.reshape(1, MW)]
    state_specs = [pl.BlockSpec((None, 2, heads, HD_M, HD_M), lambda b, h: (b, 0, h, 0, 0)),
                   pl.BlockSpec((None, 2, heads, 1, HD_M), lambda b, h: (b, 0, h, 0, 0))]
    if not zero_state:
        c0, n0, m0 = state
        in_specs += state_specs + [pl.BlockSpec((None, 2, heads, 1, 1), lambda b, h: (b, 0, h, 0, 0))]
        args += [c0, n0.reshape(nb, 2, HEADS_M, 1, HD_M), m0.reshape(nb, 2, HEADS_M, 1, 1)]
    out_shape = [jax.ShapeDtypeStruct((nb * seq, MW), BF16)]
    out_specs = [pl.BlockSpec((seq, heads * HD_M), lambda b, h: (b, h))]
    if zero_state:
        out_shape += [jax.ShapeDtypeStruct((nb, 2, HEADS_M, HD_M, HD_M), F32),
                      jax.ShapeDtypeStruct((nb, 2, HEADS_M, 1, HD_M), F32),
                      jax.ShapeDtypeStruct((nb, 2, HEADS_M, 1, LANE), F32)]
        out_specs += state_specs + [pl.BlockSpec((None, 2, heads, 1, LANE), lambda b, h: (b, 0, h, 0, 0))]
    per_block = lambda *shape: pltpu.VMEM((nblk,) + shape, F32)
    per_entry = lambda *shape, dtype=F32: pltpu.VMEM((nblk * heads,) + shape, dtype)
    return pl.pallas_call(
        functools.partial(_mlstm_kernel, seq=seq, heads=heads, zero_state=zero_state, emit_state=zero_state),
        out_shape=out_shape,
        grid=(nb, groups),
        in_specs=in_specs,
        out_specs=out_specs,
        scratch_shapes=[per_block(SCAN_BLOCK, LANE), per_block(SCAN_BLOCK, LANE),
                        per_block(n_gates, SCAN_BLOCK), per_block(n_gates, SCAN_BLOCK), per_block(n_gates, SCAN_BLOCK),
                        per_entry(HD_M, HD_M), per_entry(1, HD_M), per_entry(1, LANE), per_entry(1, LANE),
                        per_entry(HD_M, HD_M, dtype=BF16), per_entry(1, HD_M), per_entry(1, LANE),
                        pltpu.VMEM((2, heads, HD_M, HD_M), F32), pltpu.VMEM((2, heads, 1, HD_M), F32),
                        pltpu.VMEM((2, heads, 1, LANE), F32)],
        compiler_params=_cparams(("arbitrary", "arbitrary")),
        name="mlstm_%d" % seq,
    )(*args)


def _hyena_pre_kernel(v_ref, x1_ref, x2_ref, wv_ref, w1_ref, w2_ref, u_ref, x2c_ref):
    first, last = _seq_edges(SEQ_BLOCK, pl.program_id(0) < N_PBLK)
    x1c = _dwconv3(x1_ref[...].astype(F32), w1_ref, first, last)
    u_ref[...] = (x1c * _dwconv3(v_ref[...].astype(F32), wv_ref, first, last)).astype(u_ref.dtype)
    x2c_ref[...] = _dwconv3(x2_ref[...].astype(F32), w2_ref, first, last).astype(x2c_ref.dtype)


def hyena_pre(proj, w_conv_hy):
    tc = 256
    nct = HW // tc
    c0 = 4 * MW // tc
    pcol = lambda part: pl.BlockSpec((SEQ_BLOCK, tc), lambda i, j: (i, c0 + part * nct + j))
    wcol = lambda part: pl.BlockSpec((3, tc), lambda i, j: (0, part * nct + j))
    out = jax.ShapeDtypeStruct((N_TOK, HW), BF16)
    ospec = pl.BlockSpec((SEQ_BLOCK, tc), lambda i, j: (i, j))
    return pl.pallas_call(
        _hyena_pre_kernel,
        out_shape=[out, out],
        grid=(N_TOK // SEQ_BLOCK, nct),
        in_specs=[pcol(0), pcol(1), pcol(2), wcol(0), wcol(1), wcol(2)],
        out_specs=[ospec, ospec],
        compiler_params=_cparams(("arbitrary", "arbitrary")),
        name="hyena_pre",
    )(proj, proj, proj, w_conv_hy, w_conv_hy, w_conv_hy)


@functools.lru_cache(maxsize=None)
def _filter_tables(seq):
    t = np.linspace(0.0, 1.0, seq)[:, None]
    wpos = 2.0 * np.pi * np.arange(seq)[:, None] / seq
    bands = np.linspace(1e-4, N_BANDS - 1, N_BANDS)[None, :]
    z = np.concatenate([t, np.cos(bands * wpos), -np.sin(bands * wpos)], axis=-1)
    z = np.pad(z, ((0, 0), (0, LANE - FILTER_EMB)))
    max_decay = math.log(DECAY_TARGET) / DECAY_FAST
    min_decay = math.log(DECAY_TARGET) / DECAY_SLOW
    deltas = np.abs(np.linspace(min_decay, max_decay, HW))
    decay = np.exp(-t * np.concatenate([deltas, deltas])[None, :])
    return z.astype(np.float32), decay.astype(np.float32)


@functools.lru_cache(maxsize=None)
def _dft_tables(seq, tk):
    n = 2 * seq
    k = np.arange(seq)[:, None]
    t = np.arange(seq)[None, :]
    ang = 2.0 * np.pi * ((k * t) % n) / n
    alt = np.where(np.arange(seq) % 2 == 0, 1.0, -1.0)
    cm, sm = np.cos(ang), np.sin(ang)
    sm[0, :] = alt
    fwd = np.stack([cm.reshape(seq // tk, tk, seq), sm.reshape(seq // tk, tk, seq)], axis=1)
    wk = np.where(np.arange(seq) == 0, 1.0, 2.0)[None, :]
    ci = (np.cos(ang.T) * wk) / n
    si = np.sin(ang.T) * 2.0 / n
    si[:, 0] = alt / n
    inv = np.concatenate([ci, si], axis=1)
    return fwd.astype(np.float32), inv.astype(np.float32)


def _filter_kernel(z_ref, w1_ref, b1_ref, w2_ref, b2_ref, w3_ref, fr_ref, dec_ref, hs_ref, hd_ref):
    fr = fr_ref[...]
    hp = functools.partial(jnp.dot, precision=HIGHEST, preferred_element_type=F32)
    h1 = jnp.sin(fr * (hp(z_ref[...], w1_ref[...]) + b1_ref[...]))
    h2 = jnp.sin(fr * (hp(h1, w2_ref[...]) + b2_ref[...]))
    filt = hp(h2, w3_ref[...]) * dec_ref[...]
    past, fut = filt[:, :HW], filt[:, HW:]
    rows = filt.shape[0]
    grow = lax.broadcasted_iota(jnp.int32, (rows, 1), 0) + pl.program_id(0) * rows
    fut = jnp.where(grow == 0, 0.0, fut)
    hs_ref[...] = past + fut
    hd_ref[...] = past - fut


def filter_gen(seq, w1, b1, w2, b2, w3, freq):
    z, decay = _filter_tables(seq)
    tl = 256
    fh = FILTER_HIDDEN
    full = lambda shape: pl.BlockSpec(shape, lambda i: (0, 0))
    out = jax.ShapeDtypeStruct((seq, HW), F32)
    return pl.pallas_call(
        _filter_kernel,
        out_shape=[out, out],
        grid=(seq // tl,),
        in_specs=[pl.BlockSpec((tl, LANE), lambda i: (i, 0)), full((LANE, fh)), full((1, fh)), full((fh, fh)),
                  full((1, fh)), full((fh, 2 * HW)), full((1, fh)), pl.BlockSpec((tl, 2 * HW), lambda i: (i, 0))],
        out_specs=[pl.BlockSpec((tl, HW), lambda i: (i, 0))] * 2,
        compiler_params=_cparams(("arbitrary",)),
        name="filter_gen",
    )(jnp.asarray(z), jnp.pad(w1, ((0, LANE - FILTER_EMB), (0, 0))), b1.reshape(1, fh), w2, b2.reshape(1, fh), w3,
      freq.reshape(1, fh), jnp.asarray(decay))


def _dft_filter_kernel(a_ref, hs_ref, hd_ref, k_ref, hs_bf, hd_bf):
    @pl.when(pl.program_id(0) == 0)
    def _():
        hs_bf[...] = hs_ref[...].astype(BF16)
        hd_bf[...] = hd_ref[...].astype(BF16)

    k_ref[0] = _dot(a_ref[0].astype(BF16), hs_bf[...])
    k_ref[1] = _dot(a_ref[1].astype(BF16), hd_bf[...])

    @pl.when(pl.program_id(0) == 0)
    def _():
        k_ref[1, 0:1, :] = _dot(a_ref[1, 0:8, :].astype(BF16), hs_bf[...])[0:1, :]


def dft_filter(seq, tk, hs, hd):
    fwd, _ = _dft_tables(seq, tk)
    return pl.pallas_call(
        _dft_filter_kernel,
        out_shape=jax.ShapeDtypeStruct((2, seq, HW), F32),
        grid=(seq // tk,),
        in_specs=[pl.BlockSpec((None, 2, tk, seq), lambda m: (m, 0, 0, 0)),
                  pl.BlockSpec((seq, HW), lambda m: (0, 0)), pl.BlockSpec((seq, HW), lambda m: (0, 0))],
        out_specs=pl.BlockSpec((2, tk, HW), lambda m: (0, m, 0)),
        scratch_shapes=[pltpu.VMEM((seq, HW), BF16), pltpu.VMEM((seq, HW), BF16)],
        compiler_params=_cparams(("arbitrary",)),
        name="dft_filter",
    )(jnp.asarray(fwd), hs, hd)


def _dft_fwd_kernel(a_ref, u_ref, k_ref, y_ref, u_bf, tbl):
    m = pl.program_id(1)

    @pl.when(m == 0)
    def _():
        u_bf[...] = u_ref[...].astype(BF16)

    @pl.when(pl.program_id(0) == 0)
    def _():
        tbl[m] = a_ref[...].astype(BF16)

    ure = _dot(tbl[m, 0], u_bf[...])
    uim = _dot(tbl[m, 1], u_bf[...])
    kre, kim = k_ref[0], k_ref[1]
    packed = (lax.broadcasted_iota(jnp.int32, (ure.shape[0], 1), 0) == 0) & (m == 0)
    y_ref[0] = jnp.where(packed, ure * kre, ure * kre - uim * kim).astype(y_ref.dtype)
    y_ref[1] = jnp.where(packed, uim * kim, ure * kim + uim * kre).astype(y_ref.dtype)


def dft_fwd(seq, tk, u, kf, *, nb, row0):
    fwd, _ = _dft_tables(seq, tk)
    blk0 = row0 // seq
    nm = seq // tk
    return pl.pallas_call(
        _dft_fwd_kernel,
        out_shape=jax.ShapeDtypeStruct((nb, 2, seq, HW), BF16),
        grid=(nb, nm),
        in_specs=[pl.BlockSpec((None, 2, tk, seq), lambda b, m: (jnp.where(b == 0, m, nm - 1), 0, 0, 0)),
                  pl.BlockSpec((seq, HW), lambda b, m: (b + blk0, 0)),
                  pl.BlockSpec((2, tk, HW), lambda b, m: (0, m, 0))],
        out_specs=pl.BlockSpec((None, 2, tk, HW), lambda b, m: (b, 0, m, 0)),
        scratch_shapes=[pltpu.VMEM((seq, HW), BF16), pltpu.VMEM((nm, 2, tk, seq), BF16)],
        compiler_params=_cparams(("arbitrary", "arbitrary"), DFT_VMEM_LIMIT),
        name="dft_fwd",
    )(jnp.asarray(fwd), u, kf)


def _dft_inv_kernel(a_ref, y_ref, u_ref, x2_ref, bias_ref, o_ref, tbl, *, seq):
    t = pl.program_id(1)

    @pl.when(pl.program_id(0) == 0)
    def _():
        tbl[t] = a_ref[...].astype(BF16)

    conv = _dot(tbl[t, :, :seq], y_ref[0]) + _dot(tbl[t, :, seq:], y_ref[1])
    u = u_ref[...].astype(F32)
    o_ref[...] = (x2_ref[...].astype(F32) * (conv + bias_ref[...] * u)).astype(o_ref.dtype)


def dft_inv(seq, tk, y, u, x2c, bias, *, nb, row0):
    _, inv = _dft_tables(seq, tk)
    tm = min(seq, 512)
    nt = seq // tm
    blk0 = row0 // tm
    rows = lambda b, t: (b * nt + t + blk0, 0)
    return pl.pallas_call(
        functools.partial(_dft_inv_kernel, seq=seq),
        out_shape=jax.ShapeDtypeStruct((nb * seq, HW), BF16),
        grid=(nb, nt),
        in_specs=[pl.BlockSpec((tm, 2 * seq), lambda b, t: (jnp.where(b == 0, t, nt - 1), 0)),
                  pl.BlockSpec((None, 2, seq, HW), lambda b, t: (b, 0, 0, 0)),
                  pl.BlockSpec((tm, HW), rows), pl.BlockSpec((tm, HW), rows),
                  pl.BlockSpec((1, HW), lambda b, t: (0, 0))],
        out_specs=pl.BlockSpec((tm, HW), lambda b, t: (b * nt + t, 0)),
        scratch_shapes=[pltpu.VMEM((nt, tm, 2 * seq), BF16)],
        compiler_params=_cparams(("arbitrary", "arbitrary"), DFT_VMEM_LIMIT),
        name="dft_inv",
    )(jnp.asarray(inv), y, u, x2c, bias.reshape(1, HW))


def hyena_group(seq, u, x2c, filt_w, bias, *, nb, row0):
    tk = min(seq, 512)
    hs, hd = filter_gen(seq, *filt_w)
    kf = dft_filter(seq, tk, hs, hd)
    y = dft_fwd(seq, tk, u, kf, nb=nb, row0=row0)
    return dft_inv(seq, tk, y, u, x2c, bias, nb=nb, row0=row0)


def mixer_ab_parts(proj, gates, state_s, w_conv_qk, g_head, w_conv_hy, w_f1, b_f1, w_f2, b_f2, w_f3, freq, hy_bias):
    ym_p, c_p, n_p, m_p = mlstm(proj, gates, w_conv_qk, g_head, None, nb=BATCH, seq=SEQ, row0=0, heads=HEADS_M)
    (ym_s,) = mlstm(proj, gates, w_conv_qk, g_head, state_s, nb=DEC_BATCH, seq=DEC_SEQ, row0=N_P, heads=1)
    u, x2c = hyena_pre(proj, w_conv_hy)
    filt_w = (w_f1, b_f1, w_f2, b_f2, w_f3, freq)
    yh_p = hyena_group(SEQ, u, x2c, filt_w, hy_bias, nb=BATCH, row0=0)
    yh_s = hyena_group(DEC_SEQ, u, x2c, filt_w, hy_bias, nb=DEC_BATCH, row0=N_P)
    state_p = (c_p, n_p[:, :, :, 0, :], m_p[:, :, :, 0, 0])
    return (ym_p, ym_s), (yh_p, yh_s), state_p


def kernel(x_prompt, x_sample, state_mlstm_C, state_mlstm_n, state_mlstm_m, cache_na_k, cache_na_v, c, c_ctx, w_ada, b_ada, g_mix, g_ffn, g_final, w_in_ab, b_gates, w_conv_qk, g_mlstm, w_conv_hy, w_filt1, b_filt1, w_filt2, b_filt2, w_filt3, filt_freq, hyena_bias, w_out_ab, w_in_c, rpb_c, w_out_c, w_up, w_conv_ffn, w_down):
    cmat = jnp.concatenate([c, c_ctx[None, :], jnp.zeros((MOD_ROWS - DEC_BATCH - 1, D), F32)], axis=0)
    mod_all = adaln_all(cmat, w_ada, b_ada).reshape(DEPTH, MOD_ROWS, 6, 1, D)
    x = (x_prompt.reshape(N_P, D), x_sample.reshape(N_S, D))
    h = None
    new_c, new_n, new_m, new_k, new_v = [], [], [], [], []
    for l in range(DEPTH):
        e = l // 2
        if l % 2 == 0:
            assert l == 0, "a later mixer A/B layer would take the fused norm of the layer before it"
            proj, gates = ab_in(x, g_mix, mod_all, l, w_in_ab, b_gates[e], e)
            state_s = (state_mlstm_C[:, e], state_mlstm_n[:, e], state_mlstm_m[:, e])
            y_m, y_h, (c_p, n_p, m_p) = mixer_ab_parts(
                proj, gates, state_s, w_conv_qk[e], g_mlstm[e], w_conv_hy[e], w_filt1[e], b_filt1[e],
                w_filt2[e], b_filt2[e], w_filt3[e], filt_freq[e], hyena_bias[e])
            a_list = [y_m, y_h]
            w_list = [(w_out_ab, (None, MW, D), lambda i, e=e: (e, 0, 0)),
                      (w_out_ab, (None, HW, D), lambda i, e=e: (e, 1, 0))]
            new_c.append(c_p[:, None])
            new_n.append(n_p[:, None])
            new_m.append(m_p[:, None])
        else:
            o, k_new, v_new = mixer_c(h, cache_na_k[:, e], cache_na_v[:, e], w_in_c, e, rpb_c[e])
            a_list = [o]
            w_list = [(w_out_c, (None, D, D), lambda i, e=e: (e, 0, 0))]
            new_k.append(k_new)
            new_v.append(v_new)
        x, h = out_proj(a_list, w_list, x, mod_all, l, 2, g_ffn, l, (l, 3, 4), name="mixer_out")
        mid = ffn_up(h, w_up, w_conv_ffn, l)
        w_list = [(w_down, (None, FF, D), lambda i, l=l: (l, 0, 0))]
        if l + 1 < DEPTH:
            x, h = out_proj([mid], w_list, x, mod_all, l, 5, g_mix, l + 1, (l + 1, 0, 1), name="ffn_down")
        else:
            y_p, y_s = out_proj([mid], w_list, x, mod_all, l, 5, g_final, 0, None, name="ffn_down_final")
    cat = lambda parts: parts[0] if len(parts) == 1 else jnp.concatenate(parts, axis=1)
    return (y_p.reshape(BATCH, SEQ, D), y_s.reshape(DEC_BATCH, DEC_SEQ, D), cat(new_c), cat(new_n), cat(new_m),
            cat(new_k), cat(new_v))
```

```python
import functools
import math

import numpy as np
import jax
import jax.numpy as jnp
from jax import lax
from jax.experimental import pallas as pl
from jax.experimental.pallas import tpu as pltpu

F32 = jnp.float32
BF16 = jnp.bfloat16

D = 1024
BATCH, SEQ = 16, 256
DEC_BATCH, DEC_SEQ = 4, 2048
PAST_LEN = 512
DEPTH = 2
GRID_W = 64
GRID_R = DEC_SEQ // GRID_W
HEADS_M = 4
MW = D // 2
HD_M = MW // HEADS_M
CHUNK = 64
HW = D // 2
N_BANDS = 16
FILTER_EMB = 2 * N_BANDS + 1
FILTER_HIDDEN = 64
DECAY_FAST, DECAY_SLOW, DECAY_TARGET = 0.3, 1.5, 1e-2
NA_HEADS = 16
NA_HD = D // NA_HEADS
NA_KH, NA_KW = 8, 16
FF = 2816
EPS = 1e-6

N_P = BATCH * SEQ
N_S = DEC_BATCH * DEC_SEQ
N_TOK = N_P + N_S
CTX_ROW = DEC_BATCH
MOD_ROWS = 8
LANE = 128
VMEM_LIMIT = 48 * 1024 * 1024
QKV_VMEM_LIMIT = 56 * 1024 * 1024
DFT_VMEM_LIMIT = 56 * 1024 * 1024
HIGHEST = lax.Precision.HIGHEST


def _cparams(sem, vmem_limit=VMEM_LIMIT):
    return pltpu.CompilerParams(dimension_semantics=sem, vmem_limit_bytes=vmem_limit)


def _mod_row(i, bm):
    return jnp.where(i < N_P // bm, CTX_ROW, (i - N_P // bm) // (DEC_SEQ // bm))


def _dot(a, b):
    return jnp.dot(a, b, preferred_element_type=F32)


def _adaln_kernel(c_ref, w_ref, b_ref, o_ref):
    cv = c_ref[...]
    s = cv * jax.nn.sigmoid(cv)
    o_ref[...] = _dot(s.astype(BF16), w_ref[...].astype(BF16)) + b_ref[...]


def adaln_all(cmat, w_ada, b_ada):
    tn = 1024
    return pl.pallas_call(
        _adaln_kernel,
        out_shape=jax.ShapeDtypeStruct((DEPTH, MOD_ROWS, 6 * D), F32),
        grid=(DEPTH, 6 * D // tn),
        in_specs=[
            pl.BlockSpec((MOD_ROWS, D), lambda l, j: (0, 0)),
            pl.BlockSpec((None, D, tn), lambda l, j: (l, 0, j)),
            pl.BlockSpec((None, 1, tn), lambda l, j: (l, 0, j)),
        ],
        out_specs=pl.BlockSpec((None, MOD_ROWS, tn), lambda l, j: (l, 0, j)),
        compiler_params=_cparams(("arbitrary", "arbitrary")),
        name="adaln",
    )(cmat, w_ada, b_ada.reshape(DEPTH, 1, 6 * D))


ROW_BM = 512
ROW_NPB = N_P // ROW_BM


def _row_specs(arr, cols):
    if isinstance(arr, tuple):
        return ([pl.BlockSpec((ROW_BM, cols), lambda i: (jnp.minimum(i, ROW_NPB - 1), 0)),
                 pl.BlockSpec((ROW_BM, cols), lambda i: (jnp.maximum(i - ROW_NPB, 0), 0))], list(arr))
    return [pl.BlockSpec((ROW_BM, cols), lambda i: (i, 0))], [arr]


def _pick_rows(refs):
    if len(refs) == 1:
        return refs[0][...]
    return jnp.where(pl.program_id(0) < ROW_NPB, refs[0][...], refs[1][...])


def _mod_spec(layer, which):
    return pl.BlockSpec((None, None, None, 1, D), lambda i: (layer, _mod_row(i, ROW_BM), which, 0, 0))


def _layer_vec_spec(layer):
    return pl.BlockSpec((None, 1, D), lambda i: (layer, 0, 0))


def _rms_mod(x, g_ref, sh_ref, sc_ref):
    y = x * lax.rsqrt(jnp.mean(x * x, axis=-1, keepdims=True) + EPS)
    return (y * g_ref[...]) * (1.0 + sc_ref[...]) + sh_ref[...]


def _out_proj_kernel(*refs, a_counts, n_x, final):
    pos = 0
    a_groups = []
    for cnt in a_counts:
        a_groups.append(refs[pos:pos + cnt])
        pos += cnt
    w_refs = refs[pos:pos + len(a_counts)]
    pos += len(a_counts)
    x_refs = refs[pos:pos + n_x]
    pos += n_x
    gt_ref, g_ref = refs[pos:pos + 2]
    pos += 2
    if not final:
        sh_ref, sc_ref = refs[pos:pos + 2]
        pos += 2
    out_a, out_b = refs[pos:pos + 2]
    wbf = refs[pos + 2:]

    @pl.when(pl.program_id(0) == 0)
    def _():
        for w_ref, wb in zip(w_refs, wbf):
            wb[...] = w_ref[...].astype(BF16)

    acc = None
    for group, wb in zip(a_groups, wbf):
        part = _dot(_pick_rows(group), wb[...])
        acc = part if acc is None else acc + part
    xn = _pick_rows(x_refs) + gt_ref[...] * acc
    if final:
        y = xn * lax.rsqrt(jnp.mean(xn * xn, axis=-1, keepdims=True) + EPS) * g_ref[...]

        @pl.when(pl.program_id(0) < ROW_NPB)
        def _():
            out_a[...] = y

        @pl.when(pl.program_id(0) >= ROW_NPB)
        def _():
            out_b[...] = y
    else:
        out_a[...] = xn
        out_b[...] = _rms_mod(xn, g_ref, sh_ref, sc_ref).astype(out_b.dtype)


def out_proj(a_list, w_list, x, mod_all, layer, gt_idx, g_all, g_layer, norm_mod_idx=None, name="out_proj"):
    final = norm_mod_idx is None
    in_specs, args, a_counts = [], [], []
    for a in a_list:
        cols = (a[0] if isinstance(a, tuple) else a).shape[1]
        specs, ops = _row_specs(a, cols)
        in_specs += specs
        args += ops
        a_counts.append(len(ops))
    w_shapes = []
    for w, block, imap in w_list:
        in_specs.append(pl.BlockSpec(block, imap, pipeline_mode=pl.Buffered(1)))
        args.append(w)
        w_shapes.append(tuple(b for b in block if b is not None))
    x_specs, x_args = _row_specs(x, D)
    in_specs += x_specs + [_mod_spec(layer, gt_idx), _layer_vec_spec(g_layer)]
    args += x_args + [mod_all, g_all.reshape(-1, 1, D)]
    if final:
        out_shape = [jax.ShapeDtypeStruct((N_P, D), F32), jax.ShapeDtypeStruct((N_S, D), F32)]
        out_specs = [pl.BlockSpec((ROW_BM, D), lambda i: (jnp.minimum(i, ROW_NPB - 1), 0)),
                     pl.BlockSpec((ROW_BM, D), lambda i: (jnp.maximum(i - ROW_NPB, 0), 0))]
    else:
        n_layer, sh_idx, sc_idx = norm_mod_idx
        in_specs += [_mod_spec(n_layer, sh_idx), _mod_spec(n_layer, sc_idx)]
        args += [mod_all, mod_all]
        out_shape = [jax.ShapeDtypeStruct((N_TOK, D), F32), jax.ShapeDtypeStruct((N_TOK, D), BF16)]
        out_specs = [pl.BlockSpec((ROW_BM, D), lambda i: (i, 0))] * 2
    return pl.pallas_call(
        functools.partial(_out_proj_kernel, a_counts=tuple(a_counts), n_x=len(x_args), final=final),
        out_shape=out_shape,
        grid=(N_TOK // ROW_BM,),
        in_specs=in_specs,
        out_specs=out_specs,
        scratch_shapes=[pltpu.VMEM(s, BF16) for s in w_shapes],
        compiler_params=_cparams(("arbitrary",)),
        name=name,
    )(*args)


def _qkv_kernel(a_ref, w_ref, o_ref, kc_ref, vc_ref, wbf):
    @pl.when(pl.program_id(0) == 0)
    def _():
        wbf[...] = w_ref[...].astype(BF16)

    a = a_ref[...]
    for part in range(3):
        acc = _dot(a, wbf[:, part * D:(part + 1) * D])
        for pp in range(NA_HEADS // 2):
            o_ref[part, pp] = acc[:, pp * 2 * NA_HD:(pp + 1) * 2 * NA_HD].astype(o_ref.dtype)
        if part > 0:
            c_ref = kc_ref if part == 1 else vc_ref

            @pl.when(pl.program_id(0) < ROW_NPB)
            def _(acc=acc, c_ref=c_ref):
                for b in range(ROW_BM // SEQ):
                    for hh in range(NA_HEADS):
                        c_ref[b, hh] = acc[b * SEQ:(b + 1) * SEQ, hh * NA_HD:(hh + 1) * NA_HD]


def qkv_proj(h, w_in, layer):
    seqs = ROW_BM // SEQ
    cache = jax.ShapeDtypeStruct((BATCH, 1, NA_HEADS, SEQ, NA_HD), F32)
    cache_spec = pl.BlockSpec((seqs, None, NA_HEADS, SEQ, NA_HD),
                              lambda i: (jnp.minimum(i, ROW_NPB - 1), 0, 0, 0, 0))
    return pl.pallas_call(
        _qkv_kernel,
        out_shape=[jax.ShapeDtypeStruct((3, NA_HEADS // 2, N_TOK, 2 * NA_HD), BF16), cache, cache],
        grid=(N_TOK // ROW_BM,),
        in_specs=[pl.BlockSpec((ROW_BM, D), lambda i: (i, 0)),
                  pl.BlockSpec((None, D, 3 * D), lambda i: (layer, 0, 0), pipeline_mode=pl.Buffered(1))],
        out_specs=[pl.BlockSpec((3, NA_HEADS // 2, ROW_BM, 2 * NA_HD), lambda i: (0, 0, i, 0)),
                   cache_spec, cache_spec],
        scratch_shapes=[pltpu.VMEM((D, 3 * D), BF16)],
        compiler_params=_cparams(("arbitrary",), QKV_VMEM_LIMIT),
        name="qkv_proj",
    )(h, w_in)


AB_MAIN = 4 * MW + 3 * HW
N_GATES = 4 * HEADS_M


def _ab_in_kernel(*refs):
    x_refs = refs[:-8]
    g_ref, sh_ref, sc_ref, w_ref, b_ref, proj_ref, gates_ref, wbf = refs[-8:]

    @pl.when(pl.program_id(0) == 0)
    def _():
        wbf[:4 * MW, :] = w_ref[:4 * MW, :].astype(BF16)
        wbf[4 * MW:AB_MAIN, :] = w_ref[4 * MW + N_GATES:, :].astype(BF16)
        wbf[AB_MAIN:AB_MAIN + N_GATES, :] = w_ref[4 * MW:4 * MW + N_GATES, :].astype(BF16)
        wbf[AB_MAIN + N_GATES:, :] = jnp.zeros((LANE - N_GATES, D), BF16)

    h = _rms_mod(_pick_rows(x_refs), g_ref, sh_ref, sc_ref).astype(BF16)
    proj_ref[...] = lax.dot_general(h, wbf[:AB_MAIN, :], NT_DIMS, preferred_element_type=F32).astype(proj_ref.dtype)
    gates_ref[...] = lax.dot_general(h, wbf[AB_MAIN:, :], NT_DIMS, preferred_element_type=F32) + b_ref[...]


def ab_in(x, g_all, mod_all, layer, w_in_t, b_gates, e):
    x_specs, x_args = _row_specs(x, D)
    b_pad = jnp.pad(b_gates, (0, LANE - N_GATES)).reshape(1, LANE)
    return pl.pallas_call(
        _ab_in_kernel,
        out_shape=[jax.ShapeDtypeStruct((N_TOK, AB_MAIN), BF16), jax.ShapeDtypeStruct((N_TOK, LANE), F32)],
        grid=(N_TOK // ROW_BM,),
        in_specs=x_specs + [_layer_vec_spec(layer), _mod_spec(layer, 0), _mod_spec(layer, 1),
                            pl.BlockSpec((None, w_in_t.shape[1], D), lambda i: (e, 0, 0),
                                         pipeline_mode=pl.Buffered(1)),
                            pl.BlockSpec((1, LANE), lambda i: (0, 0))],
        out_specs=[pl.BlockSpec((ROW_BM, AB_MAIN), lambda i: (i, 0)), pl.BlockSpec((ROW_BM, LANE), lambda i: (i, 0))],
        scratch_shapes=[pltpu.VMEM((AB_MAIN + LANE, D), BF16)],
        compiler_params=_cparams(("arbitrary",)),
        name="ab_in",
    )(*x_args, g_all.reshape(-1, 1, D), mod_all, mod_all, w_in_t, b_pad)


SEQ_BLOCK = DEC_SEQ
N_PBLK = N_P // SEQ_BLOCK


def _seq_edges(rows, is_prompt):
    r = lax.broadcasted_iota(jnp.int32, (rows, 1), 0)
    first = (r == 0) | (is_prompt & (r % SEQ == 0))
    last = (r == rows - 1) | (is_prompt & (r % SEQ == SEQ - 1))
    return first, last


def _dwconv3(x, w_ref, first, last):
    rows = x.shape[0]
    prev = jnp.where(first, 0.0, pltpu.roll(x, 1, 0))
    nxt = jnp.where(last, 0.0, pltpu.roll(x, rows - 1, 0))
    return prev * w_ref[0:1, :] + x * w_ref[1:2, :] + nxt * w_ref[2:3, :]


def _gated_gelu_of_half(y, g):
    c = math.sqrt(2.0 / math.pi)
    t = jnp.tanh(y * (2.0 * c + (8.0 * 0.044715 * c) * (y * y)))
    return (y + y * t) * g


def _ffn_up_kernel(h_ref, wa_ref, wg_ref, wc_ref, o_ref):
    first, last = _seq_edges(SEQ_BLOCK, pl.program_id(0) < N_PBLK)
    h = h_ref[...]
    a = _dot(h, wa_ref[...].astype(BF16))
    g = _dot(h, wg_ref[...].astype(BF16))
    half_conv = _dwconv3(a, 0.5 * wc_ref[...], first, last)
    o_ref[...] = _gated_gelu_of_half(half_conv, g).astype(o_ref.dtype)


def ffn_up(h, w_up, w_conv, layer):
    tc = 256
    nct = FF // tc
    return pl.pallas_call(
        _ffn_up_kernel,
        out_shape=jax.ShapeDtypeStruct((N_TOK, FF), BF16),
        grid=(N_TOK // SEQ_BLOCK, nct),
        in_specs=[
            pl.BlockSpec((SEQ_BLOCK, D), lambda i, j: (i, 0)),
            pl.BlockSpec((None, D, tc), lambda i, j: (layer, 0, j)),
            pl.BlockSpec((None, D, tc), lambda i, j: (layer, 0, j + nct)),
            pl.BlockSpec((None, 3, tc), lambda i, j: (layer, 0, j)),
        ],
        out_specs=pl.BlockSpec((SEQ_BLOCK, tc), lambda i, j: (i, j)),
        compiler_params=_cparams(("arbitrary", "arbitrary")),
        name="ffn_up",
    )(h, w_up, w_up, w_conv)


HEAD_PAIRS = NA_HEADS // 2
NT_DIMS = (((1,), (1,)), ((), ()))


def _pair_mask(shape):
    return lax.broadcasted_iota(jnp.int32, shape, len(shape) - 1) < NA_HD


def _one_head(x2, first):
    keep = _pair_mask(x2.shape) if first else ~_pair_mask(x2.shape)
    return jnp.where(keep, x2, jnp.zeros_like(x2))


def _ctx_attn_kernel(q_ref, k_ref, v_ref, o_ref, *, pairs):
    outs = []
    for pp in range(pairs):
        q2, k2, v2 = q_ref[pp], k_ref[pp], v_ref[pp]
        res = []
        for first in (True, False):
            s = lax.dot_general(_one_head(q2, first), k2, NT_DIMS, preferred_element_type=F32) * (NA_HD ** -0.5)
            m = jnp.max(s, axis=-1, keepdims=True)
            p = jnp.exp(s - m)
            l = jnp.sum(p, axis=-1, keepdims=True)
            res.append(_dot(p.astype(BF16), v2) / l)
        outs.append(jnp.where(_pair_mask(res[0].shape), res[0], res[1]))
    o_ref[...] = jnp.concatenate(outs, axis=-1).astype(o_ref.dtype)


def ctx_attention(qkv):
    pairs = 4
    spec = lambda part: pl.BlockSpec((None, pairs, SEQ, 2 * NA_HD), lambda b, h: (part, h, b, 0))
    return pl.pallas_call(
        functools.partial(_ctx_attn_kernel, pairs=pairs),
        out_shape=jax.ShapeDtypeStruct((N_P, D), BF16),
        grid=(BATCH, HEAD_PAIRS // pairs),
        in_specs=[spec(0), spec(1), spec(2)],
        out_specs=pl.BlockSpec((SEQ, pairs * 2 * NA_HD), lambda b, h: (b, h)),
        compiler_params=_cparams(("arbitrary", "arbitrary")),
        name="ctx_attn",
    )(qkv, qkv, qkv)


def _na_tables():
    q = np.arange(GRID_W)[:, None]
    w = np.arange(GRID_W)[None, :]
    idx_c = np.clip(w - q + (NA_KW - 1), 0, 2 * NA_KW - 2)
    onehot = (idx_c.reshape(1, -1) == np.arange(32)[:, None]).astype(np.float32)
    c_start = np.clip(np.arange(GRID_W) - NA_KW // 2, 0, GRID_W - NA_KW)[:, None]
    inside = (w >= c_start) & (w < c_start + NA_KW)
    cmask = np.where(inside, 0.0, -np.inf).astype(np.float32)
    return onehot, np.tile(cmask, (1, 2))


def _rpb_expand_kernel(r_ref, e_ref, o_ref):
    o_ref[...] = jnp.dot(r_ref[...], e_ref[...], precision=HIGHEST, preferred_element_type=F32)


def rpb_expand(rpb):
    onehot, _ = _na_tables()
    rp = jnp.pad(rpb, ((0, 0), (0, 1), (0, 1)))
    return pl.pallas_call(
        _rpb_expand_kernel,
        out_shape=jax.ShapeDtypeStruct((NA_HEADS, 16, GRID_W * GRID_W), F32),
        grid=(NA_HEADS,),
        in_specs=[pl.BlockSpec((None, 16, 32), lambda h: (h, 0, 0)),
                  pl.BlockSpec((32, GRID_W * GRID_W), lambda h: (0, 0))],
        out_specs=pl.BlockSpec((None, 16, GRID_W * GRID_W), lambda h: (h, 0, 0)),
        compiler_params=_cparams(("arbitrary",)),
        name="rpb_expand",
    )(rp, jnp.asarray(onehot))


NA_QROWS = 8
NA_WIN = 2 * NA_QROWS


def _na_attn_kernel(q_ref, k_ref, v_ref, kc_ref, vc_ref, t_ref, o_ref, p_loc, p_ctx):
    nq = NA_QROWS * GRID_W
    nk = NA_WIN * GRID_W
    pair = 2 * GRID_W
    nt = NT_DIMS
    lane = lax.broadcasted_iota(jnp.int32, (GRID_W, pair), 1)
    zero_tile = jnp.zeros((GRID_W, pair), BF16)
    kc = kc_ref[...].reshape(2 * NA_HD, PAST_LEN).astype(BF16)
    vc = vc_ref[...].reshape(2 * NA_HD, PAST_LEN).astype(BF16)
    for blk in range(GRID_R // NA_QROWS):
        k0 = min(max(NA_QROWS * blk - NA_KH // 2, 0), GRID_R - NA_WIN)
        q2 = q_ref[blk * nq:(blk + 1) * nq, :] * (NA_HD ** -0.5)
        kw = k_ref[k0 * GRID_W:k0 * GRID_W + nk, :]
        vw = v_ref[k0 * GRID_W:k0 * GRID_W + nk, :]
        outs = []
        for hh in range(2):
            q = _one_head(q2, hh == 0)
            s_loc = lax.dot_general(q, kw, nt, preferred_element_type=F32)
            s_ctx = _dot(q, kc)
            denoms = []
            for qi in range(NA_QROWS):
                r = NA_QROWS * blk + qi
                r_start = min(max(r - NA_KH // 2, 0), GRID_R - NA_KH)
                rows = slice(qi * GRID_W, (qi + 1) * GRID_W)
                ctx_tiles = [s_ctx[rows, c * pair:(c + 1) * pair] for c in range(PAST_LEN // pair)]
                tiles = {}
                for j in range(nk // pair):
                    kr = k0 + 2 * j
                    ok0 = r_start <= kr < r_start + NA_KH
                    ok1 = r_start <= kr + 1 < r_start + NA_KH
                    if not (ok0 or ok1):
                        continue
                    sb = s_loc[rows, j * pair:(j + 1) * pair] + t_ref[hh, kr - r + NA_KH]
                    if not (ok0 and ok1):
                        sb = jnp.where((lane < GRID_W) if ok0 else (lane >= GRID_W), sb, -jnp.inf)
                    tiles[j] = sb
                mx = functools.reduce(jnp.maximum, list(tiles.values()) + ctx_tiles)
                m = jnp.max(mx, axis=1, keepdims=True)
                acc = None
                for j in range(nk // pair):
                    if j in tiles:
                        p = jnp.exp(tiles[j] - m)
                        acc = p if acc is None else acc + p
                        p_loc[rows, j * pair:(j + 1) * pair] = p.astype(BF16)
                    else:
                        p_loc[rows, j * pair:(j + 1) * pair] = zero_tile
                for c, t in enumerate(ctx_tiles):
                    p = jnp.exp(t - m)
                    acc = acc + p
                    p_ctx[rows, c * pair:(c + 1) * pair] = p.astype(BF16)
                denoms.append(jnp.sum(acc, axis=1, keepdims=True))
            pv = _dot(p_loc[...], vw) + lax.dot_general(p_ctx[...], vc, nt, preferred_element_type=F32)
            outs.append(pv / jnp.concatenate(denoms, axis=0))
        o_ref[blk * nq:(blk + 1) * nq, :] = jnp.where(_pair_mask(outs[0].shape), outs[0], outs[1]).astype(o_ref.dtype)


def na_attention(qkv, k_ctx, v_ctx, bias_pairs):
    blk0 = N_P // DEC_SEQ
    spec = lambda part: pl.BlockSpec((None, None, DEC_SEQ, 2 * NA_HD), lambda h, b: (part, h, b + blk0, 0))
    cspec = pl.BlockSpec((None, 2, NA_HD, PAST_LEN), lambda h, b: (b, h, 0, 0))
    return pl.pallas_call(
        _na_attn_kernel,
        out_shape=jax.ShapeDtypeStruct((N_S, D), BF16),
        grid=(HEAD_PAIRS, DEC_BATCH),
        in_specs=[spec(0), spec(1), spec(2), cspec, cspec,
                  pl.BlockSpec((2, 16, GRID_W, 2 * GRID_W), lambda h, b: (h, 0, 0, 0))],
        out_specs=pl.BlockSpec((DEC_SEQ, 2 * NA_HD), lambda h, b: (b, h)),
        scratch_shapes=[pltpu.VMEM((NA_QROWS * GRID_W, NA_WIN * GRID_W), BF16),
                        pltpu.VMEM((NA_QROWS * GRID_W, PAST_LEN), BF16)],
        compiler_params=_cparams(("arbitrary", "arbitrary")),
        name="na_attn",
    )(qkv, qkv, qkv, k_ctx, v_ctx, bias_pairs)


def mixer_c(h, k_ctx, v_ctx, w_in_all, layer, rpb):
    qkv, k_new, v_new = qkv_proj(h, w_in_all, layer)
    o_p = ctx_attention(qkv)
    _, cmask2 = _na_tables()
    b15 = rpb_expand(rpb).reshape(NA_HEADS, 16, GRID_W, GRID_W)
    b17 = jnp.pad(b15, ((0, 0), (1, 0), (0, 0), (0, 0)))
    bias_pairs = jnp.concatenate([b17[:, :16], b17[:, 1:]], axis=-1) + jnp.asarray(cmask2)
    o_s = na_attention(qkv, jnp.swapaxes(k_ctx, -1, -2), jnp.swapaxes(v_ctx, -1, -2), bias_pairs)
    return (o_p, o_s), k_new, v_new


SCAN_BLOCK = HD_M


def _mlstm_kernel(*refs, seq, heads, zero_state, emit_state):
    q_ref, k_ref, v_ref, og_ref, gates_ref, wq_ref, wk_ref, gh_ref = refs[:8]
    pos = 8
    if not zero_state:
        c0_ref, n0_ref, m0_ref = refs[pos:pos + 3]
        pos += 3
    y_ref = refs[pos]
    pos += 1
    if emit_state:
        cn_ref, nn_ref, mn_ref = refs[pos:pos + 3]
        pos += 3
    pre_s, suf_s, gt_s, pret_s, suft_s, kv_s, ks_s, be_s, mk_s, cp_s, np_s, mp_s, cst, nst, mst = refs[pos:]

    blk = SCAN_BLOCK
    nc = seq // blk
    nbatch = nc * heads
    n_gates = 4 * HEADS_M
    r = lax.broadcasted_iota(jnp.int32, (seq, 1), 0)

    @pl.when(pl.program_id(1) == 0)
    def _():
        g_all = gates_ref[...]
        lf = jax.nn.log_sigmoid(g_all)
        rin = r % blk
        pre, suf = lf, lf
        for sh in [1 << i for i in range(blk.bit_length() - 1)]:
            pre = pre + jnp.where(rin >= sh, pltpu.roll(pre, sh, 0), 0.0)
            suf = suf + jnp.where(rin < blk - sh, pltpu.roll(suf, seq - sh, 0), 0.0)
        pre3, suf3 = pre.reshape(nc, blk, LANE), suf.reshape(nc, blk, LANE)
        pre_s[...] = pre3
        suf_s[...] = suf3
        gt_s[...] = jnp.swapaxes(g_all.reshape(nc, blk, LANE), 1, 2)[:, :n_gates, :]
        pret_s[...] = jnp.swapaxes(pre3, 1, 2)[:, :n_gates, :]
        suft_s[...] = jnp.swapaxes(suf3, 1, 2)[:, :n_gates, :]

    def split_heads(x):
        x3 = x.reshape(nc, blk, heads * HD_M)
        if heads == 1:
            return x3
        tiles = jnp.stack([x3[:, :, hh * HD_M:(hh + 1) * HD_M] for hh in range(heads)], axis=1)
        return tiles.reshape(nbatch, blk, HD_M)

    def per_head(x):
        if heads == 1:
            return x
        return jnp.broadcast_to(x[:, None], (nc, heads) + x.shape[1:]).reshape((nbatch,) + x.shape[1:])

    first, last = r == 0, r == seq - 1
    qc_all = _dwconv3(q_ref[...].astype(F32), wq_ref, first, last)
    q3 = split_heads(qc_all * jax.nn.sigmoid(qc_all))
    kc_all = _dwconv3(k_ref[...].astype(F32), wk_ref, first, last)
    k3 = split_heads(kc_all * jax.nn.sigmoid(kc_all) * (HD_M ** -0.5))
    qb, kb = q3.astype(BF16), k3.astype(BF16)
    vb = split_heads(v_ref[...])
    g3 = per_head(gates_ref[...].reshape(nc, blk, LANE))
    gt3 = per_head(gt_s[...])

    if zero_state:
        cst[...] = jnp.zeros_like(cst)
        nst[...] = jnp.zeros_like(nst)
        mst[...] = jnp.zeros_like(mst)
    else:
        cst[...] = c0_ref[...]
        nst[...] = n0_ref[...]
        mst[...] = jnp.broadcast_to(m0_ref[...], mst.shape)

    tt = lax.broadcasted_iota(jnp.int32, (1, blk, blk), 1)
    ss = lax.broadcasted_iota(jnp.int32, (1, blk, blk), 2)
    lane = lax.broadcasted_iota(jnp.int32, (1, 1, LANE), 2)
    sub = lax.broadcasted_iota(jnp.int32, (1, n_gates, 1), 1)
    head = lax.broadcasted_iota(jnp.int32, (nbatch, 1, 1), 0) % heads + pl.program_id(1) * heads
    hsum = None
    for d in range(2):
        i_idx = d * 2 * HEADS_M + head
        f_idx = i_idx + HEADS_M
        mask = (ss <= tt) if d == 0 else (ss >= tt)
        b3 = per_head((pre_s if d == 0 else suf_s)[...])
        bt3 = per_head((pret_s if d == 0 else suft_s)[...])
        bcol = jnp.sum(jnp.where(lane == f_idx, b3, 0.0), axis=2, keepdims=True)
        icol = jnp.sum(jnp.where(lane == i_idx, g3, 0.0), axis=2, keepdims=True)
        brow = jnp.sum(jnp.where(sub == f_idx, bt3, 0.0), axis=1, keepdims=True)
        irow = jnp.sum(jnp.where(sub == i_idx, gt3, 0.0), axis=1, keepdims=True)
        bend = bcol[:, blk - 1:blk, :] if d == 0 else bcol[:, 0:1, :]

        dmat = jnp.where(mask, bcol - brow + irow, -jnp.inf)
        mloc = jnp.max(dmat, axis=2, keepdims=True)
        qk = jnp.einsum('ctd,csd->cts', qb, kb, preferred_element_type=F32)
        s_loc = jnp.exp(dmat - mloc) * qk
        num_loc = jnp.einsum('cts,csd->ctd', s_loc.astype(BF16), vb, preferred_element_type=F32)
        den_loc = jnp.sum(s_loc, axis=2, keepdims=True)
        to_end = bend - bcol + icol
        mk = jnp.max(to_end, axis=1, keepdims=True)
        kw = k3 * jnp.exp(to_end - mk)
        kv_s[...] = jnp.einsum('cds,cse->cde', jnp.swapaxes(kw, 1, 2).astype(BF16), vb,
                               preferred_element_type=F32)
        ks_s[...] = jnp.sum(kw, axis=1, keepdims=True)
        be_s[...] = jnp.broadcast_to(bend, be_s.shape)
        mk_s[...] = jnp.broadcast_to(mk, mk_s.shape)

        def step(j, carry, d=d):
            c = j if d == 0 else nc - 1 - j
            sl = pl.ds(c * heads, heads)
            m_prev, c_prev, n_prev = mst[d], cst[d], nst[d]
            cp_s[sl] = c_prev.astype(BF16)
            np_s[sl] = n_prev
            mp_s[sl] = m_prev
            be, mkc = be_s[sl], mk_s[sl]
            m_new = jnp.maximum(be + m_prev, mkc)
            keep = jnp.exp(be + m_prev - m_new)
            add = jnp.exp(mkc - m_new)
            cst[d] = keep * c_prev + add * kv_s[sl]
            nst[d] = keep * n_prev + add * ks_s[sl]
            mst[d] = m_new
            return carry

        lax.fori_loop(0, nc, step, 0)

        m_inter = bcol + mp_s[...][:, :, 0:1]
        m_t = jnp.maximum(m_inter, mloc)
        w_state = jnp.exp(m_inter - m_t)
        w_loc = jnp.exp(mloc - m_t)
        inter = jnp.einsum('ctd,cde->cte', qb, cp_s[...], preferred_element_type=F32)
        num = w_state * inter + w_loc * num_loc
        den = w_state * jnp.sum(q3 * np_s[...], axis=2, keepdims=True) + w_loc * den_loc
        h = num / jnp.maximum(jnp.abs(den), jnp.exp(-m_t))
        hsum = h if hsum is None else hsum + h

    hn = hsum * lax.rsqrt(jnp.mean(hsum * hsum, axis=-1, keepdims=True) + EPS)
    og3 = split_heads(og_ref[...].astype(F32))
    for c in range(nc):
        for hh in range(heads):
            cols = slice(hh * HD_M, (hh + 1) * HD_M)
            e = c * heads + hh
            y_ref[c * blk:(c + 1) * blk, cols] = (hn[e] * gh_ref[:, cols] * jax.nn.sigmoid(og3[e])).astype(y_ref.dtype)
    if emit_state:
        cn_ref[...] = cst[...]
        nn_ref[...] = nst[...]
        mn_ref[...] = mst[...]


def mlstm(proj, gates, w_conv_qk, g_head, state, *, nb, seq, row0, heads):
    blk0 = row0 // seq
    nblk = seq // SCAN_BLOCK
    groups = HEADS_M // heads
    n_gates = 4 * HEADS_M
    zero_state = state is None
    col = lambda part: pl.BlockSpec((seq, heads * HD_M), lambda b, h: (b + blk0, part * groups + h))
    in_specs = [col(0), col(1), col(2), col(3),
                pl.BlockSpec((seq, LANE), lambda b, h: (b + blk0, 0)),
                pl.BlockSpec((3, heads * HD_M), lambda b, h: (0, h)),
                pl.BlockSpec((3, heads * HD_M), lambda b, h: (0, groups + h)),
                pl.BlockSpec((1, heads * HD_M), lambda b, h: (0, h))]
    args = [proj, proj, proj, proj, gates, w_conv_qk, w_conv_qk, g_head.reshape(1, MW)]
    state_specs = [pl.BlockSpec((None, 2, heads, HD_M, HD_M), lambda b, h: (b, 0, h, 0, 0)),
                   pl.BlockSpec((None, 2, heads, 1, HD_M), lambda b, h: (b, 0, h, 0, 0))]
    if not zero_state:
        c0, n0, m0 = state
        in_specs += state_specs + [pl.BlockSpec((None, 2, heads, 1, 1), lambda b, h: (b, 0, h, 0, 0))]
        args += [c0, n0.reshape(nb, 2, HEADS_M, 1, HD_M), m0.reshape(nb, 2, HEADS_M, 1, 1)]
    out_shape = [jax.ShapeDtypeStruct((nb * seq, MW), BF16)]
    out_specs = [pl.BlockSpec((seq, heads * HD_M), lambda b, h: (b, h))]
    if zero_state:
        out_shape += [jax.ShapeDtypeStruct((nb, 2, HEADS_M, HD_M, HD_M), F32),
                      jax.ShapeDtypeStruct((nb, 2, HEADS_M, 1, HD_M), F32),
                      jax.ShapeDtypeStruct((nb, 2, HEADS_M, 1, LANE), F32)]
        out_specs += state_specs + [pl.BlockSpec((None, 2, heads, 1, LANE), lambda b, h: (b, 0, h, 0, 0))]
    per_block = lambda *shape: pltpu.VMEM((nblk,) + shape, F32)
    per_entry = lambda *shape, dtype=F32: pltpu.VMEM((nblk * heads,) + shape, dtype)
    return pl.pallas_call(
        functools.partial(_mlstm_kernel, seq=seq, heads=heads, zero_state=zero_state, emit_state=zero_state),
        out_shape=out_shape,
        grid=(nb, groups),
        in_specs=in_specs,
        out_specs=out_specs,
        scratch_shapes=[per_block(SCAN_BLOCK, LANE), per_block(SCAN_BLOCK, LANE),
                        per_block(n_gates, SCAN_BLOCK), per_block(n_gates, SCAN_BLOCK), per_block(n_gates, SCAN_BLOCK),
                        per_entry(HD_M, HD_M), per_entry(1, HD_M), per_entry(1, LANE), per_entry(1, LANE),
                        per_entry(HD_M, HD_M, dtype=BF16), per_entry(1, HD_M), per_entry(1, LANE),
                        pltpu.VMEM((2, heads, HD_M, HD_M), F32), pltpu.VMEM((2, heads, 1, HD_M), F32),
                        pltpu.VMEM((2, heads, 1, LANE), F32)],
        compiler_params=_cparams(("arbitrary", "arbitrary")),
        name="mlstm_%d" % seq,
    )(*args)


def _hyena_pre_kernel(v_ref, x1_ref, x2_ref, wv_ref, w1_ref, w2_ref, u_ref, x2c_ref):
    first, last = _seq_edges(SEQ_BLOCK, pl.program_id(0) < N_PBLK)
    x1c = _dwconv3(x1_ref[...].astype(F32), w1_ref, first, last)
    u_ref[...] = (x1c * _dwconv3(v_ref[...].astype(F32), wv_ref, first, last)).astype(u_ref.dtype)
    x2c_ref[...] = _dwconv3(x2_ref[...].astype(F32), w2_ref, first, last).astype(x2c_ref.dtype)


def hyena_pre(proj, w_conv_hy):
    tc = 256
    nct = HW // tc
    c0 = 4 * MW // tc
    pcol = lambda part: pl.BlockSpec((SEQ_BLOCK, tc), lambda i, j: (i, c0 + part * nct + j))
    wcol = lambda part: pl.BlockSpec((3, tc), lambda i, j: (0, part * nct + j))
    out = jax.ShapeDtypeStruct((N_TOK, HW), BF16)
    ospec = pl.BlockSpec((SEQ_BLOCK, tc), lambda i, j: (i, j))
    return pl.pallas_call(
        _hyena_pre_kernel,
        out_shape=[out, out],
        grid=(N_TOK // SEQ_BLOCK, nct),
        in_specs=[pcol(0), pcol(1), pcol(2), wcol(0), wcol(1), wcol(2)],
        out_specs=[ospec, ospec],
        compiler_params=_cparams(("arbitrary", "arbitrary")),
        name="hyena_pre",
    )(proj, proj, proj, w_conv_hy, w_conv_hy, w_conv_hy)


@functools.lru_cache(maxsize=None)
def _filter_tables(seq):
    t = np.linspace(0.0, 1.0, seq)[:, None]
    wpos = 2.0 * np.pi * np.arange(seq)[:, None] / seq
    bands = np.linspace(1e-4, N_BANDS - 1, N_BANDS)[None, :]
    z = np.concatenate([t, np.cos(bands * wpos), -np.sin(bands * wpos)], axis=-1)
    z = np.pad(z, ((0, 0), (0, LANE - FILTER_EMB)))
    max_decay = math.log(DECAY_TARGET) / DECAY_FAST
    min_decay = math.log(DECAY_TARGET) / DECAY_SLOW
    deltas = np.abs(np.linspace(min_decay, max_decay, HW))
    decay = np.exp(-t * np.concatenate([deltas, deltas])[None, :])
    return z.astype(np.float32), decay.astype(np.float32)


@functools.lru_cache(maxsize=None)
def _dft_tables(seq, tk):
    n = 2 * seq
    k = np.arange(seq)[:, None]
    t = np.arange(seq)[None, :]
    ang = 2.0 * np.pi * ((k * t) % n) / n
    alt = np.where(np.arange(seq) % 2 == 0, 1.0, -1.0)
    cm, sm = np.cos(ang), np.sin(ang)
    sm[0, :] = alt
    fwd = np.stack([cm.reshape(seq // tk, tk, seq), sm.reshape(seq // tk, tk, seq)], axis=1)
    wk = np.where(np.arange(seq) == 0, 1.0, 2.0)[None, :]
    ci = (np.cos(ang.T) * wk) / n
    si = np.sin(ang.T) * 2.0 / n
    si[:, 0] = alt / n
    inv = np.concatenate([ci, si], axis=1)
    return fwd.astype(np.float32), inv.astype(np.float32)


def _filter_kernel(z_ref, w1_ref, b1_ref, w2_ref, b2_ref, w3_ref, fr_ref, dec_ref, hs_ref, hd_ref):
    fr = fr_ref[...]
    hp = functools.partial(jnp.dot, precision=HIGHEST, preferred_element_type=F32)
    h1 = jnp.sin(fr * (hp(z_ref[...], w1_ref[...]) + b1_ref[...]))
    h2 = jnp.sin(fr * (hp(h1, w2_ref[...]) + b2_ref[...]))
    filt = hp(h2, w3_ref[...]) * dec_ref[...]
    past, fut = filt[:, :HW], filt[:, HW:]
    rows = filt.shape[0]
    grow = lax.broadcasted_iota(jnp.int32, (rows, 1), 0) + pl.program_id(0) * rows
    fut = jnp.where(grow == 0, 0.0, fut)
    hs_ref[...] = past + fut
    hd_ref[...] = past - fut


def filter_gen(seq, w1, b1, w2, b2, w3, freq):
    z, decay = _filter_tables(seq)
    tl = 256
    fh = FILTER_HIDDEN
    full = lambda shape: pl.BlockSpec(shape, lambda i: (0, 0))
    out = jax.ShapeDtypeStruct((seq, HW), F32)
    return pl.pallas_call(
        _filter_kernel,
        out_shape=[out, out],
        grid=(seq // tl,),
        in_specs=[pl.BlockSpec((tl, LANE), lambda i: (i, 0)), full((LANE, fh)), full((1, fh)), full((fh, fh)),
                  full((1, fh)), full((fh, 2 * HW)), full((1, fh)), pl.BlockSpec((tl, 2 * HW), lambda i: (i, 0))],
        out_specs=[pl.BlockSpec((tl, HW), lambda i: (i, 0))] * 2,
        compiler_params=_cparams(("arbitrary",)),
        name="filter_gen",
    )(jnp.asarray(z), jnp.pad(w1, ((0, LANE - FILTER_EMB), (0, 0))), b1.reshape(1, fh), w2, b2.reshape(1, fh), w3,
      freq.reshape(1, fh), jnp.asarray(decay))


def _dft_filter_kernel(a_ref, hs_ref, hd_ref, k_ref, hs_bf, hd_bf):
    @pl.when(pl.program_id(0) == 0)
    def _():
        hs_bf[...] = hs_ref[...].astype(BF16)
        hd_bf[...] = hd_ref[...].astype(BF16)

    k_ref[0] = _dot(a_ref[0].astype(BF16), hs_bf[...])
    k_ref[1] = _dot(a_ref[1].astype(BF16), hd_bf[...])

    @pl.when(pl.program_id(0) == 0)
    def _():
        k_ref[1, 0:1, :] = _dot(a_ref[1, 0:8, :].astype(BF16), hs_bf[...])[0:1, :]


def dft_filter(seq, tk, hs, hd):
    fwd, _ = _dft_tables(seq, tk)
    return pl.pallas_call(
        _dft_filter_kernel,
        out_shape=jax.ShapeDtypeStruct((2, seq, HW), F32),
        grid=(seq // tk,),
        in_specs=[pl.BlockSpec((None, 2, tk, seq), lambda m: (m, 0, 0, 0)),
                  pl.BlockSpec((seq, HW), lambda m: (0, 0)), pl.BlockSpec((seq, HW), lambda m: (0, 0))],
        out_specs=pl.BlockSpec((2, tk, HW), lambda m: (0, m, 0)),
        scratch_shapes=[pltpu.VMEM((seq, HW), BF16), pltpu.VMEM((seq, HW), BF16)],
        compiler_params=_cparams(("arbitrary",)),
        name="dft_filter",
    )(jnp.asarray(fwd), hs, hd)


def _dft_fwd_kernel(a_ref, u_ref, k_ref, y_ref, u_bf, tbl):
    m = pl.program_id(1)

    @pl.when(m == 0)
    def _():
        u_bf[...] = u_ref[...].astype(BF16)

    @pl.when(pl.program_id(0) == 0)
    def _():
        tbl[m] = a_ref[...].astype(BF16)

    ure = _dot(tbl[m, 0], u_bf[...])
    uim = _dot(tbl[m, 1], u_bf[...])
    kre, kim = k_ref[0], k_ref[1]
    packed = (lax.broadcasted_iota(jnp.int32, (ure.shape[0], 1), 0) == 0) & (m == 0)
    y_ref[0] = jnp.where(packed, ure * kre, ure * kre - uim * kim).astype(y_ref.dtype)
    y_ref[1] = jnp.where(packed, uim * kim, ure * kim + uim * kre).astype(y_ref.dtype)


def dft_fwd(seq, tk, u, kf, *, nb, row0):
    fwd, _ = _dft_tables(seq, tk)
    blk0 = row0 // seq
    nm = seq // tk
    return pl.pallas_call(
        _dft_fwd_kernel,
        out_shape=jax.ShapeDtypeStruct((nb, 2, seq, HW), BF16),
        grid=(nb, nm),
        in_specs=[pl.BlockSpec((None, 2, tk, seq), lambda b, m: (jnp.where(b == 0, m, nm - 1), 0, 0, 0)),
                  pl.BlockSpec((seq, HW), lambda b, m: (b + blk0, 0)),
                  pl.BlockSpec((2, tk, HW), lambda b, m: (0, m, 0))],
        out_specs=pl.BlockSpec((None, 2, tk, HW), lambda b, m: (b, 0, m, 0)),
        scratch_shapes=[pltpu.VMEM((seq, HW), BF16), pltpu.VMEM((nm, 2, tk, seq), BF16)],
        compiler_params=_cparams(("arbitrary", "arbitrary"), DFT_VMEM_LIMIT),
        name="dft_fwd",
    )(jnp.asarray(fwd), u, kf)


def _dft_inv_kernel(a_ref, y_ref, u_ref, x2_ref, bias_ref, o_ref, tbl, *, seq):
    t = pl.program_id(1)

    @pl.when(pl.program_id(0) == 0)
    def _():
        tbl[t] = a_ref[...].astype(BF16)

    conv = _dot(tbl[t, :, :seq], y_ref[0]) + _dot(tbl[t, :, seq:], y_ref[1])
    u = u_ref[...].astype(F32)
    o_ref[...] = (x2_ref[...].astype(F32) * (conv + bias_ref[...] * u)).astype(o_ref.dtype)


def dft_inv(seq, tk, y, u, x2c, bias, *, nb, row0):
    _, inv = _dft_tables(seq, tk)
    tm = min(seq, 512)
    nt = seq // tm
    blk0 = row0 // tm
    rows = lambda b, t: (b * nt + t + blk0, 0)
    return pl.pallas_call(
        functools.partial(_dft_inv_kernel, seq=seq),
        out_shape=jax.ShapeDtypeStruct((nb * seq, HW), BF16),
        grid=(nb, nt),
        in_specs=[pl.BlockSpec((tm, 2 * seq), lambda b, t: (jnp.where(b == 0, t, nt - 1), 0)),
                  pl.BlockSpec((None, 2, seq, HW), lambda b, t: (b, 0, 0, 0)),
                  pl.BlockSpec((tm, HW), rows), pl.BlockSpec((tm, HW), rows),
                  pl.BlockSpec((1, HW), lambda b, t: (0, 0))],
        out_specs=pl.BlockSpec((tm, HW), lambda b, t: (b * nt + t, 0)),
        scratch_shapes=[pltpu.VMEM((nt, tm, 2 * seq), BF16)],
        compiler_params=_cparams(("arbitrary", "arbitrary"), DFT_VMEM_LIMIT),
        name="dft_inv",
    )(jnp.asarray(inv), y, u, x2c, bias.reshape(1, HW))


def hyena_group(seq, u, x2c, filt_w, bias, *, nb, row0):
    tk = min(seq, 512)
    hs, hd = filter_gen(seq, *filt_w)
    kf = dft_filter(seq, tk, hs, hd)
    y = dft_fwd(seq, tk, u, kf, nb=nb, row0=row0)
    return dft_inv(seq, tk, y, u, x2c, bias, nb=nb, row0=row0)


def mixer_ab_parts(proj, gates, state_s, w_conv_qk, g_head, w_conv_hy, w_f1, b_f1, w_f2, b_f2, w_f3, freq, hy_bias):
    ym_p, c_p, n_p, m_p = mlstm(proj, gates, w_conv_qk, g_head, None, nb=BATCH, seq=SEQ, row0=0, heads=HEADS_M)
    (ym_s,) = mlstm(proj, gates, w_conv_qk, g_head, state_s, nb=DEC_BATCH, seq=DEC_SEQ, row0=N_P, heads=1)
    u, x2c = hyena_pre(proj, w_conv_hy)
    filt_w = (w_f1, b_f1, w_f2, b_f2, w_f3, freq)
    yh_p = hyena_group(SEQ, u, x2c, filt_w, hy_bias, nb=BATCH, row0=0)
    yh_s = hyena_group(DEC_SEQ, u, x2c, filt_w, hy_bias, nb=DEC_BATCH, row0=N_P)
    state_p = (c_p, n_p[:, :, :, 0, :], m_p[:, :, :, 0, 0])
    return (ym_p, ym_s), (yh_p, yh_s), state_p


def kernel(x_prompt, x_sample, state_mlstm_C, state_mlstm_n, state_mlstm_m, cache_na_k, cache_na_v, c, c_ctx, w_ada, b_ada, g_mix, g_ffn, g_final, w_in_ab, b_gates, w_conv_qk, g_mlstm, w_conv_hy, w_filt1, b_filt1, w_filt2, b_filt2, w_filt3, filt_freq, hyena_bias, w_out_ab, w_in_c, rpb_c, w_out_c, w_up, w_conv_ffn, w_down):
    cmat = jnp.concatenate([c, c_ctx[None, :], jnp.zeros((MOD_ROWS - DEC_BATCH - 1, D), F32)], axis=0)
    mod_all = adaln_all(cmat, w_ada, b_ada).reshape(DEPTH, MOD_ROWS, 6, 1, D)
    x = (x_prompt.reshape(N_P, D), x_sample.reshape(N_S, D))
    h = None
    new_c, new_n, new_m, new_k, new_v = [], [], [], [], []
    for l in range(DEPTH):
        e = l // 2
        if l % 2 == 0:
            assert l == 0, "a later mixer A/B layer would take the fused norm of the layer before it"
            proj, gates = ab_in(x, g_mix, mod_all, l, jnp.swapaxes(w_in_ab, 1, 2), b_gates[e], e)
            state_s = (state_mlstm_C[:, e], state_mlstm_n[:, e], state_mlstm_m[:, e])
            y_m, y_h, (c_p, n_p, m_p) = mixer_ab_parts(
                proj, gates, state_s, w_conv_qk[e], g_mlstm[e], w_conv_hy[e], w_filt1[e], b_filt1[e],
                w_filt2[e], b_filt2[e], w_filt3[e], filt_freq[e], hyena_bias[e])
            a_list = [y_m, y_h]
            w_list = [(w_out_ab, (None, MW, D), lambda i, e=e: (e, 0, 0)),
                      (w_out_ab, (None, HW, D), lambda i, e=e: (e, 1, 0))]
            new_c.append(c_p[:, None])
            new_n.append(n_p[:, None])
            new_m.append(m_p[:, None])
        else:
            o, k_new, v_new = mixer_c(h, cache_na_k[:, e], cache_na_v[:, e], w_in_c, e, rpb_c[e])
            a_list = [o]
            w_list = [(w_out_c, (None, D, D), lambda i, e=e: (e, 0, 0))]
            new_k.append(k_new)
            new_v.append(v_new)
        x, h = out_proj(a_list, w_list, x, mod_all, l, 2, g_ffn, l, (l, 3, 4), name="mixer_out")
        mid = ffn_up(h, w_up, w_conv_ffn, l)
        w_list = [(w_down, (None, FF, D), lambda i, l=l: (l, 0, 0))]
        if l + 1 < DEPTH:
            x, h = out_proj([mid], w_list, x, mod_all, l, 5, g_mix, l + 1, (l + 1, 0, 1), name="ffn_down")
        else:
            y_p, y_s = out_proj([mid], w_list, x, mod_all, l, 5, g_final, 0, None, name="ffn_down_final")
    cat = lambda parts: parts[0] if len(parts) == 1 else jnp.concatenate(parts, axis=1)
    return (y_p.reshape(BATCH, SEQ, D), y_s.reshape(DEC_BATCH, DEC_SEQ, D), cat(new_c), cat(new_n), cat(new_m),
            cat(new_k), cat(new_v))
```

```python
import functools
import math

import numpy as np
import jax
import jax.numpy as jnp
from jax import lax
from jax.experimental import pallas as pl
from jax.experimental.pallas import tpu as pltpu

F32 = jnp.float32
BF16 = jnp.bfloat16

D = 1024
BATCH, SEQ = 16, 256
DEC_BATCH, DEC_SEQ = 4, 2048
PAST_LEN = 512
DEPTH = 2
GRID_W = 64
GRID_R = DEC_SEQ // GRID_W
HEADS_M = 4
MW = D // 2
HD_M = MW // HEADS_M
CHUNK = 64
HW = D // 2
N_BANDS = 16
FILTER_EMB = 2 * N_BANDS + 1
FILTER_HIDDEN = 64
DECAY_FAST, DECAY_SLOW, DECAY_TARGET = 0.3, 1.5, 1e-2
NA_HEADS = 16
NA_HD = D // NA_HEADS
NA_KH, NA_KW = 8, 16
FF = 2816
EPS = 1e-6

N_P = BATCH * SEQ
N_S = DEC_BATCH * DEC_SEQ
N_TOK = N_P + N_S
CTX_ROW = DEC_BATCH
MOD_ROWS = 8
LANE = 128
VMEM_LIMIT = 48 * 1024 * 1024
QKV_VMEM_LIMIT = 56 * 1024 * 1024
DFT_VMEM_LIMIT = 56 * 1024 * 1024
HIGHEST = lax.Precision.HIGHEST


def _cparams(sem, vmem_limit=VMEM_LIMIT):
    return pltpu.CompilerParams(dimension_semantics=sem, vmem_limit_bytes=vmem_limit)


def _mod_row(i, bm):
    return jnp.where(i < N_P // bm, CTX_ROW, (i - N_P // bm) // (DEC_SEQ // bm))


def _dot(a, b):
    return jnp.dot(a, b, preferred_element_type=F32)


def _adaln_kernel(c_ref, w_ref, b_ref, o_ref):
    cv = c_ref[...]
    s = cv * jax.nn.sigmoid(cv)
    o_ref[...] = _dot(s.astype(BF16), w_ref[...].astype(BF16)) + b_ref[...]


def adaln_all(cmat, w_ada, b_ada):
    tn = 1024
    return pl.pallas_call(
        _adaln_kernel,
        out_shape=jax.ShapeDtypeStruct((DEPTH, MOD_ROWS, 6 * D), F32),
        grid=(DEPTH, 6 * D // tn),
        in_specs=[
            pl.BlockSpec((MOD_ROWS, D), lambda l, j: (0, 0)),
            pl.BlockSpec((None, D, tn), lambda l, j: (l, 0, j)),
            pl.BlockSpec((None, 1, tn), lambda l, j: (l, 0, j)),
        ],
        out_specs=pl.BlockSpec((None, MOD_ROWS, tn), lambda l, j: (l, 0, j)),
        compiler_params=_cparams(("arbitrary", "arbitrary")),
        name="adaln",
    )(cmat, w_ada, b_ada.reshape(DEPTH, 1, 6 * D))


ROW_BM = 512
ROW_NPB = N_P // ROW_BM


def _row_specs(arr, cols):
    if isinstance(arr, tuple):
        return ([pl.BlockSpec((ROW_BM, cols), lambda i: (jnp.minimum(i, ROW_NPB - 1), 0)),
                 pl.BlockSpec((ROW_BM, cols), lambda i: (jnp.maximum(i - ROW_NPB, 0), 0))], list(arr))
    return [pl.BlockSpec((ROW_BM, cols), lambda i: (i, 0))], [arr]


def _pick_rows(refs):
    if len(refs) == 1:
        return refs[0][...]
    return jnp.where(pl.program_id(0) < ROW_NPB, refs[0][...], refs[1][...])


def _mod_spec(layer, which):
    return pl.BlockSpec((None, None, None, 1, D), lambda i: (layer, _mod_row(i, ROW_BM), which, 0, 0))


def _layer_vec_spec(layer):
    return pl.BlockSpec((None, 1, D), lambda i: (layer, 0, 0))


def _rms_mod(x, g_ref, sh_ref, sc_ref):
    y = x * lax.rsqrt(jnp.mean(x * x, axis=-1, keepdims=True) + EPS)
    return (y * g_ref[...]) * (1.0 + sc_ref[...]) + sh_ref[...]


def _out_proj_kernel(*refs, a_counts, n_x, final):
    pos = 0
    a_groups = []
    for cnt in a_counts:
        a_groups.append(refs[pos:pos + cnt])
        pos += cnt
    w_refs = refs[pos:pos + len(a_counts)]
    pos += len(a_counts)
    x_refs = refs[pos:pos + n_x]
    pos += n_x
    gt_ref, g_ref = refs[pos:pos + 2]
    pos += 2
    if not final:
        sh_ref, sc_ref = refs[pos:pos + 2]
        pos += 2
    out_a, out_b = refs[pos:pos + 2]
    wbf = refs[pos + 2:]

    @pl.when(pl.program_id(0) == 0)
    def _():
        for w_ref, wb in zip(w_refs, wbf):
            wb[...] = w_ref[...].astype(BF16)

    acc = None
    for group, wb in zip(a_groups, wbf):
        part = _dot(_pick_rows(group), wb[...])
        acc = part if acc is None else acc + part
    xn = _pick_rows(x_refs) + gt_ref[...] * acc
    if final:
        y = xn * lax.rsqrt(jnp.mean(xn * xn, axis=-1, keepdims=True) + EPS) * g_ref[...]

        @pl.when(pl.program_id(0) < ROW_NPB)
        def _():
            out_a[...] = y

        @pl.when(pl.program_id(0) >= ROW_NPB)
        def _():
            out_b[...] = y
    else:
        out_a[...] = xn
        out_b[...] = _rms_mod(xn, g_ref, sh_ref, sc_ref).astype(out_b.dtype)


def out_proj(a_list, w_list, x, mod_all, layer, gt_idx, g_all, g_layer, norm_mod_idx=None, name="out_proj"):
    final = norm_mod_idx is None
    in_specs, args, a_counts = [], [], []
    for a in a_list:
        cols = (a[0] if isinstance(a, tuple) else a).shape[1]
        specs, ops = _row_specs(a, cols)
        in_specs += specs
        args += ops
        a_counts.append(len(ops))
    w_shapes = []
    for w, block, imap in w_list:
        in_specs.append(pl.BlockSpec(block, imap, pipeline_mode=pl.Buffered(1)))
        args.append(w)
        w_shapes.append(tuple(b for b in block if b is not None))
    x_specs, x_args = _row_specs(x, D)
    in_specs += x_specs + [_mod_spec(layer, gt_idx), _layer_vec_spec(g_layer)]
    args += x_args + [mod_all, g_all.reshape(-1, 1, D)]
    if final:
        out_shape = [jax.ShapeDtypeStruct((N_P, D), F32), jax.ShapeDtypeStruct((N_S, D), F32)]
        out_specs = [pl.BlockSpec((ROW_BM, D), lambda i: (jnp.minimum(i, ROW_NPB - 1), 0)),
                     pl.BlockSpec((ROW_BM, D), lambda i: (jnp.maximum(i - ROW_NPB, 0), 0))]
    else:
        n_layer, sh_idx, sc_idx = norm_mod_idx
        in_specs += [_mod_spec(n_layer, sh_idx), _mod_spec(n_layer, sc_idx)]
        args += [mod_all, mod_all]
        out_shape = [jax.ShapeDtypeStruct((N_TOK, D), F32), jax.ShapeDtypeStruct((N_TOK, D), BF16)]
        out_specs = [pl.BlockSpec((ROW_BM, D), lambda i: (i, 0))] * 2
    return pl.pallas_call(
        functools.partial(_out_proj_kernel, a_counts=tuple(a_counts), n_x=len(x_args), final=final),
        out_shape=out_shape,
        grid=(N_TOK // ROW_BM,),
        in_specs=in_specs,
        out_specs=out_specs,
        scratch_shapes=[pltpu.VMEM(s, BF16) for s in w_shapes],
        compiler_params=_cparams(("arbitrary",)),
        name=name,
    )(*args)


def _qkv_kernel(a_ref, w_ref, o_ref, kc_ref, vc_ref, wbf):
    @pl.when(pl.program_id(0) == 0)
    def _():
        wbf[...] = w_ref[...].astype(BF16)

    a = a_ref[...]
    for part in range(3):
        acc = _dot(a, wbf[:, part * D:(part + 1) * D])
        for pp in range(NA_HEADS // 2):
            o_ref[part, pp] = acc[:, pp * 2 * NA_HD:(pp + 1) * 2 * NA_HD].astype(o_ref.dtype)
        if part > 0:
            c_ref = kc_ref if part == 1 else vc_ref

            @pl.when(pl.program_id(0) < ROW_NPB)
            def _(acc=acc, c_ref=c_ref):
                for b in range(ROW_BM // SEQ):
                    for hh in range(NA_HEADS):
                        c_ref[b, hh] = acc[b * SEQ:(b + 1) * SEQ, hh * NA_HD:(hh + 1) * NA_HD]


def qkv_proj(h, w_in, layer):
    seqs = ROW_BM // SEQ
    cache = jax.ShapeDtypeStruct((BATCH, 1, NA_HEADS, SEQ, NA_HD), F32)
    cache_spec = pl.BlockSpec((seqs, None, NA_HEADS, SEQ, NA_HD),
                              lambda i: (jnp.minimum(i, ROW_NPB - 1), 0, 0, 0, 0))
    return pl.pallas_call(
        _qkv_kernel,
        out_shape=[jax.ShapeDtypeStruct((3, NA_HEADS // 2, N_TOK, 2 * NA_HD), BF16), cache, cache],
        grid=(N_TOK // ROW_BM,),
        in_specs=[pl.BlockSpec((ROW_BM, D), lambda i: (i, 0)),
                  pl.BlockSpec((None, D, 3 * D), lambda i: (layer, 0, 0), pipeline_mode=pl.Buffered(1))],
        out_specs=[pl.BlockSpec((3, NA_HEADS // 2, ROW_BM, 2 * NA_HD), lambda i: (0, 0, i, 0)),
                   cache_spec, cache_spec],
        scratch_shapes=[pltpu.VMEM((D, 3 * D), BF16)],
        compiler_params=_cparams(("arbitrary",), QKV_VMEM_LIMIT),
        name="qkv_proj",
    )(h, w_in)


AB_MAIN = 4 * MW + 3 * HW
N_GATES = 4 * HEADS_M


def _ab_in_kernel(*refs):
    x_refs = refs[:-8]
    g_ref, sh_ref, sc_ref, w_ref, b_ref, proj_ref, gates_ref, wbf = refs[-8:]

    @pl.when(pl.program_id(0) == 0)
    def _():
        wbf[:4 * MW, :] = w_ref[:4 * MW, :].astype(BF16)
        wbf[4 * MW:AB_MAIN, :] = w_ref[4 * MW + N_GATES:, :].astype(BF16)
        wbf[AB_MAIN:AB_MAIN + N_GATES, :] = w_ref[4 * MW:4 * MW + N_GATES, :].astype(BF16)
        wbf[AB_MAIN + N_GATES:, :] = jnp.zeros((LANE - N_GATES, D), BF16)

    h = _rms_mod(_pick_rows(x_refs), g_ref, sh_ref, sc_ref).astype(BF16)
    proj_ref[...] = lax.dot_general(h, wbf[:AB_MAIN, :], NT_DIMS, preferred_element_type=F32).astype(proj_ref.dtype)
    gates_ref[...] = lax.dot_general(h, wbf[AB_MAIN:, :], NT_DIMS, preferred_element_type=F32) + b_ref[...]


def ab_in(x, g_all, mod_all, layer, w_in_t, b_gates, e):
    x_specs, x_args = _row_specs(x, D)
    b_pad = jnp.pad(b_gates, (0, LANE - N_GATES)).reshape(1, LANE)
    return pl.pallas_call(
        _ab_in_kernel,
        out_shape=[jax.ShapeDtypeStruct((N_TOK, AB_MAIN), BF16), jax.ShapeDtypeStruct((N_TOK, LANE), F32)],
        grid=(N_TOK // ROW_BM,),
        in_specs=x_specs + [_layer_vec_spec(layer), _mod_spec(layer, 0), _mod_spec(layer, 1),
                            pl.BlockSpec((None, w_in_t.shape[1], D), lambda i: (e, 0, 0),
                                         pipeline_mode=pl.Buffered(1)),
                            pl.BlockSpec((1, LANE), lambda i: (0, 0))],
        out_specs=[pl.BlockSpec((ROW_BM, AB_MAIN), lambda i: (i, 0)), pl.BlockSpec((ROW_BM, LANE), lambda i: (i, 0))],
        scratch_shapes=[pltpu.VMEM((AB_MAIN + LANE, D), BF16)],
        compiler_params=_cparams(("arbitrary",)),
        name="ab_in",
    )(*x_args, g_all.reshape(-1, 1, D), mod_all, mod_all, w_in_t, b_pad)


SEQ_BLOCK = DEC_SEQ
N_PBLK = N_P // SEQ_BLOCK


def _seq_edges(rows, is_prompt):
    r = lax.broadcasted_iota(jnp.int32, (rows, 1), 0)
    first = (r == 0) | (is_prompt & (r % SEQ == 0))
    last = (r == rows - 1) | (is_prompt & (r % SEQ == SEQ - 1))
    return first, last


def _dwconv3(x, w_ref, first, last):
    rows = x.shape[0]
    prev = jnp.where(first, 0.0, pltpu.roll(x, 1, 0))
    nxt = jnp.where(last, 0.0, pltpu.roll(x, rows - 1, 0))
    return prev * w_ref[0:1, :] + x * w_ref[1:2, :] + nxt * w_ref[2:3, :]


def _gated_gelu_of_half(y, g):
    c = math.sqrt(2.0 / math.pi)
    t = jnp.tanh(y * (2.0 * c + (8.0 * 0.044715 * c) * (y * y)))
    return (y + y * t) * g


FFN_CHUNKS = 4
FFN_HALO = 8


def _ffn_up_kernel(h_ref, wa_ref, wg_ref, wc_ref, o_ref):
    is_prompt = pl.program_id(0) < N_PBLK
    rows = SEQ_BLOCK // FFN_CHUNKS
    wa = wa_ref[...].astype(BF16)
    wg = wg_ref[...].astype(BF16)
    wc_half = 0.5 * wc_ref[...]
    zeros = jnp.zeros((FFN_HALO, o_ref.shape[1]), F32)

    def matmuls(r):
        hr = h_ref[r * rows:(r + 1) * rows, :]
        return _dot(hr, wa), _dot(hr, wg)

    def activation(r, a_prev, a_cur, a_next, g):
        win = jnp.concatenate([zeros if a_prev is None else a_prev[rows - FFN_HALO:], a_cur,
                               zeros if a_next is None else a_next[:FFN_HALO]], axis=0)
        ridx = lax.broadcasted_iota(jnp.int32, (rows + 2 * FFN_HALO, 1), 0) + (r * rows - FFN_HALO)
        first = (ridx == 0) | (is_prompt & (ridx % SEQ == 0))
        last = (ridx == SEQ_BLOCK - 1) | (is_prompt & (ridx % SEQ == SEQ - 1))
        half_conv = _dwconv3(win, wc_half, first, last)[FFN_HALO:FFN_HALO + rows]
        o_ref[r * rows:(r + 1) * rows, :] = _gated_gelu_of_half(half_conv, g).astype(o_ref.dtype)

    acts = [matmuls(0)]
    for r in range(1, FFN_CHUNKS):
        acts.append(matmuls(r))
        activation(r - 1, acts[r - 2][0] if r >= 2 else None, acts[r - 1][0], acts[r][0], acts[r - 1][1])
    activation(FFN_CHUNKS - 1, acts[-2][0], acts[-1][0], None, acts[-1][1])


def ffn_up(h, w_up, w_conv, layer):
    tc = 256
    nct = FF // tc
    return pl.pallas_call(
        _ffn_up_kernel,
        out_shape=jax.ShapeDtypeStruct((N_TOK, FF), BF16),
        grid=(N_TOK // SEQ_BLOCK, nct),
        in_specs=[
            pl.BlockSpec((SEQ_BLOCK, D), lambda i, j: (i, 0)),
            pl.BlockSpec((None, D, tc), lambda i, j: (layer, 0, j)),
            pl.BlockSpec((None, D, tc), lambda i, j: (layer, 0, j + nct)),
            pl.BlockSpec((None, 3, tc), lambda i, j: (layer, 0, j)),
        ],
        out_specs=pl.BlockSpec((SEQ_BLOCK, tc), lambda i, j: (i, j)),
        compiler_params=_cparams(("arbitrary", "arbitrary")),
        name="ffn_up",
    )(h, w_up, w_up, w_conv)


HEAD_PAIRS = NA_HEADS // 2
NT_DIMS = (((1,), (1,)), ((), ()))


def _pair_mask(shape):
    return lax.broadcasted_iota(jnp.int32, shape, len(shape) - 1) < NA_HD


def _one_head(x2, first):
    keep = _pair_mask(x2.shape) if first else ~_pair_mask(x2.shape)
    return jnp.where(keep, x2, jnp.zeros_like(x2))


def _ctx_attn_kernel(q_ref, k_ref, v_ref, o_ref, *, pairs):
    outs = []
    for pp in range(pairs):
        q2, k2, v2 = q_ref[pp], k_ref[pp], v_ref[pp]
        res = []
        for first in (True, False):
            s = lax.dot_general(_one_head(q2, first), k2, NT_DIMS, preferred_element_type=F32) * (NA_HD ** -0.5)
            m = jnp.max(s, axis=-1, keepdims=True)
            p = jnp.exp(s - m)
            l = jnp.sum(p, axis=-1, keepdims=True)
            res.append(_dot(p.astype(BF16), v2) / l)
        outs.append(jnp.where(_pair_mask(res[0].shape), res[0], res[1]))
    o_ref[...] = jnp.concatenate(outs, axis=-1).astype(o_ref.dtype)


def ctx_attention(qkv):
    pairs = 4
    spec = lambda part: pl.BlockSpec((None, pairs, SEQ, 2 * NA_HD), lambda b, h: (part, h, b, 0))
    return pl.pallas_call(
        functools.partial(_ctx_attn_kernel, pairs=pairs),
        out_shape=jax.ShapeDtypeStruct((N_P, D), BF16),
        grid=(BATCH, HEAD_PAIRS // pairs),
        in_specs=[spec(0), spec(1), spec(2)],
        out_specs=pl.BlockSpec((SEQ, pairs * 2 * NA_HD), lambda b, h: (b, h)),
        compiler_params=_cparams(("arbitrary", "arbitrary")),
        name="ctx_attn",
    )(qkv, qkv, qkv)


def _na_tables():
    q = np.arange(GRID_W)[:, None]
    w = np.arange(GRID_W)[None, :]
    idx_c = np.clip(w - q + (NA_KW - 1), 0, 2 * NA_KW - 2)
    onehot = (idx_c.reshape(1, -1) == np.arange(32)[:, None]).astype(np.float32)
    c_start = np.clip(np.arange(GRID_W) - NA_KW // 2, 0, GRID_W - NA_KW)[:, None]
    inside = (w >= c_start) & (w < c_start + NA_KW)
    cmask = np.where(inside, 0.0, -np.inf).astype(np.float32)
    return onehot, np.tile(cmask, (1, 2))


def _rpb_expand_kernel(r_ref, e_ref, o_ref):
    o_ref[...] = jnp.dot(r_ref[...], e_ref[...], precision=HIGHEST, preferred_element_type=F32)


def rpb_expand(rpb):
    onehot, _ = _na_tables()
    rp = jnp.pad(rpb, ((0, 0), (0, 1), (0, 1)))
    return pl.pallas_call(
        _rpb_expand_kernel,
        out_shape=jax.ShapeDtypeStruct((NA_HEADS, 16, GRID_W * GRID_W), F32),
        grid=(NA_HEADS,),
        in_specs=[pl.BlockSpec((None, 16, 32), lambda h: (h, 0, 0)),
                  pl.BlockSpec((32, GRID_W * GRID_W), lambda h: (0, 0))],
        out_specs=pl.BlockSpec((None, 16, GRID_W * GRID_W), lambda h: (h, 0, 0)),
        compiler_params=_cparams(("arbitrary",)),
        name="rpb_expand",
    )(rp, jnp.asarray(onehot))


NA_QROWS = 8
NA_WIN = 2 * NA_QROWS


def _na_attn_kernel(q_ref, k_ref, v_ref, kc_ref, vc_ref, t_ref, o_ref, p_loc, p_ctx):
    nq = NA_QROWS * GRID_W
    nk = NA_WIN * GRID_W
    pair = 2 * GRID_W
    nt = NT_DIMS
    lane = lax.broadcasted_iota(jnp.int32, (GRID_W, pair), 1)
    zero_tile = jnp.zeros((GRID_W, pair), BF16)
    kc = kc_ref[...].reshape(2 * NA_HD, PAST_LEN).astype(BF16)
    vc = vc_ref[...].reshape(2 * NA_HD, PAST_LEN).astype(BF16)
    for blk in range(GRID_R // NA_QROWS):
        k0 = min(max(NA_QROWS * blk - NA_KH // 2, 0), GRID_R - NA_WIN)
        q2 = q_ref[blk * nq:(blk + 1) * nq, :] * (NA_HD ** -0.5)
        kw = k_ref[k0 * GRID_W:k0 * GRID_W + nk, :]
        vw = v_ref[k0 * GRID_W:k0 * GRID_W + nk, :]
        outs = []
        for hh in range(2):
            q = _one_head(q2, hh == 0)
            s_loc = lax.dot_general(q, kw, nt, preferred_element_type=F32)
            s_ctx = _dot(q, kc)
            denoms = []
            for qi in range(NA_QROWS):
                r = NA_QROWS * blk + qi
                r_start = min(max(r - NA_KH // 2, 0), GRID_R - NA_KH)
                rows = slice(qi * GRID_W, (qi + 1) * GRID_W)
                ctx_tiles = [s_ctx[rows, c * pair:(c + 1) * pair] for c in range(PAST_LEN // pair)]
                tiles = {}
                for j in range(nk // pair):
                    kr = k0 + 2 * j
                    ok0 = r_start <= kr < r_start + NA_KH
                    ok1 = r_start <= kr + 1 < r_start + NA_KH
                    if not (ok0 or ok1):
                        continue
                    sb = s_loc[rows, j * pair:(j + 1) * pair] + t_ref[hh, kr - r + NA_KH]
                    if not (ok0 and ok1):
                        sb = jnp.where((lane < GRID_W) if ok0 else (lane >= GRID_W), sb, -jnp.inf)
                    tiles[j] = sb
                mx = functools.reduce(jnp.maximum, list(tiles.values()) + ctx_tiles)
                m = jnp.max(mx, axis=1, keepdims=True)
                acc = None
                for j in range(nk // pair):
                    if j in tiles:
                        p = jnp.exp(tiles[j] - m)
                        acc = p if acc is None else acc + p
                        p_loc[rows, j * pair:(j + 1) * pair] = p.astype(BF16)
                    else:
                        p_loc[rows, j * pair:(j + 1) * pair] = zero_tile
                for c, t in enumerate(ctx_tiles):
                    p = jnp.exp(t - m)
                    acc = acc + p
                    p_ctx[rows, c * pair:(c + 1) * pair] = p.astype(BF16)
                denoms.append(jnp.sum(acc, axis=1, keepdims=True))
            pv = _dot(p_loc[...], vw) + lax.dot_general(p_ctx[...], vc, nt, preferred_element_type=F32)
            outs.append(pv / jnp.concatenate(denoms, axis=0))
        o_ref[blk * nq:(blk + 1) * nq, :] = jnp.where(_pair_mask(outs[0].shape), outs[0], outs[1]).astype(o_ref.dtype)


def na_attention(qkv, k_ctx, v_ctx, bias_pairs):
    blk0 = N_P // DEC_SEQ
    spec = lambda part: pl.BlockSpec((None, None, DEC_SEQ, 2 * NA_HD), lambda h, b: (part, h, b + blk0, 0))
    cspec = pl.BlockSpec((None, 2, NA_HD, PAST_LEN), lambda h, b: (b, h, 0, 0))
    return pl.pallas_call(
        _na_attn_kernel,
        out_shape=jax.ShapeDtypeStruct((N_S, D), BF16),
        grid=(HEAD_PAIRS, DEC_BATCH),
        in_specs=[spec(0), spec(1), spec(2), cspec, cspec,
                  pl.BlockSpec((2, 16, GRID_W, 2 * GRID_W), lambda h, b: (h, 0, 0, 0))],
        out_specs=pl.BlockSpec((DEC_SEQ, 2 * NA_HD), lambda h, b: (b, h)),
        scratch_shapes=[pltpu.VMEM((NA_QROWS * GRID_W, NA_WIN * GRID_W), BF16),
                        pltpu.VMEM((NA_QROWS * GRID_W, PAST_LEN), BF16)],
        compiler_params=_cparams(("arbitrary", "arbitrary")),
        name="na_attn",
    )(qkv, qkv, qkv, k_ctx, v_ctx, bias_pairs)


def mixer_c(h, k_ctx, v_ctx, w_in_all, layer, rpb):
    qkv, k_new, v_new = qkv_proj(h, w_in_all, layer)
    o_p = ctx_attention(qkv)
    _, cmask2 = _na_tables()
    b15 = rpb_expand(rpb).reshape(NA_HEADS, 16, GRID_W, GRID_W)
    b17 = jnp.pad(b15, ((0, 0), (1, 0), (0, 0), (0, 0)))
    bias_pairs = jnp.concatenate([b17[:, :16], b17[:, 1:]], axis=-1) + jnp.asarray(cmask2)
    o_s = na_attention(qkv, jnp.swapaxes(k_ctx, -1, -2), jnp.swapaxes(v_ctx, -1, -2), bias_pairs)
    return (o_p, o_s), k_new, v_new


SCAN_BLOCK = HD_M


def _mlstm_kernel(*refs, seq, heads, zero_state, emit_state):
    q_ref, k_ref, v_ref, og_ref, gates_ref, wq_ref, wk_ref, gh_ref = refs[:8]
    pos = 8
    if not zero_state:
        c0_ref, n0_ref, m0_ref = refs[pos:pos + 3]
        pos += 3
    y_ref = refs[pos]
    pos += 1
    if emit_state:
        cn_ref, nn_ref, mn_ref = refs[pos:pos + 3]
        pos += 3
    pre_s, suf_s, gt_s, pret_s, suft_s, kv_s, ks_s, be_s, mk_s, cp_s, np_s, mp_s, cst, nst, mst = refs[pos:]

    blk = SCAN_BLOCK
    nc = seq // blk
    nbatch = nc * heads
    n_gates = 4 * HEADS_M
    r = lax.broadcasted_iota(jnp.int32, (seq, 1), 0)

    @pl.when(pl.program_id(1) == 0)
    def _():
        g_all = gates_ref[...]
        lf = jax.nn.log_sigmoid(g_all)
        rin = r % blk
        pre, suf = lf, lf
        for sh in [1 << i for i in range(blk.bit_length() - 1)]:
            pre = pre + jnp.where(rin >= sh, pltpu.roll(pre, sh, 0), 0.0)
            suf = suf + jnp.where(rin < blk - sh, pltpu.roll(suf, seq - sh, 0), 0.0)
        pre3, suf3 = pre.reshape(nc, blk, LANE), suf.reshape(nc, blk, LANE)
        pre_s[...] = pre3
        suf_s[...] = suf3
        gt_s[...] = jnp.swapaxes(g_all.reshape(nc, blk, LANE), 1, 2)[:, :n_gates, :]
        pret_s[...] = jnp.swapaxes(pre3, 1, 2)[:, :n_gates, :]
        suft_s[...] = jnp.swapaxes(suf3, 1, 2)[:, :n_gates, :]

    def split_heads(x):
        x3 = x.reshape(nc, blk, heads * HD_M)
        if heads == 1:
            return x3
        tiles = jnp.stack([x3[:, :, hh * HD_M:(hh + 1) * HD_M] for hh in range(heads)], axis=1)
        return tiles.reshape(nbatch, blk, HD_M)

    def per_head(x):
        if heads == 1:
            return x
        return jnp.broadcast_to(x[:, None], (nc, heads) + x.shape[1:]).reshape((nbatch,) + x.shape[1:])

    first, last = r == 0, r == seq - 1
    qc_all = _dwconv3(q_ref[...].astype(F32), wq_ref, first, last)
    q3 = split_heads(qc_all * jax.nn.sigmoid(qc_all))
    kc_all = _dwconv3(k_ref[...].astype(F32), wk_ref, first, last)
    k3 = split_heads(kc_all * jax.nn.sigmoid(kc_all) * (HD_M ** -0.5))
    qb, kb = q3.astype(BF16), k3.astype(BF16)
    vb = split_heads(v_ref[...])
    g3 = per_head(gates_ref[...].reshape(nc, blk, LANE))
    gt3 = per_head(gt_s[...])

    if zero_state:
        cst[...] = jnp.zeros_like(cst)
        nst[...] = jnp.zeros_like(nst)
        mst[...] = jnp.zeros_like(mst)
    else:
        cst[...] = c0_ref[...]
        nst[...] = n0_ref[...]
        mst[...] = jnp.broadcast_to(m0_ref[...], mst.shape)

    tt = lax.broadcasted_iota(jnp.int32, (1, blk, blk), 1)
    ss = lax.broadcasted_iota(jnp.int32, (1, blk, blk), 2)
    lane = lax.broadcasted_iota(jnp.int32, (1, 1, LANE), 2)
    sub = lax.broadcasted_iota(jnp.int32, (1, n_gates, 1), 1)
    head = lax.broadcasted_iota(jnp.int32, (nbatch, 1, 1), 0) % heads + pl.program_id(1) * heads
    hsum = None
    for d in range(2):
        i_idx = d * 2 * HEADS_M + head
        f_idx = i_idx + HEADS_M
        mask = (ss <= tt) if d == 0 else (ss >= tt)
        b3 = per_head((pre_s if d == 0 else suf_s)[...])
        bt3 = per_head((pret_s if d == 0 else suft_s)[...])
        bcol = jnp.sum(jnp.where(lane == f_idx, b3, 0.0), axis=2, keepdims=True)
        icol = jnp.sum(jnp.where(lane == i_idx, g3, 0.0), axis=2, keepdims=True)
        brow = jnp.sum(jnp.where(sub == f_idx, bt3, 0.0), axis=1, keepdims=True)
        irow = jnp.sum(jnp.where(sub == i_idx, gt3, 0.0), axis=1, keepdims=True)
        bend = bcol[:, blk - 1:blk, :] if d == 0 else bcol[:, 0:1, :]

        dmat = jnp.where(mask, bcol - brow + irow, -jnp.inf)
        mloc = jnp.max(dmat, axis=2, keepdims=True)
        qk = jnp.einsum('ctd,csd->cts', qb, kb, preferred_element_type=F32)
        s_loc = jnp.exp(dmat - mloc) * qk
        num_loc = jnp.einsum('cts,csd->ctd', s_loc.astype(BF16), vb, preferred_element_type=F32)
        den_loc = jnp.sum(s_loc, axis=2, keepdims=True)
        to_end = bend - bcol + icol
        mk = jnp.max(to_end, axis=1, keepdims=True)
        kw = k3 * jnp.exp(to_end - mk)
        kv_s[...] = jnp.einsum('cds,cse->cde', jnp.swapaxes(kw, 1, 2).astype(BF16), vb,
                               preferred_element_type=F32)
        ks_s[...] = jnp.sum(kw, axis=1, keepdims=True)
        be_s[...] = jnp.broadcast_to(bend, be_s.shape)
        mk_s[...] = jnp.broadcast_to(mk, mk_s.shape)

        def step(j, carry, d=d):
            c = j if d == 0 else nc - 1 - j
            sl = pl.ds(c * heads, heads)
            m_prev, c_prev, n_prev = mst[d], cst[d], nst[d]
            cp_s[sl] = c_prev.astype(BF16)
            np_s[sl] = n_prev
            mp_s[sl] = m_prev
            be, mkc = be_s[sl], mk_s[sl]
            m_new = jnp.maximum(be + m_prev, mkc)
            keep = jnp.exp(be + m_prev - m_new)
            add = jnp.exp(mkc - m_new)
            cst[d] = keep * c_prev + add * kv_s[sl]
            nst[d] = keep * n_prev + add * ks_s[sl]
            mst[d] = m_new
            return carry

        lax.fori_loop(0, nc, step, 0)

        m_inter = bcol + mp_s[...][:, :, 0:1]
        m_t = jnp.maximum(m_inter, mloc)
        w_state = jnp.exp(m_inter - m_t)
        w_loc = jnp.exp(mloc - m_t)
        inter = jnp.einsum('ctd,cde->cte', qb, cp_s[...], preferred_element_type=F32)
        num = w_state * inter + w_loc * num_loc
        den = w_state * jnp.sum(q3 * np_s[...], axis=2, keepdims=True) + w_loc * den_loc
        h = num / jnp.maximum(jnp.abs(den), jnp.exp(-m_t))
        hsum = h if hsum is None else hsum + h

    hn = hsum * lax.rsqrt(jnp.mean(hsum * hsum, axis=-1, keepdims=True) + EPS)
    og3 = split_heads(og_ref[...].astype(F32))
    for c in range(nc):
        for hh in range(heads):
            cols = slice(hh * HD_M, (hh + 1) * HD_M)
            e = c * heads + hh
            y_ref[c * blk:(c + 1) * blk, cols] = (hn[e] * gh_ref[:, cols] * jax.nn.sigmoid(og3[e])).astype(y_ref.dtype)
    if emit_state:
        cn_ref[...] = cst[...]
        nn_ref[...] = nst[...]
        mn_ref[...] = mst[...]


def mlstm(proj, gates, w_conv_qk, g_head, state, *, nb, seq, row0, heads):
    blk0 = row0 // seq
    nblk = seq // SCAN_BLOCK
    groups = HEADS_M // heads
    n_gates = 4 * HEADS_M
    zero_state = state is None
    col = lambda part: pl.BlockSpec((seq, heads * HD_M), lambda b, h: (b + blk0, part * groups + h))
    in_specs = [col(0), col(1), col(2), col(3),
                pl.BlockSpec((seq, LANE), lambda b, h: (b + blk0, 0)),
                pl.BlockSpec((3, heads * HD_M), lambda b, h: (0, h)),
                pl.BlockSpec((3, heads * HD_M), lambda b, h: (0, groups + h)),
                pl.BlockSpec((1, heads * HD_M), lambda b, h: (0, h))]
    args = [proj, proj, proj, proj, gates, w_conv_qk, w_conv_qk, g_head.reshape(1, MW)]
    state_specs = [pl.BlockSpec((None, 2, heads, HD_M, HD_M), lambda b, h: (b, 0, h, 0, 0)),
                   pl.BlockSpec((None, 2, heads, 1, HD_M), lambda b, h: (b, 0, h, 0, 0))]
    if not zero_state:
        c0, n0, m0 = state
        in_specs += state_specs + [pl.BlockSpec((None, 2, heads, 1, 1), lambda b, h: (b, 0, h, 0, 0))]
        args += [c0, n0.reshape(nb, 2, HEADS_M, 1, HD_M), m0.reshape(nb, 2, HEADS_M, 1, 1)]
    out_shape = [jax.ShapeDtypeStruct((nb * seq, MW), BF16)]
    out_specs = [pl.BlockSpec((seq, heads * HD_M), lambda b, h: (b, h))]
    if zero_state:
        out_shape += [jax.ShapeDtypeStruct((nb, 2, HEADS_M, HD_M, HD_M), F32),
                      jax.ShapeDtypeStruct((nb, 2, HEADS_M, 1, HD_M), F32),
                      jax.ShapeDtypeStruct((nb, 2, HEADS_M, 1, LANE), F32)]
        out_specs += state_specs + [pl.BlockSpec((None, 2, heads, 1, LANE), lambda b, h: (b, 0, h, 0, 0))]
    per_block = lambda *shape: pltpu.VMEM((nblk,) + shape, F32)
    per_entry = lambda *shape, dtype=F32: pltpu.VMEM((nblk * heads,) + shape, dtype)
    return pl.pallas_call(
        functools.partial(_mlstm_kernel, seq=seq, heads=heads, zero_state=zero_state, emit_state=zero_state),
        out_shape=out_shape,
        grid=(nb, groups),
        in_specs=in_specs,
        out_specs=out_specs,
        scratch_shapes=[per_block(SCAN_BLOCK, LANE), per_block(SCAN_BLOCK, LANE),
                        per_block(n_gates, SCAN_BLOCK), per_block(n_gates, SCAN_BLOCK), per_block(n_gates, SCAN_BLOCK),
                        per_entry(HD_M, HD_M), per_entry(1, HD_M), per_entry(1, LANE), per_entry(1, LANE),
                        per_entry(HD_M, HD_M, dtype=BF16), per_entry(1, HD_M), per_entry(1, LANE),
                        pltpu.VMEM((2, heads, HD_M, HD_M), F32), pltpu.VMEM((2, heads, 1, HD_M), F32),
                        pltpu.VMEM((2, heads, 1, LANE), F32)],
        compiler_params=_cparams(("arbitrary", "arbitrary")),
        name="mlstm_%d" % seq,
    )(*args)


def _hyena_pre_kernel(v_ref, x1_ref, x2_ref, wv_ref, w1_ref, w2_ref, u_ref, x2c_ref):
    first, last = _seq_edges(SEQ_BLOCK, pl.program_id(0) < N_PBLK)
    x1c = _dwconv3(x1_ref[...].astype(F32), w1_ref, first, last)
    u_ref[...] = (x1c * _dwconv3(v_ref[...].astype(F32), wv_ref, first, last)).astype(u_ref.dtype)
    x2c_ref[...] = _dwconv3(x2_ref[...].astype(F32), w2_ref, first, last).astype(x2c_ref.dtype)


def hyena_pre(proj, w_conv_hy):
    tc = 256
    nct = HW // tc
    c0 = 4 * MW // tc
    pcol = lambda part: pl.BlockSpec((SEQ_BLOCK, tc), lambda i, j: (i, c0 + part * nct + j))
    wcol = lambda part: pl.BlockSpec((3, tc), lambda i, j: (0, part * nct + j))
    out = jax.ShapeDtypeStruct((N_TOK, HW), BF16)
    ospec = pl.BlockSpec((SEQ_BLOCK, tc), lambda i, j: (i, j))
    return pl.pallas_call(
        _hyena_pre_kernel,
        out_shape=[out, out],
        grid=(N_TOK // SEQ_BLOCK, nct),
        in_specs=[pcol(0), pcol(1), pcol(2), wcol(0), wcol(1), wcol(2)],
        out_specs=[ospec, ospec],
        compiler_params=_cparams(("arbitrary", "arbitrary")),
        name="hyena_pre",
    )(proj, proj, proj, w_conv_hy, w_conv_hy, w_conv_hy)


@functools.lru_cache(maxsize=None)
def _filter_tables(seq):
    t = np.linspace(0.0, 1.0, seq)[:, None]
    wpos = 2.0 * np.pi * np.arange(seq)[:, None] / seq
    bands = np.linspace(1e-4, N_BANDS - 1, N_BANDS)[None, :]
    z = np.concatenate([t, np.cos(bands * wpos), -np.sin(bands * wpos)], axis=-1)
    z = np.pad(z, ((0, 0), (0, LANE - FILTER_EMB)))
    max_decay = math.log(DECAY_TARGET) / DECAY_FAST
    min_decay = math.log(DECAY_TARGET) / DECAY_SLOW
    deltas = np.abs(np.linspace(min_decay, max_decay, HW))
    decay = np.exp(-t * np.concatenate([deltas, deltas])[None, :])
    return z.astype(np.float32), decay.astype(np.float32)


@functools.lru_cache(maxsize=None)
def _dft_tables(seq, tk):
    n = 2 * seq
    k = np.arange(seq)[:, None]
    t = np.arange(seq)[None, :]
    ang = 2.0 * np.pi * ((k * t) % n) / n
    alt = np.where(np.arange(seq) % 2 == 0, 1.0, -1.0)
    cm, sm = np.cos(ang), np.sin(ang)
    sm[0, :] = alt
    fwd = np.stack([cm.reshape(seq // tk, tk, seq), sm.reshape(seq // tk, tk, seq)], axis=1)
    wk = np.where(np.arange(seq) == 0, 1.0, 2.0)[None, :]
    ci = (np.cos(ang.T) * wk) / n
    si = np.sin(ang.T) * 2.0 / n
    si[:, 0] = alt / n
    inv = np.concatenate([ci, si], axis=1)
    return fwd.astype(np.float32), inv.astype(np.float32)


def _filter_kernel(z_ref, w1_ref, b1_ref, w2_ref, b2_ref, w3_ref, fr_ref, dec_ref, hs_ref, hd_ref):
    fr = fr_ref[...]
    hp = functools.partial(jnp.dot, precision=HIGHEST, preferred_element_type=F32)
    h1 = jnp.sin(fr * (hp(z_ref[...], w1_ref[...]) + b1_ref[...]))
    h2 = jnp.sin(fr * (hp(h1, w2_ref[...]) + b2_ref[...]))
    filt = hp(h2, w3_ref[...]) * dec_ref[...]
    past, fut = filt[:, :HW], filt[:, HW:]
    rows = filt.shape[0]
    grow = lax.broadcasted_iota(jnp.int32, (rows, 1), 0) + pl.program_id(0) * rows
    fut = jnp.where(grow == 0, 0.0, fut)
    hs_ref[...] = past + fut
    hd_ref[...] = past - fut


def filter_gen(seq, w1, b1, w2, b2, w3, freq):
    z, decay = _filter_tables(seq)
    tl = 256
    fh = FILTER_HIDDEN
    full = lambda shape: pl.BlockSpec(shape, lambda i: (0, 0))
    out = jax.ShapeDtypeStruct((seq, HW), F32)
    return pl.pallas_call(
        _filter_kernel,
        out_shape=[out, out],
        grid=(seq // tl,),
        in_specs=[pl.BlockSpec((tl, LANE), lambda i: (i, 0)), full((LANE, fh)), full((1, fh)), full((fh, fh)),
                  full((1, fh)), full((fh, 2 * HW)), full((1, fh)), pl.BlockSpec((tl, 2 * HW), lambda i: (i, 0))],
        out_specs=[pl.BlockSpec((tl, HW), lambda i: (i, 0))] * 2,
        compiler_params=_cparams(("arbitrary",)),
        name="filter_gen",
    )(jnp.asarray(z), jnp.pad(w1, ((0, LANE - FILTER_EMB), (0, 0))), b1.reshape(1, fh), w2, b2.reshape(1, fh), w3,
      freq.reshape(1, fh), jnp.asarray(decay))


def _dft_filter_kernel(a_ref, hs_ref, hd_ref, k_ref, hs_bf, hd_bf):
    @pl.when(pl.program_id(0) == 0)
    def _():
        hs_bf[...] = hs_ref[...].astype(BF16)
        hd_bf[...] = hd_ref[...].astype(BF16)

    k_ref[0] = _dot(a_ref[0].astype(BF16), hs_bf[...])
    k_ref[1] = _dot(a_ref[1].astype(BF16), hd_bf[...])

    @pl.when(pl.program_id(0) == 0)
    def _():
        k_ref[1, 0:1, :] = _dot(a_ref[1, 0:8, :].astype(BF16), hs_bf[...])[0:1, :]


def dft_filter(seq, tk, hs, hd):
    fwd, _ = _dft_tables(seq, tk)
    return pl.pallas_call(
        _dft_filter_kernel,
        out_shape=jax.ShapeDtypeStruct((2, seq, HW), F32),
        grid=(seq // tk,),
        in_specs=[pl.BlockSpec((None, 2, tk, seq), lambda m: (m, 0, 0, 0)),
                  pl.BlockSpec((seq, HW), lambda m: (0, 0)), pl.BlockSpec((seq, HW), lambda m: (0, 0))],
        out_specs=pl.BlockSpec((2, tk, HW), lambda m: (0, m, 0)),
        scratch_shapes=[pltpu.VMEM((seq, HW), BF16), pltpu.VMEM((seq, HW), BF16)],
        compiler_params=_cparams(("arbitrary",)),
        name="dft_filter",
    )(jnp.asarray(fwd), hs, hd)


def _dft_fwd_kernel(a_ref, u_ref, k_ref, y_ref, u_bf, tbl):
    m = pl.program_id(1)

    @pl.when(m == 0)
    def _():
        u_bf[...] = u_ref[...].astype(BF16)

    @pl.when(pl.program_id(0) == 0)
    def _():
        tbl[m] = a_ref[...].astype(BF16)

    ure = _dot(tbl[m, 0], u_bf[...])
    uim = _dot(tbl[m, 1], u_bf[...])
    kre, kim = k_ref[0], k_ref[1]
    packed = (lax.broadcasted_iota(jnp.int32, (ure.shape[0], 1), 0) == 0) & (m == 0)
    y_ref[0] = jnp.where(packed, ure * kre, ure * kre - uim * kim).astype(y_ref.dtype)
    y_ref[1] = jnp.where(packed, uim * kim, ure * kim + uim * kre).astype(y_ref.dtype)


def dft_fwd(seq, tk, u, kf, *, nb, row0):
    fwd, _ = _dft_tables(seq, tk)
    blk0 = row0 // seq
    nm = seq // tk
    return pl.pallas_call(
        _dft_fwd_kernel,
        out_shape=jax.ShapeDtypeStruct((nb, 2, seq, HW), BF16),
        grid=(nb, nm),
        in_specs=[pl.BlockSpec((None, 2, tk, seq), lambda b, m: (jnp.where(b == 0, m, nm - 1), 0, 0, 0)),
                  pl.BlockSpec((seq, HW), lambda b, m: (b + blk0, 0)),
                  pl.BlockSpec((2, tk, HW), lambda b, m: (0, m, 0))],
        out_specs=pl.BlockSpec((None, 2, tk, HW), lambda b, m: (b, 0, m, 0)),
        scratch_shapes=[pltpu.VMEM((seq, HW), BF16), pltpu.VMEM((nm, 2, tk, seq), BF16)],
        compiler_params=_cparams(("arbitrary", "arbitrary"), DFT_VMEM_LIMIT),
        name="dft_fwd",
    )(jnp.asarray(fwd), u, kf)


def _dft_inv_kernel(a_ref, y_ref, u_ref, x2_ref, bias_ref, o_ref, tbl, *, seq):
    t = pl.program_id(1)

    @pl.when(pl.program_id(0) == 0)
    def _():
        tbl[t] = a_ref[...].astype(BF16)

    conv = _dot(tbl[t, :, :seq], y_ref[0]) + _dot(tbl[t, :, seq:], y_ref[1])
    u = u_ref[...].astype(F32)
    o_ref[...] = (x2_ref[...].astype(F32) * (conv + bias_ref[...] * u)).astype(o_ref.dtype)


def dft_inv(seq, tk, y, u, x2c, bias, *, nb, row0):
    _, inv = _dft_tables(seq, tk)
    tm = min(seq, 512)
    nt = seq // tm
    blk0 = row0 // tm
    rows = lambda b, t: (b * nt + t + blk0, 0)
    return pl.pallas_call(
        functools.partial(_dft_inv_kernel, seq=seq),
        out_shape=jax.ShapeDtypeStruct((nb * seq, HW), BF16),
        grid=(nb, nt),
        in_specs=[pl.BlockSpec((tm, 2 * seq), lambda b, t: (jnp.where(b == 0, t, nt - 1), 0)),
                  pl.BlockSpec((None, 2, seq, HW), lambda b, t: (b, 0, 0, 0)),
                  pl.BlockSpec((tm, HW), rows), pl.BlockSpec((tm, HW), rows),
                  pl.BlockSpec((1, HW), lambda b, t: (0, 0))],
        out_specs=pl.BlockSpec((tm, HW), lambda b, t: (b * nt + t, 0)),
        scratch_shapes=[pltpu.VMEM((nt, tm, 2 * seq), BF16)],
        compiler_params=_cparams(("arbitrary", "arbitrary"), DFT_VMEM_LIMIT),
        name="dft_inv",
    )(jnp.asarray(inv), y, u, x2c, bias.reshape(1, HW))


def hyena_group(seq, u, x2c, filt_w, bias, *, nb, row0):
    tk = min(seq, 512)
    hs, hd = filter_gen(seq, *filt_w)
    kf = dft_filter(seq, tk, hs, hd)
    y = dft_fwd(seq, tk, u, kf, nb=nb, row0=row0)
    return dft_inv(seq, tk, y, u, x2c, bias, nb=nb, row0=row0)


def mixer_ab_parts(proj, gates, state_s, w_conv_qk, g_head, w_conv_hy, w_f1, b_f1, w_f2, b_f2, w_f3, freq, hy_bias):
    ym_p, c_p, n_p, m_p = mlstm(proj, gates, w_conv_qk, g_head, None, nb=BATCH, seq=SEQ, row0=0, heads=HEADS_M)
    (ym_s,) = mlstm(proj, gates, w_conv_qk, g_head, state_s, nb=DEC_BATCH, seq=DEC_SEQ, row0=N_P, heads=1)
    u, x2c = hyena_pre(proj, w_conv_hy)
    filt_w = (w_f1, b_f1, w_f2, b_f2, w_f3, freq)
    yh_p = hyena_group(SEQ, u, x2c, filt_w, hy_bias, nb=BATCH, row0=0)
    yh_s = hyena_group(DEC_SEQ, u, x2c, filt_w, hy_bias, nb=DEC_BATCH, row0=N_P)
    state_p = (c_p, n_p[:, :, :, 0, :], m_p[:, :, :, 0, 0])
    return (ym_p, ym_s), (yh_p, yh_s), state_p


def kernel(x_prompt, x_sample, state_mlstm_C, state_mlstm_n, state_mlstm_m, cache_na_k, cache_na_v, c, c_ctx, w_ada, b_ada, g_mix, g_ffn, g_final, w_in_ab, b_gates, w_conv_qk, g_mlstm, w_conv_hy, w_filt1, b_filt1, w_filt2, b_filt2, w_filt3, filt_freq, hyena_bias, w_out_ab, w_in_c, rpb_c, w_out_c, w_up, w_conv_ffn, w_down):
    cmat = jnp.concatenate([c, c_ctx[None, :], jnp.zeros((MOD_ROWS - DEC_BATCH - 1, D), F32)], axis=0)
    mod_all = adaln_all(cmat, w_ada, b_ada).reshape(DEPTH, MOD_ROWS, 6, 1, D)
    x = (x_prompt.reshape(N_P, D), x_sample.reshape(N_S, D))
    h = None
    new_c, new_n, new_m, new_k, new_v = [], [], [], [], []
    for l in range(DEPTH):
        e = l // 2
        if l % 2 == 0:
            assert l == 0, "a later mixer A/B layer would take the fused norm of the layer before it"
            proj, gates = ab_in(x, g_mix, mod_all, l, jnp.swapaxes(w_in_ab, 1, 2), b_gates[e], e)
            state_s = (state_mlstm_C[:, e], state_mlstm_n[:, e], state_mlstm_m[:, e])
            y_m, y_h, (c_p, n_p, m_p) = mixer_ab_parts(
                proj, gates, state_s, w_conv_qk[e], g_mlstm[e], w_conv_hy[e], w_filt1[e], b_filt1[e],
                w_filt2[e], b_filt2[e], w_filt3[e], filt_freq[e], hyena_bias[e])
            a_list = [y_m, y_h]
            w_list = [(w_out_ab, (None, MW, D), lambda i, e=e: (e, 0, 0)),
                      (w_out_ab, (None, HW, D), lambda i, e=e: (e, 1, 0))]
            new_c.append(c_p[:, None])
            new_n.append(n_p[:, None])
            new_m.append(m_p[:, None])
        else:
            o, k_new, v_new = mixer_c(h, cache_na_k[:, e], cache_na_v[:, e], w_in_c, e, rpb_c[e])
            a_list = [o]
            w_list = [(w_out_c, (None, D, D), lambda i, e=e: (e, 0, 0))]
            new_k.append(k_new)
            new_v.append(v_new)
        x, h = out_proj(a_list, w_list, x, mod_all, l, 2, g_ffn, l, (l, 3, 4), name="mixer_out")
        mid = ffn_up(h, w_up, w_conv_ffn, l)
        w_list = [(w_down, (None, FF, D), lambda i, l=l: (l, 0, 0))]
        if l + 1 < DEPTH:
            x, h = out_proj([mid], w_list, x, mod_all, l, 5, g_mix, l + 1, (l + 1, 0, 1), name="ffn_down")
        else:
            y_p, y_s = out_proj([mid], w_list, x, mod_all, l, 5, g_final, 0, None, name="ffn_down_final")
    cat = lambda parts: parts[0] if len(parts) == 1 else jnp.concatenate(parts, axis=1)
    return (y_p.reshape(BATCH, SEQ, D), y_s.reshape(DEC_BATCH, DEC_SEQ, D), cat(new_c), cat(new_n), cat(new_m),
            cat(new_k), cat(new_v))
```

```python
import functools
import math

import numpy as np
import jax
import jax.numpy as jnp
from jax import lax
from jax.experimental import pallas as pl
from jax.experimental.pallas import tpu as pltpu

F32 = jnp.float32
BF16 = jnp.bfloat16

D = 1024
BATCH, SEQ = 16, 256
DEC_BATCH, DEC_SEQ = 4, 2048
PAST_LEN = 512
DEPTH = 2
GRID_W = 64
GRID_R = DEC_SEQ // GRID_W
HEADS_M = 4
MW = D // 2
HD_M = MW // HEADS_M
CHUNK = 64
HW = D // 2
N_BANDS = 16
FILTER_EMB = 2 * N_BANDS + 1
FILTER_HIDDEN = 64
DECAY_FAST, DECAY_SLOW, DECAY_TARGET = 0.3, 1.5, 1e-2
NA_HEADS = 16
NA_HD = D // NA_HEADS
NA_KH, NA_KW = 8, 16
FF = 2816
EPS = 1e-6

N_P = BATCH * SEQ
N_S = DEC_BATCH * DEC_SEQ
N_TOK = N_P + N_S
CTX_ROW = DEC_BATCH
MOD_ROWS = 8
LANE = 128
VMEM_LIMIT = 48 * 1024 * 1024
QKV_VMEM_LIMIT = 56 * 1024 * 1024
DFT_VMEM_LIMIT = 56 * 1024 * 1024
HIGHEST = lax.Precision.HIGHEST


def _cparams(sem, vmem_limit=VMEM_LIMIT):
    return pltpu.CompilerParams(dimension_semantics=sem, vmem_limit_bytes=vmem_limit)


def _mod_row(i, bm):
    return jnp.where(i < N_P // bm, CTX_ROW, (i - N_P // bm) // (DEC_SEQ // bm))


def _dot(a, b):
    return jnp.dot(a, b, preferred_element_type=F32)


def _adaln_kernel(c_ref, w_ref, b_ref, o_ref):
    cv = c_ref[...]
    s = cv * jax.nn.sigmoid(cv)
    o_ref[...] = _dot(s.astype(BF16), w_ref[...].astype(BF16)) + b_ref[...]


def adaln_all(cmat, w_ada, b_ada):
    tn = 1024
    return pl.pallas_call(
        _adaln_kernel,
        out_shape=jax.ShapeDtypeStruct((DEPTH, MOD_ROWS, 6 * D), F32),
        grid=(DEPTH, 6 * D // tn),
        in_specs=[
            pl.BlockSpec((MOD_ROWS, D), lambda l, j: (0, 0)),
            pl.BlockSpec((None, D, tn), lambda l, j: (l, 0, j)),
            pl.BlockSpec((None, 1, tn), lambda l, j: (l, 0, j)),
        ],
        out_specs=pl.BlockSpec((None, MOD_ROWS, tn), lambda l, j: (l, 0, j)),
        compiler_params=_cparams(("arbitrary", "arbitrary")),
        name="adaln",
    )(cmat, w_ada, b_ada.reshape(DEPTH, 1, 6 * D))


ROW_BM = 512
ROW_NPB = N_P // ROW_BM


def _row_specs(arr, cols):
    if isinstance(arr, tuple):
        return ([pl.BlockSpec((ROW_BM, cols), lambda i: (jnp.minimum(i, ROW_NPB - 1), 0)),
                 pl.BlockSpec((ROW_BM, cols), lambda i: (jnp.maximum(i - ROW_NPB, 0), 0))], list(arr))
    return [pl.BlockSpec((ROW_BM, cols), lambda i: (i, 0))], [arr]


def _pick_rows(refs):
    if len(refs) == 1:
        return refs[0][...]
    return jnp.where(pl.program_id(0) < ROW_NPB, refs[0][...], refs[1][...])


def _mod_spec(layer, which):
    return pl.BlockSpec((None, None, None, 1, D), lambda i: (layer, _mod_row(i, ROW_BM), which, 0, 0))


def _layer_vec_spec(layer):
    return pl.BlockSpec((None, 1, D), lambda i: (layer, 0, 0))


def _rms_mod(x, g_ref, sh_ref, sc_ref):
    y = x * lax.rsqrt(jnp.mean(x * x, axis=-1, keepdims=True) + EPS)
    return (y * g_ref[...]) * (1.0 + sc_ref[...]) + sh_ref[...]


def _out_proj_kernel(*refs, a_counts, n_x, final):
    pos = 0
    a_groups = []
    for cnt in a_counts:
        a_groups.append(refs[pos:pos + cnt])
        pos += cnt
    w_refs = refs[pos:pos + len(a_counts)]
    pos += len(a_counts)
    x_refs = refs[pos:pos + n_x]
    pos += n_x
    gt_ref, g_ref = refs[pos:pos + 2]
    pos += 2
    if not final:
        sh_ref, sc_ref = refs[pos:pos + 2]
        pos += 2
    out_a, out_b = refs[pos:pos + 2]
    wbf = refs[pos + 2:]

    @pl.when(pl.program_id(0) == 0)
    def _():
        for w_ref, wb in zip(w_refs, wbf):
            wb[...] = w_ref[...].astype(BF16)

    acc = None
    for group, wb in zip(a_groups, wbf):
        part = _dot(_pick_rows(group), wb[...])
        acc = part if acc is None else acc + part
    xn = _pick_rows(x_refs) + gt_ref[...] * acc
    if final:
        y = xn * lax.rsqrt(jnp.mean(xn * xn, axis=-1, keepdims=True) + EPS) * g_ref[...]

        @pl.when(pl.program_id(0) < ROW_NPB)
        def _():
            out_a[...] = y

        @pl.when(pl.program_id(0) >= ROW_NPB)
        def _():
            out_b[...] = y
    else:
        out_a[...] = xn
        out_b[...] = _rms_mod(xn, g_ref, sh_ref, sc_ref).astype(out_b.dtype)


def out_proj(a_list, w_list, x, mod_all, layer, gt_idx, g_all, g_layer, norm_mod_idx=None, name="out_proj"):
    final = norm_mod_idx is None
    in_specs, args, a_counts = [], [], []
    for a in a_list:
        cols = (a[0] if isinstance(a, tuple) else a).shape[1]
        specs, ops = _row_specs(a, cols)
        in_specs += specs
        args += ops
        a_counts.append(len(ops))
    w_shapes = []
    for w, block, imap in w_list:
        in_specs.append(pl.BlockSpec(block, imap, pipeline_mode=pl.Buffered(1)))
        args.append(w)
        w_shapes.append(tuple(b for b in block if b is not None))
    x_specs, x_args = _row_specs(x, D)
    in_specs += x_specs + [_mod_spec(layer, gt_idx), _layer_vec_spec(g_layer)]
    args += x_args + [mod_all, g_all.reshape(-1, 1, D)]
    if final:
        out_shape = [jax.ShapeDtypeStruct((N_P, D), F32), jax.ShapeDtypeStruct((N_S, D), F32)]
        out_specs = [pl.BlockSpec((ROW_BM, D), lambda i: (jnp.minimum(i, ROW_NPB - 1), 0)),
                     pl.BlockSpec((ROW_BM, D), lambda i: (jnp.maximum(i - ROW_NPB, 0), 0))]
    else:
        n_layer, sh_idx, sc_idx = norm_mod_idx
        in_specs += [_mod_spec(n_layer, sh_idx), _mod_spec(n_layer, sc_idx)]
        args += [mod_all, mod_all]
        out_shape = [jax.ShapeDtypeStruct((N_TOK, D), F32), jax.ShapeDtypeStruct((N_TOK, D), BF16)]
        out_specs = [pl.BlockSpec((ROW_BM, D), lambda i: (i, 0))] * 2
    return pl.pallas_call(
        functools.partial(_out_proj_kernel, a_counts=tuple(a_counts), n_x=len(x_args), final=final),
        out_shape=out_shape,
        grid=(N_TOK // ROW_BM,),
        in_specs=in_specs,
        out_specs=out_specs,
        scratch_shapes=[pltpu.VMEM(s, BF16) for s in w_shapes],
        compiler_params=_cparams(("arbitrary",)),
        name=name,
    )(*args)


def _qkv_kernel(a_ref, w_ref, o_ref, kc_ref, vc_ref, wbf):
    @pl.when(pl.program_id(0) == 0)
    def _():
        wbf[...] = w_ref[...].astype(BF16)

    a = a_ref[...]
    for part in range(3):
        acc = _dot(a, wbf[:, part * D:(part + 1) * D])
        for pp in range(NA_HEADS // 2):
            o_ref[part, pp] = acc[:, pp * 2 * NA_HD:(pp + 1) * 2 * NA_HD].astype(o_ref.dtype)
        if part > 0:
            c_ref = kc_ref if part == 1 else vc_ref

            @pl.when(pl.program_id(0) < ROW_NPB)
            def _(acc=acc, c_ref=c_ref):
                for b in range(ROW_BM // SEQ):
                    for hh in range(NA_HEADS):
                        c_ref[b, hh] = acc[b * SEQ:(b + 1) * SEQ, hh * NA_HD:(hh + 1) * NA_HD]


def qkv_proj(h, w_in, layer):
    seqs = ROW_BM // SEQ
    cache = jax.ShapeDtypeStruct((BATCH, 1, NA_HEADS, SEQ, NA_HD), F32)
    cache_spec = pl.BlockSpec((seqs, None, NA_HEADS, SEQ, NA_HD),
                              lambda i: (jnp.minimum(i, ROW_NPB - 1), 0, 0, 0, 0))
    return pl.pallas_call(
        _qkv_kernel,
        out_shape=[jax.ShapeDtypeStruct((3, NA_HEADS // 2, N_TOK, 2 * NA_HD), BF16), cache, cache],
        grid=(N_TOK // ROW_BM,),
        in_specs=[pl.BlockSpec((ROW_BM, D), lambda i: (i, 0)),
                  pl.BlockSpec((None, D, 3 * D), lambda i: (layer, 0, 0), pipeline_mode=pl.Buffered(1))],
        out_specs=[pl.BlockSpec((3, NA_HEADS // 2, ROW_BM, 2 * NA_HD), lambda i: (0, 0, i, 0)),
                   cache_spec, cache_spec],
        scratch_shapes=[pltpu.VMEM((D, 3 * D), BF16)],
        compiler_params=_cparams(("arbitrary",), QKV_VMEM_LIMIT),
        name="qkv_proj",
    )(h, w_in)


AB_MAIN = 4 * MW + 3 * HW
N_GATES = 4 * HEADS_M


def _ab_in_kernel(*refs):
    x_refs = refs[:-8]
    g_ref, sh_ref, sc_ref, w_ref, b_ref, proj_ref, gates_ref, wbf = refs[-8:]

    @pl.when(pl.program_id(0) == 0)
    def _():
        wbf[:4 * MW, :] = w_ref[:4 * MW, :].astype(BF16)
        wbf[4 * MW:AB_MAIN, :] = w_ref[4 * MW + N_GATES:, :].astype(BF16)
        wbf[AB_MAIN:AB_MAIN + N_GATES, :] = w_ref[4 * MW:4 * MW + N_GATES, :].astype(BF16)
        wbf[AB_MAIN + N_GATES:, :] = jnp.zeros((LANE - N_GATES, D), BF16)

    h = _rms_mod(_pick_rows(x_refs), g_ref, sh_ref, sc_ref).astype(BF16)
    proj_ref[...] = lax.dot_general(h, wbf[:AB_MAIN, :], NT_DIMS, preferred_element_type=F32).astype(proj_ref.dtype)
    gates_ref[...] = lax.dot_general(h, wbf[AB_MAIN:, :], NT_DIMS, preferred_element_type=F32) + b_ref[...]


def ab_in(x, g_all, mod_all, layer, w_in_t, b_gates, e):
    x_specs, x_args = _row_specs(x, D)
    b_pad = jnp.pad(b_gates, (0, LANE - N_GATES)).reshape(1, LANE)
    return pl.pallas_call(
        _ab_in_kernel,
        out_shape=[jax.ShapeDtypeStruct((N_TOK, AB_MAIN), BF16), jax.ShapeDtypeStruct((N_TOK, LANE), F32)],
        grid=(N_TOK // ROW_BM,),
        in_specs=x_specs + [_layer_vec_spec(layer), _mod_spec(layer, 0), _mod_spec(layer, 1),
                            pl.BlockSpec((None, w_in_t.shape[1], D), lambda i: (e, 0, 0),
                                         pipeline_mode=pl.Buffered(1)),
                            pl.BlockSpec((1, LANE), lambda i: (0, 0))],
        out_specs=[pl.BlockSpec((ROW_BM, AB_MAIN), lambda i: (i, 0)), pl.BlockSpec((ROW_BM, LANE), lambda i: (i, 0))],
        scratch_shapes=[pltpu.VMEM((AB_MAIN + LANE, D), BF16)],
        compiler_params=_cparams(("arbitrary",)),
        name="ab_in",
    )(*x_args, g_all.reshape(-1, 1, D), mod_all, mod_all, w_in_t, b_pad)


SEQ_BLOCK = DEC_SEQ
N_PBLK = N_P // SEQ_BLOCK


def _seq_edges(rows, is_prompt):
    r = lax.broadcasted_iota(jnp.int32, (rows, 1), 0)
    first = (r == 0) | (is_prompt & (r % SEQ == 0))
    last = (r == rows - 1) | (is_prompt & (r % SEQ == SEQ - 1))
    return first, last


def _dwconv3(x, w_ref, first, last):
    rows = x.shape[0]
    prev = jnp.where(first, 0.0, pltpu.roll(x, 1, 0))
    nxt = jnp.where(last, 0.0, pltpu.roll(x, rows - 1, 0))
    return prev * w_ref[0:1, :] + x * w_ref[1:2, :] + nxt * w_ref[2:3, :]


def _gated_gelu_of_half(y, g):
    c = math.sqrt(2.0 / math.pi)
    t = jnp.tanh(y * (2.0 * c + (8.0 * 0.044715 * c) * (y * y)))
    return (y + y * t) * g


FFN_CHUNKS = 4
FFN_HALO = 8


def _ffn_up_kernel(h_ref, wa_ref, wg_ref, wc_ref, o_ref):
    is_prompt = pl.program_id(0) < N_PBLK
    rows = SEQ_BLOCK // FFN_CHUNKS
    wa = wa_ref[...].astype(BF16)
    wg = wg_ref[...].astype(BF16)
    wc_half = 0.5 * wc_ref[...]
    zeros = jnp.zeros((FFN_HALO, o_ref.shape[1]), F32)

    def matmuls(r):
        hr = h_ref[r * rows:(r + 1) * rows, :]
        return _dot(hr, wa), _dot(hr, wg)

    def activation(r, a_prev, a_cur, a_next, g):
        win = jnp.concatenate([zeros if a_prev is None else a_prev[rows - FFN_HALO:], a_cur,
                               zeros if a_next is None else a_next[:FFN_HALO]], axis=0)
        ridx = lax.broadcasted_iota(jnp.int32, (rows + 2 * FFN_HALO, 1), 0) + (r * rows - FFN_HALO)
        first = (ridx == 0) | (is_prompt & (ridx % SEQ == 0))
        last = (ridx == SEQ_BLOCK - 1) | (is_prompt & (ridx % SEQ == SEQ - 1))
        half_conv = _dwconv3(win, wc_half, first, last)[FFN_HALO:FFN_HALO + rows]
        o_ref[r * rows:(r + 1) * rows, :] = _gated_gelu_of_half(half_conv, g).astype(o_ref.dtype)

    acts = [matmuls(0)]
    for r in range(1, FFN_CHUNKS):
        acts.append(matmuls(r))
        activation(r - 1, acts[r - 2][0] if r >= 2 else None, acts[r - 1][0], acts[r][0], acts[r - 1][1])
    activation(FFN_CHUNKS - 1, acts[-2][0], acts[-1][0], None, acts[-1][1])


def ffn_up(h, w_up, w_conv, layer):
    tc = 256
    nct = FF // tc
    return pl.pallas_call(
        _ffn_up_kernel,
        out_shape=jax.ShapeDtypeStruct((N_TOK, FF), BF16),
        grid=(N_TOK // SEQ_BLOCK, nct),
        in_specs=[
            pl.BlockSpec((SEQ_BLOCK, D), lambda i, j: (i, 0)),
            pl.BlockSpec((None, D, tc), lambda i, j: (layer, 0, j)),
            pl.BlockSpec((None, D, tc), lambda i, j: (layer, 0, j + nct)),
            pl.BlockSpec((None, 3, tc), lambda i, j: (layer, 0, j)),
        ],
        out_specs=pl.BlockSpec((SEQ_BLOCK, tc), lambda i, j: (i, j)),
        compiler_params=_cparams(("arbitrary", "arbitrary")),
        name="ffn_up",
    )(h, w_up, w_up, w_conv)


HEAD_PAIRS = NA_HEADS // 2
NT_DIMS = (((1,), (1,)), ((), ()))


def _pair_mask(shape):
    return lax.broadcasted_iota(jnp.int32, shape, len(shape) - 1) < NA_HD


def _one_head(x2, first):
    keep = _pair_mask(x2.shape) if first else ~_pair_mask(x2.shape)
    return jnp.where(keep, x2, jnp.zeros_like(x2))


def _ctx_attn_kernel(q_ref, k_ref, v_ref, o_ref, *, pairs):
    outs = []
    for pp in range(pairs):
        q2, k2, v2 = q_ref[pp], k_ref[pp], v_ref[pp]
        res = []
        for first in (True, False):
            s = lax.dot_general(_one_head(q2, first), k2, NT_DIMS, preferred_element_type=F32) * (NA_HD ** -0.5)
            m = jnp.max(s, axis=-1, keepdims=True)
            p = jnp.exp(s - m)
            l = jnp.sum(p, axis=-1, keepdims=True)
            res.append(_dot(p.astype(BF16), v2) / l)
        outs.append(jnp.where(_pair_mask(res[0].shape), res[0], res[1]))
    o_ref[...] = jnp.concatenate(outs, axis=-1).astype(o_ref.dtype)


def ctx_attention(qkv):
    pairs = 4
    spec = lambda part: pl.BlockSpec((None, pairs, SEQ, 2 * NA_HD), lambda b, h: (part, h, b, 0))
    return pl.pallas_call(
        functools.partial(_ctx_attn_kernel, pairs=pairs),
        out_shape=jax.ShapeDtypeStruct((N_P, D), BF16),
        grid=(BATCH, HEAD_PAIRS // pairs),
        in_specs=[spec(0), spec(1), spec(2)],
        out_specs=pl.BlockSpec((SEQ, pairs * 2 * NA_HD), lambda b, h: (b, h)),
        compiler_params=_cparams(("arbitrary", "arbitrary")),
        name="ctx_attn",
    )(qkv, qkv, qkv)


def _na_tables():
    q = np.arange(GRID_W)[:, None]
    w = np.arange(GRID_W)[None, :]
    idx_c = np.clip(w - q + (NA_KW - 1), 0, 2 * NA_KW - 2)
    onehot = (idx_c.reshape(1, -1) == np.arange(32)[:, None]).astype(np.float32)
    c_start = np.clip(np.arange(GRID_W) - NA_KW // 2, 0, GRID_W - NA_KW)[:, None]
    inside = (w >= c_start) & (w < c_start + NA_KW)
    cmask = np.where(inside, 0.0, -np.inf).astype(np.float32)
    return onehot, np.tile(cmask, (1, 2))


def _rpb_expand_kernel(r_ref, e_ref, o_ref):
    o_ref[...] = jnp.dot(r_ref[...], e_ref[...], precision=HIGHEST, preferred_element_type=F32)


def rpb_expand(rpb):
    onehot, _ = _na_tables()
    rp = jnp.pad(rpb, ((0, 0), (0, 1), (0, 1)))
    return pl.pallas_call(
        _rpb_expand_kernel,
        out_shape=jax.ShapeDtypeStruct((NA_HEADS, 16, GRID_W * GRID_W), F32),
        grid=(NA_HEADS,),
        in_specs=[pl.BlockSpec((None, 16, 32), lambda h: (h, 0, 0)),
                  pl.BlockSpec((32, GRID_W * GRID_W), lambda h: (0, 0))],
        out_specs=pl.BlockSpec((None, 16, GRID_W * GRID_W), lambda h: (h, 0, 0)),
        compiler_params=_cparams(("arbitrary",)),
        name="rpb_expand",
    )(rp, jnp.asarray(onehot))


NA_QROWS = 8
NA_WIN = 2 * NA_QROWS


def _na_attn_kernel(q_ref, k_ref, v_ref, kc_ref, vc_ref, t_ref, o_ref, p_loc, p_ctx):
    nq = NA_QROWS * GRID_W
    nk = NA_WIN * GRID_W
    pair = 2 * GRID_W
    nt = NT_DIMS
    lane = lax.broadcasted_iota(jnp.int32, (GRID_W, pair), 1)
    zero_tile = jnp.zeros((GRID_W, pair), BF16)
    kc = kc_ref[...].reshape(2 * NA_HD, PAST_LEN).astype(BF16)
    vc = vc_ref[...].reshape(2 * NA_HD, PAST_LEN).astype(BF16)
    for blk in range(GRID_R // NA_QROWS):
        k0 = min(max(NA_QROWS * blk - NA_KH // 2, 0), GRID_R - NA_WIN)
        q2 = q_ref[blk * nq:(blk + 1) * nq, :] * (NA_HD ** -0.5)
        kw = k_ref[k0 * GRID_W:k0 * GRID_W + nk, :]
        vw = v_ref[k0 * GRID_W:k0 * GRID_W + nk, :]
        outs = []
        for hh in range(2):
            q = _one_head(q2, hh == 0)
            s_loc = lax.dot_general(q, kw, nt, preferred_element_type=F32)
            s_ctx = _dot(q, kc)
            denoms = []
            for qi in range(NA_QROWS):
                r = NA_QROWS * blk + qi
                r_start = min(max(r - NA_KH // 2, 0), GRID_R - NA_KH)
                rows = slice(qi * GRID_W, (qi + 1) * GRID_W)
                ctx_tiles = [s_ctx[rows, c * pair:(c + 1) * pair] for c in range(PAST_LEN // pair)]
                tiles = {}
                for j in range(nk // pair):
                    kr = k0 + 2 * j
                    ok0 = r_start <= kr < r_start + NA_KH
                    ok1 = r_start <= kr + 1 < r_start + NA_KH
                    if not (ok0 or ok1):
                        continue
                    sb = s_loc[rows, j * pair:(j + 1) * pair] + t_ref[hh, kr - r + NA_KH]
                    if not (ok0 and ok1):
                        sb = jnp.where((lane < GRID_W) if ok0 else (lane >= GRID_W), sb, -jnp.inf)
                    tiles[j] = sb
                mx = functools.reduce(jnp.maximum, list(tiles.values()) + ctx_tiles)
                m = jnp.max(mx, axis=1, keepdims=True)
                acc = None
                for j in range(nk // pair):
                    if j in tiles:
                        p = jnp.exp(tiles[j] - m)
                        acc = p if acc is None else acc + p
                        p_loc[rows, j * pair:(j + 1) * pair] = p.astype(BF16)
                    else:
                        p_loc[rows, j * pair:(j + 1) * pair] = zero_tile
                for c, t in enumerate(ctx_tiles):
                    p = jnp.exp(t - m)
                    acc = acc + p
                    p_ctx[rows, c * pair:(c + 1) * pair] = p.astype(BF16)
                denoms.append(jnp.sum(acc, axis=1, keepdims=True))
            pv = _dot(p_loc[...], vw) + lax.dot_general(p_ctx[...], vc, nt, preferred_element_type=F32)
            outs.append(pv / jnp.concatenate(denoms, axis=0))
        o_ref[blk * nq:(blk + 1) * nq, :] = jnp.where(_pair_mask(outs[0].shape), outs[0], outs[1]).astype(o_ref.dtype)


def na_attention(qkv, k_ctx, v_ctx, bias_pairs):
    blk0 = N_P // DEC_SEQ
    spec = lambda part: pl.BlockSpec((None, None, DEC_SEQ, 2 * NA_HD), lambda h, b: (part, h, b + blk0, 0))
    cspec = pl.BlockSpec((None, 2, NA_HD, PAST_LEN), lambda h, b: (b, h, 0, 0))
    return pl.pallas_call(
        _na_attn_kernel,
        out_shape=jax.ShapeDtypeStruct((N_S, D), BF16),
        grid=(HEAD_PAIRS, DEC_BATCH),
        in_specs=[spec(0), spec(1), spec(2), cspec, cspec,
                  pl.BlockSpec((2, 16, GRID_W, 2 * GRID_W), lambda h, b: (h, 0, 0, 0))],
        out_specs=pl.BlockSpec((DEC_SEQ, 2 * NA_HD), lambda h, b: (b, h)),
        scratch_shapes=[pltpu.VMEM((NA_QROWS * GRID_W, NA_WIN * GRID_W), BF16),
                        pltpu.VMEM((NA_QROWS * GRID_W, PAST_LEN), BF16)],
        compiler_params=_cparams(("arbitrary", "arbitrary")),
        name="na_attn",
    )(qkv, qkv, qkv, k_ctx, v_ctx, bias_pairs)


def mixer_c(h, k_ctx, v_ctx, w_in_all, layer, rpb):
    qkv, k_new, v_new = qkv_proj(h, w_in_all, layer)
    o_p = ctx_attention(qkv)
    _, cmask2 = _na_tables()
    b15 = rpb_expand(rpb).reshape(NA_HEADS, 16, GRID_W, GRID_W)
    b17 = jnp.pad(b15, ((0, 0), (1, 0), (0, 0), (0, 0)))
    bias_pairs = jnp.concatenate([b17[:, :16], b17[:, 1:]], axis=-1) + jnp.asarray(cmask2)
    o_s = na_attention(qkv, jnp.swapaxes(k_ctx, -1, -2), jnp.swapaxes(v_ctx, -1, -2), bias_pairs)
    return (o_p, o_s), k_new, v_new


SCAN_BLOCK = HD_M


def _mlstm_kernel(*refs, seq, heads, zero_state, emit_state):
    q_ref, k_ref, v_ref, og_ref, gates_ref, wq_ref, wk_ref, gh_ref = refs[:8]
    pos = 8
    if not zero_state:
        c0_ref, n0_ref, m0_ref = refs[pos:pos + 3]
        pos += 3
    y_ref = refs[pos]
    pos += 1
    if emit_state:
        cn_ref, nn_ref, mn_ref = refs[pos:pos + 3]
        pos += 3
    pre_s, suf_s, gt_s, pret_s, suft_s, kv_s, ks_s, be_s, mk_s, cp_s, np_s, mp_s, cst, nst, mst = refs[pos:]

    blk = SCAN_BLOCK
    nc = seq // blk
    nbatch = nc * heads
    n_gates = 4 * HEADS_M
    r = lax.broadcasted_iota(jnp.int32, (seq, 1), 0)

    @pl.when(pl.program_id(1) == 0)
    def _():
        g_all = gates_ref[...]
        lf = jax.nn.log_sigmoid(g_all)
        rin = r % blk
        pre, suf = lf, lf
        for sh in [1 << i for i in range(blk.bit_length() - 1)]:
            pre = pre + jnp.where(rin >= sh, pltpu.roll(pre, sh, 0), 0.0)
            suf = suf + jnp.where(rin < blk - sh, pltpu.roll(suf, seq - sh, 0), 0.0)
        pre3, suf3 = pre.reshape(nc, blk, LANE), suf.reshape(nc, blk, LANE)
        pre_s[...] = pre3
        suf_s[...] = suf3
        gt_s[...] = jnp.swapaxes(g_all.reshape(nc, blk, LANE), 1, 2)[:, :n_gates, :]
        pret_s[...] = jnp.swapaxes(pre3, 1, 2)[:, :n_gates, :]
        suft_s[...] = jnp.swapaxes(suf3, 1, 2)[:, :n_gates, :]

    def split_heads(x):
        x3 = x.reshape(nc, blk, heads * HD_M)
        if heads == 1:
            return x3
        tiles = jnp.stack([x3[:, :, hh * HD_M:(hh + 1) * HD_M] for hh in range(heads)], axis=1)
        return tiles.reshape(nbatch, blk, HD_M)

    def per_head(x):
        if heads == 1:
            return x
        return jnp.broadcast_to(x[:, None], (nc, heads) + x.shape[1:]).reshape((nbatch,) + x.shape[1:])

    first, last = r == 0, r == seq - 1
    qc_all = _dwconv3(q_ref[...].astype(F32), wq_ref, first, last)
    q3 = split_heads(qc_all * jax.nn.sigmoid(qc_all))
    kc_all = _dwconv3(k_ref[...].astype(F32), wk_ref, first, last)
    k3 = split_heads(kc_all * jax.nn.sigmoid(kc_all) * (HD_M ** -0.5))
    qb, kb = q3.astype(BF16), k3.astype(BF16)
    vb = split_heads(v_ref[...])
    g3 = per_head(gates_ref[...].reshape(nc, blk, LANE))
    gt3 = per_head(gt_s[...])

    if zero_state:
        cst[...] = jnp.zeros_like(cst)
        nst[...] = jnp.zeros_like(nst)
        mst[...] = jnp.zeros_like(mst)
    else:
        cst[...] = c0_ref[...]
        nst[...] = n0_ref[...]
        mst[...] = jnp.broadcast_to(m0_ref[...], mst.shape)

    tt = lax.broadcasted_iota(jnp.int32, (1, blk, blk), 1)
    ss = lax.broadcasted_iota(jnp.int32, (1, blk, blk), 2)
    lane = lax.broadcasted_iota(jnp.int32, (1, 1, LANE), 2)
    sub = lax.broadcasted_iota(jnp.int32, (1, n_gates, 1), 1)
    head = lax.broadcasted_iota(jnp.int32, (nbatch, 1, 1), 0) % heads + pl.program_id(1) * heads
    hsum = None
    for d in range(2):
        i_idx = d * 2 * HEADS_M + head
        f_idx = i_idx + HEADS_M
        mask = (ss <= tt) if d == 0 else (ss >= tt)
        b3 = per_head((pre_s if d == 0 else suf_s)[...])
        bt3 = per_head((pret_s if d == 0 else suft_s)[...])
        bcol = jnp.sum(jnp.where(lane == f_idx, b3, 0.0), axis=2, keepdims=True)
        icol = jnp.sum(jnp.where(lane == i_idx, g3, 0.0), axis=2, keepdims=True)
        brow = jnp.sum(jnp.where(sub == f_idx, bt3, 0.0), axis=1, keepdims=True)
        irow = jnp.sum(jnp.where(sub == i_idx, gt3, 0.0), axis=1, keepdims=True)
        bend = bcol[:, blk - 1:blk, :] if d == 0 else bcol[:, 0:1, :]

        dmat = jnp.where(mask, bcol - brow + irow, -jnp.inf)
        mloc = jnp.max(dmat, axis=2, keepdims=True)
        qk = jnp.einsum('ctd,csd->cts', qb, kb, preferred_element_type=F32)
        s_loc = jnp.exp(dmat - mloc) * qk
        num_loc = jnp.einsum('cts,csd->ctd', s_loc.astype(BF16), vb, preferred_element_type=F32)
        den_loc = jnp.sum(s_loc, axis=2, keepdims=True)
        to_end = bend - bcol + icol
        mk = jnp.max(to_end, axis=1, keepdims=True)
        kw = k3 * jnp.exp(to_end - mk)
        kv_s[...] = jnp.einsum('cds,cse->cde', jnp.swapaxes(kw, 1, 2).astype(BF16), vb,
                               preferred_element_type=F32)
        ks_s[...] = jnp.sum(kw, axis=1, keepdims=True)
        be_s[...] = jnp.broadcast_to(bend, be_s.shape)
        mk_s[...] = jnp.broadcast_to(mk, mk_s.shape)

        def step(j, carry, d=d):
            c = j if d == 0 else nc - 1 - j
            sl = pl.ds(c * heads, heads)
            m_prev, c_prev, n_prev = mst[d], cst[d], nst[d]
            cp_s[sl] = c_prev.astype(BF16)
            np_s[sl] = n_prev
            mp_s[sl] = m_prev
            be, mkc = be_s[sl], mk_s[sl]
            m_new = jnp.maximum(be + m_prev, mkc)
            keep = jnp.exp(be + m_prev - m_new)
            add = jnp.exp(mkc - m_new)
            cst[d] = keep * c_prev + add * kv_s[sl]
            nst[d] = keep * n_prev + add * ks_s[sl]
            mst[d] = m_new
            return carry

        lax.fori_loop(0, nc, step, 0)

        m_inter = bcol + mp_s[...][:, :, 0:1]
        m_t = jnp.maximum(m_inter, mloc)
        w_state = jnp.exp(m_inter - m_t)
        w_loc = jnp.exp(mloc - m_t)
        inter = jnp.einsum('ctd,cde->cte', qb, cp_s[...], preferred_element_type=F32)
        num = w_state * inter + w_loc * num_loc
        den = w_state * jnp.sum(q3 * np_s[...], axis=2, keepdims=True) + w_loc * den_loc
        h = num / jnp.maximum(jnp.abs(den), jnp.exp(-m_t))
        hsum = h if hsum is None else hsum + h

    hn = hsum * lax.rsqrt(jnp.mean(hsum * hsum, axis=-1, keepdims=True) + EPS)
    og3 = split_heads(og_ref[...].astype(F32))
    for c in range(nc):
        for hh in range(heads):
            cols = slice(hh * HD_M, (hh + 1) * HD_M)
            e = c * heads + hh
            y_ref[c * blk:(c + 1) * blk, cols] = (hn[e] * gh_ref[:, cols] * jax.nn.sigmoid(og3[e])).astype(y_ref.dtype)
    if emit_state:
        cn_ref[...] = cst[...]
        nn_ref[...] = nst[...]
        mn_ref[...] = mst[...]


def mlstm(proj, gates, w_conv_qk, g_head, state, *, nb, seq, row0, heads):
    blk0 = row0 // seq
    nblk = seq // SCAN_BLOCK
    groups = HEADS_M // heads
    n_gates = 4 * HEADS_M
    zero_state = state is None
    col = lambda part: pl.BlockSpec((seq, heads * HD_M), lambda b, h: (b + blk0, part * groups + h))
    in_specs = [col(0), col(1), col(2), col(3),
                pl.BlockSpec((seq, LANE), lambda b, h: (b + blk0, 0)),
                pl.BlockSpec((3, heads * HD_M), lambda b, h: (0, h)),
                pl.BlockSpec((3, heads * HD_M), lambda b, h: (0, groups + h)),
                pl.BlockSpec((1, heads * HD_M), lambda b, h: (0, h))]
    args = [proj, proj, proj, proj, gates, w_conv_qk, w_conv_qk, g_head.reshape(1, MW)]
    state_specs = [pl.BlockSpec((None, 2, heads, HD_M, HD_M), lambda b, h: (b, 0, h, 0, 0)),
                   pl.BlockSpec((None, 2, heads, 1, HD_M), lambda b, h: (b, 0, h, 0, 0))]
    if not zero_state:
        c0, n0, m0 = state
        in_specs += state_specs + [pl.BlockSpec((None, 2, heads, 1, 1), lambda b, h: (b, 0, h, 0, 0))]
        args += [c0, n0.reshape(nb, 2, HEADS_M, 1, HD_M), m0.reshape(nb, 2, HEADS_M, 1, 1)]
    out_shape = [jax.ShapeDtypeStruct((nb * seq, MW), BF16)]
    out_specs = [pl.BlockSpec((seq, heads * HD_M), lambda b, h: (b, h))]
    if zero_state:
        out_shape += [jax.ShapeDtypeStruct((nb, 2, HEADS_M, HD_M, HD_M), F32),
                      jax.ShapeDtypeStruct((nb, 2, HEADS_M, 1, HD_M), F32),
                      jax.ShapeDtypeStruct((nb, 2, HEADS_M, 1, LANE), F32)]
        out_specs += state_specs + [pl.BlockSpec((None, 2, heads, 1, LANE), lambda b, h: (b, 0, h, 0, 0))]
    per_block = lambda *shape: pltpu.VMEM((nblk,) + shape, F32)
    per_entry = lambda *shape, dtype=F32: pltpu.VMEM((nblk * heads,) + shape, dtype)
    return pl.pallas_call(
        functools.partial(_mlstm_kernel, seq=seq, heads=heads, zero_state=zero_state, emit_state=zero_state),
        out_shape=out_shape,
        grid=(nb, groups),
        in_specs=in_specs,
        out_specs=out_specs,
        scratch_shapes=[per_block(SCAN_BLOCK, LANE), per_block(SCAN_BLOCK, LANE),
                        per_block(n_gates, SCAN_BLOCK), per_block(n_gates, SCAN_BLOCK), per_block(n_gates, SCAN_BLOCK),
                        per_entry(HD_M, HD_M), per_entry(1, HD_M), per_entry(1, LANE), per_entry(1, LANE),
                        per_entry(HD_M, HD_M, dtype=BF16), per_entry(1, HD_M), per_entry(1, LANE),
                        pltpu.VMEM((2, heads, HD_M, HD_M), F32), pltpu.VMEM((2, heads, 1, HD_M), F32),
                        pltpu.VMEM((2, heads, 1, LANE), F32)],
        compiler_params=_cparams(("arbitrary", "arbitrary")),
        name="mlstm_%d" % seq,
    )(*args)


def _hyena_pre_kernel(v_ref, x1_ref, x2_ref, wv_ref, w1_ref, w2_ref, u_ref, x2c_ref):
    first, last = _seq_edges(SEQ_BLOCK, pl.program_id(0) < N_PBLK)
    x1c = _dwconv3(x1_ref[...].astype(F32), w1_ref, first, last)
    u_ref[...] = (x1c * _dwconv3(v_ref[...].astype(F32), wv_ref, first, last)).astype(u_ref.dtype)
    x2c_ref[...] = _dwconv3(x2_ref[...].astype(F32), w2_ref, first, last).astype(x2c_ref.dtype)


def hyena_pre(proj, w_conv_hy):
    tc = 256
    nct = HW // tc
    c0 = 4 * MW // tc
    pcol = lambda part: pl.BlockSpec((SEQ_BLOCK, tc), lambda i, j: (i, c0 + part * nct + j))
    wcol = lambda part: pl.BlockSpec((3, tc), lambda i, j: (0, part * nct + j))
    out = jax.ShapeDtypeStruct((N_TOK, HW), BF16)
    ospec = pl.BlockSpec((SEQ_BLOCK, tc), lambda i, j: (i, j))
    return pl.pallas_call(
        _hyena_pre_kernel,
        out_shape=[out, out],
        grid=(N_TOK // SEQ_BLOCK, nct),
        in_specs=[pcol(0), pcol(1), pcol(2), wcol(0), wcol(1), wcol(2)],
        out_specs=[ospec, ospec],
        compiler_params=_cparams(("arbitrary", "arbitrary")),
        name="hyena_pre",
    )(proj, proj, proj, w_conv_hy, w_conv_hy, w_conv_hy)


@functools.lru_cache(maxsize=None)
def _filter_tables(seq):
    t = np.linspace(0.0, 1.0, seq)[:, None]
    wpos = 2.0 * np.pi * np.arange(seq)[:, None] / seq
    bands = np.linspace(1e-4, N_BANDS - 1, N_BANDS)[None, :]
    z = np.concatenate([t, np.cos(bands * wpos), -np.sin(bands * wpos)], axis=-1)
    z = np.pad(z, ((0, 0), (0, LANE - FILTER_EMB)))
    max_decay = math.log(DECAY_TARGET) / DECAY_FAST
    min_decay = math.log(DECAY_TARGET) / DECAY_SLOW
    deltas = np.abs(np.linspace(min_decay, max_decay, HW))
    decay = np.exp(-t * np.concatenate([deltas, deltas])[None, :])
    return z.astype(np.float32), decay.astype(np.float32)


@functools.lru_cache(maxsize=None)
def _dft_tables(seq, tk):
    n = 2 * seq
    k = np.arange(seq)[:, None]
    t = np.arange(seq)[None, :]
    ang = 2.0 * np.pi * ((k * t) % n) / n
    alt = np.where(np.arange(seq) % 2 == 0, 1.0, -1.0)
    cm, sm = np.cos(ang), np.sin(ang)
    sm[0, :] = alt
    fwd = np.stack([cm.reshape(seq // tk, tk, seq), sm.reshape(seq // tk, tk, seq)], axis=1)
    wk = np.where(np.arange(seq) == 0, 1.0, 2.0)[None, :]
    ci = (np.cos(ang.T) * wk) / n
    si = np.sin(ang.T) * 2.0 / n
    si[:, 0] = alt / n
    inv = np.concatenate([ci, si], axis=1)
    return fwd.astype(np.float32), inv.astype(np.float32)


def _filter_kernel(z_ref, w1_ref, b1_ref, w2_ref, b2_ref, w3_ref, fr_ref, dec_ref, hs_ref, hd_ref):
    fr = fr_ref[...]
    hp = functools.partial(jnp.dot, precision=HIGHEST, preferred_element_type=F32)
    h1 = jnp.sin(fr * (hp(z_ref[...], w1_ref[...]) + b1_ref[...]))
    h2 = jnp.sin(fr * (hp(h1, w2_ref[...]) + b2_ref[...]))
    w3 = w3_ref[...]
    h_hi, w_hi = h2.astype(BF16), w3.astype(BF16)
    h_lo, w_lo = (h2 - h_hi.astype(F32)).astype(BF16), (w3 - w_hi.astype(F32)).astype(BF16)
    filt = (_dot(h_hi, w_hi) + _dot(h_hi, w_lo) + _dot(h_lo, w_hi)) * dec_ref[...]
    past, fut = filt[:, :HW], filt[:, HW:]
    rows = filt.shape[0]
    grow = lax.broadcasted_iota(jnp.int32, (rows, 1), 0) + pl.program_id(0) * rows
    fut = jnp.where(grow == 0, 0.0, fut)
    hs_ref[...] = past + fut
    hd_ref[...] = past - fut


def filter_gen(seq, w1, b1, w2, b2, w3, freq):
    z, decay = _filter_tables(seq)
    tl = 256
    fh = FILTER_HIDDEN
    full = lambda shape: pl.BlockSpec(shape, lambda i: (0, 0))
    out = jax.ShapeDtypeStruct((seq, HW), F32)
    return pl.pallas_call(
        _filter_kernel,
        out_shape=[out, out],
        grid=(seq // tl,),
        in_specs=[pl.BlockSpec((tl, LANE), lambda i: (i, 0)), full((LANE, fh)), full((1, fh)), full((fh, fh)),
                  full((1, fh)), full((fh, 2 * HW)), full((1, fh)), pl.BlockSpec((tl, 2 * HW), lambda i: (i, 0))],
        out_specs=[pl.BlockSpec((tl, HW), lambda i: (i, 0))] * 2,
        compiler_params=_cparams(("arbitrary",)),
        name="filter_gen",
    )(jnp.asarray(z), jnp.pad(w1, ((0, LANE - FILTER_EMB), (0, 0))), b1.reshape(1, fh), w2, b2.reshape(1, fh), w3,
      freq.reshape(1, fh), jnp.asarray(decay))


def _dft_filter_kernel(a_ref, hs_ref, hd_ref, k_ref, hs_bf, hd_bf):
    @pl.when(pl.program_id(0) == 0)
    def _():
        hs_bf[...] = hs_ref[...].astype(BF16)
        hd_bf[...] = hd_ref[...].astype(BF16)

    k_ref[0] = _dot(a_ref[0].astype(BF16), hs_bf[...])
    k_ref[1] = _dot(a_ref[1].astype(BF16), hd_bf[...])

    @pl.when(pl.program_id(0) == 0)
    def _():
        k_ref[1, 0:1, :] = _dot(a_ref[1, 0:8, :].astype(BF16), hs_bf[...])[0:1, :]


def dft_filter(seq, tk, hs, hd):
    fwd, _ = _dft_tables(seq, tk)
    return pl.pallas_call(
        _dft_filter_kernel,
        out_shape=jax.ShapeDtypeStruct((2, seq, HW), F32),
        grid=(seq // tk,),
        in_specs=[pl.BlockSpec((None, 2, tk, seq), lambda m: (m, 0, 0, 0)),
                  pl.BlockSpec((seq, HW), lambda m: (0, 0)), pl.BlockSpec((seq, HW), lambda m: (0, 0))],
        out_specs=pl.BlockSpec((2, tk, HW), lambda m: (0, m, 0)),
        scratch_shapes=[pltpu.VMEM((seq, HW), BF16), pltpu.VMEM((seq, HW), BF16)],
        compiler_params=_cparams(("arbitrary",)),
        name="dft_filter",
    )(jnp.asarray(fwd), hs, hd)


def _dft_fwd_kernel(a_ref, u_ref, k_ref, y_ref, u_bf, tbl):
    m = pl.program_id(1)

    @pl.when(m == 0)
    def _():
        u_bf[...] = u_ref[...].astype(BF16)

    @pl.when(pl.program_id(0) == 0)
    def _():
        tbl[m] = a_ref[...].astype(BF16)

    ure = _dot(tbl[m, 0], u_bf[...])
    uim = _dot(tbl[m, 1], u_bf[...])
    kre, kim = k_ref[0], k_ref[1]
    packed = (lax.broadcasted_iota(jnp.int32, (ure.shape[0], 1), 0) == 0) & (m == 0)
    y_ref[0] = jnp.where(packed, ure * kre, ure * kre - uim * kim).astype(y_ref.dtype)
    y_ref[1] = jnp.where(packed, uim * kim, ure * kim + uim * kre).astype(y_ref.dtype)


def dft_fwd(seq, tk, u, kf, *, nb, row0):
    fwd, _ = _dft_tables(seq, tk)
    blk0 = row0 // seq
    nm = seq // tk
    return pl.pallas_call(
        _dft_fwd_kernel,
        out_shape=jax.ShapeDtypeStruct((nb, 2, seq, HW), BF16),
        grid=(nb, nm),
        in_specs=[pl.BlockSpec((None, 2, tk, seq), lambda b, m: (jnp.where(b == 0, m, nm - 1), 0, 0, 0)),
                  pl.BlockSpec((seq, HW), lambda b, m: (b + blk0, 0)),
                  pl.BlockSpec((2, tk, HW), lambda b, m: (0, m, 0))],
        out_specs=pl.BlockSpec((None, 2, tk, HW), lambda b, m: (b, 0, m, 0)),
        scratch_shapes=[pltpu.VMEM((seq, HW), BF16), pltpu.VMEM((nm, 2, tk, seq), BF16)],
        compiler_params=_cparams(("arbitrary", "arbitrary"), DFT_VMEM_LIMIT),
        name="dft_fwd",
    )(jnp.asarray(fwd), u, kf)


def _dft_inv_kernel(a_ref, y_ref, u_ref, x2_ref, bias_ref, o_ref, tbl, *, seq):
    t = pl.program_id(1)

    @pl.when(pl.program_id(0) == 0)
    def _():
        tbl[t] = a_ref[...].astype(BF16)

    conv = _dot(tbl[t, :, :seq], y_ref[0]) + _dot(tbl[t, :, seq:], y_ref[1])
    u = u_ref[...].astype(F32)
    o_ref[...] = (x2_ref[...].astype(F32) * (conv + bias_ref[...] * u)).astype(o_ref.dtype)


def dft_inv(seq, tk, y, u, x2c, bias, *, nb, row0):
    _, inv = _dft_tables(seq, tk)
    tm = min(seq, 512)
    nt = seq // tm
    blk0 = row0 // tm
    rows = lambda b, t: (b * nt + t + blk0, 0)
    return pl.pallas_call(
        functools.partial(_dft_inv_kernel, seq=seq),
        out_shape=jax.ShapeDtypeStruct((nb * seq, HW), BF16),
        grid=(nb, nt),
        in_specs=[pl.BlockSpec((tm, 2 * seq), lambda b, t: (jnp.where(b == 0, t, nt - 1), 0)),
                  pl.BlockSpec((None, 2, seq, HW), lambda b, t: (b, 0, 0, 0)),
                  pl.BlockSpec((tm, HW), rows), pl.BlockSpec((tm, HW), rows),
                  pl.BlockSpec((1, HW), lambda b, t: (0, 0))],
        out_specs=pl.BlockSpec((tm, HW), lambda b, t: (b * nt + t, 0)),
        scratch_shapes=[pltpu.VMEM((nt, tm, 2 * seq), BF16)],
        compiler_params=_cparams(("arbitrary", "arbitrary"), DFT_VMEM_LIMIT),
        name="dft_inv",
    )(jnp.asarray(inv), y, u, x2c, bias.reshape(1, HW))


def hyena_group(seq, u, x2c, filt_w, bias, *, nb, row0):
    tk = min(seq, 512)
    hs, hd = filter_gen(seq, *filt_w)
    kf = dft_filter(seq, tk, hs, hd)
    y = dft_fwd(seq, tk, u, kf, nb=nb, row0=row0)
    return dft_inv(seq, tk, y, u, x2c, bias, nb=nb, row0=row0)


def mixer_ab_parts(proj, gates, state_s, w_conv_qk, g_head, w_conv_hy, w_f1, b_f1, w_f2, b_f2, w_f3, freq, hy_bias):
    ym_p, c_p, n_p, m_p = mlstm(proj, gates, w_conv_qk, g_head, None, nb=BATCH, seq=SEQ, row0=0, heads=HEADS_M)
    (ym_s,) = mlstm(proj, gates, w_conv_qk, g_head, state_s, nb=DEC_BATCH, seq=DEC_SEQ, row0=N_P, heads=1)
    u, x2c = hyena_pre(proj, w_conv_hy)
    filt_w = (w_f1, b_f1, w_f2, b_f2, w_f3, freq)
    yh_p = hyena_group(SEQ, u, x2c, filt_w, hy_bias, nb=BATCH, row0=0)
    yh_s = hyena_group(DEC_SEQ, u, x2c, filt_w, hy_bias, nb=DEC_BATCH, row0=N_P)
    state_p = (c_p, n_p[:, :, :, 0, :], m_p[:, :, :, 0, 0])
    return (ym_p, ym_s), (yh_p, yh_s), state_p


def kernel(x_prompt, x_sample, state_mlstm_C, state_mlstm_n, state_mlstm_m, cache_na_k, cache_na_v, c, c_ctx, w_ada, b_ada, g_mix, g_ffn, g_final, w_in_ab, b_gates, w_conv_qk, g_mlstm, w_conv_hy, w_filt1, b_filt1, w_filt2, b_filt2, w_filt3, filt_freq, hyena_bias, w_out_ab, w_in_c, rpb_c, w_out_c, w_up, w_conv_ffn, w_down):
    cmat = jnp.concatenate([c, c_ctx[None, :], jnp.zeros((MOD_ROWS - DEC_BATCH - 1, D), F32)], axis=0)
    mod_all = adaln_all(cmat, w_ada, b_ada).reshape(DEPTH, MOD_ROWS, 6, 1, D)
    x = (x_prompt.reshape(N_P, D), x_sample.reshape(N_S, D))
    h = None
    new_c, new_n, new_m, new_k, new_v = [], [], [], [], []
    for l in range(DEPTH):
        e = l // 2
        if l % 2 == 0:
            assert l == 0, "a later mixer A/B layer would take the fused norm of the layer before it"
            proj, gates = ab_in(x, g_mix, mod_all, l, jnp.swapaxes(w_in_ab, 1, 2), b_gates[e], e)
            state_s = (state_mlstm_C[:, e], state_mlstm_n[:, e], state_mlstm_m[:, e])
            y_m, y_h, (c_p, n_p, m_p) = mixer_ab_parts(
                proj, gates, state_s, w_conv_qk[e], g_mlstm[e], w_conv_hy[e], w_filt1[e], b_filt1[e],
                w_filt2[e], b_filt2[e], w_filt3[e], filt_freq[e], hyena_bias[e])
            a_list = [y_m, y_h]
            w_list = [(w_out_ab, (None, MW, D), lambda i, e=e: (e, 0, 0)),
                      (w_out_ab, (None, HW, D), lambda i, e=e: (e, 1, 0))]
            new_c.append(c_p[:, None])
            new_n.append(n_p[:, None])
            new_m.append(m_p[:, None])
        else:
            o, k_new, v_new = mixer_c(h, cache_na_k[:, e], cache_na_v[:, e], w_in_c, e, rpb_c[e])
            a_list = [o]
            w_list = [(w_out_c, (None, D, D), lambda i, e=e: (e, 0, 0))]
            new_k.append(k_new)
            new_v.append(v_new)
        x, h = out_proj(a_list, w_list, x, mod_all, l, 2, g_ffn, l, (l, 3, 4), name="mixer_out")
        mid = ffn_up(h, w_up, w_conv_ffn, l)
        w_list = [(w_down, (None, FF, D), lambda i, l=l: (l, 0, 0))]
        if l + 1 < DEPTH:
            x, h = out_proj([mid], w_list, x, mod_all, l, 5, g_mix, l + 1, (l + 1, 0, 1), name="ffn_down")
        else:
            y_p, y_s = out_proj([mid], w_list, x, mod_all, l, 5, g_final, 0, None, name="ffn_down_final")
    cat = lambda parts: parts[0] if len(parts) == 1 else jnp.concatenate(parts, axis=1)
    return (y_p.reshape(BATCH, SEQ, D), y_s.reshape(DEC_BATCH, DEC_SEQ, D), cat(new_c), cat(new_n), cat(new_m),
            cat(new_k), cat(new_v))
```

```python
import functools
import math

import numpy as np
import jax
import jax.numpy as jnp
from jax import lax
from jax.experimental import pallas as pl
from jax.experimental.pallas import tpu as pltpu

F32 = jnp.float32
BF16 = jnp.bfloat16

D = 1024
BATCH, SEQ = 16, 256
DEC_BATCH, DEC_SEQ = 4, 2048
PAST_LEN = 512
DEPTH = 2
GRID_W = 64
GRID_R = DEC_SEQ // GRID_W
HEADS_M = 4
MW = D // 2
HD_M = MW // HEADS_M
CHUNK = 64
HW = D // 2
N_BANDS = 16
FILTER_EMB = 2 * N_BANDS + 1
FILTER_HIDDEN = 64
DECAY_FAST, DECAY_SLOW, DECAY_TARGET = 0.3, 1.5, 1e-2
NA_HEADS = 16
NA_HD = D // NA_HEADS
NA_KH, NA_KW = 8, 16
FF = 2816
EPS = 1e-6

N_P = BATCH * SEQ
N_S = DEC_BATCH * DEC_SEQ
N_TOK = N_P + N_S
CTX_ROW = DEC_BATCH
MOD_ROWS = 8
LANE = 128
VMEM_LIMIT = 48 * 1024 * 1024
QKV_VMEM_LIMIT = 56 * 1024 * 1024
DFT_VMEM_LIMIT = 56 * 1024 * 1024
HIGHEST = lax.Precision.HIGHEST


def _cparams(sem, vmem_limit=VMEM_LIMIT):
    return pltpu.CompilerParams(dimension_semantics=sem, vmem_limit_bytes=vmem_limit)


def _mod_row(i, bm):
    return jnp.where(i < N_P // bm, CTX_ROW, (i - N_P // bm) // (DEC_SEQ // bm))


def _dot(a, b):
    return jnp.dot(a, b, preferred_element_type=F32)


def _adaln_kernel(c_ref, w_ref, b_ref, o_ref):
    cv = c_ref[...]
    s = cv * jax.nn.sigmoid(cv)
    o_ref[...] = _dot(s.astype(BF16), w_ref[...].astype(BF16)) + b_ref[...]


def adaln_all(cmat, w_ada, b_ada):
    tn = 1024
    return pl.pallas_call(
        _adaln_kernel,
        out_shape=jax.ShapeDtypeStruct((DEPTH, MOD_ROWS, 6 * D), F32),
        grid=(DEPTH, 6 * D // tn),
        in_specs=[
            pl.BlockSpec((MOD_ROWS, D), lambda l, j: (0, 0)),
            pl.BlockSpec((None, D, tn), lambda l, j: (l, 0, j)),
            pl.BlockSpec((None, 1, tn), lambda l, j: (l, 0, j)),
        ],
        out_specs=pl.BlockSpec((None, MOD_ROWS, tn), lambda l, j: (l, 0, j)),
        compiler_params=_cparams(("arbitrary", "arbitrary")),
        name="adaln",
    )(cmat, w_ada, b_ada.reshape(DEPTH, 1, 6 * D))


ROW_BM = 512
ROW_NPB = N_P // ROW_BM


def _row_specs(arr, cols):
    if isinstance(arr, tuple):
        return ([pl.BlockSpec((ROW_BM, cols), lambda i: (jnp.minimum(i, ROW_NPB - 1), 0)),
                 pl.BlockSpec((ROW_BM, cols), lambda i: (jnp.maximum(i - ROW_NPB, 0), 0))], list(arr))
    return [pl.BlockSpec((ROW_BM, cols), lambda i: (i, 0))], [arr]


def _pick_rows(refs):
    if len(refs) == 1:
        return refs[0][...]
    return jnp.where(pl.program_id(0) < ROW_NPB, refs[0][...], refs[1][...])


def _mod_spec(layer, which):
    return pl.BlockSpec((None, None, None, 1, D), lambda i: (layer, _mod_row(i, ROW_BM), which, 0, 0))


def _layer_vec_spec(layer):
    return pl.BlockSpec((None, 1, D), lambda i: (layer, 0, 0))


def _rms_mod(x, g_ref, sh_ref, sc_ref):
    y = x * lax.rsqrt(jnp.mean(x * x, axis=-1, keepdims=True) + EPS)
    return (y * g_ref[...]) * (1.0 + sc_ref[...]) + sh_ref[...]


def _out_proj_kernel(*refs, a_counts, n_x, final):
    pos = 0
    a_groups = []
    for cnt in a_counts:
        a_groups.append(refs[pos:pos + cnt])
        pos += cnt
    w_refs = refs[pos:pos + len(a_counts)]
    pos += len(a_counts)
    x_refs = refs[pos:pos + n_x]
    pos += n_x
    gt_ref, g_ref = refs[pos:pos + 2]
    pos += 2
    if not final:
        sh_ref, sc_ref = refs[pos:pos + 2]
        pos += 2
    out_a, out_b = refs[pos:pos + 2]
    wbf = refs[pos + 2:]

    @pl.when(pl.program_id(0) == 0)
    def _():
        for w_ref, wb in zip(w_refs, wbf):
            wb[...] = w_ref[...].astype(BF16)

    acc = None
    for group, wb in zip(a_groups, wbf):
        part = _dot(_pick_rows(group), wb[...])
        acc = part if acc is None else acc + part
    xn = _pick_rows(x_refs) + gt_ref[...] * acc
    if final:
        y = xn * lax.rsqrt(jnp.mean(xn * xn, axis=-1, keepdims=True) + EPS) * g_ref[...]

        @pl.when(pl.program_id(0) < ROW_NPB)
        def _():
            out_a[...] = y

        @pl.when(pl.program_id(0) >= ROW_NPB)
        def _():
            out_b[...] = y
    else:
        out_a[...] = xn
        out_b[...] = _rms_mod(xn, g_ref, sh_ref, sc_ref).astype(out_b.dtype)


def out_proj(a_list, w_list, x, mod_all, layer, gt_idx, g_all, g_layer, norm_mod_idx=None, name="out_proj"):
    final = norm_mod_idx is None
    in_specs, args, a_counts = [], [], []
    for a in a_list:
        cols = (a[0] if isinstance(a, tuple) else a).shape[1]
        specs, ops = _row_specs(a, cols)
        in_specs += specs
        args += ops
        a_counts.append(len(ops))
    w_shapes = []
    for w, block, imap in w_list:
        in_specs.append(pl.BlockSpec(block, imap, pipeline_mode=pl.Buffered(1)))
        args.append(w)
        w_shapes.append(tuple(b for b in block if b is not None))
    x_specs, x_args = _row_specs(x, D)
    in_specs += x_specs + [_mod_spec(layer, gt_idx), _layer_vec_spec(g_layer)]
    args += x_args + [mod_all, g_all.reshape(-1, 1, D)]
    if final:
        out_shape = [jax.ShapeDtypeStruct((N_P, D), F32), jax.ShapeDtypeStruct((N_S, D), F32)]
        out_specs = [pl.BlockSpec((ROW_BM, D), lambda i: (jnp.minimum(i, ROW_NPB - 1), 0)),
                     pl.BlockSpec((ROW_BM, D), lambda i: (jnp.maximum(i - ROW_NPB, 0), 0))]
    else:
        n_layer, sh_idx, sc_idx = norm_mod_idx
        in_specs += [_mod_spec(n_layer, sh_idx), _mod_spec(n_layer, sc_idx)]
        args += [mod_all, mod_all]
        out_shape = [jax.ShapeDtypeStruct((N_TOK, D), F32), jax.ShapeDtypeStruct((N_TOK, D), BF16)]
        out_specs = [pl.BlockSpec((ROW_BM, D), lambda i: (i, 0))] * 2
    return pl.pallas_call(
        functools.partial(_out_proj_kernel, a_counts=tuple(a_counts), n_x=len(x_args), final=final),
        out_shape=out_shape,
        grid=(N_TOK // ROW_BM,),
        in_specs=in_specs,
        out_specs=out_specs,
        scratch_shapes=[pltpu.VMEM(s, BF16) for s in w_shapes],
        compiler_params=_cparams(("arbitrary",)),
        name=name,
    )(*args)


def _qkv_kernel(a_ref, w_ref, o_ref, kc_ref, vc_ref, wbf):
    @pl.when(pl.program_id(0) == 0)
    def _():
        wbf[...] = w_ref[...].astype(BF16)

    a = a_ref[...]
    for part in range(3):
        acc = _dot(a, wbf[:, part * D:(part + 1) * D])
        for pp in range(NA_HEADS // 2):
            o_ref[part, pp] = acc[:, pp * 2 * NA_HD:(pp + 1) * 2 * NA_HD].astype(o_ref.dtype)
        if part > 0:
            c_ref = kc_ref if part == 1 else vc_ref

            @pl.when(pl.program_id(0) < ROW_NPB)
            def _(acc=acc, c_ref=c_ref):
                for b in range(ROW_BM // SEQ):
                    for hh in range(NA_HEADS):
                        c_ref[b, hh] = acc[b * SEQ:(b + 1) * SEQ, hh * NA_HD:(hh + 1) * NA_HD]


def qkv_proj(h, w_in, layer):
    seqs = ROW_BM // SEQ
    cache = jax.ShapeDtypeStruct((BATCH, 1, NA_HEADS, SEQ, NA_HD), F32)
    cache_spec = pl.BlockSpec((seqs, None, NA_HEADS, SEQ, NA_HD),
                              lambda i: (jnp.minimum(i, ROW_NPB - 1), 0, 0, 0, 0))
    return pl.pallas_call(
        _qkv_kernel,
        out_shape=[jax.ShapeDtypeStruct((3, NA_HEADS // 2, N_TOK, 2 * NA_HD), BF16), cache, cache],
        grid=(N_TOK // ROW_BM,),
        in_specs=[pl.BlockSpec((ROW_BM, D), lambda i: (i, 0)),
                  pl.BlockSpec((None, D, 3 * D), lambda i: (layer, 0, 0), pipeline_mode=pl.Buffered(1))],
        out_specs=[pl.BlockSpec((3, NA_HEADS // 2, ROW_BM, 2 * NA_HD), lambda i: (0, 0, i, 0)),
                   cache_spec, cache_spec],
        scratch_shapes=[pltpu.VMEM((D, 3 * D), BF16)],
        compiler_params=_cparams(("arbitrary",), QKV_VMEM_LIMIT),
        name="qkv_proj",
    )(h, w_in)


AB_MAIN = 4 * MW + 3 * HW
N_GATES = 4 * HEADS_M


def _ab_in_kernel(*refs):
    x_refs = refs[:-8]
    g_ref, sh_ref, sc_ref, w_ref, b_ref, proj_ref, gates_ref, wbf = refs[-8:]

    @pl.when(pl.program_id(0) == 0)
    def _():
        wbf[:4 * MW, :] = w_ref[:4 * MW, :].astype(BF16)
        wbf[4 * MW:AB_MAIN, :] = w_ref[4 * MW + N_GATES:, :].astype(BF16)
        wbf[AB_MAIN:AB_MAIN + N_GATES, :] = w_ref[4 * MW:4 * MW + N_GATES, :].astype(BF16)
        wbf[AB_MAIN + N_GATES:, :] = jnp.zeros((LANE - N_GATES, D), BF16)

    h = _rms_mod(_pick_rows(x_refs), g_ref, sh_ref, sc_ref).astype(BF16)
    proj_ref[...] = lax.dot_general(h, wbf[:AB_MAIN, :], NT_DIMS, preferred_element_type=F32).astype(proj_ref.dtype)
    gates_ref[...] = lax.dot_general(h, wbf[AB_MAIN:, :], NT_DIMS, preferred_element_type=F32) + b_ref[...]


def ab_in(x, g_all, mod_all, layer, w_in_t, b_gates, e):
    x_specs, x_args = _row_specs(x, D)
    b_pad = jnp.pad(b_gates, (0, LANE - N_GATES)).reshape(1, LANE)
    return pl.pallas_call(
        _ab_in_kernel,
        out_shape=[jax.ShapeDtypeStruct((N_TOK, AB_MAIN), BF16), jax.ShapeDtypeStruct((N_TOK, LANE), F32)],
        grid=(N_TOK // ROW_BM,),
        in_specs=x_specs + [_layer_vec_spec(layer), _mod_spec(layer, 0), _mod_spec(layer, 1),
                            pl.BlockSpec((None, w_in_t.shape[1], D), lambda i: (e, 0, 0),
                                         pipeline_mode=pl.Buffered(1)),
                            pl.BlockSpec((1, LANE), lambda i: (0, 0))],
        out_specs=[pl.BlockSpec((ROW_BM, AB_MAIN), lambda i: (i, 0)), pl.BlockSpec((ROW_BM, LANE), lambda i: (i, 0))],
        scratch_shapes=[pltpu.VMEM((AB_MAIN + LANE, D), BF16)],
        compiler_params=_cparams(("arbitrary",)),
        name="ab_in",
    )(*x_args, g_all.reshape(-1, 1, D), mod_all, mod_all, w_in_t, b_pad)


SEQ_BLOCK = DEC_SEQ
N_PBLK = N_P // SEQ_BLOCK


def _seq_edges(rows, is_prompt):
    r = lax.broadcasted_iota(jnp.int32, (rows, 1), 0)
    first = (r == 0) | (is_prompt & (r % SEQ == 0))
    last = (r == rows - 1) | (is_prompt & (r % SEQ == SEQ - 1))
    return first, last


def _dwconv3(x, w_ref, first, last):
    rows = x.shape[0]
    prev = jnp.where(first, 0.0, pltpu.roll(x, 1, 0))
    nxt = jnp.where(last, 0.0, pltpu.roll(x, rows - 1, 0))
    return prev * w_ref[0:1, :] + x * w_ref[1:2, :] + nxt * w_ref[2:3, :]


def _gated_gelu_of_half(y, g):
    c = math.sqrt(2.0 / math.pi)
    t = jnp.tanh(y * (2.0 * c + (8.0 * 0.044715 * c) * (y * y)))
    return (y + y * t) * g


FFN_BLOCK = 2 * DEC_SEQ
FFN_CHUNKS = 8
FFN_HALO = 8


def _ffn_up_kernel(h_ref, wa_ref, wg_ref, wc_ref, o_ref):
    is_prompt = pl.program_id(0) < N_P // FFN_BLOCK
    rows = FFN_BLOCK // FFN_CHUNKS
    wa = wa_ref[...].astype(BF16)
    wg = wg_ref[...].astype(BF16)
    wc_half = 0.5 * wc_ref[...]
    zeros = jnp.zeros((FFN_HALO, o_ref.shape[1]), F32)

    def matmuls(r):
        hr = h_ref[r * rows:(r + 1) * rows, :]
        return _dot(hr, wa), _dot(hr, wg)

    def activation(r, a_prev, a_cur, a_next, g):
        win = jnp.concatenate([zeros if a_prev is None else a_prev[rows - FFN_HALO:], a_cur,
                               zeros if a_next is None else a_next[:FFN_HALO]], axis=0)
        ridx = lax.broadcasted_iota(jnp.int32, (rows + 2 * FFN_HALO, 1), 0) + (r * rows - FFN_HALO)
        first = (ridx % DEC_SEQ == 0) | (is_prompt & (ridx % SEQ == 0))
        last = (ridx % DEC_SEQ == DEC_SEQ - 1) | (is_prompt & (ridx % SEQ == SEQ - 1))
        half_conv = _dwconv3(win, wc_half, first, last)[FFN_HALO:FFN_HALO + rows]
        o_ref[r * rows:(r + 1) * rows, :] = _gated_gelu_of_half(half_conv, g).astype(o_ref.dtype)

    acts = [matmuls(0)]
    for r in range(1, FFN_CHUNKS):
        acts.append(matmuls(r))
        activation(r - 1, acts[r - 2][0] if r >= 2 else None, acts[r - 1][0], acts[r][0], acts[r - 1][1])
    activation(FFN_CHUNKS - 1, acts[-2][0], acts[-1][0], None, acts[-1][1])


def ffn_up(h, w_up, w_conv, layer):
    tc = 256
    nct = FF // tc
    return pl.pallas_call(
        _ffn_up_kernel,
        out_shape=jax.ShapeDtypeStruct((N_TOK, FF), BF16),
        grid=(N_TOK // FFN_BLOCK, nct),
        in_specs=[
            pl.BlockSpec((FFN_BLOCK, D), lambda i, j: (i, 0)),
            pl.BlockSpec((None, D, tc), lambda i, j: (layer, 0, j)),
            pl.BlockSpec((None, D, tc), lambda i, j: (layer, 0, j + nct)),
            pl.BlockSpec((None, 3, tc), lambda i, j: (layer, 0, j)),
        ],
        out_specs=pl.BlockSpec((FFN_BLOCK, tc), lambda i, j: (i, j)),
        compiler_params=_cparams(("arbitrary", "arbitrary")),
        name="ffn_up",
    )(h, w_up, w_up, w_conv)


HEAD_PAIRS = NA_HEADS // 2
NT_DIMS = (((1,), (1,)), ((), ()))


def _pair_mask(shape):
    return lax.broadcasted_iota(jnp.int32, shape, len(shape) - 1) < NA_HD


def _one_head(x2, first):
    keep = _pair_mask(x2.shape) if first else ~_pair_mask(x2.shape)
    return jnp.where(keep, x2, jnp.zeros_like(x2))


def _ctx_attn_kernel(q_ref, k_ref, v_ref, o_ref, *, pairs):
    outs = []
    for pp in range(pairs):
        q2, k2, v2 = q_ref[pp], k_ref[pp], v_ref[pp]
        res = []
        for first in (True, False):
            s = lax.dot_general(_one_head(q2, first), k2, NT_DIMS, preferred_element_type=F32) * (NA_HD ** -0.5)
            m = jnp.max(s, axis=-1, keepdims=True)
            p = jnp.exp(s - m)
            l = jnp.sum(p, axis=-1, keepdims=True)
            res.append(_dot(p.astype(BF16), v2) / l)
        outs.append(jnp.where(_pair_mask(res[0].shape), res[0], res[1]))
    o_ref[...] = jnp.concatenate(outs, axis=-1).astype(o_ref.dtype)


def ctx_attention(qkv):
    pairs = HEAD_PAIRS
    spec = lambda part: pl.BlockSpec((None, pairs, SEQ, 2 * NA_HD), lambda b, h: (part, h, b, 0))
    return pl.pallas_call(
        functools.partial(_ctx_attn_kernel, pairs=pairs),
        out_shape=jax.ShapeDtypeStruct((N_P, D), BF16),
        grid=(BATCH, HEAD_PAIRS // pairs),
        in_specs=[spec(0), spec(1), spec(2)],
        out_specs=pl.BlockSpec((SEQ, pairs * 2 * NA_HD), lambda b, h: (b, h)),
        compiler_params=_cparams(("arbitrary", "arbitrary")),
        name="ctx_attn",
    )(qkv, qkv, qkv)


def _na_tables():
    q = np.arange(GRID_W)[:, None]
    w = np.arange(GRID_W)[None, :]
    idx_c = np.clip(w - q + (NA_KW - 1), 0, 2 * NA_KW - 2)
    onehot = (idx_c.reshape(1, -1) == np.arange(32)[:, None]).astype(np.float32)
    c_start = np.clip(np.arange(GRID_W) - NA_KW // 2, 0, GRID_W - NA_KW)[:, None]
    inside = (w >= c_start) & (w < c_start + NA_KW)
    cmask = np.where(inside, 0.0, -np.inf).astype(np.float32)
    return onehot, np.tile(cmask, (1, 2))


def _rpb_expand_kernel(r_ref, e_ref, o_ref):
    o_ref[...] = jnp.dot(r_ref[...], e_ref[...], precision=HIGHEST, preferred_element_type=F32)


def rpb_expand(rpb):
    onehot, _ = _na_tables()
    rp = jnp.pad(rpb, ((0, 0), (0, 1), (0, 1)))
    return pl.pallas_call(
        _rpb_expand_kernel,
        out_shape=jax.ShapeDtypeStruct((NA_HEADS, 16, GRID_W * GRID_W), F32),
        grid=(NA_HEADS,),
        in_specs=[pl.BlockSpec((None, 16, 32), lambda h: (h, 0, 0)),
                  pl.BlockSpec((32, GRID_W * GRID_W), lambda h: (0, 0))],
        out_specs=pl.BlockSpec((None, 16, GRID_W * GRID_W), lambda h: (h, 0, 0)),
        compiler_params=_cparams(("arbitrary",)),
        name="rpb_expand",
    )(rp, jnp.asarray(onehot))


NA_QROWS = 8
NA_WIN = 2 * NA_QROWS


def _na_attn_kernel(q_ref, k_ref, v_ref, kc_ref, vc_ref, t_ref, o_ref, p_loc, p_ctx):
    nq = NA_QROWS * GRID_W
    nk = NA_WIN * GRID_W
    pair = 2 * GRID_W
    nt = NT_DIMS
    lane = lax.broadcasted_iota(jnp.int32, (GRID_W, pair), 1)
    zero_tile = jnp.zeros((GRID_W, pair), BF16)
    kc = kc_ref[...].reshape(2 * NA_HD, PAST_LEN).astype(BF16)
    vc = vc_ref[...].reshape(2 * NA_HD, PAST_LEN).astype(BF16)
    for blk in range(GRID_R // NA_QROWS):
        k0 = min(max(NA_QROWS * blk - NA_KH // 2, 0), GRID_R - NA_WIN)
        q2 = q_ref[blk * nq:(blk + 1) * nq, :] * (NA_HD ** -0.5)
        kw = k_ref[k0 * GRID_W:k0 * GRID_W + nk, :]
        vw = v_ref[k0 * GRID_W:k0 * GRID_W + nk, :]
        outs = []
        for hh in range(2):
            q = _one_head(q2, hh == 0)
            s_loc = lax.dot_general(q, kw, nt, preferred_element_type=F32)
            s_ctx = _dot(q, kc)
            denoms = []
            for qi in range(NA_QROWS):
                r = NA_QROWS * blk + qi
                r_start = min(max(r - NA_KH // 2, 0), GRID_R - NA_KH)
                rows = slice(qi * GRID_W, (qi + 1) * GRID_W)
                ctx_tiles = [s_ctx[rows, c * pair:(c + 1) * pair] for c in range(PAST_LEN // pair)]
                tiles = {}
                for j in range(nk // pair):
                    kr = k0 + 2 * j
                    ok0 = r_start <= kr < r_start + NA_KH
                    ok1 = r_start <= kr + 1 < r_start + NA_KH
                    if not (ok0 or ok1):
                        continue
                    sb = s_loc[rows, j * pair:(j + 1) * pair] + t_ref[hh, kr - r + NA_KH]
                    if not (ok0 and ok1):
                        sb = jnp.where((lane < GRID_W) if ok0 else (lane >= GRID_W), sb, -jnp.inf)
                    tiles[j] = sb
                mx = functools.reduce(jnp.maximum, list(tiles.values()) + ctx_tiles)
                m = jnp.max(mx, axis=1, keepdims=True)
                acc = None
                for j in range(nk // pair):
                    if j in tiles:
                        p = jnp.exp(tiles[j] - m)
                        acc = p if acc is None else acc + p
                        p_loc[rows, j * pair:(j + 1) * pair] = p.astype(BF16)
                    else:
                        p_loc[rows, j * pair:(j + 1) * pair] = zero_tile
                for c, t in enumerate(ctx_tiles):
                    p = jnp.exp(t - m)
                    acc = acc + p
                    p_ctx[rows, c * pair:(c + 1) * pair] = p.astype(BF16)
                denoms.append(jnp.sum(acc, axis=1, keepdims=True))
            pv = _dot(p_loc[...], vw) + lax.dot_general(p_ctx[...], vc, nt, preferred_element_type=F32)
            outs.append(pv / jnp.concatenate(denoms, axis=0))
        o_ref[blk * nq:(blk + 1) * nq, :] = jnp.where(_pair_mask(outs[0].shape), outs[0], outs[1]).astype(o_ref.dtype)


def na_attention(qkv, k_ctx, v_ctx, bias_pairs):
    blk0 = N_P // DEC_SEQ
    spec = lambda part: pl.BlockSpec((None, None, DEC_SEQ, 2 * NA_HD), lambda h, b: (part, h, b + blk0, 0))
    cspec = pl.BlockSpec((None, 2, NA_HD, PAST_LEN), lambda h, b: (b, h, 0, 0))
    return pl.pallas_call(
        _na_attn_kernel,
        out_shape=jax.ShapeDtypeStruct((N_S, D), BF16),
        grid=(HEAD_PAIRS, DEC_BATCH),
        in_specs=[spec(0), spec(1), spec(2), cspec, cspec,
                  pl.BlockSpec((2, 16, GRID_W, 2 * GRID_W), lambda h, b: (h, 0, 0, 0))],
        out_specs=pl.BlockSpec((DEC_SEQ, 2 * NA_HD), lambda h, b: (b, h)),
        scratch_shapes=[pltpu.VMEM((NA_QROWS * GRID_W, NA_WIN * GRID_W), BF16),
                        pltpu.VMEM((NA_QROWS * GRID_W, PAST_LEN), BF16)],
        compiler_params=_cparams(("arbitrary", "arbitrary")),
        name="na_attn",
    )(qkv, qkv, qkv, k_ctx, v_ctx, bias_pairs)


def mixer_c(h, k_ctx, v_ctx, w_in_all, layer, rpb):
    qkv, k_new, v_new = qkv_proj(h, w_in_all, layer)
    o_p = ctx_attention(qkv)
    _, cmask2 = _na_tables()
    b15 = rpb_expand(rpb).reshape(NA_HEADS, 16, GRID_W, GRID_W)
    b17 = jnp.pad(b15, ((0, 0), (1, 0), (0, 0), (0, 0)))
    bias_pairs = jnp.concatenate([b17[:, :16], b17[:, 1:]], axis=-1) + jnp.asarray(cmask2)
    o_s = na_attention(qkv, jnp.swapaxes(k_ctx, -1, -2), jnp.swapaxes(v_ctx, -1, -2), bias_pairs)
    return (o_p, o_s), k_new, v_new


SCAN_BLOCK = HD_M


def _mlstm_kernel(*refs, seq, heads, zero_state, emit_state):
    q_ref, k_ref, v_ref, og_ref, gates_ref, wq_ref, wk_ref, gh_ref = refs[:8]
    pos = 8
    if not zero_state:
        c0_ref, n0_ref, m0_ref = refs[pos:pos + 3]
        pos += 3
    y_ref = refs[pos]
    pos += 1
    if emit_state:
        cn_ref, nn_ref, mn_ref = refs[pos:pos + 3]
        pos += 3
    pre_s, suf_s, gt_s, pret_s, suft_s, kv_s, ks_s, be_s, mk_s, cp_s, np_s, mp_s, cst, nst, mst = refs[pos:]

    blk = SCAN_BLOCK
    nc = seq // blk
    nbatch = nc * heads
    n_gates = 4 * HEADS_M
    r = lax.broadcasted_iota(jnp.int32, (seq, 1), 0)

    @pl.when(pl.program_id(1) == 0)
    def _():
        g_all = gates_ref[...]
        lf = jax.nn.log_sigmoid(g_all)
        rin = r % blk
        pre, suf = lf, lf
        for sh in [1 << i for i in range(blk.bit_length() - 1)]:
            pre = pre + jnp.where(rin >= sh, pltpu.roll(pre, sh, 0), 0.0)
            suf = suf + jnp.where(rin < blk - sh, pltpu.roll(suf, seq - sh, 0), 0.0)
        pre3, suf3 = pre.reshape(nc, blk, LANE), suf.reshape(nc, blk, LANE)
        pre_s[...] = pre3
        suf_s[...] = suf3
        gt_s[...] = jnp.swapaxes(g_all.reshape(nc, blk, LANE), 1, 2)[:, :n_gates, :]
        pret_s[...] = jnp.swapaxes(pre3, 1, 2)[:, :n_gates, :]
        suft_s[...] = jnp.swapaxes(suf3, 1, 2)[:, :n_gates, :]

    def split_heads(x):
        x3 = x.reshape(nc, blk, heads * HD_M)
        if heads == 1:
            return x3
        tiles = jnp.stack([x3[:, :, hh * HD_M:(hh + 1) * HD_M] for hh in range(heads)], axis=1)
        return tiles.reshape(nbatch, blk, HD_M)

    def per_head(x):
        if heads == 1:
            return x
        return jnp.broadcast_to(x[:, None], (nc, heads) + x.shape[1:]).reshape((nbatch,) + x.shape[1:])

    first, last = r == 0, r == seq - 1
    qc_all = _dwconv3(q_ref[...].astype(F32), wq_ref, first, last)
    q3 = split_heads(qc_all * jax.nn.sigmoid(qc_all))
    kc_all = _dwconv3(k_ref[...].astype(F32), wk_ref, first, last)
    k3 = split_heads(kc_all * jax.nn.sigmoid(kc_all) * (HD_M ** -0.5))
    qb, kb = q3.astype(BF16), k3.astype(BF16)
    vb = split_heads(v_ref[...])
    g3 = per_head(gates_ref[...].reshape(nc, blk, LANE))
    gt3 = per_head(gt_s[...])

    if zero_state:
        cst[...] = jnp.zeros_like(cst)
        nst[...] = jnp.zeros_like(nst)
        mst[...] = jnp.zeros_like(mst)
    else:
        cst[...] = c0_ref[...]
        nst[...] = n0_ref[...]
        mst[...] = jnp.broadcast_to(m0_ref[...], mst.shape)

    tt = lax.broadcasted_iota(jnp.int32, (1, blk, blk), 1)
    ss = lax.broadcasted_iota(jnp.int32, (1, blk, blk), 2)
    lane = lax.broadcasted_iota(jnp.int32, (1, 1, LANE), 2)
    sub = lax.broadcasted_iota(jnp.int32, (1, n_gates, 1), 1)
    head = lax.broadcasted_iota(jnp.int32, (nbatch, 1, 1), 0) % heads + pl.program_id(1) * heads
    hsum = None
    for d in range(2):
        i_idx = d * 2 * HEADS_M + head
        f_idx = i_idx + HEADS_M
        mask = (ss <= tt) if d == 0 else (ss >= tt)
        b3 = per_head((pre_s if d == 0 else suf_s)[...])
        bt3 = per_head((pret_s if d == 0 else suft_s)[...])
        bcol = jnp.sum(jnp.where(lane == f_idx, b3, 0.0), axis=2, keepdims=True)
        icol = jnp.sum(jnp.where(lane == i_idx, g3, 0.0), axis=2, keepdims=True)
        brow = jnp.sum(jnp.where(sub == f_idx, bt3, 0.0), axis=1, keepdims=True)
        irow = jnp.sum(jnp.where(sub == i_idx, gt3, 0.0), axis=1, keepdims=True)
        bend = bcol[:, blk - 1:blk, :] if d == 0 else bcol[:, 0:1, :]

        dmat = jnp.where(mask, bcol - brow + irow, -jnp.inf)
        mloc = jnp.max(dmat, axis=2, keepdims=True)
        qk = jnp.einsum('ctd,csd->cts', qb, kb, preferred_element_type=F32)
        s_loc = jnp.exp(dmat - mloc) * qk
        num_loc = jnp.einsum('cts,csd->ctd', s_loc.astype(BF16), vb, preferred_element_type=F32)
        den_loc = jnp.sum(s_loc, axis=2, keepdims=True)
        to_end = bend - bcol + icol
        mk = jnp.max(to_end, axis=1, keepdims=True)
        kw = k3 * jnp.exp(to_end - mk)
        kv_s[...] = jnp.einsum('cds,cse->cde', jnp.swapaxes(kw, 1, 2).astype(BF16), vb,
                               preferred_element_type=F32)
        ks_s[...] = jnp.sum(kw, axis=1, keepdims=True)
        be_s[...] = jnp.broadcast_to(bend, be_s.shape)
        mk_s[...] = jnp.broadcast_to(mk, mk_s.shape)

        def step(j, carry, d=d):
            c = j if d == 0 else nc - 1 - j
            sl = pl.ds(c * heads, heads)
            m_prev, c_prev, n_prev = mst[d], cst[d], nst[d]
            cp_s[sl] = c_prev.astype(BF16)
            np_s[sl] = n_prev
            mp_s[sl] = m_prev
            be, mkc = be_s[sl], mk_s[sl]
            m_new = jnp.maximum(be + m_prev, mkc)
            keep = jnp.exp(be + m_prev - m_new)
            add = jnp.exp(mkc - m_new)
            cst[d] = keep * c_prev + add * kv_s[sl]
            nst[d] = keep * n_prev + add * ks_s[sl]
            mst[d] = m_new
            return carry

        lax.fori_loop(0, nc, step, 0)

        m_inter = bcol + mp_s[...][:, :, 0:1]
        m_t = jnp.maximum(m_inter, mloc)
        w_state = jnp.exp(m_inter - m_t)
        w_loc = jnp.exp(mloc - m_t)
        inter = jnp.einsum('ctd,cde->cte', qb, cp_s[...], preferred_element_type=F32)
        num = w_state * inter + w_loc * num_loc
        den = w_state * jnp.sum(q3 * np_s[...], axis=2, keepdims=True) + w_loc * den_loc
        h = num / jnp.maximum(jnp.abs(den), jnp.exp(-m_t))
        hsum = h if hsum is None else hsum + h

    hn = hsum * lax.rsqrt(jnp.mean(hsum * hsum, axis=-1, keepdims=True) + EPS)
    og3 = split_heads(og_ref[...].astype(F32))
    for c in range(nc):
        for hh in range(heads):
            cols = slice(hh * HD_M, (hh + 1) * HD_M)
            e = c * heads + hh
            y_ref[c * blk:(c + 1) * blk, cols] = (hn[e] * gh_ref[:, cols] * jax.nn.sigmoid(og3[e])).astype(y_ref.dtype)
    if emit_state:
        cn_ref[...] = cst[...]
        nn_ref[...] = nst[...]
        mn_ref[...] = mst[...]


def mlstm(proj, gates, w_conv_qk, g_head, state, *, nb, seq, row0, heads):
    blk0 = row0 // seq
    nblk = seq // SCAN_BLOCK
    groups = HEADS_M // heads
    n_gates = 4 * HEADS_M
    zero_state = state is None
    col = lambda part: pl.BlockSpec((seq, heads * HD_M), lambda b, h: (b + blk0, part * groups + h))
    in_specs = [col(0), col(1), col(2), col(3),
                pl.BlockSpec((seq, LANE), lambda b, h: (b + blk0, 0)),
                pl.BlockSpec((3, heads * HD_M), lambda b, h: (0, h)),
                pl.BlockSpec((3, heads * HD_M), lambda b, h: (0, groups + h)),
                pl.BlockSpec((1, heads * HD_M), lambda b, h: (0, h))]
    args = [proj, proj, proj, proj, gates, w_conv_qk, w_conv_qk, g_head.reshape(1, MW)]
    state_specs = [pl.BlockSpec((None, 2, heads, HD_M, HD_M), lambda b, h: (b, 0, h, 0, 0)),
                   pl.BlockSpec((None, 2, heads, 1, HD_M), lambda b, h: (b, 0, h, 0, 0))]
    if not zero_state:
        c0, n0, m0 = state
        in_specs += state_specs + [pl.BlockSpec((None, 2, heads, 1, 1), lambda b, h: (b, 0, h, 0, 0))]
        args += [c0, n0.reshape(nb, 2, HEADS_M, 1, HD_M), m0.reshape(nb, 2, HEADS_M, 1, 1)]
    out_shape = [jax.ShapeDtypeStruct((nb * seq, MW), BF16)]
    out_specs = [pl.BlockSpec((seq, heads * HD_M), lambda b, h: (b, h))]
    if zero_state:
        out_shape += [jax.ShapeDtypeStruct((nb, 2, HEADS_M, HD_M, HD_M), F32),
                      jax.ShapeDtypeStruct((nb, 2, HEADS_M, 1, HD_M), F32),
                      jax.ShapeDtypeStruct((nb, 2, HEADS_M, 1, LANE), F32)]
        out_specs += state_specs + [pl.BlockSpec((None, 2, heads, 1, LANE), lambda b, h: (b, 0, h, 0, 0))]
    per_block = lambda *shape: pltpu.VMEM((nblk,) + shape, F32)
    per_entry = lambda *shape, dtype=F32: pltpu.VMEM((nblk * heads,) + shape, dtype)
    return pl.pallas_call(
        functools.partial(_mlstm_kernel, seq=seq, heads=heads, zero_state=zero_state, emit_state=zero_state),
        out_shape=out_shape,
        grid=(nb, groups),
        in_specs=in_specs,
        out_specs=out_specs,
        scratch_shapes=[per_block(SCAN_BLOCK, LANE), per_block(SCAN_BLOCK, LANE),
                        per_block(n_gates, SCAN_BLOCK), per_block(n_gates, SCAN_BLOCK), per_block(n_gates, SCAN_BLOCK),
                        per_entry(HD_M, HD_M), per_entry(1, HD_M), per_entry(1, LANE), per_entry(1, LANE),
                        per_entry(HD_M, HD_M, dtype=BF16), per_entry(1, HD_M), per_entry(1, LANE),
                        pltpu.VMEM((2, heads, HD_M, HD_M), F32), pltpu.VMEM((2, heads, 1, HD_M), F32),
                        pltpu.VMEM((2, heads, 1, LANE), F32)],
        compiler_params=_cparams(("arbitrary", "arbitrary")),
        name="mlstm_%d" % seq,
    )(*args)


def _hyena_pre_kernel(v_ref, x1_ref, x2_ref, wv_ref, w1_ref, w2_ref, u_ref, x2c_ref):
    first, last = _seq_edges(SEQ_BLOCK, pl.program_id(0) < N_PBLK)
    x1c = _dwconv3(x1_ref[...].astype(F32), w1_ref, first, last)
    u_ref[...] = (x1c * _dwconv3(v_ref[...].astype(F32), wv_ref, first, last)).astype(u_ref.dtype)
    x2c_ref[...] = _dwconv3(x2_ref[...].astype(F32), w2_ref, first, last).astype(x2c_ref.dtype)


def hyena_pre(proj, w_conv_hy):
    tc = 256
    nct = HW // tc
    c0 = 4 * MW // tc
    pcol = lambda part: pl.BlockSpec((SEQ_BLOCK, tc), lambda i, j: (i, c0 + part * nct + j))
    wcol = lambda part: pl.BlockSpec((3, tc), lambda i, j: (0, part * nct + j))
    out = jax.ShapeDtypeStruct((N_TOK, HW), BF16)
    ospec = pl.BlockSpec((SEQ_BLOCK, tc), lambda i, j: (i, j))
    return pl.pallas_call(
        _hyena_pre_kernel,
        out_shape=[out, out],
        grid=(N_TOK // SEQ_BLOCK, nct),
        in_specs=[pcol(0), pcol(1), pcol(2), wcol(0), wcol(1), wcol(2)],
        out_specs=[ospec, ospec],
        compiler_params=_cparams(("arbitrary", "arbitrary")),
        name="hyena_pre",
    )(proj, proj, proj, w_conv_hy, w_conv_hy, w_conv_hy)


@functools.lru_cache(maxsize=None)
def _filter_tables(seq):
    t = np.linspace(0.0, 1.0, seq)[:, None]
    wpos = 2.0 * np.pi * np.arange(seq)[:, None] / seq
    bands = np.linspace(1e-4, N_BANDS - 1, N_BANDS)[None, :]
    z = np.concatenate([t, np.cos(bands * wpos), -np.sin(bands * wpos)], axis=-1)
    z = np.pad(z, ((0, 0), (0, LANE - FILTER_EMB)))
    max_decay = math.log(DECAY_TARGET) / DECAY_FAST
    min_decay = math.log(DECAY_TARGET) / DECAY_SLOW
    deltas = np.abs(np.linspace(min_decay, max_decay, HW))
    decay = np.exp(-t * np.concatenate([deltas, deltas])[None, :])
    return z.astype(np.float32), decay.astype(np.float32)


@functools.lru_cache(maxsize=None)
def _dft_tables(seq, tk):
    n = 2 * seq
    k = np.arange(seq)[:, None]
    t = np.arange(seq)[None, :]
    ang = 2.0 * np.pi * ((k * t) % n) / n
    alt = np.where(np.arange(seq) % 2 == 0, 1.0, -1.0)
    cm, sm = np.cos(ang), np.sin(ang)
    sm[0, :] = alt
    fwd = np.stack([cm.reshape(seq // tk, tk, seq), sm.reshape(seq // tk, tk, seq)], axis=1)
    wk = np.where(np.arange(seq) == 0, 1.0, 2.0)[None, :]
    ci = (np.cos(ang.T) * wk) / n
    si = np.sin(ang.T) * 2.0 / n
    si[:, 0] = alt / n
    inv = np.concatenate([ci, si], axis=1)
    return fwd.astype(np.float32), inv.astype(np.float32)


def _filter_kernel(z_ref, w1_ref, b1_ref, w2_ref, b2_ref, w3_ref, fr_ref, dec_ref, hs_ref, hd_ref):
    fr = fr_ref[...]
    hp = functools.partial(jnp.dot, precision=HIGHEST, preferred_element_type=F32)
    h1 = jnp.sin(fr * (hp(z_ref[...], w1_ref[...]) + b1_ref[...]))
    h2 = jnp.sin(fr * (hp(h1, w2_ref[...]) + b2_ref[...]))
    w3 = w3_ref[...]
    h_hi, w_hi = h2.astype(BF16), w3.astype(BF16)
    h_lo, w_lo = (h2 - h_hi.astype(F32)).astype(BF16), (w3 - w_hi.astype(F32)).astype(BF16)
    filt = (_dot(h_hi, w_hi) + _dot(h_hi, w_lo) + _dot(h_lo, w_hi)) * dec_ref[...]
    past, fut = filt[:, :HW], filt[:, HW:]
    rows = filt.shape[0]
    grow = lax.broadcasted_iota(jnp.int32, (rows, 1), 0) + pl.program_id(0) * rows
    fut = jnp.where(grow == 0, 0.0, fut)
    hs_ref[...] = past + fut
    hd_ref[...] = past - fut


def filter_gen(seq, w1, b1, w2, b2, w3, freq):
    z, decay = _filter_tables(seq)
    tl = 256
    fh = FILTER_HIDDEN
    full = lambda shape: pl.BlockSpec(shape, lambda i: (0, 0))
    out = jax.ShapeDtypeStruct((seq, HW), F32)
    return pl.pallas_call(
        _filter_kernel,
        out_shape=[out, out],
        grid=(seq // tl,),
        in_specs=[pl.BlockSpec((tl, LANE), lambda i: (i, 0)), full((LANE, fh)), full((1, fh)), full((fh, fh)),
                  full((1, fh)), full((fh, 2 * HW)), full((1, fh)), pl.BlockSpec((tl, 2 * HW), lambda i: (i, 0))],
        out_specs=[pl.BlockSpec((tl, HW), lambda i: (i, 0))] * 2,
        compiler_params=_cparams(("arbitrary",)),
        name="filter_gen",
    )(jnp.asarray(z), jnp.pad(w1, ((0, LANE - FILTER_EMB), (0, 0))), b1.reshape(1, fh), w2, b2.reshape(1, fh), w3,
      freq.reshape(1, fh), jnp.asarray(decay))


def _dft_filter_kernel(a_ref, hs_ref, hd_ref, k_ref, hs_bf, hd_bf):
    @pl.when(pl.program_id(0) == 0)
    def _():
        hs_bf[...] = hs_ref[...].astype(BF16)
        hd_bf[...] = hd_ref[...].astype(BF16)

    k_ref[0] = _dot(a_ref[0].astype(BF16), hs_bf[...])
    k_ref[1] = _dot(a_ref[1].astype(BF16), hd_bf[...])

    @pl.when(pl.program_id(0) == 0)
    def _():
        k_ref[1, 0:1, :] = _dot(a_ref[1, 0:8, :].astype(BF16), hs_bf[...])[0:1, :]


def dft_filter(seq, tk, hs, hd):
    fwd, _ = _dft_tables(seq, tk)
    return pl.pallas_call(
        _dft_filter_kernel,
        out_shape=jax.ShapeDtypeStruct((2, seq, HW), F32),
        grid=(seq // tk,),
        in_specs=[pl.BlockSpec((None, 2, tk, seq), lambda m: (m, 0, 0, 0)),
                  pl.BlockSpec((seq, HW), lambda m: (0, 0)), pl.BlockSpec((seq, HW), lambda m: (0, 0))],
        out_specs=pl.BlockSpec((2, tk, HW), lambda m: (0, m, 0)),
        scratch_shapes=[pltpu.VMEM((seq, HW), BF16), pltpu.VMEM((seq, HW), BF16)],
        compiler_params=_cparams(("arbitrary",)),
        name="dft_filter",
    )(jnp.asarray(fwd), hs, hd)


def _dft_fwd_kernel(a_ref, u_ref, k_ref, y_ref, u_bf, tbl):
    m = pl.program_id(1)

    @pl.when(m == 0)
    def _():
        u_bf[...] = u_ref[...].astype(BF16)

    @pl.when(pl.program_id(0) == 0)
    def _():
        tbl[m] = a_ref[...].astype(BF16)

    ure = _dot(tbl[m, 0], u_bf[...])
    uim = _dot(tbl[m, 1], u_bf[...])
    kre, kim = k_ref[0], k_ref[1]
    packed = (lax.broadcasted_iota(jnp.int32, (ure.shape[0], 1), 0) == 0) & (m == 0)
    y_ref[0] = jnp.where(packed, ure * kre, ure * kre - uim * kim).astype(y_ref.dtype)
    y_ref[1] = jnp.where(packed, uim * kim, ure * kim + uim * kre).astype(y_ref.dtype)


def dft_fwd(seq, tk, u, kf, *, nb, row0):
    fwd, _ = _dft_tables(seq, tk)
    blk0 = row0 // seq
    nm = seq // tk
    return pl.pallas_call(
        _dft_fwd_kernel,
        out_shape=jax.ShapeDtypeStruct((nb, 2, seq, HW), BF16),
        grid=(nb, nm),
        in_specs=[pl.BlockSpec((None, 2, tk, seq), lambda b, m: (jnp.where(b == 0, m, nm - 1), 0, 0, 0)),
                  pl.BlockSpec((seq, HW), lambda b, m: (b + blk0, 0)),
                  pl.BlockSpec((2, tk, HW), lambda b, m: (0, m, 0))],
        out_specs=pl.BlockSpec((None, 2, tk, HW), lambda b, m: (b, 0, m, 0)),
        scratch_shapes=[pltpu.VMEM((seq, HW), BF16), pltpu.VMEM((nm, 2, tk, seq), BF16)],
        compiler_params=_cparams(("arbitrary", "arbitrary"), DFT_VMEM_LIMIT),
        name="dft_fwd",
    )(jnp.asarray(fwd), u, kf)


def _dft_inv_kernel(a_ref, y_ref, u_ref, x2_ref, bias_ref, o_ref, tbl, *, seq):
    t = pl.program_id(1)

    @pl.when(pl.program_id(0) == 0)
    def _():
        tbl[t] = a_ref[...].astype(BF16)

    conv = _dot(tbl[t, :, :seq], y_ref[0]) + _dot(tbl[t, :, seq:], y_ref[1])
    u = u_ref[...].astype(F32)
    o_ref[...] = (x2_ref[...].astype(F32) * (conv + bias_ref[...] * u)).astype(o_ref.dtype)


def dft_inv(seq, tk, y, u, x2c, bias, *, nb, row0):
    _, inv = _dft_tables(seq, tk)
    tm = min(seq, 512)
    nt = seq // tm
    blk0 = row0 // tm
    rows = lambda b, t: (b * nt + t + blk0, 0)
    return pl.pallas_call(
        functools.partial(_dft_inv_kernel, seq=seq),
        out_shape=jax.ShapeDtypeStruct((nb * seq, HW), BF16),
        grid=(nb, nt),
        in_specs=[pl.BlockSpec((tm, 2 * seq), lambda b, t: (jnp.where(b == 0, t, nt - 1), 0)),
                  pl.BlockSpec((None, 2, seq, HW), lambda b, t: (b, 0, 0, 0)),
                  pl.BlockSpec((tm, HW), rows), pl.BlockSpec((tm, HW), rows),
                  pl.BlockSpec((1, HW), lambda b, t: (0, 0))],
        out_specs=pl.BlockSpec((tm, HW), lambda b, t: (b * nt + t, 0)),
        scratch_shapes=[pltpu.VMEM((nt, tm, 2 * seq), BF16)],
        compiler_params=_cparams(("arbitrary", "arbitrary"), DFT_VMEM_LIMIT),
        name="dft_inv",
    )(jnp.asarray(inv), y, u, x2c, bias.reshape(1, HW))


def hyena_group(seq, u, x2c, filt_w, bias, *, nb, row0):
    tk = min(seq, 512)
    hs, hd = filter_gen(seq, *filt_w)
    kf = dft_filter(seq, tk, hs, hd)
    y = dft_fwd(seq, tk, u, kf, nb=nb, row0=row0)
    return dft_inv(seq, tk, y, u, x2c, bias, nb=nb, row0=row0)


def mixer_ab_parts(proj, gates, state_s, w_conv_qk, g_head, w_conv_hy, w_f1, b_f1, w_f2, b_f2, w_f3, freq, hy_bias):
    ym_p, c_p, n_p, m_p = mlstm(proj, gates, w_conv_qk, g_head, None, nb=BATCH, seq=SEQ, row0=0, heads=HEADS_M)
    (ym_s,) = mlstm(proj, gates, w_conv_qk, g_head, state_s, nb=DEC_BATCH, seq=DEC_SEQ, row0=N_P, heads=1)
    u, x2c = hyena_pre(proj, w_conv_hy)
    filt_w = (w_f1, b_f1, w_f2, b_f2, w_f3, freq)
    yh_p = hyena_group(SEQ, u, x2c, filt_w, hy_bias, nb=BATCH, row0=0)
    yh_s = hyena_group(DEC_SEQ, u, x2c, filt_w, hy_bias, nb=DEC_BATCH, row0=N_P)
    state_p = (c_p, n_p[:, :, :, 0, :], m_p[:, :, :, 0, 0])
    return (ym_p, ym_s), (yh_p, yh_s), state_p


def kernel(x_prompt, x_sample, state_mlstm_C, state_mlstm_n, state_mlstm_m, cache_na_k, cache_na_v, c, c_ctx, w_ada, b_ada, g_mix, g_ffn, g_final, w_in_ab, b_gates, w_conv_qk, g_mlstm, w_conv_hy, w_filt1, b_filt1, w_filt2, b_filt2, w_filt3, filt_freq, hyena_bias, w_out_ab, w_in_c, rpb_c, w_out_c, w_up, w_conv_ffn, w_down):
    cmat = jnp.concatenate([c, c_ctx[None, :], jnp.zeros((MOD_ROWS - DEC_BATCH - 1, D), F32)], axis=0)
    mod_all = adaln_all(cmat, w_ada, b_ada).reshape(DEPTH, MOD_ROWS, 6, 1, D)
    x = (x_prompt.reshape(N_P, D), x_sample.reshape(N_S, D))
    h = None
    new_c, new_n, new_m, new_k, new_v = [], [], [], [], []
    for l in range(DEPTH):
        e = l // 2
        if l % 2 == 0:
            assert l == 0, "a later mixer A/B layer would take the fused norm of the layer before it"
            proj, gates = ab_in(x, g_mix, mod_all, l, jnp.swapaxes(w_in_ab, 1, 2), b_gates[e], e)
            state_s = (state_mlstm_C[:, e], state_mlstm_n[:, e], state_mlstm_m[:, e])
            y_m, y_h, (c_p, n_p, m_p) = mixer_ab_parts(
                proj, gates, state_s, w_conv_qk[e], g_mlstm[e], w_conv_hy[e], w_filt1[e], b_filt1[e],
                w_filt2[e], b_filt2[e], w_filt3[e], filt_freq[e], hyena_bias[e])
            a_list = [y_m, y_h]
            w_list = [(w_out_ab, (None, MW, D), lambda i, e=e: (e, 0, 0)),
                      (w_out_ab, (None, HW, D), lambda i, e=e: (e, 1, 0))]
            new_c.append(c_p[:, None])
            new_n.append(n_p[:, None])
            new_m.append(m_p[:, None])
        else:
            o, k_new, v_new = mixer_c(h, cache_na_k[:, e], cache_na_v[:, e], w_in_c, e, rpb_c[e])
            a_list = [o]
            w_list = [(w_out_c, (None, D, D), lambda i, e=e: (e, 0, 0))]
            new_k.append(k_new)
            new_v.append(v_new)
        x, h = out_proj(a_list, w_list, x, mod_all, l, 2, g_ffn, l, (l, 3, 4), name="mixer_out")
        mid = ffn_up(h, w_up, w_conv_ffn, l)
        w_list = [(w_down, (None, FF, D), lambda i, l=l: (l, 0, 0))]
        if l + 1 < DEPTH:
            x, h = out_proj([mid], w_list, x, mod_all, l, 5, g_mix, l + 1, (l + 1, 0, 1), name="ffn_down")
        else:
            y_p, y_s = out_proj([mid], w_list, x, mod_all, l, 5, g_final, 0, None, name="ffn_down_final")
    cat = lambda parts: parts[0] if len(parts) == 1 else jnp.concatenate(parts, axis=1)
    return (y_p.reshape(BATCH, SEQ, D), y_s.reshape(DEC_BATCH, DEC_SEQ, D), cat(new_c), cat(new_n), cat(new_m),
            cat(new_k), cat(new_v))
```

```python
import functools
import math

import numpy as np
import jax
import jax.numpy as jnp
from jax import lax
from jax.experimental import pallas as pl
from jax.experimental.pallas import tpu as pltpu

F32 = jnp.float32
BF16 = jnp.bfloat16

D = 1024
BATCH, SEQ = 16, 256
DEC_BATCH, DEC_SEQ = 4, 2048
PAST_LEN = 512
DEPTH = 2
GRID_W = 64
GRID_R = DEC_SEQ // GRID_W
HEADS_M = 4
MW = D // 2
HD_M = MW // HEADS_M
HW = D // 2
N_BANDS = 16
FILTER_EMB = 2 * N_BANDS + 1
FILTER_HIDDEN = 64
DECAY_FAST, DECAY_SLOW, DECAY_TARGET = 0.3, 1.5, 1e-2
NA_HEADS = 16
NA_HD = D // NA_HEADS
NA_KH, NA_KW = 8, 16
FF = 2816
EPS = 1e-6

N_P = BATCH * SEQ
N_S = DEC_BATCH * DEC_SEQ
N_TOK = N_P + N_S
CTX_ROW = DEC_BATCH
MOD_ROWS = 8
LANE = 128
VMEM_LIMIT = 48 * 1024 * 1024
QKV_VMEM_LIMIT = 56 * 1024 * 1024
DFT_VMEM_LIMIT = 56 * 1024 * 1024
HIGHEST = lax.Precision.HIGHEST


def _cparams(sem, vmem_limit=VMEM_LIMIT):
    return pltpu.CompilerParams(dimension_semantics=sem, vmem_limit_bytes=vmem_limit)


def _mod_row(i, bm):
    return jnp.where(i < N_P // bm, CTX_ROW, (i - N_P // bm) // (DEC_SEQ // bm))


def _dot(a, b):
    return jnp.dot(a, b, preferred_element_type=F32)


def _adaln_kernel(c_ref, w_ref, b_ref, o_ref):
    cv = c_ref[...]
    s = cv * jax.nn.sigmoid(cv)
    o_ref[...] = _dot(s.astype(BF16), w_ref[...].astype(BF16)) + b_ref[...]


def adaln_all(cmat, w_ada, b_ada):
    tn = 1024
    return pl.pallas_call(
        _adaln_kernel,
        out_shape=jax.ShapeDtypeStruct((DEPTH, MOD_ROWS, 6 * D), F32),
        grid=(DEPTH, 6 * D // tn),
        in_specs=[
            pl.BlockSpec((MOD_ROWS, D), lambda l, j: (0, 0)),
            pl.BlockSpec((None, D, tn), lambda l, j: (l, 0, j)),
            pl.BlockSpec((None, 1, tn), lambda l, j: (l, 0, j)),
        ],
        out_specs=pl.BlockSpec((None, MOD_ROWS, tn), lambda l, j: (l, 0, j)),
        compiler_params=_cparams(("arbitrary", "arbitrary")),
        name="adaln",
    )(cmat, w_ada, b_ada.reshape(DEPTH, 1, 6 * D))


ROW_BM = 512
ROW_NPB = N_P // ROW_BM


def _row_specs(arr, cols):
    if isinstance(arr, tuple):
        return ([pl.BlockSpec((ROW_BM, cols), lambda i: (jnp.minimum(i, ROW_NPB - 1), 0)),
                 pl.BlockSpec((ROW_BM, cols), lambda i: (jnp.maximum(i - ROW_NPB, 0), 0))], list(arr))
    return [pl.BlockSpec((ROW_BM, cols), lambda i: (i, 0))], [arr]


def _pick_rows(refs):
    if len(refs) == 1:
        return refs[0][...]
    return jnp.where(pl.program_id(0) < ROW_NPB, refs[0][...], refs[1][...])


def _mod_spec(layer, which):
    return pl.BlockSpec((None, None, None, 1, D), lambda i: (layer, _mod_row(i, ROW_BM), which, 0, 0))


def _layer_vec_spec(layer):
    return pl.BlockSpec((None, 1, D), lambda i: (layer, 0, 0))


def _rms_mod(x, g_ref, sh_ref, sc_ref):
    y = x * lax.rsqrt(jnp.mean(x * x, axis=-1, keepdims=True) + EPS)
    return (y * g_ref[...]) * (1.0 + sc_ref[...]) + sh_ref[...]


def _out_proj_kernel(*refs, a_counts, n_x, final):
    pos = 0
    a_groups = []
    for cnt in a_counts:
        a_groups.append(refs[pos:pos + cnt])
        pos += cnt
    w_refs = refs[pos:pos + len(a_counts)]
    pos += len(a_counts)
    x_refs = refs[pos:pos + n_x]
    pos += n_x
    gt_ref, g_ref = refs[pos:pos + 2]
    pos += 2
    if not final:
        sh_ref, sc_ref = refs[pos:pos + 2]
        pos += 2
    out_a, out_b = refs[pos:pos + 2]
    wbf = refs[pos + 2:]

    @pl.when(pl.program_id(0) == 0)
    def _():
        for w_ref, wb in zip(w_refs, wbf):
            wb[...] = w_ref[...].astype(BF16)

    acc = None
    for group, wb in zip(a_groups, wbf):
        part = _dot(_pick_rows(group), wb[...])
        acc = part if acc is None else acc + part
    xn = _pick_rows(x_refs) + gt_ref[...] * acc
    if final:
        y = xn * lax.rsqrt(jnp.mean(xn * xn, axis=-1, keepdims=True) + EPS) * g_ref[...]

        @pl.when(pl.program_id(0) < ROW_NPB)
        def _():
            out_a[...] = y

        @pl.when(pl.program_id(0) >= ROW_NPB)
        def _():
            out_b[...] = y
    else:
        out_a[...] = xn
        out_b[...] = _rms_mod(xn, g_ref, sh_ref, sc_ref).astype(out_b.dtype)


def out_proj(a_list, w_list, x, mod_all, layer, gt_idx, g_all, g_layer, norm_mod_idx=None, name="out_proj"):
    final = norm_mod_idx is None
    in_specs, args, a_counts = [], [], []
    for a in a_list:
        cols = (a[0] if isinstance(a, tuple) else a).shape[1]
        specs, ops = _row_specs(a, cols)
        in_specs += specs
        args += ops
        a_counts.append(len(ops))
    w_shapes = []
    for w, block, imap in w_list:
        in_specs.append(pl.BlockSpec(block, imap, pipeline_mode=pl.Buffered(1)))
        args.append(w)
        w_shapes.append(tuple(b for b in block if b is not None))
    x_specs, x_args = _row_specs(x, D)
    in_specs += x_specs + [_mod_spec(layer, gt_idx), _layer_vec_spec(g_layer)]
    args += x_args + [mod_all, g_all.reshape(-1, 1, D)]
    if final:
        out_shape = [jax.ShapeDtypeStruct((N_P, D), F32), jax.ShapeDtypeStruct((N_S, D), F32)]
        out_specs = [pl.BlockSpec((ROW_BM, D), lambda i: (jnp.minimum(i, ROW_NPB - 1), 0)),
                     pl.BlockSpec((ROW_BM, D), lambda i: (jnp.maximum(i - ROW_NPB, 0), 0))]
    else:
        n_layer, sh_idx, sc_idx = norm_mod_idx
        in_specs += [_mod_spec(n_layer, sh_idx), _mod_spec(n_layer, sc_idx)]
        args += [mod_all, mod_all]
        out_shape = [jax.ShapeDtypeStruct((N_TOK, D), F32), jax.ShapeDtypeStruct((N_TOK, D), BF16)]
        out_specs = [pl.BlockSpec((ROW_BM, D), lambda i: (i, 0))] * 2
    return pl.pallas_call(
        functools.partial(_out_proj_kernel, a_counts=tuple(a_counts), n_x=len(x_args), final=final),
        out_shape=out_shape,
        grid=(N_TOK // ROW_BM,),
        in_specs=in_specs,
        out_specs=out_specs,
        scratch_shapes=[pltpu.VMEM(s, BF16) for s in w_shapes],
        compiler_params=_cparams(("arbitrary",)),
        name=name,
    )(*args)


def _qkv_kernel(a_ref, w_ref, o_ref, kc_ref, vc_ref, wbf):
    @pl.when(pl.program_id(0) == 0)
    def _():
        wbf[...] = w_ref[...].astype(BF16)

    a = a_ref[...]
    for part in range(3):
        acc = _dot(a, wbf[:, part * D:(part + 1) * D])
        for pp in range(NA_HEADS // 2):
            o_ref[part, pp] = acc[:, pp * 2 * NA_HD:(pp + 1) * 2 * NA_HD].astype(o_ref.dtype)
        if part > 0:
            c_ref = kc_ref if part == 1 else vc_ref

            @pl.when(pl.program_id(0) < ROW_NPB)
            def _(acc=acc, c_ref=c_ref):
                for b in range(ROW_BM // SEQ):
                    for hh in range(NA_HEADS):
                        c_ref[b, hh] = acc[b * SEQ:(b + 1) * SEQ, hh * NA_HD:(hh + 1) * NA_HD]


def qkv_proj(h, w_in, layer):
    seqs = ROW_BM // SEQ
    cache = jax.ShapeDtypeStruct((BATCH, 1, NA_HEADS, SEQ, NA_HD), F32)
    cache_spec = pl.BlockSpec((seqs, None, NA_HEADS, SEQ, NA_HD),
                              lambda i: (jnp.minimum(i, ROW_NPB - 1), 0, 0, 0, 0))
    return pl.pallas_call(
        _qkv_kernel,
        out_shape=[jax.ShapeDtypeStruct((3, NA_HEADS // 2, N_TOK, 2 * NA_HD), BF16), cache, cache],
        grid=(N_TOK // ROW_BM,),
        in_specs=[pl.BlockSpec((ROW_BM, D), lambda i: (i, 0)),
                  pl.BlockSpec((None, D, 3 * D), lambda i: (layer, 0, 0), pipeline_mode=pl.Buffered(1))],
        out_specs=[pl.BlockSpec((3, NA_HEADS // 2, ROW_BM, 2 * NA_HD), lambda i: (0, 0, i, 0)),
                   cache_spec, cache_spec],
        scratch_shapes=[pltpu.VMEM((D, 3 * D), BF16)],
        compiler_params=_cparams(("arbitrary",), QKV_VMEM_LIMIT),
        name="qkv_proj",
    )(h, w_in)


AB_MAIN = 4 * MW + 3 * HW
N_GATES = 4 * HEADS_M


def _ab_in_kernel(*refs):
    x_refs = refs[:-8]
    g_ref, sh_ref, sc_ref, w_ref, b_ref, proj_ref, gates_ref, wbf = refs[-8:]

    @pl.when(pl.program_id(0) == 0)
    def _():
        wbf[:4 * MW, :] = w_ref[:4 * MW, :].astype(BF16)
        wbf[4 * MW:AB_MAIN, :] = w_ref[4 * MW + N_GATES:, :].astype(BF16)
        wbf[AB_MAIN:AB_MAIN + N_GATES, :] = w_ref[4 * MW:4 * MW + N_GATES, :].astype(BF16)
        wbf[AB_MAIN + N_GATES:, :] = jnp.zeros((LANE - N_GATES, D), BF16)

    h = _rms_mod(_pick_rows(x_refs), g_ref, sh_ref, sc_ref).astype(BF16)
    proj_ref[...] = lax.dot_general(h, wbf[:AB_MAIN, :], NT_DIMS, preferred_element_type=F32).astype(proj_ref.dtype)
    gates_ref[...] = lax.dot_general(h, wbf[AB_MAIN:, :], NT_DIMS, preferred_element_type=F32) + b_ref[...]


def ab_in(x, g_all, mod_all, layer, w_in_t, b_gates, e):
    x_specs, x_args = _row_specs(x, D)
    b_pad = jnp.pad(b_gates, (0, LANE - N_GATES)).reshape(1, LANE)
    return pl.pallas_call(
        _ab_in_kernel,
        out_shape=[jax.ShapeDtypeStruct((N_TOK, AB_MAIN), BF16), jax.ShapeDtypeStruct((N_TOK, LANE), F32)],
        grid=(N_TOK // ROW_BM,),
        in_specs=x_specs + [_layer_vec_spec(layer), _mod_spec(layer, 0), _mod_spec(layer, 1),
                            pl.BlockSpec((None, w_in_t.shape[1], D), lambda i: (e, 0, 0),
                                         pipeline_mode=pl.Buffered(1)),
                            pl.BlockSpec((1, LANE), lambda i: (0, 0))],
        out_specs=[pl.BlockSpec((ROW_BM, AB_MAIN), lambda i: (i, 0)), pl.BlockSpec((ROW_BM, LANE), lambda i: (i, 0))],
        scratch_shapes=[pltpu.VMEM((AB_MAIN + LANE, D), BF16)],
        compiler_params=_cparams(("arbitrary",)),
        name="ab_in",
    )(*x_args, g_all.reshape(-1, 1, D), mod_all, mod_all, w_in_t, b_pad)


SEQ_BLOCK = DEC_SEQ
N_PBLK = N_P // SEQ_BLOCK


def _seq_edges(rows, is_prompt):
    r = lax.broadcasted_iota(jnp.int32, (rows, 1), 0)
    first = (r == 0) | (is_prompt & (r % SEQ == 0))
    last = (r == rows - 1) | (is_prompt & (r % SEQ == SEQ - 1))
    return first, last


def _dwconv3(x, w_ref, first, last):
    rows = x.shape[0]
    prev = jnp.where(first, 0.0, pltpu.roll(x, 1, 0))
    nxt = jnp.where(last, 0.0, pltpu.roll(x, rows - 1, 0))
    return prev * w_ref[0:1, :] + x * w_ref[1:2, :] + nxt * w_ref[2:3, :]


def _gated_gelu_of_half(y, g):
    c = math.sqrt(2.0 / math.pi)
    t = jnp.tanh(y * (2.0 * c + (8.0 * 0.044715 * c) * (y * y)))
    return (y + y * t) * g


FFN_BLOCK = 2 * DEC_SEQ
FFN_CHUNKS = 8
FFN_HALO = 8


def _ffn_up_kernel(h_ref, wa_ref, wg_ref, wc_ref, o_ref):
    is_prompt = pl.program_id(0) < N_P // FFN_BLOCK
    rows = FFN_BLOCK // FFN_CHUNKS
    wa = wa_ref[...].astype(BF16)
    wg = wg_ref[...].astype(BF16)
    wc_half = 0.5 * wc_ref[...]
    zeros = jnp.zeros((FFN_HALO, o_ref.shape[1]), F32)

    def matmuls(r):
        hr = h_ref[r * rows:(r + 1) * rows, :]
        return _dot(hr, wa), _dot(hr, wg)

    def activation(r, a_prev, a_cur, a_next, g):
        win = jnp.concatenate([zeros if a_prev is None else a_prev[rows - FFN_HALO:], a_cur,
                               zeros if a_next is None else a_next[:FFN_HALO]], axis=0)
        ridx = lax.broadcasted_iota(jnp.int32, (rows + 2 * FFN_HALO, 1), 0) + (r * rows - FFN_HALO)
        first = (ridx % DEC_SEQ == 0) | (is_prompt & (ridx % SEQ == 0))
        last = (ridx % DEC_SEQ == DEC_SEQ - 1) | (is_prompt & (ridx % SEQ == SEQ - 1))
        half_conv = _dwconv3(win, wc_half, first, last)[FFN_HALO:FFN_HALO + rows]
        o_ref[r * rows:(r + 1) * rows, :] = _gated_gelu_of_half(half_conv, g).astype(o_ref.dtype)

    acts = [matmuls(0)]
    for r in range(1, FFN_CHUNKS):
        acts.append(matmuls(r))
        activation(r - 1, acts[r - 2][0] if r >= 2 else None, acts[r - 1][0], acts[r][0], acts[r - 1][1])
    activation(FFN_CHUNKS - 1, acts[-2][0], acts[-1][0], None, acts[-1][1])


def ffn_up(h, w_up, w_conv, layer):
    tc = 256
    nct = FF // tc
    return pl.pallas_call(
        _ffn_up_kernel,
        out_shape=jax.ShapeDtypeStruct((N_TOK, FF), BF16),
        grid=(N_TOK // FFN_BLOCK, nct),
        in_specs=[
            pl.BlockSpec((FFN_BLOCK, D), lambda i, j: (i, 0)),
            pl.BlockSpec((None, D, tc), lambda i, j: (layer, 0, j)),
            pl.BlockSpec((None, D, tc), lambda i, j: (layer, 0, j + nct)),
            pl.BlockSpec((None, 3, tc), lambda i, j: (layer, 0, j)),
        ],
        out_specs=pl.BlockSpec((FFN_BLOCK, tc), lambda i, j: (i, j)),
        compiler_params=_cparams(("arbitrary", "arbitrary")),
        name="ffn_up",
    )(h, w_up, w_up, w_conv)


HEAD_PAIRS = NA_HEADS // 2
NT_DIMS = (((1,), (1,)), ((), ()))


def _pair_mask(shape):
    return lax.broadcasted_iota(jnp.int32, shape, len(shape) - 1) < NA_HD


def _one_head(x2, first):
    keep = _pair_mask(x2.shape) if first else ~_pair_mask(x2.shape)
    return jnp.where(keep, x2, jnp.zeros_like(x2))


def _ctx_attn_kernel(q_ref, k_ref, v_ref, o_ref, *, pairs):
    outs = []
    for pp in range(pairs):
        q2, k2, v2 = q_ref[pp], k_ref[pp], v_ref[pp]
        res = []
        for first in (True, False):
            s = lax.dot_general(_one_head(q2, first), k2, NT_DIMS, preferred_element_type=F32) * (NA_HD ** -0.5)
            m = jnp.max(s, axis=-1, keepdims=True)
            p = jnp.exp(s - m)
            l = jnp.sum(p, axis=-1, keepdims=True)
            res.append(_dot(p.astype(BF16), v2) / l)
        outs.append(jnp.where(_pair_mask(res[0].shape), res[0], res[1]))
    o_ref[...] = jnp.concatenate(outs, axis=-1).astype(o_ref.dtype)


def ctx_attention(qkv):
    pairs = HEAD_PAIRS
    spec = lambda part: pl.BlockSpec((None, pairs, SEQ, 2 * NA_HD), lambda b, h: (part, h, b, 0))
    return pl.pallas_call(
        functools.partial(_ctx_attn_kernel, pairs=pairs),
        out_shape=jax.ShapeDtypeStruct((N_P, D), BF16),
        grid=(BATCH, HEAD_PAIRS // pairs),
        in_specs=[spec(0), spec(1), spec(2)],
        out_specs=pl.BlockSpec((SEQ, pairs * 2 * NA_HD), lambda b, h: (b, h)),
        compiler_params=_cparams(("arbitrary", "arbitrary")),
        name="ctx_attn",
    )(qkv, qkv, qkv)


def _na_tables():
    q = np.arange(GRID_W)[:, None]
    w = np.arange(GRID_W)[None, :]
    idx_c = np.clip(w - q + (NA_KW - 1), 0, 2 * NA_KW - 2)
    onehot = (idx_c.reshape(1, -1) == np.arange(32)[:, None]).astype(np.float32)
    c_start = np.clip(np.arange(GRID_W) - NA_KW // 2, 0, GRID_W - NA_KW)[:, None]
    inside = (w >= c_start) & (w < c_start + NA_KW)
    cmask = np.where(inside, 0.0, -np.inf).astype(np.float32)
    return onehot, np.tile(cmask, (1, 2))


def _rpb_expand_kernel(r_ref, e_ref, o_ref):
    o_ref[...] = jnp.dot(r_ref[...], e_ref[...], precision=HIGHEST, preferred_element_type=F32)


def rpb_expand(rpb):
    onehot, _ = _na_tables()
    rp = jnp.pad(rpb, ((0, 0), (0, 1), (0, 1)))
    return pl.pallas_call(
        _rpb_expand_kernel,
        out_shape=jax.ShapeDtypeStruct((NA_HEADS, 16, GRID_W * GRID_W), F32),
        grid=(NA_HEADS,),
        in_specs=[pl.BlockSpec((None, 16, 32), lambda h: (h, 0, 0)),
                  pl.BlockSpec((32, GRID_W * GRID_W), lambda h: (0, 0))],
        out_specs=pl.BlockSpec((None, 16, GRID_W * GRID_W), lambda h: (h, 0, 0)),
        compiler_params=_cparams(("arbitrary",)),
        name="rpb_expand",
    )(rp, jnp.asarray(onehot))


NA_QROWS = 8
NA_WIN = NA_QROWS + NA_KH


def _na_attn_kernel(q_ref, k_ref, v_ref, kc_ref, vc_ref, t_ref, o_ref, p_loc, p_ctx):
    nq = NA_QROWS * GRID_W
    nk = NA_WIN * GRID_W
    pair = 2 * GRID_W
    nt = NT_DIMS
    lane = lax.broadcasted_iota(jnp.int32, (GRID_W, pair), 1)
    zero_tile = jnp.zeros((GRID_W, pair), BF16)
    kc = kc_ref[...].reshape(2 * NA_HD, PAST_LEN).astype(BF16)
    vc = vc_ref[...].reshape(2 * NA_HD, PAST_LEN).astype(BF16)
    for blk in range(GRID_R // NA_QROWS):
        k0 = min(max(NA_QROWS * blk - NA_KH // 2, 0), GRID_R - NA_WIN)
        q2 = q_ref[blk * nq:(blk + 1) * nq, :] * (NA_HD ** -0.5)
        kw = k_ref[k0 * GRID_W:k0 * GRID_W + nk, :]
        vw = v_ref[k0 * GRID_W:k0 * GRID_W + nk, :]
        outs = []
        for hh in range(2):
            q = _one_head(q2, hh == 0)
            s_loc = lax.dot_general(q, kw, nt, preferred_element_type=F32)
            s_ctx = _dot(q, kc)
            denoms = []
            for qi in range(NA_QROWS):
                r = NA_QROWS * blk + qi
                r_start = min(max(r - NA_KH // 2, 0), GRID_R - NA_KH)
                rows = slice(qi * GRID_W, (qi + 1) * GRID_W)
                ctx_tiles = [s_ctx[rows, c * pair:(c + 1) * pair] for c in range(PAST_LEN // pair)]
                tiles = {}
                for j in range(nk // pair):
                    kr = k0 + 2 * j
                    ok0 = r_start <= kr < r_start + NA_KH
                    ok1 = r_start <= kr + 1 < r_start + NA_KH
                    if not (ok0 or ok1):
                        continue
                    sb = s_loc[rows, j * pair:(j + 1) * pair] + t_ref[hh, kr - r + NA_KH]
                    if not (ok0 and ok1):
                        sb = jnp.where((lane < GRID_W) if ok0 else (lane >= GRID_W), sb, -jnp.inf)
                    tiles[j] = sb
                mx = functools.reduce(jnp.maximum, list(tiles.values()) + ctx_tiles)
                m = jnp.max(mx, axis=1, keepdims=True)
                acc = None
                for j in range(nk // pair):
                    if j in tiles:
                        p = jnp.exp(tiles[j] - m)
                        acc = p if acc is None else acc + p
                        p_loc[rows, j * pair:(j + 1) * pair] = p.astype(BF16)
                    else:
                        p_loc[rows, j * pair:(j + 1) * pair] = zero_tile
                for c, t in enumerate(ctx_tiles):
                    p = jnp.exp(t - m)
                    acc = acc + p
                    p_ctx[rows, c * pair:(c + 1) * pair] = p.astype(BF16)
                denoms.append(jnp.sum(acc, axis=1, keepdims=True))
            pv = _dot(p_loc[...], vw) + lax.dot_general(p_ctx[...], vc, nt, preferred_element_type=F32)
            outs.append(pv / jnp.concatenate(denoms, axis=0))
        o_ref[blk * nq:(blk + 1) * nq, :] = jnp.where(_pair_mask(outs[0].shape), outs[0], outs[1]).astype(o_ref.dtype)


def na_attention(qkv, k_ctx, v_ctx, bias_pairs):
    blk0 = N_P // DEC_SEQ
    spec = lambda part: pl.BlockSpec((None, None, DEC_SEQ, 2 * NA_HD), lambda h, b: (part, h, b + blk0, 0))
    cspec = pl.BlockSpec((None, 2, NA_HD, PAST_LEN), lambda h, b: (b, h, 0, 0))
    return pl.pallas_call(
        _na_attn_kernel,
        out_shape=jax.ShapeDtypeStruct((N_S, D), BF16),
        grid=(HEAD_PAIRS, DEC_BATCH),
        in_specs=[spec(0), spec(1), spec(2), cspec, cspec,
                  pl.BlockSpec((2, 16, GRID_W, 2 * GRID_W), lambda h, b: (h, 0, 0, 0))],
        out_specs=pl.BlockSpec((DEC_SEQ, 2 * NA_HD), lambda h, b: (b, h)),
        scratch_shapes=[pltpu.VMEM((NA_QROWS * GRID_W, NA_WIN * GRID_W), BF16),
                        pltpu.VMEM((NA_QROWS * GRID_W, PAST_LEN), BF16)],
        compiler_params=_cparams(("arbitrary", "arbitrary")),
        name="na_attn",
    )(qkv, qkv, qkv, k_ctx, v_ctx, bias_pairs)


def mixer_c(h, k_ctx, v_ctx, w_in_all, layer, rpb):
    qkv, k_new, v_new = qkv_proj(h, w_in_all, layer)
    o_p = ctx_attention(qkv)
    _, cmask2 = _na_tables()
    b15 = rpb_expand(rpb).reshape(NA_HEADS, 16, GRID_W, GRID_W)
    b17 = jnp.pad(b15, ((0, 0), (1, 0), (0, 0), (0, 0)))
    bias_pairs = jnp.concatenate([b17[:, :16], b17[:, 1:]], axis=-1) + jnp.asarray(cmask2)
    o_s = na_attention(qkv, jnp.swapaxes(k_ctx, -1, -2), jnp.swapaxes(v_ctx, -1, -2), bias_pairs)
    return (o_p, o_s), k_new, v_new


SCAN_BLOCK = HD_M


def _mlstm_kernel(*refs, seq, heads, zero_state, emit_state):
    q_ref, k_ref, v_ref, og_ref, gates_ref, wq_ref, wk_ref, gh_ref = refs[:8]
    pos = 8
    if not zero_state:
        c0_ref, n0_ref, m0_ref = refs[pos:pos + 3]
        pos += 3
    y_ref = refs[pos]
    pos += 1
    if emit_state:
        cn_ref, nn_ref, mn_ref = refs[pos:pos + 3]
        pos += 3
    pre_s, suf_s, gt_s, pret_s, suft_s, kv_s, ks_s, be_s, mk_s, cp_s, np_s, mp_s, cst, nst, mst = refs[pos:]

    blk = SCAN_BLOCK
    nc = seq // blk
    nbatch = nc * heads
    n_gates = 4 * HEADS_M
    r = lax.broadcasted_iota(jnp.int32, (seq, 1), 0)

    @pl.when(pl.program_id(1) == 0)
    def _():
        g_all = gates_ref[...]
        lf = jax.nn.log_sigmoid(g_all)
        rin = r % blk
        pre, suf = lf, lf
        for sh in [1 << i for i in range(blk.bit_length() - 1)]:
            pre = pre + jnp.where(rin >= sh, pltpu.roll(pre, sh, 0), 0.0)
            suf = suf + jnp.where(rin < blk - sh, pltpu.roll(suf, seq - sh, 0), 0.0)
        pre3, suf3 = pre.reshape(nc, blk, LANE), suf.reshape(nc, blk, LANE)
        pre_s[...] = pre3
        suf_s[...] = suf3
        gt_s[...] = jnp.swapaxes(g_all.reshape(nc, blk, LANE), 1, 2)[:, :n_gates, :]
        pret_s[...] = jnp.swapaxes(pre3, 1, 2)[:, :n_gates, :]
        suft_s[...] = jnp.swapaxes(suf3, 1, 2)[:, :n_gates, :]

    def split_heads(x):
        x3 = x.reshape(nc, blk, heads * HD_M)
        if heads == 1:
            return x3
        tiles = jnp.stack([x3[:, :, hh * HD_M:(hh + 1) * HD_M] for hh in range(heads)], axis=1)
        return tiles.reshape(nbatch, blk, HD_M)

    def per_head(x):
        if heads == 1:
            return x
        return jnp.broadcast_to(x[:, None], (nc, heads) + x.shape[1:]).reshape((nbatch,) + x.shape[1:])

    first, last = r == 0, r == seq - 1
    qc_all = _dwconv3(q_ref[...].astype(F32), wq_ref, first, last)
    q3 = split_heads(qc_all * jax.nn.sigmoid(qc_all))
    kc_all = _dwconv3(k_ref[...].astype(F32), wk_ref, first, last)
    k3 = split_heads(kc_all * jax.nn.sigmoid(kc_all) * (HD_M ** -0.5))
    qb, kb = q3.astype(BF16), k3.astype(BF16)
    vb = split_heads(v_ref[...])
    g3 = per_head(gates_ref[...].reshape(nc, blk, LANE))
    gt3 = per_head(gt_s[...])

    if zero_state:
        cst[...] = jnp.zeros_like(cst)
        nst[...] = jnp.zeros_like(nst)
        mst[...] = jnp.zeros_like(mst)
    else:
        cst[...] = c0_ref[...]
        nst[...] = n0_ref[...]
        mst[...] = jnp.broadcast_to(m0_ref[...], mst.shape)

    tt = lax.broadcasted_iota(jnp.int32, (1, blk, blk), 1)
    ss = lax.broadcasted_iota(jnp.int32, (1, blk, blk), 2)
    lane = lax.broadcasted_iota(jnp.int32, (1, 1, LANE), 2)
    sub = lax.broadcasted_iota(jnp.int32, (1, n_gates, 1), 1)
    head = lax.broadcasted_iota(jnp.int32, (nbatch, 1, 1), 0) % heads + pl.program_id(1) * heads
    hsum = None
    for d in range(2):
        i_idx = d * 2 * HEADS_M + head
        f_idx = i_idx + HEADS_M
        mask = (ss <= tt) if d == 0 else (ss >= tt)
        b3 = per_head((pre_s if d == 0 else suf_s)[...])
        bt3 = per_head((pret_s if d == 0 else suft_s)[...])
        bcol = jnp.sum(jnp.where(lane == f_idx, b3, 0.0), axis=2, keepdims=True)
        icol = jnp.sum(jnp.where(lane == i_idx, g3, 0.0), axis=2, keepdims=True)
        brow = jnp.sum(jnp.where(sub == f_idx, bt3, 0.0), axis=1, keepdims=True)
        irow = jnp.sum(jnp.where(sub == i_idx, gt3, 0.0), axis=1, keepdims=True)
        bend = bcol[:, blk - 1:blk, :] if d == 0 else bcol[:, 0:1, :]

        dmat = jnp.where(mask, bcol - brow + irow, -jnp.inf)
        mloc = jnp.max(dmat, axis=2, keepdims=True)
        qk = jnp.einsum('ctd,csd->cts', qb, kb, preferred_element_type=F32)
        s_loc = jnp.exp(dmat - mloc) * qk
        num_loc = jnp.einsum('cts,csd->ctd', s_loc.astype(BF16), vb, preferred_element_type=F32)
        den_loc = jnp.sum(s_loc, axis=2, keepdims=True)
        to_end = bend - bcol + icol
        mk = jnp.max(to_end, axis=1, keepdims=True)
        kw = k3 * jnp.exp(to_end - mk)
        kv_s[...] = jnp.einsum('cds,cse->cde', jnp.swapaxes(kw, 1, 2).astype(BF16), vb,
                               preferred_element_type=F32)
        ks_s[...] = jnp.sum(kw, axis=1, keepdims=True)
        be_s[...] = jnp.broadcast_to(bend, be_s.shape)
        mk_s[...] = jnp.broadcast_to(mk, mk_s.shape)

        def step(j, carry, d=d):
            c = j if d == 0 else nc - 1 - j
            sl = pl.ds(c * heads, heads)
            m_prev, c_prev, n_prev = mst[d], cst[d], nst[d]
            cp_s[sl] = c_prev.astype(BF16)
            np_s[sl] = n_prev
            mp_s[sl] = m_prev
            be, mkc = be_s[sl], mk_s[sl]
            m_new = jnp.maximum(be + m_prev, mkc)
            keep = jnp.exp(be + m_prev - m_new)
            add = jnp.exp(mkc - m_new)
            cst[d] = keep * c_prev + add * kv_s[sl]
            nst[d] = keep * n_prev + add * ks_s[sl]
            mst[d] = m_new
            return carry

        lax.fori_loop(0, nc, step, 0)

        m_inter = bcol + mp_s[...][:, :, 0:1]
        m_t = jnp.maximum(m_inter, mloc)
        w_state = jnp.exp(m_inter - m_t)
        w_loc = jnp.exp(mloc - m_t)
        inter = jnp.einsum('ctd,cde->cte', qb, cp_s[...], preferred_element_type=F32)
        num = w_state * inter + w_loc * num_loc
        den = w_state * jnp.sum(q3 * np_s[...], axis=2, keepdims=True) + w_loc * den_loc
        h = num / jnp.maximum(jnp.abs(den), jnp.exp(-m_t))
        hsum = h if hsum is None else hsum + h

    hn = hsum * lax.rsqrt(jnp.mean(hsum * hsum, axis=-1, keepdims=True) + EPS)
    og3 = split_heads(og_ref[...].astype(F32))
    for c in range(nc):
        for hh in range(heads):
            cols = slice(hh * HD_M, (hh + 1) * HD_M)
            e = c * heads + hh
            y_ref[c * blk:(c + 1) * blk, cols] = (hn[e] * gh_ref[:, cols] * jax.nn.sigmoid(og3[e])).astype(y_ref.dtype)
    if emit_state:
        cn_ref[...] = cst[...]
        nn_ref[...] = nst[...]
        mn_ref[...] = mst[...]


def mlstm(proj, gates, w_conv_qk, g_head, state, *, nb, seq, row0, heads):
    blk0 = row0 // seq
    nblk = seq // SCAN_BLOCK
    groups = HEADS_M // heads
    n_gates = 4 * HEADS_M
    zero_state = state is None
    col = lambda part: pl.BlockSpec((seq, heads * HD_M), lambda b, h: (b + blk0, part * groups + h))
    in_specs = [col(0), col(1), col(2), col(3),
                pl.BlockSpec((seq, LANE), lambda b, h: (b + blk0, 0)),
                pl.BlockSpec((3, heads * HD_M), lambda b, h: (0, h)),
                pl.BlockSpec((3, heads * HD_M), lambda b, h: (0, groups + h)),
                pl.BlockSpec((1, heads * HD_M), lambda b, h: (0, h))]
    args = [proj, proj, proj, proj, gates, w_conv_qk, w_conv_qk, g_head.reshape(1, MW)]
    state_specs = [pl.BlockSpec((None, 2, heads, HD_M, HD_M), lambda b, h: (b, 0, h, 0, 0)),
                   pl.BlockSpec((None, 2, heads, 1, HD_M), lambda b, h: (b, 0, h, 0, 0))]
    if not zero_state:
        c0, n0, m0 = state
        in_specs += state_specs + [pl.BlockSpec((None, 2, heads, 1, 1), lambda b, h: (b, 0, h, 0, 0))]
        args += [c0, n0.reshape(nb, 2, HEADS_M, 1, HD_M), m0.reshape(nb, 2, HEADS_M, 1, 1)]
    out_shape = [jax.ShapeDtypeStruct((nb * seq, MW), BF16)]
    out_specs = [pl.BlockSpec((seq, heads * HD_M), lambda b, h: (b, h))]
    if zero_state:
        out_shape += [jax.ShapeDtypeStruct((nb, 2, HEADS_M, HD_M, HD_M), F32),
                      jax.ShapeDtypeStruct((nb, 2, HEADS_M, 1, HD_M), F32),
                      jax.ShapeDtypeStruct((nb, 2, HEADS_M, 1, LANE), F32)]
        out_specs += state_specs + [pl.BlockSpec((None, 2, heads, 1, LANE), lambda b, h: (b, 0, h, 0, 0))]
    per_block = lambda *shape: pltpu.VMEM((nblk,) + shape, F32)
    per_entry = lambda *shape, dtype=F32: pltpu.VMEM((nblk * heads,) + shape, dtype)
    return pl.pallas_call(
        functools.partial(_mlstm_kernel, seq=seq, heads=heads, zero_state=zero_state, emit_state=zero_state),
        out_shape=out_shape,
        grid=(nb, groups),
        in_specs=in_specs,
        out_specs=out_specs,
        scratch_shapes=[per_block(SCAN_BLOCK, LANE), per_block(SCAN_BLOCK, LANE),
                        per_block(n_gates, SCAN_BLOCK), per_block(n_gates, SCAN_BLOCK), per_block(n_gates, SCAN_BLOCK),
                        per_entry(HD_M, HD_M), per_entry(1, HD_M), per_entry(1, LANE), per_entry(1, LANE),
                        per_entry(HD_M, HD_M, dtype=BF16), per_entry(1, HD_M), per_entry(1, LANE),
                        pltpu.VMEM((2, heads, HD_M, HD_M), F32), pltpu.VMEM((2, heads, 1, HD_M), F32),
                        pltpu.VMEM((2, heads, 1, LANE), F32)],
        compiler_params=_cparams(("arbitrary", "arbitrary")),
        name="mlstm_%d" % seq,
    )(*args)


def _hyena_pre_kernel(v_ref, x1_ref, x2_ref, wv_ref, w1_ref, w2_ref, u_ref, x2c_ref):
    first, last = _seq_edges(SEQ_BLOCK, pl.program_id(0) < N_PBLK)
    x1c = _dwconv3(x1_ref[...].astype(F32), w1_ref, first, last)
    u_ref[...] = (x1c * _dwconv3(v_ref[...].astype(F32), wv_ref, first, last)).astype(u_ref.dtype)
    x2c_ref[...] = _dwconv3(x2_ref[...].astype(F32), w2_ref, first, last).astype(x2c_ref.dtype)


def hyena_pre(proj, w_conv_hy):
    tc = 256
    nct = HW // tc
    c0 = 4 * MW // tc
    pcol = lambda part: pl.BlockSpec((SEQ_BLOCK, tc), lambda i, j: (i, c0 + part * nct + j))
    wcol = lambda part: pl.BlockSpec((3, tc), lambda i, j: (0, part * nct + j))
    out = jax.ShapeDtypeStruct((N_TOK, HW), BF16)
    ospec = pl.BlockSpec((SEQ_BLOCK, tc), lambda i, j: (i, j))
    return pl.pallas_call(
        _hyena_pre_kernel,
        out_shape=[out, out],
        grid=(N_TOK // SEQ_BLOCK, nct),
        in_specs=[pcol(0), pcol(1), pcol(2), wcol(0), wcol(1), wcol(2)],
        out_specs=[ospec, ospec],
        compiler_params=_cparams(("arbitrary", "arbitrary")),
        name="hyena_pre",
    )(proj, proj, proj, w_conv_hy, w_conv_hy, w_conv_hy)


@functools.lru_cache(maxsize=None)
def _filter_tables(seq):
    t = np.linspace(0.0, 1.0, seq)[:, None]
    wpos = 2.0 * np.pi * np.arange(seq)[:, None] / seq
    bands = np.linspace(1e-4, N_BANDS - 1, N_BANDS)[None, :]
    z = np.concatenate([t, np.cos(bands * wpos), -np.sin(bands * wpos)], axis=-1)
    z = np.pad(z, ((0, 0), (0, LANE - FILTER_EMB)))
    max_decay = math.log(DECAY_TARGET) / DECAY_FAST
    min_decay = math.log(DECAY_TARGET) / DECAY_SLOW
    deltas = np.abs(np.linspace(min_decay, max_decay, HW))
    decay = np.exp(-t * np.concatenate([deltas, deltas])[None, :])
    return z.astype(np.float32), decay.astype(np.float32)


@functools.lru_cache(maxsize=None)
def _dft_tables(seq, tk):
    n = 2 * seq
    k = np.arange(seq)[:, None]
    t = np.arange(seq)[None, :]
    ang = 2.0 * np.pi * ((k * t) % n) / n
    alt = np.where(np.arange(seq) % 2 == 0, 1.0, -1.0)
    cm, sm = np.cos(ang), np.sin(ang)
    sm[0, :] = alt
    fwd = np.stack([cm.reshape(seq // tk, tk, seq), sm.reshape(seq // tk, tk, seq)], axis=1)
    wk = np.where(np.arange(seq) == 0, 1.0, 2.0)[None, :]
    ci = (np.cos(ang.T) * wk) / n
    si = np.sin(ang.T) * 2.0 / n
    si[:, 0] = alt / n
    inv = np.concatenate([ci, si], axis=1)
    return fwd.astype(np.float32), inv.astype(np.float32)


def _filter_kernel(z_ref, w1_ref, b1_ref, w2_ref, b2_ref, w3_ref, fr_ref, dec_ref, hs_ref, hd_ref):
    fr = fr_ref[...]
    hp = functools.partial(jnp.dot, precision=HIGHEST, preferred_element_type=F32)
    h1 = jnp.sin(fr * (hp(z_ref[...], w1_ref[...]) + b1_ref[...]))
    h2 = jnp.sin(fr * (hp(h1, w2_ref[...]) + b2_ref[...]))
    w3 = w3_ref[...]
    h_hi, w_hi = h2.astype(BF16), w3.astype(BF16)
    h_lo, w_lo = (h2 - h_hi.astype(F32)).astype(BF16), (w3 - w_hi.astype(F32)).astype(BF16)
    filt = (_dot(h_hi, w_hi) + _dot(h_hi, w_lo) + _dot(h_lo, w_hi)) * dec_ref[...]
    past, fut = filt[:, :HW], filt[:, HW:]
    rows = filt.shape[0]
    grow = lax.broadcasted_iota(jnp.int32, (rows, 1), 0) + pl.program_id(0) * rows
    fut = jnp.where(grow == 0, 0.0, fut)
    hs_ref[...] = past + fut
    hd_ref[...] = past - fut


def filter_gen(seq, w1, b1, w2, b2, w3, freq):
    z, decay = _filter_tables(seq)
    tl = 256
    fh = FILTER_HIDDEN
    full = lambda shape: pl.BlockSpec(shape, lambda i: (0, 0))
    out = jax.ShapeDtypeStruct((seq, HW), F32)
    return pl.pallas_call(
        _filter_kernel,
        out_shape=[out, out],
        grid=(seq // tl,),
        in_specs=[pl.BlockSpec((tl, LANE), lambda i: (i, 0)), full((LANE, fh)), full((1, fh)), full((fh, fh)),
                  full((1, fh)), full((fh, 2 * HW)), full((1, fh)), pl.BlockSpec((tl, 2 * HW), lambda i: (i, 0))],
        out_specs=[pl.BlockSpec((tl, HW), lambda i: (i, 0))] * 2,
        compiler_params=_cparams(("arbitrary",)),
        name="filter_gen",
    )(jnp.asarray(z), jnp.pad(w1, ((0, LANE - FILTER_EMB), (0, 0))), b1.reshape(1, fh), w2, b2.reshape(1, fh), w3,
      freq.reshape(1, fh), jnp.asarray(decay))


def _dft_filter_kernel(a_ref, hs_ref, hd_ref, k_ref, hs_bf, hd_bf):
    @pl.when(pl.program_id(0) == 0)
    def _():
        hs_bf[...] = hs_ref[...].astype(BF16)
        hd_bf[...] = hd_ref[...].astype(BF16)

    k_ref[0] = _dot(a_ref[0].astype(BF16), hs_bf[...])
    k_ref[1] = _dot(a_ref[1].astype(BF16), hd_bf[...])

    @pl.when(pl.program_id(0) == 0)
    def _():
        k_ref[1, 0:1, :] = _dot(a_ref[1, 0:8, :].astype(BF16), hs_bf[...])[0:1, :]


def dft_filter(seq, tk, hs, hd):
    fwd, _ = _dft_tables(seq, tk)
    return pl.pallas_call(
        _dft_filter_kernel,
        out_shape=jax.ShapeDtypeStruct((2, seq, HW), F32),
        grid=(seq // tk,),
        in_specs=[pl.BlockSpec((None, 2, tk, seq), lambda m: (m, 0, 0, 0)),
                  pl.BlockSpec((seq, HW), lambda m: (0, 0)), pl.BlockSpec((seq, HW), lambda m: (0, 0))],
        out_specs=pl.BlockSpec((2, tk, HW), lambda m: (0, m, 0)),
        scratch_shapes=[pltpu.VMEM((seq, HW), BF16), pltpu.VMEM((seq, HW), BF16)],
        compiler_params=_cparams(("arbitrary",)),
        name="dft_filter",
    )(jnp.asarray(fwd), hs, hd)


def _dft_fwd_kernel(a_ref, u_ref, k_ref, y_ref, u_bf, tbl):
    m = pl.program_id(1)

    @pl.when(m == 0)
    def _():
        u_bf[...] = u_ref[...].astype(BF16)

    @pl.when(pl.program_id(0) == 0)
    def _():
        tbl[m] = a_ref[...].astype(BF16)

    ure = _dot(tbl[m, 0], u_bf[...])
    uim = _dot(tbl[m, 1], u_bf[...])
    kre, kim = k_ref[0], k_ref[1]
    packed = (lax.broadcasted_iota(jnp.int32, (ure.shape[0], 1), 0) == 0) & (m == 0)
    y_ref[0] = jnp.where(packed, ure * kre, ure * kre - uim * kim).astype(y_ref.dtype)
    y_ref[1] = jnp.where(packed, uim * kim, ure * kim + uim * kre).astype(y_ref.dtype)


def dft_fwd(seq, tk, u, kf, *, nb, row0):
    fwd, _ = _dft_tables(seq, tk)
    blk0 = row0 // seq
    nm = seq // tk
    return pl.pallas_call(
        _dft_fwd_kernel,
        out_shape=jax.ShapeDtypeStruct((nb, 2, seq, HW), BF16),
        grid=(nb, nm),
        in_specs=[pl.BlockSpec((None, 2, tk, seq), lambda b, m: (jnp.where(b == 0, m, nm - 1), 0, 0, 0)),
                  pl.BlockSpec((seq, HW), lambda b, m: (b + blk0, 0)),
                  pl.BlockSpec((2, tk, HW), lambda b, m: (0, m, 0))],
        out_specs=pl.BlockSpec((None, 2, tk, HW), lambda b, m: (b, 0, m, 0)),
        scratch_shapes=[pltpu.VMEM((seq, HW), BF16), pltpu.VMEM((nm, 2, tk, seq), BF16)],
        compiler_params=_cparams(("arbitrary", "arbitrary"), DFT_VMEM_LIMIT),
        name="dft_fwd",
    )(jnp.asarray(fwd), u, kf)


def _dft_inv_kernel(a_ref, y_ref, u_ref, x2_ref, bias_ref, o_ref, tbl, *, seq):
    t = pl.program_id(1)

    @pl.when(pl.program_id(0) == 0)
    def _():
        tbl[t] = a_ref[...].astype(BF16)

    conv = _dot(tbl[t, :, :seq], y_ref[0]) + _dot(tbl[t, :, seq:], y_ref[1])
    u = u_ref[...].astype(F32)
    o_ref[...] = (x2_ref[...].astype(F32) * (conv + bias_ref[...] * u)).astype(o_ref.dtype)


def dft_inv(seq, tk, y, u, x2c, bias, *, nb, row0):
    _, inv = _dft_tables(seq, tk)
    tm = min(seq, 512)
    nt = seq // tm
    blk0 = row0 // tm
    rows = lambda b, t: (b * nt + t + blk0, 0)
    return pl.pallas_call(
        functools.partial(_dft_inv_kernel, seq=seq),
        out_shape=jax.ShapeDtypeStruct((nb * seq, HW), BF16),
        grid=(nb, nt),
        in_specs=[pl.BlockSpec((tm, 2 * seq), lambda b, t: (jnp.where(b == 0, t, nt - 1), 0)),
                  pl.BlockSpec((None, 2, seq, HW), lambda b, t: (b, 0, 0, 0)),
                  pl.BlockSpec((tm, HW), rows), pl.BlockSpec((tm, HW), rows),
                  pl.BlockSpec((1, HW), lambda b, t: (0, 0))],
        out_specs=pl.BlockSpec((tm, HW), lambda b, t: (b * nt + t, 0)),
        scratch_shapes=[pltpu.VMEM((nt, tm, 2 * seq), BF16)],
        compiler_params=_cparams(("arbitrary", "arbitrary"), DFT_VMEM_LIMIT),
        name="dft_inv",
    )(jnp.asarray(inv), y, u, x2c, bias.reshape(1, HW))


def hyena_group(seq, u, x2c, filt_w, bias, *, nb, row0):
    tk = min(seq, 512)
    hs, hd = filter_gen(seq, *filt_w)
    kf = dft_filter(seq, tk, hs, hd)
    y = dft_fwd(seq, tk, u, kf, nb=nb, row0=row0)
    return dft_inv(seq, tk, y, u, x2c, bias, nb=nb, row0=row0)


def mixer_ab_parts(proj, gates, state_s, w_conv_qk, g_head, w_conv_hy, w_f1, b_f1, w_f2, b_f2, w_f3, freq, hy_bias):
    ym_p, c_p, n_p, m_p = mlstm(proj, gates, w_conv_qk, g_head, None, nb=BATCH, seq=SEQ, row0=0, heads=HEADS_M)
    (ym_s,) = mlstm(proj, gates, w_conv_qk, g_head, state_s, nb=DEC_BATCH, seq=DEC_SEQ, row0=N_P, heads=1)
    u, x2c = hyena_pre(proj, w_conv_hy)
    filt_w = (w_f1, b_f1, w_f2, b_f2, w_f3, freq)
    yh_p = hyena_group(SEQ, u, x2c, filt_w, hy_bias, nb=BATCH, row0=0)
    yh_s = hyena_group(DEC_SEQ, u, x2c, filt_w, hy_bias, nb=DEC_BATCH, row0=N_P)
    state_p = (c_p, n_p[:, :, :, 0, :], m_p[:, :, :, 0, 0])
    return (ym_p, ym_s), (yh_p, yh_s), state_p


def kernel(x_prompt, x_sample, state_mlstm_C, state_mlstm_n, state_mlstm_m, cache_na_k, cache_na_v, c, c_ctx, w_ada, b_ada, g_mix, g_ffn, g_final, w_in_ab, b_gates, w_conv_qk, g_mlstm, w_conv_hy, w_filt1, b_filt1, w_filt2, b_filt2, w_filt3, filt_freq, hyena_bias, w_out_ab, w_in_c, rpb_c, w_out_c, w_up, w_conv_ffn, w_down):
    cmat = jnp.concatenate([c, c_ctx[None, :], jnp.zeros((MOD_ROWS - DEC_BATCH - 1, D), F32)], axis=0)
    mod_all = adaln_all(cmat, w_ada, b_ada).reshape(DEPTH, MOD_ROWS, 6, 1, D)
    x = (x_prompt.reshape(N_P, D), x_sample.reshape(N_S, D))
    h = None
    new_c, new_n, new_m, new_k, new_v = [], [], [], [], []
    for l in range(DEPTH):
        e = l // 2
        if l % 2 == 0:
            assert l == 0, "a later mixer A/B layer would take the fused norm of the layer before it"
            proj, gates = ab_in(x, g_mix, mod_all, l, jnp.swapaxes(w_in_ab, 1, 2), b_gates[e], e)
            state_s = (state_mlstm_C[:, e], state_mlstm_n[:, e], state_mlstm_m[:, e])
            y_m, y_h, (c_p, n_p, m_p) = mixer_ab_parts(
                proj, gates, state_s, w_conv_qk[e], g_mlstm[e], w_conv_hy[e], w_filt1[e], b_filt1[e],
                w_filt2[e], b_filt2[e], w_filt3[e], filt_freq[e], hyena_bias[e])
            a_list = [y_m, y_h]
            w_list = [(w_out_ab, (None, MW, D), lambda i, e=e: (e, 0, 0)),
                      (w_out_ab, (None, HW, D), lambda i, e=e: (e, 1, 0))]
            new_c.append(c_p[:, None])
            new_n.append(n_p[:, None])
            new_m.append(m_p[:, None])
        else:
            o, k_new, v_new = mixer_c(h, cache_na_k[:, e], cache_na_v[:, e], w_in_c, e, rpb_c[e])
            a_list = [o]
            w_list = [(w_out_c, (None, D, D), lambda i, e=e: (e, 0, 0))]
            new_k.append(k_new)
            new_v.append(v_new)
        x, h = out_proj(a_list, w_list, x, mod_all, l, 2, g_ffn, l, (l, 3, 4), name="mixer_out")
        mid = ffn_up(h, w_up, w_conv_ffn, l)
        w_list = [(w_down, (None, FF, D), lambda i, l=l: (l, 0, 0))]
        if l + 1 < DEPTH:
            x, h = out_proj([mid], w_list, x, mod_all, l, 5, g_mix, l + 1, (l + 1, 0, 1), name="ffn_down")
        else:
            y_p, y_s = out_proj([mid], w_list, x, mod_all, l, 5, g_final, 0, None, name="ffn_down_final")
    cat = lambda parts: parts[0] if len(parts) == 1 else jnp.concatenate(parts, axis=1)
    return (y_p.reshape(BATCH, SEQ, D), y_s.reshape(DEC_BATCH, DEC_SEQ, D), cat(new_c), cat(new_n), cat(new_m),
            cat(new_k), cat(new_v))
```

```python
import functools
import math

import numpy as np
import jax
import jax.numpy as jnp
from jax import lax
from jax.experimental import pallas as pl
from jax.experimental.pallas import tpu as pltpu

F32 = jnp.float32
BF16 = jnp.bfloat16

D = 1024
BATCH, SEQ = 16, 256
DEC_BATCH, DEC_SEQ = 4, 2048
PAST_LEN = 512
DEPTH = 2
GRID_W = 64
GRID_R = DEC_SEQ // GRID_W
HEADS_M = 4
MW = D // 2
HD_M = MW // HEADS_M
HW = D // 2
N_BANDS = 16
FILTER_EMB = 2 * N_BANDS + 1
FILTER_HIDDEN = 64
DECAY_FAST, DECAY_SLOW, DECAY_TARGET = 0.3, 1.5, 1e-2
NA_HEADS = 16
NA_HD = D // NA_HEADS
NA_KH, NA_KW = 8, 16
FF = 2816
EPS = 1e-6

N_P = BATCH * SEQ
N_S = DEC_BATCH * DEC_SEQ
N_TOK = N_P + N_S
CTX_ROW = DEC_BATCH
MOD_ROWS = 8
LANE = 128
VMEM_LIMIT = 48 * 1024 * 1024
QKV_VMEM_LIMIT = 56 * 1024 * 1024
DFT_VMEM_LIMIT = 56 * 1024 * 1024
HIGHEST = lax.Precision.HIGHEST


def _cparams(sem, vmem_limit=VMEM_LIMIT):
    return pltpu.CompilerParams(dimension_semantics=sem, vmem_limit_bytes=vmem_limit)


def _mod_row(i, bm):
    return jnp.where(i < N_P // bm, CTX_ROW, (i - N_P // bm) // (DEC_SEQ // bm))


def _dot(a, b):
    return jnp.dot(a, b, preferred_element_type=F32)


def _adaln_kernel(c_ref, w_ref, b_ref, o_ref):
    cv = c_ref[...]
    s = cv * jax.nn.sigmoid(cv)
    o_ref[...] = _dot(s.astype(BF16), w_ref[...].astype(BF16)) + b_ref[...]


def adaln_all(cmat, w_ada, b_ada):
    tn = 1024
    return pl.pallas_call(
        _adaln_kernel,
        out_shape=jax.ShapeDtypeStruct((DEPTH, MOD_ROWS, 6 * D), F32),
        grid=(DEPTH, 6 * D // tn),
        in_specs=[
            pl.BlockSpec((MOD_ROWS, D), lambda l, j: (0, 0)),
            pl.BlockSpec((None, D, tn), lambda l, j: (l, 0, j)),
            pl.BlockSpec((None, 1, tn), lambda l, j: (l, 0, j)),
        ],
        out_specs=pl.BlockSpec((None, MOD_ROWS, tn), lambda l, j: (l, 0, j)),
        compiler_params=_cparams(("arbitrary", "arbitrary")),
        name="adaln",
    )(cmat, w_ada, b_ada.reshape(DEPTH, 1, 6 * D))


ROW_BM = 512
ROW_NPB = N_P // ROW_BM


def _row_specs(arr, cols, bm=ROW_BM):
    npb = N_P // bm
    if isinstance(arr, tuple):
        return ([pl.BlockSpec((bm, cols), lambda i: (jnp.minimum(i, npb - 1), 0)),
                 pl.BlockSpec((bm, cols), lambda i: (jnp.maximum(i - npb, 0), 0))], list(arr))
    return [pl.BlockSpec((bm, cols), lambda i: (i, 0))], [arr]


def _pick_rows(refs, npb=ROW_NPB):
    if len(refs) == 1:
        return refs[0][...]
    return jnp.where(pl.program_id(0) < npb, refs[0][...], refs[1][...])


def _mod_spec(layer, which, bm=ROW_BM):
    return pl.BlockSpec((None, None, None, 1, D), lambda i: (layer, _mod_row(i, bm), which, 0, 0))


def _layer_vec_spec(layer):
    return pl.BlockSpec((None, 1, D), lambda i: (layer, 0, 0))


def _rms_mod(x, g_ref, sh_ref, sc_ref):
    y = x * lax.rsqrt(jnp.mean(x * x, axis=-1, keepdims=True) + EPS)
    return (y * g_ref[...]) * (1.0 + sc_ref[...]) + sh_ref[...]


def _out_proj_kernel(*refs, a_counts, n_x, final, npb):
    pos = 0
    a_groups = []
    for cnt in a_counts:
        a_groups.append(refs[pos:pos + cnt])
        pos += cnt
    w_refs = refs[pos:pos + len(a_counts)]
    pos += len(a_counts)
    x_refs = refs[pos:pos + n_x]
    pos += n_x
    gt_ref, g_ref = refs[pos:pos + 2]
    pos += 2
    if not final:
        sh_ref, sc_ref = refs[pos:pos + 2]
        pos += 2
    out_a, out_b = refs[pos:pos + 2]
    wbf = refs[pos + 2:]

    @pl.when(pl.program_id(0) == 0)
    def _():
        for w_ref, wb in zip(w_refs, wbf):
            wb[...] = w_ref[...].astype(BF16)

    acc = None
    for group, wb in zip(a_groups, wbf):
        part = _dot(_pick_rows(group, npb), wb[...])
        acc = part if acc is None else acc + part
    xn = _pick_rows(x_refs, npb) + gt_ref[...] * acc
    if final:
        y = xn * lax.rsqrt(jnp.mean(xn * xn, axis=-1, keepdims=True) + EPS) * g_ref[...]

        @pl.when(pl.program_id(0) < npb)
        def _():
            out_a[...] = y

        @pl.when(pl.program_id(0) >= npb)
        def _():
            out_b[...] = y
    else:
        out_a[...] = xn
        out_b[...] = _rms_mod(xn, g_ref, sh_ref, sc_ref).astype(out_b.dtype)


def out_proj(a_list, w_list, x, mod_all, layer, gt_idx, g_all, g_layer, norm_mod_idx=None, bm=ROW_BM,
             name="out_proj"):
    final = norm_mod_idx is None
    npb = N_P // bm
    in_specs, args, a_counts = [], [], []
    for a in a_list:
        cols = (a[0] if isinstance(a, tuple) else a).shape[1]
        specs, ops = _row_specs(a, cols, bm)
        in_specs += specs
        args += ops
        a_counts.append(len(ops))
    w_shapes = []
    for w, block, imap in w_list:
        in_specs.append(pl.BlockSpec(block, imap, pipeline_mode=pl.Buffered(1)))
        args.append(w)
        w_shapes.append(tuple(b for b in block if b is not None))
    x_specs, x_args = _row_specs(x, D, bm)
    in_specs += x_specs + [_mod_spec(layer, gt_idx, bm), _layer_vec_spec(g_layer)]
    args += x_args + [mod_all, g_all.reshape(-1, 1, D)]
    if final:
        out_shape = [jax.ShapeDtypeStruct((N_P, D), F32), jax.ShapeDtypeStruct((N_S, D), F32)]
        out_specs = [pl.BlockSpec((bm, D), lambda i: (jnp.minimum(i, npb - 1), 0)),
                     pl.BlockSpec((bm, D), lambda i: (jnp.maximum(i - npb, 0), 0))]
    else:
        n_layer, sh_idx, sc_idx = norm_mod_idx
        in_specs += [_mod_spec(n_layer, sh_idx, bm), _mod_spec(n_layer, sc_idx, bm)]
        args += [mod_all, mod_all]
        out_shape = [jax.ShapeDtypeStruct((N_TOK, D), F32), jax.ShapeDtypeStruct((N_TOK, D), BF16)]
        out_specs = [pl.BlockSpec((bm, D), lambda i: (i, 0))] * 2
    return pl.pallas_call(
        functools.partial(_out_proj_kernel, a_counts=tuple(a_counts), n_x=len(x_args), final=final, npb=npb),
        out_shape=out_shape,
        grid=(N_TOK // bm,),
        in_specs=in_specs,
        out_specs=out_specs,
        scratch_shapes=[pltpu.VMEM(s, BF16) for s in w_shapes],
        compiler_params=_cparams(("arbitrary",)),
        name=name,
    )(*args)


def _qkv_kernel(a_ref, w_ref, o_ref, kc_ref, vc_ref, wbf):
    @pl.when(pl.program_id(0) == 0)
    def _():
        wbf[...] = w_ref[...].astype(BF16)

    a = a_ref[...]
    for part in range(3):
        acc = _dot(a, wbf[:, part * D:(part + 1) * D])
        for pp in range(NA_HEADS // 2):
            o_ref[part, pp] = acc[:, pp * 2 * NA_HD:(pp + 1) * 2 * NA_HD].astype(o_ref.dtype)
        if part > 0:
            c_ref = kc_ref if part == 1 else vc_ref

            @pl.when(pl.program_id(0) < ROW_NPB)
            def _(acc=acc, c_ref=c_ref):
                for b in range(ROW_BM // SEQ):
                    for hh in range(NA_HEADS):
                        c_ref[b, hh] = acc[b * SEQ:(b + 1) * SEQ, hh * NA_HD:(hh + 1) * NA_HD]


def qkv_proj(h, w_in, layer):
    seqs = ROW_BM // SEQ
    cache = jax.ShapeDtypeStruct((BATCH, 1, NA_HEADS, SEQ, NA_HD), F32)
    cache_spec = pl.BlockSpec((seqs, None, NA_HEADS, SEQ, NA_HD),
                              lambda i: (jnp.minimum(i, ROW_NPB - 1), 0, 0, 0, 0))
    return pl.pallas_call(
        _qkv_kernel,
        out_shape=[jax.ShapeDtypeStruct((3, NA_HEADS // 2, N_TOK, 2 * NA_HD), BF16), cache, cache],
        grid=(N_TOK // ROW_BM,),
        in_specs=[pl.BlockSpec((ROW_BM, D), lambda i: (i, 0)),
                  pl.BlockSpec((None, D, 3 * D), lambda i: (layer, 0, 0), pipeline_mode=pl.Buffered(1))],
        out_specs=[pl.BlockSpec((3, NA_HEADS // 2, ROW_BM, 2 * NA_HD), lambda i: (0, 0, i, 0)),
                   cache_spec, cache_spec],
        scratch_shapes=[pltpu.VMEM((D, 3 * D), BF16)],
        compiler_params=_cparams(("arbitrary",), QKV_VMEM_LIMIT),
        name="qkv_proj",
    )(h, w_in)


AB_MAIN = 4 * MW + 3 * HW
N_GATES = 4 * HEADS_M


def _ab_in_kernel(*refs):
    x_refs = refs[:-8]
    g_ref, sh_ref, sc_ref, w_ref, b_ref, proj_ref, gates_ref, wbf = refs[-8:]

    @pl.when(pl.program_id(0) == 0)
    def _():
        wbf[:4 * MW, :] = w_ref[:4 * MW, :].astype(BF16)
        wbf[4 * MW:AB_MAIN, :] = w_ref[4 * MW + N_GATES:, :].astype(BF16)
        wbf[AB_MAIN:AB_MAIN + N_GATES, :] = w_ref[4 * MW:4 * MW + N_GATES, :].astype(BF16)
        wbf[AB_MAIN + N_GATES:, :] = jnp.zeros((LANE - N_GATES, D), BF16)

    h = _rms_mod(_pick_rows(x_refs), g_ref, sh_ref, sc_ref).astype(BF16)
    proj_ref[...] = lax.dot_general(h, wbf[:AB_MAIN, :], NT_DIMS, preferred_element_type=F32).astype(proj_ref.dtype)
    gates_ref[...] = lax.dot_general(h, wbf[AB_MAIN:, :], NT_DIMS, preferred_element_type=F32) + b_ref[...]


def ab_in(x, g_all, mod_all, layer, w_in_t, b_gates, e):
    x_specs, x_args = _row_specs(x, D)
    b_pad = jnp.pad(b_gates, (0, LANE - N_GATES)).reshape(1, LANE)
    return pl.pallas_call(
        _ab_in_kernel,
        out_shape=[jax.ShapeDtypeStruct((N_TOK, AB_MAIN), BF16), jax.ShapeDtypeStruct((N_TOK, LANE), F32)],
        grid=(N_TOK // ROW_BM,),
        in_specs=x_specs + [_layer_vec_spec(layer), _mod_spec(layer, 0), _mod_spec(layer, 1),
                            pl.BlockSpec((None, w_in_t.shape[1], D), lambda i: (e, 0, 0),
                                         pipeline_mode=pl.Buffered(1)),
                            pl.BlockSpec((1, LANE), lambda i: (0, 0))],
        out_specs=[pl.BlockSpec((ROW_BM, AB_MAIN), lambda i: (i, 0)), pl.BlockSpec((ROW_BM, LANE), lambda i: (i, 0))],
        scratch_shapes=[pltpu.VMEM((AB_MAIN + LANE, D), BF16)],
        compiler_params=_cparams(("arbitrary",)),
        name="ab_in",
    )(*x_args, g_all.reshape(-1, 1, D), mod_all, mod_all, w_in_t, b_pad)


SEQ_BLOCK = DEC_SEQ
N_PBLK = N_P // SEQ_BLOCK


def _seq_edges(rows, is_prompt):
    r = lax.broadcasted_iota(jnp.int32, (rows, 1), 0)
    first = (r == 0) | (is_prompt & (r % SEQ == 0))
    last = (r == rows - 1) | (is_prompt & (r % SEQ == SEQ - 1))
    return first, last


def _dwconv3(x, w_ref, first, last):
    rows = x.shape[0]
    prev = jnp.where(first, 0.0, pltpu.roll(x, 1, 0))
    nxt = jnp.where(last, 0.0, pltpu.roll(x, rows - 1, 0))
    return prev * w_ref[0:1, :] + x * w_ref[1:2, :] + nxt * w_ref[2:3, :]


def _gated_gelu_of_half(y, g):
    c = math.sqrt(2.0 / math.pi)
    t = jnp.tanh(y * (2.0 * c + (8.0 * 0.044715 * c) * (y * y)))
    return (y + y * t) * g


FFN_BLOCK = 2 * DEC_SEQ
FFN_CHUNKS = 8
FFN_HALO = 8


def _ffn_up_kernel(h_ref, wa_ref, wg_ref, wc_ref, o_ref):
    is_prompt = pl.program_id(0) < N_P // FFN_BLOCK
    rows = FFN_BLOCK // FFN_CHUNKS
    wa = wa_ref[...].astype(BF16)
    wg = wg_ref[...].astype(BF16)
    wc_half = 0.5 * wc_ref[...]
    zeros = jnp.zeros((FFN_HALO, o_ref.shape[1]), F32)

    def matmuls(r):
        hr = h_ref[r * rows:(r + 1) * rows, :]
        return _dot(hr, wa), _dot(hr, wg)

    def activation(r, a_prev, a_cur, a_next, g):
        win = jnp.concatenate([zeros if a_prev is None else a_prev[rows - FFN_HALO:], a_cur,
                               zeros if a_next is None else a_next[:FFN_HALO]], axis=0)
        ridx = lax.broadcasted_iota(jnp.int32, (rows + 2 * FFN_HALO, 1), 0) + (r * rows - FFN_HALO)
        first = (ridx % DEC_SEQ == 0) | (is_prompt & (ridx % SEQ == 0))
        last = (ridx % DEC_SEQ == DEC_SEQ - 1) | (is_prompt & (ridx % SEQ == SEQ - 1))
        half_conv = _dwconv3(win, wc_half, first, last)[FFN_HALO:FFN_HALO + rows]
        o_ref[r * rows:(r + 1) * rows, :] = _gated_gelu_of_half(half_conv, g).astype(o_ref.dtype)

    acts = [matmuls(0)]
    for r in range(1, FFN_CHUNKS):
        acts.append(matmuls(r))
        activation(r - 1, acts[r - 2][0] if r >= 2 else None, acts[r - 1][0], acts[r][0], acts[r - 1][1])
    activation(FFN_CHUNKS - 1, acts[-2][0], acts[-1][0], None, acts[-1][1])


def ffn_up(h, w_up, w_conv, layer):
    tc = 256
    nct = FF // tc
    return pl.pallas_call(
        _ffn_up_kernel,
        out_shape=jax.ShapeDtypeStruct((N_TOK, FF), BF16),
        grid=(N_TOK // FFN_BLOCK, nct),
        in_specs=[
            pl.BlockSpec((FFN_BLOCK, D), lambda i, j: (i, 0)),
            pl.BlockSpec((None, D, tc), lambda i, j: (layer, 0, j)),
            pl.BlockSpec((None, D, tc), lambda i, j: (layer, 0, j + nct)),
            pl.BlockSpec((None, 3, tc), lambda i, j: (layer, 0, j)),
        ],
        out_specs=pl.BlockSpec((FFN_BLOCK, tc), lambda i, j: (i, j)),
        compiler_params=_cparams(("arbitrary", "arbitrary")),
        name="ffn_up",
    )(h, w_up, w_up, w_conv)


HEAD_PAIRS = NA_HEADS // 2
NT_DIMS = (((1,), (1,)), ((), ()))


def _pair_mask(shape):
    return lax.broadcasted_iota(jnp.int32, shape, len(shape) - 1) < NA_HD


def _one_head(x2, first):
    keep = _pair_mask(x2.shape) if first else ~_pair_mask(x2.shape)
    return jnp.where(keep, x2, jnp.zeros_like(x2))


def _ctx_attn_kernel(q_ref, k_ref, v_ref, o_ref, *, pairs):
    outs = []
    for pp in range(pairs):
        q2, k2, v2 = q_ref[pp], k_ref[pp], v_ref[pp]
        res = []
        for first in (True, False):
            s = lax.dot_general(_one_head(q2, first), k2, NT_DIMS, preferred_element_type=F32) * (NA_HD ** -0.5)
            m = jnp.max(s, axis=-1, keepdims=True)
            p = jnp.exp(s - m)
            l = jnp.sum(p, axis=-1, keepdims=True)
            res.append(_dot(p.astype(BF16), v2) / l)
        outs.append(jnp.where(_pair_mask(res[0].shape), res[0], res[1]))
    o_ref[...] = jnp.concatenate(outs, axis=-1).astype(o_ref.dtype)


def ctx_attention(qkv):
    pairs = HEAD_PAIRS
    spec = lambda part: pl.BlockSpec((None, pairs, SEQ, 2 * NA_HD), lambda b, h: (part, h, b, 0))
    return pl.pallas_call(
        functools.partial(_ctx_attn_kernel, pairs=pairs),
        out_shape=jax.ShapeDtypeStruct((N_P, D), BF16),
        grid=(BATCH, HEAD_PAIRS // pairs),
        in_specs=[spec(0), spec(1), spec(2)],
        out_specs=pl.BlockSpec((SEQ, pairs * 2 * NA_HD), lambda b, h: (b, h)),
        compiler_params=_cparams(("arbitrary", "arbitrary")),
        name="ctx_attn",
    )(qkv, qkv, qkv)


def _na_tables():
    q = np.arange(GRID_W)[:, None]
    w = np.arange(GRID_W)[None, :]
    idx_c = np.clip(w - q + (NA_KW - 1), 0, 2 * NA_KW - 2)
    onehot = (idx_c.reshape(1, -1) == np.arange(32)[:, None]).astype(np.float32)
    c_start = np.clip(np.arange(GRID_W) - NA_KW // 2, 0, GRID_W - NA_KW)[:, None]
    inside = (w >= c_start) & (w < c_start + NA_KW)
    cmask = np.where(inside, 0.0, -np.inf).astype(np.float32)
    return onehot, np.tile(cmask, (1, 2))


def _rpb_expand_kernel(r_ref, e_ref, o_ref):
    o_ref[...] = jnp.dot(r_ref[...], e_ref[...], precision=HIGHEST, preferred_element_type=F32)


def rpb_expand(rpb):
    onehot, _ = _na_tables()
    rp = jnp.pad(rpb, ((0, 0), (0, 1), (0, 1)))
    return pl.pallas_call(
        _rpb_expand_kernel,
        out_shape=jax.ShapeDtypeStruct((NA_HEADS, 16, GRID_W * GRID_W), F32),
        grid=(NA_HEADS,),
        in_specs=[pl.BlockSpec((None, 16, 32), lambda h: (h, 0, 0)),
                  pl.BlockSpec((32, GRID_W * GRID_W), lambda h: (0, 0))],
        out_specs=pl.BlockSpec((None, 16, GRID_W * GRID_W), lambda h: (h, 0, 0)),
        compiler_params=_cparams(("arbitrary",)),
        name="rpb_expand",
    )(rp, jnp.asarray(onehot))


NA_QROWS = 8
NA_WIN = NA_QROWS + NA_KH


def _na_attn_kernel(q_ref, k_ref, v_ref, kc_ref, vc_ref, t_ref, o_ref, p_loc, p_ctx):
    nq = NA_QROWS * GRID_W
    nk = NA_WIN * GRID_W
    pair = 2 * GRID_W
    nt = NT_DIMS
    lane = lax.broadcasted_iota(jnp.int32, (GRID_W, pair), 1)
    zero_tile = jnp.zeros((GRID_W, pair), BF16)
    kc = kc_ref[...].reshape(2 * NA_HD, PAST_LEN).astype(BF16)
    vc = vc_ref[...].reshape(2 * NA_HD, PAST_LEN).astype(BF16)
    for blk in range(GRID_R // NA_QROWS):
        k0 = min(max(NA_QROWS * blk - NA_KH // 2, 0), GRID_R - NA_WIN)
        q2 = q_ref[blk * nq:(blk + 1) * nq, :] * (NA_HD ** -0.5)
        kw = k_ref[k0 * GRID_W:k0 * GRID_W + nk, :]
        vw = v_ref[k0 * GRID_W:k0 * GRID_W + nk, :]
        outs = []
        for hh in range(2):
            q = _one_head(q2, hh == 0)
            s_loc = lax.dot_general(q, kw, nt, preferred_element_type=F32)
            s_ctx = _dot(q, kc)
            denoms = []
            for qi in range(NA_QROWS):
                r = NA_QROWS * blk + qi
                r_start = min(max(r - NA_KH // 2, 0), GRID_R - NA_KH)
                rows = slice(qi * GRID_W, (qi + 1) * GRID_W)
                ctx_tiles = [s_ctx[rows, c * pair:(c + 1) * pair] for c in range(PAST_LEN // pair)]
                tiles = {}
                for j in range(nk // pair):
                    kr = k0 + 2 * j
                    ok0 = r_start <= kr < r_start + NA_KH
                    ok1 = r_start <= kr + 1 < r_start + NA_KH
                    if not (ok0 or ok1):
                        continue
                    sb = s_loc[rows, j * pair:(j + 1) * pair] + t_ref[hh, kr - r + NA_KH]
                    if not (ok0 and ok1):
                        sb = jnp.where((lane < GRID_W) if ok0 else (lane >= GRID_W), sb, -jnp.inf)
                    tiles[j] = sb
                mx = functools.reduce(jnp.maximum, list(tiles.values()) + ctx_tiles)
                m = jnp.max(mx, axis=1, keepdims=True)
                acc = None
                for j in range(nk // pair):
                    if j in tiles:
                        p = jnp.exp(tiles[j] - m)
                        acc = p if acc is None else acc + p
                        p_loc[rows, j * pair:(j + 1) * pair] = p.astype(BF16)
                    else:
                        p_loc[rows, j * pair:(j + 1) * pair] = zero_tile
                for c, t in enumerate(ctx_tiles):
                    p = jnp.exp(t - m)
                    acc = acc + p
                    p_ctx[rows, c * pair:(c + 1) * pair] = p.astype(BF16)
                denoms.append(jnp.sum(acc, axis=1, keepdims=True))
            pv = _dot(p_loc[...], vw) + lax.dot_general(p_ctx[...], vc, nt, preferred_element_type=F32)
            outs.append(pv / jnp.concatenate(denoms, axis=0))
        o_ref[blk * nq:(blk + 1) * nq, :] = jnp.where(_pair_mask(outs[0].shape), outs[0], outs[1]).astype(o_ref.dtype)


def na_attention(qkv, k_ctx, v_ctx, bias_pairs):
    blk0 = N_P // DEC_SEQ
    spec = lambda part: pl.BlockSpec((None, None, DEC_SEQ, 2 * NA_HD), lambda h, b: (part, h, b + blk0, 0))
    cspec = pl.BlockSpec((None, 2, NA_HD, PAST_LEN), lambda h, b: (b, h, 0, 0))
    return pl.pallas_call(
        _na_attn_kernel,
        out_shape=jax.ShapeDtypeStruct((N_S, D), BF16),
        grid=(HEAD_PAIRS, DEC_BATCH),
        in_specs=[spec(0), spec(1), spec(2), cspec, cspec,
                  pl.BlockSpec((2, 16, GRID_W, 2 * GRID_W), lambda h, b: (h, 0, 0, 0))],
        out_specs=pl.BlockSpec((DEC_SEQ, 2 * NA_HD), lambda h, b: (b, h)),
        scratch_shapes=[pltpu.VMEM((NA_QROWS * GRID_W, NA_WIN * GRID_W), BF16),
                        pltpu.VMEM((NA_QROWS * GRID_W, PAST_LEN), BF16)],
        compiler_params=_cparams(("arbitrary", "arbitrary")),
        name="na_attn",
    )(qkv, qkv, qkv, k_ctx, v_ctx, bias_pairs)


def mixer_c(h, k_ctx, v_ctx, w_in_all, layer, rpb):
    qkv, k_new, v_new = qkv_proj(h, w_in_all, layer)
    o_p = ctx_attention(qkv)
    _, cmask2 = _na_tables()
    b15 = rpb_expand(rpb).reshape(NA_HEADS, 16, GRID_W, GRID_W)
    b17 = jnp.pad(b15, ((0, 0), (1, 0), (0, 0), (0, 0)))
    bias_pairs = jnp.concatenate([b17[:, :16], b17[:, 1:]], axis=-1) + jnp.asarray(cmask2)
    o_s = na_attention(qkv, jnp.swapaxes(k_ctx, -1, -2), jnp.swapaxes(v_ctx, -1, -2), bias_pairs)
    return (o_p, o_s), k_new, v_new


SCAN_BLOCK = HD_M


def _mlstm_kernel(*refs, seq, heads, zero_state, emit_state):
    q_ref, k_ref, v_ref, og_ref, gates_ref, wq_ref, wk_ref, gh_ref = refs[:8]
    pos = 8
    if not zero_state:
        c0_ref, n0_ref, m0_ref = refs[pos:pos + 3]
        pos += 3
    y_ref = refs[pos]
    pos += 1
    if emit_state:
        cn_ref, nn_ref, mn_ref = refs[pos:pos + 3]
        pos += 3
    pre_s, suf_s, gt_s, pret_s, suft_s, kv_s, ks_s, be_s, mk_s, cp_s, np_s, mp_s, cst, nst, mst = refs[pos:]

    blk = SCAN_BLOCK
    nc = seq // blk
    nbatch = nc * heads
    n_gates = 4 * HEADS_M
    r = lax.broadcasted_iota(jnp.int32, (seq, 1), 0)

    @pl.when(pl.program_id(1) == 0)
    def _():
        g_all = gates_ref[...]
        lf = jax.nn.log_sigmoid(g_all)
        rin = r % blk
        pre, suf = lf, lf
        for sh in [1 << i for i in range(blk.bit_length() - 1)]:
            pre = pre + jnp.where(rin >= sh, pltpu.roll(pre, sh, 0), 0.0)
            suf = suf + jnp.where(rin < blk - sh, pltpu.roll(suf, seq - sh, 0), 0.0)
        pre3, suf3 = pre.reshape(nc, blk, LANE), suf.reshape(nc, blk, LANE)
        pre_s[...] = pre3
        suf_s[...] = suf3
        gt_s[...] = jnp.swapaxes(g_all.reshape(nc, blk, LANE), 1, 2)[:, :n_gates, :]
        pret_s[...] = jnp.swapaxes(pre3, 1, 2)[:, :n_gates, :]
        suft_s[...] = jnp.swapaxes(suf3, 1, 2)[:, :n_gates, :]

    def split_heads(x):
        x3 = x.reshape(nc, blk, heads * HD_M)
        if heads == 1:
            return x3
        tiles = jnp.stack([x3[:, :, hh * HD_M:(hh + 1) * HD_M] for hh in range(heads)], axis=1)
        return tiles.reshape(nbatch, blk, HD_M)

    def per_head(x):
        if heads == 1:
            return x
        return jnp.broadcast_to(x[:, None], (nc, heads) + x.shape[1:]).reshape((nbatch,) + x.shape[1:])

    first, last = r == 0, r == seq - 1
    qc_all = _dwconv3(q_ref[...].astype(F32), wq_ref, first, last)
    q3 = split_heads(qc_all * jax.nn.sigmoid(qc_all))
    kc_all = _dwconv3(k_ref[...].astype(F32), wk_ref, first, last)
    k3 = split_heads(kc_all * jax.nn.sigmoid(kc_all) * (HD_M ** -0.5))
    qb, kb = q3.astype(BF16), k3.astype(BF16)
    vb = split_heads(v_ref[...])
    g3 = per_head(gates_ref[...].reshape(nc, blk, LANE))
    gt3 = per_head(gt_s[...])

    if zero_state:
        cst[...] = jnp.zeros_like(cst)
        nst[...] = jnp.zeros_like(nst)
        mst[...] = jnp.zeros_like(mst)
    else:
        cst[...] = c0_ref[...]
        nst[...] = n0_ref[...]
        mst[...] = jnp.broadcast_to(m0_ref[...], mst.shape)

    tt = lax.broadcasted_iota(jnp.int32, (1, blk, blk), 1)
    ss = lax.broadcasted_iota(jnp.int32, (1, blk, blk), 2)
    lane = lax.broadcasted_iota(jnp.int32, (1, 1, LANE), 2)
    sub = lax.broadcasted_iota(jnp.int32, (1, n_gates, 1), 1)
    head = lax.broadcasted_iota(jnp.int32, (nbatch, 1, 1), 0) % heads + pl.program_id(1) * heads
    hsum = None
    for d in range(2):
        i_idx = d * 2 * HEADS_M + head
        f_idx = i_idx + HEADS_M
        mask = (ss <= tt) if d == 0 else (ss >= tt)
        b3 = per_head((pre_s if d == 0 else suf_s)[...])
        bt3 = per_head((pret_s if d == 0 else suft_s)[...])
        bcol = jnp.sum(jnp.where(lane == f_idx, b3, 0.0), axis=2, keepdims=True)
        icol = jnp.sum(jnp.where(lane == i_idx, g3, 0.0), axis=2, keepdims=True)
        brow = jnp.sum(jnp.where(sub == f_idx, bt3, 0.0), axis=1, keepdims=True)
        irow = jnp.sum(jnp.where(sub == i_idx, gt3, 0.0), axis=1, keepdims=True)
        bend = bcol[:, blk - 1:blk, :] if d == 0 else bcol[:, 0:1, :]

        dmat = jnp.where(mask, bcol - brow + irow, -jnp.inf)
        mloc = jnp.max(dmat, axis=2, keepdims=True)
        qk = jnp.einsum('ctd,csd->cts', qb, kb, preferred_element_type=F32)
        s_loc = jnp.exp(dmat - mloc) * qk
        num_loc = jnp.einsum('cts,csd->ctd', s_loc.astype(BF16), vb, preferred_element_type=F32)
        den_loc = jnp.sum(s_loc, axis=2, keepdims=True)
        to_end = bend - bcol + icol
        mk = jnp.max(to_end, axis=1, keepdims=True)
        kw = k3 * jnp.exp(to_end - mk)
        kv_s[...] = jnp.einsum('cds,cse->cde', jnp.swapaxes(kw, 1, 2).astype(BF16), vb,
                               preferred_element_type=F32)
        ks_s[...] = jnp.sum(kw, axis=1, keepdims=True)
        be_s[...] = jnp.broadcast_to(bend, be_s.shape)
        mk_s[...] = jnp.broadcast_to(mk, mk_s.shape)

        def step(j, carry, d=d):
            c = j if d == 0 else nc - 1 - j
            sl = pl.ds(c * heads, heads)
            m_prev, c_prev, n_prev = mst[d], cst[d], nst[d]
            cp_s[sl] = c_prev.astype(BF16)
            np_s[sl] = n_prev
            mp_s[sl] = m_prev
            be, mkc = be_s[sl], mk_s[sl]
            m_new = jnp.maximum(be + m_prev, mkc)
            keep = jnp.exp(be + m_prev - m_new)
            add = jnp.exp(mkc - m_new)
            cst[d] = keep * c_prev + add * kv_s[sl]
            nst[d] = keep * n_prev + add * ks_s[sl]
            mst[d] = m_new
            return carry

        lax.fori_loop(0, nc, step, 0)

        m_inter = bcol + mp_s[...][:, :, 0:1]
        m_t = jnp.maximum(m_inter, mloc)
        w_state = jnp.exp(m_inter - m_t)
        w_loc = jnp.exp(mloc - m_t)
        inter = jnp.einsum('ctd,cde->cte', qb, cp_s[...], preferred_element_type=F32)
        num = w_state * inter + w_loc * num_loc
        den = w_state * jnp.sum(q3 * np_s[...], axis=2, keepdims=True) + w_loc * den_loc
        h = num / jnp.maximum(jnp.abs(den), jnp.exp(-m_t))
        hsum = h if hsum is None else hsum + h

    hn = hsum * lax.rsqrt(jnp.mean(hsum * hsum, axis=-1, keepdims=True) + EPS)
    og3 = split_heads(og_ref[...].astype(F32))
    for c in range(nc):
        for hh in range(heads):
            cols = slice(hh * HD_M, (hh + 1) * HD_M)
            e = c * heads + hh
            y_ref[c * blk:(c + 1) * blk, cols] = (hn[e] * gh_ref[:, cols] * jax.nn.sigmoid(og3[e])).astype(y_ref.dtype)
    if emit_state:
        cn_ref[...] = cst[...]
        nn_ref[...] = nst[...]
        mn_ref[...] = mst[...]


def mlstm(proj, gates, w_conv_qk, g_head, state, *, nb, seq, row0, heads):
    blk0 = row0 // seq
    nblk = seq // SCAN_BLOCK
    groups = HEADS_M // heads
    n_gates = 4 * HEADS_M
    zero_state = state is None
    col = lambda part: pl.BlockSpec((seq, heads * HD_M), lambda b, h: (b + blk0, part * groups + h))
    in_specs = [col(0), col(1), col(2), col(3),
                pl.BlockSpec((seq, LANE), lambda b, h: (b + blk0, 0)),
                pl.BlockSpec((3, heads * HD_M), lambda b, h: (0, h)),
                pl.BlockSpec((3, heads * HD_M), lambda b, h: (0, groups + h)),
                pl.BlockSpec((1, heads * HD_M), lambda b, h: (0, h))]
    args = [proj, proj, proj, proj, gates, w_conv_qk, w_conv_qk, g_head.reshape(1, MW)]
    state_specs = [pl.BlockSpec((None, 2, heads, HD_M, HD_M), lambda b, h: (b, 0, h, 0, 0)),
                   pl.BlockSpec((None, 2, heads, 1, HD_M), lambda b, h: (b, 0, h, 0, 0))]
    if not zero_state:
        c0, n0, m0 = state
        in_specs += state_specs + [pl.BlockSpec((None, 2, heads, 1, 1), lambda b, h: (b, 0, h, 0, 0))]
        args += [c0, n0.reshape(nb, 2, HEADS_M, 1, HD_M), m0.reshape(nb, 2, HEADS_M, 1, 1)]
    out_shape = [jax.ShapeDtypeStruct((nb * seq, MW), BF16)]
    out_specs = [pl.BlockSpec((seq, heads * HD_M), lambda b, h: (b, h))]
    if zero_state:
        out_shape += [jax.ShapeDtypeStruct((nb, 2, HEADS_M, HD_M, HD_M), F32),
                      jax.ShapeDtypeStruct((nb, 2, HEADS_M, 1, HD_M), F32),
                      jax.ShapeDtypeStruct((nb, 2, HEADS_M, 1, LANE), F32)]
        out_specs += state_specs + [pl.BlockSpec((None, 2, heads, 1, LANE), lambda b, h: (b, 0, h, 0, 0))]
    per_block = lambda *shape: pltpu.VMEM((nblk,) + shape, F32)
    per_entry = lambda *shape, dtype=F32: pltpu.VMEM((nblk * heads,) + shape, dtype)
    return pl.pallas_call(
        functools.partial(_mlstm_kernel, seq=seq, heads=heads, zero_state=zero_state, emit_state=zero_state),
        out_shape=out_shape,
        grid=(nb, groups),
        in_specs=in_specs,
        out_specs=out_specs,
        scratch_shapes=[per_block(SCAN_BLOCK, LANE), per_block(SCAN_BLOCK, LANE),
                        per_block(n_gates, SCAN_BLOCK), per_block(n_gates, SCAN_BLOCK), per_block(n_gates, SCAN_BLOCK),
                        per_entry(HD_M, HD_M), per_entry(1, HD_M), per_entry(1, LANE), per_entry(1, LANE),
                        per_entry(HD_M, HD_M, dtype=BF16), per_entry(1, HD_M), per_entry(1, LANE),
                        pltpu.VMEM((2, heads, HD_M, HD_M), F32), pltpu.VMEM((2, heads, 1, HD_M), F32),
                        pltpu.VMEM((2, heads, 1, LANE), F32)],
        compiler_params=_cparams(("arbitrary", "arbitrary")),
        name="mlstm_%d" % seq,
    )(*args)


def _hyena_pre_kernel(v_ref, x1_ref, x2_ref, wv_ref, w1_ref, w2_ref, u_ref, x2c_ref):
    first, last = _seq_edges(SEQ_BLOCK, pl.program_id(0) < N_PBLK)
    x1c = _dwconv3(x1_ref[...].astype(F32), w1_ref, first, last)
    u_ref[...] = (x1c * _dwconv3(v_ref[...].astype(F32), wv_ref, first, last)).astype(u_ref.dtype)
    x2c_ref[...] = _dwconv3(x2_ref[...].astype(F32), w2_ref, first, last).astype(x2c_ref.dtype)


def hyena_pre(proj, w_conv_hy):
    tc = 256
    nct = HW // tc
    c0 = 4 * MW // tc
    pcol = lambda part: pl.BlockSpec((SEQ_BLOCK, tc), lambda i, j: (i, c0 + part * nct + j))
    wcol = lambda part: pl.BlockSpec((3, tc), lambda i, j: (0, part * nct + j))
    out = jax.ShapeDtypeStruct((N_TOK, HW), BF16)
    ospec = pl.BlockSpec((SEQ_BLOCK, tc), lambda i, j: (i, j))
    return pl.pallas_call(
        _hyena_pre_kernel,
        out_shape=[out, out],
        grid=(N_TOK // SEQ_BLOCK, nct),
        in_specs=[pcol(0), pcol(1), pcol(2), wcol(0), wcol(1), wcol(2)],
        out_specs=[ospec, ospec],
        compiler_params=_cparams(("arbitrary", "arbitrary")),
        name="hyena_pre",
    )(proj, proj, proj, w_conv_hy, w_conv_hy, w_conv_hy)


@functools.lru_cache(maxsize=None)
def _filter_tables(seq):
    t = np.linspace(0.0, 1.0, seq)[:, None]
    wpos = 2.0 * np.pi * np.arange(seq)[:, None] / seq
    bands = np.linspace(1e-4, N_BANDS - 1, N_BANDS)[None, :]
    z = np.concatenate([t, np.cos(bands * wpos), -np.sin(bands * wpos)], axis=-1)
    z = np.pad(z, ((0, 0), (0, LANE - FILTER_EMB)))
    max_decay = math.log(DECAY_TARGET) / DECAY_FAST
    min_decay = math.log(DECAY_TARGET) / DECAY_SLOW
    deltas = np.abs(np.linspace(min_decay, max_decay, HW))
    decay = np.exp(-t * np.concatenate([deltas, deltas])[None, :])
    return z.astype(np.float32), decay.astype(np.float32)


@functools.lru_cache(maxsize=None)
def _dft_tables(seq, tk):
    n = 2 * seq
    k = np.arange(seq)[:, None]
    t = np.arange(seq)[None, :]
    ang = 2.0 * np.pi * ((k * t) % n) / n
    alt = np.where(np.arange(seq) % 2 == 0, 1.0, -1.0)
    cm, sm = np.cos(ang), np.sin(ang)
    sm[0, :] = alt
    fwd = np.stack([cm.reshape(seq // tk, tk, seq), sm.reshape(seq // tk, tk, seq)], axis=1)
    wk = np.where(np.arange(seq) == 0, 1.0, 2.0)[None, :]
    ci = (np.cos(ang.T) * wk) / n
    si = np.sin(ang.T) * 2.0 / n
    si[:, 0] = alt / n
    inv = np.concatenate([ci, si], axis=1)
    return fwd.astype(np.float32), inv.astype(np.float32)


def _filter_kernel(z_ref, w1_ref, b1_ref, w2_ref, b2_ref, w3_ref, fr_ref, dec_ref, hs_ref, hd_ref):
    fr = fr_ref[...]
    hp = functools.partial(jnp.dot, precision=HIGHEST, preferred_element_type=F32)
    h1 = jnp.sin(fr * (hp(z_ref[...], w1_ref[...]) + b1_ref[...]))
    h2 = jnp.sin(fr * (hp(h1, w2_ref[...]) + b2_ref[...]))
    w3 = w3_ref[...]
    h_hi, w_hi = h2.astype(BF16), w3.astype(BF16)
    h_lo, w_lo = (h2 - h_hi.astype(F32)).astype(BF16), (w3 - w_hi.astype(F32)).astype(BF16)
    filt = (_dot(h_hi, w_hi) + _dot(h_hi, w_lo) + _dot(h_lo, w_hi)) * dec_ref[...]
    past, fut = filt[:, :HW], filt[:, HW:]
    rows = filt.shape[0]
    grow = lax.broadcasted_iota(jnp.int32, (rows, 1), 0) + pl.program_id(0) * rows
    fut = jnp.where(grow == 0, 0.0, fut)
    hs_ref[...] = past + fut
    hd_ref[...] = past - fut


def filter_gen(seq, w1, b1, w2, b2, w3, freq):
    z, decay = _filter_tables(seq)
    tl = 256
    fh = FILTER_HIDDEN
    full = lambda shape: pl.BlockSpec(shape, lambda i: (0, 0))
    out = jax.ShapeDtypeStruct((seq, HW), F32)
    return pl.pallas_call(
        _filter_kernel,
        out_shape=[out, out],
        grid=(seq // tl,),
        in_specs=[pl.BlockSpec((tl, LANE), lambda i: (i, 0)), full((LANE, fh)), full((1, fh)), full((fh, fh)),
                  full((1, fh)), full((fh, 2 * HW)), full((1, fh)), pl.BlockSpec((tl, 2 * HW), lambda i: (i, 0))],
        out_specs=[pl.BlockSpec((tl, HW), lambda i: (i, 0))] * 2,
        compiler_params=_cparams(("arbitrary",)),
        name="filter_gen",
    )(jnp.asarray(z), jnp.pad(w1, ((0, LANE - FILTER_EMB), (0, 0))), b1.reshape(1, fh), w2, b2.reshape(1, fh), w3,
      freq.reshape(1, fh), jnp.asarray(decay))


def _dft_filter_kernel(a_ref, hs_ref, hd_ref, k_ref, hs_bf, hd_bf):
    @pl.when(pl.program_id(0) == 0)
    def _():
        hs_bf[...] = hs_ref[...].astype(BF16)
        hd_bf[...] = hd_ref[...].astype(BF16)

    k_ref[0] = _dot(a_ref[0].astype(BF16), hs_bf[...])
    k_ref[1] = _dot(a_ref[1].astype(BF16), hd_bf[...])

    @pl.when(pl.program_id(0) == 0)
    def _():
        k_ref[1, 0:1, :] = _dot(a_ref[1, 0:8, :].astype(BF16), hs_bf[...])[0:1, :]


def dft_filter(seq, tk, hs, hd):
    fwd, _ = _dft_tables(seq, tk)
    return pl.pallas_call(
        _dft_filter_kernel,
        out_shape=jax.ShapeDtypeStruct((2, seq, HW), F32),
        grid=(seq // tk,),
        in_specs=[pl.BlockSpec((None, 2, tk, seq), lambda m: (m, 0, 0, 0)),
                  pl.BlockSpec((seq, HW), lambda m: (0, 0)), pl.BlockSpec((seq, HW), lambda m: (0, 0))],
        out_specs=pl.BlockSpec((2, tk, HW), lambda m: (0, m, 0)),
        scratch_shapes=[pltpu.VMEM((seq, HW), BF16), pltpu.VMEM((seq, HW), BF16)],
        compiler_params=_cparams(("arbitrary",)),
        name="dft_filter",
    )(jnp.asarray(fwd), hs, hd)


def _dft_fwd_kernel(a_ref, u_ref, k_ref, y_ref, u_bf, tbl):
    m = pl.program_id(1)

    @pl.when(m == 0)
    def _():
        u_bf[...] = u_ref[...].astype(BF16)

    @pl.when(pl.program_id(0) == 0)
    def _():
        tbl[m] = a_ref[...].astype(BF16)

    ure = _dot(tbl[m, 0], u_bf[...])
    uim = _dot(tbl[m, 1], u_bf[...])
    kre, kim = k_ref[0], k_ref[1]
    packed = (lax.broadcasted_iota(jnp.int32, (ure.shape[0], 1), 0) == 0) & (m == 0)
    y_ref[0] = jnp.where(packed, ure * kre, ure * kre - uim * kim).astype(y_ref.dtype)
    y_ref[1] = jnp.where(packed, uim * kim, ure * kim + uim * kre).astype(y_ref.dtype)


def dft_fwd(seq, tk, u, kf, *, nb, row0):
    fwd, _ = _dft_tables(seq, tk)
    blk0 = row0 // seq
    nm = seq // tk
    return pl.pallas_call(
        _dft_fwd_kernel,
        out_shape=jax.ShapeDtypeStruct((nb, 2, seq, HW), BF16),
        grid=(nb, nm),
        in_specs=[pl.BlockSpec((None, 2, tk, seq), lambda b, m: (jnp.where(b == 0, m, nm - 1), 0, 0, 0)),
                  pl.BlockSpec((seq, HW), lambda b, m: (b + blk0, 0)),
                  pl.BlockSpec((2, tk, HW), lambda b, m: (0, m, 0))],
        out_specs=pl.BlockSpec((None, 2, tk, HW), lambda b, m: (b, 0, m, 0)),
        scratch_shapes=[pltpu.VMEM((seq, HW), BF16), pltpu.VMEM((nm, 2, tk, seq), BF16)],
        compiler_params=_cparams(("arbitrary", "arbitrary"), DFT_VMEM_LIMIT),
        name="dft_fwd",
    )(jnp.asarray(fwd), u, kf)


def _dft_inv_kernel(a_ref, y_ref, u_ref, x2_ref, bias_ref, o_ref, tbl, *, seq):
    t = pl.program_id(1)

    @pl.when(pl.program_id(0) == 0)
    def _():
        tbl[t] = a_ref[...].astype(BF16)

    conv = _dot(tbl[t, :, :seq], y_ref[0]) + _dot(tbl[t, :, seq:], y_ref[1])
    u = u_ref[...].astype(F32)
    o_ref[...] = (x2_ref[...].astype(F32) * (conv + bias_ref[...] * u)).astype(o_ref.dtype)


def dft_inv(seq, tk, y, u, x2c, bias, *, nb, row0):
    _, inv = _dft_tables(seq, tk)
    tm = min(seq, 512)
    nt = seq // tm
    blk0 = row0 // tm
    rows = lambda b, t: (b * nt + t + blk0, 0)
    return pl.pallas_call(
        functools.partial(_dft_inv_kernel, seq=seq),
        out_shape=jax.ShapeDtypeStruct((nb * seq, HW), BF16),
        grid=(nb, nt),
        in_specs=[pl.BlockSpec((tm, 2 * seq), lambda b, t: (jnp.where(b == 0, t, nt - 1), 0)),
                  pl.BlockSpec((None, 2, seq, HW), lambda b, t: (b, 0, 0, 0)),
                  pl.BlockSpec((tm, HW), rows), pl.BlockSpec((tm, HW), rows),
                  pl.BlockSpec((1, HW), lambda b, t: (0, 0))],
        out_specs=pl.BlockSpec((tm, HW), lambda b, t: (b * nt + t, 0)),
        scratch_shapes=[pltpu.VMEM((nt, tm, 2 * seq), BF16)],
        compiler_params=_cparams(("arbitrary", "arbitrary"), DFT_VMEM_LIMIT),
        name="dft_inv",
    )(jnp.asarray(inv), y, u, x2c, bias.reshape(1, HW))


def hyena_group(seq, u, x2c, filt_w, bias, *, nb, row0):
    tk = min(seq, 512)
    hs, hd = filter_gen(seq, *filt_w)
    kf = dft_filter(seq, tk, hs, hd)
    y = dft_fwd(seq, tk, u, kf, nb=nb, row0=row0)
    return dft_inv(seq, tk, y, u, x2c, bias, nb=nb, row0=row0)


def mixer_ab_parts(proj, gates, state_s, w_conv_qk, g_head, w_conv_hy, w_f1, b_f1, w_f2, b_f2, w_f3, freq, hy_bias):
    ym_p, c_p, n_p, m_p = mlstm(proj, gates, w_conv_qk, g_head, None, nb=BATCH, seq=SEQ, row0=0, heads=HEADS_M)
    (ym_s,) = mlstm(proj, gates, w_conv_qk, g_head, state_s, nb=DEC_BATCH, seq=DEC_SEQ, row0=N_P, heads=1)
    u, x2c = hyena_pre(proj, w_conv_hy)
    filt_w = (w_f1, b_f1, w_f2, b_f2, w_f3, freq)
    yh_p = hyena_group(SEQ, u, x2c, filt_w, hy_bias, nb=BATCH, row0=0)
    yh_s = hyena_group(DEC_SEQ, u, x2c, filt_w, hy_bias, nb=DEC_BATCH, row0=N_P)
    state_p = (c_p, n_p[:, :, :, 0, :], m_p[:, :, :, 0, 0])
    return (ym_p, ym_s), (yh_p, yh_s), state_p


def kernel(x_prompt, x_sample, state_mlstm_C, state_mlstm_n, state_mlstm_m, cache_na_k, cache_na_v, c, c_ctx, w_ada, b_ada, g_mix, g_ffn, g_final, w_in_ab, b_gates, w_conv_qk, g_mlstm, w_conv_hy, w_filt1, b_filt1, w_filt2, b_filt2, w_filt3, filt_freq, hyena_bias, w_out_ab, w_in_c, rpb_c, w_out_c, w_up, w_conv_ffn, w_down):
    cmat = jnp.concatenate([c, c_ctx[None, :], jnp.zeros((MOD_ROWS - DEC_BATCH - 1, D), F32)], axis=0)
    mod_all = adaln_all(cmat, w_ada, b_ada).reshape(DEPTH, MOD_ROWS, 6, 1, D)
    x = (x_prompt.reshape(N_P, D), x_sample.reshape(N_S, D))
    h = None
    new_c, new_n, new_m, new_k, new_v = [], [], [], [], []
    for l in range(DEPTH):
        e = l // 2
        if l % 2 == 0:
            assert l == 0, "a later mixer A/B layer would take the fused norm of the layer before it"
            proj, gates = ab_in(x, g_mix, mod_all, l, jnp.swapaxes(w_in_ab, 1, 2), b_gates[e], e)
            state_s = (state_mlstm_C[:, e], state_mlstm_n[:, e], state_mlstm_m[:, e])
            y_m, y_h, (c_p, n_p, m_p) = mixer_ab_parts(
                proj, gates, state_s, w_conv_qk[e], g_mlstm[e], w_conv_hy[e], w_filt1[e], b_filt1[e],
                w_filt2[e], b_filt2[e], w_filt3[e], filt_freq[e], hyena_bias[e])
            a_list = [y_m, y_h]
            w_list = [(w_out_ab, (None, MW, D), lambda i, e=e: (e, 0, 0)),
                      (w_out_ab, (None, HW, D), lambda i, e=e: (e, 1, 0))]
            new_c.append(c_p[:, None])
            new_n.append(n_p[:, None])
            new_m.append(m_p[:, None])
        else:
            o, k_new, v_new = mixer_c(h, cache_na_k[:, e], cache_na_v[:, e], w_in_c, e, rpb_c[e])
            a_list = [o]
            w_list = [(w_out_c, (None, D, D), lambda i, e=e: (e, 0, 0))]
            new_k.append(k_new)
            new_v.append(v_new)
        x, h = out_proj(a_list, w_list, x, mod_all, l, 2, g_ffn, l, (l, 3, 4), bm=2 * ROW_BM, name="mixer_out")
        mid = ffn_up(h, w_up, w_conv_ffn, l)
        w_list = [(w_down, (None, FF, D), lambda i, l=l: (l, 0, 0))]
        if l + 1 < DEPTH:
            x, h = out_proj([mid], w_list, x, mod_all, l, 5, g_mix, l + 1, (l + 1, 0, 1), name="ffn_down")
        else:
            y_p, y_s = out_proj([mid], w_list, x, mod_all, l, 5, g_final, 0, None, name="ffn_down_final")
    cat = lambda parts: parts[0] if len(parts) == 1 else jnp.concatenate(parts, axis=1)
    return (y_p.reshape(BATCH, SEQ, D), y_s.reshape(DEC_BATCH, DEC_SEQ, D), cat(new_c), cat(new_n), cat(new_m),
            cat(new_k), cat(new_v))
```

```python
import functools
import math

import numpy as np
import jax
import jax.numpy as jnp
from jax import lax
from jax.experimental import pallas as pl
from jax.experimental.pallas import tpu as pltpu

F32 = jnp.float32
BF16 = jnp.bfloat16

D = 1024
BATCH, SEQ = 16, 256
DEC_BATCH, DEC_SEQ = 4, 2048
PAST_LEN = 512
DEPTH = 2
GRID_W = 64
GRID_R = DEC_SEQ // GRID_W
HEADS_M = 4
MW = D // 2
HD_M = MW // HEADS_M
HW = D // 2
N_BANDS = 16
FILTER_EMB = 2 * N_BANDS + 1
FILTER_HIDDEN = 64
DECAY_FAST, DECAY_SLOW, DECAY_TARGET = 0.3, 1.5, 1e-2
NA_HEADS = 16
NA_HD = D // NA_HEADS
NA_KH, NA_KW = 8, 16
FF = 2816
EPS = 1e-6

N_P = BATCH * SEQ
N_S = DEC_BATCH * DEC_SEQ
N_TOK = N_P + N_S
CTX_ROW = DEC_BATCH
MOD_ROWS = 8
LANE = 128
VMEM_LIMIT = 48 * 1024 * 1024
QKV_VMEM_LIMIT = 56 * 1024 * 1024
DFT_VMEM_LIMIT = 56 * 1024 * 1024
HIGHEST = lax.Precision.HIGHEST


def _cparams(sem, vmem_limit=VMEM_LIMIT):
    return pltpu.CompilerParams(dimension_semantics=sem, vmem_limit_bytes=vmem_limit)


def _mod_row(i, bm):
    return jnp.where(i < N_P // bm, CTX_ROW, (i - N_P // bm) // (DEC_SEQ // bm))


def _dot(a, b):
    return jnp.dot(a, b, preferred_element_type=F32)


def _adaln_kernel(c_ref, w_ref, b_ref, o_ref):
    cv = c_ref[...]
    s = cv * jax.nn.sigmoid(cv)
    o_ref[...] = _dot(s.astype(BF16), w_ref[...].astype(BF16)) + b_ref[...]


def adaln_all(cmat, w_ada, b_ada):
    tn = 1024
    return pl.pallas_call(
        _adaln_kernel,
        out_shape=jax.ShapeDtypeStruct((DEPTH, MOD_ROWS, 6 * D), F32),
        grid=(DEPTH, 6 * D // tn),
        in_specs=[
            pl.BlockSpec((MOD_ROWS, D), lambda l, j: (0, 0)),
            pl.BlockSpec((None, D, tn), lambda l, j: (l, 0, j)),
            pl.BlockSpec((None, 1, tn), lambda l, j: (l, 0, j)),
        ],
        out_specs=pl.BlockSpec((None, MOD_ROWS, tn), lambda l, j: (l, 0, j)),
        compiler_params=_cparams(("arbitrary", "arbitrary")),
        name="adaln",
    )(cmat, w_ada, b_ada.reshape(DEPTH, 1, 6 * D))


ROW_BM = 512
ROW_NPB = N_P // ROW_BM


def _row_specs(arr, cols, bm=ROW_BM):
    npb = N_P // bm
    if isinstance(arr, tuple):
        return ([pl.BlockSpec((bm, cols), lambda i: (jnp.minimum(i, npb - 1), 0)),
                 pl.BlockSpec((bm, cols), lambda i: (jnp.maximum(i - npb, 0), 0))], list(arr))
    return [pl.BlockSpec((bm, cols), lambda i: (i, 0))], [arr]


def _pick_rows(refs, npb=ROW_NPB, rows=slice(None)):
    if len(refs) == 1:
        return refs[0][rows, :]
    return jnp.where(pl.program_id(0) < npb, refs[0][rows, :], refs[1][rows, :])


def _mod_spec(layer, which, bm=ROW_BM):
    return pl.BlockSpec((None, None, None, 1, D), lambda i: (layer, _mod_row(i, bm), which, 0, 0))


def _layer_vec_spec(layer):
    return pl.BlockSpec((None, 1, D), lambda i: (layer, 0, 0))


def _rms_mod(x, g_ref, sh_ref, sc_ref):
    y = x * lax.rsqrt(jnp.mean(x * x, axis=-1, keepdims=True) + EPS)
    return (y * g_ref[...]) * (1.0 + sc_ref[...]) + sh_ref[...]


OUT_CHUNKS = 2


def _out_proj_kernel(*refs, a_counts, n_x, final, npb):
    pos = 0
    a_groups = []
    for cnt in a_counts:
        a_groups.append(refs[pos:pos + cnt])
        pos += cnt
    w_refs = refs[pos:pos + len(a_counts)]
    pos += len(a_counts)
    x_refs = refs[pos:pos + n_x]
    pos += n_x
    gt_ref, g_ref = refs[pos:pos + 2]
    pos += 2
    if not final:
        sh_ref, sc_ref = refs[pos:pos + 2]
        pos += 2
    out_a, out_b = refs[pos:pos + 2]
    wbf = refs[pos + 2:]

    @pl.when(pl.program_id(0) == 0)
    def _():
        for w_ref, wb in zip(w_refs, wbf):
            wb[...] = w_ref[...].astype(BF16)

    chunk = out_a.shape[0] // OUT_CHUNKS
    for c in range(OUT_CHUNKS):
        rows = slice(c * chunk, (c + 1) * chunk)
        acc = None
        for group, wb in zip(a_groups, wbf):
            part = _dot(_pick_rows(group, npb, rows), wb[...])
            acc = part if acc is None else acc + part
        xn = _pick_rows(x_refs, npb, rows) + gt_ref[...] * acc
        if final:
            y = xn * lax.rsqrt(jnp.mean(xn * xn, axis=-1, keepdims=True) + EPS) * g_ref[...]

            @pl.when(pl.program_id(0) < npb)
            def _(y=y, rows=rows):
                out_a[rows, :] = y

            @pl.when(pl.program_id(0) >= npb)
            def _(y=y, rows=rows):
                out_b[rows, :] = y
        else:
            out_a[rows, :] = xn
            out_b[rows, :] = _rms_mod(xn, g_ref, sh_ref, sc_ref).astype(out_b.dtype)


def out_proj(a_list, w_list, x, mod_all, layer, gt_idx, g_all, g_layer, norm_mod_idx=None, bm=ROW_BM,
             name="out_proj"):
    final = norm_mod_idx is None
    npb = N_P // bm
    in_specs, args, a_counts = [], [], []
    for a in a_list:
        cols = (a[0] if isinstance(a, tuple) else a).shape[1]
        specs, ops = _row_specs(a, cols, bm)
        in_specs += specs
        args += ops
        a_counts.append(len(ops))
    w_shapes = []
    for w, block, imap in w_list:
        in_specs.append(pl.BlockSpec(block, imap, pipeline_mode=pl.Buffered(1)))
        args.append(w)
        w_shapes.append(tuple(b for b in block if b is not None))
    x_specs, x_args = _row_specs(x, D, bm)
    in_specs += x_specs + [_mod_spec(layer, gt_idx, bm), _layer_vec_spec(g_layer)]
    args += x_args + [mod_all, g_all.reshape(-1, 1, D)]
    if final:
        out_shape = [jax.ShapeDtypeStruct((N_P, D), F32), jax.ShapeDtypeStruct((N_S, D), F32)]
        out_specs = [pl.BlockSpec((bm, D), lambda i: (jnp.minimum(i, npb - 1), 0)),
                     pl.BlockSpec((bm, D), lambda i: (jnp.maximum(i - npb, 0), 0))]
    else:
        n_layer, sh_idx, sc_idx = norm_mod_idx
        in_specs += [_mod_spec(n_layer, sh_idx, bm), _mod_spec(n_layer, sc_idx, bm)]
        args += [mod_all, mod_all]
        out_shape = [jax.ShapeDtypeStruct((N_TOK, D), F32), jax.ShapeDtypeStruct((N_TOK, D), BF16)]
        out_specs = [pl.BlockSpec((bm, D), lambda i: (i, 0))] * 2
    return pl.pallas_call(
        functools.partial(_out_proj_kernel, a_counts=tuple(a_counts), n_x=len(x_args), final=final, npb=npb),
        out_shape=out_shape,
        grid=(N_TOK // bm,),
        in_specs=in_specs,
        out_specs=out_specs,
        scratch_shapes=[pltpu.VMEM(s, BF16) for s in w_shapes],
        compiler_params=_cparams(("arbitrary",)),
        name=name,
    )(*args)


def _qkv_kernel(a_ref, w_ref, o_ref, kc_ref, vc_ref, wbf):
    @pl.when(pl.program_id(0) == 0)
    def _():
        wbf[...] = w_ref[...].astype(BF16)

    a = a_ref[...]
    for part in range(3):
        acc = _dot(a, wbf[:, part * D:(part + 1) * D])
        for pp in range(NA_HEADS // 2):
            o_ref[part, pp] = acc[:, pp * 2 * NA_HD:(pp + 1) * 2 * NA_HD].astype(o_ref.dtype)
        if part > 0:
            c_ref = kc_ref if part == 1 else vc_ref

            @pl.when(pl.program_id(0) < ROW_NPB)
            def _(acc=acc, c_ref=c_ref):
                for b in range(ROW_BM // SEQ):
                    for hh in range(NA_HEADS):
                        c_ref[b, hh] = acc[b * SEQ:(b + 1) * SEQ, hh * NA_HD:(hh + 1) * NA_HD]


def qkv_proj(h, w_in, layer):
    seqs = ROW_BM // SEQ
    cache = jax.ShapeDtypeStruct((BATCH, 1, NA_HEADS, SEQ, NA_HD), F32)
    cache_spec = pl.BlockSpec((seqs, None, NA_HEADS, SEQ, NA_HD),
                              lambda i: (jnp.minimum(i, ROW_NPB - 1), 0, 0, 0, 0))
    return pl.pallas_call(
        _qkv_kernel,
        out_shape=[jax.ShapeDtypeStruct((3, NA_HEADS // 2, N_TOK, 2 * NA_HD), BF16), cache, cache],
        grid=(N_TOK // ROW_BM,),
        in_specs=[pl.BlockSpec((ROW_BM, D), lambda i: (i, 0)),
                  pl.BlockSpec((None, D, 3 * D), lambda i: (layer, 0, 0), pipeline_mode=pl.Buffered(1))],
        out_specs=[pl.BlockSpec((3, NA_HEADS // 2, ROW_BM, 2 * NA_HD), lambda i: (0, 0, i, 0)),
                   cache_spec, cache_spec],
        scratch_shapes=[pltpu.VMEM((D, 3 * D), BF16)],
        compiler_params=_cparams(("arbitrary",), QKV_VMEM_LIMIT),
        name="qkv_proj",
    )(h, w_in)


AB_MAIN = 4 * MW + 3 * HW
N_GATES = 4 * HEADS_M


def _ab_in_kernel(*refs):
    x_refs = refs[:-8]
    g_ref, sh_ref, sc_ref, w_ref, b_ref, proj_ref, gates_ref, wbf = refs[-8:]

    @pl.when(pl.program_id(0) == 0)
    def _():
        wbf[:4 * MW, :] = w_ref[:4 * MW, :].astype(BF16)
        wbf[4 * MW:AB_MAIN, :] = w_ref[4 * MW + N_GATES:, :].astype(BF16)
        wbf[AB_MAIN:AB_MAIN + N_GATES, :] = w_ref[4 * MW:4 * MW + N_GATES, :].astype(BF16)
        wbf[AB_MAIN + N_GATES:, :] = jnp.zeros((LANE - N_GATES, D), BF16)

    h = _rms_mod(_pick_rows(x_refs), g_ref, sh_ref, sc_ref).astype(BF16)
    proj_ref[...] = lax.dot_general(h, wbf[:AB_MAIN, :], NT_DIMS, preferred_element_type=F32).astype(proj_ref.dtype)
    gates_ref[...] = lax.dot_general(h, wbf[AB_MAIN:, :], NT_DIMS, preferred_element_type=F32) + b_ref[...]


def ab_in(x, g_all, mod_all, layer, w_in_t, b_gates, e):
    x_specs, x_args = _row_specs(x, D)
    b_pad = jnp.pad(b_gates, (0, LANE - N_GATES)).reshape(1, LANE)
    return pl.pallas_call(
        _ab_in_kernel,
        out_shape=[jax.ShapeDtypeStruct((N_TOK, AB_MAIN), BF16), jax.ShapeDtypeStruct((N_TOK, LANE), F32)],
        grid=(N_TOK // ROW_BM,),
        in_specs=x_specs + [_layer_vec_spec(layer), _mod_spec(layer, 0), _mod_spec(layer, 1),
                            pl.BlockSpec((None, w_in_t.shape[1], D), lambda i: (e, 0, 0),
                                         pipeline_mode=pl.Buffered(1)),
                            pl.BlockSpec((1, LANE), lambda i: (0, 0))],
        out_specs=[pl.BlockSpec((ROW_BM, AB_MAIN), lambda i: (i, 0)), pl.BlockSpec((ROW_BM, LANE), lambda i: (i, 0))],
        scratch_shapes=[pltpu.VMEM((AB_MAIN + LANE, D), BF16)],
        compiler_params=_cparams(("arbitrary",)),
        name="ab_in",
    )(*x_args, g_all.reshape(-1, 1, D), mod_all, mod_all, w_in_t, b_pad)


SEQ_BLOCK = DEC_SEQ
N_PBLK = N_P // SEQ_BLOCK


def _seq_edges(rows, is_prompt):
    r = lax.broadcasted_iota(jnp.int32, (rows, 1), 0)
    first = (r == 0) | (is_prompt & (r % SEQ == 0))
    last = (r == rows - 1) | (is_prompt & (r % SEQ == SEQ - 1))
    return first, last


def _dwconv3(x, w_ref, first, last):
    rows = x.shape[0]
    prev = jnp.where(first, 0.0, pltpu.roll(x, 1, 0))
    nxt = jnp.where(last, 0.0, pltpu.roll(x, rows - 1, 0))
    return prev * w_ref[0:1, :] + x * w_ref[1:2, :] + nxt * w_ref[2:3, :]


def _gated_gelu_of_half(y, g):
    c = math.sqrt(2.0 / math.pi)
    t = jnp.tanh(y * (2.0 * c + (8.0 * 0.044715 * c) * (y * y)))
    return (y + y * t) * g


FFN_BLOCK = 2 * DEC_SEQ
FFN_CHUNKS = 8
FFN_HALO = 8


def _ffn_up_kernel(h_ref, wa_ref, wg_ref, wc_ref, o_ref):
    is_prompt = pl.program_id(0) < N_P // FFN_BLOCK
    rows = FFN_BLOCK // FFN_CHUNKS
    wa = wa_ref[...].astype(BF16)
    wg = wg_ref[...].astype(BF16)
    wc_half = 0.5 * wc_ref[...]
    zeros = jnp.zeros((FFN_HALO, o_ref.shape[1]), F32)

    def matmuls(r):
        hr = h_ref[r * rows:(r + 1) * rows, :]
        return _dot(hr, wa), _dot(hr, wg)

    def activation(r, a_prev, a_cur, a_next, g):
        win = jnp.concatenate([zeros if a_prev is None else a_prev[rows - FFN_HALO:], a_cur,
                               zeros if a_next is None else a_next[:FFN_HALO]], axis=0)
        ridx = lax.broadcasted_iota(jnp.int32, (rows + 2 * FFN_HALO, 1), 0) + (r * rows - FFN_HALO)
        first = (ridx % DEC_SEQ == 0) | (is_prompt & (ridx % SEQ == 0))
        last = (ridx % DEC_SEQ == DEC_SEQ - 1) | (is_prompt & (ridx % SEQ == SEQ - 1))
        half_conv = _dwconv3(win, wc_half, first, last)[FFN_HALO:FFN_HALO + rows]
        o_ref[r * rows:(r + 1) * rows, :] = _gated_gelu_of_half(half_conv, g).astype(o_ref.dtype)

    acts = [matmuls(0)]
    for r in range(1, FFN_CHUNKS):
        acts.append(matmuls(r))
        activation(r - 1, acts[r - 2][0] if r >= 2 else None, acts[r - 1][0], acts[r][0], acts[r - 1][1])
    activation(FFN_CHUNKS - 1, acts[-2][0], acts[-1][0], None, acts[-1][1])


def ffn_up(h, w_up, w_conv, layer):
    tc = 256
    nct = FF // tc
    return pl.pallas_call(
        _ffn_up_kernel,
        out_shape=jax.ShapeDtypeStruct((N_TOK, FF), BF16),
        grid=(N_TOK // FFN_BLOCK, nct),
        in_specs=[
            pl.BlockSpec((FFN_BLOCK, D), lambda i, j: (i, 0)),
            pl.BlockSpec((None, D, tc), lambda i, j: (layer, 0, j)),
            pl.BlockSpec((None, D, tc), lambda i, j: (layer, 0, j + nct)),
            pl.BlockSpec((None, 3, tc), lambda i, j: (layer, 0, j)),
        ],
        out_specs=pl.BlockSpec((FFN_BLOCK, tc), lambda i, j: (i, j)),
        compiler_params=_cparams(("arbitrary", "arbitrary")),
        name="ffn_up",
    )(h, w_up, w_up, w_conv)


HEAD_PAIRS = NA_HEADS // 2
NT_DIMS = (((1,), (1,)), ((), ()))


def _pair_mask(shape):
    return lax.broadcasted_iota(jnp.int32, shape, len(shape) - 1) < NA_HD


def _one_head(x2, first):
    keep = _pair_mask(x2.shape) if first else ~_pair_mask(x2.shape)
    return jnp.where(keep, x2, jnp.zeros_like(x2))


def _ctx_attn_kernel(q_ref, k_ref, v_ref, o_ref, *, pairs):
    outs = []
    for pp in range(pairs):
        q2, k2, v2 = q_ref[pp], k_ref[pp], v_ref[pp]
        res = []
        for first in (True, False):
            s = lax.dot_general(_one_head(q2, first), k2, NT_DIMS, preferred_element_type=F32) * (NA_HD ** -0.5)
            m = jnp.max(s, axis=-1, keepdims=True)
            p = jnp.exp(s - m)
            l = jnp.sum(p, axis=-1, keepdims=True)
            res.append(_dot(p.astype(BF16), v2) / l)
        outs.append(jnp.where(_pair_mask(res[0].shape), res[0], res[1]))
    o_ref[...] = jnp.concatenate(outs, axis=-1).astype(o_ref.dtype)


def ctx_attention(qkv):
    pairs = HEAD_PAIRS
    spec = lambda part: pl.BlockSpec((None, pairs, SEQ, 2 * NA_HD), lambda b, h: (part, h, b, 0))
    return pl.pallas_call(
        functools.partial(_ctx_attn_kernel, pairs=pairs),
        out_shape=jax.ShapeDtypeStruct((N_P, D), BF16),
        grid=(BATCH, HEAD_PAIRS // pairs),
        in_specs=[spec(0), spec(1), spec(2)],
        out_specs=pl.BlockSpec((SEQ, pairs * 2 * NA_HD), lambda b, h: (b, h)),
        compiler_params=_cparams(("arbitrary", "arbitrary")),
        name="ctx_attn",
    )(qkv, qkv, qkv)


def _na_tables():
    q = np.arange(GRID_W)[:, None]
    w = np.arange(GRID_W)[None, :]
    idx_c = np.clip(w - q + (NA_KW - 1), 0, 2 * NA_KW - 2)
    onehot = (idx_c.reshape(1, -1) == np.arange(32)[:, None]).astype(np.float32)
    c_start = np.clip(np.arange(GRID_W) - NA_KW // 2, 0, GRID_W - NA_KW)[:, None]
    inside = (w >= c_start) & (w < c_start + NA_KW)
    cmask = np.where(inside, 0.0, -np.inf).astype(np.float32)
    return onehot, np.tile(cmask, (1, 2))


def _rpb_expand_kernel(r_ref, e_ref, o_ref):
    o_ref[...] = jnp.dot(r_ref[...], e_ref[...], precision=HIGHEST, preferred_element_type=F32)


def rpb_expand(rpb):
    onehot, _ = _na_tables()
    rp = jnp.pad(rpb, ((0, 0), (0, 1), (0, 1)))
    return pl.pallas_call(
        _rpb_expand_kernel,
        out_shape=jax.ShapeDtypeStruct((NA_HEADS, 16, GRID_W * GRID_W), F32),
        grid=(NA_HEADS,),
        in_specs=[pl.BlockSpec((None, 16, 32), lambda h: (h, 0, 0)),
                  pl.BlockSpec((32, GRID_W * GRID_W), lambda h: (0, 0))],
        out_specs=pl.BlockSpec((None, 16, GRID_W * GRID_W), lambda h: (h, 0, 0)),
        compiler_params=_cparams(("arbitrary",)),
        name="rpb_expand",
    )(rp, jnp.asarray(onehot))


NA_QROWS = 8
NA_WIN = NA_QROWS + NA_KH


def _na_attn_kernel(q_ref, k_ref, v_ref, kc_ref, vc_ref, t_ref, o_ref, p_loc, p_ctx):
    nq = NA_QROWS * GRID_W
    nk = NA_WIN * GRID_W
    pair = 2 * GRID_W
    nt = NT_DIMS
    lane = lax.broadcasted_iota(jnp.int32, (GRID_W, pair), 1)
    zero_tile = jnp.zeros((GRID_W, pair), BF16)
    kc = kc_ref[...].reshape(2 * NA_HD, PAST_LEN).astype(BF16)
    vc = vc_ref[...].reshape(2 * NA_HD, PAST_LEN).astype(BF16)
    for blk in range(GRID_R // NA_QROWS):
        k0 = min(max(NA_QROWS * blk - NA_KH // 2, 0), GRID_R - NA_WIN)
        q2 = q_ref[blk * nq:(blk + 1) * nq, :] * (NA_HD ** -0.5)
        kw = k_ref[k0 * GRID_W:k0 * GRID_W + nk, :]
        vw = v_ref[k0 * GRID_W:k0 * GRID_W + nk, :]
        outs = []
        for hh in range(2):
            q = _one_head(q2, hh == 0)
            s_loc = lax.dot_general(q, kw, nt, preferred_element_type=F32)
            s_ctx = _dot(q, kc)
            denoms = []
            for qi in range(NA_QROWS):
                r = NA_QROWS * blk + qi
                r_start = min(max(r - NA_KH // 2, 0), GRID_R - NA_KH)
                rows = slice(qi * GRID_W, (qi + 1) * GRID_W)
                ctx_tiles = [s_ctx[rows, c * pair:(c + 1) * pair] for c in range(PAST_LEN // pair)]
                tiles = {}
                for j in range(nk // pair):
                    kr = k0 + 2 * j
                    ok0 = r_start <= kr < r_start + NA_KH
                    ok1 = r_start <= kr + 1 < r_start + NA_KH
                    if not (ok0 or ok1):
                        continue
                    sb = s_loc[rows, j * pair:(j + 1) * pair] + t_ref[hh, kr - r + NA_KH]
                    if not (ok0 and ok1):
                        sb = jnp.where((lane < GRID_W) if ok0 else (lane >= GRID_W), sb, -jnp.inf)
                    tiles[j] = sb
                mx = functools.reduce(jnp.maximum, list(tiles.values()) + ctx_tiles)
                m = jnp.max(mx, axis=1, keepdims=True)
                acc = None
                for j in range(nk // pair):
                    if j in tiles:
                        p = jnp.exp(tiles[j] - m)
                        acc = p if acc is None else acc + p
                        p_loc[rows, j * pair:(j + 1) * pair] = p.astype(BF16)
                    else:
                        p_loc[rows, j * pair:(j + 1) * pair] = zero_tile
                for c, t in enumerate(ctx_tiles):
                    p = jnp.exp(t - m)
                    acc = acc + p
                    p_ctx[rows, c * pair:(c + 1) * pair] = p.astype(BF16)
                denoms.append(jnp.sum(acc, axis=1, keepdims=True))
            pv = _dot(p_loc[...], vw) + lax.dot_general(p_ctx[...], vc, nt, preferred_element_type=F32)
            outs.append(pv / jnp.concatenate(denoms, axis=0))
        o_ref[blk * nq:(blk + 1) * nq, :] = jnp.where(_pair_mask(outs[0].shape), outs[0], outs[1]).astype(o_ref.dtype)


def na_attention(qkv, k_ctx, v_ctx, bias_pairs):
    blk0 = N_P // DEC_SEQ
    spec = lambda part: pl.BlockSpec((None, None, DEC_SEQ, 2 * NA_HD), lambda h, b: (part, h, b + blk0, 0))
    cspec = pl.BlockSpec((None, 2, NA_HD, PAST_LEN), lambda h, b: (b, h, 0, 0))
    return pl.pallas_call(
        _na_attn_kernel,
        out_shape=jax.ShapeDtypeStruct((N_S, D), BF16),
        grid=(HEAD_PAIRS, DEC_BATCH),
        in_specs=[spec(0), spec(1), spec(2), cspec, cspec,
                  pl.BlockSpec((2, 16, GRID_W, 2 * GRID_W), lambda h, b: (h, 0, 0, 0))],
        out_specs=pl.BlockSpec((DEC_SEQ, 2 * NA_HD), lambda h, b: (b, h)),
        scratch_shapes=[pltpu.VMEM((NA_QROWS * GRID_W, NA_WIN * GRID_W), BF16),
                        pltpu.VMEM((NA_QROWS * GRID_W, PAST_LEN), BF16)],
        compiler_params=_cparams(("arbitrary", "arbitrary")),
        name="na_attn",
    )(qkv, qkv, qkv, k_ctx, v_ctx, bias_pairs)


def mixer_c(h, k_ctx, v_ctx, w_in_all, layer, rpb):
    qkv, k_new, v_new = qkv_proj(h, w_in_all, layer)
    o_p = ctx_attention(qkv)
    _, cmask2 = _na_tables()
    b15 = rpb_expand(rpb).reshape(NA_HEADS, 16, GRID_W, GRID_W)
    b17 = jnp.pad(b15, ((0, 0), (1, 0), (0, 0), (0, 0)))
    bias_pairs = jnp.concatenate([b17[:, :16], b17[:, 1:]], axis=-1) + jnp.asarray(cmask2)
    o_s = na_attention(qkv, jnp.swapaxes(k_ctx, -1, -2), jnp.swapaxes(v_ctx, -1, -2), bias_pairs)
    return (o_p, o_s), k_new, v_new


SCAN_BLOCK = HD_M


def _mlstm_kernel(*refs, seq, heads, zero_state, emit_state):
    q_ref, k_ref, v_ref, og_ref, gates_ref, wq_ref, wk_ref, gh_ref = refs[:8]
    pos = 8
    if not zero_state:
        c0_ref, n0_ref, m0_ref = refs[pos:pos + 3]
        pos += 3
    y_ref = refs[pos]
    pos += 1
    if emit_state:
        cn_ref, nn_ref, mn_ref = refs[pos:pos + 3]
        pos += 3
    pre_s, suf_s, gt_s, pret_s, suft_s, kv_s, ks_s, be_s, mk_s, cp_s, np_s, mp_s, cst, nst, mst = refs[pos:]

    blk = SCAN_BLOCK
    nc = seq // blk
    nbatch = nc * heads
    n_gates = 4 * HEADS_M
    r = lax.broadcasted_iota(jnp.int32, (seq, 1), 0)

    @pl.when(pl.program_id(1) == 0)
    def _():
        g_all = gates_ref[...]
        lf = jax.nn.log_sigmoid(g_all)
        rin = r % blk
        pre, suf = lf, lf
        for sh in [1 << i for i in range(blk.bit_length() - 1)]:
            pre = pre + jnp.where(rin >= sh, pltpu.roll(pre, sh, 0), 0.0)
            suf = suf + jnp.where(rin < blk - sh, pltpu.roll(suf, seq - sh, 0), 0.0)
        pre3, suf3 = pre.reshape(nc, blk, LANE), suf.reshape(nc, blk, LANE)
        pre_s[...] = pre3
        suf_s[...] = suf3
        gt_s[...] = jnp.swapaxes(g_all.reshape(nc, blk, LANE), 1, 2)[:, :n_gates, :]
        pret_s[...] = jnp.swapaxes(pre3, 1, 2)[:, :n_gates, :]
        suft_s[...] = jnp.swapaxes(suf3, 1, 2)[:, :n_gates, :]

    def split_heads(x):
        x3 = x.reshape(nc, blk, heads * HD_M)
        if heads == 1:
            return x3
        tiles = jnp.stack([x3[:, :, hh * HD_M:(hh + 1) * HD_M] for hh in range(heads)], axis=1)
        return tiles.reshape(nbatch, blk, HD_M)

    def per_head(x):
        if heads == 1:
            return x
        return jnp.broadcast_to(x[:, None], (nc, heads) + x.shape[1:]).reshape((nbatch,) + x.shape[1:])

    first, last = r == 0, r == seq - 1
    qc_all = _dwconv3(q_ref[...].astype(F32), wq_ref, first, last)
    q3 = split_heads(qc_all * jax.nn.sigmoid(qc_all))
    kc_all = _dwconv3(k_ref[...].astype(F32), wk_ref, first, last)
    k3 = split_heads(kc_all * jax.nn.sigmoid(kc_all) * (HD_M ** -0.5))
    qb, kb = q3.astype(BF16), k3.astype(BF16)
    vb = split_heads(v_ref[...])
    g3 = per_head(gates_ref[...].reshape(nc, blk, LANE))
    gt3 = per_head(gt_s[...])

    if zero_state:
        cst[...] = jnp.zeros_like(cst)
        nst[...] = jnp.zeros_like(nst)
        mst[...] = jnp.zeros_like(mst)
    else:
        cst[...] = c0_ref[...]
        nst[...] = n0_ref[...]
        mst[...] = jnp.broadcast_to(m0_ref[...], mst.shape)

    tt = lax.broadcasted_iota(jnp.int32, (1, blk, blk), 1)
    ss = lax.broadcasted_iota(jnp.int32, (1, blk, blk), 2)
    lane = lax.broadcasted_iota(jnp.int32, (1, 1, LANE), 2)
    sub = lax.broadcasted_iota(jnp.int32, (1, n_gates, 1), 1)
    head = lax.broadcasted_iota(jnp.int32, (nbatch, 1, 1), 0) % heads + pl.program_id(1) * heads
    hsum = None
    for d in range(2):
        i_idx = d * 2 * HEADS_M + head
        f_idx = i_idx + HEADS_M
        mask = (ss <= tt) if d == 0 else (ss >= tt)
        b3 = per_head((pre_s if d == 0 else suf_s)[...])
        bt3 = per_head((pret_s if d == 0 else suft_s)[...])
        bcol = jnp.sum(jnp.where(lane == f_idx, b3, 0.0), axis=2, keepdims=True)
        icol = jnp.sum(jnp.where(lane == i_idx, g3, 0.0), axis=2, keepdims=True)
        brow = jnp.sum(jnp.where(sub == f_idx, bt3, 0.0), axis=1, keepdims=True)
        irow = jnp.sum(jnp.where(sub == i_idx, gt3, 0.0), axis=1, keepdims=True)
        bend = bcol[:, blk - 1:blk, :] if d == 0 else bcol[:, 0:1, :]

        dmat = jnp.where(mask, bcol - brow + irow, -jnp.inf)
        mloc = jnp.max(dmat, axis=2, keepdims=True)
        qk = jnp.einsum('ctd,csd->cts', qb, kb, preferred_element_type=F32)
        s_loc = jnp.exp(dmat - mloc) * qk
        num_loc = jnp.einsum('cts,csd->ctd', s_loc.astype(BF16), vb, preferred_element_type=F32)
        den_loc = jnp.sum(s_loc, axis=2, keepdims=True)
        to_end = bend - bcol + icol
        mk = jnp.max(to_end, axis=1, keepdims=True)
        kw = k3 * jnp.exp(to_end - mk)
        kv_s[...] = jnp.einsum('cds,cse->cde', jnp.swapaxes(kw, 1, 2).astype(BF16), vb,
                               preferred_element_type=F32)
        ks_s[...] = jnp.sum(kw, axis=1, keepdims=True)
        be_s[...] = jnp.broadcast_to(bend, be_s.shape)
        mk_s[...] = jnp.broadcast_to(mk, mk_s.shape)

        def step(j, carry, d=d):
            c = j if d == 0 else nc - 1 - j
            sl = pl.ds(c * heads, heads)
            m_prev, c_prev, n_prev = mst[d], cst[d], nst[d]
            cp_s[sl] = c_prev.astype(BF16)
            np_s[sl] = n_prev
            mp_s[sl] = m_prev
            be, mkc = be_s[sl], mk_s[sl]
            m_new = jnp.maximum(be + m_prev, mkc)
            keep = jnp.exp(be + m_prev - m_new)
            add = jnp.exp(mkc - m_new)
            cst[d] = keep * c_prev + add * kv_s[sl]
            nst[d] = keep * n_prev + add * ks_s[sl]
            mst[d] = m_new
            return carry

        lax.fori_loop(0, nc, step, 0)

        m_inter = bcol + mp_s[...][:, :, 0:1]
        m_t = jnp.maximum(m_inter, mloc)
        w_state = jnp.exp(m_inter - m_t)
        w_loc = jnp.exp(mloc - m_t)
        inter = jnp.einsum('ctd,cde->cte', qb, cp_s[...], preferred_element_type=F32)
        num = w_state * inter + w_loc * num_loc
        den = w_state * jnp.sum(q3 * np_s[...], axis=2, keepdims=True) + w_loc * den_loc
        h = num / jnp.maximum(jnp.abs(den), jnp.exp(-m_t))
        hsum = h if hsum is None else hsum + h

    hn = hsum * lax.rsqrt(jnp.mean(hsum * hsum, axis=-1, keepdims=True) + EPS)
    og3 = split_heads(og_ref[...].astype(F32))
    for c in range(nc):
        for hh in range(heads):
            cols = slice(hh * HD_M, (hh + 1) * HD_M)
            e = c * heads + hh
            y_ref[c * blk:(c + 1) * blk, cols] = (hn[e] * gh_ref[:, cols] * jax.nn.sigmoid(og3[e])).astype(y_ref.dtype)
    if emit_state:
        cn_ref[...] = cst[...]
        nn_ref[...] = nst[...]
        mn_ref[...] = mst[...]


def mlstm(proj, gates, w_conv_qk, g_head, state, *, nb, seq, row0, heads):
    blk0 = row0 // seq
    nblk = seq // SCAN_BLOCK
    groups = HEADS_M // heads
    n_gates = 4 * HEADS_M
    zero_state = state is None
    col = lambda part: pl.BlockSpec((seq, heads * HD_M), lambda b, h: (b + blk0, part * groups + h))
    in_specs = [col(0), col(1), col(2), col(3),
                pl.BlockSpec((seq, LANE), lambda b, h: (b + blk0, 0)),
                pl.BlockSpec((3, heads * HD_M), lambda b, h: (0, h)),
                pl.BlockSpec((3, heads * HD_M), lambda b, h: (0, groups + h)),
                pl.BlockSpec((1, heads * HD_M), lambda b, h: (0, h))]
    args = [proj, proj, proj, proj, gates, w_conv_qk, w_conv_qk, g_head.reshape(1, MW)]
    state_specs = [pl.BlockSpec((None, 2, heads, HD_M, HD_M), lambda b, h: (b, 0, h, 0, 0)),
                   pl.BlockSpec((None, 2, heads, 1, HD_M), lambda b, h: (b, 0, h, 0, 0))]
    if not zero_state:
        c0, n0, m0 = state
        in_specs += state_specs + [pl.BlockSpec((None, 2, heads, 1, 1), lambda b, h: (b, 0, h, 0, 0))]
        args += [c0, n0.reshape(nb, 2, HEADS_M, 1, HD_M), m0.reshape(nb, 2, HEADS_M, 1, 1)]
    out_shape = [jax.ShapeDtypeStruct((nb * seq, MW), BF16)]
    out_specs = [pl.BlockSpec((seq, heads * HD_M), lambda b, h: (b, h))]
    if zero_state:
        out_shape += [jax.ShapeDtypeStruct((nb, 2, HEADS_M, HD_M, HD_M), F32),
                      jax.ShapeDtypeStruct((nb, 2, HEADS_M, 1, HD_M), F32),
                      jax.ShapeDtypeStruct((nb, 2, HEADS_M, 1, LANE), F32)]
        out_specs += state_specs + [pl.BlockSpec((None, 2, heads, 1, LANE), lambda b, h: (b, 0, h, 0, 0))]
    per_block = lambda *shape: pltpu.VMEM((nblk,) + shape, F32)
    per_entry = lambda *shape, dtype=F32: pltpu.VMEM((nblk * heads,) + shape, dtype)
    return pl.pallas_call(
        functools.partial(_mlstm_kernel, seq=seq, heads=heads, zero_state=zero_state, emit_state=zero_state),
        out_shape=out_shape,
        grid=(nb, groups),
        in_specs=in_specs,
        out_specs=out_specs,
        scratch_shapes=[per_block(SCAN_BLOCK, LANE), per_block(SCAN_BLOCK, LANE),
                        per_block(n_gates, SCAN_BLOCK), per_block(n_gates, SCAN_BLOCK), per_block(n_gates, SCAN_BLOCK),
                        per_entry(HD_M, HD_M), per_entry(1, HD_M), per_entry(1, LANE), per_entry(1, LANE),
                        per_entry(HD_M, HD_M, dtype=BF16), per_entry(1, HD_M), per_entry(1, LANE),
                        pltpu.VMEM((2, heads, HD_M, HD_M), F32), pltpu.VMEM((2, heads, 1, HD_M), F32),
                        pltpu.VMEM((2, heads, 1, LANE), F32)],
        compiler_params=_cparams(("arbitrary", "arbitrary")),
        name="mlstm_%d" % seq,
    )(*args)


def _hyena_pre_kernel(v_ref, x1_ref, x2_ref, wv_ref, w1_ref, w2_ref, u_ref, x2c_ref):
    first, last = _seq_edges(SEQ_BLOCK, pl.program_id(0) < N_PBLK)
    x1c = _dwconv3(x1_ref[...].astype(F32), w1_ref, first, last)
    u_ref[...] = (x1c * _dwconv3(v_ref[...].astype(F32), wv_ref, first, last)).astype(u_ref.dtype)
    x2c_ref[...] = _dwconv3(x2_ref[...].astype(F32), w2_ref, first, last).astype(x2c_ref.dtype)


def hyena_pre(proj, w_conv_hy):
    tc = 256
    nct = HW // tc
    c0 = 4 * MW // tc
    pcol = lambda part: pl.BlockSpec((SEQ_BLOCK, tc), lambda i, j: (i, c0 + part * nct + j))
    wcol = lambda part: pl.BlockSpec((3, tc), lambda i, j: (0, part * nct + j))
    out = jax.ShapeDtypeStruct((N_TOK, HW), BF16)
    ospec = pl.BlockSpec((SEQ_BLOCK, tc), lambda i, j: (i, j))
    return pl.pallas_call(
        _hyena_pre_kernel,
        out_shape=[out, out],
        grid=(N_TOK // SEQ_BLOCK, nct),
        in_specs=[pcol(0), pcol(1), pcol(2), wcol(0), wcol(1), wcol(2)],
        out_specs=[ospec, ospec],
        compiler_params=_cparams(("arbitrary", "arbitrary")),
        name="hyena_pre",
    )(proj, proj, proj, w_conv_hy, w_conv_hy, w_conv_hy)


@functools.lru_cache(maxsize=None)
def _filter_tables(seq):
    t = np.linspace(0.0, 1.0, seq)[:, None]
    wpos = 2.0 * np.pi * np.arange(seq)[:, None] / seq
    bands = np.linspace(1e-4, N_BANDS - 1, N_BANDS)[None, :]
    z = np.concatenate([t, np.cos(bands * wpos), -np.sin(bands * wpos)], axis=-1)
    z = np.pad(z, ((0, 0), (0, LANE - FILTER_EMB)))
    max_decay = math.log(DECAY_TARGET) / DECAY_FAST
    min_decay = math.log(DECAY_TARGET) / DECAY_SLOW
    deltas = np.abs(np.linspace(min_decay, max_decay, HW))
    decay = np.exp(-t * np.concatenate([deltas, deltas])[None, :])
    return z.astype(np.float32), decay.astype(np.float32)


@functools.lru_cache(maxsize=None)
def _dft_tables(seq, tk):
    n = 2 * seq
    k = np.arange(seq)[:, None]
    t = np.arange(seq)[None, :]
    ang = 2.0 * np.pi * ((k * t) % n) / n
    alt = np.where(np.arange(seq) % 2 == 0, 1.0, -1.0)
    cm, sm = np.cos(ang), np.sin(ang)
    sm[0, :] = alt
    fwd = np.stack([cm.reshape(seq // tk, tk, seq), sm.reshape(seq // tk, tk, seq)], axis=1)
    wk = np.where(np.arange(seq) == 0, 1.0, 2.0)[None, :]
    ci = (np.cos(ang.T) * wk) / n
    si = np.sin(ang.T) * 2.0 / n
    si[:, 0] = alt / n
    inv = np.concatenate([ci, si], axis=1)
    return fwd.astype(np.float32), inv.astype(np.float32)


def _filter_kernel(z_ref, w1_ref, b1_ref, w2_ref, b2_ref, w3_ref, fr_ref, dec_ref, hs_ref, hd_ref):
    fr = fr_ref[...]
    hp = functools.partial(jnp.dot, precision=HIGHEST, preferred_element_type=F32)
    h1 = jnp.sin(fr * (hp(z_ref[...], w1_ref[...]) + b1_ref[...]))
    h2 = jnp.sin(fr * (hp(h1, w2_ref[...]) + b2_ref[...]))
    w3 = w3_ref[...]
    h_hi, w_hi = h2.astype(BF16), w3.astype(BF16)
    h_lo, w_lo = (h2 - h_hi.astype(F32)).astype(BF16), (w3 - w_hi.astype(F32)).astype(BF16)
    filt = (_dot(h_hi, w_hi) + _dot(h_hi, w_lo) + _dot(h_lo, w_hi)) * dec_ref[...]
    past, fut = filt[:, :HW], filt[:, HW:]
    rows = filt.shape[0]
    grow = lax.broadcasted_iota(jnp.int32, (rows, 1), 0) + pl.program_id(0) * rows
    fut = jnp.where(grow == 0, 0.0, fut)
    hs_ref[...] = past + fut
    hd_ref[...] = past - fut


def filter_gen(seq, w1, b1, w2, b2, w3, freq):
    z, decay = _filter_tables(seq)
    tl = 256
    fh = FILTER_HIDDEN
    full = lambda shape: pl.BlockSpec(shape, lambda i: (0, 0))
    out = jax.ShapeDtypeStruct((seq, HW), F32)
    return pl.pallas_call(
        _filter_kernel,
        out_shape=[out, out],
        grid=(seq // tl,),
        in_specs=[pl.BlockSpec((tl, LANE), lambda i: (i, 0)), full((LANE, fh)), full((1, fh)), full((fh, fh)),
                  full((1, fh)), full((fh, 2 * HW)), full((1, fh)), pl.BlockSpec((tl, 2 * HW), lambda i: (i, 0))],
        out_specs=[pl.BlockSpec((tl, HW), lambda i: (i, 0))] * 2,
        compiler_params=_cparams(("arbitrary",)),
        name="filter_gen",
    )(jnp.asarray(z), jnp.pad(w1, ((0, LANE - FILTER_EMB), (0, 0))), b1.reshape(1, fh), w2, b2.reshape(1, fh), w3,
      freq.reshape(1, fh), jnp.asarray(decay))


def _dft_filter_kernel(a_ref, hs_ref, hd_ref, k_ref, hs_bf, hd_bf):
    @pl.when(pl.program_id(0) == 0)
    def _():
        hs_bf[...] = hs_ref[...].astype(BF16)
        hd_bf[...] = hd_ref[...].astype(BF16)

    k_ref[0] = _dot(a_ref[0].astype(BF16), hs_bf[...])
    k_ref[1] = _dot(a_ref[1].astype(BF16), hd_bf[...])

    @pl.when(pl.program_id(0) == 0)
    def _():
        k_ref[1, 0:1, :] = _dot(a_ref[1, 0:8, :].astype(BF16), hs_bf[...])[0:1, :]


def dft_filter(seq, tk, hs, hd):
    fwd, _ = _dft_tables(seq, tk)
    return pl.pallas_call(
        _dft_filter_kernel,
        out_shape=jax.ShapeDtypeStruct((2, seq, HW), F32),
        grid=(seq // tk,),
        in_specs=[pl.BlockSpec((None, 2, tk, seq), lambda m: (m, 0, 0, 0)),
                  pl.BlockSpec((seq, HW), lambda m: (0, 0)), pl.BlockSpec((seq, HW), lambda m: (0, 0))],
        out_specs=pl.BlockSpec((2, tk, HW), lambda m: (0, m, 0)),
        scratch_shapes=[pltpu.VMEM((seq, HW), BF16), pltpu.VMEM((seq, HW), BF16)],
        compiler_params=_cparams(("arbitrary",)),
        name="dft_filter",
    )(jnp.asarray(fwd), hs, hd)


def _dft_fwd_kernel(a_ref, u_ref, k_ref, y_ref, u_bf, tbl):
    m = pl.program_id(1)

    @pl.when(m == 0)
    def _():
        u_bf[...] = u_ref[...].astype(BF16)

    @pl.when(pl.program_id(0) == 0)
    def _():
        tbl[m] = a_ref[...].astype(BF16)

    ure = _dot(tbl[m, 0], u_bf[...])
    uim = _dot(tbl[m, 1], u_bf[...])
    kre, kim = k_ref[0], k_ref[1]
    packed = (lax.broadcasted_iota(jnp.int32, (ure.shape[0], 1), 0) == 0) & (m == 0)
    y_ref[0] = jnp.where(packed, ure * kre, ure * kre - uim * kim).astype(y_ref.dtype)
    y_ref[1] = jnp.where(packed, uim * kim, ure * kim + uim * kre).astype(y_ref.dtype)


def dft_fwd(seq, tk, u, kf, *, nb, row0):
    fwd, _ = _dft_tables(seq, tk)
    blk0 = row0 // seq
    nm = seq // tk
    return pl.pallas_call(
        _dft_fwd_kernel,
        out_shape=jax.ShapeDtypeStruct((nb, 2, seq, HW), BF16),
        grid=(nb, nm),
        in_specs=[pl.BlockSpec((None, 2, tk, seq), lambda b, m: (jnp.where(b == 0, m, nm - 1), 0, 0, 0)),
                  pl.BlockSpec((seq, HW), lambda b, m: (b + blk0, 0)),
                  pl.BlockSpec((2, tk, HW), lambda b, m: (0, m, 0))],
        out_specs=pl.BlockSpec((None, 2, tk, HW), lambda b, m: (b, 0, m, 0)),
        scratch_shapes=[pltpu.VMEM((seq, HW), BF16), pltpu.VMEM((nm, 2, tk, seq), BF16)],
        compiler_params=_cparams(("arbitrary", "arbitrary"), DFT_VMEM_LIMIT),
        name="dft_fwd",
    )(jnp.asarray(fwd), u, kf)


def _dft_inv_kernel(a_ref, y_ref, u_ref, x2_ref, bias_ref, o_ref, tbl, *, seq):
    t = pl.program_id(1)

    @pl.when(pl.program_id(0) == 0)
    def _():
        tbl[t] = a_ref[...].astype(BF16)

    conv = _dot(tbl[t, :, :seq], y_ref[0]) + _dot(tbl[t, :, seq:], y_ref[1])
    u = u_ref[...].astype(F32)
    o_ref[...] = (x2_ref[...].astype(F32) * (conv + bias_ref[...] * u)).astype(o_ref.dtype)


def dft_inv(seq, tk, y, u, x2c, bias, *, nb, row0):
    _, inv = _dft_tables(seq, tk)
    tm = min(seq, 512)
    nt = seq // tm
    blk0 = row0 // tm
    rows = lambda b, t: (b * nt + t + blk0, 0)
    return pl.pallas_call(
        functools.partial(_dft_inv_kernel, seq=seq),
        out_shape=jax.ShapeDtypeStruct((nb * seq, HW), BF16),
        grid=(nb, nt),
        in_specs=[pl.BlockSpec((tm, 2 * seq), lambda b, t: (jnp.where(b == 0, t, nt - 1), 0)),
                  pl.BlockSpec((None, 2, seq, HW), lambda b, t: (b, 0, 0, 0)),
                  pl.BlockSpec((tm, HW), rows), pl.BlockSpec((tm, HW), rows),
                  pl.BlockSpec((1, HW), lambda b, t: (0, 0))],
        out_specs=pl.BlockSpec((tm, HW), lambda b, t: (b * nt + t, 0)),
        scratch_shapes=[pltpu.VMEM((nt, tm, 2 * seq), BF16)],
        compiler_params=_cparams(("arbitrary", "arbitrary"), DFT_VMEM_LIMIT),
        name="dft_inv",
    )(jnp.asarray(inv), y, u, x2c, bias.reshape(1, HW))


def hyena_group(seq, u, x2c, filt_w, bias, *, nb, row0):
    tk = min(seq, 512)
    hs, hd = filter_gen(seq, *filt_w)
    kf = dft_filter(seq, tk, hs, hd)
    y = dft_fwd(seq, tk, u, kf, nb=nb, row0=row0)
    return dft_inv(seq, tk, y, u, x2c, bias, nb=nb, row0=row0)


def mixer_ab_parts(proj, gates, state_s, w_conv_qk, g_head, w_conv_hy, w_f1, b_f1, w_f2, b_f2, w_f3, freq, hy_bias):
    ym_p, c_p, n_p, m_p = mlstm(proj, gates, w_conv_qk, g_head, None, nb=BATCH, seq=SEQ, row0=0, heads=HEADS_M)
    (ym_s,) = mlstm(proj, gates, w_conv_qk, g_head, state_s, nb=DEC_BATCH, seq=DEC_SEQ, row0=N_P, heads=1)
    u, x2c = hyena_pre(proj, w_conv_hy)
    filt_w = (w_f1, b_f1, w_f2, b_f2, w_f3, freq)
    yh_p = hyena_group(SEQ, u, x2c, filt_w, hy_bias, nb=BATCH, row0=0)
    yh_s = hyena_group(DEC_SEQ, u, x2c, filt_w, hy_bias, nb=DEC_BATCH, row0=N_P)
    state_p = (c_p, n_p[:, :, :, 0, :], m_p[:, :, :, 0, 0])
    return (ym_p, ym_s), (yh_p, yh_s), state_p


def kernel(x_prompt, x_sample, state_mlstm_C, state_mlstm_n, state_mlstm_m, cache_na_k, cache_na_v, c, c_ctx, w_ada, b_ada, g_mix, g_ffn, g_final, w_in_ab, b_gates, w_conv_qk, g_mlstm, w_conv_hy, w_filt1, b_filt1, w_filt2, b_filt2, w_filt3, filt_freq, hyena_bias, w_out_ab, w_in_c, rpb_c, w_out_c, w_up, w_conv_ffn, w_down):
    cmat = jnp.concatenate([c, c_ctx[None, :], jnp.zeros((MOD_ROWS - DEC_BATCH - 1, D), F32)], axis=0)
    mod_all = adaln_all(cmat, w_ada, b_ada).reshape(DEPTH, MOD_ROWS, 6, 1, D)
    x = (x_prompt.reshape(N_P, D), x_sample.reshape(N_S, D))
    h = None
    new_c, new_n, new_m, new_k, new_v = [], [], [], [], []
    for l in range(DEPTH):
        e = l // 2
        if l % 2 == 0:
            assert l == 0, "a later mixer A/B layer would take the fused norm of the layer before it"
            proj, gates = ab_in(x, g_mix, mod_all, l, jnp.swapaxes(w_in_ab, 1, 2), b_gates[e], e)
            state_s = (state_mlstm_C[:, e], state_mlstm_n[:, e], state_mlstm_m[:, e])
            y_m, y_h, (c_p, n_p, m_p) = mixer_ab_parts(
                proj, gates, state_s, w_conv_qk[e], g_mlstm[e], w_conv_hy[e], w_filt1[e], b_filt1[e],
                w_filt2[e], b_filt2[e], w_filt3[e], filt_freq[e], hyena_bias[e])
            a_list = [y_m, y_h]
            w_list = [(w_out_ab, (None, MW, D), lambda i, e=e: (e, 0, 0)),
                      (w_out_ab, (None, HW, D), lambda i, e=e: (e, 1, 0))]
            new_c.append(c_p[:, None])
            new_n.append(n_p[:, None])
            new_m.append(m_p[:, None])
        else:
            o, k_new, v_new = mixer_c(h, cache_na_k[:, e], cache_na_v[:, e], w_in_c, e, rpb_c[e])
            a_list = [o]
            w_list = [(w_out_c, (None, D, D), lambda i, e=e: (e, 0, 0))]
            new_k.append(k_new)
            new_v.append(v_new)
        x, h = out_proj(a_list, w_list, x, mod_all, l, 2, g_ffn, l, (l, 3, 4), bm=2 * ROW_BM, name="mixer_out")
        mid = ffn_up(h, w_up, w_conv_ffn, l)
        w_list = [(w_down, (None, FF, D), lambda i, l=l: (l, 0, 0))]
        if l + 1 < DEPTH:
            x, h = out_proj([mid], w_list, x, mod_all, l, 5, g_mix, l + 1, (l + 1, 0, 1), name="ffn_down")
        else:
            y_p, y_s = out_proj([mid], w_list, x, mod_all, l, 5, g_final, 0, None, name="ffn_down_final")
    cat = lambda parts: parts[0] if len(parts) == 1 else jnp.concatenate(parts, axis=1)
    return (y_p.reshape(BATCH, SEQ, D), y_s.reshape(DEC_BATCH, DEC_SEQ, D), cat(new_c), cat(new_n), cat(new_m),
            cat(new_k), cat(new_v))
```

```python
import functools
import math

import numpy as np
import jax
import jax.numpy as jnp
from jax import lax
from jax.experimental import pallas as pl
from jax.experimental.pallas import tpu as pltpu

F32 = jnp.float32
BF16 = jnp.bfloat16

D = 1024
BATCH, SEQ = 16, 256
DEC_BATCH, DEC_SEQ = 4, 2048
PAST_LEN = 512
DEPTH = 2
GRID_W = 64
GRID_R = DEC_SEQ // GRID_W
HEADS_M = 4
MW = D // 2
HD_M = MW // HEADS_M
HW = D // 2
N_BANDS = 16
FILTER_EMB = 2 * N_BANDS + 1
FILTER_HIDDEN = 64
DECAY_FAST, DECAY_SLOW, DECAY_TARGET = 0.3, 1.5, 1e-2
NA_HEADS = 16
NA_HD = D // NA_HEADS
NA_KH, NA_KW = 8, 16
FF = 2816
EPS = 1e-6

N_P = BATCH * SEQ
N_S = DEC_BATCH * DEC_SEQ
N_TOK = N_P + N_S
CTX_ROW = DEC_BATCH
MOD_ROWS = 8
LANE = 128
VMEM_LIMIT = 48 * 1024 * 1024
QKV_VMEM_LIMIT = 56 * 1024 * 1024
DFT_VMEM_LIMIT = 56 * 1024 * 1024
HIGHEST = lax.Precision.HIGHEST


def _cparams(sem, vmem_limit=VMEM_LIMIT):
    return pltpu.CompilerParams(dimension_semantics=sem, vmem_limit_bytes=vmem_limit)


def _mod_row(i, bm):
    return jnp.where(i < N_P // bm, CTX_ROW, (i - N_P // bm) // (DEC_SEQ // bm))


def _dot(a, b):
    return jnp.dot(a, b, preferred_element_type=F32)


def _adaln_kernel(c_ref, w_ref, b_ref, o_ref):
    cv = c_ref[...]
    s = cv * jax.nn.sigmoid(cv)
    o_ref[...] = _dot(s.astype(BF16), w_ref[...].astype(BF16)) + b_ref[...]


def adaln_all(cmat, w_ada, b_ada):
    tn = 1024
    return pl.pallas_call(
        _adaln_kernel,
        out_shape=jax.ShapeDtypeStruct((DEPTH, MOD_ROWS, 6 * D), F32),
        grid=(DEPTH, 6 * D // tn),
        in_specs=[
            pl.BlockSpec((MOD_ROWS, D), lambda l, j: (0, 0)),
            pl.BlockSpec((None, D, tn), lambda l, j: (l, 0, j)),
            pl.BlockSpec((None, 1, tn), lambda l, j: (l, 0, j)),
        ],
        out_specs=pl.BlockSpec((None, MOD_ROWS, tn), lambda l, j: (l, 0, j)),
        compiler_params=_cparams(("arbitrary", "arbitrary")),
        name="adaln",
    )(cmat, w_ada, b_ada.reshape(DEPTH, 1, 6 * D))


ROW_BM = 512
ROW_NPB = N_P // ROW_BM


def _row_specs(arr, cols, bm=ROW_BM):
    npb = N_P // bm
    if isinstance(arr, tuple):
        return ([pl.BlockSpec((bm, cols), lambda i: (jnp.minimum(i, npb - 1), 0)),
                 pl.BlockSpec((bm, cols), lambda i: (jnp.maximum(i - npb, 0), 0))], list(arr))
    return [pl.BlockSpec((bm, cols), lambda i: (i, 0))], [arr]


def _pick_rows(refs, npb=ROW_NPB):
    if len(refs) == 1:
        return refs[0][...]
    return jnp.where(pl.program_id(0) < npb, refs[0][...], refs[1][...])


def _mod_spec(layer, which, bm=ROW_BM):
    return pl.BlockSpec((None, None, None, 1, D), lambda i: (layer, _mod_row(i, bm), which, 0, 0))


def _layer_vec_spec(layer):
    return pl.BlockSpec((None, 1, D), lambda i: (layer, 0, 0))


def _rms_mod(x, g_ref, sh_ref, sc_ref):
    y = x * lax.rsqrt(jnp.mean(x * x, axis=-1, keepdims=True) + EPS)
    return (y * g_ref[...]) * (1.0 + sc_ref[...]) + sh_ref[...]


def _out_proj_kernel(*refs, a_counts, n_x, final, npb):
    pos = 0
    a_groups = []
    for cnt in a_counts:
        a_groups.append(refs[pos:pos + cnt])
        pos += cnt
    w_refs = refs[pos:pos + len(a_counts)]
    pos += len(a_counts)
    x_refs = refs[pos:pos + n_x]
    pos += n_x
    gt_ref, g_ref = refs[pos:pos + 2]
    pos += 2
    if not final:
        sh_ref, sc_ref = refs[pos:pos + 2]
        pos += 2
    out_a, out_b = refs[pos:pos + 2]
    wbf = refs[pos + 2:]

    @pl.when(pl.program_id(0) == 0)
    def _():
        for w_ref, wb in zip(w_refs, wbf):
            wb[...] = w_ref[...].astype(BF16)

    acc = None
    for group, wb in zip(a_groups, wbf):
        part = _dot(_pick_rows(group, npb), wb[...])
        acc = part if acc is None else acc + part
    xn = _pick_rows(x_refs, npb) + gt_ref[...] * acc
    if final:
        y = xn * lax.rsqrt(jnp.mean(xn * xn, axis=-1, keepdims=True) + EPS) * g_ref[...]

        @pl.when(pl.program_id(0) < npb)
        def _():
            out_a[...] = y

        @pl.when(pl.program_id(0) >= npb)
        def _():
            out_b[...] = y
    else:
        out_a[...] = xn
        out_b[...] = _rms_mod(xn, g_ref, sh_ref, sc_ref).astype(out_b.dtype)


def out_proj(a_list, w_list, x, mod_all, layer, gt_idx, g_all, g_layer, norm_mod_idx=None, bm=ROW_BM,
             vmem_limit=VMEM_LIMIT, name="out_proj"):
    final = norm_mod_idx is None
    npb = N_P // bm
    in_specs, args, a_counts = [], [], []
    for a in a_list:
        cols = (a[0] if isinstance(a, tuple) else a).shape[1]
        specs, ops = _row_specs(a, cols, bm)
        in_specs += specs
        args += ops
        a_counts.append(len(ops))
    w_shapes = []
    for w, block, imap in w_list:
        in_specs.append(pl.BlockSpec(block, imap, pipeline_mode=pl.Buffered(1)))
        args.append(w)
        w_shapes.append(tuple(b for b in block if b is not None))
    x_specs, x_args = _row_specs(x, D, bm)
    in_specs += x_specs + [_mod_spec(layer, gt_idx, bm), _layer_vec_spec(g_layer)]
    args += x_args + [mod_all, g_all.reshape(-1, 1, D)]
    if final:
        out_shape = [jax.ShapeDtypeStruct((N_P, D), F32), jax.ShapeDtypeStruct((N_S, D), F32)]
        out_specs = [pl.BlockSpec((bm, D), lambda i: (jnp.minimum(i, npb - 1), 0)),
                     pl.BlockSpec((bm, D), lambda i: (jnp.maximum(i - npb, 0), 0))]
    else:
        n_layer, sh_idx, sc_idx = norm_mod_idx
        in_specs += [_mod_spec(n_layer, sh_idx, bm), _mod_spec(n_layer, sc_idx, bm)]
        args += [mod_all, mod_all]
        out_shape = [jax.ShapeDtypeStruct((N_TOK, D), F32), jax.ShapeDtypeStruct((N_TOK, D), BF16)]
        out_specs = [pl.BlockSpec((bm, D), lambda i: (i, 0))] * 2
    return pl.pallas_call(
        functools.partial(_out_proj_kernel, a_counts=tuple(a_counts), n_x=len(x_args), final=final, npb=npb),
        out_shape=out_shape,
        grid=(N_TOK // bm,),
        in_specs=in_specs,
        out_specs=out_specs,
        scratch_shapes=[pltpu.VMEM(s, BF16) for s in w_shapes],
        compiler_params=_cparams(("arbitrary",), vmem_limit),
        name=name,
    )(*args)


def _qkv_kernel(a_ref, w_ref, o_ref, kc_ref, vc_ref, wbf):
    @pl.when(pl.program_id(0) == 0)
    def _():
        wbf[...] = w_ref[...].astype(BF16)

    a = a_ref[...]
    for part in range(3):
        acc = _dot(a, wbf[:, part * D:(part + 1) * D])
        for pp in range(NA_HEADS // 2):
            o_ref[part, pp] = acc[:, pp * 2 * NA_HD:(pp + 1) * 2 * NA_HD].astype(o_ref.dtype)
        if part > 0:
            c_ref = kc_ref if part == 1 else vc_ref

            @pl.when(pl.program_id(0) < ROW_NPB)
            def _(acc=acc, c_ref=c_ref):
                for b in range(ROW_BM // SEQ):
                    for hh in range(NA_HEADS):
                        c_ref[b, hh] = acc[b * SEQ:(b + 1) * SEQ, hh * NA_HD:(hh + 1) * NA_HD]


def qkv_proj(h, w_in, layer):
    seqs = ROW_BM // SEQ
    cache = jax.ShapeDtypeStruct((BATCH, 1, NA_HEADS, SEQ, NA_HD), F32)
    cache_spec = pl.BlockSpec((seqs, None, NA_HEADS, SEQ, NA_HD),
                              lambda i: (jnp.minimum(i, ROW_NPB - 1), 0, 0, 0, 0))
    return pl.pallas_call(
        _qkv_kernel,
        out_shape=[jax.ShapeDtypeStruct((3, NA_HEADS // 2, N_TOK, 2 * NA_HD), BF16), cache, cache],
        grid=(N_TOK // ROW_BM,),
        in_specs=[pl.BlockSpec((ROW_BM, D), lambda i: (i, 0)),
                  pl.BlockSpec((None, D, 3 * D), lambda i: (layer, 0, 0), pipeline_mode=pl.Buffered(1))],
        out_specs=[pl.BlockSpec((3, NA_HEADS // 2, ROW_BM, 2 * NA_HD), lambda i: (0, 0, i, 0)),
                   cache_spec, cache_spec],
        scratch_shapes=[pltpu.VMEM((D, 3 * D), BF16)],
        compiler_params=_cparams(("arbitrary",), QKV_VMEM_LIMIT),
        name="qkv_proj",
    )(h, w_in)


AB_MAIN = 4 * MW + 3 * HW
N_GATES = 4 * HEADS_M


def _ab_in_kernel(*refs):
    x_refs = refs[:-8]
    g_ref, sh_ref, sc_ref, w_ref, b_ref, proj_ref, gates_ref, wbf = refs[-8:]

    @pl.when(pl.program_id(0) == 0)
    def _():
        wbf[:4 * MW, :] = w_ref[:4 * MW, :].astype(BF16)
        wbf[4 * MW:AB_MAIN, :] = w_ref[4 * MW + N_GATES:, :].astype(BF16)
        wbf[AB_MAIN:AB_MAIN + N_GATES, :] = w_ref[4 * MW:4 * MW + N_GATES, :].astype(BF16)
        wbf[AB_MAIN + N_GATES:, :] = jnp.zeros((LANE - N_GATES, D), BF16)

    h = _rms_mod(_pick_rows(x_refs), g_ref, sh_ref, sc_ref).astype(BF16)
    proj_ref[...] = lax.dot_general(h, wbf[:AB_MAIN, :], NT_DIMS, preferred_element_type=F32).astype(proj_ref.dtype)
    gates_ref[...] = lax.dot_general(h, wbf[AB_MAIN:, :], NT_DIMS, preferred_element_type=F32) + b_ref[...]


def ab_in(x, g_all, mod_all, layer, w_in_t, b_gates, e):
    x_specs, x_args = _row_specs(x, D)
    b_pad = jnp.pad(b_gates, (0, LANE - N_GATES)).reshape(1, LANE)
    return pl.pallas_call(
        _ab_in_kernel,
        out_shape=[jax.ShapeDtypeStruct((N_TOK, AB_MAIN), BF16), jax.ShapeDtypeStruct((N_TOK, LANE), F32)],
        grid=(N_TOK // ROW_BM,),
        in_specs=x_specs + [_layer_vec_spec(layer), _mod_spec(layer, 0), _mod_spec(layer, 1),
                            pl.BlockSpec((None, w_in_t.shape[1], D), lambda i: (e, 0, 0),
                                         pipeline_mode=pl.Buffered(1)),
                            pl.BlockSpec((1, LANE), lambda i: (0, 0))],
        out_specs=[pl.BlockSpec((ROW_BM, AB_MAIN), lambda i: (i, 0)), pl.BlockSpec((ROW_BM, LANE), lambda i: (i, 0))],
        scratch_shapes=[pltpu.VMEM((AB_MAIN + LANE, D), BF16)],
        compiler_params=_cparams(("arbitrary",)),
        name="ab_in",
    )(*x_args, g_all.reshape(-1, 1, D), mod_all, mod_all, w_in_t, b_pad)


SEQ_BLOCK = DEC_SEQ
N_PBLK = N_P // SEQ_BLOCK


def _seq_edges(rows, is_prompt):
    r = lax.broadcasted_iota(jnp.int32, (rows, 1), 0)
    first = (r == 0) | (is_prompt & (r % SEQ == 0))
    last = (r == rows - 1) | (is_prompt & (r % SEQ == SEQ - 1))
    return first, last


def _dwconv3(x, w_ref, first, last):
    rows = x.shape[0]
    prev = jnp.where(first, 0.0, pltpu.roll(x, 1, 0))
    nxt = jnp.where(last, 0.0, pltpu.roll(x, rows - 1, 0))
    return prev * w_ref[0:1, :] + x * w_ref[1:2, :] + nxt * w_ref[2:3, :]


def _gated_gelu_of_half(y, g):
    c = math.sqrt(2.0 / math.pi)
    t = jnp.tanh(y * (2.0 * c + (8.0 * 0.044715 * c) * (y * y)))
    return (y + y * t) * g


FFN_BLOCK = 2 * DEC_SEQ
FFN_CHUNKS = 8
FFN_HALO = 8


def _ffn_up_kernel(h_ref, wa_ref, wg_ref, wc_ref, o_ref):
    is_prompt = pl.program_id(0) < N_P // FFN_BLOCK
    rows = FFN_BLOCK // FFN_CHUNKS
    wa = wa_ref[...].astype(BF16)
    wg = wg_ref[...].astype(BF16)
    wc_half = 0.5 * wc_ref[...]
    zeros = jnp.zeros((FFN_HALO, o_ref.shape[1]), F32)

    def matmuls(r):
        hr = h_ref[r * rows:(r + 1) * rows, :]
        return _dot(hr, wa), _dot(hr, wg)

    def activation(r, a_prev, a_cur, a_next, g):
        win = jnp.concatenate([zeros if a_prev is None else a_prev[rows - FFN_HALO:], a_cur,
                               zeros if a_next is None else a_next[:FFN_HALO]], axis=0)
        ridx = lax.broadcasted_iota(jnp.int32, (rows + 2 * FFN_HALO, 1), 0) + (r * rows - FFN_HALO)
        first = (ridx % DEC_SEQ == 0) | (is_prompt & (ridx % SEQ == 0))
        last = (ridx % DEC_SEQ == DEC_SEQ - 1) | (is_prompt & (ridx % SEQ == SEQ - 1))
        half_conv = _dwconv3(win, wc_half, first, last)[FFN_HALO:FFN_HALO + rows]
        o_ref[r * rows:(r + 1) * rows, :] = _gated_gelu_of_half(half_conv, g).astype(o_ref.dtype)

    acts = [matmuls(0)]
    for r in range(1, FFN_CHUNKS):
        acts.append(matmuls(r))
        activation(r - 1, acts[r - 2][0] if r >= 2 else None, acts[r - 1][0], acts[r][0], acts[r - 1][1])
    activation(FFN_CHUNKS - 1, acts[-2][0], acts[-1][0], None, acts[-1][1])


def ffn_up(h, w_up, w_conv, layer):
    tc = 256
    nct = FF // tc
    return pl.pallas_call(
        _ffn_up_kernel,
        out_shape=jax.ShapeDtypeStruct((N_TOK, FF), BF16),
        grid=(N_TOK // FFN_BLOCK, nct),
        in_specs=[
            pl.BlockSpec((FFN_BLOCK, D), lambda i, j: (i, 0)),
            pl.BlockSpec((None, D, tc), lambda i, j: (layer, 0, j)),
            pl.BlockSpec((None, D, tc), lambda i, j: (layer, 0, j + nct)),
            pl.BlockSpec((None, 3, tc), lambda i, j: (layer, 0, j)),
        ],
        out_specs=pl.BlockSpec((FFN_BLOCK, tc), lambda i, j: (i, j)),
        compiler_params=_cparams(("arbitrary", "arbitrary")),
        name="ffn_up",
    )(h, w_up, w_up, w_conv)


HEAD_PAIRS = NA_HEADS // 2
NT_DIMS = (((1,), (1,)), ((), ()))


def _pair_mask(shape):
    return lax.broadcasted_iota(jnp.int32, shape, len(shape) - 1) < NA_HD


def _one_head(x2, first):
    keep = _pair_mask(x2.shape) if first else ~_pair_mask(x2.shape)
    return jnp.where(keep, x2, jnp.zeros_like(x2))


def _ctx_attn_kernel(q_ref, k_ref, v_ref, o_ref, *, pairs):
    outs = []
    for pp in range(pairs):
        q2, k2, v2 = q_ref[pp], k_ref[pp], v_ref[pp]
        res = []
        for first in (True, False):
            s = lax.dot_general(_one_head(q2, first), k2, NT_DIMS, preferred_element_type=F32) * (NA_HD ** -0.5)
            m = jnp.max(s, axis=-1, keepdims=True)
            p = jnp.exp(s - m)
            l = jnp.sum(p, axis=-1, keepdims=True)
            res.append(_dot(p.astype(BF16), v2) / l)
        outs.append(jnp.where(_pair_mask(res[0].shape), res[0], res[1]))
    o_ref[...] = jnp.concatenate(outs, axis=-1).astype(o_ref.dtype)


def ctx_attention(qkv):
    pairs = HEAD_PAIRS
    spec = lambda part: pl.BlockSpec((None, pairs, SEQ, 2 * NA_HD), lambda b, h: (part, h, b, 0))
    return pl.pallas_call(
        functools.partial(_ctx_attn_kernel, pairs=pairs),
        out_shape=jax.ShapeDtypeStruct((N_P, D), BF16),
        grid=(BATCH, HEAD_PAIRS // pairs),
        in_specs=[spec(0), spec(1), spec(2)],
        out_specs=pl.BlockSpec((SEQ, pairs * 2 * NA_HD), lambda b, h: (b, h)),
        compiler_params=_cparams(("arbitrary", "arbitrary")),
        name="ctx_attn",
    )(qkv, qkv, qkv)


def _na_tables():
    q = np.arange(GRID_W)[:, None]
    w = np.arange(GRID_W)[None, :]
    idx_c = np.clip(w - q + (NA_KW - 1), 0, 2 * NA_KW - 2)
    onehot = (idx_c.reshape(1, -1) == np.arange(32)[:, None]).astype(np.float32)
    c_start = np.clip(np.arange(GRID_W) - NA_KW // 2, 0, GRID_W - NA_KW)[:, None]
    inside = (w >= c_start) & (w < c_start + NA_KW)
    cmask = np.where(inside, 0.0, -np.inf).astype(np.float32)
    return onehot, np.tile(cmask, (1, 2))


def _rpb_expand_kernel(r_ref, e_ref, o_ref):
    o_ref[...] = jnp.dot(r_ref[...], e_ref[...], precision=HIGHEST, preferred_element_type=F32)


def rpb_expand(rpb):
    onehot, _ = _na_tables()
    rp = jnp.pad(rpb, ((0, 0), (0, 1), (0, 1)))
    return pl.pallas_call(
        _rpb_expand_kernel,
        out_shape=jax.ShapeDtypeStruct((NA_HEADS, 16, GRID_W * GRID_W), F32),
        grid=(NA_HEADS,),
        in_specs=[pl.BlockSpec((None, 16, 32), lambda h: (h, 0, 0)),
                  pl.BlockSpec((32, GRID_W * GRID_W), lambda h: (0, 0))],
        out_specs=pl.BlockSpec((None, 16, GRID_W * GRID_W), lambda h: (h, 0, 0)),
        compiler_params=_cparams(("arbitrary",)),
        name="rpb_expand",
    )(rp, jnp.asarray(onehot))


NA_QROWS = 8
NA_WIN = NA_QROWS + NA_KH


def _na_attn_kernel(q_ref, k_ref, v_ref, kc_ref, vc_ref, t_ref, o_ref, p_loc, p_ctx):
    nq = NA_QROWS * GRID_W
    nk = NA_WIN * GRID_W
    pair = 2 * GRID_W
    nt = NT_DIMS
    lane = lax.broadcasted_iota(jnp.int32, (GRID_W, pair), 1)
    zero_tile = jnp.zeros((GRID_W, pair), BF16)
    kc = kc_ref[...].reshape(2 * NA_HD, PAST_LEN).astype(BF16)
    vc = vc_ref[...].reshape(2 * NA_HD, PAST_LEN).astype(BF16)
    for blk in range(GRID_R // NA_QROWS):
        k0 = min(max(NA_QROWS * blk - NA_KH // 2, 0), GRID_R - NA_WIN)
        q2 = q_ref[blk * nq:(blk + 1) * nq, :] * (NA_HD ** -0.5)
        kw = k_ref[k0 * GRID_W:k0 * GRID_W + nk, :]
        vw = v_ref[k0 * GRID_W:k0 * GRID_W + nk, :]
        outs = []
        for hh in range(2):
            q = _one_head(q2, hh == 0)
            s_loc = lax.dot_general(q, kw, nt, preferred_element_type=F32)
            s_ctx = _dot(q, kc)
            denoms = []
            for qi in range(NA_QROWS):
                r = NA_QROWS * blk + qi
                r_start = min(max(r - NA_KH // 2, 0), GRID_R - NA_KH)
                rows = slice(qi * GRID_W, (qi + 1) * GRID_W)
                ctx_tiles = [s_ctx[rows, c * pair:(c + 1) * pair] for c in range(PAST_LEN // pair)]
                tiles = {}
                for j in range(nk // pair):
                    kr = k0 + 2 * j
                    ok0 = r_start <= kr < r_start + NA_KH
                    ok1 = r_start <= kr + 1 < r_start + NA_KH
                    if not (ok0 or ok1):
                        continue
                    sb = s_loc[rows, j * pair:(j + 1) * pair] + t_ref[hh, kr - r + NA_KH]
                    if not (ok0 and ok1):
                        sb = jnp.where((lane < GRID_W) if ok0 else (lane >= GRID_W), sb, -jnp.inf)
                    tiles[j] = sb
                mx = functools.reduce(jnp.maximum, list(tiles.values()) + ctx_tiles)
                m = jnp.max(mx, axis=1, keepdims=True)
                acc = None
                for j in range(nk // pair):
                    if j in tiles:
                        p = jnp.exp(tiles[j] - m)
                        acc = p if acc is None else acc + p
                        p_loc[rows, j * pair:(j + 1) * pair] = p.astype(BF16)
                    else:
                        p_loc[rows, j * pair:(j + 1) * pair] = zero_tile
                for c, t in enumerate(ctx_tiles):
                    p = jnp.exp(t - m)
                    acc = acc + p
                    p_ctx[rows, c * pair:(c + 1) * pair] = p.astype(BF16)
                denoms.append(jnp.sum(acc, axis=1, keepdims=True))
            pv = _dot(p_loc[...], vw) + lax.dot_general(p_ctx[...], vc, nt, preferred_element_type=F32)
            outs.append(pv / jnp.concatenate(denoms, axis=0))
        o_ref[blk * nq:(blk + 1) * nq, :] = jnp.where(_pair_mask(outs[0].shape), outs[0], outs[1]).astype(o_ref.dtype)


def na_attention(qkv, k_ctx, v_ctx, bias_pairs):
    blk0 = N_P // DEC_SEQ
    spec = lambda part: pl.BlockSpec((None, None, DEC_SEQ, 2 * NA_HD), lambda h, b: (part, h, b + blk0, 0))
    cspec = pl.BlockSpec((None, 2, NA_HD, PAST_LEN), lambda h, b: (b, h, 0, 0))
    return pl.pallas_call(
        _na_attn_kernel,
        out_shape=jax.ShapeDtypeStruct((N_S, D), BF16),
        grid=(HEAD_PAIRS, DEC_BATCH),
        in_specs=[spec(0), spec(1), spec(2), cspec, cspec,
                  pl.BlockSpec((2, 16, GRID_W, 2 * GRID_W), lambda h, b: (h, 0, 0, 0))],
        out_specs=pl.BlockSpec((DEC_SEQ, 2 * NA_HD), lambda h, b: (b, h)),
        scratch_shapes=[pltpu.VMEM((NA_QROWS * GRID_W, NA_WIN * GRID_W), BF16),
                        pltpu.VMEM((NA_QROWS * GRID_W, PAST_LEN), BF16)],
        compiler_params=_cparams(("arbitrary", "arbitrary")),
        name="na_attn",
    )(qkv, qkv, qkv, k_ctx, v_ctx, bias_pairs)


def mixer_c(h, k_ctx, v_ctx, w_in_all, layer, rpb):
    qkv, k_new, v_new = qkv_proj(h, w_in_all, layer)
    o_p = ctx_attention(qkv)
    _, cmask2 = _na_tables()
    b15 = rpb_expand(rpb).reshape(NA_HEADS, 16, GRID_W, GRID_W)
    b17 = jnp.pad(b15, ((0, 0), (1, 0), (0, 0), (0, 0)))
    bias_pairs = jnp.concatenate([b17[:, :16], b17[:, 1:]], axis=-1) + jnp.asarray(cmask2)
    o_s = na_attention(qkv, jnp.swapaxes(k_ctx, -1, -2), jnp.swapaxes(v_ctx, -1, -2), bias_pairs)
    return (o_p, o_s), k_new, v_new


SCAN_BLOCK = HD_M


def _mlstm_kernel(*refs, seq, heads, zero_state, emit_state):
    q_ref, k_ref, v_ref, og_ref, gates_ref, wq_ref, wk_ref, gh_ref = refs[:8]
    pos = 8
    if not zero_state:
        c0_ref, n0_ref, m0_ref = refs[pos:pos + 3]
        pos += 3
    y_ref = refs[pos]
    pos += 1
    if emit_state:
        cn_ref, nn_ref, mn_ref = refs[pos:pos + 3]
        pos += 3
    pre_s, suf_s, gt_s, pret_s, suft_s, kv_s, ks_s, be_s, mk_s, cp_s, np_s, mp_s, cst, nst, mst = refs[pos:]

    blk = SCAN_BLOCK
    nc = seq // blk
    nbatch = nc * heads
    n_gates = 4 * HEADS_M
    r = lax.broadcasted_iota(jnp.int32, (seq, 1), 0)

    @pl.when(pl.program_id(1) == 0)
    def _():
        g_all = gates_ref[...]
        lf = jax.nn.log_sigmoid(g_all)
        rin = r % blk
        pre, suf = lf, lf
        for sh in [1 << i for i in range(blk.bit_length() - 1)]:
            pre = pre + jnp.where(rin >= sh, pltpu.roll(pre, sh, 0), 0.0)
            suf = suf + jnp.where(rin < blk - sh, pltpu.roll(suf, seq - sh, 0), 0.0)
        pre3, suf3 = pre.reshape(nc, blk, LANE), suf.reshape(nc, blk, LANE)
        pre_s[...] = pre3
        suf_s[...] = suf3
        gt_s[...] = jnp.swapaxes(g_all.reshape(nc, blk, LANE), 1, 2)[:, :n_gates, :]
        pret_s[...] = jnp.swapaxes(pre3, 1, 2)[:, :n_gates, :]
        suft_s[...] = jnp.swapaxes(suf3, 1, 2)[:, :n_gates, :]

    def split_heads(x):
        x3 = x.reshape(nc, blk, heads * HD_M)
        if heads == 1:
            return x3
        tiles = jnp.stack([x3[:, :, hh * HD_M:(hh + 1) * HD_M] for hh in range(heads)], axis=1)
        return tiles.reshape(nbatch, blk, HD_M)

    def per_head(x):
        if heads == 1:
            return x
        return jnp.broadcast_to(x[:, None], (nc, heads) + x.shape[1:]).reshape((nbatch,) + x.shape[1:])

    first, last = r == 0, r == seq - 1
    qc_all = _dwconv3(q_ref[...].astype(F32), wq_ref, first, last)
    q3 = split_heads(qc_all * jax.nn.sigmoid(qc_all))
    kc_all = _dwconv3(k_ref[...].astype(F32), wk_ref, first, last)
    k3 = split_heads(kc_all * jax.nn.sigmoid(kc_all) * (HD_M ** -0.5))
    qb, kb = q3.astype(BF16), k3.astype(BF16)
    vb = split_heads(v_ref[...])
    g3 = per_head(gates_ref[...].reshape(nc, blk, LANE))
    gt3 = per_head(gt_s[...])

    if zero_state:
        cst[...] = jnp.zeros_like(cst)
        nst[...] = jnp.zeros_like(nst)
        mst[...] = jnp.zeros_like(mst)
    else:
        cst[...] = c0_ref[...]
        nst[...] = n0_ref[...]
        mst[...] = jnp.broadcast_to(m0_ref[...], mst.shape)

    tt = lax.broadcasted_iota(jnp.int32, (1, blk, blk), 1)
    ss = lax.broadcasted_iota(jnp.int32, (1, blk, blk), 2)
    lane = lax.broadcasted_iota(jnp.int32, (1, 1, LANE), 2)
    sub = lax.broadcasted_iota(jnp.int32, (1, n_gates, 1), 1)
    head = lax.broadcasted_iota(jnp.int32, (nbatch, 1, 1), 0) % heads + pl.program_id(1) * heads
    hsum = None
    for d in range(2):
        i_idx = d * 2 * HEADS_M + head
        f_idx = i_idx + HEADS_M
        mask = (ss <= tt) if d == 0 else (ss >= tt)
        b3 = per_head((pre_s if d == 0 else suf_s)[...])
        bt3 = per_head((pret_s if d == 0 else suft_s)[...])
        bcol = jnp.sum(jnp.where(lane == f_idx, b3, 0.0), axis=2, keepdims=True)
        icol = jnp.sum(jnp.where(lane == i_idx, g3, 0.0), axis=2, keepdims=True)
        brow = jnp.sum(jnp.where(sub == f_idx, bt3, 0.0), axis=1, keepdims=True)
        irow = jnp.sum(jnp.where(sub == i_idx, gt3, 0.0), axis=1, keepdims=True)
        bend = bcol[:, blk - 1:blk, :] if d == 0 else bcol[:, 0:1, :]

        dmat = jnp.where(mask, bcol - brow + irow, -jnp.inf)
        mloc = jnp.max(dmat, axis=2, keepdims=True)
        qk = jnp.einsum('ctd,csd->cts', qb, kb, preferred_element_type=F32)
        s_loc = jnp.exp(dmat - mloc) * qk
        num_loc = jnp.einsum('cts,csd->ctd', s_loc.astype(BF16), vb, preferred_element_type=F32)
        den_loc = jnp.sum(s_loc, axis=2, keepdims=True)
        to_end = bend - bcol + icol
        mk = jnp.max(to_end, axis=1, keepdims=True)
        kw = k3 * jnp.exp(to_end - mk)
        kv_s[...] = jnp.einsum('cds,cse->cde', jnp.swapaxes(kw, 1, 2).astype(BF16), vb,
                               preferred_element_type=F32)
        ks_s[...] = jnp.sum(kw, axis=1, keepdims=True)
        be_s[...] = jnp.broadcast_to(bend, be_s.shape)
        mk_s[...] = jnp.broadcast_to(mk, mk_s.shape)

        def step(j, carry, d=d):
            c = j if d == 0 else nc - 1 - j
            sl = pl.ds(c * heads, heads)
            m_prev, c_prev, n_prev = mst[d], cst[d], nst[d]
            cp_s[sl] = c_prev.astype(BF16)
            np_s[sl] = n_prev
            mp_s[sl] = m_prev
            be, mkc = be_s[sl], mk_s[sl]
            m_new = jnp.maximum(be + m_prev, mkc)
            keep = jnp.exp(be + m_prev - m_new)
            add = jnp.exp(mkc - m_new)
            cst[d] = keep * c_prev + add * kv_s[sl]
            nst[d] = keep * n_prev + add * ks_s[sl]
            mst[d] = m_new
            return carry

        lax.fori_loop(0, nc, step, 0)

        m_inter = bcol + mp_s[...][:, :, 0:1]
        m_t = jnp.maximum(m_inter, mloc)
        w_state = jnp.exp(m_inter - m_t)
        w_loc = jnp.exp(mloc - m_t)
        inter = jnp.einsum('ctd,cde->cte', qb, cp_s[...], preferred_element_type=F32)
        num = w_state * inter + w_loc * num_loc
        den = w_state * jnp.sum(q3 * np_s[...], axis=2, keepdims=True) + w_loc * den_loc
        h = num / jnp.maximum(jnp.abs(den), jnp.exp(-m_t))
        hsum = h if hsum is None else hsum + h

    hn = hsum * lax.rsqrt(jnp.mean(hsum * hsum, axis=-1, keepdims=True) + EPS)
    og3 = split_heads(og_ref[...].astype(F32))
    for c in range(nc):
        for hh in range(heads):
            cols = slice(hh * HD_M, (hh + 1) * HD_M)
            e = c * heads + hh
            y_ref[c * blk:(c + 1) * blk, cols] = (hn[e] * gh_ref[:, cols] * jax.nn.sigmoid(og3[e])).astype(y_ref.dtype)
    if emit_state:
        cn_ref[...] = cst[...]
        nn_ref[...] = nst[...]
        mn_ref[...] = mst[...]


def mlstm(proj, gates, w_conv_qk, g_head, state, *, nb, seq, row0, heads):
    blk0 = row0 // seq
    nblk = seq // SCAN_BLOCK
    groups = HEADS_M // heads
    n_gates = 4 * HEADS_M
    zero_state = state is None
    col = lambda part: pl.BlockSpec((seq, heads * HD_M), lambda b, h: (b + blk0, part * groups + h))
    in_specs = [col(0), col(1), col(2), col(3),
                pl.BlockSpec((seq, LANE), lambda b, h: (b + blk0, 0)),
                pl.BlockSpec((3, heads * HD_M), lambda b, h: (0, h)),
                pl.BlockSpec((3, heads * HD_M), lambda b, h: (0, groups + h)),
                pl.BlockSpec((1, heads * HD_M), lambda b, h: (0, h))]
    args = [proj, proj, proj, proj, gates, w_conv_qk, w_conv_qk, g_head.reshape(1, MW)]
    state_specs = [pl.BlockSpec((None, 2, heads, HD_M, HD_M), lambda b, h: (b, 0, h, 0, 0)),
                   pl.BlockSpec((None, 2, heads, 1, HD_M), lambda b, h: (b, 0, h, 0, 0))]
    if not zero_state:
        c0, n0, m0 = state
        in_specs += state_specs + [pl.BlockSpec((None, 2, heads, 1, 1), lambda b, h: (b, 0, h, 0, 0))]
        args += [c0, n0.reshape(nb, 2, HEADS_M, 1, HD_M), m0.reshape(nb, 2, HEADS_M, 1, 1)]
    out_shape = [jax.ShapeDtypeStruct((nb * seq, MW), BF16)]
    out_specs = [pl.BlockSpec((seq, heads * HD_M), lambda b, h: (b, h))]
    if zero_state:
        out_shape += [jax.ShapeDtypeStruct((nb, 2, HEADS_M, HD_M, HD_M), F32),
                      jax.ShapeDtypeStruct((nb, 2, HEADS_M, 1, HD_M), F32),
                      jax.ShapeDtypeStruct((nb, 2, HEADS_M, 1, LANE), F32)]
        out_specs += state_specs + [pl.BlockSpec((None, 2, heads, 1, LANE), lambda b, h: (b, 0, h, 0, 0))]
    per_block = lambda *shape: pltpu.VMEM((nblk,) + shape, F32)
    per_entry = lambda *shape, dtype=F32: pltpu.VMEM((nblk * heads,) + shape, dtype)
    return pl.pallas_call(
        functools.partial(_mlstm_kernel, seq=seq, heads=heads, zero_state=zero_state, emit_state=zero_state),
        out_shape=out_shape,
        grid=(nb, groups),
        in_specs=in_specs,
        out_specs=out_specs,
        scratch_shapes=[per_block(SCAN_BLOCK, LANE), per_block(SCAN_BLOCK, LANE),
                        per_block(n_gates, SCAN_BLOCK), per_block(n_gates, SCAN_BLOCK), per_block(n_gates, SCAN_BLOCK),
                        per_entry(HD_M, HD_M), per_entry(1, HD_M), per_entry(1, LANE), per_entry(1, LANE),
                        per_entry(HD_M, HD_M, dtype=BF16), per_entry(1, HD_M), per_entry(1, LANE),
                        pltpu.VMEM((2, heads, HD_M, HD_M), F32), pltpu.VMEM((2, heads, 1, HD_M), F32),
                        pltpu.VMEM((2, heads, 1, LANE), F32)],
        compiler_params=_cparams(("arbitrary", "arbitrary")),
        name="mlstm_%d" % seq,
    )(*args)


def _hyena_pre_kernel(v_ref, x1_ref, x2_ref, wv_ref, w1_ref, w2_ref, u_ref, x2c_ref):
    first, last = _seq_edges(SEQ_BLOCK, pl.program_id(0) < N_PBLK)
    x1c = _dwconv3(x1_ref[...].astype(F32), w1_ref, first, last)
    u_ref[...] = (x1c * _dwconv3(v_ref[...].astype(F32), wv_ref, first, last)).astype(u_ref.dtype)
    x2c_ref[...] = _dwconv3(x2_ref[...].astype(F32), w2_ref, first, last).astype(x2c_ref.dtype)


def hyena_pre(proj, w_conv_hy):
    tc = 256
    nct = HW // tc
    c0 = 4 * MW // tc
    pcol = lambda part: pl.BlockSpec((SEQ_BLOCK, tc), lambda i, j: (i, c0 + part * nct + j))
    wcol = lambda part: pl.BlockSpec((3, tc), lambda i, j: (0, part * nct + j))
    out = jax.ShapeDtypeStruct((N_TOK, HW), BF16)
    ospec = pl.BlockSpec((SEQ_BLOCK, tc), lambda i, j: (i, j))
    return pl.pallas_call(
        _hyena_pre_kernel,
        out_shape=[out, out],
        grid=(N_TOK // SEQ_BLOCK, nct),
        in_specs=[pcol(0), pcol(1), pcol(2), wcol(0), wcol(1), wcol(2)],
        out_specs=[ospec, ospec],
        compiler_params=_cparams(("arbitrary", "arbitrary")),
        name="hyena_pre",
    )(proj, proj, proj, w_conv_hy, w_conv_hy, w_conv_hy)


@functools.lru_cache(maxsize=None)
def _filter_tables(seq):
    t = np.linspace(0.0, 1.0, seq)[:, None]
    wpos = 2.0 * np.pi * np.arange(seq)[:, None] / seq
    bands = np.linspace(1e-4, N_BANDS - 1, N_BANDS)[None, :]
    z = np.concatenate([t, np.cos(bands * wpos), -np.sin(bands * wpos)], axis=-1)
    z = np.pad(z, ((0, 0), (0, LANE - FILTER_EMB)))
    max_decay = math.log(DECAY_TARGET) / DECAY_FAST
    min_decay = math.log(DECAY_TARGET) / DECAY_SLOW
    deltas = np.abs(np.linspace(min_decay, max_decay, HW))
    decay = np.exp(-t * np.concatenate([deltas, deltas])[None, :])
    return z.astype(np.float32), decay.astype(np.float32)


@functools.lru_cache(maxsize=None)
def _dft_tables(seq, tk):
    n = 2 * seq
    k = np.arange(seq)[:, None]
    t = np.arange(seq)[None, :]
    ang = 2.0 * np.pi * ((k * t) % n) / n
    alt = np.where(np.arange(seq) % 2 == 0, 1.0, -1.0)
    cm, sm = np.cos(ang), np.sin(ang)
    sm[0, :] = alt
    fwd = np.stack([cm.reshape(seq // tk, tk, seq), sm.reshape(seq // tk, tk, seq)], axis=1)
    wk = np.where(np.arange(seq) == 0, 1.0, 2.0)[None, :]
    ci = (np.cos(ang.T) * wk) / n
    si = np.sin(ang.T) * 2.0 / n
    si[:, 0] = alt / n
    inv = np.concatenate([ci, si], axis=1)
    return fwd.astype(np.float32), inv.astype(np.float32)


def _filter_kernel(z_ref, w1_ref, b1_ref, w2_ref, b2_ref, w3_ref, fr_ref, dec_ref, hs_ref, hd_ref):
    fr = fr_ref[...]
    hp = functools.partial(jnp.dot, precision=HIGHEST, preferred_element_type=F32)
    h1 = jnp.sin(fr * (hp(z_ref[...], w1_ref[...]) + b1_ref[...]))
    h2 = jnp.sin(fr * (hp(h1, w2_ref[...]) + b2_ref[...]))
    w3 = w3_ref[...]
    h_hi, w_hi = h2.astype(BF16), w3.astype(BF16)
    h_lo, w_lo = (h2 - h_hi.astype(F32)).astype(BF16), (w3 - w_hi.astype(F32)).astype(BF16)
    filt = (_dot(h_hi, w_hi) + _dot(h_hi, w_lo) + _dot(h_lo, w_hi)) * dec_ref[...]
    past, fut = filt[:, :HW], filt[:, HW:]
    rows = filt.shape[0]
    grow = lax.broadcasted_iota(jnp.int32, (rows, 1), 0) + pl.program_id(0) * rows
    fut = jnp.where(grow == 0, 0.0, fut)
    hs_ref[...] = past + fut
    hd_ref[...] = past - fut


def filter_gen(seq, w1, b1, w2, b2, w3, freq):
    z, decay = _filter_tables(seq)
    tl = 256
    fh = FILTER_HIDDEN
    full = lambda shape: pl.BlockSpec(shape, lambda i: (0, 0))
    out = jax.ShapeDtypeStruct((seq, HW), F32)
    return pl.pallas_call(
        _filter_kernel,
        out_shape=[out, out],
        grid=(seq // tl,),
        in_specs=[pl.BlockSpec((tl, LANE), lambda i: (i, 0)), full((LANE, fh)), full((1, fh)), full((fh, fh)),
                  full((1, fh)), full((fh, 2 * HW)), full((1, fh)), pl.BlockSpec((tl, 2 * HW), lambda i: (i, 0))],
        out_specs=[pl.BlockSpec((tl, HW), lambda i: (i, 0))] * 2,
        compiler_params=_cparams(("arbitrary",)),
        name="filter_gen",
    )(jnp.asarray(z), jnp.pad(w1, ((0, LANE - FILTER_EMB), (0, 0))), b1.reshape(1, fh), w2, b2.reshape(1, fh), w3,
      freq.reshape(1, fh), jnp.asarray(decay))


def _dft_filter_kernel(a_ref, hs_ref, hd_ref, k_ref, hs_bf, hd_bf):
    @pl.when(pl.program_id(0) == 0)
    def _():
        hs_bf[...] = hs_ref[...].astype(BF16)
        hd_bf[...] = hd_ref[...].astype(BF16)

    k_ref[0] = _dot(a_ref[0].astype(BF16), hs_bf[...])
    k_ref[1] = _dot(a_ref[1].astype(BF16), hd_bf[...])

    @pl.when(pl.program_id(0) == 0)
    def _():
        k_ref[1, 0:1, :] = _dot(a_ref[1, 0:8, :].astype(BF16), hs_bf[...])[0:1, :]


def dft_filter(seq, tk, hs, hd):
    fwd, _ = _dft_tables(seq, tk)
    return pl.pallas_call(
        _dft_filter_kernel,
        out_shape=jax.ShapeDtypeStruct((2, seq, HW), F32),
        grid=(seq // tk,),
        in_specs=[pl.BlockSpec((None, 2, tk, seq), lambda m: (m, 0, 0, 0)),
                  pl.BlockSpec((seq, HW), lambda m: (0, 0)), pl.BlockSpec((seq, HW), lambda m: (0, 0))],
        out_specs=pl.BlockSpec((2, tk, HW), lambda m: (0, m, 0)),
        scratch_shapes=[pltpu.VMEM((seq, HW), BF16), pltpu.VMEM((seq, HW), BF16)],
        compiler_params=_cparams(("arbitrary",)),
        name="dft_filter",
    )(jnp.asarray(fwd), hs, hd)


def _dft_fwd_kernel(a_ref, u_ref, k_ref, y_ref, u_bf, tbl):
    m = pl.program_id(1)

    @pl.when(m == 0)
    def _():
        u_bf[...] = u_ref[...].astype(BF16)

    @pl.when(pl.program_id(0) == 0)
    def _():
        tbl[m] = a_ref[...].astype(BF16)

    ure = _dot(tbl[m, 0], u_bf[...])
    uim = _dot(tbl[m, 1], u_bf[...])
    kre, kim = k_ref[0], k_ref[1]
    packed = (lax.broadcasted_iota(jnp.int32, (ure.shape[0], 1), 0) == 0) & (m == 0)
    y_ref[0] = jnp.where(packed, ure * kre, ure * kre - uim * kim).astype(y_ref.dtype)
    y_ref[1] = jnp.where(packed, uim * kim, ure * kim + uim * kre).astype(y_ref.dtype)


def dft_fwd(seq, tk, u, kf, *, nb, row0):
    fwd, _ = _dft_tables(seq, tk)
    blk0 = row0 // seq
    nm = seq // tk
    return pl.pallas_call(
        _dft_fwd_kernel,
        out_shape=jax.ShapeDtypeStruct((nb, 2, seq, HW), BF16),
        grid=(nb, nm),
        in_specs=[pl.BlockSpec((None, 2, tk, seq), lambda b, m: (jnp.where(b == 0, m, nm - 1), 0, 0, 0)),
                  pl.BlockSpec((seq, HW), lambda b, m: (b + blk0, 0)),
                  pl.BlockSpec((2, tk, HW), lambda b, m: (0, m, 0))],
        out_specs=pl.BlockSpec((None, 2, tk, HW), lambda b, m: (b, 0, m, 0)),
        scratch_shapes=[pltpu.VMEM((seq, HW), BF16), pltpu.VMEM((nm, 2, tk, seq), BF16)],
        compiler_params=_cparams(("arbitrary", "arbitrary"), DFT_VMEM_LIMIT),
        name="dft_fwd",
    )(jnp.asarray(fwd), u, kf)


def _dft_inv_kernel(a_ref, y_ref, u_ref, x2_ref, bias_ref, o_ref, tbl, *, seq):
    t = pl.program_id(1)

    @pl.when(pl.program_id(0) == 0)
    def _():
        tbl[t] = a_ref[...].astype(BF16)

    conv = _dot(tbl[t, :, :seq], y_ref[0]) + _dot(tbl[t, :, seq:], y_ref[1])
    u = u_ref[...].astype(F32)
    o_ref[...] = (x2_ref[...].astype(F32) * (conv + bias_ref[...] * u)).astype(o_ref.dtype)


def dft_inv(seq, tk, y, u, x2c, bias, *, nb, row0):
    _, inv = _dft_tables(seq, tk)
    tm = min(seq, 512)
    nt = seq // tm
    blk0 = row0 // tm
    rows = lambda b, t: (b * nt + t + blk0, 0)
    return pl.pallas_call(
        functools.partial(_dft_inv_kernel, seq=seq),
        out_shape=jax.ShapeDtypeStruct((nb * seq, HW), BF16),
        grid=(nb, nt),
        in_specs=[pl.BlockSpec((tm, 2 * seq), lambda b, t: (jnp.where(b == 0, t, nt - 1), 0)),
                  pl.BlockSpec((None, 2, seq, HW), lambda b, t: (b, 0, 0, 0)),
                  pl.BlockSpec((tm, HW), rows), pl.BlockSpec((tm, HW), rows),
                  pl.BlockSpec((1, HW), lambda b, t: (0, 0))],
        out_specs=pl.BlockSpec((tm, HW), lambda b, t: (b * nt + t, 0)),
        scratch_shapes=[pltpu.VMEM((nt, tm, 2 * seq), BF16)],
        compiler_params=_cparams(("arbitrary", "arbitrary"), DFT_VMEM_LIMIT),
        name="dft_inv",
    )(jnp.asarray(inv), y, u, x2c, bias.reshape(1, HW))


def hyena_group(seq, u, x2c, filt_w, bias, *, nb, row0):
    tk = min(seq, 512)
    hs, hd = filter_gen(seq, *filt_w)
    kf = dft_filter(seq, tk, hs, hd)
    y = dft_fwd(seq, tk, u, kf, nb=nb, row0=row0)
    return dft_inv(seq, tk, y, u, x2c, bias, nb=nb, row0=row0)


def mixer_ab_parts(proj, gates, state_s, w_conv_qk, g_head, w_conv_hy, w_f1, b_f1, w_f2, b_f2, w_f3, freq, hy_bias):
    ym_p, c_p, n_p, m_p = mlstm(proj, gates, w_conv_qk, g_head, None, nb=BATCH, seq=SEQ, row0=0, heads=HEADS_M)
    (ym_s,) = mlstm(proj, gates, w_conv_qk, g_head, state_s, nb=DEC_BATCH, seq=DEC_SEQ, row0=N_P, heads=1)
    u, x2c = hyena_pre(proj, w_conv_hy)
    filt_w = (w_f1, b_f1, w_f2, b_f2, w_f3, freq)
    yh_p = hyena_group(SEQ, u, x2c, filt_w, hy_bias, nb=BATCH, row0=0)
    yh_s = hyena_group(DEC_SEQ, u, x2c, filt_w, hy_bias, nb=DEC_BATCH, row0=N_P)
    state_p = (c_p, n_p[:, :, :, 0, :], m_p[:, :, :, 0, 0])
    return (ym_p, ym_s), (yh_p, yh_s), state_p


def kernel(x_prompt, x_sample, state_mlstm_C, state_mlstm_n, state_mlstm_m, cache_na_k, cache_na_v, c, c_ctx, w_ada, b_ada, g_mix, g_ffn, g_final, w_in_ab, b_gates, w_conv_qk, g_mlstm, w_conv_hy, w_filt1, b_filt1, w_filt2, b_filt2, w_filt3, filt_freq, hyena_bias, w_out_ab, w_in_c, rpb_c, w_out_c, w_up, w_conv_ffn, w_down):
    cmat = jnp.concatenate([c, c_ctx[None, :], jnp.zeros((MOD_ROWS - DEC_BATCH - 1, D), F32)], axis=0)
    mod_all = adaln_all(cmat, w_ada, b_ada).reshape(DEPTH, MOD_ROWS, 6, 1, D)
    x = (x_prompt.reshape(N_P, D), x_sample.reshape(N_S, D))
    h = None
    new_c, new_n, new_m, new_k, new_v = [], [], [], [], []
    for l in range(DEPTH):
        e = l // 2
        if l % 2 == 0:
            assert l == 0, "a later mixer A/B layer would take the fused norm of the layer before it"
            proj, gates = ab_in(x, g_mix, mod_all, l, jnp.swapaxes(w_in_ab, 1, 2), b_gates[e], e)
            state_s = (state_mlstm_C[:, e], state_mlstm_n[:, e], state_mlstm_m[:, e])
            y_m, y_h, (c_p, n_p, m_p) = mixer_ab_parts(
                proj, gates, state_s, w_conv_qk[e], g_mlstm[e], w_conv_hy[e], w_filt1[e], b_filt1[e],
                w_filt2[e], b_filt2[e], w_filt3[e], filt_freq[e], hyena_bias[e])
            a_list = [y_m, y_h]
            w_list = [(w_out_ab, (None, MW, D), lambda i, e=e: (e, 0, 0)),
                      (w_out_ab, (None, HW, D), lambda i, e=e: (e, 1, 0))]
            new_c.append(c_p[:, None])
            new_n.append(n_p[:, None])
            new_m.append(m_p[:, None])
        else:
            o, k_new, v_new = mixer_c(h, cache_na_k[:, e], cache_na_v[:, e], w_in_c, e, rpb_c[e])
            a_list = [o]
            w_list = [(w_out_c, (None, D, D), lambda i, e=e: (e, 0, 0))]
            new_k.append(k_new)
            new_v.append(v_new)
        x, h = out_proj(a_list, w_list, x, mod_all, l, 2, g_ffn, l, (l, 3, 4), bm=2 * ROW_BM, name="mixer_out")
        mid = ffn_up(h, w_up, w_conv_ffn, l)
        w_list = [(w_down, (None, FF, D), lambda i, l=l: (l, 0, 0))]
        if l + 1 < DEPTH:
            x, h = out_proj([mid], w_list, x, mod_all, l, 5, g_mix, l + 1, (l + 1, 0, 1), bm=2 * ROW_BM,
                            vmem_limit=QKV_VMEM_LIMIT, name="ffn_down")
        else:
            y_p, y_s = out_proj([mid], w_list, x, mod_all, l, 5, g_final, 0, None, name="ffn_down_final")
    cat = lambda parts: parts[0] if len(parts) == 1 else jnp.concatenate(parts, axis=1)
    return (y_p.reshape(BATCH, SEQ, D), y_s.reshape(DEC_BATCH, DEC_SEQ, D), cat(new_c), cat(new_n), cat(new_m),
            cat(new_k), cat(new_v))
```

```python
import functools
import math

import numpy as np
import jax
import jax.numpy as jnp
from jax import lax
from jax.experimental import pallas as pl
from jax.experimental.pallas import tpu as pltpu

F32 = jnp.float32
BF16 = jnp.bfloat16

D = 1024
BATCH, SEQ = 16, 256
DEC_BATCH, DEC_SEQ = 4, 2048
PAST_LEN = 512
DEPTH = 2
GRID_W = 64
GRID_R = DEC_SEQ // GRID_W
HEADS_M = 4
MW = D // 2
HD_M = MW // HEADS_M
HW = D // 2
N_BANDS = 16
FILTER_EMB = 2 * N_BANDS + 1
FILTER_HIDDEN = 64
DECAY_FAST, DECAY_SLOW, DECAY_TARGET = 0.3, 1.5, 1e-2
NA_HEADS = 16
NA_HD = D // NA_HEADS
NA_KH, NA_KW = 8, 16
FF = 2816
EPS = 1e-6

N_P = BATCH * SEQ
N_S = DEC_BATCH * DEC_SEQ
N_TOK = N_P + N_S
CTX_ROW = DEC_BATCH
MOD_ROWS = 8
LANE = 128
VMEM_LIMIT = 48 * 1024 * 1024
QKV_VMEM_LIMIT = 56 * 1024 * 1024
DFT_VMEM_LIMIT = 56 * 1024 * 1024
HIGHEST = lax.Precision.HIGHEST


def _cparams(sem, vmem_limit=VMEM_LIMIT):
    return pltpu.CompilerParams(dimension_semantics=sem, vmem_limit_bytes=vmem_limit)


def _mod_row(i, bm):
    return jnp.where(i < N_P // bm, CTX_ROW, (i - N_P // bm) // (DEC_SEQ // bm))


def _dot(a, b):
    return jnp.dot(a, b, preferred_element_type=F32)


def _adaln_kernel(c_ref, w_ref, b_ref, o_ref):
    cv = c_ref[...]
    s = cv * jax.nn.sigmoid(cv)
    o_ref[...] = _dot(s.astype(BF16), w_ref[...].astype(BF16)) + b_ref[...]


def adaln_all(cmat, w_ada, b_ada):
    tn = 1024
    return pl.pallas_call(
        _adaln_kernel,
        out_shape=jax.ShapeDtypeStruct((DEPTH, MOD_ROWS, 6 * D), F32),
        grid=(DEPTH, 6 * D // tn),
        in_specs=[
            pl.BlockSpec((MOD_ROWS, D), lambda l, j: (0, 0)),
            pl.BlockSpec((None, D, tn), lambda l, j: (l, 0, j)),
            pl.BlockSpec((None, 1, tn), lambda l, j: (l, 0, j)),
        ],
        out_specs=pl.BlockSpec((None, MOD_ROWS, tn), lambda l, j: (l, 0, j)),
        compiler_params=_cparams(("arbitrary", "arbitrary")),
        name="adaln",
    )(cmat, w_ada, b_ada.reshape(DEPTH, 1, 6 * D))


ROW_BM = 512
ROW_NPB = N_P // ROW_BM


def _row_specs(arr, cols, bm=ROW_BM):
    npb = N_P // bm
    if isinstance(arr, tuple):
        return ([pl.BlockSpec((bm, cols), lambda i: (jnp.minimum(i, npb - 1), 0)),
                 pl.BlockSpec((bm, cols), lambda i: (jnp.maximum(i - npb, 0), 0))], list(arr))
    return [pl.BlockSpec((bm, cols), lambda i: (i, 0))], [arr]


def _pick_rows(refs, npb=ROW_NPB):
    if len(refs) == 1:
        return refs[0][...]
    return jnp.where(pl.program_id(0) < npb, refs[0][...], refs[1][...])


def _mod_spec(layer, which, bm=ROW_BM):
    return pl.BlockSpec((None, None, None, 1, D), lambda i: (layer, _mod_row(i, bm), which, 0, 0))


def _layer_vec_spec(layer):
    return pl.BlockSpec((None, 1, D), lambda i: (layer, 0, 0))


def _rms_mod(x, g_ref, sh_ref, sc_ref):
    y = x * lax.rsqrt(jnp.mean(x * x, axis=-1, keepdims=True) + EPS)
    return (y * g_ref[...]) * (1.0 + sc_ref[...]) + sh_ref[...]


def _out_proj_kernel(*refs, a_counts, n_x, final, npb):
    pos = 0
    a_groups = []
    for cnt in a_counts:
        a_groups.append(refs[pos:pos + cnt])
        pos += cnt
    w_refs = refs[pos:pos + len(a_counts)]
    pos += len(a_counts)
    x_refs = refs[pos:pos + n_x]
    pos += n_x
    gt_ref, g_ref = refs[pos:pos + 2]
    pos += 2
    if not final:
        sh_ref, sc_ref = refs[pos:pos + 2]
        pos += 2
    out_a, out_b = refs[pos:pos + 2]
    wbf = refs[pos + 2:]

    @pl.when(pl.program_id(0) == 0)
    def _():
        for w_ref, wb in zip(w_refs, wbf):
            wb[...] = w_ref[...].astype(BF16)

    acc = None
    for group, wb in zip(a_groups, wbf):
        part = _dot(_pick_rows(group, npb), wb[...])
        acc = part if acc is None else acc + part
    xn = _pick_rows(x_refs, npb) + gt_ref[...] * acc
    if final:
        y = xn * lax.rsqrt(jnp.mean(xn * xn, axis=-1, keepdims=True) + EPS) * g_ref[...]

        @pl.when(pl.program_id(0) < npb)
        def _():
            out_a[...] = y

        @pl.when(pl.program_id(0) >= npb)
        def _():
            out_b[...] = y
    else:
        out_a[...] = xn
        out_b[...] = _rms_mod(xn, g_ref, sh_ref, sc_ref).astype(out_b.dtype)


def out_proj(a_list, w_list, x, mod_all, layer, gt_idx, g_all, g_layer, norm_mod_idx=None, bm=ROW_BM,
             name="out_proj"):
    final = norm_mod_idx is None
    npb = N_P // bm
    in_specs, args, a_counts = [], [], []
    for a in a_list:
        cols = (a[0] if isinstance(a, tuple) else a).shape[1]
        specs, ops = _row_specs(a, cols, bm)
        in_specs += specs
        args += ops
        a_counts.append(len(ops))
    w_shapes = []
    for w, block, imap in w_list:
        in_specs.append(pl.BlockSpec(block, imap, pipeline_mode=pl.Buffered(1)))
        args.append(w)
        w_shapes.append(tuple(b for b in block if b is not None))
    x_specs, x_args = _row_specs(x, D, bm)
    in_specs += x_specs + [_mod_spec(layer, gt_idx, bm), _layer_vec_spec(g_layer)]
    args += x_args + [mod_all, g_all.reshape(-1, 1, D)]
    if final:
        out_shape = [jax.ShapeDtypeStruct((N_P, D), F32), jax.ShapeDtypeStruct((N_S, D), F32)]
        out_specs = [pl.BlockSpec((bm, D), lambda i: (jnp.minimum(i, npb - 1), 0)),
                     pl.BlockSpec((bm, D), lambda i: (jnp.maximum(i - npb, 0), 0))]
    else:
        n_layer, sh_idx, sc_idx = norm_mod_idx
        in_specs += [_mod_spec(n_layer, sh_idx, bm), _mod_spec(n_layer, sc_idx, bm)]
        args += [mod_all, mod_all]
        out_shape = [jax.ShapeDtypeStruct((N_TOK, D), F32), jax.ShapeDtypeStruct((N_TOK, D), BF16)]
        out_specs = [pl.BlockSpec((bm, D), lambda i: (i, 0))] * 2
    return pl.pallas_call(
        functools.partial(_out_proj_kernel, a_counts=tuple(a_counts), n_x=len(x_args), final=final, npb=npb),
        out_shape=out_shape,
        grid=(N_TOK // bm,),
        in_specs=in_specs,
        out_specs=out_specs,
        scratch_shapes=[pltpu.VMEM(s, BF16) for s in w_shapes],
        compiler_params=_cparams(("arbitrary",)),
        name=name,
    )(*args)


def _qkv_kernel(a_ref, w_ref, o_ref, kc_ref, vc_ref, wbf):
    @pl.when(pl.program_id(0) == 0)
    def _():
        wbf[...] = w_ref[...].astype(BF16)

    a = a_ref[...]
    for part in range(3):
        acc = _dot(a, wbf[:, part * D:(part + 1) * D])
        for pp in range(NA_HEADS // 2):
            o_ref[part, pp] = acc[:, pp * 2 * NA_HD:(pp + 1) * 2 * NA_HD].astype(o_ref.dtype)
        if part > 0:
            c_ref = kc_ref if part == 1 else vc_ref

            @pl.when(pl.program_id(0) < ROW_NPB)
            def _(acc=acc, c_ref=c_ref):
                for b in range(ROW_BM // SEQ):
                    for hh in range(NA_HEADS):
                        c_ref[b, hh] = acc[b * SEQ:(b + 1) * SEQ, hh * NA_HD:(hh + 1) * NA_HD]


def qkv_proj(h, w_in, layer):
    seqs = ROW_BM // SEQ
    cache = jax.ShapeDtypeStruct((BATCH, 1, NA_HEADS, SEQ, NA_HD), F32)
    cache_spec = pl.BlockSpec((seqs, None, NA_HEADS, SEQ, NA_HD),
                              lambda i: (jnp.minimum(i, ROW_NPB - 1), 0, 0, 0, 0))
    return pl.pallas_call(
        _qkv_kernel,
        out_shape=[jax.ShapeDtypeStruct((3, NA_HEADS // 2, N_TOK, 2 * NA_HD), BF16), cache, cache],
        grid=(N_TOK // ROW_BM,),
        in_specs=[pl.BlockSpec((ROW_BM, D), lambda i: (i, 0)),
                  pl.BlockSpec((None, D, 3 * D), lambda i: (layer, 0, 0), pipeline_mode=pl.Buffered(1))],
        out_specs=[pl.BlockSpec((3, NA_HEADS // 2, ROW_BM, 2 * NA_HD), lambda i: (0, 0, i, 0)),
                   cache_spec, cache_spec],
        scratch_shapes=[pltpu.VMEM((D, 3 * D), BF16)],
        compiler_params=_cparams(("arbitrary",), QKV_VMEM_LIMIT),
        name="qkv_proj",
    )(h, w_in)


AB_MAIN = 4 * MW + 3 * HW
N_GATES = 4 * HEADS_M


def _ab_in_kernel(*refs):
    x_refs = refs[:-8]
    g_ref, sh_ref, sc_ref, w_ref, b_ref, proj_ref, gates_ref, wbf = refs[-8:]

    @pl.when(pl.program_id(0) == 0)
    def _():
        wbf[:4 * MW, :] = w_ref[:4 * MW, :].astype(BF16)
        wbf[4 * MW:AB_MAIN, :] = w_ref[4 * MW + N_GATES:, :].astype(BF16)
        wbf[AB_MAIN:AB_MAIN + N_GATES, :] = w_ref[4 * MW:4 * MW + N_GATES, :].astype(BF16)
        wbf[AB_MAIN + N_GATES:, :] = jnp.zeros((LANE - N_GATES, D), BF16)

    h = _rms_mod(_pick_rows(x_refs), g_ref, sh_ref, sc_ref).astype(BF16)
    proj_ref[...] = lax.dot_general(h, wbf[:AB_MAIN, :], NT_DIMS, preferred_element_type=F32).astype(proj_ref.dtype)
    gates_ref[...] = lax.dot_general(h, wbf[AB_MAIN:, :], NT_DIMS, preferred_element_type=F32) + b_ref[...]


def ab_in(x, g_all, mod_all, layer, w_in_t, b_gates, e):
    x_specs, x_args = _row_specs(x, D)
    b_pad = jnp.pad(b_gates, (0, LANE - N_GATES)).reshape(1, LANE)
    return pl.pallas_call(
        _ab_in_kernel,
        out_shape=[jax.ShapeDtypeStruct((N_TOK, AB_MAIN), BF16), jax.ShapeDtypeStruct((N_TOK, LANE), F32)],
        grid=(N_TOK // ROW_BM,),
        in_specs=x_specs + [_layer_vec_spec(layer), _mod_spec(layer, 0), _mod_spec(layer, 1),
                            pl.BlockSpec((None, w_in_t.shape[1], D), lambda i: (e, 0, 0),
                                         pipeline_mode=pl.Buffered(1)),
                            pl.BlockSpec((1, LANE), lambda i: (0, 0))],
        out_specs=[pl.BlockSpec((ROW_BM, AB_MAIN), lambda i: (i, 0)), pl.BlockSpec((ROW_BM, LANE), lambda i: (i, 0))],
        scratch_shapes=[pltpu.VMEM((AB_MAIN + LANE, D), BF16)],
        compiler_params=_cparams(("arbitrary",)),
        name="ab_in",
    )(*x_args, g_all.reshape(-1, 1, D), mod_all, mod_all, w_in_t, b_pad)


SEQ_BLOCK = DEC_SEQ
N_PBLK = N_P // SEQ_BLOCK


def _seq_edges(rows, is_prompt):
    r = lax.broadcasted_iota(jnp.int32, (rows, 1), 0)
    first = (r == 0) | (is_prompt & (r % SEQ == 0))
    last = (r == rows - 1) | (is_prompt & (r % SEQ == SEQ - 1))
    return first, last


def _dwconv3(x, w_ref, first, last):
    rows = x.shape[0]
    prev = jnp.where(first, 0.0, pltpu.roll(x, 1, 0))
    nxt = jnp.where(last, 0.0, pltpu.roll(x, rows - 1, 0))
    return prev * w_ref[0:1, :] + x * w_ref[1:2, :] + nxt * w_ref[2:3, :]


def _gated_gelu_of_half(y, g):
    c = math.sqrt(2.0 / math.pi)
    t = jnp.tanh(y * (2.0 * c + (8.0 * 0.044715 * c) * (y * y)))
    return (y + y * t) * g


FFN_BLOCK = 2 * DEC_SEQ
FFN_CHUNKS = 8
FFN_HALO = 8


def _ffn_up_kernel(h_ref, wa_ref, wg_ref, wc_ref, o_ref):
    is_prompt = pl.program_id(0) < N_P // FFN_BLOCK
    rows = FFN_BLOCK // FFN_CHUNKS
    wa = wa_ref[...].astype(BF16)
    wg = wg_ref[...].astype(BF16)
    wc_half = 0.5 * wc_ref[...]
    zeros = jnp.zeros((FFN_HALO, o_ref.shape[1]), F32)

    def matmuls(r):
        hr = h_ref[r * rows:(r + 1) * rows, :]
        return _dot(hr, wa), _dot(hr, wg)

    def activation(r, a_prev, a_cur, a_next, g):
        win = jnp.concatenate([zeros if a_prev is None else a_prev[rows - FFN_HALO:], a_cur,
                               zeros if a_next is None else a_next[:FFN_HALO]], axis=0)
        ridx = lax.broadcasted_iota(jnp.int32, (rows + 2 * FFN_HALO, 1), 0) + (r * rows - FFN_HALO)
        first = (ridx % DEC_SEQ == 0) | (is_prompt & (ridx % SEQ == 0))
        last = (ridx % DEC_SEQ == DEC_SEQ - 1) | (is_prompt & (ridx % SEQ == SEQ - 1))
        half_conv = _dwconv3(win, wc_half, first, last)[FFN_HALO:FFN_HALO + rows]
        o_ref[r * rows:(r + 1) * rows, :] = _gated_gelu_of_half(half_conv, g).astype(o_ref.dtype)

    acts = [matmuls(0)]
    for r in range(1, FFN_CHUNKS):
        acts.append(matmuls(r))
        activation(r - 1, acts[r - 2][0] if r >= 2 else None, acts[r - 1][0], acts[r][0], acts[r - 1][1])
    activation(FFN_CHUNKS - 1, acts[-2][0], acts[-1][0], None, acts[-1][1])


def ffn_up(h, w_up, w_conv, layer):
    tc = 256
    nct = FF // tc
    return pl.pallas_call(
        _ffn_up_kernel,
        out_shape=jax.ShapeDtypeStruct((N_TOK, FF), BF16),
        grid=(N_TOK // FFN_BLOCK, nct),
        in_specs=[
            pl.BlockSpec((FFN_BLOCK, D), lambda i, j: (i, 0)),
            pl.BlockSpec((None, D, tc), lambda i, j: (layer, 0, j)),
            pl.BlockSpec((None, D, tc), lambda i, j: (layer, 0, j + nct)),
            pl.BlockSpec((None, 3, tc), lambda i, j: (layer, 0, j)),
        ],
        out_specs=pl.BlockSpec((FFN_BLOCK, tc), lambda i, j: (i, j)),
        compiler_params=_cparams(("arbitrary", "arbitrary")),
        name="ffn_up",
    )(h, w_up, w_up, w_conv)


HEAD_PAIRS = NA_HEADS // 2
NT_DIMS = (((1,), (1,)), ((), ()))


def _pair_mask(shape):
    return lax.broadcasted_iota(jnp.int32, shape, len(shape) - 1) < NA_HD


def _one_head(x2, first):
    keep = _pair_mask(x2.shape) if first else ~_pair_mask(x2.shape)
    return jnp.where(keep, x2, jnp.zeros_like(x2))


def _ctx_attn_kernel(q_ref, k_ref, v_ref, o_ref, *, pairs):
    outs = []
    for pp in range(pairs):
        q2, k2, v2 = q_ref[pp], k_ref[pp], v_ref[pp]
        res = []
        for first in (True, False):
            s = lax.dot_general(_one_head(q2, first), k2, NT_DIMS, preferred_element_type=F32) * (NA_HD ** -0.5)
            m = jnp.max(s, axis=-1, keepdims=True)
            p = jnp.exp(s - m)
            l = jnp.sum(p, axis=-1, keepdims=True)
            res.append(_dot(p.astype(BF16), v2) / l)
        outs.append(jnp.where(_pair_mask(res[0].shape), res[0], res[1]))
    o_ref[...] = jnp.concatenate(outs, axis=-1).astype(o_ref.dtype)


def ctx_attention(qkv):
    pairs = HEAD_PAIRS
    spec = lambda part: pl.BlockSpec((None, pairs, SEQ, 2 * NA_HD), lambda b, h: (part, h, b, 0))
    return pl.pallas_call(
        functools.partial(_ctx_attn_kernel, pairs=pairs),
        out_shape=jax.ShapeDtypeStruct((N_P, D), BF16),
        grid=(BATCH, HEAD_PAIRS // pairs),
        in_specs=[spec(0), spec(1), spec(2)],
        out_specs=pl.BlockSpec((SEQ, pairs * 2 * NA_HD), lambda b, h: (b, h)),
        compiler_params=_cparams(("arbitrary", "arbitrary")),
        name="ctx_attn",
    )(qkv, qkv, qkv)


def _na_tables():
    q = np.arange(GRID_W)[:, None]
    w = np.arange(GRID_W)[None, :]
    idx_c = np.clip(w - q + (NA_KW - 1), 0, 2 * NA_KW - 2)
    onehot = (idx_c.reshape(1, -1) == np.arange(32)[:, None]).astype(np.float32)
    c_start = np.clip(np.arange(GRID_W) - NA_KW // 2, 0, GRID_W - NA_KW)[:, None]
    inside = (w >= c_start) & (w < c_start + NA_KW)
    cmask = np.where(inside, 0.0, -np.inf).astype(np.float32)
    return onehot, np.tile(cmask, (1, 2))


def _rpb_expand_kernel(r_ref, e_ref, o_ref):
    o_ref[...] = jnp.dot(r_ref[...], e_ref[...], precision=HIGHEST, preferred_element_type=F32)


def rpb_expand(rpb):
    onehot, _ = _na_tables()
    rp = jnp.pad(rpb, ((0, 0), (0, 1), (0, 1)))
    return pl.pallas_call(
        _rpb_expand_kernel,
        out_shape=jax.ShapeDtypeStruct((NA_HEADS, 16, GRID_W * GRID_W), F32),
        grid=(NA_HEADS,),
        in_specs=[pl.BlockSpec((None, 16, 32), lambda h: (h, 0, 0)),
                  pl.BlockSpec((32, GRID_W * GRID_W), lambda h: (0, 0))],
        out_specs=pl.BlockSpec((None, 16, GRID_W * GRID_W), lambda h: (h, 0, 0)),
        compiler_params=_cparams(("arbitrary",)),
        name="rpb_expand",
    )(rp, jnp.asarray(onehot))


NA_QROWS = 8
NA_WIN = NA_QROWS + NA_KH


def _na_attn_kernel(q_ref, k_ref, v_ref, kc_ref, vc_ref, t_ref, o_ref, p_loc, p_ctx):
    nq = NA_QROWS * GRID_W
    nk = NA_WIN * GRID_W
    pair = 2 * GRID_W
    nt = NT_DIMS
    lane = lax.broadcasted_iota(jnp.int32, (GRID_W, pair), 1)
    zero_tile = jnp.zeros((GRID_W, pair), BF16)
    kc = kc_ref[...].reshape(2 * NA_HD, PAST_LEN).astype(BF16)
    vc = vc_ref[...].reshape(2 * NA_HD, PAST_LEN).astype(BF16)
    for blk in range(GRID_R // NA_QROWS):
        k0 = min(max(NA_QROWS * blk - NA_KH // 2, 0), GRID_R - NA_WIN)
        q2 = q_ref[blk * nq:(blk + 1) * nq, :] * (NA_HD ** -0.5)
        kw = k_ref[k0 * GRID_W:k0 * GRID_W + nk, :]
        vw = v_ref[k0 * GRID_W:k0 * GRID_W + nk, :]
        outs = []
        for hh in range(2):
            q = _one_head(q2, hh == 0)
            s_loc = lax.dot_general(q, kw, nt, preferred_element_type=F32)
            s_ctx = _dot(q, kc)
            denoms = []
            for qi in range(NA_QROWS):
                r = NA_QROWS * blk + qi
                r_start = min(max(r - NA_KH // 2, 0), GRID_R - NA_KH)
                rows = slice(qi * GRID_W, (qi + 1) * GRID_W)
                ctx_tiles = [s_ctx[rows, c * pair:(c + 1) * pair] for c in range(PAST_LEN // pair)]
                tiles = {}
                for j in range(nk // pair):
                    kr = k0 + 2 * j
                    ok0 = r_start <= kr < r_start + NA_KH
                    ok1 = r_start <= kr + 1 < r_start + NA_KH
                    if not (ok0 or ok1):
                        continue
                    sb = s_loc[rows, j * pair:(j + 1) * pair] + t_ref[hh, kr - r + NA_KH]
                    if not (ok0 and ok1):
                        sb = jnp.where((lane < GRID_W) if ok0 else (lane >= GRID_W), sb, -jnp.inf)
                    tiles[j] = sb
                mx = functools.reduce(jnp.maximum, list(tiles.values()) + ctx_tiles)
                m = jnp.max(mx, axis=1, keepdims=True)
                acc = None
                for j in range(nk // pair):
                    if j in tiles:
                        p = jnp.exp(tiles[j] - m)
                        acc = p if acc is None else acc + p
                        p_loc[rows, j * pair:(j + 1) * pair] = p.astype(BF16)
                    else:
                        p_loc[rows, j * pair:(j + 1) * pair] = zero_tile
                for c, t in enumerate(ctx_tiles):
                    p = jnp.exp(t - m)
                    acc = acc + p
                    p_ctx[rows, c * pair:(c + 1) * pair] = p.astype(BF16)
                denoms.append(jnp.sum(acc, axis=1, keepdims=True))
            pv = _dot(p_loc[...], vw) + lax.dot_general(p_ctx[...], vc, nt, preferred_element_type=F32)
            outs.append(pv / jnp.concatenate(denoms, axis=0))
        o_ref[blk * nq:(blk + 1) * nq, :] = jnp.where(_pair_mask(outs[0].shape), outs[0], outs[1]).astype(o_ref.dtype)


def na_attention(qkv, k_ctx, v_ctx, bias_pairs):
    blk0 = N_P // DEC_SEQ
    spec = lambda part: pl.BlockSpec((None, None, DEC_SEQ, 2 * NA_HD), lambda h, b: (part, h, b + blk0, 0))
    cspec = pl.BlockSpec((None, 2, NA_HD, PAST_LEN), lambda h, b: (b, h, 0, 0))
    return pl.pallas_call(
        _na_attn_kernel,
        out_shape=jax.ShapeDtypeStruct((N_S, D), BF16),
        grid=(HEAD_PAIRS, DEC_BATCH),
        in_specs=[spec(0), spec(1), spec(2), cspec, cspec,
                  pl.BlockSpec((2, 16, GRID_W, 2 * GRID_W), lambda h, b: (h, 0, 0, 0))],
        out_specs=pl.BlockSpec((DEC_SEQ, 2 * NA_HD), lambda h, b: (b, h)),
        scratch_shapes=[pltpu.VMEM((NA_QROWS * GRID_W, NA_WIN * GRID_W), BF16),
                        pltpu.VMEM((NA_QROWS * GRID_W, PAST_LEN), BF16)],
        compiler_params=_cparams(("arbitrary", "arbitrary")),
        name="na_attn",
    )(qkv, qkv, qkv, k_ctx, v_ctx, bias_pairs)


def mixer_c(h, k_ctx, v_ctx, w_in_all, layer, rpb):
    qkv, k_new, v_new = qkv_proj(h, w_in_all, layer)
    o_p = ctx_attention(qkv)
    _, cmask2 = _na_tables()
    b15 = rpb_expand(rpb).reshape(NA_HEADS, 16, GRID_W, GRID_W)
    b17 = jnp.pad(b15, ((0, 0), (1, 0), (0, 0), (0, 0)))
    bias_pairs = jnp.concatenate([b17[:, :16], b17[:, 1:]], axis=-1) + jnp.asarray(cmask2)
    o_s = na_attention(qkv, jnp.swapaxes(k_ctx, -1, -2), jnp.swapaxes(v_ctx, -1, -2), bias_pairs)
    return (o_p, o_s), k_new, v_new


SCAN_BLOCK = HD_M


def _mlstm_kernel(*refs, seq, heads, zero_state, emit_state):
    q_ref, k_ref, v_ref, og_ref, gates_ref, wq_ref, wk_ref, gh_ref = refs[:8]
    pos = 8
    if not zero_state:
        c0_ref, n0_ref, m0_ref = refs[pos:pos + 3]
        pos += 3
    y_ref = refs[pos]
    pos += 1
    if emit_state:
        cn_ref, nn_ref, mn_ref = refs[pos:pos + 3]
        pos += 3
    pre_s, suf_s, gt_s, pret_s, suft_s, kv_s, ks_s, be_s, mk_s, cp_s, np_s, mp_s, cst, nst, mst = refs[pos:]

    blk = SCAN_BLOCK
    nc = seq // blk
    nbatch = nc * heads
    n_gates = 4 * HEADS_M
    r = lax.broadcasted_iota(jnp.int32, (seq, 1), 0)

    @pl.when(pl.program_id(1) == 0)
    def _():
        g_all = gates_ref[...]
        lf = jax.nn.log_sigmoid(g_all)
        rin = r % blk
        pre, suf = lf, lf
        for sh in [1 << i for i in range(blk.bit_length() - 1)]:
            pre = pre + jnp.where(rin >= sh, pltpu.roll(pre, sh, 0), 0.0)
            suf = suf + jnp.where(rin < blk - sh, pltpu.roll(suf, seq - sh, 0), 0.0)
        pre3, suf3 = pre.reshape(nc, blk, LANE), suf.reshape(nc, blk, LANE)
        pre_s[...] = pre3
        suf_s[...] = suf3
        gt_s[...] = jnp.swapaxes(g_all.reshape(nc, blk, LANE), 1, 2)[:, :n_gates, :]
        pret_s[...] = jnp.swapaxes(pre3, 1, 2)[:, :n_gates, :]
        suft_s[...] = jnp.swapaxes(suf3, 1, 2)[:, :n_gates, :]

    def split_heads(x):
        x3 = x.reshape(nc, blk, heads * HD_M)
        if heads == 1:
            return x3
        tiles = jnp.stack([x3[:, :, hh * HD_M:(hh + 1) * HD_M] for hh in range(heads)], axis=1)
        return tiles.reshape(nbatch, blk, HD_M)

    def per_head(x):
        if heads == 1:
            return x
        return jnp.broadcast_to(x[:, None], (nc, heads) + x.shape[1:]).reshape((nbatch,) + x.shape[1:])

    first, last = r == 0, r == seq - 1
    qc_all = _dwconv3(q_ref[...].astype(F32), wq_ref, first, last)
    q3 = split_heads(qc_all * jax.nn.sigmoid(qc_all))
    kc_all = _dwconv3(k_ref[...].astype(F32), wk_ref, first, last)
    k3 = split_heads(kc_all * jax.nn.sigmoid(kc_all) * (HD_M ** -0.5))
    qb, kb = q3.astype(BF16), k3.astype(BF16)
    vb = split_heads(v_ref[...])
    g3 = per_head(gates_ref[...].reshape(nc, blk, LANE))
    gt3 = per_head(gt_s[...])

    if zero_state:
        cst[...] = jnp.zeros_like(cst)
        nst[...] = jnp.zeros_like(nst)
        mst[...] = jnp.zeros_like(mst)
    else:
        cst[...] = c0_ref[...]
        nst[...] = n0_ref[...]
        mst[...] = jnp.broadcast_to(m0_ref[...], mst.shape)

    tt = lax.broadcasted_iota(jnp.int32, (1, blk, blk), 1)
    ss = lax.broadcasted_iota(jnp.int32, (1, blk, blk), 2)
    lane = lax.broadcasted_iota(jnp.int32, (1, 1, LANE), 2)
    sub = lax.broadcasted_iota(jnp.int32, (1, n_gates, 1), 1)
    head = lax.broadcasted_iota(jnp.int32, (nbatch, 1, 1), 0) % heads + pl.program_id(1) * heads
    hsum = None
    for d in range(2):
        i_idx = d * 2 * HEADS_M + head
        f_idx = i_idx + HEADS_M
        mask = (ss <= tt) if d == 0 else (ss >= tt)
        b3 = per_head((pre_s if d == 0 else suf_s)[...])
        bt3 = per_head((pret_s if d == 0 else suft_s)[...])
        bcol = jnp.sum(jnp.where(lane == f_idx, b3, 0.0), axis=2, keepdims=True)
        icol = jnp.sum(jnp.where(lane == i_idx, g3, 0.0), axis=2, keepdims=True)
        brow = jnp.sum(jnp.where(sub == f_idx, bt3, 0.0), axis=1, keepdims=True)
        irow = jnp.sum(jnp.where(sub == i_idx, gt3, 0.0), axis=1, keepdims=True)
        bend = bcol[:, blk - 1:blk, :] if d == 0 else bcol[:, 0:1, :]

        dmat = jnp.where(mask, bcol - brow + irow, -jnp.inf)
        mloc = jnp.max(dmat, axis=2, keepdims=True)
        qk = jnp.einsum('ctd,csd->cts', qb, kb, preferred_element_type=F32)
        s_loc = jnp.exp(dmat - mloc) * qk
        num_loc = jnp.einsum('cts,csd->ctd', s_loc.astype(BF16), vb, preferred_element_type=F32)
        den_loc = jnp.sum(s_loc, axis=2, keepdims=True)
        to_end = bend - bcol + icol
        mk = jnp.max(to_end, axis=1, keepdims=True)
        kw = k3 * jnp.exp(to_end - mk)
        kv_s[...] = jnp.einsum('cds,cse->cde', jnp.swapaxes(kw, 1, 2).astype(BF16), vb,
                               preferred_element_type=F32)
        ks_s[...] = jnp.sum(kw, axis=1, keepdims=True)
        be_s[...] = jnp.broadcast_to(bend, be_s.shape)
        mk_s[...] = jnp.broadcast_to(mk, mk_s.shape)

        def step(j, carry, d=d):
            c = j if d == 0 else nc - 1 - j
            sl = pl.ds(c * heads, heads)
            m_prev, c_prev, n_prev = mst[d], cst[d], nst[d]
            cp_s[sl] = c_prev.astype(BF16)
            np_s[sl] = n_prev
            mp_s[sl] = m_prev
            be, mkc = be_s[sl], mk_s[sl]
            m_new = jnp.maximum(be + m_prev, mkc)
            keep = jnp.exp(be + m_prev - m_new)
            add = jnp.exp(mkc - m_new)
            cst[d] = keep * c_prev + add * kv_s[sl]
            nst[d] = keep * n_prev + add * ks_s[sl]
            mst[d] = m_new
            return carry

        lax.fori_loop(0, nc, step, 0)

        m_inter = bcol + mp_s[...][:, :, 0:1]
        m_t = jnp.maximum(m_inter, mloc)
        w_state = jnp.exp(m_inter - m_t)
        w_loc = jnp.exp(mloc - m_t)
        inter = jnp.einsum('ctd,cde->cte', qb, cp_s[...], preferred_element_type=F32)
        num = w_state * inter + w_loc * num_loc
        den = w_state * jnp.sum(q3 * np_s[...], axis=2, keepdims=True) + w_loc * den_loc
        h = num / jnp.maximum(jnp.abs(den), jnp.exp(-m_t))
        hsum = h if hsum is None else hsum + h

    hn = hsum * lax.rsqrt(jnp.mean(hsum * hsum, axis=-1, keepdims=True) + EPS)
    og3 = split_heads(og_ref[...].astype(F32))
    for c in range(nc):
        for hh in range(heads):
            cols = slice(hh * HD_M, (hh + 1) * HD_M)
            e = c * heads + hh
            y_ref[c * blk:(c + 1) * blk, cols] = (hn[e] * gh_ref[:, cols] * jax.nn.sigmoid(og3[e])).astype(y_ref.dtype)
    if emit_state:
        cn_ref[...] = cst[...]
        nn_ref[...] = nst[...]
        mn_ref[...] = mst[...]


def mlstm(proj, gates, w_conv_qk, g_head, state, *, nb, seq, row0, heads):
    blk0 = row0 // seq
    nblk = seq // SCAN_BLOCK
    groups = HEADS_M // heads
    n_gates = 4 * HEADS_M
    zero_state = state is None
    col = lambda part: pl.BlockSpec((seq, heads * HD_M), lambda b, h: (b + blk0, part * groups + h))
    in_specs = [col(0), col(1), col(2), col(3),
                pl.BlockSpec((seq, LANE), lambda b, h: (b + blk0, 0)),
                pl.BlockSpec((3, heads * HD_M), lambda b, h: (0, h)),
                pl.BlockSpec((3, heads * HD_M), lambda b, h: (0, groups + h)),
                pl.BlockSpec((1, heads * HD_M), lambda b, h: (0, h))]
    args = [proj, proj, proj, proj, gates, w_conv_qk, w_conv_qk, g_head.reshape(1, MW)]
    state_specs = [pl.BlockSpec((None, 2, heads, HD_M, HD_M), lambda b, h: (b, 0, h, 0, 0)),
                   pl.BlockSpec((None, 2, heads, 1, HD_M), lambda b, h: (b, 0, h, 0, 0))]
    if not zero_state:
        c0, n0, m0 = state
        in_specs += state_specs + [pl.BlockSpec((None, 2, heads, 1, 1), lambda b, h: (b, 0, h, 0, 0))]
        args += [c0, n0.reshape(nb, 2, HEADS_M, 1, HD_M), m0.reshape(nb, 2, HEADS_M, 1, 1)]
    out_shape = [jax.ShapeDtypeStruct((nb * seq, MW), BF16)]
    out_specs = [pl.BlockSpec((seq, heads * HD_M), lambda b, h: (b, h))]
    if zero_state:
        out_shape += [jax.ShapeDtypeStruct((nb, 2, HEADS_M, HD_M, HD_M), F32),
                      jax.ShapeDtypeStruct((nb, 2, HEADS_M, 1, HD_M), F32),
                      jax.ShapeDtypeStruct((nb, 2, HEADS_M, 1, LANE), F32)]
        out_specs += state_specs + [pl.BlockSpec((None, 2, heads, 1, LANE), lambda b, h: (b, 0, h, 0, 0))]
    per_block = lambda *shape: pltpu.VMEM((nblk,) + shape, F32)
    per_entry = lambda *shape, dtype=F32: pltpu.VMEM((nblk * heads,) + shape, dtype)
    return pl.pallas_call(
        functools.partial(_mlstm_kernel, seq=seq, heads=heads, zero_state=zero_state, emit_state=zero_state),
        out_shape=out_shape,
        grid=(nb, groups),
        in_specs=in_specs,
        out_specs=out_specs,
        scratch_shapes=[per_block(SCAN_BLOCK, LANE), per_block(SCAN_BLOCK, LANE),
                        per_block(n_gates, SCAN_BLOCK), per_block(n_gates, SCAN_BLOCK), per_block(n_gates, SCAN_BLOCK),
                        per_entry(HD_M, HD_M), per_entry(1, HD_M), per_entry(1, LANE), per_entry(1, LANE),
                        per_entry(HD_M, HD_M, dtype=BF16), per_entry(1, HD_M), per_entry(1, LANE),
                        pltpu.VMEM((2, heads, HD_M, HD_M), F32), pltpu.VMEM((2, heads, 1, HD_M), F32),
                        pltpu.VMEM((2, heads, 1, LANE), F32)],
        compiler_params=_cparams(("arbitrary", "arbitrary")),
        name="mlstm_%d" % seq,
    )(*args)


def _hyena_pre_kernel(v_ref, x1_ref, x2_ref, wv_ref, w1_ref, w2_ref, u_ref, x2c_ref):
    first, last = _seq_edges(SEQ_BLOCK, pl.program_id(0) < N_PBLK)
    x1c = _dwconv3(x1_ref[...].astype(F32), w1_ref, first, last)
    u_ref[...] = (x1c * _dwconv3(v_ref[...].astype(F32), wv_ref, first, last)).astype(u_ref.dtype)
    x2c_ref[...] = _dwconv3(x2_ref[...].astype(F32), w2_ref, first, last).astype(x2c_ref.dtype)


def hyena_pre(proj, w_conv_hy):
    tc = 256
    nct = HW // tc
    c0 = 4 * MW // tc
    pcol = lambda part: pl.BlockSpec((SEQ_BLOCK, tc), lambda i, j: (i, c0 + part * nct + j))
    wcol = lambda part: pl.BlockSpec((3, tc), lambda i, j: (0, part * nct + j))
    out = jax.ShapeDtypeStruct((N_TOK, HW), BF16)
    ospec = pl.BlockSpec((SEQ_BLOCK, tc), lambda i, j: (i, j))
    return pl.pallas_call(
        _hyena_pre_kernel,
        out_shape=[out, out],
        grid=(N_TOK // SEQ_BLOCK, nct),
        in_specs=[pcol(0), pcol(1), pcol(2), wcol(0), wcol(1), wcol(2)],
        out_specs=[ospec, ospec],
        compiler_params=_cparams(("arbitrary", "arbitrary")),
        name="hyena_pre",
    )(proj, proj, proj, w_conv_hy, w_conv_hy, w_conv_hy)


@functools.lru_cache(maxsize=None)
def _filter_tables(seq):
    t = np.linspace(0.0, 1.0, seq)[:, None]
    wpos = 2.0 * np.pi * np.arange(seq)[:, None] / seq
    bands = np.linspace(1e-4, N_BANDS - 1, N_BANDS)[None, :]
    z = np.concatenate([t, np.cos(bands * wpos), -np.sin(bands * wpos)], axis=-1)
    z = np.pad(z, ((0, 0), (0, LANE - FILTER_EMB)))
    max_decay = math.log(DECAY_TARGET) / DECAY_FAST
    min_decay = math.log(DECAY_TARGET) / DECAY_SLOW
    deltas = np.abs(np.linspace(min_decay, max_decay, HW))
    decay = np.exp(-t * np.concatenate([deltas, deltas])[None, :])
    return z.astype(np.float32), decay.astype(np.float32)


@functools.lru_cache(maxsize=None)
def _dft_tables(seq, tk):
    n = 2 * seq
    k = np.arange(seq)[:, None]
    t = np.arange(seq)[None, :]
    ang = 2.0 * np.pi * ((k * t) % n) / n
    alt = np.where(np.arange(seq) % 2 == 0, 1.0, -1.0)
    cm, sm = np.cos(ang), np.sin(ang)
    sm[0, :] = alt
    fwd = np.stack([cm.reshape(seq // tk, tk, seq), sm.reshape(seq // tk, tk, seq)], axis=1)
    wk = np.where(np.arange(seq) == 0, 1.0, 2.0)[None, :]
    ci = (np.cos(ang.T) * wk) / n
    si = np.sin(ang.T) * 2.0 / n
    si[:, 0] = alt / n
    inv = np.concatenate([ci, si], axis=1)
    return fwd.astype(np.float32), inv.astype(np.float32)


def _filter_kernel(z_ref, w1_ref, b1_ref, w2_ref, b2_ref, w3_ref, fr_ref, dec_ref, hs_ref, hd_ref):
    fr = fr_ref[...]
    hp = functools.partial(jnp.dot, precision=HIGHEST, preferred_element_type=F32)
    h1 = jnp.sin(fr * (hp(z_ref[...], w1_ref[...]) + b1_ref[...]))
    h2 = jnp.sin(fr * (hp(h1, w2_ref[...]) + b2_ref[...]))
    w3 = w3_ref[...]
    h_hi, w_hi = h2.astype(BF16), w3.astype(BF16)
    h_lo, w_lo = (h2 - h_hi.astype(F32)).astype(BF16), (w3 - w_hi.astype(F32)).astype(BF16)
    filt = (_dot(h_hi, w_hi) + _dot(h_hi, w_lo) + _dot(h_lo, w_hi)) * dec_ref[...]
    past, fut = filt[:, :HW], filt[:, HW:]
    rows = filt.shape[0]
    grow = lax.broadcasted_iota(jnp.int32, (rows, 1), 0) + pl.program_id(0) * rows
    fut = jnp.where(grow == 0, 0.0, fut)
    hs_ref[...] = past + fut
    hd_ref[...] = past - fut


def filter_gen(seq, w1, b1, w2, b2, w3, freq):
    z, decay = _filter_tables(seq)
    tl = 256
    fh = FILTER_HIDDEN
    full = lambda shape: pl.BlockSpec(shape, lambda i: (0, 0))
    out = jax.ShapeDtypeStruct((seq, HW), F32)
    return pl.pallas_call(
        _filter_kernel,
        out_shape=[out, out],
        grid=(seq // tl,),
        in_specs=[pl.BlockSpec((tl, LANE), lambda i: (i, 0)), full((LANE, fh)), full((1, fh)), full((fh, fh)),
                  full((1, fh)), full((fh, 2 * HW)), full((1, fh)), pl.BlockSpec((tl, 2 * HW), lambda i: (i, 0))],
        out_specs=[pl.BlockSpec((tl, HW), lambda i: (i, 0))] * 2,
        compiler_params=_cparams(("arbitrary",)),
        name="filter_gen",
    )(jnp.asarray(z), jnp.pad(w1, ((0, LANE - FILTER_EMB), (0, 0))), b1.reshape(1, fh), w2, b2.reshape(1, fh), w3,
      freq.reshape(1, fh), jnp.asarray(decay))


def _dft_filter_kernel(a_ref, hs_ref, hd_ref, k_ref, hs_bf, hd_bf):
    @pl.when(pl.program_id(0) == 0)
    def _():
        hs_bf[...] = hs_ref[...].astype(BF16)
        hd_bf[...] = hd_ref[...].astype(BF16)

    k_ref[0] = _dot(a_ref[0].astype(BF16), hs_bf[...])
    k_ref[1] = _dot(a_ref[1].astype(BF16), hd_bf[...])

    @pl.when(pl.program_id(0) == 0)
    def _():
        k_ref[1, 0:1, :] = _dot(a_ref[1, 0:8, :].astype(BF16), hs_bf[...])[0:1, :]


def dft_filter(seq, tk, hs, hd):
    fwd, _ = _dft_tables(seq, tk)
    return pl.pallas_call(
        _dft_filter_kernel,
        out_shape=jax.ShapeDtypeStruct((2, seq, HW), F32),
        grid=(seq // tk,),
        in_specs=[pl.BlockSpec((None, 2, tk, seq), lambda m: (m, 0, 0, 0)),
                  pl.BlockSpec((seq, HW), lambda m: (0, 0)), pl.BlockSpec((seq, HW), lambda m: (0, 0))],
        out_specs=pl.BlockSpec((2, tk, HW), lambda m: (0, m, 0)),
        scratch_shapes=[pltpu.VMEM((seq, HW), BF16), pltpu.VMEM((seq, HW), BF16)],
        compiler_params=_cparams(("arbitrary",)),
        name="dft_filter",
    )(jnp.asarray(fwd), hs, hd)


def _dft_fwd_kernel(a_ref, u_ref, k_ref, y_ref, u_bf, tbl):
    m = pl.program_id(1)

    @pl.when(m == 0)
    def _():
        u_bf[...] = u_ref[...].astype(BF16)

    @pl.when(pl.program_id(0) == 0)
    def _():
        tbl[m] = a_ref[...].astype(BF16)

    ure = _dot(tbl[m, 0], u_bf[...])
    uim = _dot(tbl[m, 1], u_bf[...])
    kre, kim = k_ref[0], k_ref[1]
    packed = (lax.broadcasted_iota(jnp.int32, (ure.shape[0], 1), 0) == 0) & (m == 0)
    y_ref[0] = jnp.where(packed, ure * kre, ure * kre - uim * kim).astype(y_ref.dtype)
    y_ref[1] = jnp.where(packed, uim * kim, ure * kim + uim * kre).astype(y_ref.dtype)


def dft_fwd(seq, tk, u, kf, *, nb, row0):
    fwd, _ = _dft_tables(seq, tk)
    blk0 = row0 // seq
    nm = seq // tk
    return pl.pallas_call(
        _dft_fwd_kernel,
        out_shape=jax.ShapeDtypeStruct((nb, 2, seq, HW), BF16),
        grid=(nb, nm),
        in_specs=[pl.BlockSpec((None, 2, tk, seq), lambda b, m: (jnp.where(b == 0, m, nm - 1), 0, 0, 0)),
                  pl.BlockSpec((seq, HW), lambda b, m: (b + blk0, 0)),
                  pl.BlockSpec((2, tk, HW), lambda b, m: (0, m, 0))],
        out_specs=pl.BlockSpec((None, 2, tk, HW), lambda b, m: (b, 0, m, 0)),
        scratch_shapes=[pltpu.VMEM((seq, HW), BF16), pltpu.VMEM((nm, 2, tk, seq), BF16)],
        compiler_params=_cparams(("arbitrary", "arbitrary"), DFT_VMEM_LIMIT),
        name="dft_fwd",
    )(jnp.asarray(fwd), u, kf)


def _dft_inv_kernel(a_ref, y_ref, u_ref, x2_ref, bias_ref, o_ref, tbl, *, seq):
    t = pl.program_id(1)

    @pl.when(pl.program_id(0) == 0)
    def _():
        tbl[t] = a_ref[...].astype(BF16)

    conv = _dot(tbl[t, :, :seq], y_ref[0]) + _dot(tbl[t, :, seq:], y_ref[1])
    u = u_ref[...].astype(F32)
    o_ref[...] = (x2_ref[...].astype(F32) * (conv + bias_ref[...] * u)).astype(o_ref.dtype)


def dft_inv(seq, tk, y, u, x2c, bias, *, nb, row0):
    _, inv = _dft_tables(seq, tk)
    tm = min(seq, 512)
    nt = seq // tm
    blk0 = row0 // tm
    rows = lambda b, t: (b * nt + t + blk0, 0)
    return pl.pallas_call(
        functools.partial(_dft_inv_kernel, seq=seq),
        out_shape=jax.ShapeDtypeStruct((nb * seq, HW), BF16),
        grid=(nb, nt),
        in_specs=[pl.BlockSpec((tm, 2 * seq), lambda b, t: (jnp.where(b == 0, t, nt - 1), 0)),
                  pl.BlockSpec((None, 2, seq, HW), lambda b, t: (b, 0, 0, 0)),
                  pl.BlockSpec((tm, HW), rows), pl.BlockSpec((tm, HW), rows),
                  pl.BlockSpec((1, HW), lambda b, t: (0, 0))],
        out_specs=pl.BlockSpec((tm, HW), lambda b, t: (b * nt + t, 0)),
        scratch_shapes=[pltpu.VMEM((nt, tm, 2 * seq), BF16)],
        compiler_params=_cparams(("arbitrary", "arbitrary"), DFT_VMEM_LIMIT),
        name="dft_inv",
    )(jnp.asarray(inv), y, u, x2c, bias.reshape(1, HW))


def hyena_group(seq, u, x2c, filt_w, bias, *, nb, row0):
    tk = min(seq, 512)
    hs, hd = filter_gen(seq, *filt_w)
    kf = dft_filter(seq, tk, hs, hd)
    y = dft_fwd(seq, tk, u, kf, nb=nb, row0=row0)
    return dft_inv(seq, tk, y, u, x2c, bias, nb=nb, row0=row0)


def mixer_ab_parts(proj, gates, state_s, w_conv_qk, g_head, w_conv_hy, w_f1, b_f1, w_f2, b_f2, w_f3, freq, hy_bias):
    ym_p, c_p, n_p, m_p = mlstm(proj, gates, w_conv_qk, g_head, None, nb=BATCH, seq=SEQ, row0=0, heads=HEADS_M)
    (ym_s,) = mlstm(proj, gates, w_conv_qk, g_head, state_s, nb=DEC_BATCH, seq=DEC_SEQ, row0=N_P, heads=2)
    u, x2c = hyena_pre(proj, w_conv_hy)
    filt_w = (w_f1, b_f1, w_f2, b_f2, w_f3, freq)
    yh_p = hyena_group(SEQ, u, x2c, filt_w, hy_bias, nb=BATCH, row0=0)
    yh_s = hyena_group(DEC_SEQ, u, x2c, filt_w, hy_bias, nb=DEC_BATCH, row0=N_P)
    state_p = (c_p, n_p[:, :, :, 0, :], m_p[:, :, :, 0, 0])
    return (ym_p, ym_s), (yh_p, yh_s), state_p


def kernel(x_prompt, x_sample, state_mlstm_C, state_mlstm_n, state_mlstm_m, cache_na_k, cache_na_v, c, c_ctx, w_ada, b_ada, g_mix, g_ffn, g_final, w_in_ab, b_gates, w_conv_qk, g_mlstm, w_conv_hy, w_filt1, b_filt1, w_filt2, b_filt2, w_filt3, filt_freq, hyena_bias, w_out_ab, w_in_c, rpb_c, w_out_c, w_up, w_conv_ffn, w_down):
    cmat = jnp.concatenate([c, c_ctx[None, :], jnp.zeros((MOD_ROWS - DEC_BATCH - 1, D), F32)], axis=0)
    mod_all = adaln_all(cmat, w_ada, b_ada).reshape(DEPTH, MOD_ROWS, 6, 1, D)
    x = (x_prompt.reshape(N_P, D), x_sample.reshape(N_S, D))
    h = None
    new_c, new_n, new_m, new_k, new_v = [], [], [], [], []
    for l in range(DEPTH):
        e = l // 2
        if l % 2 == 0:
            assert l == 0, "a later mixer A/B layer would take the fused norm of the layer before it"
            proj, gates = ab_in(x, g_mix, mod_all, l, jnp.swapaxes(w_in_ab, 1, 2), b_gates[e], e)
            state_s = (state_mlstm_C[:, e], state_mlstm_n[:, e], state_mlstm_m[:, e])
            y_m, y_h, (c_p, n_p, m_p) = mixer_ab_parts(
                proj, gates, state_s, w_conv_qk[e], g_mlstm[e], w_conv_hy[e], w_filt1[e], b_filt1[e],
                w_filt2[e], b_filt2[e], w_filt3[e], filt_freq[e], hyena_bias[e])
            a_list = [y_m, y_h]
            w_list = [(w_out_ab, (None, MW, D), lambda i, e=e: (e, 0, 0)),
                      (w_out_ab, (None, HW, D), lambda i, e=e: (e, 1, 0))]
            new_c.append(c_p[:, None])
            new_n.append(n_p[:, None])
            new_m.append(m_p[:, None])
        else:
            o, k_new, v_new = mixer_c(h, cache_na_k[:, e], cache_na_v[:, e], w_in_c, e, rpb_c[e])
            a_list = [o]
            w_list = [(w_out_c, (None, D, D), lambda i, e=e: (e, 0, 0))]
            new_k.append(k_new)
            new_v.append(v_new)
        x, h = out_proj(a_list, w_list, x, mod_all, l, 2, g_ffn, l, (l, 3, 4), bm=2 * ROW_BM, name="mixer_out")
        mid = ffn_up(h, w_up, w_conv_ffn, l)
        w_list = [(w_down, (None, FF, D), lambda i, l=l: (l, 0, 0))]
        if l + 1 < DEPTH:
            x, h = out_proj([mid], w_list, x, mod_all, l, 5, g_mix, l + 1, (l + 1, 0, 1), name="ffn_down")
        else:
            y_p, y_s = out_proj([mid], w_list, x, mod_all, l, 5, g_final, 0, None, name="ffn_down_final")
    cat = lambda parts: parts[0] if len(parts) == 1 else jnp.concatenate(parts, axis=1)
    return (y_p.reshape(BATCH, SEQ, D), y_s.reshape(DEC_BATCH, DEC_SEQ, D), cat(new_c), cat(new_n), cat(new_m),
            cat(new_k), cat(new_v))
```
